```python
import math
import jax, jax.numpy as jnp
from jax import lax
import numpy as np

D_MODEL = 2048
BATCH = 1
SEQ = 16384
DEPTH = 1

GRID_W = 64
CTX_LEN = 256
N_HEADS = 8
QK_HEAD = 64
V_HEAD = 128
QK_W = N_HEADS * QK_HEAD
MV_W = N_HEADS * V_HEAD
HY_W = D_MODEL - MV_W
CHUNK = 128
FILT_BANDS = 16
FILT_EMB = 2 * FILT_BANDS + 1
FILT_HIDDEN = 64
DECAY_TARGET = 1e-2
FAST_DECAY_PCT = 0.3
SLOW_DECAY_PCT = 1.5
N_EXPERTS = 64
N_GROUPS = 8
E_PER_GROUP = N_EXPERTS // N_GROUPS
TOPK_GROUPS = 4
TOP_K = 8
D_EXPERT = 512
D_SHARED = 512
ROUTE_SCALE = 2.5
MOE_BLOCK = 128
EPS = 1e-6
OFF_K = 0
OFF_V = OFF_K + QK_W
OFF_G = OFF_V + MV_W
OFF_Q = OFF_G + 4 * N_HEADS
OFF_O = OFF_Q + QK_W
OFF_HY = OFF_O + MV_W
IN_COLS = OFF_HY + 3 * HY_W

kernel_name = 'hybrid_mlstm_hyena_moe_prefix_dit'

F32 = jnp.float32


def rmsnorm(x, g):
    xf = x.astype(F32)
    y = xf * lax.rsqrt(jnp.mean(xf * xf, axis=-1, keepdims=True) + EPS)
    return (y * g.astype(F32)).astype(x.dtype)


def modulate(h, shift, scale):
    return h * (1 + scale) + shift


def short_conv(x, w, b):
    L = x.shape[1]
    xp = jnp.pad(x, ((0, 0), (1, 1), (0, 0)))
    return xp[:, :L] * w[0] + xp[:, 1:L + 1] * w[1] + xp[:, 2:L + 2] * w[2] + b


def flip(a):
    return jnp.flip(a, axis=1)


def pos_embed_2d(n_tokens):
    rows = n_tokens // GRID_W
    quarter = D_MODEL // 4
    omega = 1.0 / (10000.0 ** (jnp.arange(quarter, dtype=F32) / quarter))

    def emb1d(pos):
        ang = pos[:, None] * omega[None]
        return jnp.concatenate([jnp.sin(ang), jnp.cos(ang)], axis=-1)

    e_row = emb1d(jnp.arange(rows, dtype=F32))
    e_col = emb1d(jnp.arange(GRID_W, dtype=F32))
    half = D_MODEL // 2
    grid = jnp.concatenate([jnp.broadcast_to(e_row[:, None], (rows, GRID_W, half)),
                            jnp.broadcast_to(e_col[None], (rows, GRID_W, half))], axis=-1)
    return grid.reshape(rows * GRID_W, D_MODEL)


def _to_chunks(a):
    B, L = a.shape[:2]
    a = a.reshape((B, L // CHUNK, CHUNK) + a.shape[2:])
    return jnp.moveaxis(a, 3, 1)


def _from_chunks(a):
    B, H, NC, CH = a.shape[:4]
    return jnp.moveaxis(a, 1, 3).reshape((B, NC * CH, H) + a.shape[4:])


def zero_state(b):
    return (jnp.zeros((b, N_HEADS, QK_HEAD, V_HEAD), F32),
            jnp.zeros((b, N_HEADS, QK_HEAD), F32),
            jnp.zeros((b, N_HEADS), F32))


def _chunk_summaries(kc, vc, ic, fc):
    b = jnp.cumsum(fc, axis=-1)
    g = b[..., -1]
    a = g[..., None] - b + ic
    m_loc = jnp.max(a, axis=-1)
    wgt = jnp.exp(a - m_loc[..., None])
    c_loc = jnp.einsum('bhncd,bhnc,bhnce->bhnde', kc, wgt, vc)
    n_loc = jnp.einsum('bhncd,bhnc->bhnd', kc, wgt)
    return b, g, m_loc, c_loc, n_loc


def _chunk_scan(g, m_loc, c_loc, n_loc, init):
    def step(state, inp):
        c_st, n_st, m_st = state
        g_c, m_l, c_l, n_l = inp
        m_new = jnp.maximum(g_c + m_st, m_l)
        s_prev = jnp.exp(g_c + m_st - m_new)
        s_loc = jnp.exp(m_l - m_new)
        c_new = s_prev[..., None, None] * c_st + s_loc[..., None, None] * c_l
        n_new = s_prev[..., None] * n_st + s_loc[..., None] * n_l
        return (c_new, n_new, m_new), (c_st, n_st, m_st)

    xs = (jnp.moveaxis(g, 2, 0), jnp.moveaxis(m_loc, 2, 0),
          jnp.moveaxis(c_loc, 2, 0), jnp.moveaxis(n_loc, 2, 0))
    final, incoming = lax.scan(step, init, xs)
    c_in, n_in, m_in = incoming
    return final, (jnp.moveaxis(c_in, 0, 2), jnp.moveaxis(n_in, 0, 2), jnp.moveaxis(m_in, 0, 2))


def mlstm_final_state(k, v, i_pre, logf, init):
    kc, vc = _to_chunks(k.astype(F32)), _to_chunks(v.astype(F32))
    ic, fc = _to_chunks(i_pre), _to_chunks(logf)
    _, g, m_loc, c_loc, n_loc = _chunk_summaries(kc, vc, ic, fc)
    final, _ = _chunk_scan(g, m_loc, c_loc, n_loc, init)
    return final


def mlstm_direction(q, k, v, i_pre, logf, init):
    qc, kc, vc = _to_chunks(q.astype(F32)), _to_chunks(k.astype(F32)), _to_chunks(v.astype(F32))
    ic, fc = _to_chunks(i_pre), _to_chunks(logf)
    b, g, m_loc, c_loc, n_loc = _chunk_summaries(kc, vc, ic, fc)
    final, (c_in, n_in, m_in) = _chunk_scan(g, m_loc, c_loc, n_loc, init)
    lower = jnp.tril(jnp.ones((CHUNK, CHUNK), bool))
    d_log = jnp.where(lower, b[..., :, None] - b[..., None, :] + ic[..., None, :], -jnp.inf)
    inter = b + m_in[..., None]
    m_t = jnp.maximum(inter, jnp.max(d_log, axis=-1))
    s = jnp.einsum('bhntd,bhnsd->bhnts', qc, kc) * jnp.exp(d_log - m_t[..., None])
    sc = jnp.exp(inter - m_t)
    num = jnp.einsum('bhnts,bhnse->bhnte', s, vc) + sc[..., None] * jnp.einsum('bhntd,bhnde->bhnte', qc, c_in)
    den = jnp.sum(s, axis=-1) + sc * jnp.einsum('bhntd,bhnd->bhnt', qc, n_in)
    h = num / jnp.maximum(jnp.abs(den), jnp.exp(-m_t))[..., None]
    return _from_chunks(h), final


def mlstm_kv_gates(z, p):
    B, L = z.shape[:2]
    k = jax.nn.silu(short_conv(z[..., OFF_K:OFF_V], p['conv_k_w'], p['conv_k_b'])).reshape(B, L, N_HEADS, QK_HEAD)
    v = z[..., OFF_V:OFF_G].reshape(B, L, N_HEADS, V_HEAD)
    gts = z[..., OFF_G:OFF_Q].astype(F32) + p['b_gates'].astype(F32)
    i_f, f_f, i_b, f_b = jnp.split(gts, 4, axis=-1)
    return k, v, (i_f, jax.nn.log_sigmoid(f_f)), (i_b, jax.nn.log_sigmoid(f_b))


def context_states(z, p):
    k, v, gf, gb = mlstm_kv_gates(z, p)
    init = zero_state(z.shape[0])
    st_f = mlstm_final_state(k, v, gf[0], gf[1], init)
    st_b = mlstm_final_state(flip(k), flip(v), flip(gb[0]), flip(gb[1]), init)
    return st_f, st_b


def hyena_filters(L, p):
    t = jnp.arange(L, dtype=F32)
    t01 = t / max(L - 1, 1)
    w = 2.0 * math.pi * t / L
    bands = jnp.linspace(1e-4, FILT_BANDS - 1, FILT_BANDS, dtype=F32)
    feats = jnp.concatenate([t01[:, None], jnp.cos(w[:, None] * bands), -jnp.sin(w[:, None] * bands)], axis=-1)
    hdn = jnp.sin(p['filt_freq1'] * (feats @ p['filt_w1'] + p['filt_b1']))
    hdn = jnp.sin(p['filt_freq2'] * (hdn @ p['filt_w2'] + p['filt_b2']))
    h = (hdn @ p['filt_w3'] + p['filt_b3']).astype(F32).reshape(L, 2, HY_W)
    rates = jnp.linspace(-math.log(DECAY_TARGET) / SLOW_DECAY_PCT, -math.log(DECAY_TARGET) / FAST_DECAY_PCT, HY_W, dtype=F32)
    h = h * jnp.exp(-t01[:, None] * rates[None])[:, None, :]
    h_f, h_b = h[:, 0], h[:, 1]
    l1 = jnp.sum(jnp.abs(h_f), axis=0) + jnp.sum(jnp.abs(h_b[1:]), axis=0)
    return h_f / l1, h_b / l1


def bidir_fftconv(u, h_f, h_b, dskip):
    L, C = h_f.shape
    k_full = jnp.concatenate([h_f, jnp.zeros((1, C), F32), h_b[1:][::-1]], axis=0)
    k_hat = jnp.fft.rfft(k_full, n=2 * L, axis=0)
    u_hat = jnp.fft.rfft(u.astype(F32), n=2 * L, axis=1)
    y = jnp.fft.irfft(u_hat * k_hat[None], n=2 * L, axis=1)[:, :L]
    return (y + u.astype(F32) * dskip.astype(F32)).astype(u.dtype)


def hyena(z_hy, p):
    L = z_hy.shape[1]
    u = short_conv(z_hy, p['conv_hy_w'], p['conv_hy_b'])
    x0, x1, v = jnp.split(u, 3, axis=-1)
    h_f, h_b = hyena_filters(L, p)
    return x0 * bidir_fftconv(x1 * v, h_f, h_b, p['hy_dskip'])


def mixer(z, p, init_f, init_b):
    B, L = z.shape[:2]
    k, v, gf, gb = mlstm_kv_gates(z, p)
    q = jax.nn.silu(short_conv(z[..., OFF_Q:OFF_O], p['conv_q_w'], p['conv_q_b'])).reshape(B, L, N_HEADS, QK_HEAD) * (QK_HEAD ** -0.5)
    h_f, st_f = mlstm_direction(q, k, v, gf[0], gf[1], init_f)
    h_b, st_b = mlstm_direction(flip(q), flip(k), flip(v), flip(gb[0]), flip(gb[1]), init_b)
    h = h_f + flip(h_b)
    h = h * lax.rsqrt(jnp.mean(h * h, axis=-1, keepdims=True) + EPS) * p['g_head'].astype(F32).reshape(N_HEADS, V_HEAD)
    y_m = h.reshape(B, L, MV_W).astype(z.dtype) * jax.nn.sigmoid(z[..., OFF_O:OFF_HY])
    y_h = hyena(z[..., OFF_HY:], p)
    y = jnp.concatenate([y_m, y_h], axis=-1) @ p['w_out']
    return y, st_f, st_b


def moe_ffn(h, p):
    B, L, D = h.shape
    xt = h.reshape(B * L, D)
    T = xt.shape[0]
    s = jax.nn.sigmoid((xt @ p['w_router']).astype(F32))
    sel = s + p['b_router'].astype(F32)
    gscore = jnp.sum(lax.top_k(sel.reshape(T, N_GROUPS, E_PER_GROUP), 2)[0], axis=-1)
    _, gidx = lax.top_k(gscore, TOPK_GROUPS)
    gmask = jnp.any(gidx[..., None] == jnp.arange(N_GROUPS), axis=1)
    masked = jnp.where(jnp.repeat(gmask, E_PER_GROUP, axis=1), sel, -jnp.inf)
    _, eidx = lax.top_k(masked, TOP_K)
    w = jnp.take_along_axis(s, eidx, axis=1)
    w = w / jnp.sum(w, axis=-1, keepdims=True) * ROUTE_SCALE
    M = T * TOP_K
    e_flat = eidx.reshape(M)
    tok_flat = jnp.arange(M, dtype=jnp.int32) // TOP_K
    w_flat = w.reshape(M)
    order = jnp.argsort(e_flat)
    e_sorted = e_flat[order]
    counts = jnp.zeros((N_EXPERTS,), jnp.int32).at[e_flat].add(1)
    starts = jnp.cumsum(counts) - counts
    pcounts = (counts + MOE_BLOCK - 1) // MOE_BLOCK * MOE_BLOCK
    pends = jnp.cumsum(pcounts)
    pstarts = pends - pcounts
    dest = pstarts[e_sorted] + (jnp.arange(M, dtype=jnp.int32) - starts[e_sorted])
    nb = -(-M // MOE_BLOCK) + N_EXPERTS
    buf_tok = jnp.zeros((nb * MOE_BLOCK,), jnp.int32).at[dest].set(tok_flat[order])
    buf_w = jnp.zeros((nb * MOE_BLOCK,), F32).at[dest].set(w_flat[order])
    blk_exp = jnp.minimum(jnp.searchsorted(pends, jnp.arange(nb, dtype=jnp.int32) * MOE_BLOCK, side='right'), N_EXPERTS - 1)
    w1_e, w3_e, w2_e = p['w1_e'], p['w3_e'], p['w2_e']

    def body(out, blk):
        tok, wt, e = blk
        xb = xt[tok]
        yb = (jax.nn.silu(xb @ w1_e[e]) * (xb @ w3_e[e])) @ w2_e[e]
        return out.at[tok].add(yb * wt.astype(yb.dtype)[:, None]), None

    routed, _ = lax.scan(body, jnp.zeros_like(xt),
                         (buf_tok.reshape(nb, MOE_BLOCK), buf_w.reshape(nb, MOE_BLOCK), blk_exp))
    shared = (jax.nn.silu(xt @ p['w1_s']) * (xt @ p['w3_s'])) @ p['w2_s']
    return (routed + shared).reshape(B, L, D)


def setup_inputs(seed: int = 0) -> dict:
    key = jax.random.key(seed)
    keys = iter(jax.random.split(key, 48))

    def nrm(shape, scale):
        return jax.random.normal(next(keys), shape, F32) * scale

    D = D_MODEL
    f_bias = jnp.broadcast_to(jnp.linspace(3.0, 6.0, N_HEADS, dtype=F32), (DEPTH, N_HEADS))
    b_gates = jnp.concatenate([nrm((DEPTH, N_HEADS), 0.1), f_bias + nrm((DEPTH, N_HEADS), 0.1),
                               nrm((DEPTH, N_HEADS), 0.1), f_bias + nrm((DEPTH, N_HEADS), 0.1)], axis=-1)
    return {
        'x': nrm((BATCH, SEQ, D), 1.0),
        'c': nrm((BATCH, D), 1.0),
        'ctx': nrm((BATCH, CTX_LEN, D), 1.0),
        'c_ctx': nrm((D,), 1.0),
        'w_ada': nrm((DEPTH, D, 6 * D), D ** -0.5),
        'b_ada': nrm((DEPTH, 6 * D), 0.02),
        'g_mix': 1.0 + nrm((DEPTH, D), 0.05),
        'g_ffn': 1.0 + nrm((DEPTH, D), 0.05),
        'w_in': nrm((DEPTH, D, IN_COLS), D ** -0.5),
        'b_gates': b_gates,
        'conv_k_w': nrm((DEPTH, 3, QK_W), 0.5),
        'conv_k_b': nrm((DEPTH, QK_W), 0.02),
        'conv_q_w': nrm((DEPTH, 3, QK_W), 0.5),
        'conv_q_b': nrm((DEPTH, QK_W), 0.02),
        'g_head': 1.0 + nrm((DEPTH, MV_W), 0.05),
        'conv_hy_w': nrm((DEPTH, 3, 3 * HY_W), 0.5),
        'conv_hy_b': nrm((DEPTH, 3 * HY_W), 0.02),
        'filt_w1': nrm((DEPTH, FILT_EMB, FILT_HIDDEN), FILT_EMB ** -0.5),
        'filt_b1': nrm((DEPTH, FILT_HIDDEN), 0.02),
        'filt_freq1': 1.0 + nrm((DEPTH, FILT_HIDDEN), 0.05),
        'filt_w2': nrm((DEPTH, FILT_HIDDEN, FILT_HIDDEN), FILT_HIDDEN ** -0.5),
        'filt_b2': nrm((DEPTH, FILT_HIDDEN), 0.02),
        'filt_freq2': 1.0 + nrm((DEPTH, FILT_HIDDEN), 0.05),
        'filt_w3': nrm((DEPTH, FILT_HIDDEN, 2 * HY_W), FILT_HIDDEN ** -0.5),
        'filt_b3': nrm((DEPTH, 2 * HY_W), 0.02),
        'hy_dskip': nrm((DEPTH, HY_W), 1.0),
        'w_out': nrm((DEPTH, D, D), D ** -0.5),
        'w_router': nrm((DEPTH, D, N_EXPERTS), D ** -0.5),
        'b_router': nrm((DEPTH, N_EXPERTS), 0.01),
        'w1_e': nrm((DEPTH, N_EXPERTS, D, D_EXPERT), D ** -0.5),
        'w3_e': nrm((DEPTH, N_EXPERTS, D, D_EXPERT), D ** -0.5),
        'w2_e': nrm((DEPTH, N_EXPERTS, D_EXPERT, D), D_EXPERT ** -0.5),
        'w1_s': nrm((DEPTH, D, D_SHARED), D ** -0.5),
        'w3_s': nrm((DEPTH, D, D_SHARED), D ** -0.5),
        'w2_s': nrm((DEPTH, D_SHARED, D), D_SHARED ** -0.5),
        'g_final': 1.0 + nrm((D,), 0.05),
    }


def reference(x, c, ctx, c_ctx, w_ada, b_ada, g_mix, g_ffn, w_in, b_gates, conv_k_w, conv_k_b,
              conv_q_w, conv_q_b, g_head, conv_hy_w, conv_hy_b, filt_w1, filt_b1, filt_freq1,
              filt_w2, filt_b2, filt_freq2, filt_w3, filt_b3, hy_dskip, w_out, w_router, b_router,
              w1_e, w3_e, w2_e, w1_s, w3_s, w2_s, g_final):
    B, L, _ = x.shape
    x = x + pos_embed_2d(L).astype(x.dtype)[None]
    x_ctx = ctx
    for li in range(DEPTH):
        p = {'b_gates': b_gates[li], 'conv_k_w': conv_k_w[li], 'conv_k_b': conv_k_b[li],
             'conv_q_w': conv_q_w[li], 'conv_q_b': conv_q_b[li], 'g_head': g_head[li],
             'conv_hy_w': conv_hy_w[li], 'conv_hy_b': conv_hy_b[li],
             'filt_w1': filt_w1[li], 'filt_b1': filt_b1[li], 'filt_freq1': filt_freq1[li],
             'filt_w2': filt_w2[li], 'filt_b2': filt_b2[li], 'filt_freq2': filt_freq2[li],
             'filt_w3': filt_w3[li], 'filt_b3': filt_b3[li], 'hy_dskip': hy_dskip[li],
             'w_out': w_out[li], 'w_router': w_router[li], 'b_router': b_router[li],
             'w1_e': w1_e[li], 'w3_e': w3_e[li], 'w2_e': w2_e[li],
             'w1_s': w1_s[li], 'w3_s': w3_s[li], 'w2_s': w2_s[li]}
        mod = jax.nn.silu(c) @ w_ada[li] + b_ada[li]
        sh1, sc1, g1, sh2, sc2, g2 = jnp.split(mod[:, None, :], 6, axis=-1)
        mod_c = jax.nn.silu(c_ctx) @ w_ada[li] + b_ada[li]
        csh1, csc1, cg1, csh2, csc2, cg2 = jnp.split(mod_c, 6, axis=-1)
        hc = modulate(rmsnorm(x_ctx, g_mix[li]), csh1, csc1)
        if li == DEPTH - 1:
            st_f, st_b = context_states(hc @ w_in[li][:, :OFF_Q], p)
        else:
            init = zero_state(B)
            y_c, st_f, st_b = mixer(hc @ w_in[li], p, init, init)
            x_ctx = x_ctx + cg1 * y_c
            x_ctx = x_ctx + cg2 * moe_ffn(modulate(rmsnorm(x_ctx, g_ffn[li]), csh2, csc2), p)
        h = modulate(rmsnorm(x, g_mix[li]), sh1, sc1)
        y, _, _ = mixer(h @ w_in[li], p, st_f, st_b)
        x = x + g1 * y
        x = x + g2 * moe_ffn(modulate(rmsnorm(x, g_ffn[li]), sh2, sc2), p)
    return rmsnorm(x, g_final)
```

```python
import functools
import math

import numpy as np
import jax
import jax.numpy as jnp
from jax import lax
from jax.experimental import pallas as pl
from jax.experimental.pallas import tpu as pltpu

F32 = jnp.float32
BF16 = jnp.bfloat16
I32 = jnp.int32

D_MODEL = 2048
GRID_W = 64
N_HEADS = 8
QK_HEAD = 64
V_HEAD = 128
QK_W = N_HEADS * QK_HEAD
MV_W = N_HEADS * V_HEAD
HY_W = D_MODEL - MV_W
CHUNK = 128
FILT_BANDS = 16
FILT_HIDDEN = 64
DECAY_TARGET = 1e-2
FAST_DECAY_PCT = 0.3
SLOW_DECAY_PCT = 1.5
N_EXPERTS = 64
N_GROUPS = 8
E_PER_GROUP = 8
TOPK_GROUPS = 4
TOP_K = 8
D_EXPERT = 512
ROUTE_SCALE = 2.5
EPS = 1e-6
OFF_K = 0
OFF_V = OFF_K + QK_W
OFF_G = OFF_V + MV_W
OFF_Q = OFF_G + 4 * N_HEADS
OFF_O = OFF_Q + QK_W
OFF_HY = OFF_O + MV_W

ZC_KQ, ZC_V, ZC_O, ZC_X0, ZC_X1, ZC_HV = 0, 1, 2, 3, 4, 5
Z_COLS = 6 * 1024
GATE_PAD = 128

NEG = -1e30
MIB = 1024 * 1024

FFT_N2 = 128
FFT_KP = 144

MOE_ROWS = 256


def _cparams(sem, vmem_mb):
    return pltpu.CompilerParams(dimension_semantics=sem, vmem_limit_bytes=vmem_mb * MIB)


def _split2(x):
    hi = x.astype(BF16)
    lo = (x - hi.astype(F32)).astype(BF16)
    return hi, lo


_NN = (((1,), (0,)), ((), ()))
_NT = (((1,), (1,)), ((), ()))
_TN = (((0,), (0,)), ((), ()))


def _dg(a, b, dims=_NN):
    return lax.dot_general(a, b, dims, preferred_element_type=F32)


def _dot3(a, b, dims=_NN):
    ah, al = _split2(a)
    bh, bl = _split2(b)
    return _dg(ah, bh, dims) + _dg(al, bh, dims) + _dg(ah, bl, dims)


def _sigmoid(x):
    return 1.0 / (1.0 + jnp.exp(-x))


def _silu(x):
    return x * _sigmoid(x)


def _norm_mod(x, g, sh, sc):
    ms = jnp.mean(x * x, axis=-1, keepdims=True)
    return (x * lax.rsqrt(ms + EPS) * g) * (1.0 + sc) + sh


def _add_pos(x, erow, ecol):
    tm, d = x.shape
    half = d // 2
    parts = []
    for r in range(tm // GRID_W):
        xs = x[r * GRID_W:(r + 1) * GRID_W, :]
        parts.append(jnp.concatenate([xs[:, :half] + erow[r:r + 1, :], xs[:, half:] + ecol], axis=-1))
    return parts[0] if len(parts) == 1 else jnp.concatenate(parts, axis=0)


def _mod_body(cc_ref, w_ref, b_ref, o_ref):
    o_ref[...] = _dot3(_silu(cc_ref[...]), w_ref[...]) + b_ref[...]


def _mod_call(cc, w, b):
    d, n = w.shape
    tn = 1024
    return pl.pallas_call(
        _mod_body,
        grid=(n // tn,),
        in_specs=[pl.BlockSpec((8, d), lambda j: (0, 0)),
                  pl.BlockSpec((d, tn), lambda j: (0, j)),
                  pl.BlockSpec((1, tn), lambda j: (0, j))],
        out_specs=pl.BlockSpec((8, tn), lambda j: (0, j)),
        out_shape=jax.ShapeDtypeStruct((8, n), F32),
        compiler_params=_cparams(("arbitrary",), 40),
        name="mod",
    )(cc, w, b)


def _inproj_body(use_pos, x_ref, erow_ref, ecol_ref, gm_ref, sh_ref, sc_ref, w_ref, wg_ref,
                 z_ref, g_ref, h_scr):
    @pl.when(pl.program_id(1) == 0)
    def _():
        x = x_ref[...]
        if use_pos:
            x = _add_pos(x, erow_ref[...], ecol_ref[...])
        h = _norm_mod(x, gm_ref[...], sh_ref[...], sc_ref[...])
        h_scr[...] = h.astype(BF16)
        g_ref[...] = _dot3(h, wg_ref[...])

    z_ref[...] = jnp.dot(h_scr[...], w_ref[...], preferred_element_type=F32).astype(BF16)


def _inproj_call(x, erow, ecol, gm, sh, sc, w, wg, use_pos, tm):
    m, d = x.shape
    tn = 512
    er = tm // GRID_W if use_pos else erow.shape[0]
    row_map = (lambda i, j: (i, 0)) if use_pos else (lambda i, j: (0, 0))
    return pl.pallas_call(
        functools.partial(_inproj_body, use_pos),
        grid=(m // tm, Z_COLS // tn),
        in_specs=[pl.BlockSpec((tm, d), lambda i, j: (i, 0)),
                  pl.BlockSpec((er, d // 2), row_map),
                  pl.BlockSpec((GRID_W, d // 2), lambda i, j: (0, 0)),
                  pl.BlockSpec((1, d), lambda i, j: (0, 0)),
                  pl.BlockSpec((1, d), lambda i, j: (0, 0)),
                  pl.BlockSpec((1, d), lambda i, j: (0, 0)),
                  pl.BlockSpec((d, tn), lambda i, j: (0, j)),
                  pl.BlockSpec((d, GATE_PAD), lambda i, j: (0, 0))],
        out_specs=[pl.BlockSpec((tm, tn), lambda i, j: (i, j)),
                   pl.BlockSpec((tm, GATE_PAD), lambda i, j: (i, 0))],
        out_shape=[jax.ShapeDtypeStruct((m, Z_COLS), BF16),
                   jax.ShapeDtypeStruct((m, GATE_PAD), F32)],
        scratch_shapes=[pltpu.VMEM((tm, d), BF16)],
        compiler_params=_cparams(("arbitrary", "arbitrary"), 48),
        name="inproj",
    )(x, erow, ecol, gm, sh, sc, w, wg)


def _conv3(zc, zp, zn, w, b, first, last):
    tm = zc.shape[0]
    row = lax.broadcasted_iota(I32, zc.shape, 0)
    prev_row = jnp.where(first, 0.0, zp[7:8, :])
    next_row = jnp.where(last, 0.0, zn[0:1, :])
    xm = jnp.where(row == 0, prev_row, pltpu.roll(zc, 1, 0))
    xp = jnp.where(row == tm - 1, next_row, pltpu.roll(zc, tm - 1, 0))
    return xm * w[0:1, :] + zc * w[1:2, :] + xp * w[2:3, :] + b


def _conv_kq_body(zc_ref, zp_ref, zn_ref, w_ref, b_ref, s_ref, o_ref):
    i = pl.program_id(0)
    u = _conv3(zc_ref[...].astype(F32), zp_ref[...].astype(F32), zn_ref[...].astype(F32),
               w_ref[...], b_ref[...], i == 0, i == pl.num_programs(0) - 1)
    o_ref[...] = (_silu(u) * s_ref[...]).astype(BF16)


def _halo_specs(tm, m, cb):
    nb8 = m // 8
    return [pl.BlockSpec((tm, 1024), lambda i: (i, cb)),
            pl.BlockSpec((8, 1024), lambda i: (jnp.maximum(i * (tm // 8) - 1, 0), cb)),
            pl.BlockSpec((8, 1024), lambda i: (jnp.minimum((i + 1) * (tm // 8), nb8 - 1), cb))]


def _conv_kq_call(z, w, b, s, tm):
    m = z.shape[0]
    vec = pl.BlockSpec((1, 1024), lambda i: (0, 0))
    return pl.pallas_call(
        _conv_kq_body,
        grid=(m // tm,),
        in_specs=_halo_specs(tm, m, ZC_KQ) + [pl.BlockSpec((3, 1024), lambda i: (0, 0)), vec, vec],
        out_specs=pl.BlockSpec((tm, 1024), lambda i: (i, 0)),
        out_shape=jax.ShapeDtypeStruct((m, 1024), BF16),
        compiler_params=_cparams(("arbitrary",), 32),
        name="conv_kq",
    )(z, z, z, w, b, s)


def _conv_hy_body(ac_ref, ap_ref, an_ref, bc_ref, bp_ref, bn_ref, cc_ref, cp_ref, cn_ref,
                  w_ref, b_ref, x0_ref, xv_ref):
    i = pl.program_id(0)
    first, last = i == 0, i == pl.num_programs(0) - 1
    w = w_ref[...]
    b = b_ref[...]

    def cv(c, p, n, k):
        return _conv3(c[...].astype(F32), p[...].astype(F32), n[...].astype(F32),
                      w[:, k * 1024:(k + 1) * 1024], b[:, k * 1024:(k + 1) * 1024], first, last)

    x0_ref[...] = cv(ac_ref, ap_ref, an_ref, 0).astype(BF16)
    xv_ref[...] = (cv(bc_ref, bp_ref, bn_ref, 1) * cv(cc_ref, cp_ref, cn_ref, 2)).astype(BF16)


def _conv_hy_call(z, w, b, tm):
    m = z.shape[0]
    out = pl.BlockSpec((tm, 1024), lambda i: (i, 0))
    return pl.pallas_call(
        _conv_hy_body,
        grid=(m // tm,),
        in_specs=(_halo_specs(tm, m, ZC_X0) + _halo_specs(tm, m, ZC_X1) + _halo_specs(tm, m, ZC_HV)
                  + [pl.BlockSpec((3, 3072), lambda i: (0, 0)), pl.BlockSpec((1, 3072), lambda i: (0, 0))]),
        out_specs=[out, out],
        out_shape=[jax.ShapeDtypeStruct((m, 1024), BF16), jax.ShapeDtypeStruct((m, 1024), BF16)],
        compiler_params=_cparams(("arbitrary",), 32),
        name="conv_hy",
    )(z, z, z, z, z, z, z, z, z, w, b)


def _mlstm_body(kqf_ref, vf_ref, gf_ref, kqb_ref, vb_ref, gb_ref, bg_ref, s0_ref, m0_ref,
                hf_ref, hb_ref, sfin_ref, mfin_ref, s_scr, m_scr):
    j = pl.program_id(0)

    @pl.when(j == 0)
    def _():
        s_scr[...] = s0_ref[...]
        m_scr[...] = m0_ref[...]

    r = lax.broadcasted_iota(I32, (CHUNK, CHUNK), 0)
    c = lax.broadcasted_iota(I32, (CHUNK, CHUNK), 1)
    e1 = jnp.where(c == 0, 1.0, 0.0).astype(BF16)
    bg = bg_ref[...]

    for d in range(2):
        kq = (kqf_ref, kqb_ref)[d][...]
        v = (vf_ref, vb_ref)[d][...]
        g_all = (gf_ref, gb_ref)[d][...] + bg
        out_ref = (hf_ref, hb_ref)[d]
        tri = (r >= c) if d == 0 else (c >= r)
        tri_b = jnp.where(tri, 1.0, 0.0).astype(BF16)
        gi = g_all if d == 0 else pltpu.roll(g_all, CHUNK - 16, 1)
        gfp = pltpu.roll(g_all, CHUNK - 8 - 16 * d, 1)
        lf = jnp.minimum(gfp, 0.0) - jnp.log(1.0 + jnp.exp(-jnp.abs(gfp)))
        l1 = lf.astype(BF16)
        r1 = lf - l1.astype(F32)
        l2 = r1.astype(BF16)
        l3 = (r1 - l2.astype(F32)).astype(BF16)
        bcum = _dg(tri_b, l1) + _dg(tri_b, l2) + _dg(tri_b, l3)
        gtot = bcum[CHUNK - 1:CHUNK, :] if d == 0 else bcum[0:1, :]
        acol = gtot - bcum + gi
        m_loc = jnp.max(acol, axis=0, keepdims=True)
        wg = jnp.exp(acol - m_loc)
        m_st = m_scr[d, 0:1, :]
        inter = bcum + m_st
        m_new = jnp.maximum(gtot + m_st, m_loc)
        sp = jnp.broadcast_to(jnp.exp(gtot + m_st - m_new), (CHUNK, CHUNK))
        sl = jnp.broadcast_to(jnp.exp(m_loc - m_new), (CHUNK, CHUNK))
        rt = jnp.transpose(gi - bcum)

        for h in range(N_HEADS):
            p, half = divmod(h, 2)
            lm = (c // QK_HEAD) == half
            kp = kq[:, p * 128:(p + 1) * 128]
            qp = kq[:, QK_W + p * 128:QK_W + (p + 1) * 128]
            vaug = jnp.concatenate([v[:, h * 128:(h + 1) * 128], e1], axis=1)
            qm = jnp.where(lm, qp, jnp.zeros_like(qp))
            dl = jnp.where(tri, bcum[:, h:h + 1] + rt[h:h + 1, :], NEG)
            icol = inter[:, h:h + 1]
            mt = jnp.maximum(icol, jnp.max(dl, axis=1, keepdims=True))
            pm = jnp.exp(dl - mt)
            s = (_dg(qm, kp, _NT) * pm).astype(BF16)
            st = s_scr[d * N_HEADS + h]
            tot = _dg(s, vaug) + jnp.exp(icol - mt) * _dg(qm, st.astype(BF16))
            den = jnp.maximum(jnp.abs(tot[:, 128:129]), jnp.exp(-mt))
            out_ref[:, h * 128:(h + 1) * 128] = (tot[:, :128] / den).astype(BF16)
            kw = jnp.where(lm, kp.astype(F32) * wg[:, h:h + 1], 0.0).astype(BF16)
            s_scr[d * N_HEADS + h] = sp[:, h:h + 1] * st + sl[:, h:h + 1] * _dg(kw, vaug, _TN)
        m_scr[d, 0:1, :] = m_new

    @pl.when(j == pl.num_programs(0) - 1)
    def _():
        sfin_ref[...] = s_scr[...]
        mfin_ref[...] = m_scr[...]


def _mlstm_call(kq, z, gates, bg, s0, m0):
    m = kq.shape[0]
    nc = m // CHUNK
    fwd = lambda cb: (lambda j: (j, cb))
    bwd = lambda cb: (lambda j: (nc - 1 - j, cb))
    st_spec = pl.BlockSpec((2 * N_HEADS, CHUNK, 256), lambda j: (0, 0, 0))
    m_spec = pl.BlockSpec((2, 8, 128), lambda j: (0, 0, 0))
    return pl.pallas_call(
        _mlstm_body,
        grid=(nc,),
        in_specs=[pl.BlockSpec((CHUNK, 1024), fwd(0)), pl.BlockSpec((CHUNK, 1024), fwd(ZC_V)),
                  pl.BlockSpec((CHUNK, GATE_PAD), fwd(0)),
                  pl.BlockSpec((CHUNK, 1024), bwd(0)), pl.BlockSpec((CHUNK, 1024), bwd(ZC_V)),
                  pl.BlockSpec((CHUNK, GATE_PAD), bwd(0)),
                  pl.BlockSpec((1, GATE_PAD), lambda j: (0, 0)), st_spec, m_spec],
        out_specs=[pl.BlockSpec((CHUNK, 1024), fwd(0)), pl.BlockSpec((CHUNK, 1024), bwd(0)), st_spec, m_spec],
        out_shape=[jax.ShapeDtypeStruct((m, 1024), BF16), jax.ShapeDtypeStruct((m, 1024), BF16),
                   jax.ShapeDtypeStruct((2 * N_HEADS, CHUNK, 256), F32),
                   jax.ShapeDtypeStruct((2, 8, 128), F32)],
        scratch_shapes=[pltpu.VMEM((2 * N_HEADS, CHUNK, 256), F32), pltpu.VMEM((2, 8, 128), F32)],
        compiler_params=_cparams(("arbitrary",), 32),
        name="mlstm",
    )(kq, z, gates, kq, z, gates, bg, s0, m0)


def _filt_body(ft_ref, w1_ref, b1_ref, f1_ref, w2_ref, b2_ref, f2_ref, w3_ref, b3_ref, rt_ref,
               kf_ref, l1_ref):
    ft = ft_ref[...]
    h1 = jnp.sin(f1_ref[...] * (_dot3(ft, w1_ref[...]) + b1_ref[...]))
    h2 = jnp.sin(f2_ref[...] * (_dot3(h1, w2_ref[...]) + b2_ref[...]))
    h = _dot3(h2, w3_ref[...]) + b3_ref[...]
    h = h * jnp.exp(-ft[:, 0:1] * rt_ref[...]) * ft[:, 33:34]
    kf_ref[...] = h.astype(BF16)

    @pl.when(pl.program_id(0) == 0)
    def _():
        l1_ref[...] = jnp.zeros_like(l1_ref)

    l1_ref[...] += jnp.sum(jnp.abs(h), axis=0, keepdims=True)


def _filt_call(feats, w1, b1, f1, w2, b2, f2, w3, b3, rates, tn):
    n = feats.shape[0]
    nt = n // tn
    half = lambda i: (0, jnp.where(i >= nt // 2, 1, 0))
    c64 = lambda shape: pl.BlockSpec(shape, lambda i: (0, 0))
    return pl.pallas_call(
        _filt_body,
        grid=(nt,),
        in_specs=[pl.BlockSpec((tn, 64), lambda i: (i, 0)),
                  c64((64, 64)), c64((1, 64)), c64((1, 64)), c64((64, 64)), c64((1, 64)), c64((1, 64)),
                  pl.BlockSpec((64, HY_W), half), pl.BlockSpec((1, HY_W), half), c64((1, HY_W))],
        out_specs=[pl.BlockSpec((tn, HY_W), lambda i: (i, 0)), pl.BlockSpec((1, HY_W), lambda i: (0, 0))],
        out_shape=[jax.ShapeDtypeStruct((n, HY_W), BF16), jax.ShapeDtypeStruct((1, HY_W), F32)],
        compiler_params=_cparams(("arbitrary",), 32),
        name="filt",
    )(feats, w1, b1, f1, w2, b2, f2, w3, b3, rates)


def _fft_consts(n1_rows):
    n = n1_rows * FFT_N2
    kv = n1_rows // 2 + 1
    k1 = np.arange(FFT_KP, dtype=np.float64)
    valid = (k1 < kv).astype(np.float64)
    n1 = np.arange(n1_rows, dtype=np.float64)
    th1 = 2.0 * np.pi * np.outer(k1, n1) / n1_rows
    f1 = np.concatenate([np.cos(th1) * valid[:, None], -np.sin(th1) * valid[:, None]], axis=0)
    n2 = np.arange(FFT_N2, dtype=np.float64)
    tht = 2.0 * np.pi * np.outer(k1, n2) / n
    rep = lambda a: jnp.broadcast_to(jnp.asarray(a, F32)[:, :, None], (FFT_KP, FFT_N2, 128))
    twr = rep(np.cos(tht) * valid[:, None])
    twi = rep(-np.sin(tht) * valid[:, None])
    th2 = 2.0 * np.pi * np.outer(n2, n2) / FFT_N2
    cs, sn = np.cos(th2), np.sin(th2)
    f2p = np.block([[cs, sn], [-sn, cs]])
    f2pc = np.block([[cs, -sn], [sn, cs]])
    wk = np.where((k1 == 0) | (k1 == kv - 1), 1.0, 2.0) * valid / n
    half = n1_rows // 2
    thi = 2.0 * np.pi * np.outer(n1[:half], k1) / n1_rows
    gc = np.cos(thi) * wk[None, :]
    gs = np.sin(thi) * wk[None, :]
    as_bf = lambda a: jnp.asarray(a, F32).astype(BF16)
    return dict(f1=as_bf(f1), twr=twr, twi=twi,
                f2p=as_bf(f2p), f2pc=as_bf(f2pc), gc=as_bf(gc), gs=as_bf(gs))


def _fft1_body(f_ref, x_ref, ar_ref, ai_ref):
    o = _dg(f_ref[...], x_ref[...])
    ar_ref[...] = o[:FFT_KP].astype(BF16)
    ai_ref[...] = o[FFT_KP:].astype(BF16)


def _fft1_call(f1, x2d, cb):
    k, cols = x2d.shape
    f1 = f1[:, :k]
    out = pl.BlockSpec((FFT_KP, cb), lambda i: (0, i))
    sh = jax.ShapeDtypeStruct((FFT_KP, cols), BF16)
    return pl.pallas_call(
        _fft1_body,
        grid=(cols // cb,),
        in_specs=[pl.BlockSpec((2 * FFT_KP, k), lambda i: (0, 0)), pl.BlockSpec((k, cb), lambda i: (0, i))],
        out_specs=[out, out],
        out_shape=[sh, sh],
        compiler_params=_cparams(("arbitrary",), 32),
        name="fft1",
    )(f1, x2d)


def _twiddled(ar_ref, ai_ref, twr_ref, twi_ref, reps):
    a_r = ar_ref[...].astype(F32)
    a_i = ai_ref[...].astype(F32)
    tr = jnp.tile(twr_ref[...], (1, reps))
    ti = jnp.tile(twi_ref[...], (1, reps))
    st = jnp.concatenate([a_r * tr - a_i * ti, a_r * ti + a_i * tr], axis=0).astype(BF16)
    return st, tr, ti


def _fft2_filt_body(ar_ref, ai_ref, twr_ref, twi_ref, f2p_ref, k_ref):
    st, _, _ = _twiddled(ar_ref, ai_ref, twr_ref, twi_ref, ar_ref.shape[-1] // 128)
    k_ref[...] = _dg(f2p_ref[...], st).astype(BF16)


def _fft2_conv_body(ar_ref, ai_ref, twr_ref, twi_ref, k_ref, f2p_ref, f2pc_ref, br_ref, bi_ref):
    st, tr, ti = _twiddled(ar_ref, ai_ref, twr_ref, twi_ref, ar_ref.shape[-1] // 128)
    x = _dg(f2p_ref[...], st)
    xr, xi = x[:FFT_N2], x[FFT_N2:]
    kr = k_ref[:FFT_N2, :].astype(F32)
    ki = k_ref[FFT_N2:, :].astype(F32)
    sy = jnp.concatenate([xr * kr - xi * ki, xr * ki + xi * kr], axis=0).astype(BF16)
    b = _dg(f2pc_ref[...], sy)
    b_r, b_i = b[:FFT_N2], b[FFT_N2:]
    br_ref[...] = (b_r * tr + b_i * ti).astype(BF16)
    bi_ref[...] = (b_i * tr - b_r * ti).astype(BF16)


def _fft2_specs(ch):
    blk = pl.BlockSpec((None, FFT_N2, ch), lambda i: (i, 0, 0))
    tw = pl.BlockSpec((None, FFT_N2, 128), lambda i: (i, 0, 0))
    mat = pl.BlockSpec((2 * FFT_N2, 2 * FFT_N2), lambda i: (0, 0))
    return blk, tw, mat


def _fft2_filt_call(ar, ai, cst):
    ch = ar.shape[-1]
    blk, tw, mat = _fft2_specs(ch)
    return pl.pallas_call(
        _fft2_filt_body,
        grid=(FFT_KP,),
        in_specs=[blk, blk, tw, tw, mat],
        out_specs=pl.BlockSpec((None, 2 * FFT_N2, ch), lambda i: (i, 0, 0)),
        out_shape=jax.ShapeDtypeStruct((FFT_KP, 2 * FFT_N2, ch), BF16),
        compiler_params=_cparams(("arbitrary",), 32),
        name="fft2_filt",
    )(ar, ai, cst["twr"], cst["twi"], cst["f2p"])


def _fft2_conv_call(ar, ai, khat, cst):
    ch = ar.shape[-1]
    blk, tw, mat = _fft2_specs(ch)
    sh = jax.ShapeDtypeStruct((FFT_KP, FFT_N2, ch), BF16)
    return pl.pallas_call(
        _fft2_conv_body,
        grid=(FFT_KP,),
        in_specs=[blk, blk, tw, tw, pl.BlockSpec((None, 2 * FFT_N2, ch), lambda i: (i, 0, 0)), mat, mat],
        out_specs=[blk, blk],
        out_shape=[sh, sh],
        compiler_params=_cparams(("arbitrary",), 32),
        name="fft2_conv",
    )(ar, ai, cst["twr"], cst["twi"], khat, cst["f2p"], cst["f2pc"])


def _ifft1_body(gc_ref, gs_ref, br_ref, bi_ref, x0_ref, xv_ref, il_ref, ds_ref, o_ref):
    y = _dg(gc_ref[...], br_ref[...]) - _dg(gs_ref[...], bi_ref[...])
    o_ref[...] = (x0_ref[...].astype(F32)
                  * (y * il_ref[...] + ds_ref[...] * xv_ref[...].astype(F32))).astype(BF16)


def _ifft1_call(cst, br2d, bi2d, x0_2d, xv_2d, il_t, ds_t, cb):
    rows, cols = x0_2d.shape
    g = pl.BlockSpec((rows, FFT_KP), lambda i: (0, 0))
    kb = pl.BlockSpec((FFT_KP, cb), lambda i: (0, i))
    xb = pl.BlockSpec((rows, cb), lambda i: (0, i))
    vb = pl.BlockSpec((1, cb), lambda i: (0, 0))
    return pl.pallas_call(
        _ifft1_body,
        grid=(cols // cb,),
        in_specs=[g, g, kb, kb, xb, xb, vb, vb],
        out_specs=xb,
        out_shape=jax.ShapeDtypeStruct((rows, cols), BF16),
        compiler_params=_cparams(("arbitrary",), 32),
        name="ifft1",
    )(cst["gc"], cst["gs"], br2d, bi2d, x0_2d, xv_2d, il_t, ds_t)


def _outproj_body(hf_ref, hb_ref, zo_ref, yh_ref, x_ref, erow_ref, ecol_ref, gh_ref, wa_ref, wb_ref,
                  g1_ref, gf_ref, sh_ref, sc_ref, wr_ref, x1_ref, h2_ref, s_ref):
    hs = hf_ref[...].astype(F32) + hb_ref[...].astype(F32)
    gh = gh_ref[...]
    parts = []
    for h in range(N_HEADS):
        hh = hs[:, h * 128:(h + 1) * 128]
        ms = jnp.mean(hh * hh, axis=-1, keepdims=True)
        parts.append(hh * lax.rsqrt(ms + EPS) * gh[:, h * 128:(h + 1) * 128])
    ym = jnp.concatenate(parts, axis=-1) * _sigmoid(zo_ref[...].astype(F32))
    y = _dg(ym.astype(BF16), wa_ref[...]) + _dg(yh_ref[...], wb_ref[...])
    rp = x_ref.shape[0] // GRID_W
    erow8 = erow_ref[...]
    erow = erow8[0:rp, :]
    for q in range(1, 8 // rp):
        erow = jnp.where(pl.program_id(0) % (8 // rp) == q, erow8[q * rp:(q + 1) * rp, :], erow)
    x1 = _add_pos(x_ref[...], erow, ecol_ref[...]) + g1_ref[...] * y
    x1_ref[...] = x1
    h2 = _norm_mod(x1, gf_ref[...], sh_ref[...], sc_ref[...])
    h2_ref[...] = h2
    s_ref[...] = _sigmoid(_dot3(wr_ref[...], h2, _NT))


def _outproj_call(hf, hb, z, yh, x, erow, ecol, gh, wa, wb, g1, gf, sh2, sc2, wrt, tm):
    m, d = x.shape
    row = lambda cb: pl.BlockSpec((tm, 1024), lambda i: (i, cb))
    vec = lambda n: pl.BlockSpec((1, n), lambda i: (0, 0))
    full = pl.BlockSpec((tm, d), lambda i: (i, 0))
    return pl.pallas_call(
        _outproj_body,
        grid=(m // tm,),
        in_specs=[row(0), row(0), row(ZC_O), row(0), full,
                  pl.BlockSpec((8, d // 2), lambda i: (i * (tm // GRID_W) // 8, 0)),
                  pl.BlockSpec((GRID_W, d // 2), lambda i: (0, 0)),
                  vec(MV_W),
                  pl.BlockSpec((MV_W, d), lambda i: (0, 0)), pl.BlockSpec((HY_W, d), lambda i: (0, 0)),
                  vec(d), vec(d), vec(d), vec(d),
                  pl.BlockSpec((N_EXPERTS, d), lambda i: (0, 0))],
        out_specs=[full, full, pl.BlockSpec((N_EXPERTS, tm), lambda i: (0, i))],
        out_shape=[jax.ShapeDtypeStruct((m, d), F32), jax.ShapeDtypeStruct((m, d), F32),
                   jax.ShapeDtypeStruct((N_EXPERTS, m), F32)],
        compiler_params=_cparams(("arbitrary",), 56),
        name="outproj",
    )(hf, hb, z, yh, x, erow, ecol, gh, wa, wb, g1, gf, sh2, sc2, wrt)


def _first_max(x, idx, sentinel):
    m = jnp.max(x, axis=0, keepdims=True)
    return m, jnp.min(jnp.where(x == m, idx, sentinel), axis=0, keepdims=True)


def _route_body(s_ref, b_ref, e_ref, w_ref, r_ref, cnt_ref, u_scr, run_scr):
    i = pl.program_id(0)
    tt = s_ref.shape[1]

    @pl.when(i == 0)
    def _():
        rr = lax.broadcasted_iota(I32, (tt, tt), 0)
        cc = lax.broadcasted_iota(I32, (tt, tt), 1)
        u_scr[...] = jnp.where(rr < cc, 1.0, 0.0).astype(BF16)
        run_scr[...] = jnp.zeros_like(run_scr)

    s = s_ref[...]
    sel = s + b_ref[...][:, 0:1]
    sub8 = lax.broadcasted_iota(I32, (E_PER_GROUP, tt), 0).astype(F32)
    gs = jnp.zeros((N_GROUPS, tt), F32)
    for g in range(N_GROUPS):
        grp = sel[g * E_PER_GROUP:(g + 1) * E_PER_GROUP, :]
        m1, i1 = _first_max(grp, sub8, float(E_PER_GROUP))
        m2 = jnp.max(jnp.where(sub8 == i1, -jnp.inf, grp), axis=0, keepdims=True)
        gs = jnp.where(sub8 == g, m1 + m2, gs)
    gmask = jnp.zeros((N_GROUPS, tt), F32)
    for _ in range(TOPK_GROUPS):
        _, ig = _first_max(gs, sub8, float(N_GROUPS))
        hit = sub8 == ig
        gmask = jnp.where(hit, 1.0, gmask)
        gs = jnp.where(hit, -jnp.inf, gs)
    masked = jnp.concatenate(
        [jnp.where(jnp.broadcast_to(gmask[g:g + 1, :], (E_PER_GROUP, tt)) > 0.5,
                   sel[g * E_PER_GROUP:(g + 1) * E_PER_GROUP, :], -jnp.inf) for g in range(N_GROUPS)], axis=0)
    sub64 = lax.broadcasted_iota(I32, (N_EXPERTS, tt), 0).astype(F32)
    oh = jnp.zeros((N_EXPERTS, tt), F32)
    eks, wks = [], []
    for _ in range(TOP_K):
        _, ie = _first_max(masked, sub64, float(N_EXPERTS))
        hit = sub64 == ie
        wks.append(jnp.sum(jnp.where(hit, s, 0.0), axis=0, keepdims=True))
        eks.append(ie)
        masked = jnp.where(hit, -jnp.inf, masked)
        oh = jnp.where(hit, 1.0, oh)
    wsum = wks[0]
    for k in range(1, TOP_K):
        wsum = wsum + wks[k]
    run = run_scr[...]
    rank_t = _dg(oh.astype(BF16), u_scr[...]) + jnp.tile(run, (1, tt // 128))
    for k in range(TOP_K):
        e_ref[k:k + 1, :] = eks[k].astype(I32)
        w_ref[k:k + 1, :] = wks[k] / wsum * ROUTE_SCALE
        r_ref[k:k + 1, :] = jnp.sum(jnp.where(sub64 == eks[k], rank_t, 0.0), axis=0, keepdims=True).astype(I32)
    run_new = run + jnp.sum(oh, axis=1, keepdims=True)
    run_scr[...] = run_new
    cnt_ref[...] = run_new.astype(I32)


def _route_call(s_t, b_col, tt):
    m = s_t.shape[1]
    out = pl.BlockSpec((TOP_K, tt), lambda i: (0, i))
    return pl.pallas_call(
        _route_body,
        grid=(m // tt,),
        in_specs=[pl.BlockSpec((N_EXPERTS, tt), lambda i: (0, i)),
                  pl.BlockSpec((N_EXPERTS, 128), lambda i: (0, 0))],
        out_specs=[out, out, out, pl.BlockSpec((N_EXPERTS, 128), lambda i: (0, 0))],
        out_shape=[jax.ShapeDtypeStruct((TOP_K, m), I32), jax.ShapeDtypeStruct((TOP_K, m), F32),
                   jax.ShapeDtypeStruct((TOP_K, m), I32), jax.ShapeDtypeStruct((N_EXPERTS, 128), I32)],
        scratch_shapes=[pltpu.VMEM((tt, tt), BF16), pltpu.VMEM((N_EXPERTS, 128), F32)],
        compiler_params=_cparams(("arbitrary",), 32),
        name="route",
    )(s_t, b_col)


def _posk_body(pst_ref, e_ref, r_ref, p_ref):
    e = e_ref[...]
    acc = r_ref[...]
    for x in range(N_EXPERTS):
        acc = acc + jnp.where(e == x, pst_ref[x], 0)
    p_ref[...] = acc


def _posk_call(pstart, eidx, rank):
    k, m = eidx.shape
    tt = min(m, 2048)
    blk = pl.BlockSpec((k, tt), lambda i, pst: (0, i))
    return pl.pallas_call(
        _posk_body,
        grid_spec=pltpu.PrefetchScalarGridSpec(num_scalar_prefetch=1, grid=(m // tt,),
                                               in_specs=[blk, blk], out_specs=blk),
        out_shape=jax.ShapeDtypeStruct((k, m), I32),
        compiler_params=_cparams(("arbitrary",), 32),
        name="posk",
    )(pstart, eidx, rank)


def _dispatch_body(cnt_ref, pst_ref, pcn_ref, h2_ref, pos_ref, xs_ref, zrow, sem):
    i = pl.program_id(0)
    td = h2_ref.shape[0]

    def row_copy(t, dst):
        return pltpu.make_async_copy(h2_ref.at[pl.ds(t, 1), :], xs_ref.at[pl.ds(dst, 1), :], sem)

    def issue(t, carry):
        for k in range(TOP_K):
            row_copy(t, pos_ref[k, t]).start()
        return carry

    lax.fori_loop(0, td, issue, 0)

    def drain(t, carry):
        for k in range(TOP_K):
            row_copy(0, 0).wait()
        return carry

    lax.fori_loop(0, td, drain, 0)

    @pl.when(i == pl.num_programs(0) - 1)
    def _():
        zrow[...] = jnp.zeros_like(zrow)

        def zero_copy(dst):
            return pltpu.make_async_copy(zrow.at[pl.ds(0, 1), :], xs_ref.at[pl.ds(dst, 1), :], sem)

        def per_expert(e, carry):
            base = pst_ref[e]
            lax.fori_loop(cnt_ref[e], pcn_ref[e], lambda rr, cc: (zero_copy(base + rr).start(), cc)[1], 0)
            lax.fori_loop(cnt_ref[e], pcn_ref[e], lambda rr, cc: (zero_copy(0).wait(), cc)[1], 0)
            return carry

        lax.fori_loop(0, N_EXPERTS, per_expert, 0)


def _dispatch_call(cnt, pstart, pcnt, h2, pos, rows, td):
    m, d = h2.shape
    return pl.pallas_call(
        _dispatch_body,
        grid_spec=pltpu.PrefetchScalarGridSpec(
            num_scalar_prefetch=3, grid=(m // td,),
            in_specs=[pl.BlockSpec((td, d), lambda i, *_: (i, 0)),
                      pl.BlockSpec((TOP_K, td), lambda i, *_: (0, i), memory_space=pltpu.SMEM)],
            out_specs=pl.BlockSpec(memory_space=pl.ANY),
            scratch_shapes=[pltpu.VMEM((8, d), F32), pltpu.SemaphoreType.DMA(())]),
        out_shape=jax.ShapeDtypeStruct((rows, d), F32),
        compiler_params=_cparams(("arbitrary",), 32),
        name="dispatch",
    )(cnt, pstart, pcnt, h2, pos)


def _moe_body(te_ref, nu_ref, x_ref, w1_ref, w3_ref, w2_ref, y_ref, w1b, w3b, w2b):
    i = pl.program_id(0)
    used = i < nu_ref[0]
    prev = te_ref[jnp.maximum(i - 1, 0)]

    @pl.when(jnp.logical_and(used, jnp.logical_or(i == 0, te_ref[i] != prev)))
    def _():
        w1b[...] = w1_ref[...].astype(BF16)
        w3b[...] = w3_ref[...].astype(BF16)
        w2b[...] = w2_ref[...].astype(BF16)

    @pl.when(used)
    def _():
        x = x_ref[...].astype(BF16)
        a = _silu(_dg(x, w1b[...])) * _dg(x, w3b[...])
        y_ref[...] = _dg(a.astype(BF16), w2b[...])


def _moe_call(tile_e, n_used, xs, w1, w3, w2):
    rows, d = xs.shape
    nt = rows // MOE_ROWS
    de = w1.shape[-1]
    rmap = lambda i, te, nu: (jnp.minimum(i, nu[0] - 1), 0)
    wmap = lambda i, te, nu: (te[i], 0, 0)
    return pl.pallas_call(
        _moe_body,
        grid_spec=pltpu.PrefetchScalarGridSpec(
            num_scalar_prefetch=2, grid=(nt,),
            in_specs=[pl.BlockSpec((MOE_ROWS, d), rmap),
                      pl.BlockSpec((None, d, de), wmap), pl.BlockSpec((None, d, de), wmap),
                      pl.BlockSpec((None, de, d), wmap)],
            out_specs=pl.BlockSpec((MOE_ROWS, d), rmap),
            scratch_shapes=[pltpu.VMEM((d, de), BF16), pltpu.VMEM((d, de), BF16), pltpu.VMEM((de, d), BF16)]),
        out_shape=jax.ShapeDtypeStruct((rows, d), F32),
        compiler_params=_cparams(("arbitrary",), 56),
        name="moe",
    )(tile_e, n_used, xs, w1, w3, w2)


def _final_body(x1_ref, h2_ref, pos_ref, wt_ref, ys_ref, g2_ref, gn_ref, w1_ref, w3_ref, w2_ref,
                o_ref, ybuf, sem):
    tf = x1_ref.shape[0]

    def row_copy(k, t, src):
        return pltpu.make_async_copy(ys_ref.at[pl.ds(src, 1), :], ybuf.at[k, pl.ds(t, 1), :], sem)

    def issue(t, carry):
        for k in range(TOP_K):
            row_copy(k, t, pos_ref[k, t]).start()
        return carry

    lax.fori_loop(0, tf, issue, 0)

    hb = h2_ref[...].astype(BF16)
    shared = _dg((_silu(_dg(hb, w1_ref[...])) * _dg(hb, w3_ref[...])).astype(BF16), w2_ref[...])

    def drain(t, carry):
        for k in range(TOP_K):
            row_copy(0, 0, 0).wait()
        return carry

    lax.fori_loop(0, tf, drain, 0)

    wt = jnp.transpose(jnp.concatenate([wt_ref[...], jnp.zeros((tf - TOP_K, tf), F32)], axis=0))
    acc = shared
    for k in range(TOP_K):
        acc = acc + wt[:, k:k + 1] * ybuf[k]
    xo = x1_ref[...] + g2_ref[...] * acc
    ms = jnp.mean(xo * xo, axis=-1, keepdims=True)
    o_ref[...] = xo * lax.rsqrt(ms + EPS) * gn_ref[...]


def _final_call(x1, h2, pos, wts, ys, g2, gn, w1s, w3s, w2s, tf):
    m, d = x1.shape
    ds = w1s.shape[1]
    full = pl.BlockSpec((tf, d), lambda i: (i, 0))
    vec = pl.BlockSpec((1, d), lambda i: (0, 0))
    return pl.pallas_call(
        _final_body,
        grid=(m // tf,),
        in_specs=[full, full,
                  pl.BlockSpec((TOP_K, tf), lambda i: (0, i), memory_space=pltpu.SMEM),
                  pl.BlockSpec((TOP_K, tf), lambda i: (0, i)),
                  pl.BlockSpec(memory_space=pl.ANY), vec, vec,
                  pl.BlockSpec((d, ds), lambda i: (0, 0)), pl.BlockSpec((d, ds), lambda i: (0, 0)),
                  pl.BlockSpec((ds, d), lambda i: (0, 0))],
        out_specs=full,
        out_shape=jax.ShapeDtypeStruct((m, d), F32),
        scratch_shapes=[pltpu.VMEM((TOP_K, tf, d), F32), pltpu.SemaphoreType.DMA(())],
        compiler_params=_cparams(("arbitrary",), 48),
        name="final",
    )(x1, h2, pos, wts, ys, g2, gn, w1s, w3s, w2s)


def _pos_tables(n_tokens):
    rows = n_tokens // GRID_W
    quarter = D_MODEL // 4
    omega = 1.0 / (10000.0 ** (jnp.arange(quarter, dtype=F32) / quarter))

    def emb1d(pos):
        ang = pos[:, None] * omega[None]
        return jnp.concatenate([jnp.sin(ang), jnp.cos(ang)], axis=-1)

    return emb1d(jnp.arange(rows, dtype=F32)), emb1d(jnp.arange(GRID_W, dtype=F32))


def _filter_feats(L):
    n = jnp.arange(2 * L, dtype=I32)
    t = jnp.where(n <= L, n, 2 * L - n).astype(F32)
    t01 = t / max(L - 1, 1)
    w = 2.0 * math.pi * t / L
    bands = jnp.linspace(1e-4, FILT_BANDS - 1, FILT_BANDS, dtype=F32)
    valid = (n != L).astype(F32)
    feats = jnp.concatenate([t01[:, None], jnp.cos(w[:, None] * bands), -jnp.sin(w[:, None] * bands),
                             valid[:, None], jnp.zeros((2 * L, 64 - 34), F32)], axis=-1)
    return feats


def _pad_rows(a, rows):
    return jnp.concatenate([a, jnp.zeros((rows - a.shape[0],) + a.shape[1:], a.dtype)], axis=0)


def _layer(x, c, ctx, c_ctx, w_ada, b_ada, g_mix, g_ffn, w_in, b_gates, conv_k_w, conv_k_b,
           conv_q_w, conv_q_b, g_head, conv_hy_w, conv_hy_b, filt_w1, filt_b1, filt_freq1,
           filt_w2, filt_b2, filt_freq2, filt_w3, filt_b3, hy_dskip, w_out, w_router, b_router,
           w1_e, w3_e, w2_e, w1_s, w3_s, w2_s, g_final):
    L, d = x.shape
    lc = ctx.shape[0]
    row = lambda v: v.reshape(1, -1)

    cc = _pad_rows(jnp.stack([c, c_ctx], axis=0), 8)
    mods = _mod_call(cc, w_ada, row(b_ada))
    sh1, sc1, g1, sh2, sc2, g2 = [mods[0:1, k * d:(k + 1) * d] for k in range(6)]
    csh1, csc1 = mods[1:2, 0:d], mods[1:2, d:2 * d]

    w_r = jnp.concatenate([w_in[:, OFF_K:OFF_V], w_in[:, OFF_Q:OFF_O], w_in[:, OFF_V:OFF_G],
                           w_in[:, OFF_O:]], axis=1).astype(BF16)
    w_g = jnp.concatenate([w_in[:, OFF_G:OFF_Q], jnp.zeros((d, GATE_PAD - 4 * N_HEADS), F32)], axis=1)
    bg = jnp.concatenate([b_gates, jnp.zeros((GATE_PAD - 4 * N_HEADS,), F32)]).reshape(1, GATE_PAD)
    e_row, e_col = _pos_tables(L)
    conv_w = jnp.concatenate([conv_k_w, conv_q_w], axis=1)
    conv_b = jnp.concatenate([conv_k_b, conv_q_b]).reshape(1, -1)
    conv_s = jnp.concatenate([jnp.ones((QK_W,), F32), jnp.full((QK_W,), QK_HEAD ** -0.5, F32)]).reshape(1, -1)

    z_c, gt_c = _inproj_call(ctx, jnp.zeros((8, d // 2), F32), e_col, row(g_mix), csh1, csc1, w_r, w_g,
                             use_pos=False, tm=min(lc, 256))
    kq_c = _conv_kq_call(z_c, conv_w, conv_b, conv_s, tm=min(lc, 256))
    s0 = jnp.zeros((2 * N_HEADS, CHUNK, 256), F32)
    m0 = jnp.zeros((2, 8, 128), F32)
    _, _, s_ctx, m_ctx = _mlstm_call(kq_c, z_c, gt_c, bg, s0, m0)

    z, gates = _inproj_call(x, e_row, e_col, row(g_mix), sh1, sc1, w_r, w_g, use_pos=True, tm=min(L, 1024))
    kq = _conv_kq_call(z, conv_w, conv_b, conv_s, tm=min(L, 512))
    x0c, xv = _conv_hy_call(z, conv_hy_w, row(conv_hy_b), tm=min(L, 512))
    hf, hb, _, _ = _mlstm_call(kq, z, gates, bg, s_ctx, m_ctx)

    n1 = 2 * L // FFT_N2
    cst = _fft_consts(n1)
    rates = jnp.linspace(-math.log(DECAY_TARGET) / SLOW_DECAY_PCT, -math.log(DECAY_TARGET) / FAST_DECAY_PCT,
                         HY_W, dtype=F32).reshape(1, -1)
    w1p = _pad_rows(filt_w1, 64)
    kf, l1 = _filt_call(_filter_feats(L), w1p, row(filt_b1), row(filt_freq1), filt_w2, row(filt_b2),
                        row(filt_freq2), filt_w3, row(filt_b3), rates, tn=min(L, 1024))
    cols = FFT_N2 * HY_W
    cb = 2048
    kar, kai = _fft1_call(cst["f1"], kf.reshape(n1, cols), cb)
    khat = _fft2_filt_call(kar.reshape(FFT_KP, FFT_N2, HY_W), kai.reshape(FFT_KP, FFT_N2, HY_W), cst)
    uar, uai = _fft1_call(cst["f1"], xv.reshape(n1 // 2, cols), cb)
    br, bi = _fft2_conv_call(uar.reshape(FFT_KP, FFT_N2, HY_W), uai.reshape(FFT_KP, FFT_N2, HY_W), khat, cst)
    reps = cb // HY_W
    il_t = jnp.tile(1.0 / l1, (1, reps))
    ds_t = jnp.tile(row(hy_dskip), (1, reps))
    yh = _ifft1_call(cst, br.reshape(FFT_KP, cols), bi.reshape(FFT_KP, cols),
                     x0c.reshape(n1 // 2, cols), xv.reshape(n1 // 2, cols), il_t, ds_t, cb).reshape(L, HY_W)

    wo = w_out.astype(BF16)
    x1, h2, s_t = _outproj_call(hf, hb, z, yh, x, e_row, e_col, row(g_head), wo[:MV_W], wo[MV_W:],
                                g1, row(g_ffn), sh2, sc2, jnp.transpose(w_router), tm=min(L, 256))

    b_col = jnp.broadcast_to(b_router.reshape(N_EXPERTS, 1), (N_EXPERTS, 128))
    eidx, wts, rank, cnt2 = _route_call(s_t, b_col, tt=min(L, 1024))
    cnt = cnt2[:, 0]
    pcnt = (cnt + MOE_ROWS - 1) // MOE_ROWS * MOE_ROWS
    pend = jnp.cumsum(pcnt)
    pstart = pend - pcnt
    rows = L * TOP_K + N_EXPERTS * MOE_ROWS
    nt = rows // MOE_ROWS
    tile_e = jnp.minimum(jnp.searchsorted(pend, jnp.arange(nt, dtype=I32) * MOE_ROWS, side='right'),
                         N_EXPERTS - 1).astype(I32)
    n_used = (pend[-1] // MOE_ROWS).astype(I32).reshape(1)
    pos = _posk_call(pstart.astype(I32), eidx, rank)

    xs = _dispatch_call(cnt, pstart.astype(I32), pcnt.astype(I32), h2, pos, rows, td=min(L, 256))
    ys = _moe_call(tile_e, n_used, xs, w1_e, w3_e, w2_e)
    return _final_call(x1, h2, pos, wts, ys, g2, row(g_final), w1_s.astype(BF16), w3_s.astype(BF16),
                       w2_s.astype(BF16), tf=min(L, 128))


def kernel(x, c, ctx, c_ctx, w_ada, b_ada, g_mix, g_ffn, w_in, b_gates, conv_k_w, conv_k_b, conv_q_w,
           conv_q_b, g_head, conv_hy_w, conv_hy_b, filt_w1, filt_b1, filt_freq1, filt_w2, filt_b2,
           filt_freq2, filt_w3, filt_b3, hy_dskip, w_out, w_router, b_router, w1_e, w3_e, w2_e,
           w1_s, w3_s, w2_s, g_final):
    assert x.shape[0] == 1 and w_ada.shape[0] == 1, "one batch element, one layer"
    out = _layer(x[0], c[0], ctx[0], c_ctx, w_ada[0], b_ada[0], g_mix[0], g_ffn[0], w_in[0], b_gates[0],
                 conv_k_w[0], conv_k_b[0], conv_q_w[0], conv_q_b[0], g_head[0], conv_hy_w[0], conv_hy_b[0],
                 filt_w1[0], filt_b1[0], filt_freq1[0], filt_w2[0], filt_b2[0], filt_freq2[0], filt_w3[0],
                 filt_b3[0], hy_dskip[0], w_out[0], w_router[0], b_router[0], w1_e[0], w3_e[0], w2_e[0],
                 w1_s[0], w3_s[0], w2_s[0], g_final)
    return out[None]
```

```python
import functools
import math

import numpy as np
import jax
import jax.numpy as jnp
from jax import lax
from jax.experimental import pallas as pl
from jax.experimental.pallas import tpu as pltpu

F32 = jnp.float32
BF16 = jnp.bfloat16
I32 = jnp.int32

D_MODEL = 2048
GRID_W = 64
N_HEADS = 8
QK_HEAD = 64
V_HEAD = 128
QK_W = N_HEADS * QK_HEAD
MV_W = N_HEADS * V_HEAD
HY_W = D_MODEL - MV_W
CHUNK = 128
FILT_BANDS = 16
FILT_HIDDEN = 64
DECAY_TARGET = 1e-2
FAST_DECAY_PCT = 0.3
SLOW_DECAY_PCT = 1.5
N_EXPERTS = 64
N_GROUPS = 8
E_PER_GROUP = 8
TOPK_GROUPS = 4
TOP_K = 8
D_EXPERT = 512
ROUTE_SCALE = 2.5
EPS = 1e-6
OFF_K = 0
OFF_V = OFF_K + QK_W
OFF_G = OFF_V + MV_W
OFF_Q = OFF_G + 4 * N_HEADS
OFF_O = OFF_Q + QK_W
OFF_HY = OFF_O + MV_W

ZC_KQ, ZC_V, ZC_O, ZC_X0, ZC_X1, ZC_HV = 0, 1, 2, 3, 4, 5
Z_COLS = 6 * 1024
GATE_PAD = 128

NEG = -1e30
MIB = 1024 * 1024

FFT_N2 = 128
FFT_KP = 144

MOE_ROWS = 256


def _cparams(sem, vmem_mb):
    return pltpu.CompilerParams(dimension_semantics=sem, vmem_limit_bytes=vmem_mb * MIB)


def _split2(x):
    hi = x.astype(BF16)
    lo = (x - hi.astype(F32)).astype(BF16)
    return hi, lo


_NN = (((1,), (0,)), ((), ()))
_NT = (((1,), (1,)), ((), ()))
_TN = (((0,), (0,)), ((), ()))


def _dg(a, b, dims=_NN):
    return lax.dot_general(a, b, dims, preferred_element_type=F32)


def _dot3(a, b, dims=_NN):
    ah, al = _split2(a)
    bh, bl = _split2(b)
    return _dg(ah, bh, dims) + _dg(al, bh, dims) + _dg(ah, bl, dims)


def _sigmoid(x):
    return 1.0 / (1.0 + jnp.exp(-x))


def _silu(x):
    return x * _sigmoid(x)


def _norm_mod(x, g, sh, sc):
    ms = jnp.mean(x * x, axis=-1, keepdims=True)
    return (x * lax.rsqrt(ms + EPS) * g) * (1.0 + sc) + sh


def _add_pos(x, erow, ecol):
    tm, d = x.shape
    half = d // 2
    parts = []
    for r in range(tm // GRID_W):
        xs = x[r * GRID_W:(r + 1) * GRID_W, :]
        parts.append(jnp.concatenate([xs[:, :half] + erow[r:r + 1, :], xs[:, half:] + ecol], axis=-1))
    return parts[0] if len(parts) == 1 else jnp.concatenate(parts, axis=0)


def _mod_body(cc_ref, w_ref, b_ref, o_ref):
    o_ref[...] = _dot3(_silu(cc_ref[...]), w_ref[...]) + b_ref[...]


def _mod_call(cc, w, b):
    d, n = w.shape
    tn = 1024
    return pl.pallas_call(
        _mod_body,
        grid=(n // tn,),
        in_specs=[pl.BlockSpec((8, d), lambda j: (0, 0)),
                  pl.BlockSpec((d, tn), lambda j: (0, j)),
                  pl.BlockSpec((1, tn), lambda j: (0, j))],
        out_specs=pl.BlockSpec((8, tn), lambda j: (0, j)),
        out_shape=jax.ShapeDtypeStruct((8, n), F32),
        compiler_params=_cparams(("arbitrary",), 40),
        name="mod",
    )(cc, w, b)


def _inproj_body(use_pos, x_ref, erow_ref, ecol_ref, gm_ref, sh_ref, sc_ref, w_ref, wg_ref,
                 z_ref, g_ref, h_scr):
    @pl.when(pl.program_id(1) == 0)
    def _():
        x = x_ref[...]
        if use_pos:
            x = _add_pos(x, erow_ref[...], ecol_ref[...])
        h = _norm_mod(x, gm_ref[...], sh_ref[...], sc_ref[...])
        h_scr[...] = h.astype(BF16)
        g_ref[...] = _dot3(h, wg_ref[...])

    z_ref[...] = jnp.dot(h_scr[...], w_ref[...], preferred_element_type=F32).astype(BF16)


def _inproj_call(x, erow, ecol, gm, sh, sc, w, wg, use_pos, tm):
    m, d = x.shape
    tn = 512
    er = tm // GRID_W if use_pos else erow.shape[0]
    row_map = (lambda i, j: (i, 0)) if use_pos else (lambda i, j: (0, 0))
    return pl.pallas_call(
        functools.partial(_inproj_body, use_pos),
        grid=(m // tm, Z_COLS // tn),
        in_specs=[pl.BlockSpec((tm, d), lambda i, j: (i, 0)),
                  pl.BlockSpec((er, d // 2), row_map),
                  pl.BlockSpec((GRID_W, d // 2), lambda i, j: (0, 0)),
                  pl.BlockSpec((1, d), lambda i, j: (0, 0)),
                  pl.BlockSpec((1, d), lambda i, j: (0, 0)),
                  pl.BlockSpec((1, d), lambda i, j: (0, 0)),
                  pl.BlockSpec((d, tn), lambda i, j: (0, j)),
                  pl.BlockSpec((d, GATE_PAD), lambda i, j: (0, 0))],
        out_specs=[pl.BlockSpec((tm, tn), lambda i, j: (i, j)),
                   pl.BlockSpec((tm, GATE_PAD), lambda i, j: (i, 0))],
        out_shape=[jax.ShapeDtypeStruct((m, Z_COLS), BF16),
                   jax.ShapeDtypeStruct((m, GATE_PAD), F32)],
        scratch_shapes=[pltpu.VMEM((tm, d), BF16)],
        compiler_params=_cparams(("arbitrary", "arbitrary"), 48),
        name="inproj",
    )(x, erow, ecol, gm, sh, sc, w, wg)


def _conv3(zc, zp, zn, w, b, first, last):
    tm = zc.shape[0]
    row = lax.broadcasted_iota(I32, zc.shape, 0)
    prev_row = jnp.where(first, 0.0, zp[7:8, :])
    next_row = jnp.where(last, 0.0, zn[0:1, :])
    xm = jnp.where(row == 0, prev_row, pltpu.roll(zc, 1, 0))
    xp = jnp.where(row == tm - 1, next_row, pltpu.roll(zc, tm - 1, 0))
    return xm * w[0:1, :] + zc * w[1:2, :] + xp * w[2:3, :] + b


def _conv_kq_body(zc_ref, zp_ref, zn_ref, w_ref, b_ref, s_ref, o_ref):
    i = pl.program_id(0)
    u = _conv3(zc_ref[...].astype(F32), zp_ref[...].astype(F32), zn_ref[...].astype(F32),
               w_ref[...], b_ref[...], i == 0, i == pl.num_programs(0) - 1)
    o_ref[...] = (_silu(u) * s_ref[...]).astype(BF16)


def _halo_specs(tm, m, cb):
    nb8 = m // 8
    return [pl.BlockSpec((tm, 1024), lambda i: (i, cb)),
            pl.BlockSpec((8, 1024), lambda i: (jnp.maximum(i * (tm // 8) - 1, 0), cb)),
            pl.BlockSpec((8, 1024), lambda i: (jnp.minimum((i + 1) * (tm // 8), nb8 - 1), cb))]


def _conv_kq_call(z, w, b, s, tm):
    m = z.shape[0]
    vec = pl.BlockSpec((1, 1024), lambda i: (0, 0))
    return pl.pallas_call(
        _conv_kq_body,
        grid=(m // tm,),
        in_specs=_halo_specs(tm, m, ZC_KQ) + [pl.BlockSpec((3, 1024), lambda i: (0, 0)), vec, vec],
        out_specs=pl.BlockSpec((tm, 1024), lambda i: (i, 0)),
        out_shape=jax.ShapeDtypeStruct((m, 1024), BF16),
        compiler_params=_cparams(("arbitrary",), 32),
        name="conv_kq",
    )(z, z, z, w, b, s)


def _conv_hy_body(ac_ref, ap_ref, an_ref, bc_ref, bp_ref, bn_ref, cc_ref, cp_ref, cn_ref,
                  w_ref, b_ref, x0_ref, xv_ref):
    i = pl.program_id(0)
    first, last = i == 0, i == pl.num_programs(0) - 1
    w = w_ref[...]
    b = b_ref[...]

    def cv(c, p, n, k):
        return _conv3(c[...].astype(F32), p[...].astype(F32), n[...].astype(F32),
                      w[:, k * 1024:(k + 1) * 1024], b[:, k * 1024:(k + 1) * 1024], first, last)

    x0_ref[...] = cv(ac_ref, ap_ref, an_ref, 0).astype(BF16)
    xv_ref[...] = (cv(bc_ref, bp_ref, bn_ref, 1) * cv(cc_ref, cp_ref, cn_ref, 2)).astype(BF16)


def _conv_hy_call(z, w, b, tm):
    m = z.shape[0]
    out = pl.BlockSpec((tm, 1024), lambda i: (i, 0))
    return pl.pallas_call(
        _conv_hy_body,
        grid=(m // tm,),
        in_specs=(_halo_specs(tm, m, ZC_X0) + _halo_specs(tm, m, ZC_X1) + _halo_specs(tm, m, ZC_HV)
                  + [pl.BlockSpec((3, 3072), lambda i: (0, 0)), pl.BlockSpec((1, 3072), lambda i: (0, 0))]),
        out_specs=[out, out],
        out_shape=[jax.ShapeDtypeStruct((m, 1024), BF16), jax.ShapeDtypeStruct((m, 1024), BF16)],
        compiler_params=_cparams(("arbitrary",), 32),
        name="conv_hy",
    )(z, z, z, z, z, z, z, z, z, w, b)


def _mlstm_body(kqf_ref, vf_ref, gf_ref, kqb_ref, vb_ref, gb_ref, bg_ref, s0_ref, m0_ref,
                hf_ref, hb_ref, sfin_ref, mfin_ref, s_scr, m_scr):
    j = pl.program_id(0)

    @pl.when(j == 0)
    def _():
        s_scr[...] = s0_ref[...]
        m_scr[...] = m0_ref[...]

    r = lax.broadcasted_iota(I32, (CHUNK, CHUNK), 0)
    c = lax.broadcasted_iota(I32, (CHUNK, CHUNK), 1)
    e1 = jnp.where(c == 0, 1.0, 0.0).astype(BF16)
    bg = bg_ref[...]

    for d in range(2):
        kq = (kqf_ref, kqb_ref)[d][...]
        v = (vf_ref, vb_ref)[d][...]
        g_all = (gf_ref, gb_ref)[d][...] + bg
        out_ref = (hf_ref, hb_ref)[d]
        tri = (r >= c) if d == 0 else (c >= r)
        tri_b = jnp.where(tri, 1.0, 0.0).astype(BF16)
        gi = g_all if d == 0 else pltpu.roll(g_all, CHUNK - 16, 1)
        gfp = pltpu.roll(g_all, CHUNK - 8 - 16 * d, 1)
        lf = jnp.minimum(gfp, 0.0) - jnp.log(1.0 + jnp.exp(-jnp.abs(gfp)))
        l1 = lf.astype(BF16)
        r1 = lf - l1.astype(F32)
        l2 = r1.astype(BF16)
        l3 = (r1 - l2.astype(F32)).astype(BF16)
        bcum = _dg(tri_b, l1) + _dg(tri_b, l2) + _dg(tri_b, l3)
        gtot = bcum[CHUNK - 1:CHUNK, :] if d == 0 else bcum[0:1, :]
        acol = gtot - bcum + gi
        m_loc = jnp.max(acol, axis=0, keepdims=True)
        wg = jnp.exp(acol - m_loc)
        m_st = m_scr[d, 0:1, :]
        inter = bcum + m_st
        m_new = jnp.maximum(gtot + m_st, m_loc)
        sp = jnp.broadcast_to(jnp.exp(gtot + m_st - m_new), (CHUNK, CHUNK))
        sl = jnp.broadcast_to(jnp.exp(m_loc - m_new), (CHUNK, CHUNK))
        rt = jnp.transpose(gi - bcum)

        for h in range(N_HEADS):
            p, half = divmod(h, 2)
            lm = (c // QK_HEAD) == half
            kp = kq[:, p * 128:(p + 1) * 128]
            qp = kq[:, QK_W + p * 128:QK_W + (p + 1) * 128]
            vaug = jnp.concatenate([v[:, h * 128:(h + 1) * 128], e1], axis=1)
            qm = jnp.where(lm, qp, jnp.zeros_like(qp))
            dl = jnp.where(tri, bcum[:, h:h + 1] + rt[h:h + 1, :], NEG)
            icol = inter[:, h:h + 1]
            mt = jnp.maximum(icol, jnp.max(dl, axis=1, keepdims=True))
            pm = jnp.exp(dl - mt)
            s = (_dg(qm, kp, _NT) * pm).astype(BF16)
            st = s_scr[d * N_HEADS + h]
            tot = _dg(s, vaug) + jnp.exp(icol - mt) * _dg(qm, st.astype(BF16))
            den = jnp.maximum(jnp.abs(tot[:, 128:129]), jnp.exp(-mt))
            out_ref[:, h * 128:(h + 1) * 128] = (tot[:, :128] / den).astype(BF16)
            kw = jnp.where(lm, kp.astype(F32) * wg[:, h:h + 1], 0.0).astype(BF16)
            s_scr[d * N_HEADS + h] = sp[:, h:h + 1] * st + sl[:, h:h + 1] * _dg(kw, vaug, _TN)
        m_scr[d, 0:1, :] = m_new

    @pl.when(j == pl.num_programs(0) - 1)
    def _():
        sfin_ref[...] = s_scr[...]
        mfin_ref[...] = m_scr[...]


def _mlstm_call(kq, z, gates, bg, s0, m0):
    m = kq.shape[0]
    nc = m // CHUNK
    fwd = lambda cb: (lambda j: (j, cb))
    bwd = lambda cb: (lambda j: (nc - 1 - j, cb))
    st_spec = pl.BlockSpec((2 * N_HEADS, CHUNK, 256), lambda j: (0, 0, 0))
    m_spec = pl.BlockSpec((2, 8, 128), lambda j: (0, 0, 0))
    return pl.pallas_call(
        _mlstm_body,
        grid=(nc,),
        in_specs=[pl.BlockSpec((CHUNK, 1024), fwd(0)), pl.BlockSpec((CHUNK, 1024), fwd(ZC_V)),
                  pl.BlockSpec((CHUNK, GATE_PAD), fwd(0)),
                  pl.BlockSpec((CHUNK, 1024), bwd(0)), pl.BlockSpec((CHUNK, 1024), bwd(ZC_V)),
                  pl.BlockSpec((CHUNK, GATE_PAD), bwd(0)),
                  pl.BlockSpec((1, GATE_PAD), lambda j: (0, 0)), st_spec, m_spec],
        out_specs=[pl.BlockSpec((CHUNK, 1024), fwd(0)), pl.BlockSpec((CHUNK, 1024), bwd(0)), st_spec, m_spec],
        out_shape=[jax.ShapeDtypeStruct((m, 1024), BF16), jax.ShapeDtypeStruct((m, 1024), BF16),
                   jax.ShapeDtypeStruct((2 * N_HEADS, CHUNK, 256), F32),
                   jax.ShapeDtypeStruct((2, 8, 128), F32)],
        scratch_shapes=[pltpu.VMEM((2 * N_HEADS, CHUNK, 256), F32), pltpu.VMEM((2, 8, 128), F32)],
        compiler_params=_cparams(("arbitrary",), 32),
        name="mlstm",
    )(kq, z, gates, kq, z, gates, bg, s0, m0)


def _filt_body(ft_ref, w1_ref, b1_ref, f1_ref, w2_ref, b2_ref, f2_ref, w3_ref, b3_ref, rt_ref,
               kf_ref, l1_ref):
    ft = ft_ref[...]
    h1 = jnp.sin(f1_ref[...] * (_dot3(ft, w1_ref[...]) + b1_ref[...]))
    h2 = jnp.sin(f2_ref[...] * (_dot3(h1, w2_ref[...]) + b2_ref[...]))
    h = _dot3(h2, w3_ref[...]) + b3_ref[...]
    h = h * jnp.exp(-ft[:, 0:1] * rt_ref[...]) * ft[:, 33:34]
    kf_ref[...] = h.astype(BF16)

    @pl.when(pl.program_id(0) == 0)
    def _():
        l1_ref[...] = jnp.zeros_like(l1_ref)

    l1_ref[...] += jnp.sum(jnp.abs(h), axis=0, keepdims=True)


def _filt_call(feats, w1, b1, f1, w2, b2, f2, w3, b3, rates, tn):
    n = feats.shape[0]
    nt = n // tn
    half = lambda i: (0, jnp.where(i >= nt // 2, 1, 0))
    c64 = lambda shape: pl.BlockSpec(shape, lambda i: (0, 0))
    return pl.pallas_call(
        _filt_body,
        grid=(nt,),
        in_specs=[pl.BlockSpec((tn, 64), lambda i: (i, 0)),
                  c64((64, 64)), c64((1, 64)), c64((1, 64)), c64((64, 64)), c64((1, 64)), c64((1, 64)),
                  pl.BlockSpec((64, HY_W), half), pl.BlockSpec((1, HY_W), half), c64((1, HY_W))],
        out_specs=[pl.BlockSpec((tn, HY_W), lambda i: (i, 0)), pl.BlockSpec((1, HY_W), lambda i: (0, 0))],
        out_shape=[jax.ShapeDtypeStruct((n, HY_W), BF16), jax.ShapeDtypeStruct((1, HY_W), F32)],
        compiler_params=_cparams(("arbitrary",), 32),
        name="filt",
    )(feats, w1, b1, f1, w2, b2, f2, w3, b3, rates)


def _fft_consts(n1_rows):
    n = n1_rows * FFT_N2
    kv = n1_rows // 2 + 1
    k1 = np.arange(FFT_KP, dtype=np.float64)
    valid = (k1 < kv).astype(np.float64)
    n1 = np.arange(n1_rows, dtype=np.float64)
    th1 = 2.0 * np.pi * np.outer(k1, n1) / n1_rows
    f1 = np.concatenate([np.cos(th1) * valid[:, None], -np.sin(th1) * valid[:, None]], axis=0)
    n2 = np.arange(FFT_N2, dtype=np.float64)
    tht = 2.0 * np.pi * np.outer(k1, n2) / n
    rep = lambda a: jnp.broadcast_to(jnp.asarray(a, F32)[:, :, None], (FFT_KP, FFT_N2, 128))
    twr = rep(np.cos(tht) * valid[:, None])
    twi = rep(-np.sin(tht) * valid[:, None])
    th2 = 2.0 * np.pi * np.outer(n2, n2) / FFT_N2
    cs, sn = np.cos(th2), np.sin(th2)
    f2p = np.block([[cs, sn], [-sn, cs]])
    f2pc = np.block([[cs, -sn], [sn, cs]])
    wk = np.where((k1 == 0) | (k1 == kv - 1), 1.0, 2.0) * valid / n
    half = n1_rows // 2
    thi = 2.0 * np.pi * np.outer(n1[:half], k1) / n1_rows
    gc = np.cos(thi) * wk[None, :]
    gs = np.sin(thi) * wk[None, :]
    as_bf = lambda a: jnp.asarray(a, F32).astype(BF16)
    return dict(f1=as_bf(f1), twr=twr, twi=twi,
                f2p=as_bf(f2p), f2pc=as_bf(f2pc), gc=as_bf(gc), gs=as_bf(gs))


def _fft1_body(f_ref, x_ref, ar_ref, ai_ref):
    o = _dg(f_ref[...], x_ref[...])
    ar_ref[...] = o[:FFT_KP].astype(BF16)
    ai_ref[...] = o[FFT_KP:].astype(BF16)


def _fft1_call(f1, x2d, cb):
    k, cols = x2d.shape
    f1 = f1[:, :k]
    out = pl.BlockSpec((FFT_KP, cb), lambda i: (0, i))
    sh = jax.ShapeDtypeStruct((FFT_KP, cols), BF16)
    return pl.pallas_call(
        _fft1_body,
        grid=(cols // cb,),
        in_specs=[pl.BlockSpec((2 * FFT_KP, k), lambda i: (0, 0)), pl.BlockSpec((k, cb), lambda i: (0, i))],
        out_specs=[out, out],
        out_shape=[sh, sh],
        compiler_params=_cparams(("arbitrary",), 32),
        name="fft1",
    )(f1, x2d)


def _twiddled(ar_ref, ai_ref, twr_ref, twi_ref, reps):
    a_r = ar_ref[...].astype(F32)
    a_i = ai_ref[...].astype(F32)
    tr = jnp.tile(twr_ref[...], (1, reps))
    ti = jnp.tile(twi_ref[...], (1, reps))
    st = jnp.concatenate([a_r * tr - a_i * ti, a_r * ti + a_i * tr], axis=0).astype(BF16)
    return st, tr, ti


def _fft2_filt_body(ar_ref, ai_ref, twr_ref, twi_ref, f2p_ref, k_ref):
    st, _, _ = _twiddled(ar_ref, ai_ref, twr_ref, twi_ref, ar_ref.shape[-1] // 128)
    k_ref[...] = _dg(f2p_ref[...], st).astype(BF16)


def _fft2_conv_body(ar_ref, ai_ref, twr_ref, twi_ref, k_ref, f2p_ref, f2pc_ref, br_ref, bi_ref):
    st, tr, ti = _twiddled(ar_ref, ai_ref, twr_ref, twi_ref, ar_ref.shape[-1] // 128)
    x = _dg(f2p_ref[...], st)
    xr, xi = x[:FFT_N2], x[FFT_N2:]
    kr = k_ref[:FFT_N2, :].astype(F32)
    ki = k_ref[FFT_N2:, :].astype(F32)
    sy = jnp.concatenate([xr * kr - xi * ki, xr * ki + xi * kr], axis=0).astype(BF16)
    b = _dg(f2pc_ref[...], sy)
    b_r, b_i = b[:FFT_N2], b[FFT_N2:]
    br_ref[...] = (b_r * tr + b_i * ti).astype(BF16)
    bi_ref[...] = (b_i * tr - b_r * ti).astype(BF16)


def _fft2_specs(ch):
    blk = pl.BlockSpec((None, FFT_N2, ch), lambda i: (i, 0, 0))
    tw = pl.BlockSpec((None, FFT_N2, 128), lambda i: (i, 0, 0))
    mat = pl.BlockSpec((2 * FFT_N2, 2 * FFT_N2), lambda i: (0, 0))
    return blk, tw, mat


def _fft2_filt_call(ar, ai, cst):
    ch = ar.shape[-1]
    blk, tw, mat = _fft2_specs(ch)
    return pl.pallas_call(
        _fft2_filt_body,
        grid=(FFT_KP,),
        in_specs=[blk, blk, tw, tw, mat],
        out_specs=pl.BlockSpec((None, 2 * FFT_N2, ch), lambda i: (i, 0, 0)),
        out_shape=jax.ShapeDtypeStruct((FFT_KP, 2 * FFT_N2, ch), BF16),
        compiler_params=_cparams(("arbitrary",), 32),
        name="fft2_filt",
    )(ar, ai, cst["twr"], cst["twi"], cst["f2p"])


def _fft2_conv_call(ar, ai, khat, cst):
    ch = ar.shape[-1]
    blk, tw, mat = _fft2_specs(ch)
    sh = jax.ShapeDtypeStruct((FFT_KP, FFT_N2, ch), BF16)
    return pl.pallas_call(
        _fft2_conv_body,
        grid=(FFT_KP,),
        in_specs=[blk, blk, tw, tw, pl.BlockSpec((None, 2 * FFT_N2, ch), lambda i: (i, 0, 0)), mat, mat],
        out_specs=[blk, blk],
        out_shape=[sh, sh],
        compiler_params=_cparams(("arbitrary",), 32),
        name="fft2_conv",
    )(ar, ai, cst["twr"], cst["twi"], khat, cst["f2p"], cst["f2pc"])


def _ifft1_body(gc_ref, gs_ref, br_ref, bi_ref, x0_ref, xv_ref, il_ref, ds_ref, o_ref):
    y = _dg(gc_ref[...], br_ref[...]) - _dg(gs_ref[...], bi_ref[...])
    o_ref[...] = (x0_ref[...].astype(F32)
                  * (y * il_ref[...] + ds_ref[...] * xv_ref[...].astype(F32))).astype(BF16)


def _ifft1_call(cst, br2d, bi2d, x0_2d, xv_2d, il_t, ds_t, cb):
    rows, cols = x0_2d.shape
    g = pl.BlockSpec((rows, FFT_KP), lambda i: (0, 0))
    kb = pl.BlockSpec((FFT_KP, cb), lambda i: (0, i))
    xb = pl.BlockSpec((rows, cb), lambda i: (0, i))
    vb = pl.BlockSpec((1, cb), lambda i: (0, 0))
    return pl.pallas_call(
        _ifft1_body,
        grid=(cols // cb,),
        in_specs=[g, g, kb, kb, xb, xb, vb, vb],
        out_specs=xb,
        out_shape=jax.ShapeDtypeStruct((rows, cols), BF16),
        compiler_params=_cparams(("arbitrary",), 32),
        name="ifft1",
    )(cst["gc"], cst["gs"], br2d, bi2d, x0_2d, xv_2d, il_t, ds_t)


def _outproj_body(hf_ref, hb_ref, zo_ref, yh_ref, x_ref, erow_ref, ecol_ref, gh_ref, wa_ref, wb_ref,
                  g1_ref, gf_ref, sh_ref, sc_ref, wr_ref, x1_ref, h2_ref, s_ref):
    hs = hf_ref[...].astype(F32) + hb_ref[...].astype(F32)
    gh = gh_ref[...]
    parts = []
    for h in range(N_HEADS):
        hh = hs[:, h * 128:(h + 1) * 128]
        ms = jnp.mean(hh * hh, axis=-1, keepdims=True)
        parts.append(hh * lax.rsqrt(ms + EPS) * gh[:, h * 128:(h + 1) * 128])
    ym = jnp.concatenate(parts, axis=-1) * _sigmoid(zo_ref[...].astype(F32))
    y = _dg(ym.astype(BF16), wa_ref[...]) + _dg(yh_ref[...], wb_ref[...])
    rp = x_ref.shape[0] // GRID_W
    erow8 = erow_ref[...]
    erow = erow8[0:rp, :]
    for q in range(1, 8 // rp):
        erow = jnp.where(pl.program_id(0) % (8 // rp) == q, erow8[q * rp:(q + 1) * rp, :], erow)
    x1 = _add_pos(x_ref[...], erow, ecol_ref[...]) + g1_ref[...] * y
    x1_ref[...] = x1
    h2 = _norm_mod(x1, gf_ref[...], sh_ref[...], sc_ref[...])
    h2_ref[...] = h2
    s_ref[...] = _sigmoid(_dot3(wr_ref[...], h2, _NT))


def _outproj_call(hf, hb, z, yh, x, erow, ecol, gh, wa, wb, g1, gf, sh2, sc2, wrt, tm):
    m, d = x.shape
    row = lambda cb: pl.BlockSpec((tm, 1024), lambda i: (i, cb))
    vec = lambda n: pl.BlockSpec((1, n), lambda i: (0, 0))
    full = pl.BlockSpec((tm, d), lambda i: (i, 0))
    return pl.pallas_call(
        _outproj_body,
        grid=(m // tm,),
        in_specs=[row(0), row(0), row(ZC_O), row(0), full,
                  pl.BlockSpec((8, d // 2), lambda i: (i * (tm // GRID_W) // 8, 0)),
                  pl.BlockSpec((GRID_W, d // 2), lambda i: (0, 0)),
                  vec(MV_W),
                  pl.BlockSpec((MV_W, d), lambda i: (0, 0)), pl.BlockSpec((HY_W, d), lambda i: (0, 0)),
                  vec(d), vec(d), vec(d), vec(d),
                  pl.BlockSpec((N_EXPERTS, d), lambda i: (0, 0))],
        out_specs=[full, full, pl.BlockSpec((N_EXPERTS, tm), lambda i: (0, i))],
        out_shape=[jax.ShapeDtypeStruct((m, d), F32), jax.ShapeDtypeStruct((m, d), F32),
                   jax.ShapeDtypeStruct((N_EXPERTS, m), F32)],
        compiler_params=_cparams(("arbitrary",), 56),
        name="outproj",
    )(hf, hb, z, yh, x, erow, ecol, gh, wa, wb, g1, gf, sh2, sc2, wrt)


def _first_max(x, idx, sentinel):
    m = jnp.max(x, axis=0, keepdims=True)
    return m, jnp.min(jnp.where(x == m, idx, sentinel), axis=0, keepdims=True)


def _route_body(s_ref, b_ref, e_ref, w_ref, r_ref, cnt_ref, u_scr, run_scr):
    i = pl.program_id(0)
    tt = s_ref.shape[1]

    @pl.when(i == 0)
    def _():
        rr = lax.broadcasted_iota(I32, (tt, tt), 0)
        cc = lax.broadcasted_iota(I32, (tt, tt), 1)
        u_scr[...] = jnp.where(rr < cc, 1.0, 0.0).astype(BF16)
        run_scr[...] = jnp.zeros_like(run_scr)

    s = s_ref[...]
    sel = s + b_ref[...][:, 0:1]
    sub8 = lax.broadcasted_iota(I32, (E_PER_GROUP, tt), 0).astype(F32)
    gs = jnp.zeros((N_GROUPS, tt), F32)
    for g in range(N_GROUPS):
        grp = sel[g * E_PER_GROUP:(g + 1) * E_PER_GROUP, :]
        m1, i1 = _first_max(grp, sub8, float(E_PER_GROUP))
        m2 = jnp.max(jnp.where(sub8 == i1, -jnp.inf, grp), axis=0, keepdims=True)
        gs = jnp.where(sub8 == g, m1 + m2, gs)
    gmask = jnp.zeros((N_GROUPS, tt), F32)
    for _ in range(TOPK_GROUPS):
        _, ig = _first_max(gs, sub8, float(N_GROUPS))
        hit = sub8 == ig
        gmask = jnp.where(hit, 1.0, gmask)
        gs = jnp.where(hit, -jnp.inf, gs)
    masked = jnp.concatenate(
        [jnp.where(jnp.broadcast_to(gmask[g:g + 1, :], (E_PER_GROUP, tt)) > 0.5,
                   sel[g * E_PER_GROUP:(g + 1) * E_PER_GROUP, :], -jnp.inf) for g in range(N_GROUPS)], axis=0)
    sub64 = lax.broadcasted_iota(I32, (N_EXPERTS, tt), 0).astype(F32)
    oh = jnp.zeros((N_EXPERTS, tt), F32)
    eks, wks = [], []
    for _ in range(TOP_K):
        _, ie = _first_max(masked, sub64, float(N_EXPERTS))
        hit = sub64 == ie
        wks.append(jnp.sum(jnp.where(hit, s, 0.0), axis=0, keepdims=True))
        eks.append(ie)
        masked = jnp.where(hit, -jnp.inf, masked)
        oh = jnp.where(hit, 1.0, oh)
    wsum = wks[0]
    for k in range(1, TOP_K):
        wsum = wsum + wks[k]
    run = run_scr[...]
    rank_t = _dg(oh.astype(BF16), u_scr[...]) + jnp.tile(run, (1, tt // 128))
    for k in range(TOP_K):
        e_ref[k:k + 1, :] = eks[k].astype(I32)
        w_ref[k:k + 1, :] = wks[k] / wsum * ROUTE_SCALE
        r_ref[k:k + 1, :] = jnp.sum(jnp.where(sub64 == eks[k], rank_t, 0.0), axis=0, keepdims=True).astype(I32)
    run_new = run + jnp.sum(oh, axis=1, keepdims=True)
    run_scr[...] = run_new
    cnt_ref[...] = run_new.astype(I32)


def _route_call(s_t, b_col, tt):
    m = s_t.shape[1]
    out = pl.BlockSpec((TOP_K, tt), lambda i: (0, i))
    return pl.pallas_call(
        _route_body,
        grid=(m // tt,),
        in_specs=[pl.BlockSpec((N_EXPERTS, tt), lambda i: (0, i)),
                  pl.BlockSpec((N_EXPERTS, 128), lambda i: (0, 0))],
        out_specs=[out, out, out, pl.BlockSpec((N_EXPERTS, 128), lambda i: (0, 0))],
        out_shape=[jax.ShapeDtypeStruct((TOP_K, m), I32), jax.ShapeDtypeStruct((TOP_K, m), F32),
                   jax.ShapeDtypeStruct((TOP_K, m), I32), jax.ShapeDtypeStruct((N_EXPERTS, 128), I32)],
        scratch_shapes=[pltpu.VMEM((tt, tt), BF16), pltpu.VMEM((N_EXPERTS, 128), F32)],
        compiler_params=_cparams(("arbitrary",), 32),
        name="route",
    )(s_t, b_col)


def _posk_body(pst_ref, e_ref, r_ref, p_ref):
    e = e_ref[...]
    acc = r_ref[...]
    for x in range(N_EXPERTS):
        acc = acc + jnp.where(e == x, pst_ref[x], 0)
    p_ref[...] = acc


def _posk_call(pstart, eidx, rank):
    k, m = eidx.shape
    tt = min(m, 2048)
    blk = pl.BlockSpec((k, tt), lambda i, pst: (0, i))
    return pl.pallas_call(
        _posk_body,
        grid_spec=pltpu.PrefetchScalarGridSpec(num_scalar_prefetch=1, grid=(m // tt,),
                                               in_specs=[blk, blk], out_specs=blk),
        out_shape=jax.ShapeDtypeStruct((k, m), I32),
        compiler_params=_cparams(("arbitrary",), 32),
        name="posk",
    )(pstart, eidx, rank)


def _dispatch_body(cnt_ref, pst_ref, pcn_ref, h2_ref, pos_ref, xs_ref, zrow, sem):
    i = pl.program_id(0)
    td = h2_ref.shape[0]

    def row_copy(t, dst):
        return pltpu.make_async_copy(h2_ref.at[pl.ds(t, 1), :], xs_ref.at[pl.ds(dst, 1), :], sem)

    def issue(t, carry):
        for k in range(TOP_K):
            row_copy(t, pos_ref[k, t]).start(priority=k % 2)
        return carry

    lax.fori_loop(0, td, issue, 0)

    def drain(t, carry):
        for k in range(TOP_K):
            row_copy(0, 0).wait()
        return carry

    lax.fori_loop(0, td, drain, 0)

    @pl.when(i == pl.num_programs(0) - 1)
    def _():
        zrow[...] = jnp.zeros_like(zrow)

        def zero_copy(dst):
            return pltpu.make_async_copy(zrow.at[pl.ds(0, 1), :], xs_ref.at[pl.ds(dst, 1), :], sem)

        def per_expert(e, carry):
            base = pst_ref[e]
            lax.fori_loop(cnt_ref[e], pcn_ref[e], lambda rr, cc: (zero_copy(base + rr).start(), cc)[1], 0)
            lax.fori_loop(cnt_ref[e], pcn_ref[e], lambda rr, cc: (zero_copy(0).wait(), cc)[1], 0)
            return carry

        lax.fori_loop(0, N_EXPERTS, per_expert, 0)


def _dispatch_call(cnt, pstart, pcnt, h2, pos, rows, td):
    m, d = h2.shape
    return pl.pallas_call(
        _dispatch_body,
        grid_spec=pltpu.PrefetchScalarGridSpec(
            num_scalar_prefetch=3, grid=(m // td,),
            in_specs=[pl.BlockSpec((td, d), lambda i, *_: (i, 0)),
                      pl.BlockSpec((TOP_K, td), lambda i, *_: (0, i), memory_space=pltpu.SMEM)],
            out_specs=pl.BlockSpec(memory_space=pl.ANY),
            scratch_shapes=[pltpu.VMEM((8, d), F32), pltpu.SemaphoreType.DMA(())]),
        out_shape=jax.ShapeDtypeStruct((rows, d), F32),
        compiler_params=_cparams(("arbitrary",), 32),
        name="dispatch",
    )(cnt, pstart, pcnt, h2, pos)


def _moe_body(te_ref, nu_ref, x_ref, w1_ref, w3_ref, w2_ref, y_ref, w1b, w3b, w2b):
    i = pl.program_id(0)
    used = i < nu_ref[0]
    prev = te_ref[jnp.maximum(i - 1, 0)]

    @pl.when(jnp.logical_and(used, jnp.logical_or(i == 0, te_ref[i] != prev)))
    def _():
        w1b[...] = w1_ref[...].astype(BF16)
        w3b[...] = w3_ref[...].astype(BF16)
        w2b[...] = w2_ref[...].astype(BF16)

    @pl.when(used)
    def _():
        x = x_ref[...].astype(BF16)
        a = _silu(_dg(x, w1b[...])) * _dg(x, w3b[...])
        y_ref[...] = _dg(a.astype(BF16), w2b[...])


def _moe_call(tile_e, n_used, xs, w1, w3, w2):
    rows, d = xs.shape
    nt = rows // MOE_ROWS
    de = w1.shape[-1]
    rmap = lambda i, te, nu: (jnp.minimum(i, nu[0] - 1), 0)
    wmap = lambda i, te, nu: (te[i], 0, 0)
    return pl.pallas_call(
        _moe_body,
        grid_spec=pltpu.PrefetchScalarGridSpec(
            num_scalar_prefetch=2, grid=(nt,),
            in_specs=[pl.BlockSpec((MOE_ROWS, d), rmap),
                      pl.BlockSpec((None, d, de), wmap), pl.BlockSpec((None, d, de), wmap),
                      pl.BlockSpec((None, de, d), wmap)],
            out_specs=pl.BlockSpec((MOE_ROWS, d), rmap),
            scratch_shapes=[pltpu.VMEM((d, de), BF16), pltpu.VMEM((d, de), BF16), pltpu.VMEM((de, d), BF16)]),
        out_shape=jax.ShapeDtypeStruct((rows, d), F32),
        compiler_params=_cparams(("arbitrary",), 56),
        name="moe",
    )(tile_e, n_used, xs, w1, w3, w2)


def _final_body(x1_ref, h2_ref, pos_ref, wt_ref, ys_ref, g2_ref, gn_ref, w1_ref, w3_ref, w2_ref,
                o_ref, ybuf, sem):
    tf = x1_ref.shape[0]

    def row_copy(k, t, src):
        return pltpu.make_async_copy(ys_ref.at[pl.ds(src, 1), :], ybuf.at[k, pl.ds(t, 1), :], sem)

    def issue(t, carry):
        for k in range(TOP_K):
            row_copy(k, t, pos_ref[k, t]).start(priority=k % 2)
        return carry

    lax.fori_loop(0, tf, issue, 0)

    hb = h2_ref[...].astype(BF16)
    shared = _dg((_silu(_dg(hb, w1_ref[...])) * _dg(hb, w3_ref[...])).astype(BF16), w2_ref[...])

    def drain(t, carry):
        for k in range(TOP_K):
            row_copy(0, 0, 0).wait()
        return carry

    lax.fori_loop(0, tf, drain, 0)

    wt = jnp.transpose(jnp.concatenate([wt_ref[...], jnp.zeros((tf - TOP_K, tf), F32)], axis=0))
    acc = shared
    for k in range(TOP_K):
        acc = acc + wt[:, k:k + 1] * ybuf[k]
    xo = x1_ref[...] + g2_ref[...] * acc
    ms = jnp.mean(xo * xo, axis=-1, keepdims=True)
    o_ref[...] = xo * lax.rsqrt(ms + EPS) * gn_ref[...]


def _final_call(x1, h2, pos, wts, ys, g2, gn, w1s, w3s, w2s, tf):
    m, d = x1.shape
    ds = w1s.shape[1]
    full = pl.BlockSpec((tf, d), lambda i: (i, 0))
    vec = pl.BlockSpec((1, d), lambda i: (0, 0))
    return pl.pallas_call(
        _final_body,
        grid=(m // tf,),
        in_specs=[full, full,
                  pl.BlockSpec((TOP_K, tf), lambda i: (0, i), memory_space=pltpu.SMEM),
                  pl.BlockSpec((TOP_K, tf), lambda i: (0, i)),
                  pl.BlockSpec(memory_space=pl.ANY), vec, vec,
                  pl.BlockSpec((d, ds), lambda i: (0, 0)), pl.BlockSpec((d, ds), lambda i: (0, 0)),
                  pl.BlockSpec((ds, d), lambda i: (0, 0))],
        out_specs=full,
        out_shape=jax.ShapeDtypeStruct((m, d), F32),
        scratch_shapes=[pltpu.VMEM((TOP_K, tf, d), F32), pltpu.SemaphoreType.DMA(())],
        compiler_params=_cparams(("arbitrary",), 48),
        name="final",
    )(x1, h2, pos, wts, ys, g2, gn, w1s, w3s, w2s)


def _pos_tables(n_tokens):
    rows = n_tokens // GRID_W
    quarter = D_MODEL // 4
    omega = 1.0 / (10000.0 ** (jnp.arange(quarter, dtype=F32) / quarter))

    def emb1d(pos):
        ang = pos[:, None] * omega[None]
        return jnp.concatenate([jnp.sin(ang), jnp.cos(ang)], axis=-1)

    return emb1d(jnp.arange(rows, dtype=F32)), emb1d(jnp.arange(GRID_W, dtype=F32))


def _filter_feats(L):
    n = jnp.arange(2 * L, dtype=I32)
    t = jnp.where(n <= L, n, 2 * L - n).astype(F32)
    t01 = t / max(L - 1, 1)
    w = 2.0 * math.pi * t / L
    bands = jnp.linspace(1e-4, FILT_BANDS - 1, FILT_BANDS, dtype=F32)
    valid = (n != L).astype(F32)
    feats = jnp.concatenate([t01[:, None], jnp.cos(w[:, None] * bands), -jnp.sin(w[:, None] * bands),
                             valid[:, None], jnp.zeros((2 * L, 64 - 34), F32)], axis=-1)
    return feats


def _pad_rows(a, rows):
    return jnp.concatenate([a, jnp.zeros((rows - a.shape[0],) + a.shape[1:], a.dtype)], axis=0)


def _layer(x, c, ctx, c_ctx, w_ada, b_ada, g_mix, g_ffn, w_in, b_gates, conv_k_w, conv_k_b,
           conv_q_w, conv_q_b, g_head, conv_hy_w, conv_hy_b, filt_w1, filt_b1, filt_freq1,
           filt_w2, filt_b2, filt_freq2, filt_w3, filt_b3, hy_dskip, w_out, w_router, b_router,
           w1_e, w3_e, w2_e, w1_s, w3_s, w2_s, g_final):
    L, d = x.shape
    lc = ctx.shape[0]
    row = lambda v: v.reshape(1, -1)

    cc = _pad_rows(jnp.stack([c, c_ctx], axis=0), 8)
    mods = _mod_call(cc, w_ada, row(b_ada))
    sh1, sc1, g1, sh2, sc2, g2 = [mods[0:1, k * d:(k + 1) * d] for k in range(6)]
    csh1, csc1 = mods[1:2, 0:d], mods[1:2, d:2 * d]

    w_r = jnp.concatenate([w_in[:, OFF_K:OFF_V], w_in[:, OFF_Q:OFF_O], w_in[:, OFF_V:OFF_G],
                           w_in[:, OFF_O:]], axis=1).astype(BF16)
    w_g = jnp.concatenate([w_in[:, OFF_G:OFF_Q], jnp.zeros((d, GATE_PAD - 4 * N_HEADS), F32)], axis=1)
    bg = jnp.concatenate([b_gates, jnp.zeros((GATE_PAD - 4 * N_HEADS,), F32)]).reshape(1, GATE_PAD)
    e_row, e_col = _pos_tables(L)
    conv_w = jnp.concatenate([conv_k_w, conv_q_w], axis=1)
    conv_b = jnp.concatenate([conv_k_b, conv_q_b]).reshape(1, -1)
    conv_s = jnp.concatenate([jnp.ones((QK_W,), F32), jnp.full((QK_W,), QK_HEAD ** -0.5, F32)]).reshape(1, -1)

    z_c, gt_c = _inproj_call(ctx, jnp.zeros((8, d // 2), F32), e_col, row(g_mix), csh1, csc1, w_r, w_g,
                             use_pos=False, tm=min(lc, 256))
    kq_c = _conv_kq_call(z_c, conv_w, conv_b, conv_s, tm=min(lc, 256))
    s0 = jnp.zeros((2 * N_HEADS, CHUNK, 256), F32)
    m0 = jnp.zeros((2, 8, 128), F32)
    _, _, s_ctx, m_ctx = _mlstm_call(kq_c, z_c, gt_c, bg, s0, m0)

    z, gates = _inproj_call(x, e_row, e_col, row(g_mix), sh1, sc1, w_r, w_g, use_pos=True, tm=min(L, 1024))
    kq = _conv_kq_call(z, conv_w, conv_b, conv_s, tm=min(L, 512))
    x0c, xv = _conv_hy_call(z, conv_hy_w, row(conv_hy_b), tm=min(L, 512))
    hf, hb, _, _ = _mlstm_call(kq, z, gates, bg, s_ctx, m_ctx)

    n1 = 2 * L // FFT_N2
    cst = _fft_consts(n1)
    rates = jnp.linspace(-math.log(DECAY_TARGET) / SLOW_DECAY_PCT, -math.log(DECAY_TARGET) / FAST_DECAY_PCT,
                         HY_W, dtype=F32).reshape(1, -1)
    w1p = _pad_rows(filt_w1, 64)
    kf, l1 = _filt_call(_filter_feats(L), w1p, row(filt_b1), row(filt_freq1), filt_w2, row(filt_b2),
                        row(filt_freq2), filt_w3, row(filt_b3), rates, tn=min(L, 1024))
    cols = FFT_N2 * HY_W
    cb = 2048
    kar, kai = _fft1_call(cst["f1"], kf.reshape(n1, cols), cb)
    khat = _fft2_filt_call(kar.reshape(FFT_KP, FFT_N2, HY_W), kai.reshape(FFT_KP, FFT_N2, HY_W), cst)
    uar, uai = _fft1_call(cst["f1"], xv.reshape(n1 // 2, cols), cb)
    br, bi = _fft2_conv_call(uar.reshape(FFT_KP, FFT_N2, HY_W), uai.reshape(FFT_KP, FFT_N2, HY_W), khat, cst)
    reps = cb // HY_W
    il_t = jnp.tile(1.0 / l1, (1, reps))
    ds_t = jnp.tile(row(hy_dskip), (1, reps))
    yh = _ifft1_call(cst, br.reshape(FFT_KP, cols), bi.reshape(FFT_KP, cols),
                     x0c.reshape(n1 // 2, cols), xv.reshape(n1 // 2, cols), il_t, ds_t, cb).reshape(L, HY_W)

    wo = w_out.astype(BF16)
    x1, h2, s_t = _outproj_call(hf, hb, z, yh, x, e_row, e_col, row(g_head), wo[:MV_W], wo[MV_W:],
                                g1, row(g_ffn), sh2, sc2, jnp.transpose(w_router), tm=min(L, 256))

    b_col = jnp.broadcast_to(b_router.reshape(N_EXPERTS, 1), (N_EXPERTS, 128))
    eidx, wts, rank, cnt2 = _route_call(s_t, b_col, tt=min(L, 1024))
    cnt = cnt2[:, 0]
    pcnt = (cnt + MOE_ROWS - 1) // MOE_ROWS * MOE_ROWS
    pend = jnp.cumsum(pcnt)
    pstart = pend - pcnt
    rows = L * TOP_K + N_EXPERTS * MOE_ROWS
    nt = rows // MOE_ROWS
    tile_row = jnp.arange(nt, dtype=I32) * MOE_ROWS
    tile_e = jnp.minimum(jnp.sum((pend[None, :] <= tile_row[:, None]).astype(I32), axis=1), N_EXPERTS - 1)
    n_used = (pend[-1] // MOE_ROWS).astype(I32).reshape(1)
    pos = _posk_call(pstart.astype(I32), eidx, rank)

    xs = _dispatch_call(cnt, pstart.astype(I32), pcnt.astype(I32), h2, pos, rows, td=min(L, 256))
    ys = _moe_call(tile_e, n_used, xs, w1_e, w3_e, w2_e)
    return _final_call(x1, h2, pos, wts, ys, g2, row(g_final), w1_s.astype(BF16), w3_s.astype(BF16),
                       w2_s.astype(BF16), tf=min(L, 128))


def kernel(x, c, ctx, c_ctx, w_ada, b_ada, g_mix, g_ffn, w_in, b_gates, conv_k_w, conv_k_b, conv_q_w,
           conv_q_b, g_head, conv_hy_w, conv_hy_b, filt_w1, filt_b1, filt_freq1, filt_w2, filt_b2,
           filt_freq2, filt_w3, filt_b3, hy_dskip, w_out, w_router, b_router, w1_e, w3_e, w2_e,
           w1_s, w3_s, w2_s, g_final):
    assert x.shape[0] == 1 and w_ada.shape[0] == 1, "one batch element, one layer"
    out = _layer(x[0], c[0], ctx[0], c_ctx, w_ada[0], b_ada[0], g_mix[0], g_ffn[0], w_in[0], b_gates[0],
                 conv_k_w[0], conv_k_b[0], conv_q_w[0], conv_q_b[0], g_head[0], conv_hy_w[0], conv_hy_b[0],
                 filt_w1[0], filt_b1[0], filt_freq1[0], filt_w2[0], filt_b2[0], filt_freq2[0], filt_w3[0],
                 filt_b3[0], hy_dskip[0], w_out[0], w_router[0], b_router[0], w1_e[0], w3_e[0], w2_e[0],
                 w1_s[0], w3_s[0], w2_s[0], g_final)
    return out[None]
```

```python
import functools
import math

import numpy as np
import jax
import jax.numpy as jnp
from jax import lax
from jax.experimental import pallas as pl
from jax.experimental.pallas import tpu as pltpu

F32 = jnp.float32
BF16 = jnp.bfloat16
I32 = jnp.int32
U32 = jnp.uint32

D_MODEL = 2048
GRID_W = 64
N_HEADS = 8
QK_HEAD = 64
V_HEAD = 128
QK_W = N_HEADS * QK_HEAD
MV_W = N_HEADS * V_HEAD
HY_W = D_MODEL - MV_W
CHUNK = 128
FILT_BANDS = 16
FILT_HIDDEN = 64
DECAY_TARGET = 1e-2
FAST_DECAY_PCT = 0.3
SLOW_DECAY_PCT = 1.5
N_EXPERTS = 64
N_GROUPS = 8
E_PER_GROUP = 8
TOPK_GROUPS = 4
TOP_K = 8
D_EXPERT = 512
ROUTE_SCALE = 2.5
EPS = 1e-6
OFF_K = 0
OFF_V = OFF_K + QK_W
OFF_G = OFF_V + MV_W
OFF_Q = OFF_G + 4 * N_HEADS
OFF_O = OFF_Q + QK_W
OFF_HY = OFF_O + MV_W

ZC_KQ, ZC_V, ZC_O, ZC_X0, ZC_X1, ZC_HV = 0, 1, 2, 3, 4, 5
Z_COLS = 6 * 1024
GATE_PAD = 128

NEG = -1e30
MIB = 1024 * 1024

FFT_N2 = 128
FFT_KP = 144

MOE_ROWS = 256


def _cparams(sem, vmem_mb):
    return pltpu.CompilerParams(dimension_semantics=sem, vmem_limit_bytes=vmem_mb * MIB)


def _split2(x):
    hi = x.astype(BF16)
    lo = (x - hi.astype(F32)).astype(BF16)
    return hi, lo


_NN = (((1,), (0,)), ((), ()))
_NT = (((1,), (1,)), ((), ()))
_TN = (((0,), (0,)), ((), ()))


def _dg(a, b, dims=_NN):
    return lax.dot_general(a, b, dims, preferred_element_type=F32)


def _dot3(a, b, dims=_NN):
    ah, al = _split2(a)
    bh, bl = _split2(b)
    return _dg(ah, bh, dims) + _dg(al, bh, dims) + _dg(ah, bl, dims)


def _sigmoid(x):
    return 1.0 / (1.0 + jnp.exp(-x))


def _silu(x):
    return x * _sigmoid(x)


def _pack_pair(a, b):
    hi = lax.bitcast_convert_type(a.astype(BF16).astype(F32), U32)
    lo = lax.bitcast_convert_type(b.astype(BF16).astype(F32), U32)
    return hi | (lo >> 16)


def _unpack_pair(w):
    a = lax.bitcast_convert_type(w & jnp.uint32(0xFFFF0000), F32)
    b = lax.bitcast_convert_type(w << 16, F32)
    return a, b


SLAB = 8
SLAB_W = 128


def _store_slabs(ref, w, base=0):
    r = w.shape[0]
    for j in range(SLAB):
        ref[pl.ds(base + j, r, stride=SLAB), :] = w[:, j * SLAB_W:(j + 1) * SLAB_W]


def _load_slabs(ref, r, base=0):
    return jnp.concatenate([ref[pl.ds(base + j, r, stride=SLAB), :] for j in range(SLAB)], axis=1)


def _norm_mod(x, g, sh, sc):
    ms = jnp.mean(x * x, axis=-1, keepdims=True)
    return (x * lax.rsqrt(ms + EPS) * g) * (1.0 + sc) + sh


def _add_pos(x, erow, ecol):
    tm, d = x.shape
    half = d // 2
    parts = []
    for r in range(tm // GRID_W):
        xs = x[r * GRID_W:(r + 1) * GRID_W, :]
        parts.append(jnp.concatenate([xs[:, :half] + erow[r:r + 1, :], xs[:, half:] + ecol], axis=-1))
    return parts[0] if len(parts) == 1 else jnp.concatenate(parts, axis=0)


def _mod_body(cc_ref, w_ref, b_ref, o_ref):
    o_ref[...] = _dot3(_silu(cc_ref[...]), w_ref[...]) + b_ref[...]


def _mod_call(cc, w, b):
    d, n = w.shape
    tn = 1024
    return pl.pallas_call(
        _mod_body,
        grid=(n // tn,),
        in_specs=[pl.BlockSpec((8, d), lambda j: (0, 0)),
                  pl.BlockSpec((d, tn), lambda j: (0, j)),
                  pl.BlockSpec((1, tn), lambda j: (0, j))],
        out_specs=pl.BlockSpec((8, tn), lambda j: (0, j)),
        out_shape=jax.ShapeDtypeStruct((8, n), F32),
        compiler_params=_cparams(("arbitrary",), 40),
        name="mod",
    )(cc, w, b)


def _inproj_body(use_pos, x_ref, erow_ref, ecol_ref, gm_ref, sh_ref, sc_ref, w_ref, wg_ref,
                 z_ref, g_ref, h_scr):
    @pl.when(pl.program_id(1) == 0)
    def _():
        x = x_ref[...]
        if use_pos:
            x = _add_pos(x, erow_ref[...], ecol_ref[...])
        h = _norm_mod(x, gm_ref[...], sh_ref[...], sc_ref[...])
        h_scr[...] = h.astype(BF16)
        g_ref[...] = _dot3(h, wg_ref[...])

    z_ref[...] = jnp.dot(h_scr[...], w_ref[...], preferred_element_type=F32).astype(BF16)


def _inproj_call(x, erow, ecol, gm, sh, sc, w, wg, use_pos, tm):
    m, d = x.shape
    tn = 512
    er = tm // GRID_W if use_pos else erow.shape[0]
    row_map = (lambda i, j: (i, 0)) if use_pos else (lambda i, j: (0, 0))
    return pl.pallas_call(
        functools.partial(_inproj_body, use_pos),
        grid=(m // tm, Z_COLS // tn),
        in_specs=[pl.BlockSpec((tm, d), lambda i, j: (i, 0)),
                  pl.BlockSpec((er, d // 2), row_map),
                  pl.BlockSpec((GRID_W, d // 2), lambda i, j: (0, 0)),
                  pl.BlockSpec((1, d), lambda i, j: (0, 0)),
                  pl.BlockSpec((1, d), lambda i, j: (0, 0)),
                  pl.BlockSpec((1, d), lambda i, j: (0, 0)),
                  pl.BlockSpec((d, tn), lambda i, j: (0, j)),
                  pl.BlockSpec((d, GATE_PAD), lambda i, j: (0, 0))],
        out_specs=[pl.BlockSpec((tm, tn), lambda i, j: (i, j)),
                   pl.BlockSpec((tm, GATE_PAD), lambda i, j: (i, 0))],
        out_shape=[jax.ShapeDtypeStruct((m, Z_COLS), BF16),
                   jax.ShapeDtypeStruct((m, GATE_PAD), F32)],
        scratch_shapes=[pltpu.VMEM((tm, d), BF16)],
        compiler_params=_cparams(("arbitrary", "arbitrary"), 48),
        name="inproj",
    )(x, erow, ecol, gm, sh, sc, w, wg)


def _conv3(zc, zp, zn, w, b, first, last):
    tm = zc.shape[0]
    row = lax.broadcasted_iota(I32, zc.shape, 0)
    prev_row = jnp.where(first, 0.0, zp[7:8, :])
    next_row = jnp.where(last, 0.0, zn[0:1, :])
    xm = jnp.where(row == 0, prev_row, pltpu.roll(zc, 1, 0))
    xp = jnp.where(row == tm - 1, next_row, pltpu.roll(zc, tm - 1, 0))
    return xm * w[0:1, :] + zc * w[1:2, :] + xp * w[2:3, :] + b


def _conv_kq_body(zc_ref, zp_ref, zn_ref, w_ref, b_ref, s_ref, o_ref):
    i = pl.program_id(0)
    u = _conv3(zc_ref[...].astype(F32), zp_ref[...].astype(F32), zn_ref[...].astype(F32),
               w_ref[...], b_ref[...], i == 0, i == pl.num_programs(0) - 1)
    o_ref[...] = (_silu(u) * s_ref[...]).astype(BF16)


def _halo_specs(tm, m, cb):
    nb8 = m // 8
    return [pl.BlockSpec((tm, 1024), lambda i: (i, cb)),
            pl.BlockSpec((8, 1024), lambda i: (jnp.maximum(i * (tm // 8) - 1, 0), cb)),
            pl.BlockSpec((8, 1024), lambda i: (jnp.minimum((i + 1) * (tm // 8), nb8 - 1), cb))]


def _conv_kq_call(z, w, b, s, tm):
    m = z.shape[0]
    vec = pl.BlockSpec((1, 1024), lambda i: (0, 0))
    return pl.pallas_call(
        _conv_kq_body,
        grid=(m // tm,),
        in_specs=_halo_specs(tm, m, ZC_KQ) + [pl.BlockSpec((3, 1024), lambda i: (0, 0)), vec, vec],
        out_specs=pl.BlockSpec((tm, 1024), lambda i: (i, 0)),
        out_shape=jax.ShapeDtypeStruct((m, 1024), BF16),
        compiler_params=_cparams(("arbitrary",), 32),
        name="conv_kq",
    )(z, z, z, w, b, s)


def _conv_hy_body(ac_ref, ap_ref, an_ref, bc_ref, bp_ref, bn_ref, cc_ref, cp_ref, cn_ref,
                  w_ref, b_ref, x0_ref, xv_ref):
    i = pl.program_id(0)
    first, last = i == 0, i == pl.num_programs(0) - 1
    w = w_ref[...]
    b = b_ref[...]

    def cv(c, p, n, k):
        return _conv3(c[...].astype(F32), p[...].astype(F32), n[...].astype(F32),
                      w[:, k * 1024:(k + 1) * 1024], b[:, k * 1024:(k + 1) * 1024], first, last)

    x0_ref[...] = cv(ac_ref, ap_ref, an_ref, 0).astype(BF16)
    xv_ref[...] = (cv(bc_ref, bp_ref, bn_ref, 1) * cv(cc_ref, cp_ref, cn_ref, 2)).astype(BF16)


def _conv_hy_call(z, w, b, tm):
    m = z.shape[0]
    out = pl.BlockSpec((tm, 1024), lambda i: (i, 0))
    return pl.pallas_call(
        _conv_hy_body,
        grid=(m // tm,),
        in_specs=(_halo_specs(tm, m, ZC_X0) + _halo_specs(tm, m, ZC_X1) + _halo_specs(tm, m, ZC_HV)
                  + [pl.BlockSpec((3, 3072), lambda i: (0, 0)), pl.BlockSpec((1, 3072), lambda i: (0, 0))]),
        out_specs=[out, out],
        out_shape=[jax.ShapeDtypeStruct((m, 1024), BF16), jax.ShapeDtypeStruct((m, 1024), BF16)],
        compiler_params=_cparams(("arbitrary",), 32),
        name="conv_hy",
    )(z, z, z, z, z, z, z, z, z, w, b)


def _mlstm_body(kqf_ref, vf_ref, gf_ref, kqb_ref, vb_ref, gb_ref, bg_ref, s0_ref, m0_ref,
                hf_ref, hb_ref, sfin_ref, mfin_ref, s_scr, m_scr):
    j = pl.program_id(0)

    @pl.when(j == 0)
    def _():
        s_scr[...] = s0_ref[...]
        m_scr[...] = m0_ref[...]

    r = lax.broadcasted_iota(I32, (CHUNK, CHUNK), 0)
    c = lax.broadcasted_iota(I32, (CHUNK, CHUNK), 1)
    e1 = jnp.where(c == 0, 1.0, 0.0).astype(BF16)
    bg = bg_ref[...]

    for d in range(2):
        kq = (kqf_ref, kqb_ref)[d][...]
        v = (vf_ref, vb_ref)[d][...]
        g_all = (gf_ref, gb_ref)[d][...] + bg
        out_ref = (hf_ref, hb_ref)[d]
        tri = (r >= c) if d == 0 else (c >= r)
        tri_b = jnp.where(tri, 1.0, 0.0).astype(BF16)
        gi = g_all if d == 0 else pltpu.roll(g_all, CHUNK - 16, 1)
        gfp = pltpu.roll(g_all, CHUNK - 8 - 16 * d, 1)
        lf = jnp.minimum(gfp, 0.0) - jnp.log(1.0 + jnp.exp(-jnp.abs(gfp)))
        l1 = lf.astype(BF16)
        r1 = lf - l1.astype(F32)
        l2 = r1.astype(BF16)
        l3 = (r1 - l2.astype(F32)).astype(BF16)
        bcum = _dg(tri_b, l1) + _dg(tri_b, l2) + _dg(tri_b, l3)
        gtot = bcum[CHUNK - 1:CHUNK, :] if d == 0 else bcum[0:1, :]
        acol = gtot - bcum + gi
        m_loc = jnp.max(acol, axis=0, keepdims=True)
        wg = jnp.exp(acol - m_loc)
        m_st = m_scr[d, 0:1, :]
        inter = bcum + m_st
        m_new = jnp.maximum(gtot + m_st, m_loc)
        sp = jnp.broadcast_to(jnp.exp(gtot + m_st - m_new), (CHUNK, CHUNK))
        sl = jnp.broadcast_to(jnp.exp(m_loc - m_new), (CHUNK, CHUNK))
        rt = jnp.transpose(gi - bcum)

        for h in range(N_HEADS):
            p, half = divmod(h, 2)
            lm = (c // QK_HEAD) == half
            kp = kq[:, p * 128:(p + 1) * 128]
            qp = kq[:, QK_W + p * 128:QK_W + (p + 1) * 128]
            vaug = jnp.concatenate([v[:, h * 128:(h + 1) * 128], e1], axis=1)
            qm = jnp.where(lm, qp, jnp.zeros_like(qp))
            dl = jnp.where(tri, bcum[:, h:h + 1] + rt[h:h + 1, :], NEG)
            icol = inter[:, h:h + 1]
            mt = jnp.maximum(icol, jnp.max(dl, axis=1, keepdims=True))
            pm = jnp.exp(dl - mt)
            s = (_dg(qm, kp, _NT) * pm).astype(BF16)
            st = s_scr[d * N_HEADS + h]
            tot = _dg(s, vaug) + jnp.exp(icol - mt) * _dg(qm, st.astype(BF16))
            den = jnp.maximum(jnp.abs(tot[:, 128:129]), jnp.exp(-mt))
            out_ref[:, h * 128:(h + 1) * 128] = (tot[:, :128] / den).astype(BF16)
            kw = jnp.where(lm, kp.astype(F32) * wg[:, h:h + 1], 0.0).astype(BF16)
            s_scr[d * N_HEADS + h] = sp[:, h:h + 1] * st + sl[:, h:h + 1] * _dg(kw, vaug, _TN)
        m_scr[d, 0:1, :] = m_new

    @pl.when(j == pl.num_programs(0) - 1)
    def _():
        sfin_ref[...] = s_scr[...]
        mfin_ref[...] = m_scr[...]


def _mlstm_call(kq, z, gates, bg, s0, m0):
    m = kq.shape[0]
    nc = m // CHUNK
    fwd = lambda cb: (lambda j: (j, cb))
    bwd = lambda cb: (lambda j: (nc - 1 - j, cb))
    st_spec = pl.BlockSpec((2 * N_HEADS, CHUNK, 256), lambda j: (0, 0, 0))
    m_spec = pl.BlockSpec((2, 8, 128), lambda j: (0, 0, 0))
    return pl.pallas_call(
        _mlstm_body,
        grid=(nc,),
        in_specs=[pl.BlockSpec((CHUNK, 1024), fwd(0)), pl.BlockSpec((CHUNK, 1024), fwd(ZC_V)),
                  pl.BlockSpec((CHUNK, GATE_PAD), fwd(0)),
                  pl.BlockSpec((CHUNK, 1024), bwd(0)), pl.BlockSpec((CHUNK, 1024), bwd(ZC_V)),
                  pl.BlockSpec((CHUNK, GATE_PAD), bwd(0)),
                  pl.BlockSpec((1, GATE_PAD), lambda j: (0, 0)), st_spec, m_spec],
        out_specs=[pl.BlockSpec((CHUNK, 1024), fwd(0)), pl.BlockSpec((CHUNK, 1024), bwd(0)), st_spec, m_spec],
        out_shape=[jax.ShapeDtypeStruct((m, 1024), BF16), jax.ShapeDtypeStruct((m, 1024), BF16),
                   jax.ShapeDtypeStruct((2 * N_HEADS, CHUNK, 256), F32),
                   jax.ShapeDtypeStruct((2, 8, 128), F32)],
        scratch_shapes=[pltpu.VMEM((2 * N_HEADS, CHUNK, 256), F32), pltpu.VMEM((2, 8, 128), F32)],
        compiler_params=_cparams(("arbitrary",), 32),
        name="mlstm",
    )(kq, z, gates, kq, z, gates, bg, s0, m0)


def _filt_body(ft_ref, w1_ref, b1_ref, f1_ref, w2_ref, b2_ref, f2_ref, w3_ref, b3_ref, rt_ref,
               kf_ref, l1_ref):
    ft = ft_ref[...]
    h1 = jnp.sin(f1_ref[...] * (_dot3(ft, w1_ref[...]) + b1_ref[...]))
    h2 = jnp.sin(f2_ref[...] * (_dot3(h1, w2_ref[...]) + b2_ref[...]))
    h = _dot3(h2, w3_ref[...]) + b3_ref[...]
    h = h * jnp.exp(-ft[:, 0:1] * rt_ref[...]) * ft[:, 33:34]
    kf_ref[...] = h.astype(BF16)

    @pl.when(pl.program_id(0) == 0)
    def _():
        l1_ref[...] = jnp.zeros_like(l1_ref)

    l1_ref[...] += jnp.sum(jnp.abs(h), axis=0, keepdims=True)


def _filt_call(feats, w1, b1, f1, w2, b2, f2, w3, b3, rates, tn):
    n = feats.shape[0]
    nt = n // tn
    half = lambda i: (0, jnp.where(i >= nt // 2, 1, 0))
    c64 = lambda shape: pl.BlockSpec(shape, lambda i: (0, 0))
    return pl.pallas_call(
        _filt_body,
        grid=(nt,),
        in_specs=[pl.BlockSpec((tn, 64), lambda i: (i, 0)),
                  c64((64, 64)), c64((1, 64)), c64((1, 64)), c64((64, 64)), c64((1, 64)), c64((1, 64)),
                  pl.BlockSpec((64, HY_W), half), pl.BlockSpec((1, HY_W), half), c64((1, HY_W))],
        out_specs=[pl.BlockSpec((tn, HY_W), lambda i: (i, 0)), pl.BlockSpec((1, HY_W), lambda i: (0, 0))],
        out_shape=[jax.ShapeDtypeStruct((n, HY_W), BF16), jax.ShapeDtypeStruct((1, HY_W), F32)],
        compiler_params=_cparams(("arbitrary",), 32),
        name="filt",
    )(feats, w1, b1, f1, w2, b2, f2, w3, b3, rates)


def _fft_consts(n1_rows):
    n = n1_rows * FFT_N2
    kv = n1_rows // 2 + 1
    k1 = np.arange(FFT_KP, dtype=np.float64)
    valid = (k1 < kv).astype(np.float64)
    n1 = np.arange(n1_rows, dtype=np.float64)
    th1 = 2.0 * np.pi * np.outer(k1, n1) / n1_rows
    f1 = np.concatenate([np.cos(th1) * valid[:, None], -np.sin(th1) * valid[:, None]], axis=0)
    n2 = np.arange(FFT_N2, dtype=np.float64)
    tht = 2.0 * np.pi * np.outer(k1, n2) / n
    rep = lambda a: jnp.broadcast_to(jnp.asarray(a, F32)[:, :, None], (FFT_KP, FFT_N2, 128))
    twr = rep(np.cos(tht) * valid[:, None])
    twi = rep(-np.sin(tht) * valid[:, None])
    th2 = 2.0 * np.pi * np.outer(n2, n2) / FFT_N2
    cs, sn = np.cos(th2), np.sin(th2)
    f2p = np.block([[cs, sn], [-sn, cs]])
    f2pc = np.block([[cs, -sn], [sn, cs]])
    wk = np.where((k1 == 0) | (k1 == kv - 1), 1.0, 2.0) * valid / n
    half = n1_rows // 2
    thi = 2.0 * np.pi * np.outer(n1[:half], k1) / n1_rows
    gc = np.cos(thi) * wk[None, :]
    gs = np.sin(thi) * wk[None, :]
    as_bf = lambda a: jnp.asarray(a, F32).astype(BF16)
    return dict(f1=as_bf(f1), twr=twr, twi=twi,
                f2p=as_bf(f2p), f2pc=as_bf(f2pc), gc=as_bf(gc), gs=as_bf(gs))


def _fft1_body(f_ref, x_ref, ar_ref, ai_ref):
    o = _dg(f_ref[...], x_ref[...])
    ar_ref[...] = o[:FFT_KP].astype(BF16)
    ai_ref[...] = o[FFT_KP:].astype(BF16)


def _fft1_call(f1, x2d, cb):
    k, cols = x2d.shape
    f1 = f1[:, :k]
    out = pl.BlockSpec((FFT_KP, cb), lambda i: (0, i))
    sh = jax.ShapeDtypeStruct((FFT_KP, cols), BF16)
    return pl.pallas_call(
        _fft1_body,
        grid=(cols // cb,),
        in_specs=[pl.BlockSpec((2 * FFT_KP, k), lambda i: (0, 0)), pl.BlockSpec((k, cb), lambda i: (0, i))],
        out_specs=[out, out],
        out_shape=[sh, sh],
        compiler_params=_cparams(("arbitrary",), 32),
        name="fft1",
    )(f1, x2d)


def _twiddled(ar_ref, ai_ref, twr_ref, twi_ref, reps):
    a_r = ar_ref[...].astype(F32)
    a_i = ai_ref[...].astype(F32)
    tr = jnp.tile(twr_ref[...], (1, reps))
    ti = jnp.tile(twi_ref[...], (1, reps))
    st = jnp.concatenate([a_r * tr - a_i * ti, a_r * ti + a_i * tr], axis=0).astype(BF16)
    return st, tr, ti


def _fft2_filt_body(ar_ref, ai_ref, twr_ref, twi_ref, f2p_ref, k_ref):
    st, _, _ = _twiddled(ar_ref, ai_ref, twr_ref, twi_ref, ar_ref.shape[-1] // 128)
    k_ref[...] = _dg(f2p_ref[...], st).astype(BF16)


def _fft2_conv_body(ar_ref, ai_ref, twr_ref, twi_ref, k_ref, f2p_ref, f2pc_ref, br_ref, bi_ref):
    st, tr, ti = _twiddled(ar_ref, ai_ref, twr_ref, twi_ref, ar_ref.shape[-1] // 128)
    x = _dg(f2p_ref[...], st)
    xr, xi = x[:FFT_N2], x[FFT_N2:]
    kr = k_ref[:FFT_N2, :].astype(F32)
    ki = k_ref[FFT_N2:, :].astype(F32)
    sy = jnp.concatenate([xr * kr - xi * ki, xr * ki + xi * kr], axis=0).astype(BF16)
    b = _dg(f2pc_ref[...], sy)
    b_r, b_i = b[:FFT_N2], b[FFT_N2:]
    br_ref[...] = (b_r * tr + b_i * ti).astype(BF16)
    bi_ref[...] = (b_i * tr - b_r * ti).astype(BF16)


def _fft2_specs(ch):
    blk = pl.BlockSpec((None, FFT_N2, ch), lambda i: (i, 0, 0))
    tw = pl.BlockSpec((None, FFT_N2, 128), lambda i: (i, 0, 0))
    mat = pl.BlockSpec((2 * FFT_N2, 2 * FFT_N2), lambda i: (0, 0))
    return blk, tw, mat


def _fft2_filt_call(ar, ai, cst):
    ch = ar.shape[-1]
    blk, tw, mat = _fft2_specs(ch)
    return pl.pallas_call(
        _fft2_filt_body,
        grid=(FFT_KP,),
        in_specs=[blk, blk, tw, tw, mat],
        out_specs=pl.BlockSpec((None, 2 * FFT_N2, ch), lambda i: (i, 0, 0)),
        out_shape=jax.ShapeDtypeStruct((FFT_KP, 2 * FFT_N2, ch), BF16),
        compiler_params=_cparams(("arbitrary",), 32),
        name="fft2_filt",
    )(ar, ai, cst["twr"], cst["twi"], cst["f2p"])


def _fft2_conv_call(ar, ai, khat, cst):
    ch = ar.shape[-1]
    blk, tw, mat = _fft2_specs(ch)
    sh = jax.ShapeDtypeStruct((FFT_KP, FFT_N2, ch), BF16)
    return pl.pallas_call(
        _fft2_conv_body,
        grid=(FFT_KP,),
        in_specs=[blk, blk, tw, tw, pl.BlockSpec((None, 2 * FFT_N2, ch), lambda i: (i, 0, 0)), mat, mat],
        out_specs=[blk, blk],
        out_shape=[sh, sh],
        compiler_params=_cparams(("arbitrary",), 32),
        name="fft2_conv",
    )(ar, ai, cst["twr"], cst["twi"], khat, cst["f2p"], cst["f2pc"])


def _ifft1_body(gc_ref, gs_ref, br_ref, bi_ref, x0_ref, xv_ref, il_ref, ds_ref, o_ref):
    y = _dg(gc_ref[...], br_ref[...]) - _dg(gs_ref[...], bi_ref[...])
    o_ref[...] = (x0_ref[...].astype(F32)
                  * (y * il_ref[...] + ds_ref[...] * xv_ref[...].astype(F32))).astype(BF16)


def _ifft1_call(cst, br2d, bi2d, x0_2d, xv_2d, il_t, ds_t, cb):
    rows, cols = x0_2d.shape
    g = pl.BlockSpec((rows, FFT_KP), lambda i: (0, 0))
    kb = pl.BlockSpec((FFT_KP, cb), lambda i: (0, i))
    xb = pl.BlockSpec((rows, cb), lambda i: (0, i))
    vb = pl.BlockSpec((1, cb), lambda i: (0, 0))
    return pl.pallas_call(
        _ifft1_body,
        grid=(cols // cb,),
        in_specs=[g, g, kb, kb, xb, xb, vb, vb],
        out_specs=xb,
        out_shape=jax.ShapeDtypeStruct((rows, cols), BF16),
        compiler_params=_cparams(("arbitrary",), 32),
        name="ifft1",
    )(cst["gc"], cst["gs"], br2d, bi2d, x0_2d, xv_2d, il_t, ds_t)


def _outproj_body(hf_ref, hb_ref, zo_ref, yh_ref, x_ref, erow_ref, ecol_ref, gh_ref, wa_ref, wb_ref,
                  g1_ref, gf_ref, sh_ref, sc_ref, wr_ref, x1_ref, h2_ref, s_ref):
    hs = hf_ref[...].astype(F32) + hb_ref[...].astype(F32)
    gh = gh_ref[...]
    parts = []
    for h in range(N_HEADS):
        hh = hs[:, h * 128:(h + 1) * 128]
        ms = jnp.mean(hh * hh, axis=-1, keepdims=True)
        parts.append(hh * lax.rsqrt(ms + EPS) * gh[:, h * 128:(h + 1) * 128])
    ym = jnp.concatenate(parts, axis=-1) * _sigmoid(zo_ref[...].astype(F32))
    y = _dg(ym.astype(BF16), wa_ref[...]) + _dg(yh_ref[...], wb_ref[...])
    rp = x_ref.shape[0] // GRID_W
    erow8 = erow_ref[...]
    erow = erow8[0:rp, :]
    for q in range(1, 8 // rp):
        erow = jnp.where(pl.program_id(0) % (8 // rp) == q, erow8[q * rp:(q + 1) * rp, :], erow)
    x1 = _add_pos(x_ref[...], erow, ecol_ref[...]) + g1_ref[...] * y
    x1_ref[...] = x1
    h2 = _norm_mod(x1, gf_ref[...], sh_ref[...], sc_ref[...])
    half = h2.shape[1] // 2
    _store_slabs(h2_ref, _pack_pair(h2[:, :half], h2[:, half:]))
    s_ref[...] = _sigmoid(_dot3(wr_ref[...], h2, _NT))


def _outproj_call(hf, hb, z, yh, x, erow, ecol, gh, wa, wb, g1, gf, sh2, sc2, wrt, tm):
    m, d = x.shape
    row = lambda cb: pl.BlockSpec((tm, 1024), lambda i: (i, cb))
    vec = lambda n: pl.BlockSpec((1, n), lambda i: (0, 0))
    full = pl.BlockSpec((tm, d), lambda i: (i, 0))
    return pl.pallas_call(
        _outproj_body,
        grid=(m // tm,),
        in_specs=[row(0), row(0), row(ZC_O), row(0), full,
                  pl.BlockSpec((8, d // 2), lambda i: (i * (tm // GRID_W) // 8, 0)),
                  pl.BlockSpec((GRID_W, d // 2), lambda i: (0, 0)),
                  vec(MV_W),
                  pl.BlockSpec((MV_W, d), lambda i: (0, 0)), pl.BlockSpec((HY_W, d), lambda i: (0, 0)),
                  vec(d), vec(d), vec(d), vec(d),
                  pl.BlockSpec((N_EXPERTS, d), lambda i: (0, 0))],
        out_specs=[full, pl.BlockSpec((tm * SLAB, SLAB_W), lambda i: (i, 0)),
                   pl.BlockSpec((N_EXPERTS, tm), lambda i: (0, i))],
        out_shape=[jax.ShapeDtypeStruct((m, d), F32), jax.ShapeDtypeStruct((m * SLAB, SLAB_W), U32),
                   jax.ShapeDtypeStruct((N_EXPERTS, m), F32)],
        compiler_params=_cparams(("arbitrary",), 56),
        name="outproj",
    )(hf, hb, z, yh, x, erow, ecol, gh, wa, wb, g1, gf, sh2, sc2, wrt)


def _first_max(x, idx, sentinel):
    m = jnp.max(x, axis=0, keepdims=True)
    return m, jnp.min(jnp.where(x == m, idx, sentinel), axis=0, keepdims=True)


def _route_body(s_ref, b_ref, e_ref, w_ref, r_ref, cnt_ref, u_scr, run_scr):
    i = pl.program_id(0)
    tt = s_ref.shape[1]

    @pl.when(i == 0)
    def _():
        rr = lax.broadcasted_iota(I32, (tt, tt), 0)
        cc = lax.broadcasted_iota(I32, (tt, tt), 1)
        u_scr[...] = jnp.where(rr < cc, 1.0, 0.0).astype(BF16)
        run_scr[...] = jnp.zeros_like(run_scr)

    s = s_ref[...]
    sel = s + b_ref[...][:, 0:1]
    sub8 = lax.broadcasted_iota(I32, (E_PER_GROUP, tt), 0).astype(F32)
    gs = jnp.zeros((N_GROUPS, tt), F32)
    for g in range(N_GROUPS):
        grp = sel[g * E_PER_GROUP:(g + 1) * E_PER_GROUP, :]
        m1, i1 = _first_max(grp, sub8, float(E_PER_GROUP))
        m2 = jnp.max(jnp.where(sub8 == i1, -jnp.inf, grp), axis=0, keepdims=True)
        gs = jnp.where(sub8 == g, m1 + m2, gs)
    gmask = jnp.zeros((N_GROUPS, tt), F32)
    for _ in range(TOPK_GROUPS):
        _, ig = _first_max(gs, sub8, float(N_GROUPS))
        hit = sub8 == ig
        gmask = jnp.where(hit, 1.0, gmask)
        gs = jnp.where(hit, -jnp.inf, gs)
    masked = jnp.concatenate(
        [jnp.where(jnp.broadcast_to(gmask[g:g + 1, :], (E_PER_GROUP, tt)) > 0.5,
                   sel[g * E_PER_GROUP:(g + 1) * E_PER_GROUP, :], -jnp.inf) for g in range(N_GROUPS)], axis=0)
    sub64 = lax.broadcasted_iota(I32, (N_EXPERTS, tt), 0).astype(F32)
    oh = jnp.zeros((N_EXPERTS, tt), F32)
    eks, wks = [], []
    for _ in range(TOP_K):
        _, ie = _first_max(masked, sub64, float(N_EXPERTS))
        hit = sub64 == ie
        wks.append(jnp.sum(jnp.where(hit, s, 0.0), axis=0, keepdims=True))
        eks.append(ie)
        masked = jnp.where(hit, -jnp.inf, masked)
        oh = jnp.where(hit, 1.0, oh)
    wsum = wks[0]
    for k in range(1, TOP_K):
        wsum = wsum + wks[k]
    run = run_scr[...]
    rank_t = _dg(oh.astype(BF16), u_scr[...]) + jnp.tile(run, (1, tt // 128))
    for k in range(TOP_K):
        e_ref[k:k + 1, :] = eks[k].astype(I32)
        w_ref[k:k + 1, :] = wks[k] / wsum * ROUTE_SCALE
        r_ref[k:k + 1, :] = jnp.sum(jnp.where(sub64 == eks[k], rank_t, 0.0), axis=0, keepdims=True).astype(I32)
    run_new = run + jnp.sum(oh, axis=1, keepdims=True)
    run_scr[...] = run_new
    cnt_ref[...] = run_new.astype(I32)


def _route_call(s_t, b_col, tt):
    m = s_t.shape[1]
    out = pl.BlockSpec((TOP_K, tt), lambda i: (0, i))
    return pl.pallas_call(
        _route_body,
        grid=(m // tt,),
        in_specs=[pl.BlockSpec((N_EXPERTS, tt), lambda i: (0, i)),
                  pl.BlockSpec((N_EXPERTS, 128), lambda i: (0, 0))],
        out_specs=[out, out, out, pl.BlockSpec((N_EXPERTS, 128), lambda i: (0, 0))],
        out_shape=[jax.ShapeDtypeStruct((TOP_K, m), I32), jax.ShapeDtypeStruct((TOP_K, m), F32),
                   jax.ShapeDtypeStruct((TOP_K, m), I32), jax.ShapeDtypeStruct((N_EXPERTS, 128), I32)],
        scratch_shapes=[pltpu.VMEM((tt, tt), BF16), pltpu.VMEM((N_EXPERTS, 128), F32)],
        compiler_params=_cparams(("arbitrary",), 32),
        name="route",
    )(s_t, b_col)


def _posk_body(pst_ref, e_ref, r_ref, p_ref):
    e = e_ref[...]
    acc = r_ref[...]
    for x in range(N_EXPERTS):
        acc = acc + jnp.where(e == x, pst_ref[x], 0)
    p_ref[...] = acc


def _posk_call(pstart, eidx, rank):
    k, m = eidx.shape
    tt = min(m, 2048)
    blk = pl.BlockSpec((k, tt), lambda i, pst: (0, i))
    return pl.pallas_call(
        _posk_body,
        grid_spec=pltpu.PrefetchScalarGridSpec(num_scalar_prefetch=1, grid=(m // tt,),
                                               in_specs=[blk, blk], out_specs=blk),
        out_shape=jax.ShapeDtypeStruct((k, m), I32),
        compiler_params=_cparams(("arbitrary",), 32),
        name="posk",
    )(pstart, eidx, rank)


def _slab(ref, r):
    return ref.at[pl.ds(pl.multiple_of(r * SLAB, SLAB), SLAB), :]


def _dispatch_body(cnt_ref, pst_ref, pcn_ref, h2_ref, pos_ref, xs_ref, zrow, sem):
    i = pl.program_id(0)
    td = h2_ref.shape[0] // SLAB

    def row_copy(t, dst):
        return pltpu.make_async_copy(_slab(h2_ref, t), _slab(xs_ref, dst), sem)

    def issue(t, carry):
        for k in range(TOP_K):
            row_copy(t, pos_ref[k, t]).start(priority=k % 2)
        return carry

    lax.fori_loop(0, td, issue, 0)

    def drain(t, carry):
        for k in range(TOP_K):
            row_copy(0, 0).wait()
        return carry

    lax.fori_loop(0, td, drain, 0)

    @pl.when(i == pl.num_programs(0) - 1)
    def _():
        zrow[...] = jnp.zeros_like(zrow)

        def zero_copy(dst):
            return pltpu.make_async_copy(zrow, _slab(xs_ref, dst), sem)

        def per_expert(e, carry):
            base = pst_ref[e]
            lax.fori_loop(cnt_ref[e], pcn_ref[e], lambda rr, cc: (zero_copy(base + rr).start(), cc)[1], 0)
            lax.fori_loop(cnt_ref[e], pcn_ref[e], lambda rr, cc: (zero_copy(0).wait(), cc)[1], 0)
            return carry

        lax.fori_loop(0, N_EXPERTS, per_expert, 0)


def _dispatch_call(cnt, pstart, pcnt, h2s, pos, rows, td):
    m = h2s.shape[0] // SLAB
    return pl.pallas_call(
        _dispatch_body,
        grid_spec=pltpu.PrefetchScalarGridSpec(
            num_scalar_prefetch=3, grid=(m // td,),
            in_specs=[pl.BlockSpec((td * SLAB, SLAB_W), lambda i, *_: (i, 0)),
                      pl.BlockSpec((TOP_K, td), lambda i, *_: (0, i), memory_space=pltpu.SMEM)],
            out_specs=pl.BlockSpec(memory_space=pl.ANY),
            scratch_shapes=[pltpu.VMEM((SLAB, SLAB_W), U32), pltpu.SemaphoreType.DMA(())]),
        out_shape=jax.ShapeDtypeStruct((rows * SLAB, SLAB_W), U32),
        compiler_params=_cparams(("arbitrary",), 32),
        name="dispatch",
    )(cnt, pstart, pcnt, h2s, pos)


def _moe_body(te_ref, nu_ref, x_ref, w1_ref, w3_ref, w2_ref, y_ref, w1b, w3b, w2b):
    i = pl.program_id(0)
    used = i < nu_ref[0]
    prev = te_ref[jnp.maximum(i - 1, 0)]

    @pl.when(jnp.logical_and(used, jnp.logical_or(i == 0, te_ref[i] != prev)))
    def _():
        w1b[...] = w1_ref[...].astype(BF16)
        w3b[...] = w3_ref[...].astype(BF16)
        w2b[...] = w2_ref[...].astype(BF16)

    @pl.when(used)
    def _():
        half = w1b.shape[0] // 2
        xa, xb = _unpack_pair(_load_slabs(x_ref, MOE_ROWS))
        xa = xa.astype(BF16)
        xb = xb.astype(BF16)
        h1 = _dg(xa, w1b[0:half, :]) + _dg(xb, w1b[half:, :])
        h3 = _dg(xa, w3b[0:half, :]) + _dg(xb, w3b[half:, :])
        y = _dg((_silu(h1) * h3).astype(BF16), w2b[...])
        _store_slabs(y_ref, _pack_pair(y[:, :half], y[:, half:]))


def _moe_call(tile_e, n_used, xs, w1, w3, w2):
    rows = xs.shape[0] // SLAB
    nt = rows // MOE_ROWS
    d, de = w1.shape[-2:]
    rmap = lambda i, te, nu: (jnp.minimum(i, nu[0] - 1), 0)
    wmap = lambda i, te, nu: (te[i], 0, 0)
    return pl.pallas_call(
        _moe_body,
        grid_spec=pltpu.PrefetchScalarGridSpec(
            num_scalar_prefetch=2, grid=(nt,),
            in_specs=[pl.BlockSpec((MOE_ROWS * SLAB, SLAB_W), rmap),
                      pl.BlockSpec((None, d, de), wmap), pl.BlockSpec((None, d, de), wmap),
                      pl.BlockSpec((None, de, d), wmap)],
            out_specs=pl.BlockSpec((MOE_ROWS * SLAB, SLAB_W), rmap),
            scratch_shapes=[pltpu.VMEM((d, de), BF16), pltpu.VMEM((d, de), BF16), pltpu.VMEM((de, d), BF16)]),
        out_shape=jax.ShapeDtypeStruct((rows * SLAB, SLAB_W), U32),
        compiler_params=_cparams(("arbitrary",), 56),
        name="moe",
    )(tile_e, n_used, xs, w1, w3, w2)


def _final_body(x1_ref, h2_ref, pos_ref, wt_ref, ys_ref, g2_ref, gn_ref, w1_ref, w3_ref, w2_ref,
                o_ref, ybuf, sem):
    tf, d = x1_ref.shape
    half = d // 2

    def row_copy(k, t, src):
        return pltpu.make_async_copy(_slab(ys_ref, src), _slab(ybuf, k * tf + t), sem)

    def issue(t, carry):
        for k in range(TOP_K):
            row_copy(k, t, pos_ref[k, t]).start(priority=k % 2)
        return carry

    lax.fori_loop(0, tf, issue, 0)

    ha, hb = _unpack_pair(_load_slabs(h2_ref, tf))
    ha = ha.astype(BF16)
    hb = hb.astype(BF16)
    s1 = _dg(ha, w1_ref[0:half, :]) + _dg(hb, w1_ref[half:, :])
    s3 = _dg(ha, w3_ref[0:half, :]) + _dg(hb, w3_ref[half:, :])
    shared = _dg((_silu(s1) * s3).astype(BF16), w2_ref[...])

    def drain(t, carry):
        for k in range(TOP_K):
            row_copy(0, 0, 0).wait()
        return carry

    lax.fori_loop(0, tf, drain, 0)

    wt = jnp.concatenate([jnp.transpose(jnp.concatenate(
        [wt_ref[:, c * 128:(c + 1) * 128], jnp.zeros((128 - TOP_K, 128), F32)], axis=0)) for c in range(tf // 128)],
        axis=0)
    acc_a = shared[:, :half]
    acc_b = shared[:, half:]
    for k in range(TOP_K):
        ya, yb = _unpack_pair(_load_slabs(ybuf, tf, base=k * tf * SLAB))
        acc_a = acc_a + wt[:, k:k + 1] * ya
        acc_b = acc_b + wt[:, k:k + 1] * yb
    xo = x1_ref[...] + g2_ref[...] * jnp.concatenate([acc_a, acc_b], axis=1)
    ms = jnp.mean(xo * xo, axis=-1, keepdims=True)
    o_ref[...] = xo * lax.rsqrt(ms + EPS) * gn_ref[...]


def _final_call(x1, h2s, pos, wts, ys, g2, gn, w1s, w3s, w2s, tf):
    m, d = x1.shape
    ds = w1s.shape[1]
    full = pl.BlockSpec((tf, d), lambda i: (i, 0))
    vec = pl.BlockSpec((1, d), lambda i: (0, 0))
    return pl.pallas_call(
        _final_body,
        grid=(m // tf,),
        in_specs=[full, pl.BlockSpec((tf * SLAB, SLAB_W), lambda i: (i, 0)),
                  pl.BlockSpec((TOP_K, tf), lambda i: (0, i), memory_space=pltpu.SMEM),
                  pl.BlockSpec((TOP_K, tf), lambda i: (0, i)),
                  pl.BlockSpec(memory_space=pl.ANY), vec, vec,
                  pl.BlockSpec((d, ds), lambda i: (0, 0)), pl.BlockSpec((d, ds), lambda i: (0, 0)),
                  pl.BlockSpec((ds, d), lambda i: (0, 0))],
        out_specs=full,
        out_shape=jax.ShapeDtypeStruct((m, d), F32),
        scratch_shapes=[pltpu.VMEM((TOP_K * tf * SLAB, SLAB_W), U32), pltpu.SemaphoreType.DMA(())],
        compiler_params=_cparams(("arbitrary",), 48),
        name="final",
    )(x1, h2s, pos, wts, ys, g2, gn, w1s, w3s, w2s)


def _pos_tables(n_tokens):
    rows = n_tokens // GRID_W
    quarter = D_MODEL // 4
    omega = 1.0 / (10000.0 ** (jnp.arange(quarter, dtype=F32) / quarter))

    def emb1d(pos):
        ang = pos[:, None] * omega[None]
        return jnp.concatenate([jnp.sin(ang), jnp.cos(ang)], axis=-1)

    return emb1d(jnp.arange(rows, dtype=F32)), emb1d(jnp.arange(GRID_W, dtype=F32))


def _filter_feats(L):
    n = jnp.arange(2 * L, dtype=I32)
    t = jnp.where(n <= L, n, 2 * L - n).astype(F32)
    t01 = t / max(L - 1, 1)
    w = 2.0 * math.pi * t / L
    bands = jnp.linspace(1e-4, FILT_BANDS - 1, FILT_BANDS, dtype=F32)
    valid = (n != L).astype(F32)
    feats = jnp.concatenate([t01[:, None], jnp.cos(w[:, None] * bands), -jnp.sin(w[:, None] * bands),
                             valid[:, None], jnp.zeros((2 * L, 64 - 34), F32)], axis=-1)
    return feats


def _pad_rows(a, rows):
    return jnp.concatenate([a, jnp.zeros((rows - a.shape[0],) + a.shape[1:], a.dtype)], axis=0)


def _layer(x, c, ctx, c_ctx, w_ada, b_ada, g_mix, g_ffn, w_in, b_gates, conv_k_w, conv_k_b,
           conv_q_w, conv_q_b, g_head, conv_hy_w, conv_hy_b, filt_w1, filt_b1, filt_freq1,
           filt_w2, filt_b2, filt_freq2, filt_w3, filt_b3, hy_dskip, w_out, w_router, b_router,
           w1_e, w3_e, w2_e, w1_s, w3_s, w2_s, g_final):
    L, d = x.shape
    lc = ctx.shape[0]
    row = lambda v: v.reshape(1, -1)

    cc = _pad_rows(jnp.stack([c, c_ctx], axis=0), 8)
    mods = _mod_call(cc, w_ada, row(b_ada))
    sh1, sc1, g1, sh2, sc2, g2 = [mods[0:1, k * d:(k + 1) * d] for k in range(6)]
    csh1, csc1 = mods[1:2, 0:d], mods[1:2, d:2 * d]

    w_r = jnp.concatenate([w_in[:, OFF_K:OFF_V], w_in[:, OFF_Q:OFF_O], w_in[:, OFF_V:OFF_G],
                           w_in[:, OFF_O:]], axis=1).astype(BF16)
    w_g = jnp.concatenate([w_in[:, OFF_G:OFF_Q], jnp.zeros((d, GATE_PAD - 4 * N_HEADS), F32)], axis=1)
    bg = jnp.concatenate([b_gates, jnp.zeros((GATE_PAD - 4 * N_HEADS,), F32)]).reshape(1, GATE_PAD)
    e_row, e_col = _pos_tables(L)
    conv_w = jnp.concatenate([conv_k_w, conv_q_w], axis=1)
    conv_b = jnp.concatenate([conv_k_b, conv_q_b]).reshape(1, -1)
    conv_s = jnp.concatenate([jnp.ones((QK_W,), F32), jnp.full((QK_W,), QK_HEAD ** -0.5, F32)]).reshape(1, -1)

    z_c, gt_c = _inproj_call(ctx, jnp.zeros((8, d // 2), F32), e_col, row(g_mix), csh1, csc1, w_r, w_g,
                             use_pos=False, tm=min(lc, 256))
    kq_c = _conv_kq_call(z_c, conv_w, conv_b, conv_s, tm=min(lc, 256))
    s0 = jnp.zeros((2 * N_HEADS, CHUNK, 256), F32)
    m0 = jnp.zeros((2, 8, 128), F32)
    _, _, s_ctx, m_ctx = _mlstm_call(kq_c, z_c, gt_c, bg, s0, m0)

    z, gates = _inproj_call(x, e_row, e_col, row(g_mix), sh1, sc1, w_r, w_g, use_pos=True, tm=min(L, 1024))
    kq = _conv_kq_call(z, conv_w, conv_b, conv_s, tm=min(L, 512))
    x0c, xv = _conv_hy_call(z, conv_hy_w, row(conv_hy_b), tm=min(L, 512))
    hf, hb, _, _ = _mlstm_call(kq, z, gates, bg, s_ctx, m_ctx)

    n1 = 2 * L // FFT_N2
    cst = _fft_consts(n1)
    rates = jnp.linspace(-math.log(DECAY_TARGET) / SLOW_DECAY_PCT, -math.log(DECAY_TARGET) / FAST_DECAY_PCT,
                         HY_W, dtype=F32).reshape(1, -1)
    w1p = _pad_rows(filt_w1, 64)
    kf, l1 = _filt_call(_filter_feats(L), w1p, row(filt_b1), row(filt_freq1), filt_w2, row(filt_b2),
                        row(filt_freq2), filt_w3, row(filt_b3), rates, tn=min(L, 1024))
    cols = FFT_N2 * HY_W
    cb = 2048
    kar, kai = _fft1_call(cst["f1"], kf.reshape(n1, cols), cb)
    khat = _fft2_filt_call(kar.reshape(FFT_KP, FFT_N2, HY_W), kai.reshape(FFT_KP, FFT_N2, HY_W), cst)
    uar, uai = _fft1_call(cst["f1"], xv.reshape(n1 // 2, cols), cb)
    br, bi = _fft2_conv_call(uar.reshape(FFT_KP, FFT_N2, HY_W), uai.reshape(FFT_KP, FFT_N2, HY_W), khat, cst)
    reps = cb // HY_W
    il_t = jnp.tile(1.0 / l1, (1, reps))
    ds_t = jnp.tile(row(hy_dskip), (1, reps))
    yh = _ifft1_call(cst, br.reshape(FFT_KP, cols), bi.reshape(FFT_KP, cols),
                     x0c.reshape(n1 // 2, cols), xv.reshape(n1 // 2, cols), il_t, ds_t, cb).reshape(L, HY_W)

    wo = w_out.astype(BF16)
    x1, h2s, s_t = _outproj_call(hf, hb, z, yh, x, e_row, e_col, row(g_head), wo[:MV_W], wo[MV_W:],
                                g1, row(g_ffn), sh2, sc2, jnp.transpose(w_router), tm=min(L, 256))

    b_col = jnp.broadcast_to(b_router.reshape(N_EXPERTS, 1), (N_EXPERTS, 128))
    eidx, wts, rank, cnt2 = _route_call(s_t, b_col, tt=min(L, 1024))
    cnt = cnt2[:, 0]
    pcnt = (cnt + MOE_ROWS - 1) // MOE_ROWS * MOE_ROWS
    pend = jnp.cumsum(pcnt)
    pstart = pend - pcnt
    rows = L * TOP_K + N_EXPERTS * MOE_ROWS
    nt = rows // MOE_ROWS
    tile_row = jnp.arange(nt, dtype=I32) * MOE_ROWS
    tile_e = jnp.minimum(jnp.sum((pend[None, :] <= tile_row[:, None]).astype(I32), axis=1), N_EXPERTS - 1)
    n_used = (pend[-1] // MOE_ROWS).astype(I32).reshape(1)
    pos = _posk_call(pstart.astype(I32), eidx, rank)

    xs = _dispatch_call(cnt, pstart.astype(I32), pcnt.astype(I32), h2s, pos, rows, td=min(L, 256))
    ys = _moe_call(tile_e, n_used, xs, w1_e, w3_e, w2_e)
    return _final_call(x1, h2s, pos, wts, ys, g2, row(g_final), w1_s.astype(BF16), w3_s.astype(BF16),
                       w2_s.astype(BF16), tf=min(L, 256))


def kernel(x, c, ctx, c_ctx, w_ada, b_ada, g_mix, g_ffn, w_in, b_gates, conv_k_w, conv_k_b, conv_q_w,
           conv_q_b, g_head, conv_hy_w, conv_hy_b, filt_w1, filt_b1, filt_freq1, filt_w2, filt_b2,
           filt_freq2, filt_w3, filt_b3, hy_dskip, w_out, w_router, b_router, w1_e, w3_e, w2_e,
           w1_s, w3_s, w2_s, g_final):
    assert x.shape[0] == 1 and w_ada.shape[0] == 1, "one batch element, one layer"
    out = _layer(x[0], c[0], ctx[0], c_ctx, w_ada[0], b_ada[0], g_mix[0], g_ffn[0], w_in[0], b_gates[0],
                 conv_k_w[0], conv_k_b[0], conv_q_w[0], conv_q_b[0], g_head[0], conv_hy_w[0], conv_hy_b[0],
                 filt_w1[0], filt_b1[0], filt_freq1[0], filt_w2[0], filt_b2[0], filt_freq2[0], filt_w3[0],
                 filt_b3[0], hy_dskip[0], w_out[0], w_router[0], b_router[0], w1_e[0], w3_e[0], w2_e[0],
                 w1_s[0], w3_s[0], w2_s[0], g_final)
    return out[None]
```

```python
import functools
import math

import numpy as np
import jax
import jax.numpy as jnp
from jax import lax
from jax.experimental import pallas as pl
from jax.experimental.pallas import tpu as pltpu

F32 = jnp.float32
BF16 = jnp.bfloat16
I32 = jnp.int32
U32 = jnp.uint32

D_MODEL = 2048
GRID_W = 64
N_HEADS = 8
QK_HEAD = 64
V_HEAD = 128
QK_W = N_HEADS * QK_HEAD
MV_W = N_HEADS * V_HEAD
HY_W = D_MODEL - MV_W
CHUNK = 128
FILT_BANDS = 16
FILT_HIDDEN = 64
DECAY_TARGET = 1e-2
FAST_DECAY_PCT = 0.3
SLOW_DECAY_PCT = 1.5
N_EXPERTS = 64
N_GROUPS = 8
E_PER_GROUP = 8
TOPK_GROUPS = 4
TOP_K = 8
D_EXPERT = 512
ROUTE_SCALE = 2.5
EPS = 1e-6
OFF_K = 0
OFF_V = OFF_K + QK_W
OFF_G = OFF_V + MV_W
OFF_Q = OFF_G + 4 * N_HEADS
OFF_O = OFF_Q + QK_W
OFF_HY = OFF_O + MV_W

ZC_KQ, ZC_V, ZC_O, ZC_X0, ZC_X1, ZC_HV = 0, 1, 2, 3, 4, 5
Z_COLS = 6 * 1024
GATE_PAD = 128

NEG = -1e30
MIB = 1024 * 1024

FFT_N2 = 128
FFT_KP = 144

MOE_ROWS = 256


def _cparams(sem, vmem_mb, flags=None):
    return pltpu.CompilerParams(dimension_semantics=sem, vmem_limit_bytes=vmem_mb * MIB, flags=flags)


def _split2(x):
    hi = x.astype(BF16)
    lo = (x - hi.astype(F32)).astype(BF16)
    return hi, lo


_NN = (((1,), (0,)), ((), ()))
_NT = (((1,), (1,)), ((), ()))
_TN = (((0,), (0,)), ((), ()))


def _dg(a, b, dims=_NN):
    return lax.dot_general(a, b, dims, preferred_element_type=F32)


def _dot3(a, b, dims=_NN):
    ah, al = _split2(a)
    bh, bl = _split2(b)
    return _dg(ah, bh, dims) + _dg(al, bh, dims) + _dg(ah, bl, dims)


def _sigmoid(x):
    return 1.0 / (1.0 + jnp.exp(-x))


def _silu(x):
    return x * _sigmoid(x)


def _pack_pair(a, b):
    hi = lax.bitcast_convert_type(a.astype(BF16).astype(F32), U32)
    lo = lax.bitcast_convert_type(b.astype(BF16).astype(F32), U32)
    return hi | (lo >> 16)


def _unpack_pair(w):
    a = lax.bitcast_convert_type(w & jnp.uint32(0xFFFF0000), F32)
    b = lax.bitcast_convert_type(w << 16, F32)
    return a, b


SLAB = 8
SLAB_W = 128


def _store_slabs(ref, w, base=0):
    r = w.shape[0]
    for j in range(SLAB):
        ref[pl.ds(base + j, r, stride=SLAB), :] = w[:, j * SLAB_W:(j + 1) * SLAB_W]


def _load_slabs(ref, r, base=0):
    return jnp.concatenate([ref[pl.ds(base + j, r, stride=SLAB), :] for j in range(SLAB)], axis=1)


def _norm_mod(x, g, sh, sc):
    ms = jnp.mean(x * x, axis=-1, keepdims=True)
    return (x * lax.rsqrt(ms + EPS) * g) * (1.0 + sc) + sh


def _add_pos(x, erow, ecol):
    tm, d = x.shape
    half = d // 2
    parts = []
    for r in range(tm // GRID_W):
        xs = x[r * GRID_W:(r + 1) * GRID_W, :]
        parts.append(jnp.concatenate([xs[:, :half] + erow[r:r + 1, :], xs[:, half:] + ecol], axis=-1))
    return parts[0] if len(parts) == 1 else jnp.concatenate(parts, axis=0)


def _mod_body(cc_ref, w_ref, b_ref, o_ref):
    o_ref[...] = _dot3(_silu(cc_ref[...]), w_ref[...]) + b_ref[...]


def _mod_call(cc, w, b):
    d, n = w.shape
    tn = 1024
    return pl.pallas_call(
        _mod_body,
        grid=(n // tn,),
        in_specs=[pl.BlockSpec((8, d), lambda j: (0, 0)),
                  pl.BlockSpec((d, tn), lambda j: (0, j)),
                  pl.BlockSpec((1, tn), lambda j: (0, j))],
        out_specs=pl.BlockSpec((8, tn), lambda j: (0, j)),
        out_shape=jax.ShapeDtypeStruct((8, n), F32),
        compiler_params=_cparams(("arbitrary",), 40),
        name="mod",
    )(cc, w, b)


def _inproj_body(use_pos, x_ref, erow_ref, ecol_ref, gm_ref, sh_ref, sc_ref, w_ref, wg_ref,
                 z_ref, g_ref, h_scr):
    @pl.when(pl.program_id(1) == 0)
    def _():
        x = x_ref[...]
        if use_pos:
            x = _add_pos(x, erow_ref[...], ecol_ref[...])
        h = _norm_mod(x, gm_ref[...], sh_ref[...], sc_ref[...])
        h_scr[...] = h.astype(BF16)
        g_ref[...] = _dot3(h, wg_ref[...])

    z_ref[...] = jnp.dot(h_scr[...], w_ref[...], preferred_element_type=F32).astype(BF16)


def _inproj_call(x, erow, ecol, gm, sh, sc, w, wg, use_pos, tm):
    m, d = x.shape
    tn = 1024
    er = tm // GRID_W if use_pos else erow.shape[0]
    row_map = (lambda i, j: (i, 0)) if use_pos else (lambda i, j: (0, 0))
    return pl.pallas_call(
        functools.partial(_inproj_body, use_pos),
        grid=(m // tm, Z_COLS // tn),
        in_specs=[pl.BlockSpec((tm, d), lambda i, j: (i, 0)),
                  pl.BlockSpec((er, d // 2), row_map),
                  pl.BlockSpec((GRID_W, d // 2), lambda i, j: (0, 0)),
                  pl.BlockSpec((1, d), lambda i, j: (0, 0)),
                  pl.BlockSpec((1, d), lambda i, j: (0, 0)),
                  pl.BlockSpec((1, d), lambda i, j: (0, 0)),
                  pl.BlockSpec((d, tn), lambda i, j: (0, j)),
                  pl.BlockSpec((d, GATE_PAD), lambda i, j: (0, 0))],
        out_specs=[pl.BlockSpec((tm, tn), lambda i, j: (i, j)),
                   pl.BlockSpec((tm, GATE_PAD), lambda i, j: (i, 0))],
        out_shape=[jax.ShapeDtypeStruct((m, Z_COLS), BF16),
                   jax.ShapeDtypeStruct((m, GATE_PAD), F32)],
        scratch_shapes=[pltpu.VMEM((tm, d), BF16)],
        compiler_params=_cparams(("arbitrary", "arbitrary"), 48),
        name="inproj",
    )(x, erow, ecol, gm, sh, sc, w, wg)


def _conv3(zc, zp, zn, w, b, first, last):
    tm = zc.shape[0]
    row = lax.broadcasted_iota(I32, zc.shape, 0)
    prev_row = jnp.where(first, 0.0, zp[7:8, :])
    next_row = jnp.where(last, 0.0, zn[0:1, :])
    xm = jnp.where(row == 0, prev_row, pltpu.roll(zc, 1, 0))
    xp = jnp.where(row == tm - 1, next_row, pltpu.roll(zc, tm - 1, 0))
    return xm * w[0:1, :] + zc * w[1:2, :] + xp * w[2:3, :] + b


def _conv_kq_body(zc_ref, zp_ref, zn_ref, w_ref, b_ref, s_ref, o_ref):
    i = pl.program_id(0)
    u = _conv3(zc_ref[...].astype(F32), zp_ref[...].astype(F32), zn_ref[...].astype(F32),
               w_ref[...], b_ref[...], i == 0, i == pl.num_programs(0) - 1)
    o_ref[...] = (_silu(u) * s_ref[...]).astype(BF16)


def _halo_specs(tm, m, cb):
    nb8 = m // 8
    return [pl.BlockSpec((tm, 1024), lambda i: (i, cb)),
            pl.BlockSpec((8, 1024), lambda i: (jnp.maximum(i * (tm // 8) - 1, 0), cb)),
            pl.BlockSpec((8, 1024), lambda i: (jnp.minimum((i + 1) * (tm // 8), nb8 - 1), cb))]


def _conv_kq_call(z, w, b, s, tm):
    m = z.shape[0]
    vec = pl.BlockSpec((1, 1024), lambda i: (0, 0))
    return pl.pallas_call(
        _conv_kq_body,
        grid=(m // tm,),
        in_specs=_halo_specs(tm, m, ZC_KQ) + [pl.BlockSpec((3, 1024), lambda i: (0, 0)), vec, vec],
        out_specs=pl.BlockSpec((tm, 1024), lambda i: (i, 0)),
        out_shape=jax.ShapeDtypeStruct((m, 1024), BF16),
        compiler_params=_cparams(("arbitrary",), 32),
        name="conv_kq",
    )(z, z, z, w, b, s)


def _conv_hy_body(ac_ref, ap_ref, an_ref, bc_ref, bp_ref, bn_ref, cc_ref, cp_ref, cn_ref,
                  w_ref, b_ref, x0_ref, xv_ref):
    i = pl.program_id(0)
    first, last = i == 0, i == pl.num_programs(0) - 1
    w = w_ref[...]
    b = b_ref[...]

    def cv(c, p, n, k):
        return _conv3(c[...].astype(F32), p[...].astype(F32), n[...].astype(F32),
                      w[:, k * 1024:(k + 1) * 1024], b[:, k * 1024:(k + 1) * 1024], first, last)

    x0_ref[...] = cv(ac_ref, ap_ref, an_ref, 0).astype(BF16)
    xv_ref[...] = (cv(bc_ref, bp_ref, bn_ref, 1) * cv(cc_ref, cp_ref, cn_ref, 2)).astype(BF16)


def _conv_hy_call(z, w, b, tm):
    m = z.shape[0]
    out = pl.BlockSpec((tm, 1024), lambda i: (i, 0))
    return pl.pallas_call(
        _conv_hy_body,
        grid=(m // tm,),
        in_specs=(_halo_specs(tm, m, ZC_X0) + _halo_specs(tm, m, ZC_X1) + _halo_specs(tm, m, ZC_HV)
                  + [pl.BlockSpec((3, 3072), lambda i: (0, 0)), pl.BlockSpec((1, 3072), lambda i: (0, 0))]),
        out_specs=[out, out],
        out_shape=[jax.ShapeDtypeStruct((m, 1024), BF16), jax.ShapeDtypeStruct((m, 1024), BF16)],
        compiler_params=_cparams(("arbitrary",), 32),
        name="conv_hy",
    )(z, z, z, z, z, z, z, z, z, w, b)


def _mlstm_body(kqf_ref, vf_ref, gf_ref, kqb_ref, vb_ref, gb_ref, bg_ref, s0_ref, m0_ref,
                hf_ref, hb_ref, sfin_ref, mfin_ref, s_scr, m_scr):
    j = pl.program_id(0)

    @pl.when(j == 0)
    def _():
        s_scr[...] = s0_ref[...]
        m_scr[...] = m0_ref[...]

    r = lax.broadcasted_iota(I32, (CHUNK, CHUNK), 0)
    c = lax.broadcasted_iota(I32, (CHUNK, CHUNK), 1)
    e1 = jnp.where(c == 0, 1.0, 0.0).astype(BF16)
    bg = bg_ref[...]

    for d in range(2):
        kq = (kqf_ref, kqb_ref)[d][...]
        v = (vf_ref, vb_ref)[d][...]
        g_all = (gf_ref, gb_ref)[d][...] + bg
        out_ref = (hf_ref, hb_ref)[d]
        tri = (r >= c) if d == 0 else (c >= r)
        tri_b = jnp.where(tri, 1.0, 0.0).astype(BF16)
        gi = g_all if d == 0 else pltpu.roll(g_all, CHUNK - 16, 1)
        gfp = pltpu.roll(g_all, CHUNK - 8 - 16 * d, 1)
        lf = jnp.minimum(gfp, 0.0) - jnp.log(1.0 + jnp.exp(-jnp.abs(gfp)))
        l1 = lf.astype(BF16)
        r1 = lf - l1.astype(F32)
        l2 = r1.astype(BF16)
        l3 = (r1 - l2.astype(F32)).astype(BF16)
        bcum = _dg(tri_b, l1) + _dg(tri_b, l2) + _dg(tri_b, l3)
        gtot = bcum[CHUNK - 1:CHUNK, :] if d == 0 else bcum[0:1, :]
        acol = gtot - bcum + gi
        m_loc = jnp.max(acol, axis=0, keepdims=True)
        wg = jnp.exp(acol - m_loc)
        m_st = m_scr[d, 0:1, :]
        inter = bcum + m_st
        m_new = jnp.maximum(gtot + m_st, m_loc)
        sp = jnp.broadcast_to(jnp.exp(gtot + m_st - m_new), (CHUNK, CHUNK))
        sl = jnp.broadcast_to(jnp.exp(m_loc - m_new), (CHUNK, CHUNK))
        rt = jnp.transpose(gi - bcum)

        for h in range(N_HEADS):
            p, half = divmod(h, 2)
            lm = (c // QK_HEAD) == half
            kp = kq[:, p * 128:(p + 1) * 128]
            qp = kq[:, QK_W + p * 128:QK_W + (p + 1) * 128]
            vaug = jnp.concatenate([v[:, h * 128:(h + 1) * 128], e1], axis=1)
            qm = jnp.where(lm, qp, jnp.zeros_like(qp))
            dl = jnp.where(tri, bcum[:, h:h + 1] + rt[h:h + 1, :], NEG)
            icol = inter[:, h:h + 1]
            mt = jnp.maximum(icol, jnp.max(dl, axis=1, keepdims=True))
            pm = jnp.exp(dl - mt)
            s = (_dg(qm, kp, _NT) * pm).astype(BF16)
            st = s_scr[d * N_HEADS + h]
            tot = _dg(s, vaug) + jnp.exp(icol - mt) * _dg(qm, st.astype(BF16))
            den = jnp.maximum(jnp.abs(tot[:, 128:129]), jnp.exp(-mt))
            out_ref[:, h * 128:(h + 1) * 128] = (tot[:, :128] / den).astype(BF16)
            kw = jnp.where(lm, kp.astype(F32) * wg[:, h:h + 1], 0.0).astype(BF16)
            s_scr[d * N_HEADS + h] = sp[:, h:h + 1] * st + sl[:, h:h + 1] * _dg(kw, vaug, _TN)
        m_scr[d, 0:1, :] = m_new

    @pl.when(j == pl.num_programs(0) - 1)
    def _():
        sfin_ref[...] = s_scr[...]
        mfin_ref[...] = m_scr[...]


def _mlstm_call(kq, z, gates, bg, s0, m0):
    m = kq.shape[0]
    nc = m // CHUNK
    fwd = lambda cb: (lambda j: (j, cb))
    bwd = lambda cb: (lambda j: (nc - 1 - j, cb))
    st_spec = pl.BlockSpec((2 * N_HEADS, CHUNK, 256), lambda j: (0, 0, 0))
    m_spec = pl.BlockSpec((2, 8, 128), lambda j: (0, 0, 0))
    return pl.pallas_call(
        _mlstm_body,
        grid=(nc,),
        in_specs=[pl.BlockSpec((CHUNK, 1024), fwd(0)), pl.BlockSpec((CHUNK, 1024), fwd(ZC_V)),
                  pl.BlockSpec((CHUNK, GATE_PAD), fwd(0)),
                  pl.BlockSpec((CHUNK, 1024), bwd(0)), pl.BlockSpec((CHUNK, 1024), bwd(ZC_V)),
                  pl.BlockSpec((CHUNK, GATE_PAD), bwd(0)),
                  pl.BlockSpec((1, GATE_PAD), lambda j: (0, 0)), st_spec, m_spec],
        out_specs=[pl.BlockSpec((CHUNK, 1024), fwd(0)), pl.BlockSpec((CHUNK, 1024), bwd(0)), st_spec, m_spec],
        out_shape=[jax.ShapeDtypeStruct((m, 1024), BF16), jax.ShapeDtypeStruct((m, 1024), BF16),
                   jax.ShapeDtypeStruct((2 * N_HEADS, CHUNK, 256), F32),
                   jax.ShapeDtypeStruct((2, 8, 128), F32)],
        scratch_shapes=[pltpu.VMEM((2 * N_HEADS, CHUNK, 256), F32), pltpu.VMEM((2, 8, 128), F32)],
        compiler_params=_cparams(("arbitrary",), 32),
        name="mlstm",
    )(kq, z, gates, kq, z, gates, bg, s0, m0)


def _filt_body(seq_len, ft_ref, w1_ref, b1_ref, f1_ref, w2_ref, b2_ref, f2_ref, w3_ref, b3_ref, rt_ref,
               kf_ref, l1_ref):
    tn = ft_ref.shape[1]
    reps = tn // 128
    col = lambda ref: jnp.tile(ref[...], (1, reps))
    h1 = jnp.sin(col(f1_ref) * (_dot3(w1_ref[...], ft_ref[...]) + col(b1_ref)))
    h2 = jnp.sin(col(f2_ref) * (_dot3(w2_ref[...], h1) + col(b2_ref)))
    h = _dot3(h2, w3_ref[...], _TN) + b3_ref[...]
    n = pl.program_id(0) * tn + lax.broadcasted_iota(I32, h.shape, 0)
    t01 = jnp.where(n <= seq_len, n, 2 * seq_len - n).astype(F32) / float(max(seq_len - 1, 1))
    h = jnp.where(n == seq_len, 0.0, h * jnp.exp(-t01 * rt_ref[...]))
    kf_ref[...] = h.astype(BF16)

    @pl.when(pl.program_id(0) == 0)
    def _():
        l1_ref[...] = jnp.zeros_like(l1_ref)

    l1_ref[...] += jnp.sum(jnp.abs(h), axis=0, keepdims=True)


def _filt_call(feats_t, w1t, b1, f1, w2t, b2, f2, w3, b3, rates, tn):
    n = feats_t.shape[1]
    nt = n // tn
    half = lambda i: (0, jnp.where(i >= nt // 2, 1, 0))
    c64 = lambda shape: pl.BlockSpec(shape, lambda i: (0, 0))
    return pl.pallas_call(
        functools.partial(_filt_body, n // 2),
        grid=(nt,),
        in_specs=[pl.BlockSpec((64, tn), lambda i: (0, i)),
                  c64((64, 64)), c64((64, 128)), c64((64, 128)), c64((64, 64)), c64((64, 128)), c64((64, 128)),
                  pl.BlockSpec((64, HY_W), half), pl.BlockSpec((1, HY_W), half), c64((1, HY_W))],
        out_specs=[pl.BlockSpec((tn, HY_W), lambda i: (i, 0)), pl.BlockSpec((1, HY_W), lambda i: (0, 0))],
        out_shape=[jax.ShapeDtypeStruct((n, HY_W), BF16), jax.ShapeDtypeStruct((1, HY_W), F32)],
        compiler_params=_cparams(("arbitrary",), 32),
        name="filt",
    )(feats_t, w1t, b1, f1, w2t, b2, f2, w3, b3, rates)


def _fft_consts(n1_rows):
    n = n1_rows * FFT_N2
    kv = n1_rows // 2 + 1
    k1 = np.arange(FFT_KP, dtype=np.float64)
    valid = (k1 < kv).astype(np.float64)
    n1 = np.arange(n1_rows, dtype=np.float64)
    th1 = 2.0 * np.pi * np.outer(k1, n1) / n1_rows
    f1 = np.concatenate([np.cos(th1) * valid[:, None], -np.sin(th1) * valid[:, None]], axis=0)
    n2 = np.arange(FFT_N2, dtype=np.float64)
    tht = 2.0 * np.pi * np.outer(k1, n2) / n
    rep = lambda a: jnp.broadcast_to(jnp.asarray(a, F32)[:, :, None], (FFT_KP, FFT_N2, 128))
    twr = rep(np.cos(tht) * valid[:, None])
    twi = rep(-np.sin(tht) * valid[:, None])
    th2 = 2.0 * np.pi * np.outer(n2, n2) / FFT_N2
    cs, sn = np.cos(th2), np.sin(th2)
    f2p = np.block([[cs, sn], [-sn, cs]])
    f2pc = np.block([[cs, -sn], [sn, cs]])
    wk = np.where((k1 == 0) | (k1 == kv - 1), 1.0, 2.0) * valid / n
    half = n1_rows // 2
    thi = 2.0 * np.pi * np.outer(n1[:half], k1) / n1_rows
    gc = np.cos(thi) * wk[None, :]
    gs = np.sin(thi) * wk[None, :]
    as_bf = lambda a: jnp.asarray(a, F32).astype(BF16)
    return dict(f1=as_bf(f1), twr=twr, twi=twi,
                f2p=as_bf(f2p), f2pc=as_bf(f2pc), gc=as_bf(gc), gs=as_bf(gs))


def _fft1_body(f_ref, x_ref, ar_ref, ai_ref):
    o = _dg(f_ref[...], x_ref[...])
    ar_ref[...] = o[:FFT_KP].astype(BF16)
    ai_ref[...] = o[FFT_KP:].astype(BF16)


def _fft1_call(f1, x2d, cb):
    k, cols = x2d.shape
    f1 = f1[:, :k]
    out = pl.BlockSpec((FFT_KP, cb), lambda i: (0, i))
    sh = jax.ShapeDtypeStruct((FFT_KP, cols), BF16)
    return pl.pallas_call(
        _fft1_body,
        grid=(cols // cb,),
        in_specs=[pl.BlockSpec((2 * FFT_KP, k), lambda i: (0, 0)), pl.BlockSpec((k, cb), lambda i: (0, i))],
        out_specs=[out, out],
        out_shape=[sh, sh],
        compiler_params=_cparams(("arbitrary",), 32),
        name="fft1",
    )(f1, x2d)


def _twiddled(ar_ref, ai_ref, twr_ref, twi_ref, reps):
    a_r = ar_ref[...].astype(F32)
    a_i = ai_ref[...].astype(F32)
    tr = jnp.tile(twr_ref[...], (1, reps))
    ti = jnp.tile(twi_ref[...], (1, reps))
    st = jnp.concatenate([a_r * tr - a_i * ti, a_r * ti + a_i * tr], axis=0).astype(BF16)
    return st, tr, ti


def _fft2_filt_body(ar_ref, ai_ref, twr_ref, twi_ref, f2p_ref, k_ref):
    st, _, _ = _twiddled(ar_ref, ai_ref, twr_ref, twi_ref, ar_ref.shape[-1] // 128)
    k_ref[...] = _dg(f2p_ref[...], st).astype(BF16)


def _fft2_conv_body(ar_ref, ai_ref, twr_ref, twi_ref, k_ref, f2p_ref, f2pc_ref, br_ref, bi_ref):
    st, tr, ti = _twiddled(ar_ref, ai_ref, twr_ref, twi_ref, ar_ref.shape[-1] // 128)
    x = _dg(f2p_ref[...], st)
    xr, xi = x[:FFT_N2], x[FFT_N2:]
    kr = k_ref[:FFT_N2, :].astype(F32)
    ki = k_ref[FFT_N2:, :].astype(F32)
    sy = jnp.concatenate([xr * kr - xi * ki, xr * ki + xi * kr], axis=0).astype(BF16)
    b = _dg(f2pc_ref[...], sy)
    b_r, b_i = b[:FFT_N2], b[FFT_N2:]
    br_ref[...] = (b_r * tr + b_i * ti).astype(BF16)
    bi_ref[...] = (b_i * tr - b_r * ti).astype(BF16)


def _fft2_specs(ch):
    blk = pl.BlockSpec((None, FFT_N2, ch), lambda i: (i, 0, 0))
    tw = pl.BlockSpec((None, FFT_N2, 128), lambda i: (i, 0, 0))
    mat = pl.BlockSpec((2 * FFT_N2, 2 * FFT_N2), lambda i: (0, 0))
    return blk, tw, mat


def _fft2_filt_call(ar, ai, cst):
    ch = ar.shape[-1]
    blk, tw, mat = _fft2_specs(ch)
    return pl.pallas_call(
        _fft2_filt_body,
        grid=(FFT_KP,),
        in_specs=[blk, blk, tw, tw, mat],
        out_specs=pl.BlockSpec((None, 2 * FFT_N2, ch), lambda i: (i, 0, 0)),
        out_shape=jax.ShapeDtypeStruct((FFT_KP, 2 * FFT_N2, ch), BF16),
        compiler_params=_cparams(("arbitrary",), 32),
        name="fft2_filt",
    )(ar, ai, cst["twr"], cst["twi"], cst["f2p"])


def _fft2_conv_call(ar, ai, khat, cst):
    ch = ar.shape[-1]
    blk, tw, mat = _fft2_specs(ch)
    sh = jax.ShapeDtypeStruct((FFT_KP, FFT_N2, ch), BF16)
    return pl.pallas_call(
        _fft2_conv_body,
        grid=(FFT_KP,),
        in_specs=[blk, blk, tw, tw, pl.BlockSpec((None, 2 * FFT_N2, ch), lambda i: (i, 0, 0)), mat, mat],
        out_specs=[blk, blk],
        out_shape=[sh, sh],
        compiler_params=_cparams(("arbitrary",), 32),
        name="fft2_conv",
    )(ar, ai, cst["twr"], cst["twi"], khat, cst["f2p"], cst["f2pc"])


def _ifft1_body(gc_ref, gs_ref, br_ref, bi_ref, x0_ref, xv_ref, il_ref, ds_ref, o_ref):
    y = _dg(gc_ref[...], br_ref[...]) - _dg(gs_ref[...], bi_ref[...])
    o_ref[...] = (x0_ref[...].astype(F32)
                  * (y * il_ref[...] + ds_ref[...] * xv_ref[...].astype(F32))).astype(BF16)


def _ifft1_call(cst, br2d, bi2d, x0_2d, xv_2d, il_t, ds_t, cb):
    rows, cols = x0_2d.shape
    g = pl.BlockSpec((rows, FFT_KP), lambda i: (0, 0))
    kb = pl.BlockSpec((FFT_KP, cb), lambda i: (0, i))
    xb = pl.BlockSpec((rows, cb), lambda i: (0, i))
    vb = pl.BlockSpec((1, cb), lambda i: (0, 0))
    return pl.pallas_call(
        _ifft1_body,
        grid=(cols // cb,),
        in_specs=[g, g, kb, kb, xb, xb, vb, vb],
        out_specs=xb,
        out_shape=jax.ShapeDtypeStruct((rows, cols), BF16),
        compiler_params=_cparams(("arbitrary",), 32),
        name="ifft1",
    )(cst["gc"], cst["gs"], br2d, bi2d, x0_2d, xv_2d, il_t, ds_t)


def _outproj_body(hf_ref, hb_ref, zo_ref, yh_ref, x_ref, erow_ref, ecol_ref, gh_ref, wa_ref, wb_ref,
                  g1_ref, gf_ref, sh_ref, sc_ref, wr_ref, x1_ref, h2_ref, s_ref):
    hs = hf_ref[...].astype(F32) + hb_ref[...].astype(F32)
    gh = gh_ref[...]
    parts = []
    for h in range(N_HEADS):
        hh = hs[:, h * 128:(h + 1) * 128]
        ms = jnp.mean(hh * hh, axis=-1, keepdims=True)
        parts.append(hh * lax.rsqrt(ms + EPS) * gh[:, h * 128:(h + 1) * 128])
    ym = jnp.concatenate(parts, axis=-1) * _sigmoid(zo_ref[...].astype(F32))
    y = _dg(ym.astype(BF16), wa_ref[...]) + _dg(yh_ref[...], wb_ref[...])
    rp = x_ref.shape[0] // GRID_W
    erow8 = erow_ref[...]
    erow = erow8[0:rp, :]
    for q in range(1, 8 // rp):
        erow = jnp.where(pl.program_id(0) % (8 // rp) == q, erow8[q * rp:(q + 1) * rp, :], erow)
    x1 = _add_pos(x_ref[...], erow, ecol_ref[...]) + g1_ref[...] * y
    x1_ref[...] = x1
    h2 = _norm_mod(x1, gf_ref[...], sh_ref[...], sc_ref[...])
    half = h2.shape[1] // 2
    _store_slabs(h2_ref, _pack_pair(h2[:, :half], h2[:, half:]))
    s_ref[...] = _sigmoid(_dot3(wr_ref[...], h2, _NT))


def _outproj_call(hf, hb, z, yh, x, erow, ecol, gh, wa, wb, g1, gf, sh2, sc2, wrt, tm):
    m, d = x.shape
    row = lambda cb: pl.BlockSpec((tm, 1024), lambda i: (i, cb))
    vec = lambda n: pl.BlockSpec((1, n), lambda i: (0, 0))
    full = pl.BlockSpec((tm, d), lambda i: (i, 0))
    return pl.pallas_call(
        _outproj_body,
        grid=(m // tm,),
        in_specs=[row(0), row(0), row(ZC_O), row(0), full,
                  pl.BlockSpec((8, d // 2), lambda i: (i * (tm // GRID_W) // 8, 0)),
                  pl.BlockSpec((GRID_W, d // 2), lambda i: (0, 0)),
                  vec(MV_W),
                  pl.BlockSpec((MV_W, d), lambda i: (0, 0)), pl.BlockSpec((HY_W, d), lambda i: (0, 0)),
                  vec(d), vec(d), vec(d), vec(d),
                  pl.BlockSpec((N_EXPERTS, d), lambda i: (0, 0))],
        out_specs=[full, pl.BlockSpec((tm * SLAB, SLAB_W), lambda i: (i, 0)),
                   pl.BlockSpec((N_EXPERTS, tm), lambda i: (0, i))],
        out_shape=[jax.ShapeDtypeStruct((m, d), F32), jax.ShapeDtypeStruct((m * SLAB, SLAB_W), U32),
                   jax.ShapeDtypeStruct((N_EXPERTS, m), F32)],
        compiler_params=_cparams(("arbitrary",), 56),
        name="outproj",
    )(hf, hb, z, yh, x, erow, ecol, gh, wa, wb, g1, gf, sh2, sc2, wrt)


def _first_max(x, idx, sentinel):
    m = jnp.max(x, axis=0, keepdims=True)
    return m, jnp.min(jnp.where(x == m, idx, sentinel), axis=0, keepdims=True)


def _route_body(s_ref, b_ref, e_ref, w_ref, r_ref, cnt_ref, u_scr, run_scr):
    i = pl.program_id(0)
    tt = s_ref.shape[1]

    @pl.when(i == 0)
    def _():
        rr = lax.broadcasted_iota(I32, (tt, tt), 0)
        cc = lax.broadcasted_iota(I32, (tt, tt), 1)
        u_scr[...] = jnp.where(rr < cc, 1.0, 0.0).astype(BF16)
        run_scr[...] = jnp.zeros_like(run_scr)

    s = s_ref[...]
    sel = s + b_ref[...][:, 0:1]
    sub8 = lax.broadcasted_iota(I32, (E_PER_GROUP, tt), 0).astype(F32)
    gs = jnp.zeros((N_GROUPS, tt), F32)
    for g in range(N_GROUPS):
        grp = sel[g * E_PER_GROUP:(g + 1) * E_PER_GROUP, :]
        m1, i1 = _first_max(grp, sub8, float(E_PER_GROUP))
        m2 = jnp.max(jnp.where(sub8 == i1, -jnp.inf, grp), axis=0, keepdims=True)
        gs = jnp.where(sub8 == g, m1 + m2, gs)
    gmask = jnp.zeros((N_GROUPS, tt), F32)
    for _ in range(TOPK_GROUPS):
        _, ig = _first_max(gs, sub8, float(N_GROUPS))
        hit = sub8 == ig
        gmask = jnp.where(hit, 1.0, gmask)
        gs = jnp.where(hit, -jnp.inf, gs)
    masked = jnp.concatenate(
        [jnp.where(jnp.broadcast_to(gmask[g:g + 1, :], (E_PER_GROUP, tt)) > 0.5,
                   sel[g * E_PER_GROUP:(g + 1) * E_PER_GROUP, :], -jnp.inf) for g in range(N_GROUPS)], axis=0)
    sub64 = lax.broadcasted_iota(I32, (N_EXPERTS, tt), 0).astype(F32)
    oh = jnp.zeros((N_EXPERTS, tt), F32)
    eks, wks = [], []
    for _ in range(TOP_K):
        _, ie = _first_max(masked, sub64, float(N_EXPERTS))
        hit = sub64 == ie
        wks.append(jnp.sum(jnp.where(hit, s, 0.0), axis=0, keepdims=True))
        eks.append(ie)
        masked = jnp.where(hit, -jnp.inf, masked)
        oh = jnp.where(hit, 1.0, oh)
    wsum = wks[0]
    for k in range(1, TOP_K):
        wsum = wsum + wks[k]
    run = run_scr[...]
    rank_t = _dg(oh.astype(BF16), u_scr[...]) + jnp.tile(run, (1, tt // 128))
    for k in range(TOP_K):
        e_ref[k:k + 1, :] = eks[k].astype(I32)
        w_ref[k:k + 1, :] = wks[k] / wsum * ROUTE_SCALE
        r_ref[k:k + 1, :] = jnp.sum(jnp.where(sub64 == eks[k], rank_t, 0.0), axis=0, keepdims=True).astype(I32)
    run_new = run + jnp.sum(oh, axis=1, keepdims=True)
    run_scr[...] = run_new
    cnt_ref[...] = run_new.astype(I32)


def _route_call(s_t, b_col, tt):
    m = s_t.shape[1]
    out = pl.BlockSpec((TOP_K, tt), lambda i: (0, i))
    return pl.pallas_call(
        _route_body,
        grid=(m // tt,),
        in_specs=[pl.BlockSpec((N_EXPERTS, tt), lambda i: (0, i)),
                  pl.BlockSpec((N_EXPERTS, 128), lambda i: (0, 0))],
        out_specs=[out, out, out, pl.BlockSpec((N_EXPERTS, 128), lambda i: (0, 0))],
        out_shape=[jax.ShapeDtypeStruct((TOP_K, m), I32), jax.ShapeDtypeStruct((TOP_K, m), F32),
                   jax.ShapeDtypeStruct((TOP_K, m), I32), jax.ShapeDtypeStruct((N_EXPERTS, 128), I32)],
        scratch_shapes=[pltpu.VMEM((tt, tt), BF16), pltpu.VMEM((N_EXPERTS, 128), F32)],
        compiler_params=_cparams(("arbitrary",), 32),
        name="route",
    )(s_t, b_col)


def _posk_body(pst_ref, e_ref, r_ref, p_ref):
    e = e_ref[...]
    acc = r_ref[...]
    for x in range(N_EXPERTS):
        acc = acc + jnp.where(e == x, pst_ref[x], 0)
    p_ref[...] = acc


def _posk_call(pstart, eidx, rank):
    k, m = eidx.shape
    tt = min(m, 2048)
    blk = pl.BlockSpec((k, tt), lambda i, pst: (0, i))
    return pl.pallas_call(
        _posk_body,
        grid_spec=pltpu.PrefetchScalarGridSpec(num_scalar_prefetch=1, grid=(m // tt,),
                                               in_specs=[blk, blk], out_specs=blk),
        out_shape=jax.ShapeDtypeStruct((k, m), I32),
        compiler_params=_cparams(("arbitrary",), 32),
        name="posk",
    )(pstart, eidx, rank)


def _slab(ref, r):
    return ref.at[pl.ds(pl.multiple_of(r * SLAB, SLAB), SLAB), :]


def _ffn_packed(x_ref, rows, w1, w3, w2):
    half = w1.shape[0] // 2
    xa, xb = _unpack_pair(_load_slabs(x_ref, rows))
    xa = xa.astype(BF16)
    xb = xb.astype(BF16)
    h1 = _dg(xa, w1[0:half, :]) + _dg(xb, w1[half:, :])
    h3 = _dg(xa, w3[0:half, :]) + _dg(xb, w3[half:, :])
    return _dg((_silu(h1) * h3).astype(BF16), w2[...])


def _dispatch_body(cnt_ref, pst_ref, pcn_ref, h2_ref, pos_ref, w1_ref, w3_ref, w2_ref,
                   xs_ref, sh_ref, zrow, sem):
    i = pl.program_id(0)
    td = h2_ref.shape[0] // SLAB

    def row_copy(t, dst):
        return pltpu.make_async_copy(_slab(h2_ref, t), _slab(xs_ref, dst), sem)

    def issue(t, carry):
        for k in range(TOP_K):
            row_copy(t, pos_ref[k, t]).start(priority=k % 2)
        return carry

    lax.fori_loop(0, td, issue, 0)

    sh_ref[...] = _ffn_packed(h2_ref, td, w1_ref, w3_ref, w2_ref).astype(BF16)

    def drain(t, carry):
        for k in range(TOP_K):
            row_copy(0, 0).wait()
        return carry

    lax.fori_loop(0, td, drain, 0)

    @pl.when(i == pl.num_programs(0) - 1)
    def _():
        zrow[...] = jnp.zeros_like(zrow)

        def zero_copy(dst):
            return pltpu.make_async_copy(zrow, _slab(xs_ref, dst), sem)

        def per_expert(e, carry):
            base = pst_ref[e]
            lax.fori_loop(cnt_ref[e], pcn_ref[e], lambda rr, cc: (zero_copy(base + rr).start(), cc)[1], 0)
            lax.fori_loop(cnt_ref[e], pcn_ref[e], lambda rr, cc: (zero_copy(0).wait(), cc)[1], 0)
            return carry

        lax.fori_loop(0, N_EXPERTS, per_expert, 0)


def _dispatch_call(cnt, pstart, pcnt, h2s, pos, w1s, w3s, w2s, rows, td):
    m = h2s.shape[0] // SLAB
    d, ds = w1s.shape
    return pl.pallas_call(
        _dispatch_body,
        grid_spec=pltpu.PrefetchScalarGridSpec(
            num_scalar_prefetch=3, grid=(m // td,),
            in_specs=[pl.BlockSpec((td * SLAB, SLAB_W), lambda i, *_: (i, 0)),
                      pl.BlockSpec((TOP_K, td), lambda i, *_: (0, i), memory_space=pltpu.SMEM),
                      pl.BlockSpec((d, ds), lambda i, *_: (0, 0)), pl.BlockSpec((d, ds), lambda i, *_: (0, 0)),
                      pl.BlockSpec((ds, d), lambda i, *_: (0, 0))],
            out_specs=[pl.BlockSpec(memory_space=pl.ANY), pl.BlockSpec((td, d), lambda i, *_: (i, 0))],
            scratch_shapes=[pltpu.VMEM((SLAB, SLAB_W), U32), pltpu.SemaphoreType.DMA(())]),
        out_shape=[jax.ShapeDtypeStruct((rows * SLAB, SLAB_W), U32), jax.ShapeDtypeStruct((m, d), BF16)],
        compiler_params=_cparams(("arbitrary",), 40),
        name="dispatch",
    )(cnt, pstart, pcnt, h2s, pos, w1s, w3s, w2s)


def _moe_body(te_ref, nu_ref, x_ref, w1_ref, w3_ref, w2_ref, y_ref, w1b, w3b, w2b):
    i = pl.program_id(0)
    used = i < nu_ref[0]
    prev = te_ref[jnp.maximum(i - 1, 0)]

    @pl.when(jnp.logical_and(used, jnp.logical_or(i == 0, te_ref[i] != prev)))
    def _():
        w1b[...] = w1_ref[...].astype(BF16)
        w3b[...] = w3_ref[...].astype(BF16)
        w2b[...] = w2_ref[...].astype(BF16)

    @pl.when(used)
    def _():
        half = w1b.shape[0] // 2
        y = _ffn_packed(x_ref, MOE_ROWS, w1b, w3b, w2b)
        _store_slabs(y_ref, _pack_pair(y[:, :half], y[:, half:]))


def _moe_call(tile_e, n_used, xs, w1, w3, w2):
    rows = xs.shape[0] // SLAB
    nt = rows // MOE_ROWS
    d, de = w1.shape[-2:]
    rmap = lambda i, te, nu: (jnp.minimum(i, nu[0] - 1), 0)
    wmap = lambda i, te, nu: (te[i], 0, 0)
    return pl.pallas_call(
        _moe_body,
        grid_spec=pltpu.PrefetchScalarGridSpec(
            num_scalar_prefetch=2, grid=(nt,),
            in_specs=[pl.BlockSpec((MOE_ROWS * SLAB, SLAB_W), rmap),
                      pl.BlockSpec((None, d, de), wmap), pl.BlockSpec((None, d, de), wmap),
                      pl.BlockSpec((None, de, d), wmap)],
            out_specs=pl.BlockSpec((MOE_ROWS * SLAB, SLAB_W), rmap),
            scratch_shapes=[pltpu.VMEM((d, de), BF16), pltpu.VMEM((d, de), BF16), pltpu.VMEM((de, d), BF16)]),
        out_shape=jax.ShapeDtypeStruct((rows * SLAB, SLAB_W), U32),
        compiler_params=_cparams(("arbitrary",), 56),
        name="moe",
    )(tile_e, n_used, xs, w1, w3, w2)


def _final_body(x1_ref, sh_ref, pos_ref, posn_ref, wt_ref, ys_ref, g2_ref, gn_ref, o_ref, ybuf, sems):
    i = pl.program_id(0)
    n = pl.num_programs(0)
    tf, d = x1_ref.shape
    half = d // 2
    slot = i % 2
    slot_slabs = TOP_K * tf

    def row_copy(p_ref, s, k, t):
        return pltpu.make_async_copy(_slab(ys_ref, p_ref[k, t]), _slab(ybuf, s * slot_slabs + k * tf + t),
                                     sems.at[s])

    def issue(p_ref, s):
        def body(t, carry):
            for k in range(TOP_K):
                row_copy(p_ref, s, k, t).start(priority=k % 2)
            return carry

        lax.fori_loop(0, tf, body, 0)

    @pl.when(i == 0)
    def _():
        issue(pos_ref, 0)

    @pl.when(i + 1 < n)
    def _():
        issue(posn_ref, 1 - slot)

    def drain(t, carry):
        for k in range(TOP_K):
            pltpu.make_async_copy(_slab(ys_ref, 0), _slab(ybuf, 0), sems.at[slot]).wait()
        return carry

    lax.fori_loop(0, tf, drain, 0)

    wt = jnp.concatenate([jnp.transpose(jnp.concatenate(
        [wt_ref[:, c * 128:(c + 1) * 128], jnp.zeros((128 - TOP_K, 128), F32)], axis=0)) for c in range(tf // 128)],
        axis=0)
    shared = sh_ref[...].astype(F32)
    acc_a = shared[:, :half]
    acc_b = shared[:, half:]
    base = slot * slot_slabs * SLAB
    for k in range(TOP_K):
        ya, yb = _unpack_pair(_load_slabs(ybuf, tf, base=base + k * tf * SLAB))
        acc_a = acc_a + wt[:, k:k + 1] * ya
        acc_b = acc_b + wt[:, k:k + 1] * yb
    xo = x1_ref[...] + g2_ref[...] * jnp.concatenate([acc_a, acc_b], axis=1)
    ms = jnp.mean(xo * xo, axis=-1, keepdims=True)
    o_ref[...] = xo * lax.rsqrt(ms + EPS) * gn_ref[...]


def _final_call(x1, sh, pos, wts, ys, g2, gn, tf):
    m, d = x1.shape
    nt = m // tf
    full = pl.BlockSpec((tf, d), lambda i: (i, 0))
    vec = pl.BlockSpec((1, d), lambda i: (0, 0))
    return pl.pallas_call(
        _final_body,
        grid=(nt,),
        in_specs=[full, full,
                  pl.BlockSpec((TOP_K, tf), lambda i: (0, i), memory_space=pltpu.SMEM),
                  pl.BlockSpec((TOP_K, tf), lambda i: (0, jnp.minimum(i + 1, nt - 1)), memory_space=pltpu.SMEM),
                  pl.BlockSpec((TOP_K, tf), lambda i: (0, i)),
                  pl.BlockSpec(memory_space=pl.ANY), vec, vec],
        out_specs=full,
        out_shape=jax.ShapeDtypeStruct((m, d), F32),
        scratch_shapes=[pltpu.VMEM((2 * TOP_K * tf * SLAB, SLAB_W), U32), pltpu.SemaphoreType.DMA((2,))],
        compiler_params=_cparams(("arbitrary",), 48),
        name="final",
    )(x1, sh, pos, pos, wts, ys, g2, gn)


def _pos_tables(n_tokens):
    rows = n_tokens // GRID_W
    quarter = D_MODEL // 4
    omega = 1.0 / (10000.0 ** (jnp.arange(quarter, dtype=F32) / quarter))

    def emb1d(pos):
        ang = pos[:, None] * omega[None]
        return jnp.concatenate([jnp.sin(ang), jnp.cos(ang)], axis=-1)

    return emb1d(jnp.arange(rows, dtype=F32)), emb1d(jnp.arange(GRID_W, dtype=F32))


def _filter_feats(L):
    n = jnp.arange(2 * L, dtype=I32)
    t = jnp.where(n <= L, n, 2 * L - n).astype(F32)
    t01 = t / max(L - 1, 1)
    w = 2.0 * math.pi * t / L
    bands = jnp.linspace(1e-4, FILT_BANDS - 1, FILT_BANDS, dtype=F32)
    feats = jnp.concatenate([t01[None, :], jnp.cos(bands[:, None] * w[None, :]), -jnp.sin(bands[:, None] * w[None, :]),
                             jnp.zeros((64 - 33, 2 * L), F32)], axis=0)
    return feats


def _pad_rows(a, rows):
    return jnp.concatenate([a, jnp.zeros((rows - a.shape[0],) + a.shape[1:], a.dtype)], axis=0)


def _layer(x, c, ctx, c_ctx, w_ada, b_ada, g_mix, g_ffn, w_in, b_gates, conv_k_w, conv_k_b,
           conv_q_w, conv_q_b, g_head, conv_hy_w, conv_hy_b, filt_w1, filt_b1, filt_freq1,
           filt_w2, filt_b2, filt_freq2, filt_w3, filt_b3, hy_dskip, w_out, w_router, b_router,
           w1_e, w3_e, w2_e, w1_s, w3_s, w2_s, g_final):
    L, d = x.shape
    lc = ctx.shape[0]
    row = lambda v: v.reshape(1, -1)

    cc = _pad_rows(jnp.stack([c, c_ctx], axis=0), 8)
    mods = _mod_call(cc, w_ada, row(b_ada))
    sh1, sc1, g1, sh2, sc2, g2 = [mods[0:1, k * d:(k + 1) * d] for k in range(6)]
    csh1, csc1 = mods[1:2, 0:d], mods[1:2, d:2 * d]

    w_r = jnp.concatenate([w_in[:, OFF_K:OFF_V], w_in[:, OFF_Q:OFF_O], w_in[:, OFF_V:OFF_G],
                           w_in[:, OFF_O:]], axis=1).astype(BF16)
    w_g = jnp.concatenate([w_in[:, OFF_G:OFF_Q], jnp.zeros((d, GATE_PAD - 4 * N_HEADS), F32)], axis=1)
    bg = jnp.concatenate([b_gates, jnp.zeros((GATE_PAD - 4 * N_HEADS,), F32)]).reshape(1, GATE_PAD)
    e_row, e_col = _pos_tables(L)
    conv_w = jnp.concatenate([conv_k_w, conv_q_w], axis=1)
    conv_b = jnp.concatenate([conv_k_b, conv_q_b]).reshape(1, -1)
    conv_s = jnp.concatenate([jnp.ones((QK_W,), F32), jnp.full((QK_W,), QK_HEAD ** -0.5, F32)]).reshape(1, -1)

    z_c, gt_c = _inproj_call(ctx, jnp.zeros((8, d // 2), F32), e_col, row(g_mix), csh1, csc1, w_r, w_g,
                             use_pos=False, tm=min(lc, 256))
    kq_c = _conv_kq_call(z_c, conv_w, conv_b, conv_s, tm=min(lc, 256))
    s0 = jnp.zeros((2 * N_HEADS, CHUNK, 256), F32)
    m0 = jnp.zeros((2, 8, 128), F32)
    _, _, s_ctx, m_ctx = _mlstm_call(kq_c, z_c, gt_c, bg, s0, m0)

    z, gates = _inproj_call(x, e_row, e_col, row(g_mix), sh1, sc1, w_r, w_g, use_pos=True, tm=min(L, 1024))
    kq = _conv_kq_call(z, conv_w, conv_b, conv_s, tm=min(L, 512))
    x0c, xv = _conv_hy_call(z, conv_hy_w, row(conv_hy_b), tm=min(L, 512))
    hf, hb, _, _ = _mlstm_call(kq, z, gates, bg, s_ctx, m_ctx)

    n1 = 2 * L // FFT_N2
    cst = _fft_consts(n1)
    rates = jnp.linspace(-math.log(DECAY_TARGET) / SLOW_DECAY_PCT, -math.log(DECAY_TARGET) / FAST_DECAY_PCT,
                         HY_W, dtype=F32).reshape(1, -1)
    w1t = jnp.transpose(_pad_rows(filt_w1, 64))
    colrep = lambda v: jnp.broadcast_to(v.reshape(-1, 1), (v.shape[0], 128))
    kf, l1 = _filt_call(_filter_feats(L), w1t, colrep(filt_b1), colrep(filt_freq1), jnp.transpose(filt_w2),
                        colrep(filt_b2), colrep(filt_freq2), filt_w3, row(filt_b3), rates, tn=min(L, 1024))
    cols = FFT_N2 * HY_W
    cb = 2048
    kar, kai = _fft1_call(cst["f1"], kf.reshape(n1, cols), cb)
    khat = _fft2_filt_call(kar.reshape(FFT_KP, FFT_N2, HY_W), kai.reshape(FFT_KP, FFT_N2, HY_W), cst)
    uar, uai = _fft1_call(cst["f1"], xv.reshape(n1 // 2, cols), cb)
    br, bi = _fft2_conv_call(uar.reshape(FFT_KP, FFT_N2, HY_W), uai.reshape(FFT_KP, FFT_N2, HY_W), khat, cst)
    reps = cb // HY_W
    il_t = jnp.tile(1.0 / l1, (1, reps))
    ds_t = jnp.tile(row(hy_dskip), (1, reps))
    yh = _ifft1_call(cst, br.reshape(FFT_KP, cols), bi.reshape(FFT_KP, cols),
                     x0c.reshape(n1 // 2, cols), xv.reshape(n1 // 2, cols), il_t, ds_t, cb).reshape(L, HY_W)

    wo = w_out.astype(BF16)
    x1, h2s, s_t = _outproj_call(hf, hb, z, yh, x, e_row, e_col, row(g_head), wo[:MV_W], wo[MV_W:],
                                g1, row(g_ffn), sh2, sc2, jnp.transpose(w_router), tm=min(L, 256))

    b_col = jnp.broadcast_to(b_router.reshape(N_EXPERTS, 1), (N_EXPERTS, 128))
    eidx, wts, rank, cnt2 = _route_call(s_t, b_col, tt=min(L, 1024))
    cnt = cnt2[:, 0]
    pcnt = (cnt + MOE_ROWS - 1) // MOE_ROWS * MOE_ROWS
    pend = jnp.cumsum(pcnt)
    pstart = pend - pcnt
    rows = L * TOP_K + N_EXPERTS * MOE_ROWS
    nt = rows // MOE_ROWS
    tile_row = jnp.arange(nt, dtype=I32) * MOE_ROWS
    tile_e = jnp.minimum(jnp.sum((pend[None, :] <= tile_row[:, None]).astype(I32), axis=1), N_EXPERTS - 1)
    n_used = (pend[-1] // MOE_ROWS).astype(I32).reshape(1)
    pos = _posk_call(pstart.astype(I32), eidx, rank)

    xs, sh = _dispatch_call(cnt, pstart.astype(I32), pcnt.astype(I32), h2s, pos, w1_s.astype(BF16),
                            w3_s.astype(BF16), w2_s.astype(BF16), rows, td=min(L, 256))
    ys = _moe_call(tile_e, n_used, xs, w1_e, w3_e, w2_e)
    return _final_call(x1, sh, pos, wts, ys, g2, row(g_final), tf=min(L, 256))


def kernel(x, c, ctx, c_ctx, w_ada, b_ada, g_mix, g_ffn, w_in, b_gates, conv_k_w, conv_k_b, conv_q_w,
           conv_q_b, g_head, conv_hy_w, conv_hy_b, filt_w1, filt_b1, filt_freq1, filt_w2, filt_b2,
           filt_freq2, filt_w3, filt_b3, hy_dskip, w_out, w_router, b_router, w1_e, w3_e, w2_e,
           w1_s, w3_s, w2_s, g_final):
    assert x.shape[0] == 1 and w_ada.shape[0] == 1, "one batch element, one layer"
    out = _layer(x[0], c[0], ctx[0], c_ctx, w_ada[0], b_ada[0], g_mix[0], g_ffn[0], w_in[0], b_gates[0],
                 conv_k_w[0], conv_k_b[0], conv_q_w[0], conv_q_b[0], g_head[0], conv_hy_w[0], conv_hy_b[0],
                 filt_w1[0], filt_b1[0], filt_freq1[0], filt_w2[0], filt_b2[0], filt_freq2[0], filt_w3[0],
                 filt_b3[0], hy_dskip[0], w_out[0], w_router[0], b_router[0], w1_e[0], w3_e[0], w2_e[0],
                 w1_s[0], w3_s[0], w2_s[0], g_final)
    return out[None]
```

```python
import functools
import math

import numpy as np
import jax
import jax.numpy as jnp
from jax import lax
from jax.experimental import pallas as pl
from jax.experimental.pallas import tpu as pltpu

F32 = jnp.float32
BF16 = jnp.bfloat16
I32 = jnp.int32
U32 = jnp.uint32

D_MODEL = 2048
GRID_W = 64
N_HEADS = 8
QK_HEAD = 64
V_HEAD = 128
QK_W = N_HEADS * QK_HEAD
MV_W = N_HEADS * V_HEAD
HY_W = D_MODEL - MV_W
CHUNK = 128
FILT_BANDS = 16
FILT_HIDDEN = 64
DECAY_TARGET = 1e-2
FAST_DECAY_PCT = 0.3
SLOW_DECAY_PCT = 1.5
N_EXPERTS = 64
N_GROUPS = 8
E_PER_GROUP = 8
TOPK_GROUPS = 4
TOP_K = 8
D_EXPERT = 512
ROUTE_SCALE = 2.5
EPS = 1e-6
OFF_K = 0
OFF_V = OFF_K + QK_W
OFF_G = OFF_V + MV_W
OFF_Q = OFF_G + 4 * N_HEADS
OFF_O = OFF_Q + QK_W
OFF_HY = OFF_O + MV_W

ZC_KQ, ZC_V, ZC_O, ZC_X0, ZC_X1, ZC_HV = 0, 1, 2, 3, 4, 5
Z_COLS = 6 * 1024
GATE_PAD = 128

NEG = -1e30
MIB = 1024 * 1024

FFT_N2 = 128
FFT_KP = 144

MOE_ROWS = 256


def _cparams(sem, vmem_mb, flags=None):
    return pltpu.CompilerParams(dimension_semantics=sem, vmem_limit_bytes=vmem_mb * MIB, flags=flags)


def _split2(x):
    hi = x.astype(BF16)
    lo = (x - hi.astype(F32)).astype(BF16)
    return hi, lo


_NN = (((1,), (0,)), ((), ()))
_NT = (((1,), (1,)), ((), ()))
_TN = (((0,), (0,)), ((), ()))


def _dg(a, b, dims=_NN):
    return lax.dot_general(a, b, dims, preferred_element_type=F32)


def _dot3(a, b, dims=_NN):
    ah, al = _split2(a)
    bh, bl = _split2(b)
    return _dg(ah, bh, dims) + _dg(al, bh, dims) + _dg(ah, bl, dims)


def _sigmoid(x):
    return 1.0 / (1.0 + jnp.exp(-x))


def _silu(x):
    return x * _sigmoid(x)


def _pack_pair(a, b):
    hi = lax.bitcast_convert_type(a.astype(BF16).astype(F32), U32)
    lo = lax.bitcast_convert_type(b.astype(BF16).astype(F32), U32)
    return hi | (lo >> 16)


def _unpack_pair(w):
    a = lax.bitcast_convert_type(w & jnp.uint32(0xFFFF0000), F32)
    b = lax.bitcast_convert_type(w << 16, F32)
    return a, b


SLAB = 8
SLAB_W = 128


def _store_slabs(ref, w, base=0):
    r = w.shape[0]
    for j in range(SLAB):
        ref[pl.ds(base + j, r, stride=SLAB), :] = w[:, j * SLAB_W:(j + 1) * SLAB_W]


def _load_slabs(ref, r, base=0):
    return jnp.concatenate([ref[pl.ds(base + j, r, stride=SLAB), :] for j in range(SLAB)], axis=1)


def _norm_mod(x, g, sh, sc):
    ms = jnp.mean(x * x, axis=-1, keepdims=True)
    return (x * lax.rsqrt(ms + EPS) * g) * (1.0 + sc) + sh


def _add_pos(x, erow, ecol):
    tm, d = x.shape
    half = d // 2
    parts = []
    for r in range(tm // GRID_W):
        xs = x[r * GRID_W:(r + 1) * GRID_W, :]
        parts.append(jnp.concatenate([xs[:, :half] + erow[r:r + 1, :], xs[:, half:] + ecol], axis=-1))
    return parts[0] if len(parts) == 1 else jnp.concatenate(parts, axis=0)


def _mod_body(cc_ref, w_ref, b_ref, o_ref):
    o_ref[...] = _dot3(_silu(cc_ref[...]), w_ref[...]) + b_ref[...]


def _mod_call(cc, w, b):
    d, n = w.shape
    tn = 1024
    return pl.pallas_call(
        _mod_body,
        grid=(n // tn,),
        in_specs=[pl.BlockSpec((8, d), lambda j: (0, 0)),
                  pl.BlockSpec((d, tn), lambda j: (0, j)),
                  pl.BlockSpec((1, tn), lambda j: (0, j))],
        out_specs=pl.BlockSpec((8, tn), lambda j: (0, j)),
        out_shape=jax.ShapeDtypeStruct((8, n), F32),
        compiler_params=_cparams(("arbitrary",), 40),
        name="mod",
    )(cc, w, b)


def _wprep_body(w_ref, wr_ref, wg_ref):
    w = w_ref[...]
    wr_ref[...] = jnp.concatenate([w[:, OFF_K:OFF_V], w[:, OFF_Q:OFF_O], w[:, OFF_V:OFF_G], w[:, OFF_O:]],
                                  axis=1).astype(BF16)
    wg_ref[...] = jnp.concatenate([w[:, OFF_G:OFF_Q], jnp.zeros((w.shape[0], GATE_PAD - 4 * N_HEADS), F32)], axis=1)


def _wprep_call(w_in):
    d, n = w_in.shape
    tr = 128
    return pl.pallas_call(
        _wprep_body,
        grid=(d // tr,),
        in_specs=[pl.BlockSpec((tr, n), lambda i: (i, 0))],
        out_specs=[pl.BlockSpec((tr, Z_COLS), lambda i: (i, 0)), pl.BlockSpec((tr, GATE_PAD), lambda i: (i, 0))],
        out_shape=[jax.ShapeDtypeStruct((d, Z_COLS), BF16), jax.ShapeDtypeStruct((d, GATE_PAD), F32)],
        compiler_params=_cparams(("arbitrary",), 32),
        name="wprep",
    )(w_in)


def _inproj_body(use_pos, x_ref, erow_ref, ecol_ref, gm_ref, sh_ref, sc_ref, w_ref, wg_ref,
                 z_ref, g_ref, h_scr):
    @pl.when(pl.program_id(1) == 0)
    def _():
        x = x_ref[...]
        if use_pos:
            x = _add_pos(x, erow_ref[...], ecol_ref[...])
        h = _norm_mod(x, gm_ref[...], sh_ref[...], sc_ref[...])
        h_scr[...] = h.astype(BF16)
        g_ref[...] = _dot3(h, wg_ref[...])

    z_ref[...] = jnp.dot(h_scr[...], w_ref[...], preferred_element_type=F32).astype(BF16)


def _inproj_call(x, erow, ecol, gm, sh, sc, w, wg, use_pos, tm):
    m, d = x.shape
    tn = 1024
    er = tm // GRID_W if use_pos else erow.shape[0]
    row_map = (lambda i, j: (i, 0)) if use_pos else (lambda i, j: (0, 0))
    return pl.pallas_call(
        functools.partial(_inproj_body, use_pos),
        grid=(m // tm, Z_COLS // tn),
        in_specs=[pl.BlockSpec((tm, d), lambda i, j: (i, 0)),
                  pl.BlockSpec((er, d // 2), row_map),
                  pl.BlockSpec((GRID_W, d // 2), lambda i, j: (0, 0)),
                  pl.BlockSpec((1, d), lambda i, j: (0, 0)),
                  pl.BlockSpec((1, d), lambda i, j: (0, 0)),
                  pl.BlockSpec((1, d), lambda i, j: (0, 0)),
                  pl.BlockSpec((d, tn), lambda i, j: (0, j)),
                  pl.BlockSpec((d, GATE_PAD), lambda i, j: (0, 0))],
        out_specs=[pl.BlockSpec((tm, tn), lambda i, j: (i, j)),
                   pl.BlockSpec((tm, GATE_PAD), lambda i, j: (i, 0))],
        out_shape=[jax.ShapeDtypeStruct((m, Z_COLS), BF16),
                   jax.ShapeDtypeStruct((m, GATE_PAD), F32)],
        scratch_shapes=[pltpu.VMEM((tm, d), BF16)],
        compiler_params=_cparams(("arbitrary", "arbitrary"), 48),
        name="inproj",
    )(x, erow, ecol, gm, sh, sc, w, wg)


def _conv3(zc, zp, zn, w, b, first, last):
    tm = zc.shape[0]
    row = lax.broadcasted_iota(I32, zc.shape, 0)
    prev_row = jnp.where(first, 0.0, zp[7:8, :])
    next_row = jnp.where(last, 0.0, zn[0:1, :])
    xm = jnp.where(row == 0, prev_row, pltpu.roll(zc, 1, 0))
    xp = jnp.where(row == tm - 1, next_row, pltpu.roll(zc, tm - 1, 0))
    return xm * w[0:1, :] + zc * w[1:2, :] + xp * w[2:3, :] + b


def _conv_kq_body(zc_ref, zp_ref, zn_ref, w_ref, b_ref, s_ref, o_ref):
    i = pl.program_id(0)
    u = _conv3(zc_ref[...].astype(F32), zp_ref[...].astype(F32), zn_ref[...].astype(F32),
               w_ref[...], b_ref[...], i == 0, i == pl.num_programs(0) - 1)
    o_ref[...] = (_silu(u) * s_ref[...]).astype(BF16)


def _halo_specs(tm, m, cb):
    nb8 = m // 8
    return [pl.BlockSpec((tm, 1024), lambda i: (i, cb)),
            pl.BlockSpec((8, 1024), lambda i: (jnp.maximum(i * (tm // 8) - 1, 0), cb)),
            pl.BlockSpec((8, 1024), lambda i: (jnp.minimum((i + 1) * (tm // 8), nb8 - 1), cb))]


def _conv_kq_call(z, w, b, s, tm):
    m = z.shape[0]
    vec = pl.BlockSpec((1, 1024), lambda i: (0, 0))
    return pl.pallas_call(
        _conv_kq_body,
        grid=(m // tm,),
        in_specs=_halo_specs(tm, m, ZC_KQ) + [pl.BlockSpec((3, 1024), lambda i: (0, 0)), vec, vec],
        out_specs=pl.BlockSpec((tm, 1024), lambda i: (i, 0)),
        out_shape=jax.ShapeDtypeStruct((m, 1024), BF16),
        compiler_params=_cparams(("arbitrary",), 32),
        name="conv_kq",
    )(z, z, z, w, b, s)


def _conv_hy_body(ac_ref, ap_ref, an_ref, bc_ref, bp_ref, bn_ref, cc_ref, cp_ref, cn_ref,
                  w_ref, b_ref, x0_ref, xv_ref):
    i = pl.program_id(0)
    first, last = i == 0, i == pl.num_programs(0) - 1
    w = w_ref[...]
    b = b_ref[...]

    def cv(c, p, n, k):
        return _conv3(c[...].astype(F32), p[...].astype(F32), n[...].astype(F32),
                      w[:, k * 1024:(k + 1) * 1024], b[:, k * 1024:(k + 1) * 1024], first, last)

    x0_ref[...] = cv(ac_ref, ap_ref, an_ref, 0).astype(BF16)
    xv_ref[...] = (cv(bc_ref, bp_ref, bn_ref, 1) * cv(cc_ref, cp_ref, cn_ref, 2)).astype(BF16)


def _conv_hy_call(z, w, b, tm):
    m = z.shape[0]
    out = pl.BlockSpec((tm, 1024), lambda i: (i, 0))
    return pl.pallas_call(
        _conv_hy_body,
        grid=(m // tm,),
        in_specs=(_halo_specs(tm, m, ZC_X0) + _halo_specs(tm, m, ZC_X1) + _halo_specs(tm, m, ZC_HV)
                  + [pl.BlockSpec((3, 3072), lambda i: (0, 0)), pl.BlockSpec((1, 3072), lambda i: (0, 0))]),
        out_specs=[out, out],
        out_shape=[jax.ShapeDtypeStruct((m, 1024), BF16), jax.ShapeDtypeStruct((m, 1024), BF16)],
        compiler_params=_cparams(("arbitrary",), 32),
        name="conv_hy",
    )(z, z, z, z, z, z, z, z, z, w, b)


def _mlstm_body(kqf_ref, vf_ref, gf_ref, kqb_ref, vb_ref, gb_ref, bg_ref, s0_ref, m0_ref,
                hf_ref, hb_ref, sfin_ref, mfin_ref, s_scr, m_scr):
    j = pl.program_id(0)

    @pl.when(j == 0)
    def _():
        s_scr[...] = s0_ref[...]
        m_scr[...] = m0_ref[...]

    r = lax.broadcasted_iota(I32, (CHUNK, CHUNK), 0)
    c = lax.broadcasted_iota(I32, (CHUNK, CHUNK), 1)
    e1 = jnp.where(c == 0, 1.0, 0.0).astype(BF16)
    bg = bg_ref[...]

    for d in range(2):
        kq = (kqf_ref, kqb_ref)[d][...]
        v = (vf_ref, vb_ref)[d][...]
        g_all = (gf_ref, gb_ref)[d][...] + bg
        out_ref = (hf_ref, hb_ref)[d]
        tri = (r >= c) if d == 0 else (c >= r)
        tri_b = jnp.where(tri, 1.0, 0.0).astype(BF16)
        gi = g_all if d == 0 else pltpu.roll(g_all, CHUNK - 16, 1)
        gfp = pltpu.roll(g_all, CHUNK - 8 - 16 * d, 1)
        lf = jnp.minimum(gfp, 0.0) - jnp.log(1.0 + jnp.exp(-jnp.abs(gfp)))
        l1 = lf.astype(BF16)
        r1 = lf - l1.astype(F32)
        l2 = r1.astype(BF16)
        l3 = (r1 - l2.astype(F32)).astype(BF16)
        bcum = _dg(tri_b, l1) + _dg(tri_b, l2) + _dg(tri_b, l3)
        gtot = bcum[CHUNK - 1:CHUNK, :] if d == 0 else bcum[0:1, :]
        acol = gtot - bcum + gi
        m_loc = jnp.max(acol, axis=0, keepdims=True)
        wg = jnp.exp(acol - m_loc)
        m_st = m_scr[d, 0:1, :]
        inter = bcum + m_st
        m_new = jnp.maximum(gtot + m_st, m_loc)
        sp = jnp.broadcast_to(jnp.exp(gtot + m_st - m_new), (CHUNK, CHUNK))
        sl = jnp.broadcast_to(jnp.exp(m_loc - m_new), (CHUNK, CHUNK))
        rt = jnp.transpose(gi - bcum)

        for h in range(N_HEADS):
            p, half = divmod(h, 2)
            lm = (c // QK_HEAD) == half
            kp = kq[:, p * 128:(p + 1) * 128]
            qp = kq[:, QK_W + p * 128:QK_W + (p + 1) * 128]
            vaug = jnp.concatenate([v[:, h * 128:(h + 1) * 128], e1], axis=1)
            qm = jnp.where(lm, qp, jnp.zeros_like(qp))
            dl = jnp.where(tri, bcum[:, h:h + 1] + rt[h:h + 1, :], NEG)
            icol = inter[:, h:h + 1]
            mt = jnp.maximum(icol, jnp.max(dl, axis=1, keepdims=True))
            pm = jnp.exp(dl - mt)
            s = (_dg(qm, kp, _NT) * pm).astype(BF16)
            st = s_scr[d * N_HEADS + h]
            tot = _dg(s, vaug) + jnp.exp(icol - mt) * _dg(qm, st.astype(BF16))
            den = jnp.maximum(jnp.abs(tot[:, 128:129]), jnp.exp(-mt))
            out_ref[:, h * 128:(h + 1) * 128] = (tot[:, :128] / den).astype(BF16)
            kw = jnp.where(lm, kp.astype(F32) * wg[:, h:h + 1], 0.0).astype(BF16)
            s_scr[d * N_HEADS + h] = sp[:, h:h + 1] * st + sl[:, h:h + 1] * _dg(kw, vaug, _TN)
        m_scr[d, 0:1, :] = m_new

    @pl.when(j == pl.num_programs(0) - 1)
    def _():
        sfin_ref[...] = s_scr[...]
        mfin_ref[...] = m_scr[...]


def _mlstm_call(kq, z, gates, bg, s0, m0):
    m = kq.shape[0]
    nc = m // CHUNK
    fwd = lambda cb: (lambda j: (j, cb))
    bwd = lambda cb: (lambda j: (nc - 1 - j, cb))
    st_spec = pl.BlockSpec((2 * N_HEADS, CHUNK, 256), lambda j: (0, 0, 0))
    m_spec = pl.BlockSpec((2, 8, 128), lambda j: (0, 0, 0))
    return pl.pallas_call(
        _mlstm_body,
        grid=(nc,),
        in_specs=[pl.BlockSpec((CHUNK, 1024), fwd(0)), pl.BlockSpec((CHUNK, 1024), fwd(ZC_V)),
                  pl.BlockSpec((CHUNK, GATE_PAD), fwd(0)),
                  pl.BlockSpec((CHUNK, 1024), bwd(0)), pl.BlockSpec((CHUNK, 1024), bwd(ZC_V)),
                  pl.BlockSpec((CHUNK, GATE_PAD), bwd(0)),
                  pl.BlockSpec((1, GATE_PAD), lambda j: (0, 0)), st_spec, m_spec],
        out_specs=[pl.BlockSpec((CHUNK, 1024), fwd(0)), pl.BlockSpec((CHUNK, 1024), bwd(0)), st_spec, m_spec],
        out_shape=[jax.ShapeDtypeStruct((m, 1024), BF16), jax.ShapeDtypeStruct((m, 1024), BF16),
                   jax.ShapeDtypeStruct((2 * N_HEADS, CHUNK, 256), F32),
                   jax.ShapeDtypeStruct((2, 8, 128), F32)],
        scratch_shapes=[pltpu.VMEM((2 * N_HEADS, CHUNK, 256), F32), pltpu.VMEM((2, 8, 128), F32)],
        compiler_params=_cparams(("arbitrary",), 32),
        name="mlstm",
    )(kq, z, gates, kq, z, gates, bg, s0, m0)


def _filt_body(seq_len, ft_ref, w1_ref, b1_ref, f1_ref, w2_ref, b2_ref, f2_ref, w3_ref, b3_ref, rt_ref,
               kf_ref, l1_ref):
    tn = ft_ref.shape[1]
    reps = tn // 128
    col = lambda ref: jnp.tile(ref[...], (1, reps))
    h1 = jnp.sin(col(f1_ref) * (_dot3(w1_ref[...], ft_ref[...]) + col(b1_ref)))
    h2 = jnp.sin(col(f2_ref) * (_dot3(w2_ref[...], h1) + col(b2_ref)))
    h = _dot3(h2, w3_ref[...], _TN) + b3_ref[...]
    n = pl.program_id(0) * tn + lax.broadcasted_iota(I32, h.shape, 0)
    t01 = jnp.where(n <= seq_len, n, 2 * seq_len - n).astype(F32) / float(max(seq_len - 1, 1))
    h = jnp.where(n == seq_len, 0.0, h * jnp.exp(-t01 * rt_ref[...]))
    kf_ref[...] = h.astype(BF16)

    @pl.when(pl.program_id(0) == 0)
    def _():
        l1_ref[...] = jnp.zeros_like(l1_ref)

    l1_ref[...] += jnp.sum(jnp.abs(h), axis=0, keepdims=True)


def _filt_call(feats_t, w1t, b1, f1, w2t, b2, f2, w3, b3, rates, tn):
    n = feats_t.shape[1]
    nt = n // tn
    half = lambda i: (0, jnp.where(i >= nt // 2, 1, 0))
    c64 = lambda shape: pl.BlockSpec(shape, lambda i: (0, 0))
    return pl.pallas_call(
        functools.partial(_filt_body, n // 2),
        grid=(nt,),
        in_specs=[pl.BlockSpec((64, tn), lambda i: (0, i)),
                  c64((64, 64)), c64((64, 128)), c64((64, 128)), c64((64, 64)), c64((64, 128)), c64((64, 128)),
                  pl.BlockSpec((64, HY_W), half), pl.BlockSpec((1, HY_W), half), c64((1, HY_W))],
        out_specs=[pl.BlockSpec((tn, HY_W), lambda i: (i, 0)), pl.BlockSpec((1, HY_W), lambda i: (0, 0))],
        out_shape=[jax.ShapeDtypeStruct((n, HY_W), BF16), jax.ShapeDtypeStruct((1, HY_W), F32)],
        compiler_params=_cparams(("arbitrary",), 32),
        name="filt",
    )(feats_t, w1t, b1, f1, w2t, b2, f2, w3, b3, rates)


def _fft_consts(n1_rows):
    n = n1_rows * FFT_N2
    kv = n1_rows // 2 + 1
    k1 = np.arange(FFT_KP, dtype=np.float64)
    valid = (k1 < kv).astype(np.float64)
    n1 = np.arange(n1_rows, dtype=np.float64)
    th1 = 2.0 * np.pi * np.outer(k1, n1) / n1_rows
    f1 = np.concatenate([np.cos(th1) * valid[:, None], -np.sin(th1) * valid[:, None]], axis=0)
    n2 = np.arange(FFT_N2, dtype=np.float64)
    tht = 2.0 * np.pi * np.outer(k1, n2) / n
    rep = lambda a: jnp.broadcast_to(jnp.asarray(a, F32)[:, :, None], (FFT_KP, FFT_N2, 128))
    twr = rep(np.cos(tht) * valid[:, None])
    twi = rep(-np.sin(tht) * valid[:, None])
    th2 = 2.0 * np.pi * np.outer(n2, n2) / FFT_N2
    cs, sn = np.cos(th2), np.sin(th2)
    f2p = np.block([[cs, sn], [-sn, cs]])
    f2pc = np.block([[cs, -sn], [sn, cs]])
    wk = np.where((k1 == 0) | (k1 == kv - 1), 1.0, 2.0) * valid / n
    half = n1_rows // 2
    thi = 2.0 * np.pi * np.outer(n1[:half], k1) / n1_rows
    gc = np.cos(thi) * wk[None, :]
    gs = np.sin(thi) * wk[None, :]
    as_bf = lambda a: jnp.asarray(a, F32).astype(BF16)
    return dict(f1=as_bf(f1), twr=twr, twi=twi,
                f2p=as_bf(f2p), f2pc=as_bf(f2pc), gc=as_bf(gc), gs=as_bf(gs))


def _fft1_body(f_ref, x_ref, ar_ref, ai_ref):
    o = _dg(f_ref[...], x_ref[...])
    ar_ref[...] = o[:FFT_KP].astype(BF16)
    ai_ref[...] = o[FFT_KP:].astype(BF16)


def _fft1_call(f1, x2d, cb):
    k, cols = x2d.shape
    f1 = f1[:, :k]
    out = pl.BlockSpec((FFT_KP, cb), lambda i: (0, i))
    sh = jax.ShapeDtypeStruct((FFT_KP, cols), BF16)
    return pl.pallas_call(
        _fft1_body,
        grid=(cols // cb,),
        in_specs=[pl.BlockSpec((2 * FFT_KP, k), lambda i: (0, 0)), pl.BlockSpec((k, cb), lambda i: (0, i))],
        out_specs=[out, out],
        out_shape=[sh, sh],
        compiler_params=_cparams(("arbitrary",), 32),
        name="fft1",
    )(f1, x2d)


def _twiddled(ar_ref, ai_ref, twr_ref, twi_ref, reps):
    a_r = ar_ref[...].astype(F32)
    a_i = ai_ref[...].astype(F32)
    tr = jnp.tile(twr_ref[...], (1, reps))
    ti = jnp.tile(twi_ref[...], (1, reps))
    st = jnp.concatenate([a_r * tr - a_i * ti, a_r * ti + a_i * tr], axis=0).astype(BF16)
    return st, tr, ti


def _fft2_filt_body(ar_ref, ai_ref, twr_ref, twi_ref, f2p_ref, k_ref):
    st, _, _ = _twiddled(ar_ref, ai_ref, twr_ref, twi_ref, ar_ref.shape[-1] // 128)
    k_ref[...] = _dg(f2p_ref[...], st).astype(BF16)


def _fft2_conv_body(ar_ref, ai_ref, twr_ref, twi_ref, k_ref, f2p_ref, f2pc_ref, br_ref, bi_ref):
    st, tr, ti = _twiddled(ar_ref, ai_ref, twr_ref, twi_ref, ar_ref.shape[-1] // 128)
    x = _dg(f2p_ref[...], st)
    xr, xi = x[:FFT_N2], x[FFT_N2:]
    kr = k_ref[:FFT_N2, :].astype(F32)
    ki = k_ref[FFT_N2:, :].astype(F32)
    sy = jnp.concatenate([xr * kr - xi * ki, xr * ki + xi * kr], axis=0).astype(BF16)
    b = _dg(f2pc_ref[...], sy)
    b_r, b_i = b[:FFT_N2], b[FFT_N2:]
    br_ref[...] = (b_r * tr + b_i * ti).astype(BF16)
    bi_ref[...] = (b_i * tr - b_r * ti).astype(BF16)


def _fft2_specs(ch):
    blk = pl.BlockSpec((None, FFT_N2, ch), lambda i: (i, 0, 0))
    tw = pl.BlockSpec((None, FFT_N2, 128), lambda i: (i, 0, 0))
    mat = pl.BlockSpec((2 * FFT_N2, 2 * FFT_N2), lambda i: (0, 0))
    return blk, tw, mat


def _fft2_filt_call(ar, ai, cst):
    ch = ar.shape[-1]
    blk, tw, mat = _fft2_specs(ch)
    return pl.pallas_call(
        _fft2_filt_body,
        grid=(FFT_KP,),
        in_specs=[blk, blk, tw, tw, mat],
        out_specs=pl.BlockSpec((None, 2 * FFT_N2, ch), lambda i: (i, 0, 0)),
        out_shape=jax.ShapeDtypeStruct((FFT_KP, 2 * FFT_N2, ch), BF16),
        compiler_params=_cparams(("arbitrary",), 32),
        name="fft2_filt",
    )(ar, ai, cst["twr"], cst["twi"], cst["f2p"])


def _fft2_conv_call(ar, ai, khat, cst):
    ch = ar.shape[-1]
    blk, tw, mat = _fft2_specs(ch)
    sh = jax.ShapeDtypeStruct((FFT_KP, FFT_N2, ch), BF16)
    return pl.pallas_call(
        _fft2_conv_body,
        grid=(FFT_KP,),
        in_specs=[blk, blk, tw, tw, pl.BlockSpec((None, 2 * FFT_N2, ch), lambda i: (i, 0, 0)), mat, mat],
        out_specs=[blk, blk],
        out_shape=[sh, sh],
        compiler_params=_cparams(("arbitrary",), 32),
        name="fft2_conv",
    )(ar, ai, cst["twr"], cst["twi"], khat, cst["f2p"], cst["f2pc"])


def _ifft1_body(gc_ref, gs_ref, br_ref, bi_ref, x0_ref, xv_ref, il_ref, ds_ref, o_ref):
    y = _dg(gc_ref[...], br_ref[...]) - _dg(gs_ref[...], bi_ref[...])
    o_ref[...] = (x0_ref[...].astype(F32)
                  * (y * il_ref[...] + ds_ref[...] * xv_ref[...].astype(F32))).astype(BF16)


def _ifft1_call(cst, br2d, bi2d, x0_2d, xv_2d, il_t, ds_t, cb):
    rows, cols = x0_2d.shape
    g = pl.BlockSpec((rows, FFT_KP), lambda i: (0, 0))
    kb = pl.BlockSpec((FFT_KP, cb), lambda i: (0, i))
    xb = pl.BlockSpec((rows, cb), lambda i: (0, i))
    vb = pl.BlockSpec((1, cb), lambda i: (0, 0))
    return pl.pallas_call(
        _ifft1_body,
        grid=(cols // cb,),
        in_specs=[g, g, kb, kb, xb, xb, vb, vb],
        out_specs=xb,
        out_shape=jax.ShapeDtypeStruct((rows, cols), BF16),
        compiler_params=_cparams(("arbitrary",), 32),
        name="ifft1",
    )(cst["gc"], cst["gs"], br2d, bi2d, x0_2d, xv_2d, il_t, ds_t)


def _outproj_body(hf_ref, hb_ref, zo_ref, yh_ref, x_ref, erow_ref, ecol_ref, gh_ref, wa_ref, wb_ref,
                  g1_ref, gf_ref, sh_ref, sc_ref, wr_ref, x1_ref, h2_ref, s_ref):
    hs = hf_ref[...].astype(F32) + hb_ref[...].astype(F32)
    gh = gh_ref[...]
    parts = []
    for h in range(N_HEADS):
        hh = hs[:, h * 128:(h + 1) * 128]
        ms = jnp.mean(hh * hh, axis=-1, keepdims=True)
        parts.append(hh * lax.rsqrt(ms + EPS) * gh[:, h * 128:(h + 1) * 128])
    ym = jnp.concatenate(parts, axis=-1) * _sigmoid(zo_ref[...].astype(F32))
    y = _dg(ym.astype(BF16), wa_ref[...]) + _dg(yh_ref[...], wb_ref[...])
    rp = x_ref.shape[0] // GRID_W
    erow8 = erow_ref[...]
    erow = erow8[0:rp, :]
    for q in range(1, 8 // rp):
        erow = jnp.where(pl.program_id(0) % (8 // rp) == q, erow8[q * rp:(q + 1) * rp, :], erow)
    x1 = _add_pos(x_ref[...], erow, ecol_ref[...]) + g1_ref[...] * y
    x1_ref[...] = x1
    h2 = _norm_mod(x1, gf_ref[...], sh_ref[...], sc_ref[...])
    half = h2.shape[1] // 2
    _store_slabs(h2_ref, _pack_pair(h2[:, :half], h2[:, half:]))
    s_ref[...] = _sigmoid(_dot3(wr_ref[...], h2, _NT))


def _outproj_call(hf, hb, z, yh, x, erow, ecol, gh, wa, wb, g1, gf, sh2, sc2, wrt, tm):
    m, d = x.shape
    row = lambda cb: pl.BlockSpec((tm, 1024), lambda i: (i, cb))
    vec = lambda n: pl.BlockSpec((1, n), lambda i: (0, 0))
    full = pl.BlockSpec((tm, d), lambda i: (i, 0))
    return pl.pallas_call(
        _outproj_body,
        grid=(m // tm,),
        in_specs=[row(0), row(0), row(ZC_O), row(0), full,
                  pl.BlockSpec((8, d // 2), lambda i: (i * (tm // GRID_W) // 8, 0)),
                  pl.BlockSpec((GRID_W, d // 2), lambda i: (0, 0)),
                  vec(MV_W),
                  pl.BlockSpec((MV_W, d), lambda i: (0, 0)), pl.BlockSpec((HY_W, d), lambda i: (0, 0)),
                  vec(d), vec(d), vec(d), vec(d),
                  pl.BlockSpec((N_EXPERTS, d), lambda i: (0, 0))],
        out_specs=[full, pl.BlockSpec((tm * SLAB, SLAB_W), lambda i: (i, 0)),
                   pl.BlockSpec((N_EXPERTS, tm), lambda i: (0, i))],
        out_shape=[jax.ShapeDtypeStruct((m, d), F32), jax.ShapeDtypeStruct((m * SLAB, SLAB_W), U32),
                   jax.ShapeDtypeStruct((N_EXPERTS, m), F32)],
        compiler_params=_cparams(("arbitrary",), 56),
        name="outproj",
    )(hf, hb, z, yh, x, erow, ecol, gh, wa, wb, g1, gf, sh2, sc2, wrt)


def _first_max(x, idx, sentinel):
    m = jnp.max(x, axis=0, keepdims=True)
    return m, jnp.min(jnp.where(x == m, idx, sentinel), axis=0, keepdims=True)


def _route_body(s_ref, b_ref, e_ref, w_ref, r_ref, cnt_ref, u_scr, run_scr):
    i = pl.program_id(0)
    tt = s_ref.shape[1]

    @pl.when(i == 0)
    def _():
        rr = lax.broadcasted_iota(I32, (tt, tt), 0)
        cc = lax.broadcasted_iota(I32, (tt, tt), 1)
        u_scr[...] = jnp.where(rr < cc, 1.0, 0.0).astype(BF16)
        run_scr[...] = jnp.zeros_like(run_scr)

    s = s_ref[...]
    sel = s + b_ref[...][:, 0:1]
    sub8 = lax.broadcasted_iota(I32, (E_PER_GROUP, tt), 0).astype(F32)
    gs = jnp.zeros((N_GROUPS, tt), F32)
    for g in range(N_GROUPS):
        grp = sel[g * E_PER_GROUP:(g + 1) * E_PER_GROUP, :]
        m1, i1 = _first_max(grp, sub8, float(E_PER_GROUP))
        m2 = jnp.max(jnp.where(sub8 == i1, -jnp.inf, grp), axis=0, keepdims=True)
        gs = jnp.where(sub8 == g, m1 + m2, gs)
    gmask = jnp.zeros((N_GROUPS, tt), F32)
    for _ in range(TOPK_GROUPS):
        _, ig = _first_max(gs, sub8, float(N_GROUPS))
        hit = sub8 == ig
        gmask = jnp.where(hit, 1.0, gmask)
        gs = jnp.where(hit, -jnp.inf, gs)
    masked = jnp.concatenate(
        [jnp.where(jnp.broadcast_to(gmask[g:g + 1, :], (E_PER_GROUP, tt)) > 0.5,
                   sel[g * E_PER_GROUP:(g + 1) * E_PER_GROUP, :], -jnp.inf) for g in range(N_GROUPS)], axis=0)
    sub64 = lax.broadcasted_iota(I32, (N_EXPERTS, tt), 0).astype(F32)
    oh = jnp.zeros((N_EXPERTS, tt), F32)
    eks, wks = [], []
    for _ in range(TOP_K):
        _, ie = _first_max(masked, sub64, float(N_EXPERTS))
        hit = sub64 == ie
        wks.append(jnp.sum(jnp.where(hit, s, 0.0), axis=0, keepdims=True))
        eks.append(ie)
        masked = jnp.where(hit, -jnp.inf, masked)
        oh = jnp.where(hit, 1.0, oh)
    wsum = wks[0]
    for k in range(1, TOP_K):
        wsum = wsum + wks[k]
    run = run_scr[...]
    rank_t = _dg(oh.astype(BF16), u_scr[...]) + jnp.tile(run, (1, tt // 128))
    for k in range(TOP_K):
        e_ref[k:k + 1, :] = eks[k].astype(I32)
        w_ref[k:k + 1, :] = wks[k] / wsum * ROUTE_SCALE
        r_ref[k:k + 1, :] = jnp.sum(jnp.where(sub64 == eks[k], rank_t, 0.0), axis=0, keepdims=True).astype(I32)
    run_new = run + jnp.sum(oh, axis=1, keepdims=True)
    run_scr[...] = run_new
    cnt_ref[...] = run_new.astype(I32)


def _route_call(s_t, b_col, tt):
    m = s_t.shape[1]
    out = pl.BlockSpec((TOP_K, tt), lambda i: (0, i))
    return pl.pallas_call(
        _route_body,
        grid=(m // tt,),
        in_specs=[pl.BlockSpec((N_EXPERTS, tt), lambda i: (0, i)),
                  pl.BlockSpec((N_EXPERTS, 128), lambda i: (0, 0))],
        out_specs=[out, out, out, pl.BlockSpec((N_EXPERTS, 128), lambda i: (0, 0))],
        out_shape=[jax.ShapeDtypeStruct((TOP_K, m), I32), jax.ShapeDtypeStruct((TOP_K, m), F32),
                   jax.ShapeDtypeStruct((TOP_K, m), I32), jax.ShapeDtypeStruct((N_EXPERTS, 128), I32)],
        scratch_shapes=[pltpu.VMEM((tt, tt), BF16), pltpu.VMEM((N_EXPERTS, 128), F32)],
        compiler_params=_cparams(("arbitrary",), 32),
        name="route",
    )(s_t, b_col)


def _posk_body(pst_ref, e_ref, r_ref, p_ref):
    e = e_ref[...]
    acc = r_ref[...]
    for x in range(N_EXPERTS):
        acc = acc + jnp.where(e == x, pst_ref[x], 0)
    p_ref[...] = acc


def _posk_call(pstart, eidx, rank):
    k, m = eidx.shape
    tt = min(m, 2048)
    blk = pl.BlockSpec((k, tt), lambda i, pst: (0, i))
    return pl.pallas_call(
        _posk_body,
        grid_spec=pltpu.PrefetchScalarGridSpec(num_scalar_prefetch=1, grid=(m // tt,),
                                               in_specs=[blk, blk], out_specs=blk),
        out_shape=jax.ShapeDtypeStruct((k, m), I32),
        compiler_params=_cparams(("arbitrary",), 32),
        name="posk",
    )(pstart, eidx, rank)


def _slab(ref, r):
    return ref.at[pl.ds(pl.multiple_of(r * SLAB, SLAB), SLAB), :]


def _ffn_packed(x_ref, rows, w1, w3, w2):
    half = w1.shape[0] // 2
    xa, xb = _unpack_pair(_load_slabs(x_ref, rows))
    xa = xa.astype(BF16)
    xb = xb.astype(BF16)
    h1 = _dg(xa, w1[0:half, :]) + _dg(xb, w1[half:, :])
    h3 = _dg(xa, w3[0:half, :]) + _dg(xb, w3[half:, :])
    return _dg((_silu(h1) * h3).astype(BF16), w2[...])


DISPATCH_PHASES = 8


def _dispatch_body(cnt_ref, pst_ref, pcn_ref, h2_ref, pos_ref, w1_ref, w3_ref, w2_ref,
                   xs_ref, sh_ref, zrow, xa_s, xb_s, h1_s, h3_s, a_s, sem):
    i = pl.program_id(0)
    td = h2_ref.shape[0] // SLAB
    half = w1_ref.shape[0] // 2
    per = td // DISPATCH_PHASES
    ncol = sh_ref.shape[1] // (DISPATCH_PHASES - 4)

    def row_copy(t, dst):
        return pltpu.make_async_copy(_slab(h2_ref, t), _slab(xs_ref, dst), sem)

    def issue(t, carry):
        for k in range(TOP_K):
            row_copy(t, pos_ref[k, t]).start(priority=k % 2)
        return carry

    for p in range(DISPATCH_PHASES):
        lax.fori_loop(p * per, (p + 1) * per, issue, 0)
        if p == 0:
            xa, xb = _unpack_pair(_load_slabs(h2_ref, td))
            xa_s[...] = xa.astype(BF16)
            xb_s[...] = xb.astype(BF16)
            h1_s[...] = _dg(xa_s[...], w1_ref[0:half, :])
        elif p == 1:
            h1_s[...] += _dg(xb_s[...], w1_ref[half:, :])
        elif p == 2:
            h3_s[...] = _dg(xa_s[...], w3_ref[0:half, :])
        elif p == 3:
            a_s[...] = (_silu(h1_s[...]) * (h3_s[...] + _dg(xb_s[...], w3_ref[half:, :]))).astype(BF16)
        else:
            c0 = (p - 4) * ncol
            sh_ref[:, c0:c0 + ncol] = _dg(a_s[...], w2_ref[:, c0:c0 + ncol]).astype(BF16)

    def drain(t, carry):
        for k in range(TOP_K):
            row_copy(0, 0).wait()
        return carry

    lax.fori_loop(0, td, drain, 0)

    @pl.when(i == pl.num_programs(0) - 1)
    def _():
        zrow[...] = jnp.zeros_like(zrow)

        def zero_copy(dst):
            return pltpu.make_async_copy(zrow, _slab(xs_ref, dst), sem)

        def per_expert(e, carry):
            base = pst_ref[e]
            lax.fori_loop(cnt_ref[e], pcn_ref[e], lambda rr, cc: (zero_copy(base + rr).start(), cc)[1], 0)
            lax.fori_loop(cnt_ref[e], pcn_ref[e], lambda rr, cc: (zero_copy(0).wait(), cc)[1], 0)
            return carry

        lax.fori_loop(0, N_EXPERTS, per_expert, 0)


def _dispatch_call(cnt, pstart, pcnt, h2s, pos, w1s, w3s, w2s, rows, td):
    m = h2s.shape[0] // SLAB
    d, ds = w1s.shape
    return pl.pallas_call(
        _dispatch_body,
        grid_spec=pltpu.PrefetchScalarGridSpec(
            num_scalar_prefetch=3, grid=(m // td,),
            in_specs=[pl.BlockSpec((td * SLAB, SLAB_W), lambda i, *_: (i, 0)),
                      pl.BlockSpec((TOP_K, td), lambda i, *_: (0, i), memory_space=pltpu.SMEM),
                      pl.BlockSpec((d, ds), lambda i, *_: (0, 0)), pl.BlockSpec((d, ds), lambda i, *_: (0, 0)),
                      pl.BlockSpec((ds, d), lambda i, *_: (0, 0))],
            out_specs=[pl.BlockSpec(memory_space=pl.ANY), pl.BlockSpec((td, d), lambda i, *_: (i, 0))],
            scratch_shapes=[pltpu.VMEM((SLAB, SLAB_W), U32),
                            pltpu.VMEM((td, d // 2), BF16), pltpu.VMEM((td, d // 2), BF16),
                            pltpu.VMEM((td, ds), F32), pltpu.VMEM((td, ds), F32), pltpu.VMEM((td, ds), BF16),
                            pltpu.SemaphoreType.DMA(())]),
        out_shape=[jax.ShapeDtypeStruct((rows * SLAB, SLAB_W), U32), jax.ShapeDtypeStruct((m, d), BF16)],
        compiler_params=_cparams(("arbitrary",), 40),
        name="dispatch",
    )(cnt, pstart, pcnt, h2s, pos, w1s, w3s, w2s)


def _moe_body(te_ref, nu_ref, x_ref, w1_ref, w3_ref, w2_ref, y_ref, w1b, w3b, w2b):
    i = pl.program_id(0)
    used = i < nu_ref[0]
    prev = te_ref[jnp.maximum(i - 1, 0)]

    @pl.when(jnp.logical_and(used, jnp.logical_or(i == 0, te_ref[i] != prev)))
    def _():
        w1b[...] = w1_ref[...].astype(BF16)
        w3b[...] = w3_ref[...].astype(BF16)
        w2b[...] = w2_ref[...].astype(BF16)

    @pl.when(used)
    def _():
        half = w1b.shape[0] // 2
        y = _ffn_packed(x_ref, MOE_ROWS, w1b, w3b, w2b)
        _store_slabs(y_ref, _pack_pair(y[:, :half], y[:, half:]))


def _moe_call(tile_e, n_used, xs, w1, w3, w2):
    rows = xs.shape[0] // SLAB
    nt = rows // MOE_ROWS
    d, de = w1.shape[-2:]
    rmap = lambda i, te, nu: (jnp.minimum(i, nu[0] - 1), 0)
    wmap = lambda i, te, nu: (te[i], 0, 0)
    return pl.pallas_call(
        _moe_body,
        grid_spec=pltpu.PrefetchScalarGridSpec(
            num_scalar_prefetch=2, grid=(nt,),
            in_specs=[pl.BlockSpec((MOE_ROWS * SLAB, SLAB_W), rmap),
                      pl.BlockSpec((None, d, de), wmap), pl.BlockSpec((None, d, de), wmap),
                      pl.BlockSpec((None, de, d), wmap)],
            out_specs=pl.BlockSpec((MOE_ROWS * SLAB, SLAB_W), rmap),
            scratch_shapes=[pltpu.VMEM((d, de), BF16), pltpu.VMEM((d, de), BF16), pltpu.VMEM((de, d), BF16)]),
        out_shape=jax.ShapeDtypeStruct((rows * SLAB, SLAB_W), U32),
        compiler_params=_cparams(("arbitrary",), 56),
        name="moe",
    )(tile_e, n_used, xs, w1, w3, w2)


def _final_body(x1_ref, sh_ref, pos_ref, posn_ref, wt_ref, ys_ref, g2_ref, gn_ref, o_ref, ybuf, sems):
    i = pl.program_id(0)
    n = pl.num_programs(0)
    tf, d = x1_ref.shape
    half = d // 2
    slot = i % 2
    slot_slabs = TOP_K * tf

    def row_copy(p_ref, s, k, t):
        return pltpu.make_async_copy(_slab(ys_ref, p_ref[k, t]), _slab(ybuf, s * slot_slabs + k * tf + t),
                                     sems.at[s])

    def issue(p_ref, s, k0, k1):
        unroll = TOP_K // (k1 - k0)

        def body(tb, carry):
            for u in range(unroll):
                for k in range(k0, k1):
                    row_copy(p_ref, s, k, tb * unroll + u).start(priority=(k + u) % 2)
            return carry

        lax.fori_loop(0, tf // unroll, body, 0)

    @pl.when(i == 0)
    def _():
        issue(pos_ref, 0, 0, TOP_K)

    def drain(t, carry):
        for k in range(TOP_K):
            pltpu.make_async_copy(_slab(ys_ref, 0), _slab(ybuf, 0), sems.at[slot]).wait()
        return carry

    lax.fori_loop(0, tf, drain, 0)

    wt = jnp.concatenate([jnp.transpose(jnp.concatenate(
        [wt_ref[:, c * 128:(c + 1) * 128], jnp.zeros((128 - TOP_K, 128), F32)], axis=0)) for c in range(tf // 128)],
        axis=0)
    shared = sh_ref[...].astype(F32)
    acc_a = shared[:, :half]
    acc_b = shared[:, half:]
    base = slot * slot_slabs * SLAB
    for k in range(TOP_K):
        @pl.when(i + 1 < n)
        def _():
            issue(posn_ref, 1 - slot, k, k + 1)

        ya, yb = _unpack_pair(_load_slabs(ybuf, tf, base=base + k * tf * SLAB))
        acc_a = acc_a + wt[:, k:k + 1] * ya
        acc_b = acc_b + wt[:, k:k + 1] * yb
    xo = x1_ref[...] + g2_ref[...] * jnp.concatenate([acc_a, acc_b], axis=1)
    ms = jnp.mean(xo * xo, axis=-1, keepdims=True)
    o_ref[...] = xo * lax.rsqrt(ms + EPS) * gn_ref[...]


def _final_call(x1, sh, pos, wts, ys, g2, gn, tf):
    m, d = x1.shape
    nt = m // tf
    full = pl.BlockSpec((tf, d), lambda i: (i, 0))
    vec = pl.BlockSpec((1, d), lambda i: (0, 0))
    return pl.pallas_call(
        _final_body,
        grid=(nt,),
        in_specs=[full, full,
                  pl.BlockSpec((TOP_K, tf), lambda i: (0, i), memory_space=pltpu.SMEM),
                  pl.BlockSpec((TOP_K, tf), lambda i: (0, jnp.minimum(i + 1, nt - 1)), memory_space=pltpu.SMEM),
                  pl.BlockSpec((TOP_K, tf), lambda i: (0, i)),
                  pl.BlockSpec(memory_space=pl.ANY), vec, vec],
        out_specs=full,
        out_shape=jax.ShapeDtypeStruct((m, d), F32),
        scratch_shapes=[pltpu.VMEM((2 * TOP_K * tf * SLAB, SLAB_W), U32), pltpu.SemaphoreType.DMA((2,))],
        compiler_params=_cparams(("arbitrary",), 48),
        name="final",
    )(x1, sh, pos, pos, wts, ys, g2, gn)


def _pos_tables(n_tokens):
    rows = n_tokens // GRID_W
    quarter = D_MODEL // 4
    omega = 1.0 / (10000.0 ** (jnp.arange(quarter, dtype=F32) / quarter))

    def emb1d(pos):
        ang = pos[:, None] * omega[None]
        return jnp.concatenate([jnp.sin(ang), jnp.cos(ang)], axis=-1)

    return emb1d(jnp.arange(rows, dtype=F32)), emb1d(jnp.arange(GRID_W, dtype=F32))


def _filter_feats(L):
    n = jnp.arange(2 * L, dtype=I32)
    t = jnp.where(n <= L, n, 2 * L - n).astype(F32)
    t01 = t / max(L - 1, 1)
    w = 2.0 * math.pi * t / L
    bands = jnp.linspace(1e-4, FILT_BANDS - 1, FILT_BANDS, dtype=F32)
    feats = jnp.concatenate([t01[None, :], jnp.cos(bands[:, None] * w[None, :]), -jnp.sin(bands[:, None] * w[None, :]),
                             jnp.zeros((64 - 33, 2 * L), F32)], axis=0)
    return feats


def _pad_rows(a, rows):
    return jnp.concatenate([a, jnp.zeros((rows - a.shape[0],) + a.shape[1:], a.dtype)], axis=0)


def _layer(x, c, ctx, c_ctx, w_ada, b_ada, g_mix, g_ffn, w_in, b_gates, conv_k_w, conv_k_b,
           conv_q_w, conv_q_b, g_head, conv_hy_w, conv_hy_b, filt_w1, filt_b1, filt_freq1,
           filt_w2, filt_b2, filt_freq2, filt_w3, filt_b3, hy_dskip, w_out, w_router, b_router,
           w1_e, w3_e, w2_e, w1_s, w3_s, w2_s, g_final):
    L, d = x.shape
    lc = ctx.shape[0]
    row = lambda v: v.reshape(1, -1)

    cc = _pad_rows(jnp.stack([c, c_ctx], axis=0), 8)
    mods = _mod_call(cc, w_ada, row(b_ada))
    sh1, sc1, g1, sh2, sc2, g2 = [mods[0:1, k * d:(k + 1) * d] for k in range(6)]
    csh1, csc1 = mods[1:2, 0:d], mods[1:2, d:2 * d]

    w_r, w_g = _wprep_call(w_in)
    bg = jnp.concatenate([b_gates, jnp.zeros((GATE_PAD - 4 * N_HEADS,), F32)]).reshape(1, GATE_PAD)
    e_row, e_col = _pos_tables(L)
    conv_w = jnp.concatenate([conv_k_w, conv_q_w], axis=1)
    conv_b = jnp.concatenate([conv_k_b, conv_q_b]).reshape(1, -1)
    conv_s = jnp.concatenate([jnp.ones((QK_W,), F32), jnp.full((QK_W,), QK_HEAD ** -0.5, F32)]).reshape(1, -1)

    z_c, gt_c = _inproj_call(ctx, jnp.zeros((8, d // 2), F32), e_col, row(g_mix), csh1, csc1, w_r, w_g,
                             use_pos=False, tm=min(lc, 256))
    kq_c = _conv_kq_call(z_c, conv_w, conv_b, conv_s, tm=min(lc, 256))
    s0 = jnp.zeros((2 * N_HEADS, CHUNK, 256), F32)
    m0 = jnp.zeros((2, 8, 128), F32)
    _, _, s_ctx, m_ctx = _mlstm_call(kq_c, z_c, gt_c, bg, s0, m0)

    z, gates = _inproj_call(x, e_row, e_col, row(g_mix), sh1, sc1, w_r, w_g, use_pos=True, tm=min(L, 1024))
    kq = _conv_kq_call(z, conv_w, conv_b, conv_s, tm=min(L, 512))
    x0c, xv = _conv_hy_call(z, conv_hy_w, row(conv_hy_b), tm=min(L, 512))
    hf, hb, _, _ = _mlstm_call(kq, z, gates, bg, s_ctx, m_ctx)

    n1 = 2 * L // FFT_N2
    cst = _fft_consts(n1)
    rates = jnp.linspace(-math.log(DECAY_TARGET) / SLOW_DECAY_PCT, -math.log(DECAY_TARGET) / FAST_DECAY_PCT,
                         HY_W, dtype=F32).reshape(1, -1)
    w1t = jnp.transpose(_pad_rows(filt_w1, 64))
    colrep = lambda v: jnp.broadcast_to(v.reshape(-1, 1), (v.shape[0], 128))
    kf, l1 = _filt_call(_filter_feats(L), w1t, colrep(filt_b1), colrep(filt_freq1), jnp.transpose(filt_w2),
                        colrep(filt_b2), colrep(filt_freq2), filt_w3, row(filt_b3), rates, tn=min(L, 1024))
    cols = FFT_N2 * HY_W
    cb = 2048
    kar, kai = _fft1_call(cst["f1"], kf.reshape(n1, cols), cb)
    khat = _fft2_filt_call(kar.reshape(FFT_KP, FFT_N2, HY_W), kai.reshape(FFT_KP, FFT_N2, HY_W), cst)
    uar, uai = _fft1_call(cst["f1"], xv.reshape(n1 // 2, cols), cb)
    br, bi = _fft2_conv_call(uar.reshape(FFT_KP, FFT_N2, HY_W), uai.reshape(FFT_KP, FFT_N2, HY_W), khat, cst)
    reps = cb // HY_W
    il_t = jnp.tile(1.0 / l1, (1, reps))
    ds_t = jnp.tile(row(hy_dskip), (1, reps))
    yh = _ifft1_call(cst, br.reshape(FFT_KP, cols), bi.reshape(FFT_KP, cols),
                     x0c.reshape(n1 // 2, cols), xv.reshape(n1 // 2, cols), il_t, ds_t, cb).reshape(L, HY_W)

    wo = w_out.astype(BF16)
    x1, h2s, s_t = _outproj_call(hf, hb, z, yh, x, e_row, e_col, row(g_head), wo[:MV_W], wo[MV_W:],
                                g1, row(g_ffn), sh2, sc2, jnp.transpose(w_router), tm=min(L, 256))

    b_col = jnp.broadcast_to(b_router.reshape(N_EXPERTS, 1), (N_EXPERTS, 128))
    eidx, wts, rank, cnt2 = _route_call(s_t, b_col, tt=min(L, 1024))
    cnt = cnt2[:, 0]
    pcnt = (cnt + MOE_ROWS - 1) // MOE_ROWS * MOE_ROWS
    pend = jnp.cumsum(pcnt)
    pstart = pend - pcnt
    rows = L * TOP_K + N_EXPERTS * MOE_ROWS
    nt = rows // MOE_ROWS
    tile_row = jnp.arange(nt, dtype=I32) * MOE_ROWS
    tile_e = jnp.minimum(jnp.sum((pend[None, :] <= tile_row[:, None]).astype(I32), axis=1), N_EXPERTS - 1)
    n_used = (pend[-1] // MOE_ROWS).astype(I32).reshape(1)
    pos = _posk_call(pstart.astype(I32), eidx, rank)

    xs, sh = _dispatch_call(cnt, pstart.astype(I32), pcnt.astype(I32), h2s, pos, w1_s.astype(BF16),
                            w3_s.astype(BF16), w2_s.astype(BF16), rows, td=min(L, 256))
    ys = _moe_call(tile_e, n_used, xs, w1_e, w3_e, w2_e)
    return _final_call(x1, sh, pos, wts, ys, g2, row(g_final), tf=min(L, 256))


def kernel(x, c, ctx, c_ctx, w_ada, b_ada, g_mix, g_ffn, w_in, b_gates, conv_k_w, conv_k_b, conv_q_w,
           conv_q_b, g_head, conv_hy_w, conv_hy_b, filt_w1, filt_b1, filt_freq1, filt_w2, filt_b2,
           filt_freq2, filt_w3, filt_b3, hy_dskip, w_out, w_router, b_router, w1_e, w3_e, w2_e,
           w1_s, w3_s, w2_s, g_final):
    assert x.shape[0] == 1 and w_ada.shape[0] == 1, "one batch element, one layer"
    out = _layer(x[0], c[0], ctx[0], c_ctx, w_ada[0], b_ada[0], g_mix[0], g_ffn[0], w_in[0], b_gates[0],
                 conv_k_w[0], conv_k_b[0], conv_q_w[0], conv_q_b[0], g_head[0], conv_hy_w[0], conv_hy_b[0],
                 filt_w1[0], filt_b1[0], filt_freq1[0], filt_w2[0], filt_b2[0], filt_freq2[0], filt_w3[0],
                 filt_b3[0], hy_dskip[0], w_out[0], w_router[0], b_router[0], w1_e[0], w3_e[0], w2_e[0],
                 w1_s[0], w3_s[0], w2_s[0], g_final)
    return out[None]
```

```python
import functools
import math

import numpy as np
import jax
import jax.numpy as jnp
from jax import lax
from jax.experimental import pallas as pl
from jax.experimental.pallas import tpu as pltpu

F32 = jnp.float32
BF16 = jnp.bfloat16
I32 = jnp.int32
U32 = jnp.uint32

D_MODEL = 2048
GRID_W = 64
N_HEADS = 8
QK_HEAD = 64
V_HEAD = 128
QK_W = N_HEADS * QK_HEAD
MV_W = N_HEADS * V_HEAD
HY_W = D_MODEL - MV_W
CHUNK = 128
FILT_BANDS = 16
FILT_HIDDEN = 64
DECAY_TARGET = 1e-2
FAST_DECAY_PCT = 0.3
SLOW_DECAY_PCT = 1.5
N_EXPERTS = 64
N_GROUPS = 8
E_PER_GROUP = 8
TOPK_GROUPS = 4
TOP_K = 8
D_EXPERT = 512
ROUTE_SCALE = 2.5
EPS = 1e-6
OFF_K = 0
OFF_V = OFF_K + QK_W
OFF_G = OFF_V + MV_W
OFF_Q = OFF_G + 4 * N_HEADS
OFF_O = OFF_Q + QK_W
OFF_HY = OFF_O + MV_W

ZC_KQ, ZC_V, ZC_O, ZC_X0, ZC_X1, ZC_HV = 0, 1, 2, 3, 4, 5
Z_COLS = 6 * 1024
GATE_PAD = 128

NEG = -1e30
MIB = 1024 * 1024

FFT_N2 = 128
FFT_KP = 144

MOE_ROWS = 256


def _cparams(sem, vmem_mb, flags=None):
    return pltpu.CompilerParams(dimension_semantics=sem, vmem_limit_bytes=vmem_mb * MIB, flags=flags)


def _split2(x):
    hi = x.astype(BF16)
    lo = (x - hi.astype(F32)).astype(BF16)
    return hi, lo


_NN = (((1,), (0,)), ((), ()))
_NT = (((1,), (1,)), ((), ()))
_TN = (((0,), (0,)), ((), ()))


def _dg(a, b, dims=_NN):
    return lax.dot_general(a, b, dims, preferred_element_type=F32)


def _dot3(a, b, dims=_NN):
    ah, al = _split2(a)
    bh, bl = _split2(b)
    return _dg(ah, bh, dims) + _dg(al, bh, dims) + _dg(ah, bl, dims)


def _sigmoid(x):
    return 1.0 / (1.0 + jnp.exp(-x))


def _silu(x):
    return x * _sigmoid(x)


def _pack_pair(a, b):
    hi = lax.bitcast_convert_type(a.astype(BF16).astype(F32), U32)
    lo = lax.bitcast_convert_type(b.astype(BF16).astype(F32), U32)
    return hi | (lo >> 16)


def _unpack_pair(w):
    a = lax.bitcast_convert_type(w & jnp.uint32(0xFFFF0000), F32)
    b = lax.bitcast_convert_type(w << 16, F32)
    return a, b


SLAB = 8
SLAB_W = 128


def _store_slabs(ref, w, base=0):
    r = w.shape[0]
    for j in range(SLAB):
        ref[pl.ds(base + j, r, stride=SLAB), :] = w[:, j * SLAB_W:(j + 1) * SLAB_W]


def _load_slabs(ref, r, base=0):
    return jnp.concatenate([ref[pl.ds(base + j, r, stride=SLAB), :] for j in range(SLAB)], axis=1)


def _norm_mod(x, g, sh, sc):
    ms = jnp.mean(x * x, axis=-1, keepdims=True)
    return (x * lax.rsqrt(ms + EPS) * g) * (1.0 + sc) + sh


def _add_pos(x, erow, ecol):
    tm, d = x.shape
    half = d // 2
    parts = []
    for r in range(tm // GRID_W):
        xs = x[r * GRID_W:(r + 1) * GRID_W, :]
        parts.append(jnp.concatenate([xs[:, :half] + erow[r:r + 1, :], xs[:, half:] + ecol], axis=-1))
    return parts[0] if len(parts) == 1 else jnp.concatenate(parts, axis=0)


def _mod_body(cc_ref, w_ref, b_ref, o_ref):
    o_ref[...] = _dot3(_silu(cc_ref[...]), w_ref[...]) + b_ref[...]


def _mod_call(cc, w, b):
    d, n = w.shape
    tn = 1024
    return pl.pallas_call(
        _mod_body,
        grid=(n // tn,),
        in_specs=[pl.BlockSpec((8, d), lambda j: (0, 0)),
                  pl.BlockSpec((d, tn), lambda j: (0, j)),
                  pl.BlockSpec((1, tn), lambda j: (0, j))],
        out_specs=pl.BlockSpec((8, tn), lambda j: (0, j)),
        out_shape=jax.ShapeDtypeStruct((8, n), F32),
        compiler_params=_cparams(("arbitrary",), 40),
        name="mod",
    )(cc, w, b)


def _wprep_body(w_ref, wr_ref, wg_ref):
    w = w_ref[...]
    wr_ref[...] = jnp.concatenate([w[:, OFF_K:OFF_V], w[:, OFF_Q:OFF_O], w[:, OFF_V:OFF_G], w[:, OFF_O:]],
                                  axis=1).astype(BF16)
    wg_ref[...] = jnp.concatenate([w[:, OFF_G:OFF_Q], jnp.zeros((w.shape[0], GATE_PAD - 4 * N_HEADS), F32)], axis=1)


def _wprep_call(w_in):
    _, d, n = w_in.shape
    tr = 128
    return pl.pallas_call(
        _wprep_body,
        grid=(d // tr,),
        in_specs=[pl.BlockSpec((None, tr, n), lambda i: (0, i, 0))],
        out_specs=[pl.BlockSpec((tr, Z_COLS), lambda i: (i, 0)), pl.BlockSpec((tr, GATE_PAD), lambda i: (i, 0))],
        out_shape=[jax.ShapeDtypeStruct((d, Z_COLS), BF16), jax.ShapeDtypeStruct((d, GATE_PAD), F32)],
        compiler_params=_cparams(("arbitrary",), 32),
        name="wprep",
    )(w_in)


def _inproj_body(use_pos, x_ref, erow_ref, ecol_ref, gm_ref, sh_ref, sc_ref, w_ref, wg_ref,
                 z_ref, g_ref, h_scr):
    @pl.when(pl.program_id(1) == 0)
    def _():
        x = x_ref[...]
        if use_pos:
            x = _add_pos(x, erow_ref[...], ecol_ref[...])
        h = _norm_mod(x, gm_ref[...], sh_ref[...], sc_ref[...])
        h_scr[...] = h.astype(BF16)
        g_ref[...] = _dot3(h, wg_ref[...])

    z_ref[...] = jnp.dot(h_scr[...], w_ref[...], preferred_element_type=F32).astype(BF16)


def _inproj_call(x, erow, ecol, gm, sh, sc, w, wg, use_pos, tm):
    m, d = x.shape
    tn = 1024
    er = tm // GRID_W if use_pos else erow.shape[0]
    row_map = (lambda i, j: (i, 0)) if use_pos else (lambda i, j: (0, 0))
    return pl.pallas_call(
        functools.partial(_inproj_body, use_pos),
        grid=(m // tm, Z_COLS // tn),
        in_specs=[pl.BlockSpec((tm, d), lambda i, j: (i, 0)),
                  pl.BlockSpec((er, d // 2), row_map),
                  pl.BlockSpec((GRID_W, d // 2), lambda i, j: (0, 0)),
                  pl.BlockSpec((1, d), lambda i, j: (0, 0)),
                  pl.BlockSpec((1, d), lambda i, j: (0, 0)),
                  pl.BlockSpec((1, d), lambda i, j: (0, 0)),
                  pl.BlockSpec((d, tn), lambda i, j: (0, j)),
                  pl.BlockSpec((d, GATE_PAD), lambda i, j: (0, 0))],
        out_specs=[pl.BlockSpec((tm, tn), lambda i, j: (i, j)),
                   pl.BlockSpec((tm, GATE_PAD), lambda i, j: (i, 0))],
        out_shape=[jax.ShapeDtypeStruct((m, Z_COLS), BF16),
                   jax.ShapeDtypeStruct((m, GATE_PAD), F32)],
        scratch_shapes=[pltpu.VMEM((tm, d), BF16)],
        compiler_params=_cparams(("arbitrary", "arbitrary"), 48),
        name="inproj",
    )(x, erow, ecol, gm, sh, sc, w, wg)


def _conv3(zc, zp, zn, w, b, first, last):
    tm = zc.shape[0]
    row = lax.broadcasted_iota(I32, zc.shape, 0)
    prev_row = jnp.where(first, 0.0, zp[7:8, :])
    next_row = jnp.where(last, 0.0, zn[0:1, :])
    xm = jnp.where(row == 0, prev_row, pltpu.roll(zc, 1, 0))
    xp = jnp.where(row == tm - 1, next_row, pltpu.roll(zc, tm - 1, 0))
    return xm * w[0:1, :] + zc * w[1:2, :] + xp * w[2:3, :] + b


def _conv_kq_body(zc_ref, zp_ref, zn_ref, w_ref, b_ref, s_ref, o_ref):
    i = pl.program_id(0)
    u = _conv3(zc_ref[...].astype(F32), zp_ref[...].astype(F32), zn_ref[...].astype(F32),
               w_ref[...], b_ref[...], i == 0, i == pl.num_programs(0) - 1)
    o_ref[...] = (_silu(u) * s_ref[...]).astype(BF16)


def _halo_specs(tm, m, cb):
    nb8 = m // 8
    return [pl.BlockSpec((tm, 1024), lambda i: (i, cb)),
            pl.BlockSpec((8, 1024), lambda i: (jnp.maximum(i * (tm // 8) - 1, 0), cb)),
            pl.BlockSpec((8, 1024), lambda i: (jnp.minimum((i + 1) * (tm // 8), nb8 - 1), cb))]


def _conv_kq_call(z, w, b, s, tm):
    m = z.shape[0]
    vec = pl.BlockSpec((1, 1024), lambda i: (0, 0))
    return pl.pallas_call(
        _conv_kq_body,
        grid=(m // tm,),
        in_specs=_halo_specs(tm, m, ZC_KQ) + [pl.BlockSpec((3, 1024), lambda i: (0, 0)), vec, vec],
        out_specs=pl.BlockSpec((tm, 1024), lambda i: (i, 0)),
        out_shape=jax.ShapeDtypeStruct((m, 1024), BF16),
        compiler_params=_cparams(("arbitrary",), 32),
        name="conv_kq",
    )(z, z, z, w, b, s)


def _conv_hy_body(ac_ref, ap_ref, an_ref, bc_ref, bp_ref, bn_ref, cc_ref, cp_ref, cn_ref,
                  w_ref, b_ref, x0_ref, xv_ref):
    i = pl.program_id(0)
    first, last = i == 0, i == pl.num_programs(0) - 1
    w = w_ref[...]
    b = b_ref[...]

    def cv(c, p, n, k):
        return _conv3(c[...].astype(F32), p[...].astype(F32), n[...].astype(F32),
                      w[:, k * 1024:(k + 1) * 1024], b[:, k * 1024:(k + 1) * 1024], first, last)

    x0_ref[...] = cv(ac_ref, ap_ref, an_ref, 0).astype(BF16)
    xv_ref[...] = (cv(bc_ref, bp_ref, bn_ref, 1) * cv(cc_ref, cp_ref, cn_ref, 2)).astype(BF16)


def _conv_hy_call(z, w, b, tm):
    m = z.shape[0]
    out = pl.BlockSpec((tm, 1024), lambda i: (i, 0))
    return pl.pallas_call(
        _conv_hy_body,
        grid=(m // tm,),
        in_specs=(_halo_specs(tm, m, ZC_X0) + _halo_specs(tm, m, ZC_X1) + _halo_specs(tm, m, ZC_HV)
                  + [pl.BlockSpec((3, 3072), lambda i: (0, 0)), pl.BlockSpec((1, 3072), lambda i: (0, 0))]),
        out_specs=[out, out],
        out_shape=[jax.ShapeDtypeStruct((m, 1024), BF16), jax.ShapeDtypeStruct((m, 1024), BF16)],
        compiler_params=_cparams(("arbitrary",), 32),
        name="conv_hy",
    )(z, z, z, z, z, z, z, z, z, w, b)


def _mlstm_body(kqf_ref, vf_ref, gf_ref, kqb_ref, vb_ref, gb_ref, bg_ref, s0_ref, m0_ref,
                hf_ref, hb_ref, sfin_ref, mfin_ref, s_scr, m_scr):
    j = pl.program_id(0)

    @pl.when(j == 0)
    def _():
        s_scr[...] = s0_ref[...]
        m_scr[...] = m0_ref[...]

    r = lax.broadcasted_iota(I32, (CHUNK, CHUNK), 0)
    c = lax.broadcasted_iota(I32, (CHUNK, CHUNK), 1)
    e1 = jnp.where(c == 0, 1.0, 0.0).astype(BF16)
    bg = bg_ref[...]

    for d in range(2):
        kq = (kqf_ref, kqb_ref)[d][...]
        v = (vf_ref, vb_ref)[d][...]
        g_all = (gf_ref, gb_ref)[d][...] + bg
        out_ref = (hf_ref, hb_ref)[d]
        tri = (r >= c) if d == 0 else (c >= r)
        tri_b = jnp.where(tri, 1.0, 0.0).astype(BF16)
        gi = g_all if d == 0 else pltpu.roll(g_all, CHUNK - 16, 1)
        gfp = pltpu.roll(g_all, CHUNK - 8 - 16 * d, 1)
        lf = jnp.minimum(gfp, 0.0) - jnp.log(1.0 + jnp.exp(-jnp.abs(gfp)))
        l1 = lf.astype(BF16)
        r1 = lf - l1.astype(F32)
        l2 = r1.astype(BF16)
        l3 = (r1 - l2.astype(F32)).astype(BF16)
        bcum = _dg(tri_b, l1) + _dg(tri_b, l2) + _dg(tri_b, l3)
        gtot = bcum[CHUNK - 1:CHUNK, :] if d == 0 else bcum[0:1, :]
        acol = gtot - bcum + gi
        m_loc = jnp.max(acol, axis=0, keepdims=True)
        wg = jnp.exp(acol - m_loc)
        m_st = m_scr[d, 0:1, :]
        inter = bcum + m_st
        m_new = jnp.maximum(gtot + m_st, m_loc)
        sp = jnp.broadcast_to(jnp.exp(gtot + m_st - m_new), (CHUNK, CHUNK))
        sl = jnp.broadcast_to(jnp.exp(m_loc - m_new), (CHUNK, CHUNK))
        rt = jnp.transpose(gi - bcum)

        for h in range(N_HEADS):
            p, half = divmod(h, 2)
            lm = (c // QK_HEAD) == half
            kp = kq[:, p * 128:(p + 1) * 128]
            qp = kq[:, QK_W + p * 128:QK_W + (p + 1) * 128]
            vaug = jnp.concatenate([v[:, h * 128:(h + 1) * 128], e1], axis=1)
            qm = jnp.where(lm, qp, jnp.zeros_like(qp))
            dl = jnp.where(tri, bcum[:, h:h + 1] + rt[h:h + 1, :], NEG)
            icol = inter[:, h:h + 1]
            mt = jnp.maximum(icol, jnp.max(dl, axis=1, keepdims=True))
            pm = jnp.exp(dl - mt)
            s = (_dg(qm, kp, _NT) * pm).astype(BF16)
            st = s_scr[d * N_HEADS + h]
            tot = _dg(s, vaug) + jnp.exp(icol - mt) * _dg(qm, st.astype(BF16))
            den = jnp.maximum(jnp.abs(tot[:, 128:129]), jnp.exp(-mt))
            out_ref[:, h * 128:(h + 1) * 128] = (tot[:, :128] / den).astype(BF16)
            kw = jnp.where(lm, kp.astype(F32) * wg[:, h:h + 1], 0.0).astype(BF16)
            s_scr[d * N_HEADS + h] = sp[:, h:h + 1] * st + sl[:, h:h + 1] * _dg(kw, vaug, _TN)
        m_scr[d, 0:1, :] = m_new

    @pl.when(j == pl.num_programs(0) - 1)
    def _():
        sfin_ref[...] = s_scr[...]
        mfin_ref[...] = m_scr[...]


def _mlstm_call(kq, z, gates, bg, s0, m0):
    m = kq.shape[0]
    nc = m // CHUNK
    fwd = lambda cb: (lambda j: (j, cb))
    bwd = lambda cb: (lambda j: (nc - 1 - j, cb))
    st_spec = pl.BlockSpec((2 * N_HEADS, CHUNK, 256), lambda j: (0, 0, 0))
    m_spec = pl.BlockSpec((2, 8, 128), lambda j: (0, 0, 0))
    return pl.pallas_call(
        _mlstm_body,
        grid=(nc,),
        in_specs=[pl.BlockSpec((CHUNK, 1024), fwd(0)), pl.BlockSpec((CHUNK, 1024), fwd(ZC_V)),
                  pl.BlockSpec((CHUNK, GATE_PAD), fwd(0)),
                  pl.BlockSpec((CHUNK, 1024), bwd(0)), pl.BlockSpec((CHUNK, 1024), bwd(ZC_V)),
                  pl.BlockSpec((CHUNK, GATE_PAD), bwd(0)),
                  pl.BlockSpec((1, GATE_PAD), lambda j: (0, 0)), st_spec, m_spec],
        out_specs=[pl.BlockSpec((CHUNK, 1024), fwd(0)), pl.BlockSpec((CHUNK, 1024), bwd(0)), st_spec, m_spec],
        out_shape=[jax.ShapeDtypeStruct((m, 1024), BF16), jax.ShapeDtypeStruct((m, 1024), BF16),
                   jax.ShapeDtypeStruct((2 * N_HEADS, CHUNK, 256), F32),
                   jax.ShapeDtypeStruct((2, 8, 128), F32)],
        scratch_shapes=[pltpu.VMEM((2 * N_HEADS, CHUNK, 256), F32), pltpu.VMEM((2, 8, 128), F32)],
        compiler_params=_cparams(("arbitrary",), 32),
        name="mlstm",
    )(kq, z, gates, kq, z, gates, bg, s0, m0)


def _filt_body(seq_len, ft_ref, w1_ref, b1_ref, f1_ref, w2_ref, b2_ref, f2_ref, w3_ref, b3_ref, rt_ref,
               kf_ref, l1_ref):
    i = pl.program_id(0)
    tn = ft_ref.shape[1]
    hp = tn // 2
    reps = tn // 128
    col = lambda ref: jnp.tile(ref[...], (1, reps))
    h1 = jnp.sin(col(f1_ref) * (_dot3(w1_ref[...], ft_ref[...]) + col(b1_ref)))
    h2 = jnp.sin(col(f2_ref) * (_dot3(w2_ref[...], h1) + col(b2_ref)))
    r = lax.broadcasted_iota(I32, (hp, HY_W), 0)
    n_fwd = (8 * i + (r & 7)) * FFT_N2 + (r >> 3)
    rates = rt_ref[...]
    halves = []
    l1 = jnp.zeros((1, HY_W), F32)
    for hx in range(2):
        h = (_dot3(h2[:, hx * hp:(hx + 1) * hp], w3_ref[:, hx * HY_W:(hx + 1) * HY_W], _TN)
             + b3_ref[:, hx * HY_W:(hx + 1) * HY_W])
        n = n_fwd + hx * seq_len
        t01 = jnp.where(n <= seq_len, n, 2 * seq_len - n).astype(F32) / float(max(seq_len - 1, 1))
        h = jnp.where(n == seq_len, 0.0, h * jnp.exp(-t01 * rates))
        l1 = l1 + jnp.sum(jnp.abs(h), axis=0, keepdims=True)
        halves.append(h)
    word = _pack_pair(halves[0], halves[1])
    for b in range(FFT_N2):
        kf_ref[:, b * HY_W:(b + 1) * HY_W] = word[8 * b:8 * b + 8, :]

    @pl.when(i == 0)
    def _():
        l1_ref[...] = jnp.zeros_like(l1_ref)

    l1_ref[...] += l1


def _filt_call(feats_t, w1t, b1, f1, w2t, b2, f2, w3, b3, rates):
    n = feats_t.shape[1]
    seq_len = n // 2
    tn = 2 * 8 * FFT_N2
    c64 = lambda shape: pl.BlockSpec(shape, lambda i: (0, 0))
    return pl.pallas_call(
        functools.partial(_filt_body, seq_len),
        grid=(n // tn,),
        in_specs=[pl.BlockSpec((64, tn), lambda i: (0, i)),
                  c64((64, 64)), c64((64, 128)), c64((64, 128)), c64((64, 64)), c64((64, 128)), c64((64, 128)),
                  c64((64, 2 * HY_W)), c64((1, 2 * HY_W)), c64((1, HY_W))],
        out_specs=[pl.BlockSpec((8, FFT_N2 * HY_W), lambda i: (i, 0)), pl.BlockSpec((1, HY_W), lambda i: (0, 0))],
        out_shape=[jax.ShapeDtypeStruct((seq_len // FFT_N2, FFT_N2 * HY_W), U32),
                   jax.ShapeDtypeStruct((1, HY_W), F32)],
        compiler_params=_cparams(("arbitrary",), 48),
        name="filt",
    )(feats_t, w1t, b1, f1, w2t, b2, f2, w3, b3, rates)


def _fft_consts(n1_rows):
    n = n1_rows * FFT_N2
    kv = n1_rows // 2 + 1
    k1 = np.arange(FFT_KP, dtype=np.float64)
    valid = (k1 < kv).astype(np.float64)
    n1 = np.arange(n1_rows, dtype=np.float64)
    th1 = 2.0 * np.pi * np.outer(k1, n1) / n1_rows
    f1 = np.concatenate([np.cos(th1) * valid[:, None], -np.sin(th1) * valid[:, None]], axis=0)
    n2 = np.arange(FFT_N2, dtype=np.float64)
    tht = 2.0 * np.pi * np.outer(k1, n2) / n
    rep = lambda a: jnp.broadcast_to(jnp.asarray(a, F32)[:, :, None], (FFT_KP, FFT_N2, 128))
    twr = rep(np.cos(tht) * valid[:, None])
    twi = rep(-np.sin(tht) * valid[:, None])
    th2 = 2.0 * np.pi * np.outer(n2, n2) / FFT_N2
    cs, sn = np.cos(th2), np.sin(th2)
    f2p = np.block([[cs, sn], [-sn, cs]])
    f2pc = np.block([[cs, -sn], [sn, cs]])
    wk = np.where((k1 == 0) | (k1 == kv - 1), 1.0, 2.0) * valid / n
    half = n1_rows // 2
    thi = 2.0 * np.pi * np.outer(n1[:half], k1) / n1_rows
    gc = np.cos(thi) * wk[None, :]
    gs = np.sin(thi) * wk[None, :]
    as_bf = lambda a: jnp.asarray(a, F32).astype(BF16)
    return dict(f1=as_bf(f1), twr=twr, twi=twi,
                f2p=as_bf(f2p), f2pc=as_bf(f2pc), gc=as_bf(gc), gs=as_bf(gs))


def _fft1_body(f_ref, x_ref, ar_ref, ai_ref):
    o = _dg(f_ref[...], x_ref[...])
    ar_ref[...] = o[:FFT_KP].astype(BF16)
    ai_ref[...] = o[FFT_KP:].astype(BF16)


def _fft1_packed_body(f_ref, x_ref, ar_ref, ai_ref):
    k = x_ref.shape[0]
    hi, lo = _unpack_pair(x_ref[...])
    o = _dg(f_ref[:, 0:k], hi.astype(BF16)) + _dg(f_ref[:, k:], lo.astype(BF16))
    ar_ref[...] = o[:FFT_KP].astype(BF16)
    ai_ref[...] = o[FFT_KP:].astype(BF16)


def _fft1_call(f1, x2d, cb, packed=False):
    k, cols = x2d.shape
    f1 = f1[:, :2 * k] if packed else f1[:, :k]
    out = pl.BlockSpec((FFT_KP, cb), lambda i: (0, i))
    sh = jax.ShapeDtypeStruct((FFT_KP, cols), BF16)
    return pl.pallas_call(
        _fft1_packed_body if packed else _fft1_body,
        grid=(cols // cb,),
        in_specs=[pl.BlockSpec(f1.shape, lambda i: (0, 0)), pl.BlockSpec((k, cb), lambda i: (0, i))],
        out_specs=[out, out],
        out_shape=[sh, sh],
        compiler_params=_cparams(("arbitrary",), 32),
        name="fft1",
    )(f1, x2d)


def _twiddled(ar_ref, ai_ref, twr_ref, twi_ref, reps):
    a_r = ar_ref[...].astype(F32)
    a_i = ai_ref[...].astype(F32)
    tr = jnp.tile(twr_ref[...], (1, reps))
    ti = jnp.tile(twi_ref[...], (1, reps))
    st = jnp.concatenate([a_r * tr - a_i * ti, a_r * ti + a_i * tr], axis=0).astype(BF16)
    return st, tr, ti


def _fft2_filt_body(kv, ar_ref, ai_ref, twr_ref, twi_ref, f2p_ref, k_ref):
    @pl.when(pl.program_id(0) < kv)
    def _():
        st, _, _ = _twiddled(ar_ref, ai_ref, twr_ref, twi_ref, ar_ref.shape[-1] // 128)
        k_ref[...] = _dg(f2p_ref[...], st).astype(BF16)

    @pl.when(pl.program_id(0) >= kv)
    def _():
        k_ref[...] = jnp.zeros_like(k_ref)


def _fft2_conv_body(kv, ar_ref, ai_ref, twr_ref, twi_ref, k_ref, f2p_ref, f2pc_ref, br_ref, bi_ref):
    @pl.when(pl.program_id(0) < kv)
    def _():
        st, tr, ti = _twiddled(ar_ref, ai_ref, twr_ref, twi_ref, ar_ref.shape[-1] // 128)
        x = _dg(f2p_ref[...], st)
        xr, xi = x[:FFT_N2], x[FFT_N2:]
        kr = k_ref[:FFT_N2, :].astype(F32)
        ki = k_ref[FFT_N2:, :].astype(F32)
        sy = jnp.concatenate([xr * kr - xi * ki, xr * ki + xi * kr], axis=0).astype(BF16)
        b = _dg(f2pc_ref[...], sy)
        b_r, b_i = b[:FFT_N2], b[FFT_N2:]
        br_ref[...] = (b_r * tr + b_i * ti).astype(BF16)
        bi_ref[...] = (b_i * tr - b_r * ti).astype(BF16)

    @pl.when(pl.program_id(0) >= kv)
    def _():
        br_ref[...] = jnp.zeros_like(br_ref)
        bi_ref[...] = jnp.zeros_like(bi_ref)


def _fft2_specs(ch, kv):
    src = lambda i: (jnp.minimum(i, kv - 1), 0, 0)
    blk = pl.BlockSpec((None, FFT_N2, ch), src)
    tw = pl.BlockSpec((None, FFT_N2, 128), src)
    mat = pl.BlockSpec((2 * FFT_N2, 2 * FFT_N2), lambda i: (0, 0))
    return blk, tw, mat, src


def _fft2_filt_call(ar, ai, cst, kv):
    ch = ar.shape[-1]
    blk, tw, mat, _ = _fft2_specs(ch, kv)
    return pl.pallas_call(
        functools.partial(_fft2_filt_body, kv),
        grid=(FFT_KP,),
        in_specs=[blk, blk, tw, tw, mat],
        out_specs=pl.BlockSpec((None, 2 * FFT_N2, ch), lambda i: (i, 0, 0)),
        out_shape=jax.ShapeDtypeStruct((FFT_KP, 2 * FFT_N2, ch), BF16),
        compiler_params=_cparams(("arbitrary",), 32),
        name="fft2_filt",
    )(ar, ai, cst["twr"], cst["twi"], cst["f2p"])


def _fft2_conv_call(ar, ai, khat, cst, kv):
    ch = ar.shape[-1]
    blk, tw, mat, src = _fft2_specs(ch, kv)
    sh = jax.ShapeDtypeStruct((FFT_KP, FFT_N2, ch), BF16)
    out = pl.BlockSpec((None, FFT_N2, ch), lambda i: (i, 0, 0))
    return pl.pallas_call(
        functools.partial(_fft2_conv_body, kv),
        grid=(FFT_KP,),
        in_specs=[blk, blk, tw, tw, pl.BlockSpec((None, 2 * FFT_N2, ch), src), mat, mat],
        out_specs=[out, out],
        out_shape=[sh, sh],
        compiler_params=_cparams(("arbitrary",), 32),
        name="fft2_conv",
    )(ar, ai, cst["twr"], cst["twi"], khat, cst["f2p"], cst["f2pc"])


def _ifft1_body(gc_ref, gs_ref, br_ref, bi_ref, x0_ref, xv_ref, il_ref, ds_ref, o_ref):
    y = _dg(gc_ref[...], br_ref[...]) - _dg(gs_ref[...], bi_ref[...])
    o_ref[...] = (x0_ref[...].astype(F32)
                  * (y * il_ref[...] + ds_ref[...] * xv_ref[...].astype(F32))).astype(BF16)


def _ifft1_call(cst, br2d, bi2d, x0_2d, xv_2d, il_t, ds_t, cb):
    rows, cols = x0_2d.shape
    g = pl.BlockSpec((rows, FFT_KP), lambda i: (0, 0))
    kb = pl.BlockSpec((FFT_KP, cb), lambda i: (0, i))
    xb = pl.BlockSpec((rows, cb), lambda i: (0, i))
    vb = pl.BlockSpec((1, cb), lambda i: (0, 0))
    return pl.pallas_call(
        _ifft1_body,
        grid=(cols // cb,),
        in_specs=[g, g, kb, kb, xb, xb, vb, vb],
        out_specs=xb,
        out_shape=jax.ShapeDtypeStruct((rows, cols), BF16),
        compiler_params=_cparams(("arbitrary",), 32),
        name="ifft1",
    )(cst["gc"], cst["gs"], br2d, bi2d, x0_2d, xv_2d, il_t, ds_t)


def _outproj_body(hf_ref, hb_ref, zo_ref, yh_ref, x_ref, erow_ref, ecol_ref, gh_ref, wa_ref, wb_ref,
                  g1_ref, gf_ref, sh_ref, sc_ref, wr_ref, x1_ref, h2_ref, s_ref):
    hs = hf_ref[...].astype(F32) + hb_ref[...].astype(F32)
    gh = gh_ref[...]
    parts = []
    for h in range(N_HEADS):
        hh = hs[:, h * 128:(h + 1) * 128]
        ms = jnp.mean(hh * hh, axis=-1, keepdims=True)
        parts.append(hh * lax.rsqrt(ms + EPS) * gh[:, h * 128:(h + 1) * 128])
    ym = jnp.concatenate(parts, axis=-1) * _sigmoid(zo_ref[...].astype(F32))
    y = _dg(ym.astype(BF16), wa_ref[...]) + _dg(yh_ref[...], wb_ref[...])
    rp = x_ref.shape[0] // GRID_W
    erow8 = erow_ref[...]
    erow = erow8[0:rp, :]
    for q in range(1, 8 // rp):
        erow = jnp.where(pl.program_id(0) % (8 // rp) == q, erow8[q * rp:(q + 1) * rp, :], erow)
    x1 = _add_pos(x_ref[...], erow, ecol_ref[...]) + g1_ref[...] * y
    x1_ref[...] = x1
    h2 = _norm_mod(x1, gf_ref[...], sh_ref[...], sc_ref[...])
    half = h2.shape[1] // 2
    _store_slabs(h2_ref, _pack_pair(h2[:, :half], h2[:, half:]))
    s_ref[...] = _sigmoid(_dot3(wr_ref[...], h2, _NT))


def _outproj_call(hf, hb, z, yh, x, erow, ecol, gh, wa, wb, g1, gf, sh2, sc2, wrt, tm):
    m, d = x.shape
    row = lambda cb: pl.BlockSpec((tm, 1024), lambda i: (i, cb))
    vec = lambda n: pl.BlockSpec((1, n), lambda i: (0, 0))
    full = pl.BlockSpec((tm, d), lambda i: (i, 0))
    return pl.pallas_call(
        _outproj_body,
        grid=(m // tm,),
        in_specs=[row(0), row(0), row(ZC_O), row(0), full,
                  pl.BlockSpec((8, d // 2), lambda i: (i * (tm // GRID_W) // 8, 0)),
                  pl.BlockSpec((GRID_W, d // 2), lambda i: (0, 0)),
                  vec(MV_W),
                  pl.BlockSpec((MV_W, d), lambda i: (0, 0)), pl.BlockSpec((HY_W, d), lambda i: (0, 0)),
                  vec(d), vec(d), vec(d), vec(d),
                  pl.BlockSpec((N_EXPERTS, d), lambda i: (0, 0))],
        out_specs=[full, pl.BlockSpec((tm * SLAB, SLAB_W), lambda i: (i, 0)),
                   pl.BlockSpec((N_EXPERTS, tm), lambda i: (0, i))],
        out_shape=[jax.ShapeDtypeStruct((m, d), F32), jax.ShapeDtypeStruct((m * SLAB, SLAB_W), U32),
                   jax.ShapeDtypeStruct((N_EXPERTS, m), F32)],
        compiler_params=_cparams(("arbitrary",), 56),
        name="outproj",
    )(hf, hb, z, yh, x, erow, ecol, gh, wa, wb, g1, gf, sh2, sc2, wrt)


def _first_max(x, idx, sentinel):
    m = jnp.max(x, axis=0, keepdims=True)
    return m, jnp.min(jnp.where(x == m, idx, sentinel), axis=0, keepdims=True)


def _route_body(s_ref, b_ref, e_ref, w_ref, r_ref, cnt_ref, u_scr, run_scr):
    i = pl.program_id(0)
    tt = s_ref.shape[1]

    @pl.when(i == 0)
    def _():
        rr = lax.broadcasted_iota(I32, (tt, tt), 0)
        cc = lax.broadcasted_iota(I32, (tt, tt), 1)
        u_scr[...] = jnp.where(rr < cc, 1.0, 0.0).astype(BF16)
        run_scr[...] = jnp.zeros_like(run_scr)

    s = s_ref[...]
    sel = s + b_ref[...][:, 0:1]
    sub8 = lax.broadcasted_iota(I32, (E_PER_GROUP, tt), 0).astype(F32)
    gs = jnp.zeros((N_GROUPS, tt), F32)
    for g in range(N_GROUPS):
        grp = sel[g * E_PER_GROUP:(g + 1) * E_PER_GROUP, :]
        m1, i1 = _first_max(grp, sub8, float(E_PER_GROUP))
        m2 = jnp.max(jnp.where(sub8 == i1, -jnp.inf, grp), axis=0, keepdims=True)
        gs = jnp.where(sub8 == g, m1 + m2, gs)
    gmask = jnp.zeros((N_GROUPS, tt), F32)
    for _ in range(TOPK_GROUPS):
        _, ig = _first_max(gs, sub8, float(N_GROUPS))
        hit = sub8 == ig
        gmask = jnp.where(hit, 1.0, gmask)
        gs = jnp.where(hit, -jnp.inf, gs)
    masked = jnp.concatenate(
        [jnp.where(jnp.broadcast_to(gmask[g:g + 1, :], (E_PER_GROUP, tt)) > 0.5,
                   sel[g * E_PER_GROUP:(g + 1) * E_PER_GROUP, :], -jnp.inf) for g in range(N_GROUPS)], axis=0)
    sub64 = lax.broadcasted_iota(I32, (N_EXPERTS, tt), 0).astype(F32)
    oh = jnp.zeros((N_EXPERTS, tt), F32)
    eks, wks = [], []
    for _ in range(TOP_K):
        _, ie = _first_max(masked, sub64, float(N_EXPERTS))
        hit = sub64 == ie
        wks.append(jnp.sum(jnp.where(hit, s, 0.0), axis=0, keepdims=True))
        eks.append(ie)
        masked = jnp.where(hit, -jnp.inf, masked)
        oh = jnp.where(hit, 1.0, oh)
    wsum = wks[0]
    for k in range(1, TOP_K):
        wsum = wsum + wks[k]
    run = run_scr[...]
    rank_t = _dg(oh.astype(BF16), u_scr[...]) + jnp.tile(run, (1, tt // 128))
    for k in range(TOP_K):
        e_ref[k:k + 1, :] = eks[k].astype(I32)
        w_ref[k:k + 1, :] = wks[k] / wsum * ROUTE_SCALE
        r_ref[k:k + 1, :] = jnp.sum(jnp.where(sub64 == eks[k], rank_t, 0.0), axis=0, keepdims=True).astype(I32)
    run_new = run + jnp.sum(oh, axis=1, keepdims=True)
    run_scr[...] = run_new
    cnt_ref[...] = run_new.astype(I32)


def _route_call(s_t, b_col, tt):
    m = s_t.shape[1]
    out = pl.BlockSpec((TOP_K, tt), lambda i: (0, i))
    return pl.pallas_call(
        _route_body,
        grid=(m // tt,),
        in_specs=[pl.BlockSpec((N_EXPERTS, tt), lambda i: (0, i)),
                  pl.BlockSpec((N_EXPERTS, 128), lambda i: (0, 0))],
        out_specs=[out, out, out, pl.BlockSpec((N_EXPERTS, 128), lambda i: (0, 0))],
        out_shape=[jax.ShapeDtypeStruct((TOP_K, m), I32), jax.ShapeDtypeStruct((TOP_K, m), F32),
                   jax.ShapeDtypeStruct((TOP_K, m), I32), jax.ShapeDtypeStruct((N_EXPERTS, 128), I32)],
        scratch_shapes=[pltpu.VMEM((tt, tt), BF16), pltpu.VMEM((N_EXPERTS, 128), F32)],
        compiler_params=_cparams(("arbitrary",), 32),
        name="route",
    )(s_t, b_col)


def _posk_body(pst_ref, e_ref, r_ref, p_ref):
    e = e_ref[...]
    acc = r_ref[...]
    for x in range(N_EXPERTS):
        acc = acc + jnp.where(e == x, pst_ref[x], 0)
    p_ref[...] = acc


def _posk_call(pstart, eidx, rank):
    k, m = eidx.shape
    tt = min(m, 2048)
    blk = pl.BlockSpec((k, tt), lambda i, pst: (0, i))
    return pl.pallas_call(
        _posk_body,
        grid_spec=pltpu.PrefetchScalarGridSpec(num_scalar_prefetch=1, grid=(m // tt,),
                                               in_specs=[blk, blk], out_specs=blk),
        out_shape=jax.ShapeDtypeStruct((k, m), I32),
        compiler_params=_cparams(("arbitrary",), 32),
        name="posk",
    )(pstart, eidx, rank)


def _slab(ref, r):
    return ref.at[pl.ds(pl.multiple_of(r * SLAB, SLAB), SLAB), :]


def _ffn_packed(x_ref, rows, w1, w3, w2):
    half = w1.shape[0] // 2
    xa, xb = _unpack_pair(_load_slabs(x_ref, rows))
    xa = xa.astype(BF16)
    xb = xb.astype(BF16)
    h1 = _dg(xa, w1[0:half, :]) + _dg(xb, w1[half:, :])
    h3 = _dg(xa, w3[0:half, :]) + _dg(xb, w3[half:, :])
    return _dg((_silu(h1) * h3).astype(BF16), w2[...])


def _dispatch_body(cnt_ref, pst_ref, pcn_ref, h2_ref, pos_ref, w1_ref, w3_ref, w2_ref,
                   xs_ref, sh_ref, zrow, sem):
    i = pl.program_id(0)
    td = h2_ref.shape[0] // SLAB

    def row_copy(t, dst):
        return pltpu.make_async_copy(_slab(h2_ref, t), _slab(xs_ref, dst), sem)

    def issue(t, carry):
        for k in range(TOP_K):
            row_copy(t, pos_ref[k, t]).start(priority=k % 2)
        return carry

    lax.fori_loop(0, td, issue, 0)

    sh_ref[...] = _ffn_packed(h2_ref, td, w1_ref, w3_ref, w2_ref).astype(BF16)

    def drain(t, carry):
        for k in range(TOP_K):
            row_copy(0, 0).wait()
        return carry

    lax.fori_loop(0, td, drain, 0)

    @pl.when(i == pl.num_programs(0) - 1)
    def _():
        zrow[...] = jnp.zeros_like(zrow)

        def zero_copy(dst):
            return pltpu.make_async_copy(zrow, _slab(xs_ref, dst), sem)

        def per_expert(e, carry):
            base = pst_ref[e]
            lax.fori_loop(cnt_ref[e], pcn_ref[e], lambda rr, cc: (zero_copy(base + rr).start(), cc)[1], 0)
            lax.fori_loop(cnt_ref[e], pcn_ref[e], lambda rr, cc: (zero_copy(0).wait(), cc)[1], 0)
            return carry

        lax.fori_loop(0, N_EXPERTS, per_expert, 0)


def _dispatch_call(cnt, pstart, pcnt, h2s, pos, w1s, w3s, w2s, rows, td):
    m = h2s.shape[0] // SLAB
    d, ds = w1s.shape
    return pl.pallas_call(
        _dispatch_body,
        grid_spec=pltpu.PrefetchScalarGridSpec(
            num_scalar_prefetch=3, grid=(m // td,),
            in_specs=[pl.BlockSpec((td * SLAB, SLAB_W), lambda i, *_: (i, 0)),
                      pl.BlockSpec((TOP_K, td), lambda i, *_: (0, i), memory_space=pltpu.SMEM),
                      pl.BlockSpec((d, ds), lambda i, *_: (0, 0)), pl.BlockSpec((d, ds), lambda i, *_: (0, 0)),
                      pl.BlockSpec((ds, d), lambda i, *_: (0, 0))],
            out_specs=[pl.BlockSpec(memory_space=pl.ANY), pl.BlockSpec((td, d), lambda i, *_: (i, 0))],
            scratch_shapes=[pltpu.VMEM((SLAB, SLAB_W), U32), pltpu.SemaphoreType.DMA(())]),
        out_shape=[jax.ShapeDtypeStruct((rows * SLAB, SLAB_W), U32), jax.ShapeDtypeStruct((m, d), BF16)],
        compiler_params=_cparams(("arbitrary",), 40),
        name="dispatch",
    )(cnt, pstart, pcnt, h2s, pos, w1s, w3s, w2s)


def _moe_body(te_ref, nu_ref, nxt_ref, par_ref, x_ref, w1_ref, w3_ref, w2_ref, y_ref,
              f1, f3, f2, w1b, w3b, w2b, sems):
    i = pl.program_id(0)
    used = i < nu_ref[0]
    e = te_ref[i]
    first = jnp.logical_or(i == 0, e != te_ref[jnp.maximum(i - 1, 0)])
    slot = par_ref[e]

    def fetch(ex, s):
        return (pltpu.make_async_copy(w1_ref.at[ex], f1.at[s], sems.at[s]),
                pltpu.make_async_copy(w3_ref.at[ex], f3.at[s], sems.at[s]),
                pltpu.make_async_copy(w2_ref.at[ex], f2.at[s], sems.at[s]))

    @pl.when(jnp.logical_and(used, i == 0))
    def _():
        for c in fetch(e, slot):
            c.start()

    @pl.when(jnp.logical_and(used, first))
    def _():
        for c in fetch(e, slot):
            c.wait()
        nx = nxt_ref[e]

        @pl.when(nx < N_EXPERTS)
        def _():
            for c in fetch(nx, 1 - slot):
                c.start()

        w1b[...] = f1[slot].astype(BF16)
        w3b[...] = f3[slot].astype(BF16)
        w2b[...] = f2[slot].astype(BF16)

    @pl.when(used)
    def _():
        half = w1b.shape[0] // 2
        y = _ffn_packed(x_ref, MOE_ROWS, w1b, w3b, w2b)
        _store_slabs(y_ref, _pack_pair(y[:, :half], y[:, half:]))


def _moe_call(tile_e, n_used, nxt, par, xs, w1, w3, w2):
    rows = xs.shape[0] // SLAB
    nt = rows // MOE_ROWS
    d, de = w1.shape[-2:]
    rmap = lambda i, te, nu, *_: (jnp.minimum(i, nu[0] - 1), 0)
    hbm = pl.BlockSpec(memory_space=pl.ANY)
    return pl.pallas_call(
        _moe_body,
        grid_spec=pltpu.PrefetchScalarGridSpec(
            num_scalar_prefetch=4, grid=(nt,),
            in_specs=[pl.BlockSpec((MOE_ROWS * SLAB, SLAB_W), rmap), hbm, hbm, hbm],
            out_specs=pl.BlockSpec((MOE_ROWS * SLAB, SLAB_W), rmap),
            scratch_shapes=[pltpu.VMEM((2, d, de), F32), pltpu.VMEM((2, d, de), F32), pltpu.VMEM((2, de, d), F32),
                            pltpu.VMEM((d, de), BF16), pltpu.VMEM((d, de), BF16), pltpu.VMEM((de, d), BF16),
                            pltpu.SemaphoreType.DMA((2,))]),
        out_shape=jax.ShapeDtypeStruct((rows * SLAB, SLAB_W), U32),
        compiler_params=_cparams(("arbitrary",), 56),
        name="moe",
    )(tile_e, n_used, nxt, par, xs, w1, w3, w2)


def _final_body(x1_ref, sh_ref, pos_ref, posn_ref, wt_ref, ys_ref, g2_ref, gn_ref, o_ref, ybuf, sems):
    i = pl.program_id(0)
    n = pl.num_programs(0)
    tf, d = x1_ref.shape
    half = d // 2
    slot = i % 2
    slot_slabs = TOP_K * tf

    def row_copy(p_ref, s, k, t):
        return pltpu.make_async_copy(_slab(ys_ref, p_ref[k, t]), _slab(ybuf, s * slot_slabs + k * tf + t),
                                     sems.at[s])

    def issue(p_ref, s):
        def body(t, carry):
            for k in range(TOP_K):
                row_copy(p_ref, s, k, t).start(priority=k % 2)
            return carry

        lax.fori_loop(0, tf, body, 0)

    @pl.when(i == 0)
    def _():
        issue(pos_ref, 0)

    @pl.when(i + 1 < n)
    def _():
        issue(posn_ref, 1 - slot)

    def drain(t, carry):
        for k in range(TOP_K):
            pltpu.make_async_copy(_slab(ys_ref, 0), _slab(ybuf, 0), sems.at[slot]).wait()
        return carry

    lax.fori_loop(0, tf, drain, 0)

    wt = jnp.concatenate([jnp.transpose(jnp.concatenate(
        [wt_ref[:, c * 128:(c + 1) * 128], jnp.zeros((128 - TOP_K, 128), F32)], axis=0)) for c in range(tf // 128)],
        axis=0)
    shared = sh_ref[...].astype(F32)
    acc_a = shared[:, :half]
    acc_b = shared[:, half:]
    base = slot * slot_slabs * SLAB
    for k in range(TOP_K):
        ya, yb = _unpack_pair(_load_slabs(ybuf, tf, base=base + k * tf * SLAB))
        acc_a = acc_a + wt[:, k:k + 1] * ya
        acc_b = acc_b + wt[:, k:k + 1] * yb
    xo = x1_ref[...] + g2_ref[...] * jnp.concatenate([acc_a, acc_b], axis=1)
    ms = jnp.mean(xo * xo, axis=-1, keepdims=True)
    o_ref[...] = xo * lax.rsqrt(ms + EPS) * gn_ref[...]


def _final_call(x1, sh, pos, wts, ys, g2, gn, tf):
    m, d = x1.shape
    nt = m // tf
    full = pl.BlockSpec((tf, d), lambda i: (i, 0))
    vec = pl.BlockSpec((1, d), lambda i: (0, 0))
    return pl.pallas_call(
        _final_body,
        grid=(nt,),
        in_specs=[full, full,
                  pl.BlockSpec((TOP_K, tf), lambda i: (0, i), memory_space=pltpu.SMEM),
                  pl.BlockSpec((TOP_K, tf), lambda i: (0, jnp.minimum(i + 1, nt - 1)), memory_space=pltpu.SMEM),
                  pl.BlockSpec((TOP_K, tf), lambda i: (0, i)),
                  pl.BlockSpec(memory_space=pl.ANY), vec, vec],
        out_specs=full,
        out_shape=jax.ShapeDtypeStruct((m, d), F32),
        scratch_shapes=[pltpu.VMEM((2 * TOP_K * tf * SLAB, SLAB_W), U32), pltpu.SemaphoreType.DMA((2,))],
        compiler_params=_cparams(("arbitrary",), 48),
        name="final",
    )(x1, sh, pos, pos, wts, ys, g2, gn)


def _pos_tables(n_tokens):
    rows = n_tokens // GRID_W
    quarter = D_MODEL // 4
    omega = 1.0 / (10000.0 ** (jnp.arange(quarter, dtype=F32) / quarter))

    def emb1d(pos):
        ang = pos[:, None] * omega[None]
        return jnp.concatenate([jnp.sin(ang), jnp.cos(ang)], axis=-1)

    return emb1d(jnp.arange(rows, dtype=F32)), emb1d(jnp.arange(GRID_W, dtype=F32))


def _filter_feats(L):
    n1h = L // FFT_N2
    i_, h_, b_, a_ = jnp.meshgrid(jnp.arange(n1h // 8, dtype=I32), jnp.arange(2, dtype=I32),
                                  jnp.arange(FFT_N2, dtype=I32), jnp.arange(8, dtype=I32), indexing="ij")
    n = ((8 * i_ + a_ + h_ * n1h) * FFT_N2 + b_).reshape(-1)
    t = jnp.where(n <= L, n, 2 * L - n).astype(F32)
    t01 = t / max(L - 1, 1)
    w = 2.0 * math.pi * t / L
    bands = jnp.linspace(1e-4, FILT_BANDS - 1, FILT_BANDS, dtype=F32)
    feats = jnp.concatenate([t01[None, :], jnp.cos(bands[:, None] * w[None, :]), -jnp.sin(bands[:, None] * w[None, :]),
                             jnp.zeros((64 - 33, 2 * L), F32)], axis=0)
    return feats


def _pad_rows(a, rows):
    return jnp.concatenate([a, jnp.zeros((rows - a.shape[0],) + a.shape[1:], a.dtype)], axis=0)


def _layer(x, c, ctx, c_ctx, w_ada, b_ada, g_mix, g_ffn, w_in, b_gates, conv_k_w, conv_k_b,
           conv_q_w, conv_q_b, g_head, conv_hy_w, conv_hy_b, filt_w1, filt_b1, filt_freq1,
           filt_w2, filt_b2, filt_freq2, filt_w3, filt_b3, hy_dskip, w_out, w_router, b_router,
           w1_e, w3_e, w2_e, w1_s, w3_s, w2_s, g_final):
    L, d = x.shape
    lc = ctx.shape[0]
    row = lambda v: v.reshape(1, -1)

    cc = _pad_rows(jnp.stack([c, c_ctx], axis=0), 8)
    mods = _mod_call(cc, w_ada, row(b_ada))
    sh1, sc1, g1, sh2, sc2, g2 = [mods[0:1, k * d:(k + 1) * d] for k in range(6)]
    csh1, csc1 = mods[1:2, 0:d], mods[1:2, d:2 * d]

    w_r, w_g = _wprep_call(w_in)
    bg = jnp.concatenate([b_gates, jnp.zeros((GATE_PAD - 4 * N_HEADS,), F32)]).reshape(1, GATE_PAD)
    e_row, e_col = _pos_tables(L)
    conv_w = jnp.concatenate([conv_k_w, conv_q_w], axis=1)
    conv_b = jnp.concatenate([conv_k_b, conv_q_b]).reshape(1, -1)
    conv_s = jnp.concatenate([jnp.ones((QK_W,), F32), jnp.full((QK_W,), QK_HEAD ** -0.5, F32)]).reshape(1, -1)

    z_c, gt_c = _inproj_call(ctx, jnp.zeros((8, d // 2), F32), e_col, row(g_mix), csh1, csc1, w_r, w_g,
                             use_pos=False, tm=min(lc, 256))
    kq_c = _conv_kq_call(z_c, conv_w, conv_b, conv_s, tm=min(lc, 256))
    s0 = jnp.zeros((2 * N_HEADS, CHUNK, 256), F32)
    m0 = jnp.zeros((2, 8, 128), F32)
    _, _, s_ctx, m_ctx = _mlstm_call(kq_c, z_c, gt_c, bg, s0, m0)

    z, gates = _inproj_call(x, e_row, e_col, row(g_mix), sh1, sc1, w_r, w_g, use_pos=True, tm=min(L, 1024))
    kq = _conv_kq_call(z, conv_w, conv_b, conv_s, tm=min(L, 512))
    x0c, xv = _conv_hy_call(z, conv_hy_w, row(conv_hy_b), tm=min(L, 512))
    hf, hb, _, _ = _mlstm_call(kq, z, gates, bg, s_ctx, m_ctx)

    n1 = 2 * L // FFT_N2
    cst = _fft_consts(n1)
    rates = jnp.linspace(-math.log(DECAY_TARGET) / SLOW_DECAY_PCT, -math.log(DECAY_TARGET) / FAST_DECAY_PCT,
                         HY_W, dtype=F32).reshape(1, -1)
    w1t = jnp.transpose(_pad_rows(filt_w1, 64))
    colrep = lambda v: jnp.broadcast_to(v.reshape(-1, 1), (v.shape[0], 128))
    kf, l1 = _filt_call(_filter_feats(L), w1t, colrep(filt_b1), colrep(filt_freq1), jnp.transpose(filt_w2),
                        colrep(filt_b2), colrep(filt_freq2), filt_w3, row(filt_b3), rates)
    cols = FFT_N2 * HY_W
    cb = 2048
    kar, kai = _fft1_call(cst["f1"], kf, cb, packed=True)
    kv = n1 // 2 + 1
    khat = _fft2_filt_call(kar.reshape(FFT_KP, FFT_N2, HY_W), kai.reshape(FFT_KP, FFT_N2, HY_W), cst, kv)
    uar, uai = _fft1_call(cst["f1"], xv.reshape(n1 // 2, cols), cb)
    br, bi = _fft2_conv_call(uar.reshape(FFT_KP, FFT_N2, HY_W), uai.reshape(FFT_KP, FFT_N2, HY_W), khat, cst, kv)
    reps = cb // HY_W
    il_t = jnp.tile(1.0 / l1, (1, reps))
    ds_t = jnp.tile(row(hy_dskip), (1, reps))
    yh = _ifft1_call(cst, br.reshape(FFT_KP, cols), bi.reshape(FFT_KP, cols),
                     x0c.reshape(n1 // 2, cols), xv.reshape(n1 // 2, cols), il_t, ds_t, cb).reshape(L, HY_W)

    wo = w_out.astype(BF16)
    x1, h2s, s_t = _outproj_call(hf, hb, z, yh, x, e_row, e_col, row(g_head), wo[:MV_W], wo[MV_W:],
                                g1, row(g_ffn), sh2, sc2, jnp.transpose(w_router), tm=min(L, 256))

    b_col = jnp.broadcast_to(b_router.reshape(N_EXPERTS, 1), (N_EXPERTS, 128))
    eidx, wts, rank, cnt2 = _route_call(s_t, b_col, tt=min(L, 1024))
    cnt = cnt2[:, 0]
    pcnt = (cnt + MOE_ROWS - 1) // MOE_ROWS * MOE_ROWS
    pend = jnp.cumsum(pcnt)
    pstart = pend - pcnt
    rows = L * TOP_K + N_EXPERTS * MOE_ROWS
    nt = rows // MOE_ROWS
    tile_row = jnp.arange(nt, dtype=I32) * MOE_ROWS
    tile_e = jnp.minimum(jnp.sum((pend[None, :] <= tile_row[:, None]).astype(I32), axis=1), N_EXPERTS - 1)
    n_used = (pend[-1] // MOE_ROWS).astype(I32).reshape(1)
    pos = _posk_call(pstart.astype(I32), eidx, rank)

    xs, sh = _dispatch_call(cnt, pstart.astype(I32), pcnt.astype(I32), h2s, pos, w1_s.astype(BF16),
                            w3_s.astype(BF16), w2_s.astype(BF16), rows, td=min(L, 256))
    ex = jnp.arange(N_EXPERTS, dtype=I32)
    nonempty = pcnt > 0
    nxt = jnp.min(jnp.where((ex[None, :] > ex[:, None]) & nonempty[None, :], ex[None, :], N_EXPERTS), axis=1)
    par = (jnp.cumsum(nonempty.astype(I32)) + 1) % 2
    ys = _moe_call(tile_e, n_used, nxt.astype(I32), par.astype(I32), xs, w1_e, w3_e, w2_e)
    return _final_call(x1, sh, pos, wts, ys, g2, row(g_final), tf=min(L, 256))


def kernel(x, c, ctx, c_ctx, w_ada, b_ada, g_mix, g_ffn, w_in, b_gates, conv_k_w, conv_k_b, conv_q_w,
           conv_q_b, g_head, conv_hy_w, conv_hy_b, filt_w1, filt_b1, filt_freq1, filt_w2, filt_b2,
           filt_freq2, filt_w3, filt_b3, hy_dskip, w_out, w_router, b_router, w1_e, w3_e, w2_e,
           w1_s, w3_s, w2_s, g_final):
    assert x.shape[0] == 1 and w_ada.shape[0] == 1, "one batch element, one layer"
    out = _layer(x[0], c[0], ctx[0], c_ctx, w_ada[0], b_ada[0], g_mix[0], g_ffn[0], w_in, b_gates[0],
                 conv_k_w[0], conv_k_b[0], conv_q_w[0], conv_q_b[0], g_head[0], conv_hy_w[0], conv_hy_b[0],
                 filt_w1[0], filt_b1[0], filt_freq1[0], filt_w2[0], filt_b2[0], filt_freq2[0], filt_w3[0],
                 filt_b3[0], hy_dskip[0], w_out[0], w_router[0], b_router[0], w1_e[0], w3_e[0], w2_e[0],
                 w1_s[0], w3_s[0], w2_s[0], g_final)
    return out[None]
```

```python
import functools
import math

import numpy as np
import jax
import jax.numpy as jnp
from jax import lax
from jax.experimental import pallas as pl
from jax.experimental.pallas import tpu as pltpu

F32 = jnp.float32
BF16 = jnp.bfloat16
I32 = jnp.int32
U32 = jnp.uint32

D_MODEL = 2048
GRID_W = 64
N_HEADS = 8
QK_HEAD = 64
V_HEAD = 128
QK_W = N_HEADS * QK_HEAD
MV_W = N_HEADS * V_HEAD
HY_W = D_MODEL - MV_W
CHUNK = 128
FILT_BANDS = 16
FILT_HIDDEN = 64
DECAY_TARGET = 1e-2
FAST_DECAY_PCT = 0.3
SLOW_DECAY_PCT = 1.5
N_EXPERTS = 64
N_GROUPS = 8
E_PER_GROUP = 8
TOPK_GROUPS = 4
TOP_K = 8
D_EXPERT = 512
ROUTE_SCALE = 2.5
EPS = 1e-6
OFF_K = 0
OFF_V = OFF_K + QK_W
OFF_G = OFF_V + MV_W
OFF_Q = OFF_G + 4 * N_HEADS
OFF_O = OFF_Q + QK_W
OFF_HY = OFF_O + MV_W

ZC_KQ, ZC_V, ZC_O, ZC_X0, ZC_X1, ZC_HV = 0, 1, 2, 3, 4, 5
Z_COLS = 6 * 1024
GATE_PAD = 128

NEG = -1e30
MIB = 1024 * 1024

FFT_N2 = 128
FFT_KP = 144

MOE_ROWS = 256


def _cparams(sem, vmem_mb, flags=None):
    return pltpu.CompilerParams(dimension_semantics=sem, vmem_limit_bytes=vmem_mb * MIB, flags=flags)


def _split2(x):
    hi = x.astype(BF16)
    lo = (x - hi.astype(F32)).astype(BF16)
    return hi, lo


_NN = (((1,), (0,)), ((), ()))
_NT = (((1,), (1,)), ((), ()))
_TN = (((0,), (0,)), ((), ()))


def _dg(a, b, dims=_NN):
    return lax.dot_general(a, b, dims, preferred_element_type=F32)


def _dot3(a, b, dims=_NN):
    ah, al = _split2(a)
    bh, bl = _split2(b)
    return _dg(ah, bh, dims) + _dg(al, bh, dims) + _dg(ah, bl, dims)


def _sigmoid(x):
    return 1.0 / (1.0 + jnp.exp(-x))


def _silu(x):
    return x * _sigmoid(x)


def _pack_pair(a, b):
    hi = lax.bitcast_convert_type(a.astype(BF16).astype(F32), U32)
    lo = lax.bitcast_convert_type(b.astype(BF16).astype(F32), U32)
    return hi | (lo >> 16)


def _unpack_pair(w):
    a = lax.bitcast_convert_type(w & jnp.uint32(0xFFFF0000), F32)
    b = lax.bitcast_convert_type(w << 16, F32)
    return a, b


SLAB = 8
SLAB_W = 128


def _store_slabs(ref, w, base=0):
    r = w.shape[0]
    for j in range(SLAB):
        ref[pl.ds(base + j, r, stride=SLAB), :] = w[:, j * SLAB_W:(j + 1) * SLAB_W]


def _load_slabs(ref, r, base=0):
    return jnp.concatenate([ref[pl.ds(base + j, r, stride=SLAB), :] for j in range(SLAB)], axis=1)


def _norm_mod(x, g, sh, sc):
    ms = jnp.mean(x * x, axis=-1, keepdims=True)
    return (x * lax.rsqrt(ms + EPS) * g) * (1.0 + sc) + sh


def _add_pos(x, erow, ecol):
    tm, d = x.shape
    half = d // 2
    parts = []
    for r in range(tm // GRID_W):
        xs = x[r * GRID_W:(r + 1) * GRID_W, :]
        parts.append(jnp.concatenate([xs[:, :half] + erow[r:r + 1, :], xs[:, half:] + ecol], axis=-1))
    return parts[0] if len(parts) == 1 else jnp.concatenate(parts, axis=0)


def _mod_body(cc_ref, w_ref, b_ref, o_ref):
    o_ref[...] = _dot3(_silu(cc_ref[...]), w_ref[...]) + b_ref[...]


def _mod_call(cc, w, b):
    d, n = w.shape
    tn = 1024
    return pl.pallas_call(
        _mod_body,
        grid=(n // tn,),
        in_specs=[pl.BlockSpec((8, d), lambda j: (0, 0)),
                  pl.BlockSpec((d, tn), lambda j: (0, j)),
                  pl.BlockSpec((1, tn), lambda j: (0, j))],
        out_specs=pl.BlockSpec((8, tn), lambda j: (0, j)),
        out_shape=jax.ShapeDtypeStruct((8, n), F32),
        compiler_params=_cparams(("arbitrary",), 40),
        name="mod",
    )(cc, w, b)


def _wprep_body(w_ref, wr_ref, wg_ref):
    w = w_ref[...]
    wt = jnp.concatenate([w[OFF_K:OFF_V], w[OFF_Q:OFF_O], w[OFF_V:OFF_G], w[OFF_O:]], axis=0)
    wr_ref[...] = jnp.transpose(wt).astype(BF16)
    g = jnp.concatenate([w[OFF_G:OFF_Q], jnp.zeros((GATE_PAD - 4 * N_HEADS, w.shape[1]), F32)], axis=0)
    wg_ref[...] = jnp.transpose(g)


def _wprep_call(w_t):
    n, d = w_t.shape
    tr = 256
    return pl.pallas_call(
        _wprep_body,
        grid=(d // tr,),
        in_specs=[pl.BlockSpec((n, tr), lambda i: (0, i))],
        out_specs=[pl.BlockSpec((tr, Z_COLS), lambda i: (i, 0)), pl.BlockSpec((tr, GATE_PAD), lambda i: (i, 0))],
        out_shape=[jax.ShapeDtypeStruct((d, Z_COLS), BF16), jax.ShapeDtypeStruct((d, GATE_PAD), F32)],
        compiler_params=_cparams(("arbitrary",), 32),
        name="wprep",
    )(w_t)


def _inproj_body(use_pos, x_ref, erow_ref, ecol_ref, gm_ref, sh_ref, sc_ref, w_ref, wg_ref,
                 z_ref, g_ref, h_scr):
    @pl.when(pl.program_id(1) == 0)
    def _():
        x = x_ref[...]
        if use_pos:
            x = _add_pos(x, erow_ref[...], ecol_ref[...])
        h = _norm_mod(x, gm_ref[...], sh_ref[...], sc_ref[...])
        h_scr[...] = h.astype(BF16)
        g_ref[...] = _dot3(h, wg_ref[...])

    z_ref[...] = jnp.dot(h_scr[...], w_ref[...], preferred_element_type=F32).astype(BF16)


def _inproj_call(x, erow, ecol, gm, sh, sc, w, wg, use_pos, tm):
    m, d = x.shape
    tn = 1024
    er = tm // GRID_W if use_pos else erow.shape[0]
    row_map = (lambda i, j: (i, 0)) if use_pos else (lambda i, j: (0, 0))
    return pl.pallas_call(
        functools.partial(_inproj_body, use_pos),
        grid=(m // tm, Z_COLS // tn),
        in_specs=[pl.BlockSpec((tm, d), lambda i, j: (i, 0)),
                  pl.BlockSpec((er, d // 2), row_map),
                  pl.BlockSpec((GRID_W, d // 2), lambda i, j: (0, 0)),
                  pl.BlockSpec((1, d), lambda i, j: (0, 0)),
                  pl.BlockSpec((1, d), lambda i, j: (0, 0)),
                  pl.BlockSpec((1, d), lambda i, j: (0, 0)),
                  pl.BlockSpec((d, tn), lambda i, j: (0, j)),
                  pl.BlockSpec((d, GATE_PAD), lambda i, j: (0, 0))],
        out_specs=[pl.BlockSpec((tm, tn), lambda i, j: (i, j)),
                   pl.BlockSpec((tm, GATE_PAD), lambda i, j: (i, 0))],
        out_shape=[jax.ShapeDtypeStruct((m, Z_COLS), BF16),
                   jax.ShapeDtypeStruct((m, GATE_PAD), F32)],
        scratch_shapes=[pltpu.VMEM((tm, d), BF16)],
        compiler_params=_cparams(("arbitrary", "arbitrary"), 48),
        name="inproj",
    )(x, erow, ecol, gm, sh, sc, w, wg)


def _conv3(zc, zp, zn, w, b, first, last):
    tm = zc.shape[0]
    row = lax.broadcasted_iota(I32, zc.shape, 0)
    prev_row = jnp.where(first, 0.0, zp[7:8, :])
    next_row = jnp.where(last, 0.0, zn[0:1, :])
    xm = jnp.where(row == 0, prev_row, pltpu.roll(zc, 1, 0))
    xp = jnp.where(row == tm - 1, next_row, pltpu.roll(zc, tm - 1, 0))
    return xm * w[0:1, :] + zc * w[1:2, :] + xp * w[2:3, :] + b


def _conv_kq_body(zc_ref, zp_ref, zn_ref, w_ref, b_ref, s_ref, o_ref):
    i = pl.program_id(0)
    u = _conv3(zc_ref[...].astype(F32), zp_ref[...].astype(F32), zn_ref[...].astype(F32),
               w_ref[...], b_ref[...], i == 0, i == pl.num_programs(0) - 1)
    o_ref[...] = (_silu(u) * s_ref[...]).astype(BF16)


def _halo_specs(tm, m, cb):
    nb8 = m // 8
    return [pl.BlockSpec((tm, 1024), lambda i: (i, cb)),
            pl.BlockSpec((8, 1024), lambda i: (jnp.maximum(i * (tm // 8) - 1, 0), cb)),
            pl.BlockSpec((8, 1024), lambda i: (jnp.minimum((i + 1) * (tm // 8), nb8 - 1), cb))]


def _conv_kq_call(z, w, b, s, tm):
    m = z.shape[0]
    vec = pl.BlockSpec((1, 1024), lambda i: (0, 0))
    return pl.pallas_call(
        _conv_kq_body,
        grid=(m // tm,),
        in_specs=_halo_specs(tm, m, ZC_KQ) + [pl.BlockSpec((3, 1024), lambda i: (0, 0)), vec, vec],
        out_specs=pl.BlockSpec((tm, 1024), lambda i: (i, 0)),
        out_shape=jax.ShapeDtypeStruct((m, 1024), BF16),
        compiler_params=_cparams(("arbitrary",), 32),
        name="conv_kq",
    )(z, z, z, w, b, s)


def _conv_hy_body(ac_ref, ap_ref, an_ref, bc_ref, bp_ref, bn_ref, cc_ref, cp_ref, cn_ref,
                  w_ref, b_ref, x0_ref, xv_ref):
    i = pl.program_id(0)
    first, last = i == 0, i == pl.num_programs(0) - 1
    w = w_ref[...]
    b = b_ref[...]

    def cv(c, p, n, k):
        return _conv3(c[...].astype(F32), p[...].astype(F32), n[...].astype(F32),
                      w[:, k * 1024:(k + 1) * 1024], b[:, k * 1024:(k + 1) * 1024], first, last)

    x0_ref[...] = cv(ac_ref, ap_ref, an_ref, 0).astype(BF16)
    xv_ref[...] = (cv(bc_ref, bp_ref, bn_ref, 1) * cv(cc_ref, cp_ref, cn_ref, 2)).astype(BF16)


def _conv_hy_call(z, w, b, tm):
    m = z.shape[0]
    out = pl.BlockSpec((tm, 1024), lambda i: (i, 0))
    return pl.pallas_call(
        _conv_hy_body,
        grid=(m // tm,),
        in_specs=(_halo_specs(tm, m, ZC_X0) + _halo_specs(tm, m, ZC_X1) + _halo_specs(tm, m, ZC_HV)
                  + [pl.BlockSpec((3, 3072), lambda i: (0, 0)), pl.BlockSpec((1, 3072), lambda i: (0, 0))]),
        out_specs=[out, out],
        out_shape=[jax.ShapeDtypeStruct((m, 1024), BF16), jax.ShapeDtypeStruct((m, 1024), BF16)],
        compiler_params=_cparams(("arbitrary",), 32),
        name="conv_hy",
    )(z, z, z, z, z, z, z, z, z, w, b)


def _mlstm_body(kqf_ref, vf_ref, gf_ref, kqb_ref, vb_ref, gb_ref, bg_ref, s0_ref, m0_ref,
                hf_ref, hb_ref, sfin_ref, mfin_ref, s_scr, m_scr):
    j = pl.program_id(0)

    @pl.when(j == 0)
    def _():
        s_scr[...] = s0_ref[...]
        m_scr[...] = m0_ref[...]

    r = lax.broadcasted_iota(I32, (CHUNK, CHUNK), 0)
    c = lax.broadcasted_iota(I32, (CHUNK, CHUNK), 1)
    ones_b = jnp.ones((CHUNK, CHUNK), BF16)
    bg = bg_ref[...]

    def lane_bcast(x, h, width=CHUNK):
        return jnp.broadcast_to(x[:, h:h + 1], (x.shape[0], width))

    for d in range(2):
        kq = (kqf_ref, kqb_ref)[d][...]
        v = (vf_ref, vb_ref)[d][...]
        g_all = (gf_ref, gb_ref)[d][...] + bg
        out_ref = (hf_ref, hb_ref)[d]
        tri = (r >= c) if d == 0 else (c >= r)
        tri_b = jnp.where(tri, 1.0, 0.0).astype(BF16)
        gi = g_all if d == 0 else pltpu.roll(g_all, CHUNK - 16, 1)
        gfp = pltpu.roll(g_all, CHUNK - 8 - 16 * d, 1)
        lf = jnp.minimum(gfp, 0.0) - jnp.log(1.0 + jnp.exp(-jnp.abs(gfp)))
        l1 = lf.astype(BF16)
        r1 = lf - l1.astype(F32)
        l2 = r1.astype(BF16)
        l3 = (r1 - l2.astype(F32)).astype(BF16)
        bcum = _dg(tri_b, l1) + _dg(tri_b, l2) + _dg(tri_b, l3)
        gtot = bcum[CHUNK - 1:CHUNK, :] if d == 0 else bcum[0:1, :]
        acol = gtot - bcum + gi
        m_loc = jnp.max(acol, axis=0, keepdims=True)
        m_st = m_scr[d, 0:1, :]
        m_new = jnp.maximum(gtot + m_st, m_loc)
        sp8 = jnp.broadcast_to(jnp.exp(gtot + m_st - m_new), (8, CHUNK))
        wst = jnp.exp(acol - m_new)
        rr = gi - bcum
        cm = rr
        for sh in (1, 2, 4, 8, 16, 32, 64):
            if d == 0:
                cm = jnp.maximum(cm, jnp.where(r >= sh, pltpu.roll(cm, sh, 0), NEG))
            else:
                cm = jnp.maximum(cm, jnp.where(r < CHUNK - sh, pltpu.roll(cm, CHUNK - sh, 0), NEG))
        mt = jnp.maximum(bcum + m_st, bcum + cm)
        c1 = bcum - mt
        rt = jnp.transpose(rr)
        wt = jnp.transpose(wst)
        m8 = jnp.broadcast_to(m_st, (8, CHUNK))
        kts = {}

        for h in range(N_HEADS):
            p, half = divmod(h, 2)
            lm = (c // QK_HEAD) == half
            kp = kq[:, p * 128:(p + 1) * 128]
            qp = kq[:, QK_W + p * 128:QK_W + (p + 1) * 128]
            vaug = jnp.concatenate([v[:, h * 128:(h + 1) * 128], ones_b], axis=1)
            qm = jnp.where(lm, qp, jnp.zeros_like(qp))
            c1b = lane_bcast(c1, h)
            pm = jnp.exp(jnp.where(tri, c1b + rt[h:h + 1, :], NEG))
            s = (_dg(qm, kp, _NT) * pm).astype(BF16)
            m_in = jnp.tile(lane_bcast(m8, h), (CHUNK // 8, 1))
            qs = (qm.astype(F32) * jnp.exp(c1b + m_in)).astype(BF16)
            st = s_scr[d * N_HEADS + h]
            tot = _dg(jnp.concatenate([s, qs], axis=1), jnp.concatenate([vaug, st.astype(BF16)], axis=0))
            den = jnp.maximum(jnp.abs(tot[:, 128:]), jnp.exp(-lane_bcast(mt, h)))
            out_ref[:, h * 128:(h + 1) * 128] = (tot[:, :128] / den).astype(BF16)
            if p not in kts:
                kts[p] = jnp.transpose(kp.astype(F32))
            kw = jnp.where((r // QK_HEAD) == half, kts[p] * wt[h:h + 1, :], 0.0).astype(BF16)
            spb = jnp.tile(lane_bcast(sp8, h, 256), (CHUNK // 8, 1))
            s_scr[d * N_HEADS + h] = spb * st + _dg(kw, vaug)
        m_scr[d, 0:1, :] = m_new

    @pl.when(j == pl.num_programs(0) - 1)
    def _():
        sfin_ref[...] = s_scr[...]
        mfin_ref[...] = m_scr[...]


def _mlstm_call(kq, z, gates, bg, s0, m0):
    m = kq.shape[0]
    nc = m // CHUNK
    fwd = lambda cb: (lambda j: (j, cb))
    bwd = lambda cb: (lambda j: (nc - 1 - j, cb))
    st_spec = pl.BlockSpec((2 * N_HEADS, CHUNK, 256), lambda j: (0, 0, 0))
    m_spec = pl.BlockSpec((2, 8, 128), lambda j: (0, 0, 0))
    return pl.pallas_call(
        _mlstm_body,
        grid=(nc,),
        in_specs=[pl.BlockSpec((CHUNK, 1024), fwd(0)), pl.BlockSpec((CHUNK, 1024), fwd(ZC_V)),
                  pl.BlockSpec((CHUNK, GATE_PAD), fwd(0)),
                  pl.BlockSpec((CHUNK, 1024), bwd(0)), pl.BlockSpec((CHUNK, 1024), bwd(ZC_V)),
                  pl.BlockSpec((CHUNK, GATE_PAD), bwd(0)),
                  pl.BlockSpec((1, GATE_PAD), lambda j: (0, 0)), st_spec, m_spec],
        out_specs=[pl.BlockSpec((CHUNK, 1024), fwd(0)), pl.BlockSpec((CHUNK, 1024), bwd(0)), st_spec, m_spec],
        out_shape=[jax.ShapeDtypeStruct((m, 1024), BF16), jax.ShapeDtypeStruct((m, 1024), BF16),
                   jax.ShapeDtypeStruct((2 * N_HEADS, CHUNK, 256), F32),
                   jax.ShapeDtypeStruct((2, 8, 128), F32)],
        scratch_shapes=[pltpu.VMEM((2 * N_HEADS, CHUNK, 256), F32), pltpu.VMEM((2, 8, 128), F32)],
        compiler_params=_cparams(("arbitrary",), 32),
        name="mlstm",
    )(kq, z, gates, kq, z, gates, bg, s0, m0)


def _filt_body(seq_len, ft_ref, w1_ref, b1_ref, f1_ref, w2_ref, b2_ref, f2_ref, w3_ref, b3_ref, rt_ref,
               kf_ref, l1_ref):
    i = pl.program_id(0)
    tn = ft_ref.shape[1]
    hp = tn // 2
    reps = tn // 128
    col = lambda ref: jnp.tile(ref[...], (1, reps))
    h1 = jnp.sin(col(f1_ref) * (_dot3(w1_ref[...], ft_ref[...]) + col(b1_ref)))
    h2 = jnp.sin(col(f2_ref) * (_dot3(w2_ref[...], h1) + col(b2_ref)))
    r = lax.broadcasted_iota(I32, (hp, HY_W), 0)
    n_fwd = (8 * i + (r & 7)) * FFT_N2 + (r >> 3)
    rates = rt_ref[...]
    halves = []
    l1 = jnp.zeros((1, HY_W), F32)
    for hx in range(2):
        h = (_dot3(h2[:, hx * hp:(hx + 1) * hp], w3_ref[:, hx * HY_W:(hx + 1) * HY_W], _TN)
             + b3_ref[:, hx * HY_W:(hx + 1) * HY_W])
        n = n_fwd + hx * seq_len
        t01 = jnp.where(n <= seq_len, n, 2 * seq_len - n).astype(F32) / float(max(seq_len - 1, 1))
        h = jnp.where(n == seq_len, 0.0, h * jnp.exp(-t01 * rates))
        l1 = l1 + jnp.sum(jnp.abs(h), axis=0, keepdims=True)
        halves.append(h)
    word = _pack_pair(halves[0], halves[1])
    for b in range(FFT_N2):
        kf_ref[:, b * HY_W:(b + 1) * HY_W] = word[8 * b:8 * b + 8, :]

    @pl.when(i == 0)
    def _():
        l1_ref[...] = jnp.zeros_like(l1_ref)

    l1_ref[...] += l1


def _filt_call(feats_t, w1t, b1, f1, w2t, b2, f2, w3, b3, rates):
    n = feats_t.shape[1]
    seq_len = n // 2
    tn = 2 * 8 * FFT_N2
    c64 = lambda shape: pl.BlockSpec(shape, lambda i: (0, 0))
    return pl.pallas_call(
        functools.partial(_filt_body, seq_len),
        grid=(n // tn,),
        in_specs=[pl.BlockSpec((64, tn), lambda i: (0, i)),
                  c64((64, 64)), c64((64, 128)), c64((64, 128)), c64((64, 64)), c64((64, 128)), c64((64, 128)),
                  c64((64, 2 * HY_W)), c64((1, 2 * HY_W)), c64((1, HY_W))],
        out_specs=[pl.BlockSpec((8, FFT_N2 * HY_W), lambda i: (i, 0)), pl.BlockSpec((1, HY_W), lambda i: (0, 0))],
        out_shape=[jax.ShapeDtypeStruct((seq_len // FFT_N2, FFT_N2 * HY_W), U32),
                   jax.ShapeDtypeStruct((1, HY_W), F32)],
        compiler_params=_cparams(("arbitrary",), 48),
        name="filt",
    )(feats_t, w1t, b1, f1, w2t, b2, f2, w3, b3, rates)


def _fft_consts(n1_rows):
    n = n1_rows * FFT_N2
    kv = n1_rows // 2 + 1
    k1 = np.arange(FFT_KP, dtype=np.float64)
    valid = (k1 < kv).astype(np.float64)
    n1 = np.arange(n1_rows, dtype=np.float64)
    th1 = 2.0 * np.pi * np.outer(k1, n1) / n1_rows
    f1 = np.concatenate([np.cos(th1) * valid[:, None], -np.sin(th1) * valid[:, None]], axis=0)
    n2 = np.arange(FFT_N2, dtype=np.float64)
    tht = 2.0 * np.pi * np.outer(k1, n2) / n
    rep = lambda a: jnp.broadcast_to(jnp.asarray(a, F32)[:, :, None], (FFT_KP, FFT_N2, 128))
    twr = rep(np.cos(tht) * valid[:, None])
    twi = rep(-np.sin(tht) * valid[:, None])
    th2 = 2.0 * np.pi * np.outer(n2, n2) / FFT_N2
    cs, sn = np.cos(th2), np.sin(th2)
    f2p = np.block([[cs, sn], [-sn, cs]])
    f2pc = np.block([[cs, -sn], [sn, cs]])
    wk = np.where((k1 == 0) | (k1 == kv - 1), 1.0, 2.0) * valid / n
    half = n1_rows // 2
    thi = 2.0 * np.pi * np.outer(n1[:half], k1) / n1_rows
    gc = np.cos(thi) * wk[None, :]
    gs = np.sin(thi) * wk[None, :]
    as_bf = lambda a: jnp.asarray(a, F32).astype(BF16)
    return dict(f1=as_bf(f1), twr=twr, twi=twi,
                f2p=as_bf(f2p), f2pc=as_bf(f2pc), gc=as_bf(gc), gs=as_bf(gs))


def _fft1_body(f_ref, x_ref, ar_ref, ai_ref):
    o = _dg(f_ref[...], x_ref[...])
    ar_ref[...] = o[:FFT_KP].astype(BF16)
    ai_ref[...] = o[FFT_KP:].astype(BF16)


def _fft1_packed_body(f_ref, x_ref, ar_ref, ai_ref):
    k = x_ref.shape[0]
    hi, lo = _unpack_pair(x_ref[...])
    o = _dg(f_ref[:, 0:k], hi.astype(BF16)) + _dg(f_ref[:, k:], lo.astype(BF16))
    ar_ref[...] = o[:FFT_KP].astype(BF16)
    ai_ref[...] = o[FFT_KP:].astype(BF16)


def _fft1_call(f1, x2d, cb, packed=False):
    k, cols = x2d.shape
    f1 = f1[:, :2 * k] if packed else f1[:, :k]
    out = pl.BlockSpec((FFT_KP, cb), lambda i: (0, i))
    sh = jax.ShapeDtypeStruct((FFT_KP, cols), BF16)
    return pl.pallas_call(
        _fft1_packed_body if packed else _fft1_body,
        grid=(cols // cb,),
        in_specs=[pl.BlockSpec(f1.shape, lambda i: (0, 0)), pl.BlockSpec((k, cb), lambda i: (0, i))],
        out_specs=[out, out],
        out_shape=[sh, sh],
        compiler_params=_cparams(("arbitrary",), 32),
        name="fft1",
    )(f1, x2d)


def _twiddled(ar_ref, ai_ref, twr_ref, twi_ref, reps):
    a_r = ar_ref[...].astype(F32)
    a_i = ai_ref[...].astype(F32)
    tr = jnp.tile(twr_ref[...], (1, reps))
    ti = jnp.tile(twi_ref[...], (1, reps))
    st = jnp.concatenate([a_r * tr - a_i * ti, a_r * ti + a_i * tr], axis=0).astype(BF16)
    return st, tr, ti


def _fft2_filt_body(kv, ar_ref, ai_ref, twr_ref, twi_ref, f2p_ref, k_ref):
    @pl.when(pl.program_id(0) < kv)
    def _():
        st, _, _ = _twiddled(ar_ref, ai_ref, twr_ref, twi_ref, ar_ref.shape[-1] // 128)
        k_ref[...] = _dg(f2p_ref[...], st).astype(BF16)

    @pl.when(pl.program_id(0) >= kv)
    def _():
        k_ref[...] = jnp.zeros_like(k_ref)


def _fft2_conv_body(kv, ar_ref, ai_ref, twr_ref, twi_ref, k_ref, f2p_ref, f2pc_ref, br_ref, bi_ref):
    @pl.when(pl.program_id(0) < kv)
    def _():
        st, tr, ti = _twiddled(ar_ref, ai_ref, twr_ref, twi_ref, ar_ref.shape[-1] // 128)
        x = _dg(f2p_ref[...], st)
        xr, xi = x[:FFT_N2], x[FFT_N2:]
        kr = k_ref[:FFT_N2, :].astype(F32)
        ki = k_ref[FFT_N2:, :].astype(F32)
        sy = jnp.concatenate([xr * kr - xi * ki, xr * ki + xi * kr], axis=0).astype(BF16)
        b = _dg(f2pc_ref[...], sy)
        b_r, b_i = b[:FFT_N2], b[FFT_N2:]
        br_ref[...] = (b_r * tr + b_i * ti).astype(BF16)
        bi_ref[...] = (b_i * tr - b_r * ti).astype(BF16)

    @pl.when(pl.program_id(0) >= kv)
    def _():
        br_ref[...] = jnp.zeros_like(br_ref)
        bi_ref[...] = jnp.zeros_like(bi_ref)


def _fft2_specs(ch, kv):
    src = lambda i: (jnp.minimum(i, kv - 1), 0, 0)
    blk = pl.BlockSpec((None, FFT_N2, ch), src)
    tw = pl.BlockSpec((None, FFT_N2, 128), src)
    mat = pl.BlockSpec((2 * FFT_N2, 2 * FFT_N2), lambda i: (0, 0))
    return blk, tw, mat, src


def _fft2_filt_call(ar, ai, cst, kv):
    ch = ar.shape[-1]
    blk, tw, mat, _ = _fft2_specs(ch, kv)
    return pl.pallas_call(
        functools.partial(_fft2_filt_body, kv),
        grid=(FFT_KP,),
        in_specs=[blk, blk, tw, tw, mat],
        out_specs=pl.BlockSpec((None, 2 * FFT_N2, ch), lambda i: (i, 0, 0)),
        out_shape=jax.ShapeDtypeStruct((FFT_KP, 2 * FFT_N2, ch), BF16),
        compiler_params=_cparams(("arbitrary",), 32),
        name="fft2_filt",
    )(ar, ai, cst["twr"], cst["twi"], cst["f2p"])


def _fft2_conv_call(ar, ai, khat, cst, kv):
    ch = ar.shape[-1]
    blk, tw, mat, src = _fft2_specs(ch, kv)
    sh = jax.ShapeDtypeStruct((FFT_KP, FFT_N2, ch), BF16)
    out = pl.BlockSpec((None, FFT_N2, ch), lambda i: (i, 0, 0))
    return pl.pallas_call(
        functools.partial(_fft2_conv_body, kv),
        grid=(FFT_KP,),
        in_specs=[blk, blk, tw, tw, pl.BlockSpec((None, 2 * FFT_N2, ch), src), mat, mat],
        out_specs=[out, out],
        out_shape=[sh, sh],
        compiler_params=_cparams(("arbitrary",), 32),
        name="fft2_conv",
    )(ar, ai, cst["twr"], cst["twi"], khat, cst["f2p"], cst["f2pc"])


def _ifft1_body(gc_ref, gs_ref, br_ref, bi_ref, x0_ref, xv_ref, il_ref, ds_ref, o_ref):
    y = _dg(gc_ref[...], br_ref[...]) - _dg(gs_ref[...], bi_ref[...])
    o_ref[...] = (x0_ref[...].astype(F32)
                  * (y * il_ref[...] + ds_ref[...] * xv_ref[...].astype(F32))).astype(BF16)


def _ifft1_call(cst, br2d, bi2d, x0_2d, xv_2d, il_t, ds_t, cb):
    rows, cols = x0_2d.shape
    g = pl.BlockSpec((rows, FFT_KP), lambda i: (0, 0))
    kb = pl.BlockSpec((FFT_KP, cb), lambda i: (0, i))
    xb = pl.BlockSpec((rows, cb), lambda i: (0, i))
    vb = pl.BlockSpec((1, cb), lambda i: (0, 0))
    return pl.pallas_call(
        _ifft1_body,
        grid=(cols // cb,),
        in_specs=[g, g, kb, kb, xb, xb, vb, vb],
        out_specs=xb,
        out_shape=jax.ShapeDtypeStruct((rows, cols), BF16),
        compiler_params=_cparams(("arbitrary",), 32),
        name="ifft1",
    )(cst["gc"], cst["gs"], br2d, bi2d, x0_2d, xv_2d, il_t, ds_t)


def _outproj_body(hf_ref, hb_ref, zo_ref, yh_ref, x_ref, erow_ref, ecol_ref, gh_ref, wa_ref, wb_ref,
                  g1_ref, gf_ref, sh_ref, sc_ref, wr_ref, x1_ref, h2_ref, s_ref):
    hs = hf_ref[...].astype(F32) + hb_ref[...].astype(F32)
    gh = gh_ref[...]
    parts = []
    for h in range(N_HEADS):
        hh = hs[:, h * 128:(h + 1) * 128]
        ms = jnp.mean(hh * hh, axis=-1, keepdims=True)
        parts.append(hh * lax.rsqrt(ms + EPS) * gh[:, h * 128:(h + 1) * 128])
    ym = jnp.concatenate(parts, axis=-1) * _sigmoid(zo_ref[...].astype(F32))
    y = _dg(ym.astype(BF16), wa_ref[...]) + _dg(yh_ref[...], wb_ref[...])
    rp = x_ref.shape[0] // GRID_W
    erow8 = erow_ref[...]
    erow = erow8[0:rp, :]
    for q in range(1, 8 // rp):
        erow = jnp.where(pl.program_id(0) % (8 // rp) == q, erow8[q * rp:(q + 1) * rp, :], erow)
    x1 = _add_pos(x_ref[...], erow, ecol_ref[...]) + g1_ref[...] * y
    x1_ref[...] = x1
    h2 = _norm_mod(x1, gf_ref[...], sh_ref[...], sc_ref[...])
    half = h2.shape[1] // 2
    _store_slabs(h2_ref, _pack_pair(h2[:, :half], h2[:, half:]))
    s_ref[...] = _sigmoid(_dot3(wr_ref[...], h2, _NT))


def _outproj_call(hf, hb, z, yh, x, erow, ecol, gh, wa, wb, g1, gf, sh2, sc2, wrt, tm):
    m, d = x.shape
    row = lambda cb: pl.BlockSpec((tm, 1024), lambda i: (i, cb))
    vec = lambda n: pl.BlockSpec((1, n), lambda i: (0, 0))
    full = pl.BlockSpec((tm, d), lambda i: (i, 0))
    return pl.pallas_call(
        _outproj_body,
        grid=(m // tm,),
        in_specs=[row(0), row(0), row(ZC_O), row(0), full,
                  pl.BlockSpec((8, d // 2), lambda i: (i * (tm // GRID_W) // 8, 0)),
                  pl.BlockSpec((GRID_W, d // 2), lambda i: (0, 0)),
                  vec(MV_W),
                  pl.BlockSpec((MV_W, d), lambda i: (0, 0)), pl.BlockSpec((HY_W, d), lambda i: (0, 0)),
                  vec(d), vec(d), vec(d), vec(d),
                  pl.BlockSpec((N_EXPERTS, d), lambda i: (0, 0))],
        out_specs=[full, pl.BlockSpec((tm * SLAB, SLAB_W), lambda i: (i, 0)),
                   pl.BlockSpec((N_EXPERTS, tm), lambda i: (0, i))],
        out_shape=[jax.ShapeDtypeStruct((m, d), F32), jax.ShapeDtypeStruct((m * SLAB, SLAB_W), U32),
                   jax.ShapeDtypeStruct((N_EXPERTS, m), F32)],
        compiler_params=_cparams(("arbitrary",), 56),
        name="outproj",
    )(hf, hb, z, yh, x, erow, ecol, gh, wa, wb, g1, gf, sh2, sc2, wrt)


def _first_max(x, idx, sentinel):
    m = jnp.max(x, axis=0, keepdims=True)
    return m, jnp.min(jnp.where(x == m, idx, sentinel), axis=0, keepdims=True)


def _route_body(s_ref, b_ref, e_ref, w_ref, r_ref, cnt_ref, u_scr, run_scr):
    i = pl.program_id(0)
    tt = s_ref.shape[1]

    @pl.when(i == 0)
    def _():
        rr = lax.broadcasted_iota(I32, (tt, tt), 0)
        cc = lax.broadcasted_iota(I32, (tt, tt), 1)
        u_scr[...] = jnp.where(rr < cc, 1.0, 0.0).astype(BF16)
        run_scr[...] = jnp.zeros_like(run_scr)

    s = s_ref[...]
    sel = s + b_ref[...][:, 0:1]
    sub8 = lax.broadcasted_iota(I32, (E_PER_GROUP, tt), 0).astype(F32)
    gs = jnp.zeros((N_GROUPS, tt), F32)
    for g in range(N_GROUPS):
        grp = sel[g * E_PER_GROUP:(g + 1) * E_PER_GROUP, :]
        m1, i1 = _first_max(grp, sub8, float(E_PER_GROUP))
        m2 = jnp.max(jnp.where(sub8 == i1, -jnp.inf, grp), axis=0, keepdims=True)
        gs = jnp.where(sub8 == g, m1 + m2, gs)
    gmask = jnp.zeros((N_GROUPS, tt), F32)
    for _ in range(TOPK_GROUPS):
        _, ig = _first_max(gs, sub8, float(N_GROUPS))
        hit = sub8 == ig
        gmask = jnp.where(hit, 1.0, gmask)
        gs = jnp.where(hit, -jnp.inf, gs)
    masked = jnp.concatenate(
        [jnp.where(jnp.broadcast_to(gmask[g:g + 1, :], (E_PER_GROUP, tt)) > 0.5,
                   sel[g * E_PER_GROUP:(g + 1) * E_PER_GROUP, :], -jnp.inf) for g in range(N_GROUPS)], axis=0)
    sub64 = lax.broadcasted_iota(I32, (N_EXPERTS, tt), 0).astype(F32)
    oh = jnp.zeros((N_EXPERTS, tt), F32)
    eks, wks = [], []
    for _ in range(TOP_K):
        _, ie = _first_max(masked, sub64, float(N_EXPERTS))
        hit = sub64 == ie
        wks.append(jnp.sum(jnp.where(hit, s, 0.0), axis=0, keepdims=True))
        eks.append(ie)
        masked = jnp.where(hit, -jnp.inf, masked)
        oh = jnp.where(hit, 1.0, oh)
    wsum = wks[0]
    for k in range(1, TOP_K):
        wsum = wsum + wks[k]
    run = run_scr[...]
    rank_t = _dg(oh.astype(BF16), u_scr[...]) + jnp.tile(run, (1, tt // 128))
    for k in range(TOP_K):
        e_ref[k:k + 1, :] = eks[k].astype(I32)
        w_ref[k:k + 1, :] = wks[k] / wsum * ROUTE_SCALE
        r_ref[k:k + 1, :] = jnp.sum(jnp.where(sub64 == eks[k], rank_t, 0.0), axis=0, keepdims=True).astype(I32)
    run_new = run + jnp.sum(oh, axis=1, keepdims=True)
    run_scr[...] = run_new
    cnt_ref[...] = run_new.astype(I32)


def _route_call(s_t, b_col, tt):
    m = s_t.shape[1]
    out = pl.BlockSpec((TOP_K, tt), lambda i: (0, i))
    return pl.pallas_call(
        _route_body,
        grid=(m // tt,),
        in_specs=[pl.BlockSpec((N_EXPERTS, tt), lambda i: (0, i)),
                  pl.BlockSpec((N_EXPERTS, 128), lambda i: (0, 0))],
        out_specs=[out, out, out, pl.BlockSpec((N_EXPERTS, 128), lambda i: (0, 0))],
        out_shape=[jax.ShapeDtypeStruct((TOP_K, m), I32), jax.ShapeDtypeStruct((TOP_K, m), F32),
                   jax.ShapeDtypeStruct((TOP_K, m), I32), jax.ShapeDtypeStruct((N_EXPERTS, 128), I32)],
        scratch_shapes=[pltpu.VMEM((tt, tt), BF16), pltpu.VMEM((N_EXPERTS, 128), F32)],
        compiler_params=_cparams(("arbitrary",), 32),
        name="route",
    )(s_t, b_col)


def _posk_body(pst_ref, e_ref, r_ref, p_ref):
    e = e_ref[...]
    acc = r_ref[...]
    for x in range(N_EXPERTS):
        acc = acc + jnp.where(e == x, pst_ref[x], 0)
    p_ref[...] = acc


def _posk_call(pstart, eidx, rank):
    k, m = eidx.shape
    tt = min(m, 2048)
    blk = pl.BlockSpec((k, tt), lambda i, pst: (0, i))
    return pl.pallas_call(
        _posk_body,
        grid_spec=pltpu.PrefetchScalarGridSpec(num_scalar_prefetch=1, grid=(m // tt,),
                                               in_specs=[blk, blk], out_specs=blk),
        out_shape=jax.ShapeDtypeStruct((k, m), I32),
        compiler_params=_cparams(("arbitrary",), 32),
        name="posk",
    )(pstart, eidx, rank)


def _slab(ref, r):
    return ref.at[pl.ds(pl.multiple_of(r * SLAB, SLAB), SLAB), :]


def _ffn_packed(x_ref, rows, w1, w3, w2):
    half = w1.shape[0] // 2
    xa, xb = _unpack_pair(_load_slabs(x_ref, rows))
    xa = xa.astype(BF16)
    xb = xb.astype(BF16)
    h1 = _dg(xa, w1[0:half, :]) + _dg(xb, w1[half:, :])
    h3 = _dg(xa, w3[0:half, :]) + _dg(xb, w3[half:, :])
    return _dg((_silu(h1) * h3).astype(BF16), w2[...])


def _dispatch_body(cnt_ref, pst_ref, pcn_ref, h2_ref, pos_ref, w1_ref, w3_ref, w2_ref,
                   xs_ref, sh_ref, zrow, sem):
    i = pl.program_id(0)
    td = h2_ref.shape[0] // SLAB

    def row_copy(t, dst):
        return pltpu.make_async_copy(_slab(h2_ref, t), _slab(xs_ref, dst), sem)

    def issue(t, carry):
        for k in range(TOP_K):
            row_copy(t, pos_ref[k, t]).start(priority=k % 2)
        return carry

    lax.fori_loop(0, td, issue, 0)

    sh_ref[...] = _ffn_packed(h2_ref, td, w1_ref, w3_ref, w2_ref).astype(BF16)

    def drain(t, carry):
        for k in range(TOP_K):
            row_copy(0, 0).wait()
        return carry

    lax.fori_loop(0, td, drain, 0)

    @pl.when(i == pl.num_programs(0) - 1)
    def _():
        zrow[...] = jnp.zeros_like(zrow)

        def zero_copy(dst):
            return pltpu.make_async_copy(zrow, _slab(xs_ref, dst), sem)

        def per_expert(e, carry):
            base = pst_ref[e]
            lax.fori_loop(cnt_ref[e], pcn_ref[e], lambda rr, cc: (zero_copy(base + rr).start(), cc)[1], 0)
            lax.fori_loop(cnt_ref[e], pcn_ref[e], lambda rr, cc: (zero_copy(0).wait(), cc)[1], 0)
            return carry

        lax.fori_loop(0, N_EXPERTS, per_expert, 0)


def _dispatch_call(cnt, pstart, pcnt, h2s, pos, w1s, w3s, w2s, rows, td):
    m = h2s.shape[0] // SLAB
    d, ds = w1s.shape
    return pl.pallas_call(
        _dispatch_body,
        grid_spec=pltpu.PrefetchScalarGridSpec(
            num_scalar_prefetch=3, grid=(m // td,),
            in_specs=[pl.BlockSpec((td * SLAB, SLAB_W), lambda i, *_: (i, 0)),
                      pl.BlockSpec((TOP_K, td), lambda i, *_: (0, i), memory_space=pltpu.SMEM),
                      pl.BlockSpec((d, ds), lambda i, *_: (0, 0)), pl.BlockSpec((d, ds), lambda i, *_: (0, 0)),
                      pl.BlockSpec((ds, d), lambda i, *_: (0, 0))],
            out_specs=[pl.BlockSpec(memory_space=pl.ANY), pl.BlockSpec((td, d), lambda i, *_: (i, 0))],
            scratch_shapes=[pltpu.VMEM((SLAB, SLAB_W), U32), pltpu.SemaphoreType.DMA(())]),
        out_shape=[jax.ShapeDtypeStruct((rows * SLAB, SLAB_W), U32), jax.ShapeDtypeStruct((m, d), BF16)],
        compiler_params=_cparams(("arbitrary",), 40),
        name="dispatch",
    )(cnt, pstart, pcnt, h2s, pos, w1s, w3s, w2s)


def _moe_body(te_ref, nu_ref, nxt_ref, par_ref, x_ref, w1_ref, w3_ref, w2_ref, y_ref,
              f1, f3, f2, w1b, w3b, w2b, sems):
    i = pl.program_id(0)
    used = i < nu_ref[0]
    e = te_ref[i]
    first = jnp.logical_or(i == 0, e != te_ref[jnp.maximum(i - 1, 0)])
    slot = par_ref[e]

    def fetch(ex, s):
        return (pltpu.make_async_copy(w1_ref.at[ex], f1.at[s], sems.at[s]),
                pltpu.make_async_copy(w3_ref.at[ex], f3.at[s], sems.at[s]),
                pltpu.make_async_copy(w2_ref.at[ex], f2.at[s], sems.at[s]))

    @pl.when(jnp.logical_and(used, i == 0))
    def _():
        for c in fetch(e, slot):
            c.start()

    @pl.when(jnp.logical_and(used, first))
    def _():
        for c in fetch(e, slot):
            c.wait()
        nx = nxt_ref[e]

        @pl.when(nx < N_EXPERTS)
        def _():
            for c in fetch(nx, 1 - slot):
                c.start()

        w1b[...] = f1[slot].astype(BF16)
        w3b[...] = f3[slot].astype(BF16)
        w2b[...] = f2[slot].astype(BF16)

    @pl.when(used)
    def _():
        half = w1b.shape[0] // 2
        y = _ffn_packed(x_ref, MOE_ROWS, w1b, w3b, w2b)
        _store_slabs(y_ref, _pack_pair(y[:, :half], y[:, half:]))


def _moe_call(tile_e, n_used, nxt, par, xs, w1, w3, w2):
    rows = xs.shape[0] // SLAB
    nt = rows // MOE_ROWS
    d, de = w1.shape[-2:]
    rmap = lambda i, te, nu, *_: (jnp.minimum(i, nu[0] - 1), 0)
    hbm = pl.BlockSpec(memory_space=pl.ANY)
    return pl.pallas_call(
        _moe_body,
        grid_spec=pltpu.PrefetchScalarGridSpec(
            num_scalar_prefetch=4, grid=(nt,),
            in_specs=[pl.BlockSpec((MOE_ROWS * SLAB, SLAB_W), rmap), hbm, hbm, hbm],
            out_specs=pl.BlockSpec((MOE_ROWS * SLAB, SLAB_W), rmap),
            scratch_shapes=[pltpu.VMEM((2, d, de), F32), pltpu.VMEM((2, d, de), F32), pltpu.VMEM((2, de, d), F32),
                            pltpu.VMEM((d, de), BF16), pltpu.VMEM((d, de), BF16), pltpu.VMEM((de, d), BF16),
                            pltpu.SemaphoreType.DMA((2,))]),
        out_shape=jax.ShapeDtypeStruct((rows * SLAB, SLAB_W), U32),
        compiler_params=_cparams(("arbitrary",), 56),
        name="moe",
    )(tile_e, n_used, nxt, par, xs, w1, w3, w2)


def _final_body(x1_ref, sh_ref, pos_ref, posn_ref, wt_ref, ys_ref, g2_ref, gn_ref, o_ref, ybuf, sems):
    i = pl.program_id(0)
    n = pl.num_programs(0)
    tf, d = x1_ref.shape
    half = d // 2
    slot = i % 2
    slot_slabs = TOP_K * tf

    def row_copy(p_ref, s, k, t):
        return pltpu.make_async_copy(_slab(ys_ref, p_ref[k, t]), _slab(ybuf, s * slot_slabs + k * tf + t),
                                     sems.at[s])

    def issue(p_ref, s):
        def body(t, carry):
            for k in range(TOP_K):
                row_copy(p_ref, s, k, t).start(priority=k % 2)
            return carry

        lax.fori_loop(0, tf, body, 0)

    @pl.when(i == 0)
    def _():
        issue(pos_ref, 0)

    @pl.when(i + 1 < n)
    def _():
        issue(posn_ref, 1 - slot)

    def drain(t, carry):
        for k in range(TOP_K):
            pltpu.make_async_copy(_slab(ys_ref, 0), _slab(ybuf, 0), sems.at[slot]).wait()
        return carry

    lax.fori_loop(0, tf, drain, 0)

    wt = jnp.concatenate([jnp.transpose(jnp.concatenate(
        [wt_ref[:, c * 128:(c + 1) * 128], jnp.zeros((128 - TOP_K, 128), F32)], axis=0)) for c in range(tf // 128)],
        axis=0)
    shared = sh_ref[...].astype(F32)
    acc_a = shared[:, :half]
    acc_b = shared[:, half:]
    base = slot * slot_slabs * SLAB
    for k in range(TOP_K):
        ya, yb = _unpack_pair(_load_slabs(ybuf, tf, base=base + k * tf * SLAB))
        acc_a = acc_a + wt[:, k:k + 1] * ya
        acc_b = acc_b + wt[:, k:k + 1] * yb
    xo = x1_ref[...] + g2_ref[...] * jnp.concatenate([acc_a, acc_b], axis=1)
    ms = jnp.mean(xo * xo, axis=-1, keepdims=True)
    o_ref[...] = xo * lax.rsqrt(ms + EPS) * gn_ref[...]


def _final_call(x1, sh, pos, wts, ys, g2, gn, tf):
    m, d = x1.shape
    nt = m // tf
    full = pl.BlockSpec((tf, d), lambda i: (i, 0))
    vec = pl.BlockSpec((1, d), lambda i: (0, 0))
    return pl.pallas_call(
        _final_body,
        grid=(nt,),
        in_specs=[full, full,
                  pl.BlockSpec((TOP_K, tf), lambda i: (0, i), memory_space=pltpu.SMEM),
                  pl.BlockSpec((TOP_K, tf), lambda i: (0, jnp.minimum(i + 1, nt - 1)), memory_space=pltpu.SMEM),
                  pl.BlockSpec((TOP_K, tf), lambda i: (0, i)),
                  pl.BlockSpec(memory_space=pl.ANY), vec, vec],
        out_specs=full,
        out_shape=jax.ShapeDtypeStruct((m, d), F32),
        scratch_shapes=[pltpu.VMEM((2 * TOP_K * tf * SLAB, SLAB_W), U32), pltpu.SemaphoreType.DMA((2,))],
        compiler_params=_cparams(("arbitrary",), 48),
        name="final",
    )(x1, sh, pos, pos, wts, ys, g2, gn)


def _pos_tables(n_tokens):
    rows = n_tokens // GRID_W
    quarter = D_MODEL // 4
    omega = 1.0 / (10000.0 ** (jnp.arange(quarter, dtype=F32) / quarter))

    def emb1d(pos):
        ang = pos[:, None] * omega[None]
        return jnp.concatenate([jnp.sin(ang), jnp.cos(ang)], axis=-1)

    return emb1d(jnp.arange(rows, dtype=F32)), emb1d(jnp.arange(GRID_W, dtype=F32))


def _filter_feats(L):
    n1h = L // FFT_N2
    i_, h_, b_, a_ = jnp.meshgrid(jnp.arange(n1h // 8, dtype=I32), jnp.arange(2, dtype=I32),
                                  jnp.arange(FFT_N2, dtype=I32), jnp.arange(8, dtype=I32), indexing="ij")
    n = ((8 * i_ + a_ + h_ * n1h) * FFT_N2 + b_).reshape(-1)
    t = jnp.where(n <= L, n, 2 * L - n).astype(F32)
    t01 = t / max(L - 1, 1)
    w = 2.0 * math.pi * t / L
    bands = jnp.linspace(1e-4, FILT_BANDS - 1, FILT_BANDS, dtype=F32)
    feats = jnp.concatenate([t01[None, :], jnp.cos(bands[:, None] * w[None, :]), -jnp.sin(bands[:, None] * w[None, :]),
                             jnp.zeros((64 - 33, 2 * L), F32)], axis=0)
    return feats


def _pad_rows(a, rows):
    return jnp.concatenate([a, jnp.zeros((rows - a.shape[0],) + a.shape[1:], a.dtype)], axis=0)


def _layer(x, c, ctx, c_ctx, w_ada, b_ada, g_mix, g_ffn, w_in, b_gates, conv_k_w, conv_k_b,
           conv_q_w, conv_q_b, g_head, conv_hy_w, conv_hy_b, filt_w1, filt_b1, filt_freq1,
           filt_w2, filt_b2, filt_freq2, filt_w3, filt_b3, hy_dskip, w_out, w_router, b_router,
           w1_e, w3_e, w2_e, w1_s, w3_s, w2_s, g_final):
    L, d = x.shape
    lc = ctx.shape[0]
    row = lambda v: v.reshape(1, -1)

    cc = _pad_rows(jnp.stack([c, c_ctx], axis=0), 8)
    mods = _mod_call(cc, w_ada, row(b_ada))
    sh1, sc1, g1, sh2, sc2, g2 = [mods[0:1, k * d:(k + 1) * d] for k in range(6)]
    csh1, csc1 = mods[1:2, 0:d], mods[1:2, d:2 * d]

    w_r, w_g = _wprep_call(jnp.transpose(w_in[0]))
    bg = jnp.concatenate([b_gates, jnp.zeros((GATE_PAD - 4 * N_HEADS,), F32)]).reshape(1, GATE_PAD)
    e_row, e_col = _pos_tables(L)
    conv_w = jnp.concatenate([conv_k_w, conv_q_w], axis=1)
    conv_b = jnp.concatenate([conv_k_b, conv_q_b]).reshape(1, -1)
    conv_s = jnp.concatenate([jnp.ones((QK_W,), F32), jnp.full((QK_W,), QK_HEAD ** -0.5, F32)]).reshape(1, -1)

    z_c, gt_c = _inproj_call(ctx, jnp.zeros((8, d // 2), F32), e_col, row(g_mix), csh1, csc1, w_r, w_g,
                             use_pos=False, tm=min(lc, 256))
    kq_c = _conv_kq_call(z_c, conv_w, conv_b, conv_s, tm=min(lc, 256))
    s0 = jnp.zeros((2 * N_HEADS, CHUNK, 256), F32)
    m0 = jnp.zeros((2, 8, 128), F32)
    _, _, s_ctx, m_ctx = _mlstm_call(kq_c, z_c, gt_c, bg, s0, m0)

    z, gates = _inproj_call(x, e_row, e_col, row(g_mix), sh1, sc1, w_r, w_g, use_pos=True, tm=min(L, 1024))
    kq = _conv_kq_call(z, conv_w, conv_b, conv_s, tm=min(L, 512))
    x0c, xv = _conv_hy_call(z, conv_hy_w, row(conv_hy_b), tm=min(L, 512))
    hf, hb, _, _ = _mlstm_call(kq, z, gates, bg, s_ctx, m_ctx)

    n1 = 2 * L // FFT_N2
    cst = _fft_consts(n1)
    rates = jnp.linspace(-math.log(DECAY_TARGET) / SLOW_DECAY_PCT, -math.log(DECAY_TARGET) / FAST_DECAY_PCT,
                         HY_W, dtype=F32).reshape(1, -1)
    w1t = jnp.transpose(_pad_rows(filt_w1, 64))
    colrep = lambda v: jnp.broadcast_to(v.reshape(-1, 1), (v.shape[0], 128))
    kf, l1 = _filt_call(_filter_feats(L), w1t, colrep(filt_b1), colrep(filt_freq1), jnp.transpose(filt_w2),
                        colrep(filt_b2), colrep(filt_freq2), filt_w3, row(filt_b3), rates)
    cols = FFT_N2 * HY_W
    cb = 2048
    kar, kai = _fft1_call(cst["f1"], kf, cb, packed=True)
    kv = n1 // 2 + 1
    khat = _fft2_filt_call(kar.reshape(FFT_KP, FFT_N2, HY_W), kai.reshape(FFT_KP, FFT_N2, HY_W), cst, kv)
    uar, uai = _fft1_call(cst["f1"], xv.reshape(n1 // 2, cols), cb)
    br, bi = _fft2_conv_call(uar.reshape(FFT_KP, FFT_N2, HY_W), uai.reshape(FFT_KP, FFT_N2, HY_W), khat, cst, kv)
    reps = cb // HY_W
    il_t = jnp.tile(1.0 / l1, (1, reps))
    ds_t = jnp.tile(row(hy_dskip), (1, reps))
    yh = _ifft1_call(cst, br.reshape(FFT_KP, cols), bi.reshape(FFT_KP, cols),
                     x0c.reshape(n1 // 2, cols), xv.reshape(n1 // 2, cols), il_t, ds_t, cb).reshape(L, HY_W)

    wo = w_out.astype(BF16)
    x1, h2s, s_t = _outproj_call(hf, hb, z, yh, x, e_row, e_col, row(g_head), wo[:MV_W], wo[MV_W:],
                                g1, row(g_ffn), sh2, sc2, jnp.transpose(w_router), tm=min(L, 256))

    b_col = jnp.broadcast_to(b_router.reshape(N_EXPERTS, 1), (N_EXPERTS, 128))
    eidx, wts, rank, cnt2 = _route_call(s_t, b_col, tt=min(L, 1024))
    cnt = cnt2[:, 0]
    pcnt = (cnt + MOE_ROWS - 1) // MOE_ROWS * MOE_ROWS
    pend = jnp.cumsum(pcnt)
    pstart = pend - pcnt
    rows = L * TOP_K + N_EXPERTS * MOE_ROWS
    nt = rows // MOE_ROWS
    tile_row = jnp.arange(nt, dtype=I32) * MOE_ROWS
    tile_e = jnp.minimum(jnp.sum((pend[None, :] <= tile_row[:, None]).astype(I32), axis=1), N_EXPERTS - 1)
    n_used = (pend[-1] // MOE_ROWS).astype(I32).reshape(1)
    pos = _posk_call(pstart.astype(I32), eidx, rank)

    xs, sh = _dispatch_call(cnt, pstart.astype(I32), pcnt.astype(I32), h2s, pos, w1_s.astype(BF16),
                            w3_s.astype(BF16), w2_s.astype(BF16), rows, td=min(L, 256))
    ex = jnp.arange(N_EXPERTS, dtype=I32)
    nonempty = pcnt > 0
    nxt = jnp.min(jnp.where((ex[None, :] > ex[:, None]) & nonempty[None, :], ex[None, :], N_EXPERTS), axis=1)
    par = (jnp.cumsum(nonempty.astype(I32)) + 1) % 2
    ys = _moe_call(tile_e, n_used, nxt.astype(I32), par.astype(I32), xs, w1_e, w3_e, w2_e)
    return _final_call(x1, sh, pos, wts, ys, g2, row(g_final), tf=min(L, 256))


def kernel(x, c, ctx, c_ctx, w_ada, b_ada, g_mix, g_ffn, w_in, b_gates, conv_k_w, conv_k_b, conv_q_w,
           conv_q_b, g_head, conv_hy_w, conv_hy_b, filt_w1, filt_b1, filt_freq1, filt_w2, filt_b2,
           filt_freq2, filt_w3, filt_b3, hy_dskip, w_out, w_router, b_router, w1_e, w3_e, w2_e,
           w1_s, w3_s, w2_s, g_final):
    assert x.shape[0] == 1 and w_ada.shape[0] == 1, "one batch element, one layer"
    out = _layer(x[0], c[0], ctx[0], c_ctx, w_ada[0], b_ada[0], g_mix[0], g_ffn[0], w_in, b_gates[0],
                 conv_k_w[0], conv_k_b[0], conv_q_w[0], conv_q_b[0], g_head[0], conv_hy_w[0], conv_hy_b[0],
                 filt_w1[0], filt_b1[0], filt_freq1[0], filt_w2[0], filt_b2[0], filt_freq2[0], filt_w3[0],
                 filt_b3[0], hy_dskip[0], w_out[0], w_router[0], b_router[0], w1_e[0], w3_e[0], w2_e[0],
                 w1_s[0], w3_s[0], w2_s[0], g_final)
    return out[None]
```

```python
import functools
import math

import numpy as np
import jax
import jax.numpy as jnp
from jax import lax
from jax.experimental import pallas as pl
from jax.experimental.pallas import tpu as pltpu

F32 = jnp.float32
BF16 = jnp.bfloat16
I32 = jnp.int32
U32 = jnp.uint32

D_MODEL = 2048
GRID_W = 64
N_HEADS = 8
QK_HEAD = 64
V_HEAD = 128
QK_W = N_HEADS * QK_HEAD
MV_W = N_HEADS * V_HEAD
HY_W = D_MODEL - MV_W
CHUNK = 128
FILT_BANDS = 16
FILT_HIDDEN = 64
DECAY_TARGET = 1e-2
FAST_DECAY_PCT = 0.3
SLOW_DECAY_PCT = 1.5
N_EXPERTS = 64
N_GROUPS = 8
E_PER_GROUP = 8
TOPK_GROUPS = 4
TOP_K = 8
D_EXPERT = 512
ROUTE_SCALE = 2.5
EPS = 1e-6
OFF_K = 0
OFF_V = OFF_K + QK_W
OFF_G = OFF_V + MV_W
OFF_Q = OFF_G + 4 * N_HEADS
OFF_O = OFF_Q + QK_W
OFF_HY = OFF_O + MV_W

ZC_KQ, ZC_V, ZC_O, ZC_X0, ZC_X1, ZC_HV = 0, 1, 2, 3, 4, 5
Z_COLS = 6 * 1024
GATE_PAD = 128

NEG = -1e30
MIB = 1024 * 1024

FFT_N2 = 128
FFT_KP = 144

MOE_ROWS = 256
MOE_TILES_PER_STEP = 2


def _cparams(sem, vmem_mb, flags=None):
    return pltpu.CompilerParams(dimension_semantics=sem, vmem_limit_bytes=vmem_mb * MIB, flags=flags)


def _split2(x):
    hi = x.astype(BF16)
    lo = (x - hi.astype(F32)).astype(BF16)
    return hi, lo


_NN = (((1,), (0,)), ((), ()))
_NT = (((1,), (1,)), ((), ()))
_TN = (((0,), (0,)), ((), ()))


def _dg(a, b, dims=_NN):
    return lax.dot_general(a, b, dims, preferred_element_type=F32)


def _dot3(a, b, dims=_NN):
    ah, al = _split2(a)
    bh, bl = _split2(b)
    return _dg(ah, bh, dims) + _dg(al, bh, dims) + _dg(ah, bl, dims)


def _sigmoid(x):
    return 1.0 / (1.0 + jnp.exp(-x))


def _silu(x):
    return x * _sigmoid(x)


def _pack_pair(a, b):
    hi = lax.bitcast_convert_type(a.astype(BF16).astype(F32), U32)
    lo = lax.bitcast_convert_type(b.astype(BF16).astype(F32), U32)
    return hi | (lo >> 16)


def _unpack_pair(w):
    a = lax.bitcast_convert_type(w & jnp.uint32(0xFFFF0000), F32)
    b = lax.bitcast_convert_type(w << 16, F32)
    return a, b


SLAB = 8
SLAB_W = 128


def _store_slabs(ref, w, base=0):
    r = w.shape[0]
    for j in range(SLAB):
        ref[pl.ds(base + j, r, stride=SLAB), :] = w[:, j * SLAB_W:(j + 1) * SLAB_W]


def _load_slabs(ref, r, base=0):
    return jnp.concatenate([ref[pl.ds(base + j, r, stride=SLAB), :] for j in range(SLAB)], axis=1)


def _norm_mod(x, g, sh, sc):
    ms = jnp.mean(x * x, axis=-1, keepdims=True)
    return (x * lax.rsqrt(ms + EPS) * g) * (1.0 + sc) + sh


def _add_pos(x, erow, ecol):
    tm, d = x.shape
    half = d // 2
    parts = []
    for r in range(tm // GRID_W):
        xs = x[r * GRID_W:(r + 1) * GRID_W, :]
        parts.append(jnp.concatenate([xs[:, :half] + erow[r:r + 1, :], xs[:, half:] + ecol], axis=-1))
    return parts[0] if len(parts) == 1 else jnp.concatenate(parts, axis=0)


def _mod_body(cc_ref, w_ref, b_ref, o_ref):
    o_ref[...] = _dot3(_silu(cc_ref[...]), w_ref[...]) + b_ref[...]


def _mod_call(cc, w, b):
    d, n = w.shape
    tn = 1024
    return pl.pallas_call(
        _mod_body,
        grid=(n // tn,),
        in_specs=[pl.BlockSpec((8, d), lambda j: (0, 0)),
                  pl.BlockSpec((d, tn), lambda j: (0, j)),
                  pl.BlockSpec((1, tn), lambda j: (0, j))],
        out_specs=pl.BlockSpec((8, tn), lambda j: (0, j)),
        out_shape=jax.ShapeDtypeStruct((8, n), F32),
        compiler_params=_cparams(("arbitrary",), 40),
        name="mod",
    )(cc, w, b)


def _wprep_body(w_ref, wr_ref, wg_ref):
    w = w_ref[...]
    wt = jnp.concatenate([w[OFF_K:OFF_V], w[OFF_Q:OFF_O], w[OFF_V:OFF_G], w[OFF_O:]], axis=0)
    wr_ref[...] = jnp.transpose(wt).astype(BF16)
    g = jnp.concatenate([w[OFF_G:OFF_Q], jnp.zeros((GATE_PAD - 4 * N_HEADS, w.shape[1]), F32)], axis=0)
    wg_ref[...] = jnp.transpose(g)


def _wprep_call(w_t):
    n, d = w_t.shape
    tr = 256
    return pl.pallas_call(
        _wprep_body,
        grid=(d // tr,),
        in_specs=[pl.BlockSpec((n, tr), lambda i: (0, i))],
        out_specs=[pl.BlockSpec((tr, Z_COLS), lambda i: (i, 0)), pl.BlockSpec((tr, GATE_PAD), lambda i: (i, 0))],
        out_shape=[jax.ShapeDtypeStruct((d, Z_COLS), BF16), jax.ShapeDtypeStruct((d, GATE_PAD), F32)],
        compiler_params=_cparams(("arbitrary",), 32),
        name="wprep",
    )(w_t)


def _inproj_body(use_pos, x_ref, erow_ref, ecol_ref, gm_ref, sh_ref, sc_ref, w_ref, wg_ref,
                 z_ref, g_ref, h_scr):
    @pl.when(pl.program_id(1) == 0)
    def _():
        x = x_ref[...]
        if use_pos:
            x = _add_pos(x, erow_ref[...], ecol_ref[...])
        h = _norm_mod(x, gm_ref[...], sh_ref[...], sc_ref[...])
        h_scr[...] = h.astype(BF16)
        g_ref[...] = _dot3(h, wg_ref[...])

    z_ref[...] = jnp.dot(h_scr[...], w_ref[...], preferred_element_type=F32).astype(BF16)


def _inproj_call(x, erow, ecol, gm, sh, sc, w, wg, use_pos, tm):
    m, d = x.shape
    tn = 1024
    er = tm // GRID_W if use_pos else erow.shape[0]
    row_map = (lambda i, j: (i, 0)) if use_pos else (lambda i, j: (0, 0))
    return pl.pallas_call(
        functools.partial(_inproj_body, use_pos),
        grid=(m // tm, Z_COLS // tn),
        in_specs=[pl.BlockSpec((tm, d), lambda i, j: (i, 0)),
                  pl.BlockSpec((er, d // 2), row_map),
                  pl.BlockSpec((GRID_W, d // 2), lambda i, j: (0, 0)),
                  pl.BlockSpec((1, d), lambda i, j: (0, 0)),
                  pl.BlockSpec((1, d), lambda i, j: (0, 0)),
                  pl.BlockSpec((1, d), lambda i, j: (0, 0)),
                  pl.BlockSpec((d, tn), lambda i, j: (0, j)),
                  pl.BlockSpec((d, GATE_PAD), lambda i, j: (0, 0))],
        out_specs=[pl.BlockSpec((tm, tn), lambda i, j: (i, j)),
                   pl.BlockSpec((tm, GATE_PAD), lambda i, j: (i, 0))],
        out_shape=[jax.ShapeDtypeStruct((m, Z_COLS), BF16),
                   jax.ShapeDtypeStruct((m, GATE_PAD), F32)],
        scratch_shapes=[pltpu.VMEM((tm, d), BF16)],
        compiler_params=_cparams(("arbitrary", "arbitrary"), 48),
        name="inproj",
    )(x, erow, ecol, gm, sh, sc, w, wg)


def _conv3(zc, zp, zn, w, b, first, last):
    tm = zc.shape[0]
    row = lax.broadcasted_iota(I32, zc.shape, 0)
    prev_row = jnp.where(first, 0.0, zp[7:8, :])
    next_row = jnp.where(last, 0.0, zn[0:1, :])
    xm = jnp.where(row == 0, prev_row, pltpu.roll(zc, 1, 0))
    xp = jnp.where(row == tm - 1, next_row, pltpu.roll(zc, tm - 1, 0))
    return xm * w[0:1, :] + zc * w[1:2, :] + xp * w[2:3, :] + b


def _conv_kq_body(zc_ref, zp_ref, zn_ref, w_ref, b_ref, s_ref, o_ref):
    i = pl.program_id(0)
    u = _conv3(zc_ref[...].astype(F32), zp_ref[...].astype(F32), zn_ref[...].astype(F32),
               w_ref[...], b_ref[...], i == 0, i == pl.num_programs(0) - 1)
    o_ref[...] = (_silu(u) * s_ref[...]).astype(BF16)


def _halo_specs(tm, m, cb):
    nb8 = m // 8
    return [pl.BlockSpec((tm, 1024), lambda i: (i, cb)),
            pl.BlockSpec((8, 1024), lambda i: (jnp.maximum(i * (tm // 8) - 1, 0), cb)),
            pl.BlockSpec((8, 1024), lambda i: (jnp.minimum((i + 1) * (tm // 8), nb8 - 1), cb))]


def _conv_kq_call(z, w, b, s, tm):
    m = z.shape[0]
    vec = pl.BlockSpec((1, 1024), lambda i: (0, 0))
    return pl.pallas_call(
        _conv_kq_body,
        grid=(m // tm,),
        in_specs=_halo_specs(tm, m, ZC_KQ) + [pl.BlockSpec((3, 1024), lambda i: (0, 0)), vec, vec],
        out_specs=pl.BlockSpec((tm, 1024), lambda i: (i, 0)),
        out_shape=jax.ShapeDtypeStruct((m, 1024), BF16),
        compiler_params=_cparams(("arbitrary",), 32),
        name="conv_kq",
    )(z, z, z, w, b, s)


def _conv_hy_body(ac_ref, ap_ref, an_ref, bc_ref, bp_ref, bn_ref, cc_ref, cp_ref, cn_ref,
                  w_ref, b_ref, x0_ref, xv_ref):
    i = pl.program_id(0)
    first, last = i == 0, i == pl.num_programs(0) - 1
    w = w_ref[...]
    b = b_ref[...]

    def cv(c, p, n, k):
        return _conv3(c[...].astype(F32), p[...].astype(F32), n[...].astype(F32),
                      w[:, k * 1024:(k + 1) * 1024], b[:, k * 1024:(k + 1) * 1024], first, last)

    x0_ref[...] = cv(ac_ref, ap_ref, an_ref, 0).astype(BF16)
    xv_ref[...] = (cv(bc_ref, bp_ref, bn_ref, 1) * cv(cc_ref, cp_ref, cn_ref, 2)).astype(BF16)


def _conv_hy_call(z, w, b, tm):
    m = z.shape[0]
    out = pl.BlockSpec((tm, 1024), lambda i: (i, 0))
    return pl.pallas_call(
        _conv_hy_body,
        grid=(m // tm,),
        in_specs=(_halo_specs(tm, m, ZC_X0) + _halo_specs(tm, m, ZC_X1) + _halo_specs(tm, m, ZC_HV)
                  + [pl.BlockSpec((3, 3072), lambda i: (0, 0)), pl.BlockSpec((1, 3072), lambda i: (0, 0))]),
        out_specs=[out, out],
        out_shape=[jax.ShapeDtypeStruct((m, 1024), BF16), jax.ShapeDtypeStruct((m, 1024), BF16)],
        compiler_params=_cparams(("arbitrary",), 32),
        name="conv_hy",
    )(z, z, z, z, z, z, z, z, z, w, b)


def _mlstm_body(kqf_ref, vf_ref, gf_ref, kqb_ref, vb_ref, gb_ref, bg_ref, s0_ref, m0_ref,
                hf_ref, hb_ref, sfin_ref, mfin_ref, s_scr, m_scr):
    j = pl.program_id(0)

    @pl.when(j == 0)
    def _():
        s_scr[...] = s0_ref[...]
        m_scr[...] = m0_ref[...]

    r = lax.broadcasted_iota(I32, (CHUNK, CHUNK), 0)
    c = lax.broadcasted_iota(I32, (CHUNK, CHUNK), 1)
    ones_b = jnp.ones((CHUNK, CHUNK), BF16)
    bg = bg_ref[...]

    def lane_bcast(x, h, width=CHUNK):
        return jnp.broadcast_to(x[:, h:h + 1], (x.shape[0], width))

    for d in range(2):
        kq = (kqf_ref, kqb_ref)[d][...]
        v = (vf_ref, vb_ref)[d][...]
        g_all = (gf_ref, gb_ref)[d][...] + bg
        out_ref = (hf_ref, hb_ref)[d]
        tri = (r >= c) if d == 0 else (c >= r)
        tri_b = jnp.where(tri, 1.0, 0.0).astype(BF16)
        gi = g_all if d == 0 else pltpu.roll(g_all, CHUNK - 16, 1)
        gfp = pltpu.roll(g_all, CHUNK - 8 - 16 * d, 1)
        lf = jnp.minimum(gfp, 0.0) - jnp.log(1.0 + jnp.exp(-jnp.abs(gfp)))
        l1 = lf.astype(BF16)
        r1 = lf - l1.astype(F32)
        l2 = r1.astype(BF16)
        l3 = (r1 - l2.astype(F32)).astype(BF16)
        bcum = _dg(tri_b, l1) + _dg(tri_b, l2) + _dg(tri_b, l3)
        gtot = bcum[CHUNK - 1:CHUNK, :] if d == 0 else bcum[0:1, :]
        acol = gtot - bcum + gi
        m_loc = jnp.max(acol, axis=0, keepdims=True)
        m_st = m_scr[d, 0:1, :]
        m_new = jnp.maximum(gtot + m_st, m_loc)
        sp8 = jnp.broadcast_to(jnp.exp(gtot + m_st - m_new), (8, CHUNK))
        wst = jnp.exp(acol - m_new)
        rr = gi - bcum
        cm = rr
        for sh in (1, 2, 4, 8, 16, 32, 64):
            if d == 0:
                cm = jnp.maximum(cm, jnp.where(r >= sh, pltpu.roll(cm, sh, 0), NEG))
            else:
                cm = jnp.maximum(cm, jnp.where(r < CHUNK - sh, pltpu.roll(cm, CHUNK - sh, 0), NEG))
        mt = jnp.maximum(bcum + m_st, bcum + cm)
        c1 = bcum - mt
        rt = jnp.transpose(rr)
        wt = jnp.transpose(wst)
        m8 = jnp.broadcast_to(m_st, (8, CHUNK))
        kts = {}

        for h in range(N_HEADS):
            p, half = divmod(h, 2)
            lm = (c // QK_HEAD) == half
            kp = kq[:, p * 128:(p + 1) * 128]
            qp = kq[:, QK_W + p * 128:QK_W + (p + 1) * 128]
            vaug = jnp.concatenate([v[:, h * 128:(h + 1) * 128], ones_b], axis=1)
            qm = jnp.where(lm, qp, jnp.zeros_like(qp))
            c1b = lane_bcast(c1, h)
            pm = jnp.exp(jnp.where(tri, c1b + rt[h:h + 1, :], NEG))
            s = (_dg(qm, kp, _NT) * pm).astype(BF16)
            m_in = jnp.tile(lane_bcast(m8, h), (CHUNK // 8, 1))
            qs = (qm.astype(F32) * jnp.exp(c1b + m_in)).astype(BF16)
            st = s_scr[d * N_HEADS + h]
            tot = _dg(jnp.concatenate([s, qs], axis=1), jnp.concatenate([vaug, st.astype(BF16)], axis=0))
            den = jnp.maximum(jnp.abs(tot[:, 128:]), jnp.exp(-lane_bcast(mt, h)))
            out_ref[:, h * 128:(h + 1) * 128] = (tot[:, :128] / den).astype(BF16)
            if p not in kts:
                kts[p] = jnp.transpose(kp.astype(F32))
            kw = jnp.where((r // QK_HEAD) == half, kts[p] * wt[h:h + 1, :], 0.0).astype(BF16)
            spb = jnp.tile(lane_bcast(sp8, h, 256), (CHUNK // 8, 1))
            s_scr[d * N_HEADS + h] = spb * st + _dg(kw, vaug)
        m_scr[d, 0:1, :] = m_new

    @pl.when(j == pl.num_programs(0) - 1)
    def _():
        sfin_ref[...] = s_scr[...]
        mfin_ref[...] = m_scr[...]


def _mlstm_call(kq, z, gates, bg, s0, m0):
    m = kq.shape[0]
    nc = m // CHUNK
    fwd = lambda cb: (lambda j: (j, cb))
    bwd = lambda cb: (lambda j: (nc - 1 - j, cb))
    st_spec = pl.BlockSpec((2 * N_HEADS, CHUNK, 256), lambda j: (0, 0, 0))
    m_spec = pl.BlockSpec((2, 8, 128), lambda j: (0, 0, 0))
    return pl.pallas_call(
        _mlstm_body,
        grid=(nc,),
        in_specs=[pl.BlockSpec((CHUNK, 1024), fwd(0)), pl.BlockSpec((CHUNK, 1024), fwd(ZC_V)),
                  pl.BlockSpec((CHUNK, GATE_PAD), fwd(0)),
                  pl.BlockSpec((CHUNK, 1024), bwd(0)), pl.BlockSpec((CHUNK, 1024), bwd(ZC_V)),
                  pl.BlockSpec((CHUNK, GATE_PAD), bwd(0)),
                  pl.BlockSpec((1, GATE_PAD), lambda j: (0, 0)), st_spec, m_spec],
        out_specs=[pl.BlockSpec((CHUNK, 1024), fwd(0)), pl.BlockSpec((CHUNK, 1024), bwd(0)), st_spec, m_spec],
        out_shape=[jax.ShapeDtypeStruct((m, 1024), BF16), jax.ShapeDtypeStruct((m, 1024), BF16),
                   jax.ShapeDtypeStruct((2 * N_HEADS, CHUNK, 256), F32),
                   jax.ShapeDtypeStruct((2, 8, 128), F32)],
        scratch_shapes=[pltpu.VMEM((2 * N_HEADS, CHUNK, 256), F32), pltpu.VMEM((2, 8, 128), F32)],
        compiler_params=_cparams(("arbitrary",), 32),
        name="mlstm",
    )(kq, z, gates, kq, z, gates, bg, s0, m0)


def _filt_body(seq_len, ft_ref, w1_ref, b1_ref, f1_ref, w2_ref, b2_ref, f2_ref, w3_ref, b3_ref, rt_ref,
               kf_ref, l1_ref):
    i = pl.program_id(0)
    tn = ft_ref.shape[1]
    hp = tn // 2
    reps = tn // 128
    col = lambda ref: jnp.tile(ref[...], (1, reps))
    h1 = jnp.sin(col(f1_ref) * (_dot3(w1_ref[...], ft_ref[...]) + col(b1_ref)))
    h2 = jnp.sin(col(f2_ref) * (_dot3(w2_ref[...], h1) + col(b2_ref)))
    r = lax.broadcasted_iota(I32, (hp, HY_W), 0)
    n_fwd = (8 * i + (r & 7)) * FFT_N2 + (r >> 3)
    rates = rt_ref[...]
    halves = []
    l1 = jnp.zeros((1, HY_W), F32)
    for hx in range(2):
        h = (_dot3(h2[:, hx * hp:(hx + 1) * hp], w3_ref[:, hx * HY_W:(hx + 1) * HY_W], _TN)
             + b3_ref[:, hx * HY_W:(hx + 1) * HY_W])
        n = n_fwd + hx * seq_len
        t01 = jnp.where(n <= seq_len, n, 2 * seq_len - n).astype(F32) / float(max(seq_len - 1, 1))
        h = jnp.where(n == seq_len, 0.0, h * jnp.exp(-t01 * rates))
        l1 = l1 + jnp.sum(jnp.abs(h), axis=0, keepdims=True)
        halves.append(h)
    word = _pack_pair(halves[0], halves[1])
    for b in range(FFT_N2):
        kf_ref[:, b * HY_W:(b + 1) * HY_W] = word[8 * b:8 * b + 8, :]

    @pl.when(i == 0)
    def _():
        l1_ref[...] = jnp.zeros_like(l1_ref)

    l1_ref[...] += l1


def _filt_call(feats_t, w1t, b1, f1, w2t, b2, f2, w3, b3, rates):
    n = feats_t.shape[1]
    seq_len = n // 2
    tn = 2 * 8 * FFT_N2
    c64 = lambda shape: pl.BlockSpec(shape, lambda i: (0, 0))
    return pl.pallas_call(
        functools.partial(_filt_body, seq_len),
        grid=(n // tn,),
        in_specs=[pl.BlockSpec((64, tn), lambda i: (0, i)),
                  c64((64, 64)), c64((64, 128)), c64((64, 128)), c64((64, 64)), c64((64, 128)), c64((64, 128)),
                  c64((64, 2 * HY_W)), c64((1, 2 * HY_W)), c64((1, HY_W))],
        out_specs=[pl.BlockSpec((8, FFT_N2 * HY_W), lambda i: (i, 0)), pl.BlockSpec((1, HY_W), lambda i: (0, 0))],
        out_shape=[jax.ShapeDtypeStruct((seq_len // FFT_N2, FFT_N2 * HY_W), U32),
                   jax.ShapeDtypeStruct((1, HY_W), F32)],
        compiler_params=_cparams(("arbitrary",), 48),
        name="filt",
    )(feats_t, w1t, b1, f1, w2t, b2, f2, w3, b3, rates)


def _fft_consts(n1_rows):
    n = n1_rows * FFT_N2
    kv = n1_rows // 2 + 1
    k1 = np.arange(FFT_KP, dtype=np.float64)
    valid = (k1 < kv).astype(np.float64)
    n1 = np.arange(n1_rows, dtype=np.float64)
    th1 = 2.0 * np.pi * np.outer(k1, n1) / n1_rows
    f1 = np.concatenate([np.cos(th1) * valid[:, None], -np.sin(th1) * valid[:, None]], axis=0)
    n2 = np.arange(FFT_N2, dtype=np.float64)
    tht = 2.0 * np.pi * np.outer(k1, n2) / n
    rep = lambda a: jnp.broadcast_to(jnp.asarray(a, F32)[:, :, None], (FFT_KP, FFT_N2, 128))
    twr = rep(np.cos(tht) * valid[:, None])
    twi = rep(-np.sin(tht) * valid[:, None])
    th2 = 2.0 * np.pi * np.outer(n2, n2) / FFT_N2
    cs, sn = np.cos(th2), np.sin(th2)
    f2p = np.block([[cs, sn], [-sn, cs]])
    f2pc = np.block([[cs, -sn], [sn, cs]])
    wk = np.where((k1 == 0) | (k1 == kv - 1), 1.0, 2.0) * valid / n
    half = n1_rows // 2
    thi = 2.0 * np.pi * np.outer(n1[:half], k1) / n1_rows
    gc = np.cos(thi) * wk[None, :]
    gs = np.sin(thi) * wk[None, :]
    as_bf = lambda a: jnp.asarray(a, F32).astype(BF16)
    return dict(f1=as_bf(f1), twr=twr, twi=twi,
                f2p=as_bf(f2p), f2pc=as_bf(f2pc), gc=as_bf(gc), gs=as_bf(gs))


def _fft1_body(f_ref, x_ref, ar_ref, ai_ref):
    o = _dg(f_ref[...], x_ref[...])
    ar_ref[...] = o[:FFT_KP].astype(BF16)
    ai_ref[...] = o[FFT_KP:].astype(BF16)


def _fft1_packed_body(f_ref, x_ref, ar_ref, ai_ref):
    k = x_ref.shape[0]
    hi, lo = _unpack_pair(x_ref[...])
    o = _dg(f_ref[:, 0:k], hi.astype(BF16)) + _dg(f_ref[:, k:], lo.astype(BF16))
    ar_ref[...] = o[:FFT_KP].astype(BF16)
    ai_ref[...] = o[FFT_KP:].astype(BF16)


def _fft1_call(f1, x2d, cb, packed=False):
    k, cols = x2d.shape
    f1 = f1[:, :2 * k] if packed else f1[:, :k]
    out = pl.BlockSpec((FFT_KP, cb), lambda i: (0, i))
    sh = jax.ShapeDtypeStruct((FFT_KP, cols), BF16)
    return pl.pallas_call(
        _fft1_packed_body if packed else _fft1_body,
        grid=(cols // cb,),
        in_specs=[pl.BlockSpec(f1.shape, lambda i: (0, 0)), pl.BlockSpec((k, cb), lambda i: (0, i))],
        out_specs=[out, out],
        out_shape=[sh, sh],
        compiler_params=_cparams(("arbitrary",), 32),
        name="fft1",
    )(f1, x2d)


def _twiddled(ar_ref, ai_ref, twr_ref, twi_ref, reps):
    a_r = ar_ref[...].astype(F32)
    a_i = ai_ref[...].astype(F32)
    tr = jnp.tile(twr_ref[...], (1, reps))
    ti = jnp.tile(twi_ref[...], (1, reps))
    st = jnp.concatenate([a_r * tr - a_i * ti, a_r * ti + a_i * tr], axis=0).astype(BF16)
    return st, tr, ti


FFT2_ROWS = 2


def _fft2_filt_body(kv, ar_ref, ai_ref, twr_ref, twi_ref, f2p_ref, k_ref):
    for j in range(FFT2_ROWS):
        k1 = pl.program_id(0) * FFT2_ROWS + j

        @pl.when(k1 < kv)
        def _():
            st, _, _ = _twiddled(ar_ref.at[j], ai_ref.at[j], twr_ref.at[j], twi_ref.at[j], ar_ref.shape[-1] // 128)
            k_ref[j] = _dg(f2p_ref[...], st).astype(BF16)

        @pl.when(k1 >= kv)
        def _():
            k_ref[j] = jnp.zeros(k_ref.shape[1:], BF16)


def _fft2_conv_body(kv, ar_ref, ai_ref, twr_ref, twi_ref, k_ref, f2p_ref, f2pc_ref, br_ref, bi_ref):
    for j in range(FFT2_ROWS):
        k1 = pl.program_id(0) * FFT2_ROWS + j

        @pl.when(k1 < kv)
        def _():
            st, tr, ti = _twiddled(ar_ref.at[j], ai_ref.at[j], twr_ref.at[j], twi_ref.at[j], ar_ref.shape[-1] // 128)
            x = _dg(f2p_ref[...], st)
            xr, xi = x[:FFT_N2], x[FFT_N2:]
            kr = k_ref[j, :FFT_N2, :].astype(F32)
            ki = k_ref[j, FFT_N2:, :].astype(F32)
            sy = jnp.concatenate([xr * kr - xi * ki, xr * ki + xi * kr], axis=0).astype(BF16)
            b = _dg(f2pc_ref[...], sy)
            b_r, b_i = b[:FFT_N2], b[FFT_N2:]
            br_ref[j] = (b_r * tr + b_i * ti).astype(BF16)
            bi_ref[j] = (b_i * tr - b_r * ti).astype(BF16)

        @pl.when(k1 >= kv)
        def _():
            br_ref[j] = jnp.zeros(br_ref.shape[1:], BF16)
            bi_ref[j] = jnp.zeros(bi_ref.shape[1:], BF16)


def _fft2_specs(ch, kv):
    src = lambda i: (jnp.minimum(i, (kv - 1) // FFT2_ROWS), 0, 0)
    blk = pl.BlockSpec((FFT2_ROWS, FFT_N2, ch), src)
    tw = pl.BlockSpec((FFT2_ROWS, FFT_N2, 128), src)
    mat = pl.BlockSpec((2 * FFT_N2, 2 * FFT_N2), lambda i: (0, 0))
    return blk, tw, mat, src


def _fft2_filt_call(ar, ai, cst, kv):
    ch = ar.shape[-1]
    blk, tw, mat, _ = _fft2_specs(ch, kv)
    return pl.pallas_call(
        functools.partial(_fft2_filt_body, kv),
        grid=(FFT_KP // FFT2_ROWS,),
        in_specs=[blk, blk, tw, tw, mat],
        out_specs=pl.BlockSpec((FFT2_ROWS, 2 * FFT_N2, ch), lambda i: (i, 0, 0)),
        out_shape=jax.ShapeDtypeStruct((FFT_KP, 2 * FFT_N2, ch), BF16),
        compiler_params=_cparams(("arbitrary",), 32),
        name="fft2_filt",
    )(ar, ai, cst["twr"], cst["twi"], cst["f2p"])


def _fft2_conv_call(ar, ai, khat, cst, kv):
    ch = ar.shape[-1]
    blk, tw, mat, src = _fft2_specs(ch, kv)
    sh = jax.ShapeDtypeStruct((FFT_KP, FFT_N2, ch), BF16)
    out = pl.BlockSpec((FFT2_ROWS, FFT_N2, ch), lambda i: (i, 0, 0))
    return pl.pallas_call(
        functools.partial(_fft2_conv_body, kv),
        grid=(FFT_KP // FFT2_ROWS,),
        in_specs=[blk, blk, tw, tw, pl.BlockSpec((FFT2_ROWS, 2 * FFT_N2, ch), src), mat, mat],
        out_specs=[out, out],
        out_shape=[sh, sh],
        compiler_params=_cparams(("arbitrary",), 32),
        name="fft2_conv",
    )(ar, ai, cst["twr"], cst["twi"], khat, cst["f2p"], cst["f2pc"])


def _ifft1_body(gc_ref, gs_ref, br_ref, bi_ref, x0_ref, xv_ref, il_ref, ds_ref, o_ref):
    y = _dg(gc_ref[...], br_ref[...]) - _dg(gs_ref[...], bi_ref[...])
    o_ref[...] = (x0_ref[...].astype(F32)
                  * (y * il_ref[...] + ds_ref[...] * xv_ref[...].astype(F32))).astype(BF16)


def _ifft1_call(cst, br2d, bi2d, x0_2d, xv_2d, il_t, ds_t, cb):
    rows, cols = x0_2d.shape
    g = pl.BlockSpec((rows, FFT_KP), lambda i: (0, 0))
    kb = pl.BlockSpec((FFT_KP, cb), lambda i: (0, i))
    xb = pl.BlockSpec((rows, cb), lambda i: (0, i))
    vb = pl.BlockSpec((1, cb), lambda i: (0, 0))
    return pl.pallas_call(
        _ifft1_body,
        grid=(cols // cb,),
        in_specs=[g, g, kb, kb, xb, xb, vb, vb],
        out_specs=xb,
        out_shape=jax.ShapeDtypeStruct((rows, cols), BF16),
        compiler_params=_cparams(("arbitrary",), 32),
        name="ifft1",
    )(cst["gc"], cst["gs"], br2d, bi2d, x0_2d, xv_2d, il_t, ds_t)


def _outproj_body(hf_ref, hb_ref, zo_ref, yh_ref, x_ref, erow_ref, ecol_ref, gh_ref, wa_ref, wb_ref,
                  g1_ref, gf_ref, sh_ref, sc_ref, wr_ref, x1_ref, h2_ref, s_ref):
    hs = hf_ref[...].astype(F32) + hb_ref[...].astype(F32)
    gh = gh_ref[...]
    parts = []
    for h in range(N_HEADS):
        hh = hs[:, h * 128:(h + 1) * 128]
        ms = jnp.mean(hh * hh, axis=-1, keepdims=True)
        parts.append(hh * lax.rsqrt(ms + EPS) * gh[:, h * 128:(h + 1) * 128])
    ym = jnp.concatenate(parts, axis=-1) * _sigmoid(zo_ref[...].astype(F32))
    y = _dg(ym.astype(BF16), wa_ref[...]) + _dg(yh_ref[...], wb_ref[...])
    rp = x_ref.shape[0] // GRID_W
    erow8 = erow_ref[...]
    erow = erow8[0:rp, :]
    for q in range(1, 8 // rp):
        erow = jnp.where(pl.program_id(0) % (8 // rp) == q, erow8[q * rp:(q + 1) * rp, :], erow)
    x1 = _add_pos(x_ref[...], erow, ecol_ref[...]) + g1_ref[...] * y
    x1_ref[...] = x1
    h2 = _norm_mod(x1, gf_ref[...], sh_ref[...], sc_ref[...])
    half = h2.shape[1] // 2
    _store_slabs(h2_ref, _pack_pair(h2[:, :half], h2[:, half:]))
    s_ref[...] = _sigmoid(_dot3(wr_ref[...], h2, _NT))


def _outproj_call(hf, hb, z, yh, x, erow, ecol, gh, wa, wb, g1, gf, sh2, sc2, wrt, tm):
    m, d = x.shape
    row = lambda cb: pl.BlockSpec((tm, 1024), lambda i: (i, cb))
    vec = lambda n: pl.BlockSpec((1, n), lambda i: (0, 0))
    full = pl.BlockSpec((tm, d), lambda i: (i, 0))
    return pl.pallas_call(
        _outproj_body,
        grid=(m // tm,),
        in_specs=[row(0), row(0), row(ZC_O), row(0), full,
                  pl.BlockSpec((8, d // 2), lambda i: (i * (tm // GRID_W) // 8, 0)),
                  pl.BlockSpec((GRID_W, d // 2), lambda i: (0, 0)),
                  vec(MV_W),
                  pl.BlockSpec((MV_W, d), lambda i: (0, 0)), pl.BlockSpec((HY_W, d), lambda i: (0, 0)),
                  vec(d), vec(d), vec(d), vec(d),
                  pl.BlockSpec((N_EXPERTS, d), lambda i: (0, 0))],
        out_specs=[full, pl.BlockSpec((tm * SLAB, SLAB_W), lambda i: (i, 0)),
                   pl.BlockSpec((N_EXPERTS, tm), lambda i: (0, i))],
        out_shape=[jax.ShapeDtypeStruct((m, d), F32), jax.ShapeDtypeStruct((m * SLAB, SLAB_W), U32),
                   jax.ShapeDtypeStruct((N_EXPERTS, m), F32)],
        compiler_params=_cparams(("arbitrary",), 56),
        name="outproj",
    )(hf, hb, z, yh, x, erow, ecol, gh, wa, wb, g1, gf, sh2, sc2, wrt)


def _first_max(x, idx, sentinel):
    m = jnp.max(x, axis=0, keepdims=True)
    return m, jnp.min(jnp.where(x == m, idx, sentinel), axis=0, keepdims=True)


def _route_body(s_ref, b_ref, e_ref, w_ref, r_ref, cnt_ref, u_scr, run_scr):
    i = pl.program_id(0)
    tt = s_ref.shape[1]

    @pl.when(i == 0)
    def _():
        rr = lax.broadcasted_iota(I32, (tt, tt), 0)
        cc = lax.broadcasted_iota(I32, (tt, tt), 1)
        u_scr[...] = jnp.where(rr < cc, 1.0, 0.0).astype(BF16)
        run_scr[...] = jnp.zeros_like(run_scr)

    s = s_ref[...]
    sel = s + b_ref[...][:, 0:1]
    sub8 = lax.broadcasted_iota(I32, (E_PER_GROUP, tt), 0).astype(F32)
    gs = jnp.zeros((N_GROUPS, tt), F32)
    for g in range(N_GROUPS):
        grp = sel[g * E_PER_GROUP:(g + 1) * E_PER_GROUP, :]
        m1, i1 = _first_max(grp, sub8, float(E_PER_GROUP))
        m2 = jnp.max(jnp.where(sub8 == i1, -jnp.inf, grp), axis=0, keepdims=True)
        gs = jnp.where(sub8 == g, m1 + m2, gs)
    gmask = jnp.zeros((N_GROUPS, tt), F32)
    for _ in range(TOPK_GROUPS):
        _, ig = _first_max(gs, sub8, float(N_GROUPS))
        hit = sub8 == ig
        gmask = jnp.where(hit, 1.0, gmask)
        gs = jnp.where(hit, -jnp.inf, gs)
    masked = jnp.concatenate(
        [jnp.where(jnp.broadcast_to(gmask[g:g + 1, :], (E_PER_GROUP, tt)) > 0.5,
                   sel[g * E_PER_GROUP:(g + 1) * E_PER_GROUP, :], -jnp.inf) for g in range(N_GROUPS)], axis=0)
    sub64 = lax.broadcasted_iota(I32, (N_EXPERTS, tt), 0).astype(F32)
    oh = jnp.zeros((N_EXPERTS, tt), F32)
    eks, wks = [], []
    for _ in range(TOP_K):
        _, ie = _first_max(masked, sub64, float(N_EXPERTS))
        hit = sub64 == ie
        wks.append(jnp.sum(jnp.where(hit, s, 0.0), axis=0, keepdims=True))
        eks.append(ie)
        masked = jnp.where(hit, -jnp.inf, masked)
        oh = jnp.where(hit, 1.0, oh)
    wsum = wks[0]
    for k in range(1, TOP_K):
        wsum = wsum + wks[k]
    run = run_scr[...]
    rank_t = _dg(oh.astype(BF16), u_scr[...]) + jnp.tile(run, (1, tt // 128))
    for k in range(TOP_K):
        e_ref[k:k + 1, :] = eks[k].astype(I32)
        w_ref[k:k + 1, :] = wks[k] / wsum * ROUTE_SCALE
        r_ref[k:k + 1, :] = jnp.sum(jnp.where(sub64 == eks[k], rank_t, 0.0), axis=0, keepdims=True).astype(I32)
    run_new = run + jnp.sum(oh, axis=1, keepdims=True)
    run_scr[...] = run_new
    cnt_ref[...] = run_new.astype(I32)


def _route_call(s_t, b_col, tt):
    m = s_t.shape[1]
    out = pl.BlockSpec((TOP_K, tt), lambda i: (0, i))
    return pl.pallas_call(
        _route_body,
        grid=(m // tt,),
        in_specs=[pl.BlockSpec((N_EXPERTS, tt), lambda i: (0, i)),
                  pl.BlockSpec((N_EXPERTS, 128), lambda i: (0, 0))],
        out_specs=[out, out, out, pl.BlockSpec((N_EXPERTS, 128), lambda i: (0, 0))],
        out_shape=[jax.ShapeDtypeStruct((TOP_K, m), I32), jax.ShapeDtypeStruct((TOP_K, m), F32),
                   jax.ShapeDtypeStruct((TOP_K, m), I32), jax.ShapeDtypeStruct((N_EXPERTS, 128), I32)],
        scratch_shapes=[pltpu.VMEM((tt, tt), BF16), pltpu.VMEM((N_EXPERTS, 128), F32)],
        compiler_params=_cparams(("arbitrary",), 32),
        name="route",
    )(s_t, b_col)


def _posk_body(pst_ref, e_ref, r_ref, p_ref):
    e = e_ref[...]
    acc = r_ref[...]
    for x in range(N_EXPERTS):
        acc = acc + jnp.where(e == x, pst_ref[x], 0)
    p_ref[...] = acc


def _posk_call(pstart, eidx, rank):
    k, m = eidx.shape
    tt = min(m, 2048)
    blk = pl.BlockSpec((k, tt), lambda i, pst: (0, i))
    return pl.pallas_call(
        _posk_body,
        grid_spec=pltpu.PrefetchScalarGridSpec(num_scalar_prefetch=1, grid=(m // tt,),
                                               in_specs=[blk, blk], out_specs=blk),
        out_shape=jax.ShapeDtypeStruct((k, m), I32),
        compiler_params=_cparams(("arbitrary",), 32),
        name="posk",
    )(pstart, eidx, rank)


def _slab(ref, r):
    return ref.at[pl.ds(pl.multiple_of(r * SLAB, SLAB), SLAB), :]


def _ffn_packed(x_ref, rows, w1, w3, w2, base=0):
    half = w1.shape[0] // 2
    xa, xb = _unpack_pair(_load_slabs(x_ref, rows, base))
    xa = xa.astype(BF16)
    xb = xb.astype(BF16)
    h1 = _dg(xa, w1[0:half, :]) + _dg(xb, w1[half:, :])
    h3 = _dg(xa, w3[0:half, :]) + _dg(xb, w3[half:, :])
    return _dg((_silu(h1) * h3).astype(BF16), w2[...])


def _dispatch_body(cnt_ref, pst_ref, pcn_ref, h2_ref, pos_ref, w1_ref, w3_ref, w2_ref, h2_hbm,
                   xs_ref, sh_ref, zrow, sem):
    i = pl.program_id(0)
    td = h2_ref.shape[0] // SLAB

    def row_copy(t, dst):
        return pltpu.make_async_copy(_slab(h2_hbm, i * td + t), _slab(xs_ref, dst), sem)

    def issue(t, carry):
        for k in range(TOP_K):
            row_copy(t, pos_ref[k, t]).start(priority=k % 2)
        return carry

    lax.fori_loop(0, td, issue, 0)

    sh_ref[...] = _ffn_packed(h2_ref, td, w1_ref, w3_ref, w2_ref).astype(BF16)

    def drain(t, carry):
        for k in range(TOP_K):
            row_copy(0, 0).wait()
        return carry

    lax.fori_loop(0, td, drain, 0)

    @pl.when(i == pl.num_programs(0) - 1)
    def _():
        zrow[...] = jnp.zeros_like(zrow)

        def zero_copy(dst):
            return pltpu.make_async_copy(zrow, _slab(xs_ref, dst), sem)

        def per_expert(e, carry):
            base = pst_ref[e]
            lax.fori_loop(cnt_ref[e], pcn_ref[e], lambda rr, cc: (zero_copy(base + rr).start(), cc)[1], 0)
            lax.fori_loop(cnt_ref[e], pcn_ref[e], lambda rr, cc: (zero_copy(0).wait(), cc)[1], 0)
            return carry

        lax.fori_loop(0, N_EXPERTS, per_expert, 0)


def _dispatch_call(cnt, pstart, pcnt, h2s, pos, w1s, w3s, w2s, rows, td):
    m = h2s.shape[0] // SLAB
    d, ds = w1s.shape
    return pl.pallas_call(
        _dispatch_body,
        grid_spec=pltpu.PrefetchScalarGridSpec(
            num_scalar_prefetch=3, grid=(m // td,),
            in_specs=[pl.BlockSpec((td * SLAB, SLAB_W), lambda i, *_: (i, 0)),
                      pl.BlockSpec((TOP_K, td), lambda i, *_: (0, i), memory_space=pltpu.SMEM),
                      pl.BlockSpec((d, ds), lambda i, *_: (0, 0)), pl.BlockSpec((d, ds), lambda i, *_: (0, 0)),
                      pl.BlockSpec((ds, d), lambda i, *_: (0, 0)), pl.BlockSpec(memory_space=pl.ANY)],
            out_specs=[pl.BlockSpec(memory_space=pl.ANY), pl.BlockSpec((td, d), lambda i, *_: (i, 0))],
            scratch_shapes=[pltpu.VMEM((SLAB, SLAB_W), U32), pltpu.SemaphoreType.DMA(())]),
        out_shape=[jax.ShapeDtypeStruct((rows * SLAB, SLAB_W), U32), jax.ShapeDtypeStruct((m, d), BF16)],
        compiler_params=_cparams(("arbitrary",), 40),
        name="dispatch",
    )(cnt, pstart, pcnt, h2s, pos, w1s, w3s, w2s, h2s)


def _moe_body(te_ref, nu_ref, nxt_ref, par_ref, x_ref, w1_ref, w3_ref, w2_ref, y_ref,
              f1, f3, f2, w1b, w3b, w2b, sems):
    def fetch(ex, s):
        return (pltpu.make_async_copy(w1_ref.at[ex], f1.at[s], sems.at[s]),
                pltpu.make_async_copy(w3_ref.at[ex], f3.at[s], sems.at[s]),
                pltpu.make_async_copy(w2_ref.at[ex], f2.at[s], sems.at[s]))

    for sub in range(MOE_TILES_PER_STEP):
        ti = pl.program_id(0) * MOE_TILES_PER_STEP + sub
        used = ti < nu_ref[0]
        e = te_ref[ti]
        first = jnp.logical_or(ti == 0, e != te_ref[jnp.maximum(ti - 1, 0)])
        slot = par_ref[e]

        @pl.when(jnp.logical_and(used, ti == 0))
        def _():
            for c in fetch(e, slot):
                c.start()

        @pl.when(jnp.logical_and(used, first))
        def _():
            for c in fetch(e, slot):
                c.wait()
            nx = nxt_ref[e]

            @pl.when(nx < N_EXPERTS)
            def _():
                for c in fetch(nx, 1 - slot):
                    c.start()

            w1b[...] = f1[slot].astype(BF16)
            w3b[...] = f3[slot].astype(BF16)
            w2b[...] = f2[slot].astype(BF16)

        @pl.when(used)
        def _():
            half = w1b.shape[0] // 2
            base = sub * MOE_ROWS * SLAB
            y = _ffn_packed(x_ref, MOE_ROWS, w1b, w3b, w2b, base)
            _store_slabs(y_ref, _pack_pair(y[:, :half], y[:, half:]), base)


def _moe_call(tile_e, n_used, nxt, par, xs, w1, w3, w2):
    rows = xs.shape[0] // SLAB
    step_rows = MOE_ROWS * MOE_TILES_PER_STEP
    nt = rows // step_rows
    d, de = w1.shape[-2:]
    rmap = lambda i, te, nu, *_: (jnp.minimum(i, (nu[0] - 1) // MOE_TILES_PER_STEP), 0)
    hbm = pl.BlockSpec(memory_space=pl.ANY)
    return pl.pallas_call(
        _moe_body,
        grid_spec=pltpu.PrefetchScalarGridSpec(
            num_scalar_prefetch=4, grid=(nt,),
            in_specs=[pl.BlockSpec((step_rows * SLAB, SLAB_W), rmap), hbm, hbm, hbm],
            out_specs=pl.BlockSpec((step_rows * SLAB, SLAB_W), rmap),
            scratch_shapes=[pltpu.VMEM((2, d, de), F32), pltpu.VMEM((2, d, de), F32), pltpu.VMEM((2, de, d), F32),
                            pltpu.VMEM((d, de), BF16), pltpu.VMEM((d, de), BF16), pltpu.VMEM((de, d), BF16),
                            pltpu.SemaphoreType.DMA((2,))]),
        out_shape=jax.ShapeDtypeStruct((rows * SLAB, SLAB_W), U32),
        compiler_params=_cparams(("arbitrary",), 56),
        name="moe",
    )(tile_e, n_used, nxt, par, xs, w1, w3, w2)


def _final_body(x1_ref, sh_ref, pos_ref, posn_ref, wt_ref, ys_ref, g2_ref, gn_ref, o_ref, ybuf, sems):
    i = pl.program_id(0)
    n = pl.num_programs(0)
    tf, d = x1_ref.shape
    half = d // 2
    slot = i % 2
    slot_slabs = TOP_K * tf

    def row_copy(p_ref, s, k, t):
        return pltpu.make_async_copy(_slab(ys_ref, p_ref[k, t]), _slab(ybuf, s * slot_slabs + k * tf + t),
                                     sems.at[s])

    def issue(p_ref, s):
        def body(t, carry):
            for k in range(TOP_K):
                row_copy(p_ref, s, k, t).start(priority=k % 2)
            return carry

        lax.fori_loop(0, tf, body, 0)

    @pl.when(i == 0)
    def _():
        issue(pos_ref, 0)

    @pl.when(i + 1 < n)
    def _():
        issue(posn_ref, 1 - slot)

    def drain(t, carry):
        for k in range(TOP_K):
            pltpu.make_async_copy(_slab(ys_ref, 0), _slab(ybuf, 0), sems.at[slot]).wait()
        return carry

    lax.fori_loop(0, tf, drain, 0)

    wt = jnp.concatenate([jnp.transpose(jnp.concatenate(
        [wt_ref[:, c * 128:(c + 1) * 128], jnp.zeros((128 - TOP_K, 128), F32)], axis=0)) for c in range(tf // 128)],
        axis=0)
    shared = sh_ref[...].astype(F32)
    acc_a = shared[:, :half]
    acc_b = shared[:, half:]
    base = slot * slot_slabs * SLAB
    for k in range(TOP_K):
        ya, yb = _unpack_pair(_load_slabs(ybuf, tf, base=base + k * tf * SLAB))
        acc_a = acc_a + wt[:, k:k + 1] * ya
        acc_b = acc_b + wt[:, k:k + 1] * yb
    xo = x1_ref[...] + g2_ref[...] * jnp.concatenate([acc_a, acc_b], axis=1)
    ms = jnp.mean(xo * xo, axis=-1, keepdims=True)
    o_ref[...] = xo * lax.rsqrt(ms + EPS) * gn_ref[...]


def _final_call(x1, sh, pos, wts, ys, g2, gn, tf):
    m, d = x1.shape
    nt = m // tf
    full = pl.BlockSpec((tf, d), lambda i: (i, 0))
    vec = pl.BlockSpec((1, d), lambda i: (0, 0))
    return pl.pallas_call(
        _final_body,
        grid=(nt,),
        in_specs=[full, full,
                  pl.BlockSpec((TOP_K, tf), lambda i: (0, i), memory_space=pltpu.SMEM),
                  pl.BlockSpec((TOP_K, tf), lambda i: (0, jnp.minimum(i + 1, nt - 1)), memory_space=pltpu.SMEM),
                  pl.BlockSpec((TOP_K, tf), lambda i: (0, i)),
                  pl.BlockSpec(memory_space=pl.ANY), vec, vec],
        out_specs=full,
        out_shape=jax.ShapeDtypeStruct((m, d), F32),
        scratch_shapes=[pltpu.VMEM((2 * TOP_K * tf * SLAB, SLAB_W), U32), pltpu.SemaphoreType.DMA((2,))],
        compiler_params=_cparams(("arbitrary",), 48),
        name="final",
    )(x1, sh, pos, pos, wts, ys, g2, gn)


def _pos_tables(n_tokens):
    rows = n_tokens // GRID_W
    quarter = D_MODEL // 4
    omega = 1.0 / (10000.0 ** (jnp.arange(quarter, dtype=F32) / quarter))

    def emb1d(pos):
        ang = pos[:, None] * omega[None]
        return jnp.concatenate([jnp.sin(ang), jnp.cos(ang)], axis=-1)

    return emb1d(jnp.arange(rows, dtype=F32)), emb1d(jnp.arange(GRID_W, dtype=F32))


def _filter_feats(L):
    n1h = L // FFT_N2
    i_, h_, b_, a_ = jnp.meshgrid(jnp.arange(n1h // 8, dtype=I32), jnp.arange(2, dtype=I32),
                                  jnp.arange(FFT_N2, dtype=I32), jnp.arange(8, dtype=I32), indexing="ij")
    n = ((8 * i_ + a_ + h_ * n1h) * FFT_N2 + b_).reshape(-1)
    t = jnp.where(n <= L, n, 2 * L - n).astype(F32)
    t01 = t / max(L - 1, 1)
    w = 2.0 * math.pi * t / L
    bands = jnp.linspace(1e-4, FILT_BANDS - 1, FILT_BANDS, dtype=F32)
    feats = jnp.concatenate([t01[None, :], jnp.cos(bands[:, None] * w[None, :]), -jnp.sin(bands[:, None] * w[None, :]),
                             jnp.zeros((64 - 33, 2 * L), F32)], axis=0)
    return feats


def _pad_rows(a, rows):
    return jnp.concatenate([a, jnp.zeros((rows - a.shape[0],) + a.shape[1:], a.dtype)], axis=0)


def _layer(x, c, ctx, c_ctx, w_ada, b_ada, g_mix, g_ffn, w_in, b_gates, conv_k_w, conv_k_b,
           conv_q_w, conv_q_b, g_head, conv_hy_w, conv_hy_b, filt_w1, filt_b1, filt_freq1,
           filt_w2, filt_b2, filt_freq2, filt_w3, filt_b3, hy_dskip, w_out, w_router, b_router,
           w1_e, w3_e, w2_e, w1_s, w3_s, w2_s, g_final):
    L, d = x.shape
    lc = ctx.shape[0]
    row = lambda v: v.reshape(1, -1)

    cc = _pad_rows(jnp.stack([c, c_ctx], axis=0), 8)
    mods = _mod_call(cc, w_ada, row(b_ada))
    sh1, sc1, g1, sh2, sc2, g2 = [mods[0:1, k * d:(k + 1) * d] for k in range(6)]
    csh1, csc1 = mods[1:2, 0:d], mods[1:2, d:2 * d]

    w_r, w_g = _wprep_call(jnp.transpose(w_in[0]))
    bg = jnp.concatenate([b_gates, jnp.zeros((GATE_PAD - 4 * N_HEADS,), F32)]).reshape(1, GATE_PAD)
    e_row, e_col = _pos_tables(L)
    conv_w = jnp.concatenate([conv_k_w, conv_q_w], axis=1)
    conv_b = jnp.concatenate([conv_k_b, conv_q_b]).reshape(1, -1)
    conv_s = jnp.concatenate([jnp.ones((QK_W,), F32), jnp.full((QK_W,), QK_HEAD ** -0.5, F32)]).reshape(1, -1)

    z_c, gt_c = _inproj_call(ctx, jnp.zeros((8, d // 2), F32), e_col, row(g_mix), csh1, csc1, w_r, w_g,
                             use_pos=False, tm=min(lc, 256))
    kq_c = _conv_kq_call(z_c, conv_w, conv_b, conv_s, tm=min(lc, 256))
    s0 = jnp.zeros((2 * N_HEADS, CHUNK, 256), F32)
    m0 = jnp.zeros((2, 8, 128), F32)
    _, _, s_ctx, m_ctx = _mlstm_call(kq_c, z_c, gt_c, bg, s0, m0)

    z, gates = _inproj_call(x, e_row, e_col, row(g_mix), sh1, sc1, w_r, w_g, use_pos=True, tm=min(L, 1024))
    kq = _conv_kq_call(z, conv_w, conv_b, conv_s, tm=min(L, 512))
    x0c, xv = _conv_hy_call(z, conv_hy_w, row(conv_hy_b), tm=min(L, 512))
    hf, hb, _, _ = _mlstm_call(kq, z, gates, bg, s_ctx, m_ctx)

    n1 = 2 * L // FFT_N2
    cst = _fft_consts(n1)
    rates = jnp.linspace(-math.log(DECAY_TARGET) / SLOW_DECAY_PCT, -math.log(DECAY_TARGET) / FAST_DECAY_PCT,
                         HY_W, dtype=F32).reshape(1, -1)
    w1t = jnp.transpose(_pad_rows(filt_w1, 64))
    colrep = lambda v: jnp.broadcast_to(v.reshape(-1, 1), (v.shape[0], 128))
    kf, l1 = _filt_call(_filter_feats(L), w1t, colrep(filt_b1), colrep(filt_freq1), jnp.transpose(filt_w2),
                        colrep(filt_b2), colrep(filt_freq2), filt_w3, row(filt_b3), rates)
    cols = FFT_N2 * HY_W
    cb = 2048
    kar, kai = _fft1_call(cst["f1"], kf, cb, packed=True)
    kv = n1 // 2 + 1
    khat = _fft2_filt_call(kar.reshape(FFT_KP, FFT_N2, HY_W), kai.reshape(FFT_KP, FFT_N2, HY_W), cst, kv)
    uar, uai = _fft1_call(cst["f1"], xv.reshape(n1 // 2, cols), cb)
    br, bi = _fft2_conv_call(uar.reshape(FFT_KP, FFT_N2, HY_W), uai.reshape(FFT_KP, FFT_N2, HY_W), khat, cst, kv)
    reps = cb // HY_W
    il_t = jnp.tile(1.0 / l1, (1, reps))
    ds_t = jnp.tile(row(hy_dskip), (1, reps))
    yh = _ifft1_call(cst, br.reshape(FFT_KP, cols), bi.reshape(FFT_KP, cols),
                     x0c.reshape(n1 // 2, cols), xv.reshape(n1 // 2, cols), il_t, ds_t, cb).reshape(L, HY_W)

    wo = w_out.astype(BF16)
    x1, h2s, s_t = _outproj_call(hf, hb, z, yh, x, e_row, e_col, row(g_head), wo[:MV_W], wo[MV_W:],
                                g1, row(g_ffn), sh2, sc2, jnp.transpose(w_router), tm=min(L, 256))

    b_col = jnp.broadcast_to(b_router.reshape(N_EXPERTS, 1), (N_EXPERTS, 128))
    eidx, wts, rank, cnt2 = _route_call(s_t, b_col, tt=min(L, 1024))
    cnt = cnt2[:, 0]
    pcnt = (cnt + MOE_ROWS - 1) // MOE_ROWS * MOE_ROWS
    pend = jnp.cumsum(pcnt)
    pstart = pend - pcnt
    rows = L * TOP_K + N_EXPERTS * MOE_ROWS
    nt = rows // MOE_ROWS
    tile_row = jnp.arange(nt, dtype=I32) * MOE_ROWS
    tile_e = jnp.minimum(jnp.sum((pend[None, :] <= tile_row[:, None]).astype(I32), axis=1), N_EXPERTS - 1)
    n_used = (pend[-1] // MOE_ROWS).astype(I32).reshape(1)
    pos = _posk_call(pstart.astype(I32), eidx, rank)

    xs, sh = _dispatch_call(cnt, pstart.astype(I32), pcnt.astype(I32), h2s, pos, w1_s.astype(BF16),
                            w3_s.astype(BF16), w2_s.astype(BF16), rows, td=min(L, 256))
    ex = jnp.arange(N_EXPERTS, dtype=I32)
    nonempty = pcnt > 0
    nxt = jnp.min(jnp.where((ex[None, :] > ex[:, None]) & nonempty[None, :], ex[None, :], N_EXPERTS), axis=1)
    par = (jnp.cumsum(nonempty.astype(I32)) + 1) % 2
    ys = _moe_call(tile_e, n_used, nxt.astype(I32), par.astype(I32), xs, w1_e, w3_e, w2_e)
    return _final_call(x1, sh, pos, wts, ys, g2, row(g_final), tf=min(L, 256))


def kernel(x, c, ctx, c_ctx, w_ada, b_ada, g_mix, g_ffn, w_in, b_gates, conv_k_w, conv_k_b, conv_q_w,
           conv_q_b, g_head, conv_hy_w, conv_hy_b, filt_w1, filt_b1, filt_freq1, filt_w2, filt_b2,
           filt_freq2, filt_w3, filt_b3, hy_dskip, w_out, w_router, b_router, w1_e, w3_e, w2_e,
           w1_s, w3_s, w2_s, g_final):
    assert x.shape[0] == 1 and w_ada.shape[0] == 1, "one batch element, one layer"
    out = _layer(x[0], c[0], ctx[0], c_ctx, w_ada[0], b_ada[0], g_mix[0], g_ffn[0], w_in, b_gates[0],
                 conv_k_w[0], conv_k_b[0], conv_q_w[0], conv_q_b[0], g_head[0], conv_hy_w[0], conv_hy_b[0],
                 filt_w1[0], filt_b1[0], filt_freq1[0], filt_w2[0], filt_b2[0], filt_freq2[0], filt_w3[0],
                 filt_b3[0], hy_dskip[0], w_out[0], w_router[0], b_router[0], w1_e[0], w3_e[0], w2_e[0],
                 w1_s[0], w3_s[0], w2_s[0], g_final)
    return out[None]
```

```python
import functools
import math

import numpy as np
import jax
import jax.numpy as jnp
from jax import lax
from jax.experimental import pallas as pl
from jax.experimental.pallas import tpu as pltpu

F32 = jnp.float32
BF16 = jnp.bfloat16
I32 = jnp.int32
U32 = jnp.uint32

D_MODEL = 2048
GRID_W = 64
N_HEADS = 8
QK_HEAD = 64
V_HEAD = 128
QK_W = N_HEADS * QK_HEAD
MV_W = N_HEADS * V_HEAD
HY_W = D_MODEL - MV_W
CHUNK = 128
FILT_BANDS = 16
FILT_HIDDEN = 64
DECAY_TARGET = 1e-2
FAST_DECAY_PCT = 0.3
SLOW_DECAY_PCT = 1.5
N_EXPERTS = 64
N_GROUPS = 8
E_PER_GROUP = 8
TOPK_GROUPS = 4
TOP_K = 8
D_EXPERT = 512
ROUTE_SCALE = 2.5
EPS = 1e-6
OFF_K = 0
OFF_V = OFF_K + QK_W
OFF_G = OFF_V + MV_W
OFF_Q = OFF_G + 4 * N_HEADS
OFF_O = OFF_Q + QK_W
OFF_HY = OFF_O + MV_W

ZC_KQ, ZC_V, ZC_O, ZC_X0, ZC_X1, ZC_HV = 0, 1, 2, 3, 4, 5
Z_COLS = 6 * 1024
GATE_PAD = 128

NEG = -1e30
MIB = 1024 * 1024

FFT_N2 = 128
FFT_KP = 144

MOE_ROWS = 256
MOE_TILES_PER_STEP = 4


def _cparams(sem, vmem_mb, flags=None):
    return pltpu.CompilerParams(dimension_semantics=sem, vmem_limit_bytes=vmem_mb * MIB, flags=flags)


def _split2(x):
    hi = x.astype(BF16)
    lo = (x - hi.astype(F32)).astype(BF16)
    return hi, lo


_NN = (((1,), (0,)), ((), ()))
_NT = (((1,), (1,)), ((), ()))
_TN = (((0,), (0,)), ((), ()))


def _dg(a, b, dims=_NN):
    return lax.dot_general(a, b, dims, preferred_element_type=F32)


def _dot3(a, b, dims=_NN):
    ah, al = _split2(a)
    bh, bl = _split2(b)
    return _dg(ah, bh, dims) + _dg(al, bh, dims) + _dg(ah, bl, dims)


def _sigmoid(x):
    return 1.0 / (1.0 + jnp.exp(-x))


def _silu(x):
    return x * _sigmoid(x)


def _pack_pair(a, b):
    hi = lax.bitcast_convert_type(a.astype(BF16).astype(F32), U32)
    lo = lax.bitcast_convert_type(b.astype(BF16).astype(F32), U32)
    return hi | (lo >> 16)


def _unpack_pair(w):
    a = lax.bitcast_convert_type(w & jnp.uint32(0xFFFF0000), F32)
    b = lax.bitcast_convert_type(w << 16, F32)
    return a, b


SLAB = 8
SLAB_W = 128


def _store_slabs(ref, w, base=0):
    r = w.shape[0]
    for j in range(SLAB):
        ref[pl.ds(base + j, r, stride=SLAB), :] = w[:, j * SLAB_W:(j + 1) * SLAB_W]


def _load_slabs(ref, r, base=0):
    return jnp.concatenate([ref[pl.ds(base + j, r, stride=SLAB), :] for j in range(SLAB)], axis=1)


def _norm_mod(x, g, sh, sc):
    ms = jnp.mean(x * x, axis=-1, keepdims=True)
    return (x * lax.rsqrt(ms + EPS) * g) * (1.0 + sc) + sh


def _add_pos(x, erow, ecol):
    tm, d = x.shape
    half = d // 2
    parts = []
    for r in range(tm // GRID_W):
        xs = x[r * GRID_W:(r + 1) * GRID_W, :]
        parts.append(jnp.concatenate([xs[:, :half] + erow[r:r + 1, :], xs[:, half:] + ecol], axis=-1))
    return parts[0] if len(parts) == 1 else jnp.concatenate(parts, axis=0)


def _mod_body(cc_ref, w_ref, b_ref, o_ref):
    o_ref[...] = _dot3(_silu(cc_ref[...]), w_ref[...]) + b_ref[...]


def _mod_call(cc, w, b):
    d, n = w.shape
    tn = 1024
    return pl.pallas_call(
        _mod_body,
        grid=(n // tn,),
        in_specs=[pl.BlockSpec((8, d), lambda j: (0, 0)),
                  pl.BlockSpec((d, tn), lambda j: (0, j)),
                  pl.BlockSpec((1, tn), lambda j: (0, j))],
        out_specs=pl.BlockSpec((8, tn), lambda j: (0, j)),
        out_shape=jax.ShapeDtypeStruct((8, n), F32),
        compiler_params=_cparams(("arbitrary",), 40),
        name="mod",
    )(cc, w, b)


def _wprep_body(w_ref, wr_ref, wg_ref):
    w = w_ref[...]
    wt = jnp.concatenate([w[OFF_K:OFF_V], w[OFF_Q:OFF_O], w[OFF_V:OFF_G], w[OFF_O:]], axis=0)
    wr_ref[...] = jnp.transpose(wt).astype(BF16)
    g = jnp.concatenate([w[OFF_G:OFF_Q], jnp.zeros((GATE_PAD - 4 * N_HEADS, w.shape[1]), F32)], axis=0)
    wg_ref[...] = jnp.transpose(g)


def _wprep_call(w_t):
    n, d = w_t.shape
    tr = 256
    return pl.pallas_call(
        _wprep_body,
        grid=(d // tr,),
        in_specs=[pl.BlockSpec((n, tr), lambda i: (0, i))],
        out_specs=[pl.BlockSpec((tr, Z_COLS), lambda i: (i, 0)), pl.BlockSpec((tr, GATE_PAD), lambda i: (i, 0))],
        out_shape=[jax.ShapeDtypeStruct((d, Z_COLS), BF16), jax.ShapeDtypeStruct((d, GATE_PAD), F32)],
        compiler_params=_cparams(("arbitrary",), 32),
        name="wprep",
    )(w_t)


def _inproj_body(use_pos, x_ref, erow_ref, ecol_ref, gm_ref, sh_ref, sc_ref, w_ref, wg_ref,
                 z_ref, g_ref, h_scr):
    @pl.when(pl.program_id(1) == 0)
    def _():
        x = x_ref[...]
        if use_pos:
            x = _add_pos(x, erow_ref[...], ecol_ref[...])
        h = _norm_mod(x, gm_ref[...], sh_ref[...], sc_ref[...])
        h_scr[...] = h.astype(BF16)
        g_ref[...] = _dot3(h, wg_ref[...])

    z_ref[...] = jnp.dot(h_scr[...], w_ref[...], preferred_element_type=F32).astype(BF16)


def _inproj_call(x, erow, ecol, gm, sh, sc, w, wg, use_pos, tm):
    m, d = x.shape
    tn = 1024
    er = tm // GRID_W if use_pos else erow.shape[0]
    row_map = (lambda i, j: (i, 0)) if use_pos else (lambda i, j: (0, 0))
    return pl.pallas_call(
        functools.partial(_inproj_body, use_pos),
        grid=(m // tm, Z_COLS // tn),
        in_specs=[pl.BlockSpec((tm, d), lambda i, j: (i, 0)),
                  pl.BlockSpec((er, d // 2), row_map),
                  pl.BlockSpec((GRID_W, d // 2), lambda i, j: (0, 0)),
                  pl.BlockSpec((1, d), lambda i, j: (0, 0)),
                  pl.BlockSpec((1, d), lambda i, j: (0, 0)),
                  pl.BlockSpec((1, d), lambda i, j: (0, 0)),
                  pl.BlockSpec((d, tn), lambda i, j: (0, j)),
                  pl.BlockSpec((d, GATE_PAD), lambda i, j: (0, 0))],
        out_specs=[pl.BlockSpec((tm, tn), lambda i, j: (i, j)),
                   pl.BlockSpec((tm, GATE_PAD), lambda i, j: (i, 0))],
        out_shape=[jax.ShapeDtypeStruct((m, Z_COLS), BF16),
                   jax.ShapeDtypeStruct((m, GATE_PAD), F32)],
        scratch_shapes=[pltpu.VMEM((tm, d), BF16)],
        compiler_params=_cparams(("arbitrary", "arbitrary"), 48),
        name="inproj",
    )(x, erow, ecol, gm, sh, sc, w, wg)


def _conv3(zc, zp, zn, w, b, first, last):
    tm = zc.shape[0]
    row = lax.broadcasted_iota(I32, zc.shape, 0)
    prev_row = jnp.where(first, 0.0, zp[7:8, :])
    next_row = jnp.where(last, 0.0, zn[0:1, :])
    xm = jnp.where(row == 0, prev_row, pltpu.roll(zc, 1, 0))
    xp = jnp.where(row == tm - 1, next_row, pltpu.roll(zc, tm - 1, 0))
    return xm * w[0:1, :] + zc * w[1:2, :] + xp * w[2:3, :] + b


def _conv_kq_body(zc_ref, zp_ref, zn_ref, w_ref, b_ref, s_ref, o_ref):
    i = pl.program_id(0)
    u = _conv3(zc_ref[...].astype(F32), zp_ref[...].astype(F32), zn_ref[...].astype(F32),
               w_ref[...], b_ref[...], i == 0, i == pl.num_programs(0) - 1)
    o_ref[...] = (_silu(u) * s_ref[...]).astype(BF16)


def _halo_specs(tm, m, cb):
    nb8 = m // 8
    return [pl.BlockSpec((tm, 1024), lambda i: (i, cb)),
            pl.BlockSpec((8, 1024), lambda i: (jnp.maximum(i * (tm // 8) - 1, 0), cb)),
            pl.BlockSpec((8, 1024), lambda i: (jnp.minimum((i + 1) * (tm // 8), nb8 - 1), cb))]


def _conv_kq_call(z, w, b, s, tm):
    m = z.shape[0]
    vec = pl.BlockSpec((1, 1024), lambda i: (0, 0))
    return pl.pallas_call(
        _conv_kq_body,
        grid=(m // tm,),
        in_specs=_halo_specs(tm, m, ZC_KQ) + [pl.BlockSpec((3, 1024), lambda i: (0, 0)), vec, vec],
        out_specs=pl.BlockSpec((tm, 1024), lambda i: (i, 0)),
        out_shape=jax.ShapeDtypeStruct((m, 1024), BF16),
        compiler_params=_cparams(("arbitrary",), 32),
        name="conv_kq",
    )(z, z, z, w, b, s)


def _conv_hy_body(ac_ref, ap_ref, an_ref, bc_ref, bp_ref, bn_ref, cc_ref, cp_ref, cn_ref,
                  w_ref, b_ref, x0_ref, xv_ref):
    i = pl.program_id(0)
    first, last = i == 0, i == pl.num_programs(0) - 1
    w = w_ref[...]
    b = b_ref[...]

    def cv(c, p, n, k):
        return _conv3(c[...].astype(F32), p[...].astype(F32), n[...].astype(F32),
                      w[:, k * 1024:(k + 1) * 1024], b[:, k * 1024:(k + 1) * 1024], first, last)

    x0_ref[...] = cv(ac_ref, ap_ref, an_ref, 0).astype(BF16)
    xv_ref[...] = (cv(bc_ref, bp_ref, bn_ref, 1) * cv(cc_ref, cp_ref, cn_ref, 2)).astype(BF16)


def _conv_hy_call(z, w, b, tm):
    m = z.shape[0]
    out = pl.BlockSpec((tm, 1024), lambda i: (i, 0))
    return pl.pallas_call(
        _conv_hy_body,
        grid=(m // tm,),
        in_specs=(_halo_specs(tm, m, ZC_X0) + _halo_specs(tm, m, ZC_X1) + _halo_specs(tm, m, ZC_HV)
                  + [pl.BlockSpec((3, 3072), lambda i: (0, 0)), pl.BlockSpec((1, 3072), lambda i: (0, 0))]),
        out_specs=[out, out],
        out_shape=[jax.ShapeDtypeStruct((m, 1024), BF16), jax.ShapeDtypeStruct((m, 1024), BF16)],
        compiler_params=_cparams(("arbitrary",), 32),
        name="conv_hy",
    )(z, z, z, z, z, z, z, z, z, w, b)


def _mlstm_body(kqf_ref, vf_ref, gf_ref, kqb_ref, vb_ref, gb_ref, bg_ref, s0_ref, m0_ref,
                hf_ref, hb_ref, sfin_ref, mfin_ref, s_scr, m_scr):
    j = pl.program_id(0)

    @pl.when(j == 0)
    def _():
        s_scr[...] = s0_ref[...]
        m_scr[...] = m0_ref[...]

    r = lax.broadcasted_iota(I32, (CHUNK, CHUNK), 0)
    c = lax.broadcasted_iota(I32, (CHUNK, CHUNK), 1)
    ones_b = jnp.ones((CHUNK, CHUNK), BF16)
    bg = bg_ref[...]

    def lane_bcast(x, h, width=CHUNK):
        return jnp.broadcast_to(x[:, h:h + 1], (x.shape[0], width))

    for d in range(2):
        kq = (kqf_ref, kqb_ref)[d][...]
        v = (vf_ref, vb_ref)[d][...]
        g_all = (gf_ref, gb_ref)[d][...] + bg
        out_ref = (hf_ref, hb_ref)[d]
        tri = (r >= c) if d == 0 else (c >= r)
        tri_b = jnp.where(tri, 1.0, 0.0).astype(BF16)
        gi = g_all if d == 0 else pltpu.roll(g_all, CHUNK - 16, 1)
        gfp = pltpu.roll(g_all, CHUNK - 8 - 16 * d, 1)
        lf = jnp.minimum(gfp, 0.0) - jnp.log(1.0 + jnp.exp(-jnp.abs(gfp)))
        l1 = lf.astype(BF16)
        r1 = lf - l1.astype(F32)
        l2 = r1.astype(BF16)
        l3 = (r1 - l2.astype(F32)).astype(BF16)
        bcum = _dg(tri_b, l1) + _dg(tri_b, l2) + _dg(tri_b, l3)
        gtot = bcum[CHUNK - 1:CHUNK, :] if d == 0 else bcum[0:1, :]
        acol = gtot - bcum + gi
        m_loc = jnp.max(acol, axis=0, keepdims=True)
        m_st = m_scr[d, 0:1, :]
        m_new = jnp.maximum(gtot + m_st, m_loc)
        sp8 = jnp.broadcast_to(jnp.exp(gtot + m_st - m_new), (8, CHUNK))
        wst = jnp.exp(acol - m_new)
        rr = gi - bcum
        cm = rr
        for sh in (1, 2, 4, 8, 16, 32, 64):
            if d == 0:
                cm = jnp.maximum(cm, jnp.where(r >= sh, pltpu.roll(cm, sh, 0), NEG))
            else:
                cm = jnp.maximum(cm, jnp.where(r < CHUNK - sh, pltpu.roll(cm, CHUNK - sh, 0), NEG))
        mt = jnp.maximum(bcum + m_st, bcum + cm)
        c1 = bcum - mt
        rt = jnp.transpose(rr)
        wt = jnp.transpose(wst)
        m8 = jnp.broadcast_to(m_st, (8, CHUNK))
        kts = {}

        for h in range(N_HEADS):
            p, half = divmod(h, 2)
            lm = (c // QK_HEAD) == half
            kp = kq[:, p * 128:(p + 1) * 128]
            qp = kq[:, QK_W + p * 128:QK_W + (p + 1) * 128]
            vaug = jnp.concatenate([v[:, h * 128:(h + 1) * 128], ones_b], axis=1)
            qm = jnp.where(lm, qp, jnp.zeros_like(qp))
            c1b = lane_bcast(c1, h)
            pm = jnp.exp(jnp.where(tri, c1b + rt[h:h + 1, :], NEG))
            s = (_dg(qm, kp, _NT) * pm).astype(BF16)
            m_in = jnp.tile(lane_bcast(m8, h), (CHUNK // 8, 1))
            qs = (qm.astype(F32) * jnp.exp(c1b + m_in)).astype(BF16)
            st = s_scr[d * N_HEADS + h]
            tot = _dg(jnp.concatenate([s, qs], axis=1), jnp.concatenate([vaug, st.astype(BF16)], axis=0))
            den = jnp.maximum(jnp.abs(tot[:, 128:]), jnp.exp(-lane_bcast(mt, h)))
            out_ref[:, h * 128:(h + 1) * 128] = (tot[:, :128] / den).astype(BF16)
            if p not in kts:
                kts[p] = jnp.transpose(kp.astype(F32))
            kw = jnp.where((r // QK_HEAD) == half, kts[p] * wt[h:h + 1, :], 0.0).astype(BF16)
            spb = jnp.tile(lane_bcast(sp8, h, 256), (CHUNK // 8, 1))
            s_scr[d * N_HEADS + h] = spb * st + _dg(kw, vaug)
        m_scr[d, 0:1, :] = m_new

    @pl.when(j == pl.num_programs(0) - 1)
    def _():
        sfin_ref[...] = s_scr[...]
        mfin_ref[...] = m_scr[...]


def _mlstm_call(kq, z, gates, bg, s0, m0):
    m = kq.shape[0]
    nc = m // CHUNK
    fwd = lambda cb: (lambda j: (j, cb))
    bwd = lambda cb: (lambda j: (nc - 1 - j, cb))
    st_spec = pl.BlockSpec((2 * N_HEADS, CHUNK, 256), lambda j: (0, 0, 0))
    m_spec = pl.BlockSpec((2, 8, 128), lambda j: (0, 0, 0))
    return pl.pallas_call(
        _mlstm_body,
        grid=(nc,),
        in_specs=[pl.BlockSpec((CHUNK, 1024), fwd(0)), pl.BlockSpec((CHUNK, 1024), fwd(ZC_V)),
                  pl.BlockSpec((CHUNK, GATE_PAD), fwd(0)),
                  pl.BlockSpec((CHUNK, 1024), bwd(0)), pl.BlockSpec((CHUNK, 1024), bwd(ZC_V)),
                  pl.BlockSpec((CHUNK, GATE_PAD), bwd(0)),
                  pl.BlockSpec((1, GATE_PAD), lambda j: (0, 0)), st_spec, m_spec],
        out_specs=[pl.BlockSpec((CHUNK, 1024), fwd(0)), pl.BlockSpec((CHUNK, 1024), bwd(0)), st_spec, m_spec],
        out_shape=[jax.ShapeDtypeStruct((m, 1024), BF16), jax.ShapeDtypeStruct((m, 1024), BF16),
                   jax.ShapeDtypeStruct((2 * N_HEADS, CHUNK, 256), F32),
                   jax.ShapeDtypeStruct((2, 8, 128), F32)],
        scratch_shapes=[pltpu.VMEM((2 * N_HEADS, CHUNK, 256), F32), pltpu.VMEM((2, 8, 128), F32)],
        compiler_params=_cparams(("arbitrary",), 32),
        name="mlstm",
    )(kq, z, gates, kq, z, gates, bg, s0, m0)


def _filt_body(seq_len, ft_ref, w1_ref, b1_ref, f1_ref, w2_ref, b2_ref, f2_ref, w3_ref, b3_ref, rt_ref,
               kf_ref, l1_ref):
    i = pl.program_id(0)
    tn = ft_ref.shape[1]
    hp = tn // 2
    reps = tn // 128
    col = lambda ref: jnp.tile(ref[...], (1, reps))
    h1 = jnp.sin(col(f1_ref) * (_dot3(w1_ref[...], ft_ref[...]) + col(b1_ref)))
    h2 = jnp.sin(col(f2_ref) * (_dot3(w2_ref[...], h1) + col(b2_ref)))
    r = lax.broadcasted_iota(I32, (hp, HY_W), 0)
    n_fwd = (8 * i + (r & 7)) * FFT_N2 + (r >> 3)
    rates = rt_ref[...]
    halves = []
    l1 = jnp.zeros((1, HY_W), F32)
    for hx in range(2):
        h = (_dot3(h2[:, hx * hp:(hx + 1) * hp], w3_ref[:, hx * HY_W:(hx + 1) * HY_W], _TN)
             + b3_ref[:, hx * HY_W:(hx + 1) * HY_W])
        n = n_fwd + hx * seq_len
        t01 = jnp.where(n <= seq_len, n, 2 * seq_len - n).astype(F32) / float(max(seq_len - 1, 1))
        h = jnp.where(n == seq_len, 0.0, h * jnp.exp(-t01 * rates))
        l1 = l1 + jnp.sum(jnp.abs(h), axis=0, keepdims=True)
        halves.append(h)
    word = _pack_pair(halves[0], halves[1])
    for b in range(FFT_N2):
        kf_ref[:, b * HY_W:(b + 1) * HY_W] = word[8 * b:8 * b + 8, :]

    @pl.when(i == 0)
    def _():
        l1_ref[...] = jnp.zeros_like(l1_ref)

    l1_ref[...] += l1


def _filt_call(feats_t, w1t, b1, f1, w2t, b2, f2, w3, b3, rates):
    n = feats_t.shape[1]
    seq_len = n // 2
    tn = 2 * 8 * FFT_N2
    c64 = lambda shape: pl.BlockSpec(shape, lambda i: (0, 0))
    return pl.pallas_call(
        functools.partial(_filt_body, seq_len),
        grid=(n // tn,),
        in_specs=[pl.BlockSpec((64, tn), lambda i: (0, i)),
                  c64((64, 64)), c64((64, 128)), c64((64, 128)), c64((64, 64)), c64((64, 128)), c64((64, 128)),
                  c64((64, 2 * HY_W)), c64((1, 2 * HY_W)), c64((1, HY_W))],
        out_specs=[pl.BlockSpec((8, FFT_N2 * HY_W), lambda i: (i, 0)), pl.BlockSpec((1, HY_W), lambda i: (0, 0))],
        out_shape=[jax.ShapeDtypeStruct((seq_len // FFT_N2, FFT_N2 * HY_W), U32),
                   jax.ShapeDtypeStruct((1, HY_W), F32)],
        compiler_params=_cparams(("arbitrary",), 48),
        name="filt",
    )(feats_t, w1t, b1, f1, w2t, b2, f2, w3, b3, rates)


def _fft_consts(n1_rows):
    n = n1_rows * FFT_N2
    kv = n1_rows // 2 + 1
    k1 = np.arange(FFT_KP, dtype=np.float64)
    valid = (k1 < kv).astype(np.float64)
    n1 = np.arange(n1_rows, dtype=np.float64)
    th1 = 2.0 * np.pi * np.outer(k1, n1) / n1_rows
    f1 = np.concatenate([np.cos(th1) * valid[:, None], -np.sin(th1) * valid[:, None]], axis=0)
    n2 = np.arange(FFT_N2, dtype=np.float64)
    tht = 2.0 * np.pi * np.outer(k1, n2) / n
    rep = lambda a: jnp.broadcast_to(jnp.asarray(a, F32)[:, :, None], (FFT_KP, FFT_N2, 128))
    twr = rep(np.cos(tht) * valid[:, None])
    twi = rep(-np.sin(tht) * valid[:, None])
    th2 = 2.0 * np.pi * np.outer(n2, n2) / FFT_N2
    cs, sn = np.cos(th2), np.sin(th2)
    f2p = np.block([[cs, sn], [-sn, cs]])
    f2pc = np.block([[cs, -sn], [sn, cs]])
    wk = np.where((k1 == 0) | (k1 == kv - 1), 1.0, 2.0) * valid / n
    half = n1_rows // 2
    thi = 2.0 * np.pi * np.outer(n1[:half], k1) / n1_rows
    gc = np.cos(thi) * wk[None, :]
    gs = np.sin(thi) * wk[None, :]
    as_bf = lambda a: jnp.asarray(a, F32).astype(BF16)
    return dict(f1=as_bf(f1), twr=twr, twi=twi,
                f2p=as_bf(f2p), f2pc=as_bf(f2pc), gc=as_bf(gc), gs=as_bf(gs))


def _fft1_body(f_ref, x_ref, ar_ref, ai_ref):
    o = _dg(f_ref[...], x_ref[...])
    ar_ref[...] = o[:FFT_KP].astype(BF16)
    ai_ref[...] = o[FFT_KP:].astype(BF16)


def _fft1_packed_body(f_ref, x_ref, ar_ref, ai_ref):
    k = x_ref.shape[0]
    hi, lo = _unpack_pair(x_ref[...])
    o = _dg(f_ref[:, 0:k], hi.astype(BF16)) + _dg(f_ref[:, k:], lo.astype(BF16))
    ar_ref[...] = o[:FFT_KP].astype(BF16)
    ai_ref[...] = o[FFT_KP:].astype(BF16)


def _fft1_call(f1, x2d, cb, packed=False):
    k, cols = x2d.shape
    f1 = f1[:, :2 * k] if packed else f1[:, :k]
    out = pl.BlockSpec((FFT_KP, cb), lambda i: (0, i))
    sh = jax.ShapeDtypeStruct((FFT_KP, cols), BF16)
    return pl.pallas_call(
        _fft1_packed_body if packed else _fft1_body,
        grid=(cols // cb,),
        in_specs=[pl.BlockSpec(f1.shape, lambda i: (0, 0)), pl.BlockSpec((k, cb), lambda i: (0, i))],
        out_specs=[out, out],
        out_shape=[sh, sh],
        compiler_params=_cparams(("arbitrary",), 32),
        name="fft1",
    )(f1, x2d)


def _twiddled(ar_ref, ai_ref, twr_ref, twi_ref, reps):
    a_r = ar_ref[...].astype(F32)
    a_i = ai_ref[...].astype(F32)
    tr = jnp.tile(twr_ref[...], (1, reps))
    ti = jnp.tile(twi_ref[...], (1, reps))
    st = jnp.concatenate([a_r * tr - a_i * ti, a_r * ti + a_i * tr], axis=0).astype(BF16)
    return st, tr, ti


FFT2_ROWS = 2


def _fft2_filt_body(kv, ar_ref, ai_ref, twr_ref, twi_ref, f2p_ref, k_ref):
    for j in range(FFT2_ROWS):
        k1 = pl.program_id(0) * FFT2_ROWS + j

        @pl.when(k1 < kv)
        def _():
            st, _, _ = _twiddled(ar_ref.at[j], ai_ref.at[j], twr_ref.at[j], twi_ref.at[j], ar_ref.shape[-1] // 128)
            k_ref[j] = _dg(f2p_ref[...], st).astype(BF16)

        @pl.when(k1 >= kv)
        def _():
            k_ref[j] = jnp.zeros(k_ref.shape[1:], BF16)


def _fft2_conv_body(kv, ar_ref, ai_ref, twr_ref, twi_ref, k_ref, f2p_ref, f2pc_ref, br_ref, bi_ref):
    for j in range(FFT2_ROWS):
        k1 = pl.program_id(0) * FFT2_ROWS + j

        @pl.when(k1 < kv)
        def _():
            st, tr, ti = _twiddled(ar_ref.at[j], ai_ref.at[j], twr_ref.at[j], twi_ref.at[j], ar_ref.shape[-1] // 128)
            x = _dg(f2p_ref[...], st)
            xr, xi = x[:FFT_N2], x[FFT_N2:]
            kr = k_ref[j, :FFT_N2, :].astype(F32)
            ki = k_ref[j, FFT_N2:, :].astype(F32)
            sy = jnp.concatenate([xr * kr - xi * ki, xr * ki + xi * kr], axis=0).astype(BF16)
            b = _dg(f2pc_ref[...], sy)
            b_r, b_i = b[:FFT_N2], b[FFT_N2:]
            br_ref[j] = (b_r * tr + b_i * ti).astype(BF16)
            bi_ref[j] = (b_i * tr - b_r * ti).astype(BF16)

        @pl.when(k1 >= kv)
        def _():
            br_ref[j] = jnp.zeros(br_ref.shape[1:], BF16)
            bi_ref[j] = jnp.zeros(bi_ref.shape[1:], BF16)


def _fft2_specs(ch, kv):
    src = lambda i: (jnp.minimum(i, (kv - 1) // FFT2_ROWS), 0, 0)
    blk = pl.BlockSpec((FFT2_ROWS, FFT_N2, ch), src)
    tw = pl.BlockSpec((FFT2_ROWS, FFT_N2, 128), src)
    mat = pl.BlockSpec((2 * FFT_N2, 2 * FFT_N2), lambda i: (0, 0))
    return blk, tw, mat, src


def _fft2_filt_call(ar, ai, cst, kv):
    ch = ar.shape[-1]
    blk, tw, mat, _ = _fft2_specs(ch, kv)
    return pl.pallas_call(
        functools.partial(_fft2_filt_body, kv),
        grid=(FFT_KP // FFT2_ROWS,),
        in_specs=[blk, blk, tw, tw, mat],
        out_specs=pl.BlockSpec((FFT2_ROWS, 2 * FFT_N2, ch), lambda i: (i, 0, 0)),
        out_shape=jax.ShapeDtypeStruct((FFT_KP, 2 * FFT_N2, ch), BF16),
        compiler_params=_cparams(("arbitrary",), 32),
        name="fft2_filt",
    )(ar, ai, cst["twr"], cst["twi"], cst["f2p"])


def _fft2_conv_call(ar, ai, khat, cst, kv):
    ch = ar.shape[-1]
    blk, tw, mat, src = _fft2_specs(ch, kv)
    sh = jax.ShapeDtypeStruct((FFT_KP, FFT_N2, ch), BF16)
    out = pl.BlockSpec((FFT2_ROWS, FFT_N2, ch), lambda i: (i, 0, 0))
    return pl.pallas_call(
        functools.partial(_fft2_conv_body, kv),
        grid=(FFT_KP // FFT2_ROWS,),
        in_specs=[blk, blk, tw, tw, pl.BlockSpec((FFT2_ROWS, 2 * FFT_N2, ch), src), mat, mat],
        out_specs=[out, out],
        out_shape=[sh, sh],
        compiler_params=_cparams(("arbitrary",), 32),
        name="fft2_conv",
    )(ar, ai, cst["twr"], cst["twi"], khat, cst["f2p"], cst["f2pc"])


def _ifft1_body(gc_ref, gs_ref, br_ref, bi_ref, x0_ref, xv_ref, il_ref, ds_ref, o_ref):
    y = _dg(gc_ref[...], br_ref[...]) - _dg(gs_ref[...], bi_ref[...])
    o_ref[...] = (x0_ref[...].astype(F32)
                  * (y * il_ref[...] + ds_ref[...] * xv_ref[...].astype(F32))).astype(BF16)


def _ifft1_call(cst, br2d, bi2d, x0_2d, xv_2d, il_t, ds_t, cb):
    rows, cols = x0_2d.shape
    g = pl.BlockSpec((rows, FFT_KP), lambda i: (0, 0))
    kb = pl.BlockSpec((FFT_KP, cb), lambda i: (0, i))
    xb = pl.BlockSpec((rows, cb), lambda i: (0, i))
    vb = pl.BlockSpec((1, cb), lambda i: (0, 0))
    return pl.pallas_call(
        _ifft1_body,
        grid=(cols // cb,),
        in_specs=[g, g, kb, kb, xb, xb, vb, vb],
        out_specs=xb,
        out_shape=jax.ShapeDtypeStruct((rows, cols), BF16),
        compiler_params=_cparams(("arbitrary",), 32),
        name="ifft1",
    )(cst["gc"], cst["gs"], br2d, bi2d, x0_2d, xv_2d, il_t, ds_t)


def _outproj_body(hf_ref, hb_ref, zo_ref, yh_ref, x_ref, erow_ref, ecol_ref, gh_ref, wa_ref, wb_ref,
                  g1_ref, gf_ref, sh_ref, sc_ref, wr_ref, x1_ref, h2_ref, s_ref):
    hs = hf_ref[...].astype(F32) + hb_ref[...].astype(F32)
    gh = gh_ref[...]
    parts = []
    for h in range(N_HEADS):
        hh = hs[:, h * 128:(h + 1) * 128]
        ms = jnp.mean(hh * hh, axis=-1, keepdims=True)
        parts.append(hh * lax.rsqrt(ms + EPS) * gh[:, h * 128:(h + 1) * 128])
    ym = jnp.concatenate(parts, axis=-1) * _sigmoid(zo_ref[...].astype(F32))
    y = _dg(ym.astype(BF16), wa_ref[...]) + _dg(yh_ref[...], wb_ref[...])
    rp = x_ref.shape[0] // GRID_W
    erow8 = erow_ref[...]
    erow = erow8[0:rp, :]
    for q in range(1, 8 // rp):
        erow = jnp.where(pl.program_id(0) % (8 // rp) == q, erow8[q * rp:(q + 1) * rp, :], erow)
    x1 = _add_pos(x_ref[...], erow, ecol_ref[...]) + g1_ref[...] * y
    x1_ref[...] = x1
    h2 = _norm_mod(x1, gf_ref[...], sh_ref[...], sc_ref[...])
    half = h2.shape[1] // 2
    _store_slabs(h2_ref, _pack_pair(h2[:, :half], h2[:, half:]))
    s_ref[...] = _sigmoid(_dot3(wr_ref[...], h2, _NT))


def _outproj_call(hf, hb, z, yh, x, erow, ecol, gh, wa, wb, g1, gf, sh2, sc2, wrt, tm):
    m, d = x.shape
    row = lambda cb: pl.BlockSpec((tm, 1024), lambda i: (i, cb))
    vec = lambda n: pl.BlockSpec((1, n), lambda i: (0, 0))
    full = pl.BlockSpec((tm, d), lambda i: (i, 0))
    return pl.pallas_call(
        _outproj_body,
        grid=(m // tm,),
        in_specs=[row(0), row(0), row(ZC_O), row(0), full,
                  pl.BlockSpec((8, d // 2), lambda i: (i * (tm // GRID_W) // 8, 0)),
                  pl.BlockSpec((GRID_W, d // 2), lambda i: (0, 0)),
                  vec(MV_W),
                  pl.BlockSpec((MV_W, d), lambda i: (0, 0)), pl.BlockSpec((HY_W, d), lambda i: (0, 0)),
                  vec(d), vec(d), vec(d), vec(d),
                  pl.BlockSpec((N_EXPERTS, d), lambda i: (0, 0))],
        out_specs=[full, pl.BlockSpec((tm * SLAB, SLAB_W), lambda i: (i, 0)),
                   pl.BlockSpec((N_EXPERTS, tm), lambda i: (0, i))],
        out_shape=[jax.ShapeDtypeStruct((m, d), F32), jax.ShapeDtypeStruct((m * SLAB, SLAB_W), U32),
                   jax.ShapeDtypeStruct((N_EXPERTS, m), F32)],
        compiler_params=_cparams(("arbitrary",), 56),
        name="outproj",
    )(hf, hb, z, yh, x, erow, ecol, gh, wa, wb, g1, gf, sh2, sc2, wrt)


def _first_max(x, idx, sentinel):
    m = jnp.max(x, axis=0, keepdims=True)
    return m, jnp.min(jnp.where(x == m, idx, sentinel), axis=0, keepdims=True)


def _route_body(s_ref, b_ref, e_ref, w_ref, r_ref, cnt_ref, u_scr, run_scr):
    i = pl.program_id(0)
    tt = s_ref.shape[1]

    @pl.when(i == 0)
    def _():
        rr = lax.broadcasted_iota(I32, (tt, tt), 0)
        cc = lax.broadcasted_iota(I32, (tt, tt), 1)
        u_scr[...] = jnp.where(rr < cc, 1.0, 0.0).astype(BF16)
        run_scr[...] = jnp.zeros_like(run_scr)

    s = s_ref[...]
    sel = s + b_ref[...][:, 0:1]
    sub8 = lax.broadcasted_iota(I32, (E_PER_GROUP, tt), 0).astype(F32)
    gs = jnp.zeros((N_GROUPS, tt), F32)
    for g in range(N_GROUPS):
        grp = sel[g * E_PER_GROUP:(g + 1) * E_PER_GROUP, :]
        m1, i1 = _first_max(grp, sub8, float(E_PER_GROUP))
        m2 = jnp.max(jnp.where(sub8 == i1, -jnp.inf, grp), axis=0, keepdims=True)
        gs = jnp.where(sub8 == g, m1 + m2, gs)
    gmask = jnp.zeros((N_GROUPS, tt), F32)
    for _ in range(TOPK_GROUPS):
        _, ig = _first_max(gs, sub8, float(N_GROUPS))
        hit = sub8 == ig
        gmask = jnp.where(hit, 1.0, gmask)
        gs = jnp.where(hit, -jnp.inf, gs)
    masked = jnp.concatenate(
        [jnp.where(jnp.broadcast_to(gmask[g:g + 1, :], (E_PER_GROUP, tt)) > 0.5,
                   sel[g * E_PER_GROUP:(g + 1) * E_PER_GROUP, :], -jnp.inf) for g in range(N_GROUPS)], axis=0)
    sub64 = lax.broadcasted_iota(I32, (N_EXPERTS, tt), 0).astype(F32)
    oh = jnp.zeros((N_EXPERTS, tt), F32)
    eks, wks = [], []
    for _ in range(TOP_K):
        _, ie = _first_max(masked, sub64, float(N_EXPERTS))
        hit = sub64 == ie
        wks.append(jnp.sum(jnp.where(hit, s, 0.0), axis=0, keepdims=True))
        eks.append(ie)
        masked = jnp.where(hit, -jnp.inf, masked)
        oh = jnp.where(hit, 1.0, oh)
    wsum = wks[0]
    for k in range(1, TOP_K):
        wsum = wsum + wks[k]
    run = run_scr[...]
    rank_t = _dg(oh.astype(BF16), u_scr[...]) + jnp.tile(run, (1, tt // 128))
    for k in range(TOP_K):
        e_ref[k:k + 1, :] = eks[k].astype(I32)
        w_ref[k:k + 1, :] = wks[k] / wsum * ROUTE_SCALE
        r_ref[k:k + 1, :] = jnp.sum(jnp.where(sub64 == eks[k], rank_t, 0.0), axis=0, keepdims=True).astype(I32)
    run_new = run + jnp.sum(oh, axis=1, keepdims=True)
    run_scr[...] = run_new
    cnt_ref[...] = run_new.astype(I32)


def _route_call(s_t, b_col, tt):
    m = s_t.shape[1]
    out = pl.BlockSpec((TOP_K, tt), lambda i: (0, i))
    return pl.pallas_call(
        _route_body,
        grid=(m // tt,),
        in_specs=[pl.BlockSpec((N_EXPERTS, tt), lambda i: (0, i)),
                  pl.BlockSpec((N_EXPERTS, 128), lambda i: (0, 0))],
        out_specs=[out, out, out, pl.BlockSpec((N_EXPERTS, 128), lambda i: (0, 0))],
        out_shape=[jax.ShapeDtypeStruct((TOP_K, m), I32), jax.ShapeDtypeStruct((TOP_K, m), F32),
                   jax.ShapeDtypeStruct((TOP_K, m), I32), jax.ShapeDtypeStruct((N_EXPERTS, 128), I32)],
        scratch_shapes=[pltpu.VMEM((tt, tt), BF16), pltpu.VMEM((N_EXPERTS, 128), F32)],
        compiler_params=_cparams(("arbitrary",), 32),
        name="route",
    )(s_t, b_col)


def _posk_body(pst_ref, e_ref, r_ref, p_ref):
    e = e_ref[...]
    acc = r_ref[...]
    for x in range(N_EXPERTS):
        acc = acc + jnp.where(e == x, pst_ref[x], 0)
    p_ref[...] = acc


def _posk_call(pstart, eidx, rank):
    k, m = eidx.shape
    tt = min(m, 2048)
    blk = pl.BlockSpec((k, tt), lambda i, pst: (0, i))
    return pl.pallas_call(
        _posk_body,
        grid_spec=pltpu.PrefetchScalarGridSpec(num_scalar_prefetch=1, grid=(m // tt,),
                                               in_specs=[blk, blk], out_specs=blk),
        out_shape=jax.ShapeDtypeStruct((k, m), I32),
        compiler_params=_cparams(("arbitrary",), 32),
        name="posk",
    )(pstart, eidx, rank)


def _slab(ref, r):
    return ref.at[pl.ds(pl.multiple_of(r * SLAB, SLAB), SLAB), :]


def _ffn_packed(x_ref, rows, w1, w3, w2, base=0):
    half = w1.shape[0] // 2
    xa, xb = _unpack_pair(_load_slabs(x_ref, rows, base))
    xa = xa.astype(BF16)
    xb = xb.astype(BF16)
    h1 = _dg(xa, w1[0:half, :]) + _dg(xb, w1[half:, :])
    h3 = _dg(xa, w3[0:half, :]) + _dg(xb, w3[half:, :])
    return _dg((_silu(h1) * h3).astype(BF16), w2[...])


def _dispatch_body(cnt_ref, pst_ref, pcn_ref, h2_ref, pos_ref, w1_ref, w3_ref, w2_ref,
                   xs_ref, sh_ref, zrow, sem):
    i = pl.program_id(0)
    td = h2_ref.shape[0] // SLAB

    def row_copy(t, dst):
        return pltpu.make_async_copy(_slab(h2_ref, t), _slab(xs_ref, dst), sem)

    def issue(t, carry):
        for k in range(TOP_K):
            row_copy(t, pos_ref[k, t]).start(priority=k % 2)
        return carry

    lax.fori_loop(0, td, issue, 0)

    sh_ref[...] = _ffn_packed(h2_ref, td, w1_ref, w3_ref, w2_ref).astype(BF16)

    def drain(t, carry):
        for k in range(TOP_K):
            row_copy(0, 0).wait()
        return carry

    lax.fori_loop(0, td, drain, 0)

    @pl.when(i == pl.num_programs(0) - 1)
    def _():
        zrow[...] = jnp.zeros_like(zrow)

        def zero_copy(dst):
            return pltpu.make_async_copy(zrow, _slab(xs_ref, dst), sem)

        def per_expert(e, carry):
            base = pst_ref[e]
            lax.fori_loop(cnt_ref[e], pcn_ref[e], lambda rr, cc: (zero_copy(base + rr).start(), cc)[1], 0)
            lax.fori_loop(cnt_ref[e], pcn_ref[e], lambda rr, cc: (zero_copy(0).wait(), cc)[1], 0)
            return carry

        lax.fori_loop(0, N_EXPERTS, per_expert, 0)


def _dispatch_call(cnt, pstart, pcnt, h2s, pos, w1s, w3s, w2s, rows, td):
    m = h2s.shape[0] // SLAB
    d, ds = w1s.shape
    return pl.pallas_call(
        _dispatch_body,
        grid_spec=pltpu.PrefetchScalarGridSpec(
            num_scalar_prefetch=3, grid=(m // td,),
            in_specs=[pl.BlockSpec((td * SLAB, SLAB_W), lambda i, *_: (i, 0)),
                      pl.BlockSpec((TOP_K, td), lambda i, *_: (0, i), memory_space=pltpu.SMEM),
                      pl.BlockSpec((d, ds), lambda i, *_: (0, 0)), pl.BlockSpec((d, ds), lambda i, *_: (0, 0)),
                      pl.BlockSpec((ds, d), lambda i, *_: (0, 0))],
            out_specs=[pl.BlockSpec(memory_space=pl.ANY), pl.BlockSpec((td, d), lambda i, *_: (i, 0))],
            scratch_shapes=[pltpu.VMEM((SLAB, SLAB_W), U32), pltpu.SemaphoreType.DMA(())]),
        out_shape=[jax.ShapeDtypeStruct((rows * SLAB, SLAB_W), U32), jax.ShapeDtypeStruct((m, d), BF16)],
        compiler_params=_cparams(("arbitrary",), 40),
        name="dispatch",
    )(cnt, pstart, pcnt, h2s, pos, w1s, w3s, w2s)


def _moe_body(te_ref, nu_ref, nxt_ref, par_ref, x_ref, w1_ref, w3_ref, w2_ref, y_ref,
              f1, f3, f2, w1b, w3b, w2b, sems):
    def fetch(ex, s):
        return (pltpu.make_async_copy(w1_ref.at[ex], f1.at[s], sems.at[s]),
                pltpu.make_async_copy(w3_ref.at[ex], f3.at[s], sems.at[s]),
                pltpu.make_async_copy(w2_ref.at[ex], f2.at[s], sems.at[s]))

    for sub in range(MOE_TILES_PER_STEP):
        ti = pl.program_id(0) * MOE_TILES_PER_STEP + sub
        used = ti < nu_ref[0]
        e = te_ref[ti]
        first = jnp.logical_or(ti == 0, e != te_ref[jnp.maximum(ti - 1, 0)])
        slot = par_ref[e]

        @pl.when(jnp.logical_and(used, ti == 0))
        def _():
            for c in fetch(e, slot):
                c.start()

        @pl.when(jnp.logical_and(used, first))
        def _():
            for c in fetch(e, slot):
                c.wait()
            nx = nxt_ref[e]

            @pl.when(nx < N_EXPERTS)
            def _():
                for c in fetch(nx, 1 - slot):
                    c.start()

            w1b[...] = f1[slot].astype(BF16)
            w3b[...] = f3[slot].astype(BF16)
            w2b[...] = f2[slot].astype(BF16)

        @pl.when(used)
        def _():
            half = w1b.shape[0] // 2
            base = sub * MOE_ROWS * SLAB
            y = _ffn_packed(x_ref, MOE_ROWS, w1b, w3b, w2b, base)
            _store_slabs(y_ref, _pack_pair(y[:, :half], y[:, half:]), base)


def _moe_call(tile_e, n_used, nxt, par, xs, w1, w3, w2):
    rows = xs.shape[0] // SLAB
    step_rows = MOE_ROWS * MOE_TILES_PER_STEP
    nt = rows // step_rows
    d, de = w1.shape[-2:]
    rmap = lambda i, te, nu, *_: (jnp.minimum(i, (nu[0] - 1) // MOE_TILES_PER_STEP), 0)
    hbm = pl.BlockSpec(memory_space=pl.ANY)
    return pl.pallas_call(
        _moe_body,
        grid_spec=pltpu.PrefetchScalarGridSpec(
            num_scalar_prefetch=4, grid=(nt,),
            in_specs=[pl.BlockSpec((step_rows * SLAB, SLAB_W), rmap), hbm, hbm, hbm],
            out_specs=pl.BlockSpec((step_rows * SLAB, SLAB_W), rmap),
            scratch_shapes=[pltpu.VMEM((2, d, de), F32), pltpu.VMEM((2, d, de), F32), pltpu.VMEM((2, de, d), F32),
                            pltpu.VMEM((d, de), BF16), pltpu.VMEM((d, de), BF16), pltpu.VMEM((de, d), BF16),
                            pltpu.SemaphoreType.DMA((2,))]),
        out_shape=jax.ShapeDtypeStruct((rows * SLAB, SLAB_W), U32),
        compiler_params=_cparams(("arbitrary",), 56),
        name="moe",
    )(tile_e, n_used, nxt, par, xs, w1, w3, w2)


def _final_body(x1_ref, sh_ref, pos_ref, posn_ref, wt_ref, ys_ref, g2_ref, gn_ref, o_ref, ybuf, sems):
    i = pl.program_id(0)
    n = pl.num_programs(0)
    tf, d = x1_ref.shape
    half = d // 2
    slot = i % 2
    slot_slabs = TOP_K * tf

    def row_copy(p_ref, s, k, t):
        return pltpu.make_async_copy(_slab(ys_ref, p_ref[k, t]), _slab(ybuf, s * slot_slabs + k * tf + t),
                                     sems.at[s])

    def issue(p_ref, s):
        def body(t, carry):
            for k in range(TOP_K):
                row_copy(p_ref, s, k, t).start(priority=k % 2)
            return carry

        lax.fori_loop(0, tf, body, 0)

    @pl.when(i == 0)
    def _():
        issue(pos_ref, 0)

    @pl.when(i + 1 < n)
    def _():
        issue(posn_ref, 1 - slot)

    def drain(t, carry):
        for k in range(TOP_K):
            pltpu.make_async_copy(_slab(ys_ref, 0), _slab(ybuf, 0), sems.at[slot]).wait()
        return carry

    lax.fori_loop(0, tf, drain, 0)

    wt = jnp.concatenate([jnp.transpose(jnp.concatenate(
        [wt_ref[:, c * 128:(c + 1) * 128], jnp.zeros((128 - TOP_K, 128), F32)], axis=0)) for c in range(tf // 128)],
        axis=0)
    shared = sh_ref[...].astype(F32)
    acc_a = shared[:, :half]
    acc_b = shared[:, half:]
    base = slot * slot_slabs * SLAB
    for k in range(TOP_K):
        ya, yb = _unpack_pair(_load_slabs(ybuf, tf, base=base + k * tf * SLAB))
        acc_a = acc_a + wt[:, k:k + 1] * ya
        acc_b = acc_b + wt[:, k:k + 1] * yb
    xo = x1_ref[...] + g2_ref[...] * jnp.concatenate([acc_a, acc_b], axis=1)
    ms = jnp.mean(xo * xo, axis=-1, keepdims=True)
    o_ref[...] = xo * lax.rsqrt(ms + EPS) * gn_ref[...]


def _final_call(x1, sh, pos, wts, ys, g2, gn, tf):
    m, d = x1.shape
    nt = m // tf
    full = pl.BlockSpec((tf, d), lambda i: (i, 0))
    vec = pl.BlockSpec((1, d), lambda i: (0, 0))
    return pl.pallas_call(
        _final_body,
        grid=(nt,),
        in_specs=[full, full,
                  pl.BlockSpec((TOP_K, tf), lambda i: (0, i), memory_space=pltpu.SMEM),
                  pl.BlockSpec((TOP_K, tf), lambda i: (0, jnp.minimum(i + 1, nt - 1)), memory_space=pltpu.SMEM),
                  pl.BlockSpec((TOP_K, tf), lambda i: (0, i)),
                  pl.BlockSpec(memory_space=pl.ANY), vec, vec],
        out_specs=full,
        out_shape=jax.ShapeDtypeStruct((m, d), F32),
        scratch_shapes=[pltpu.VMEM((2 * TOP_K * tf * SLAB, SLAB_W), U32), pltpu.SemaphoreType.DMA((2,))],
        compiler_params=_cparams(("arbitrary",), 48),
        name="final",
    )(x1, sh, pos, pos, wts, ys, g2, gn)


def _pos_tables(n_tokens):
    rows = n_tokens // GRID_W
    quarter = D_MODEL // 4
    omega = 1.0 / (10000.0 ** (jnp.arange(quarter, dtype=F32) / quarter))

    def emb1d(pos):
        ang = pos[:, None] * omega[None]
        return jnp.concatenate([jnp.sin(ang), jnp.cos(ang)], axis=-1)

    return emb1d(jnp.arange(rows, dtype=F32)), emb1d(jnp.arange(GRID_W, dtype=F32))


def _filter_feats(L):
    n1h = L // FFT_N2
    i_, h_, b_, a_ = jnp.meshgrid(jnp.arange(n1h // 8, dtype=I32), jnp.arange(2, dtype=I32),
                                  jnp.arange(FFT_N2, dtype=I32), jnp.arange(8, dtype=I32), indexing="ij")
    n = ((8 * i_ + a_ + h_ * n1h) * FFT_N2 + b_).reshape(-1)
    t = jnp.where(n <= L, n, 2 * L - n).astype(F32)
    t01 = t / max(L - 1, 1)
    w = 2.0 * math.pi * t / L
    bands = jnp.linspace(1e-4, FILT_BANDS - 1, FILT_BANDS, dtype=F32)
    feats = jnp.concatenate([t01[None, :], jnp.cos(bands[:, None] * w[None, :]), -jnp.sin(bands[:, None] * w[None, :]),
                             jnp.zeros((64 - 33, 2 * L), F32)], axis=0)
    return feats


def _pad_rows(a, rows):
    return jnp.concatenate([a, jnp.zeros((rows - a.shape[0],) + a.shape[1:], a.dtype)], axis=0)


def _layer(x, c, ctx, c_ctx, w_ada, b_ada, g_mix, g_ffn, w_in, b_gates, conv_k_w, conv_k_b,
           conv_q_w, conv_q_b, g_head, conv_hy_w, conv_hy_b, filt_w1, filt_b1, filt_freq1,
           filt_w2, filt_b2, filt_freq2, filt_w3, filt_b3, hy_dskip, w_out, w_router, b_router,
           w1_e, w3_e, w2_e, w1_s, w3_s, w2_s, g_final):
    L, d = x.shape
    lc = ctx.shape[0]
    row = lambda v: v.reshape(1, -1)

    cc = _pad_rows(jnp.stack([c, c_ctx], axis=0), 8)
    mods = _mod_call(cc, w_ada, row(b_ada))
    sh1, sc1, g1, sh2, sc2, g2 = [mods[0:1, k * d:(k + 1) * d] for k in range(6)]
    csh1, csc1 = mods[1:2, 0:d], mods[1:2, d:2 * d]

    w_r, w_g = _wprep_call(jnp.transpose(w_in[0]))
    bg = jnp.concatenate([b_gates, jnp.zeros((GATE_PAD - 4 * N_HEADS,), F32)]).reshape(1, GATE_PAD)
    e_row, e_col = _pos_tables(L)
    conv_w = jnp.concatenate([conv_k_w, conv_q_w], axis=1)
    conv_b = jnp.concatenate([conv_k_b, conv_q_b]).reshape(1, -1)
    conv_s = jnp.concatenate([jnp.ones((QK_W,), F32), jnp.full((QK_W,), QK_HEAD ** -0.5, F32)]).reshape(1, -1)

    z_c, gt_c = _inproj_call(ctx, jnp.zeros((8, d // 2), F32), e_col, row(g_mix), csh1, csc1, w_r, w_g,
                             use_pos=False, tm=min(lc, 256))
    kq_c = _conv_kq_call(z_c, conv_w, conv_b, conv_s, tm=min(lc, 256))
    s0 = jnp.zeros((2 * N_HEADS, CHUNK, 256), F32)
    m0 = jnp.zeros((2, 8, 128), F32)
    _, _, s_ctx, m_ctx = _mlstm_call(kq_c, z_c, gt_c, bg, s0, m0)

    z, gates = _inproj_call(x, e_row, e_col, row(g_mix), sh1, sc1, w_r, w_g, use_pos=True, tm=min(L, 1024))
    kq = _conv_kq_call(z, conv_w, conv_b, conv_s, tm=min(L, 512))
    x0c, xv = _conv_hy_call(z, conv_hy_w, row(conv_hy_b), tm=min(L, 512))
    hf, hb, _, _ = _mlstm_call(kq, z, gates, bg, s_ctx, m_ctx)

    n1 = 2 * L // FFT_N2
    cst = _fft_consts(n1)
    rates = jnp.linspace(-math.log(DECAY_TARGET) / SLOW_DECAY_PCT, -math.log(DECAY_TARGET) / FAST_DECAY_PCT,
                         HY_W, dtype=F32).reshape(1, -1)
    w1t = jnp.transpose(_pad_rows(filt_w1, 64))
    colrep = lambda v: jnp.broadcast_to(v.reshape(-1, 1), (v.shape[0], 128))
    kf, l1 = _filt_call(_filter_feats(L), w1t, colrep(filt_b1), colrep(filt_freq1), jnp.transpose(filt_w2),
                        colrep(filt_b2), colrep(filt_freq2), filt_w3, row(filt_b3), rates)
    cols = FFT_N2 * HY_W
    cb = 2048
    kar, kai = _fft1_call(cst["f1"], kf, cb, packed=True)
    kv = n1 // 2 + 1
    khat = _fft2_filt_call(kar.reshape(FFT_KP, FFT_N2, HY_W), kai.reshape(FFT_KP, FFT_N2, HY_W), cst, kv)
    uar, uai = _fft1_call(cst["f1"], xv.reshape(n1 // 2, cols), cb)
    br, bi = _fft2_conv_call(uar.reshape(FFT_KP, FFT_N2, HY_W), uai.reshape(FFT_KP, FFT_N2, HY_W), khat, cst, kv)
    reps = cb // HY_W
    il_t = jnp.tile(1.0 / l1, (1, reps))
    ds_t = jnp.tile(row(hy_dskip), (1, reps))
    yh = _ifft1_call(cst, br.reshape(FFT_KP, cols), bi.reshape(FFT_KP, cols),
                     x0c.reshape(n1 // 2, cols), xv.reshape(n1 // 2, cols), il_t, ds_t, cb).reshape(L, HY_W)

    wo = w_out.astype(BF16)
    x1, h2s, s_t = _outproj_call(hf, hb, z, yh, x, e_row, e_col, row(g_head), wo[:MV_W], wo[MV_W:],
                                g1, row(g_ffn), sh2, sc2, jnp.transpose(w_router), tm=min(L, 256))

    b_col = jnp.broadcast_to(b_router.reshape(N_EXPERTS, 1), (N_EXPERTS, 128))
    eidx, wts, rank, cnt2 = _route_call(s_t, b_col, tt=min(L, 1024))
    cnt = cnt2[:, 0]
    pcnt = (cnt + MOE_ROWS - 1) // MOE_ROWS * MOE_ROWS
    pend = jnp.cumsum(pcnt)
    pstart = pend - pcnt
    rows = L * TOP_K + N_EXPERTS * MOE_ROWS
    nt = rows // MOE_ROWS
    tile_row = jnp.arange(nt, dtype=I32) * MOE_ROWS
    tile_e = jnp.minimum(jnp.sum((pend[None, :] <= tile_row[:, None]).astype(I32), axis=1), N_EXPERTS - 1)
    n_used = (pend[-1] // MOE_ROWS).astype(I32).reshape(1)
    pos = _posk_call(pstart.astype(I32), eidx, rank)

    xs, sh = _dispatch_call(cnt, pstart.astype(I32), pcnt.astype(I32), h2s, pos, w1_s.astype(BF16),
                            w3_s.astype(BF16), w2_s.astype(BF16), rows, td=min(L, 256))
    ex = jnp.arange(N_EXPERTS, dtype=I32)
    nonempty = pcnt > 0
    nxt = jnp.min(jnp.where((ex[None, :] > ex[:, None]) & nonempty[None, :], ex[None, :], N_EXPERTS), axis=1)
    par = (jnp.cumsum(nonempty.astype(I32)) + 1) % 2
    ys = _moe_call(tile_e, n_used, nxt.astype(I32), par.astype(I32), xs, w1_e, w3_e, w2_e)
    return _final_call(x1, sh, pos, wts, ys, g2, row(g_final), tf=min(L, 256))


def kernel(x, c, ctx, c_ctx, w_ada, b_ada, g_mix, g_ffn, w_in, b_gates, conv_k_w, conv_k_b, conv_q_w,
           conv_q_b, g_head, conv_hy_w, conv_hy_b, filt_w1, filt_b1, filt_freq1, filt_w2, filt_b2,
           filt_freq2, filt_w3, filt_b3, hy_dskip, w_out, w_router, b_router, w1_e, w3_e, w2_e,
           w1_s, w3_s, w2_s, g_final):
    assert x.shape[0] == 1 and w_ada.shape[0] == 1, "one batch element, one layer"
    out = _layer(x[0], c[0], ctx[0], c_ctx, w_ada[0], b_ada[0], g_mix[0], g_ffn[0], w_in, b_gates[0],
                 conv_k_w[0], conv_k_b[0], conv_q_w[0], conv_q_b[0], g_head[0], conv_hy_w[0], conv_hy_b[0],
                 filt_w1[0], filt_b1[0], filt_freq1[0], filt_w2[0], filt_b2[0], filt_freq2[0], filt_w3[0],
                 filt_b3[0], hy_dskip[0], w_out[0], w_router[0], b_router[0], w1_e[0], w3_e[0], w2_e[0],
                 w1_s[0], w3_s[0], w2_s[0], g_final)
    return out[None]
```

```python
import functools
import math

import numpy as np
import jax
import jax.numpy as jnp
from jax import lax
from jax.experimental import pallas as pl
from jax.experimental.pallas import tpu as pltpu

F32 = jnp.float32
BF16 = jnp.bfloat16
I32 = jnp.int32
U32 = jnp.uint32

D_MODEL = 2048
GRID_W = 64
N_HEADS = 8
QK_HEAD = 64
V_HEAD = 128
QK_W = N_HEADS * QK_HEAD
MV_W = N_HEADS * V_HEAD
HY_W = D_MODEL - MV_W
CHUNK = 128
FILT_BANDS = 16
FILT_HIDDEN = 64
DECAY_TARGET = 1e-2
FAST_DECAY_PCT = 0.3
SLOW_DECAY_PCT = 1.5
N_EXPERTS = 64
N_GROUPS = 8
E_PER_GROUP = 8
TOPK_GROUPS = 4
TOP_K = 8
D_EXPERT = 512
ROUTE_SCALE = 2.5
EPS = 1e-6
OFF_K = 0
OFF_V = OFF_K + QK_W
OFF_G = OFF_V + MV_W
OFF_Q = OFF_G + 4 * N_HEADS
OFF_O = OFF_Q + QK_W
OFF_HY = OFF_O + MV_W

ZC_KQ, ZC_V, ZC_O, ZC_X0, ZC_X1, ZC_HV = 0, 1, 2, 3, 4, 5
Z_COLS = 6 * 1024
GATE_PAD = 128

NEG = -1e30
MIB = 1024 * 1024

FFT_N2 = 128
FFT_KP = 144

MOE_ROWS = 256
MOE_TILES_PER_STEP = 4


def _cparams(sem, vmem_mb, flags=None):
    return pltpu.CompilerParams(dimension_semantics=sem, vmem_limit_bytes=vmem_mb * MIB, flags=flags)


def _split2(x):
    hi = x.astype(BF16)
    lo = (x - hi.astype(F32)).astype(BF16)
    return hi, lo


_NN = (((1,), (0,)), ((), ()))
_NT = (((1,), (1,)), ((), ()))
_TN = (((0,), (0,)), ((), ()))


def _dg(a, b, dims=_NN):
    return lax.dot_general(a, b, dims, preferred_element_type=F32)


def _dot3(a, b, dims=_NN):
    ah, al = _split2(a)
    bh, bl = _split2(b)
    return _dg(ah, bh, dims) + _dg(al, bh, dims) + _dg(ah, bl, dims)


def _sigmoid(x):
    return 1.0 / (1.0 + jnp.exp(-x))


def _silu(x):
    return x * _sigmoid(x)


def _pack_pair(a, b):
    hi = lax.bitcast_convert_type(a.astype(BF16).astype(F32), U32)
    lo = lax.bitcast_convert_type(b.astype(BF16).astype(F32), U32)
    return hi | (lo >> 16)


def _unpack_pair(w):
    a = lax.bitcast_convert_type(w & jnp.uint32(0xFFFF0000), F32)
    b = lax.bitcast_convert_type(w << 16, F32)
    return a, b


SLAB = 8
SLAB_W = 128


def _store_slabs(ref, w, base=0):
    r = w.shape[0]
    for j in range(SLAB):
        ref[pl.ds(base + j, r, stride=SLAB), :] = w[:, j * SLAB_W:(j + 1) * SLAB_W]


def _load_slabs(ref, r, base=0):
    return jnp.concatenate([ref[pl.ds(base + j, r, stride=SLAB), :] for j in range(SLAB)], axis=1)


def _norm_mod(x, g, sh, sc):
    ms = jnp.mean(x * x, axis=-1, keepdims=True)
    return (x * lax.rsqrt(ms + EPS) * g) * (1.0 + sc) + sh


def _add_pos(x, erow, ecol):
    tm, d = x.shape
    half = d // 2
    parts = []
    for r in range(tm // GRID_W):
        xs = x[r * GRID_W:(r + 1) * GRID_W, :]
        parts.append(jnp.concatenate([xs[:, :half] + erow[r:r + 1, :], xs[:, half:] + ecol], axis=-1))
    return parts[0] if len(parts) == 1 else jnp.concatenate(parts, axis=0)


def _mod_body(cc_ref, w_ref, b_ref, o_ref):
    o_ref[...] = _dot3(_silu(cc_ref[...]), w_ref[...]) + b_ref[...]


def _mod_call(cc, w, b):
    d, n = w.shape
    tn = 1024
    return pl.pallas_call(
        _mod_body,
        grid=(n // tn,),
        in_specs=[pl.BlockSpec((8, d), lambda j: (0, 0)),
                  pl.BlockSpec((d, tn), lambda j: (0, j)),
                  pl.BlockSpec((1, tn), lambda j: (0, j))],
        out_specs=pl.BlockSpec((8, tn), lambda j: (0, j)),
        out_shape=jax.ShapeDtypeStruct((8, n), F32),
        compiler_params=_cparams(("arbitrary",), 40),
        name="mod",
    )(cc, w, b)


def _wprep_body(w_ref, wr_ref, wg_ref):
    w = w_ref[...]
    wt = jnp.concatenate([w[OFF_K:OFF_V], w[OFF_Q:OFF_O], w[OFF_V:OFF_G], w[OFF_O:]], axis=0)
    wr_ref[...] = jnp.transpose(wt).astype(BF16)
    g = jnp.concatenate([w[OFF_G:OFF_Q], jnp.zeros((GATE_PAD - 4 * N_HEADS, w.shape[1]), F32)], axis=0)
    wg_ref[...] = jnp.transpose(g)


def _wprep_call(w_t):
    n, d = w_t.shape
    tr = 256
    return pl.pallas_call(
        _wprep_body,
        grid=(d // tr,),
        in_specs=[pl.BlockSpec((n, tr), lambda i: (0, i))],
        out_specs=[pl.BlockSpec((tr, Z_COLS), lambda i: (i, 0)), pl.BlockSpec((tr, GATE_PAD), lambda i: (i, 0))],
        out_shape=[jax.ShapeDtypeStruct((d, Z_COLS), BF16), jax.ShapeDtypeStruct((d, GATE_PAD), F32)],
        compiler_params=_cparams(("arbitrary",), 32),
        name="wprep",
    )(w_t)


def _inproj_body(use_pos, x_ref, erow_ref, ecol_ref, gm_ref, sh_ref, sc_ref, w_ref, wg_ref,
                 z_ref, g_ref, h_scr):
    @pl.when(pl.program_id(1) == 0)
    def _():
        x = x_ref[...]
        if use_pos:
            x = _add_pos(x, erow_ref[...], ecol_ref[...])
        h = _norm_mod(x, gm_ref[...], sh_ref[...], sc_ref[...])
        h_scr[...] = h.astype(BF16)
        g_ref[...] = _dg(h_scr[...], wg_ref[...].astype(BF16))

    z_ref[...] = jnp.dot(h_scr[...], w_ref[...], preferred_element_type=F32).astype(BF16)


def _inproj_call(x, erow, ecol, gm, sh, sc, w, wg, use_pos, tm):
    m, d = x.shape
    tn = 1024
    er = tm // GRID_W if use_pos else erow.shape[0]
    row_map = (lambda i, j: (i, 0)) if use_pos else (lambda i, j: (0, 0))
    return pl.pallas_call(
        functools.partial(_inproj_body, use_pos),
        grid=(m // tm, Z_COLS // tn),
        in_specs=[pl.BlockSpec((tm, d), lambda i, j: (i, 0)),
                  pl.BlockSpec((er, d // 2), row_map),
                  pl.BlockSpec((GRID_W, d // 2), lambda i, j: (0, 0)),
                  pl.BlockSpec((1, d), lambda i, j: (0, 0)),
                  pl.BlockSpec((1, d), lambda i, j: (0, 0)),
                  pl.BlockSpec((1, d), lambda i, j: (0, 0)),
                  pl.BlockSpec((d, tn), lambda i, j: (0, j)),
                  pl.BlockSpec((d, GATE_PAD), lambda i, j: (0, 0))],
        out_specs=[pl.BlockSpec((tm, tn), lambda i, j: (i, j)),
                   pl.BlockSpec((tm, GATE_PAD), lambda i, j: (i, 0))],
        out_shape=[jax.ShapeDtypeStruct((m, Z_COLS), BF16),
                   jax.ShapeDtypeStruct((m, GATE_PAD), F32)],
        scratch_shapes=[pltpu.VMEM((tm, d), BF16)],
        compiler_params=_cparams(("arbitrary", "arbitrary"), 48),
        name="inproj",
    )(x, erow, ecol, gm, sh, sc, w, wg)


def _conv3(zc, zp, zn, w, b, first, last):
    tm = zc.shape[0]
    row = lax.broadcasted_iota(I32, zc.shape, 0)
    prev_row = jnp.where(first, 0.0, zp[7:8, :])
    next_row = jnp.where(last, 0.0, zn[0:1, :])
    xm = jnp.where(row == 0, prev_row, pltpu.roll(zc, 1, 0))
    xp = jnp.where(row == tm - 1, next_row, pltpu.roll(zc, tm - 1, 0))
    return xm * w[0:1, :] + zc * w[1:2, :] + xp * w[2:3, :] + b


def _conv_kq_body(zc_ref, zp_ref, zn_ref, w_ref, b_ref, s_ref, o_ref):
    i = pl.program_id(0)
    u = _conv3(zc_ref[...].astype(F32), zp_ref[...].astype(F32), zn_ref[...].astype(F32),
               w_ref[...], b_ref[...], i == 0, i == pl.num_programs(0) - 1)
    o_ref[...] = (_silu(u) * s_ref[...]).astype(BF16)


def _halo_specs(tm, m, cb):
    nb8 = m // 8
    return [pl.BlockSpec((tm, 1024), lambda i: (i, cb)),
            pl.BlockSpec((8, 1024), lambda i: (jnp.maximum(i * (tm // 8) - 1, 0), cb)),
            pl.BlockSpec((8, 1024), lambda i: (jnp.minimum((i + 1) * (tm // 8), nb8 - 1), cb))]


def _conv_kq_call(z, w, b, s, tm):
    m = z.shape[0]
    vec = pl.BlockSpec((1, 1024), lambda i: (0, 0))
    return pl.pallas_call(
        _conv_kq_body,
        grid=(m // tm,),
        in_specs=_halo_specs(tm, m, ZC_KQ) + [pl.BlockSpec((3, 1024), lambda i: (0, 0)), vec, vec],
        out_specs=pl.BlockSpec((tm, 1024), lambda i: (i, 0)),
        out_shape=jax.ShapeDtypeStruct((m, 1024), BF16),
        compiler_params=_cparams(("arbitrary",), 32),
        name="conv_kq",
    )(z, z, z, w, b, s)


def _conv_hy_body(ac_ref, ap_ref, an_ref, bc_ref, bp_ref, bn_ref, cc_ref, cp_ref, cn_ref,
                  w_ref, b_ref, x0_ref, xv_ref):
    i = pl.program_id(0)
    first, last = i == 0, i == pl.num_programs(0) - 1
    w = w_ref[...]
    b = b_ref[...]

    def cv(c, p, n, k):
        return _conv3(c[...].astype(F32), p[...].astype(F32), n[...].astype(F32),
                      w[:, k * 1024:(k + 1) * 1024], b[:, k * 1024:(k + 1) * 1024], first, last)

    x0_ref[...] = cv(ac_ref, ap_ref, an_ref, 0).astype(BF16)
    xv_ref[...] = (cv(bc_ref, bp_ref, bn_ref, 1) * cv(cc_ref, cp_ref, cn_ref, 2)).astype(BF16)


def _conv_hy_call(z, w, b, tm):
    m = z.shape[0]
    out = pl.BlockSpec((tm, 1024), lambda i: (i, 0))
    return pl.pallas_call(
        _conv_hy_body,
        grid=(m // tm,),
        in_specs=(_halo_specs(tm, m, ZC_X0) + _halo_specs(tm, m, ZC_X1) + _halo_specs(tm, m, ZC_HV)
                  + [pl.BlockSpec((3, 3072), lambda i: (0, 0)), pl.BlockSpec((1, 3072), lambda i: (0, 0))]),
        out_specs=[out, out],
        out_shape=[jax.ShapeDtypeStruct((m, 1024), BF16), jax.ShapeDtypeStruct((m, 1024), BF16)],
        compiler_params=_cparams(("arbitrary",), 32),
        name="conv_hy",
    )(z, z, z, z, z, z, z, z, z, w, b)


def _mlstm_body(kqf_ref, vf_ref, gf_ref, kqb_ref, vb_ref, gb_ref, bg_ref, s0_ref, m0_ref,
                hf_ref, hb_ref, sfin_ref, mfin_ref, s_scr, m_scr):
    j = pl.program_id(0)

    @pl.when(j == 0)
    def _():
        s_scr[...] = s0_ref[...]
        m_scr[...] = m0_ref[...]

    r = lax.broadcasted_iota(I32, (CHUNK, CHUNK), 0)
    c = lax.broadcasted_iota(I32, (CHUNK, CHUNK), 1)
    ones_b = jnp.ones((CHUNK, CHUNK), BF16)
    bg = bg_ref[...]

    def lane_bcast(x, h, width=CHUNK):
        return jnp.broadcast_to(x[:, h:h + 1], (x.shape[0], width))

    for d in range(2):
        kq = (kqf_ref, kqb_ref)[d][...]
        v = (vf_ref, vb_ref)[d][...]
        g_all = (gf_ref, gb_ref)[d][...] + bg
        out_ref = (hf_ref, hb_ref)[d]
        tri = (r >= c) if d == 0 else (c >= r)
        tri_b = jnp.where(tri, 1.0, 0.0).astype(BF16)
        gi = g_all if d == 0 else pltpu.roll(g_all, CHUNK - 16, 1)
        gfp = pltpu.roll(g_all, CHUNK - 8 - 16 * d, 1)
        lf = jnp.minimum(gfp, 0.0) - jnp.log(1.0 + jnp.exp(-jnp.abs(gfp)))
        l1 = lf.astype(BF16)
        r1 = lf - l1.astype(F32)
        l2 = r1.astype(BF16)
        l3 = (r1 - l2.astype(F32)).astype(BF16)
        bcum = _dg(tri_b, l1) + _dg(tri_b, l2) + _dg(tri_b, l3)
        gtot = bcum[CHUNK - 1:CHUNK, :] if d == 0 else bcum[0:1, :]
        acol = gtot - bcum + gi
        m_loc = jnp.max(acol, axis=0, keepdims=True)
        m_st = m_scr[d, 0:1, :]
        m_new = jnp.maximum(gtot + m_st, m_loc)
        sp8 = jnp.broadcast_to(jnp.exp(gtot + m_st - m_new), (8, CHUNK))
        wst = jnp.exp(acol - m_new)
        rr = gi - bcum
        cm = rr
        for sh in (1, 2, 4, 8, 16, 32, 64):
            if d == 0:
                cm = jnp.maximum(cm, jnp.where(r >= sh, pltpu.roll(cm, sh, 0), NEG))
            else:
                cm = jnp.maximum(cm, jnp.where(r < CHUNK - sh, pltpu.roll(cm, CHUNK - sh, 0), NEG))
        mt = jnp.maximum(bcum + m_st, bcum + cm)
        c1 = bcum - mt
        rt = jnp.transpose(rr)
        wt = jnp.transpose(wst)
        m8 = jnp.broadcast_to(m_st, (8, CHUNK))
        kts = {}

        for h in range(N_HEADS):
            p, half = divmod(h, 2)
            lm = (c // QK_HEAD) == half
            kp = kq[:, p * 128:(p + 1) * 128]
            qp = kq[:, QK_W + p * 128:QK_W + (p + 1) * 128]
            vaug = jnp.concatenate([v[:, h * 128:(h + 1) * 128], ones_b], axis=1)
            qm = jnp.where(lm, qp, jnp.zeros_like(qp))
            c1b = lane_bcast(c1, h)
            pm = jnp.exp(jnp.where(tri, c1b + rt[h:h + 1, :], NEG))
            s = (_dg(qm, kp, _NT) * pm).astype(BF16)
            m_in = jnp.tile(lane_bcast(m8, h), (CHUNK // 8, 1))
            qs = (qm.astype(F32) * jnp.exp(c1b + m_in)).astype(BF16)
            st = s_scr[d * N_HEADS + h]
            tot = _dg(jnp.concatenate([s, qs], axis=1), jnp.concatenate([vaug, st.astype(BF16)], axis=0))
            den = jnp.maximum(jnp.abs(tot[:, 128:]), jnp.exp(-lane_bcast(mt, h)))
            out_ref[:, h * 128:(h + 1) * 128] = (tot[:, :128] / den).astype(BF16)
            if p not in kts:
                kts[p] = jnp.transpose(kp.astype(F32))
            kw = jnp.where((r // QK_HEAD) == half, kts[p] * wt[h:h + 1, :], 0.0).astype(BF16)
            spb = jnp.tile(lane_bcast(sp8, h, 256), (CHUNK // 8, 1))
            s_scr[d * N_HEADS + h] = spb * st + _dg(kw, vaug)
        m_scr[d, 0:1, :] = m_new

    @pl.when(j == pl.num_programs(0) - 1)
    def _():
        sfin_ref[...] = s_scr[...]
        mfin_ref[...] = m_scr[...]


def _mlstm_call(kq, z, gates, bg, s0, m0):
    m = kq.shape[0]
    nc = m // CHUNK
    fwd = lambda cb: (lambda j: (j, cb))
    bwd = lambda cb: (lambda j: (nc - 1 - j, cb))
    st_spec = pl.BlockSpec((2 * N_HEADS, CHUNK, 256), lambda j: (0, 0, 0))
    m_spec = pl.BlockSpec((2, 8, 128), lambda j: (0, 0, 0))
    return pl.pallas_call(
        _mlstm_body,
        grid=(nc,),
        in_specs=[pl.BlockSpec((CHUNK, 1024), fwd(0)), pl.BlockSpec((CHUNK, 1024), fwd(ZC_V)),
                  pl.BlockSpec((CHUNK, GATE_PAD), fwd(0)),
                  pl.BlockSpec((CHUNK, 1024), bwd(0)), pl.BlockSpec((CHUNK, 1024), bwd(ZC_V)),
                  pl.BlockSpec((CHUNK, GATE_PAD), bwd(0)),
                  pl.BlockSpec((1, GATE_PAD), lambda j: (0, 0)), st_spec, m_spec],
        out_specs=[pl.BlockSpec((CHUNK, 1024), fwd(0)), pl.BlockSpec((CHUNK, 1024), bwd(0)), st_spec, m_spec],
        out_shape=[jax.ShapeDtypeStruct((m, 1024), BF16), jax.ShapeDtypeStruct((m, 1024), BF16),
                   jax.ShapeDtypeStruct((2 * N_HEADS, CHUNK, 256), F32),
                   jax.ShapeDtypeStruct((2, 8, 128), F32)],
        scratch_shapes=[pltpu.VMEM((2 * N_HEADS, CHUNK, 256), F32), pltpu.VMEM((2, 8, 128), F32)],
        compiler_params=_cparams(("arbitrary",), 32),
        name="mlstm",
    )(kq, z, gates, kq, z, gates, bg, s0, m0)


def _filt_body(seq_len, ft_ref, w1_ref, b1_ref, f1_ref, w2_ref, b2_ref, f2_ref, w3_ref, b3_ref, rt_ref,
               kf_ref, l1_ref):
    i = pl.program_id(0)
    tn = ft_ref.shape[1]
    hp = tn // 2
    reps = tn // 128
    col = lambda ref: jnp.tile(ref[...], (1, reps))
    h1 = jnp.sin(col(f1_ref) * (_dot3(w1_ref[...], ft_ref[...]) + col(b1_ref)))
    h2 = jnp.sin(col(f2_ref) * (_dot3(w2_ref[...], h1) + col(b2_ref)))
    r = lax.broadcasted_iota(I32, (hp, HY_W), 0)
    n_fwd = (8 * i + (r & 7)) * FFT_N2 + (r >> 3)
    rates = rt_ref[...]
    halves = []
    l1 = jnp.zeros((1, HY_W), F32)
    for hx in range(2):
        h = (_dot3(h2[:, hx * hp:(hx + 1) * hp], w3_ref[:, hx * HY_W:(hx + 1) * HY_W], _TN)
             + b3_ref[:, hx * HY_W:(hx + 1) * HY_W])
        n = n_fwd + hx * seq_len
        t01 = jnp.where(n <= seq_len, n, 2 * seq_len - n).astype(F32) / float(max(seq_len - 1, 1))
        h = jnp.where(n == seq_len, 0.0, h * jnp.exp(-t01 * rates))
        l1 = l1 + jnp.sum(jnp.abs(h), axis=0, keepdims=True)
        halves.append(h)
    word = _pack_pair(halves[0], halves[1])
    for b in range(FFT_N2):
        kf_ref[:, b * HY_W:(b + 1) * HY_W] = word[8 * b:8 * b + 8, :]

    @pl.when(i == 0)
    def _():
        l1_ref[...] = jnp.zeros_like(l1_ref)

    l1_ref[...] += l1


def _filt_call(feats_t, w1t, b1, f1, w2t, b2, f2, w3, b3, rates):
    n = feats_t.shape[1]
    seq_len = n // 2
    tn = 2 * 8 * FFT_N2
    c64 = lambda shape: pl.BlockSpec(shape, lambda i: (0, 0))
    return pl.pallas_call(
        functools.partial(_filt_body, seq_len),
        grid=(n // tn,),
        in_specs=[pl.BlockSpec((64, tn), lambda i: (0, i)),
                  c64((64, 64)), c64((64, 128)), c64((64, 128)), c64((64, 64)), c64((64, 128)), c64((64, 128)),
                  c64((64, 2 * HY_W)), c64((1, 2 * HY_W)), c64((1, HY_W))],
        out_specs=[pl.BlockSpec((8, FFT_N2 * HY_W), lambda i: (i, 0)), pl.BlockSpec((1, HY_W), lambda i: (0, 0))],
        out_shape=[jax.ShapeDtypeStruct((seq_len // FFT_N2, FFT_N2 * HY_W), U32),
                   jax.ShapeDtypeStruct((1, HY_W), F32)],
        compiler_params=_cparams(("arbitrary",), 48),
        name="filt",
    )(feats_t, w1t, b1, f1, w2t, b2, f2, w3, b3, rates)


def _fft_consts(n1_rows):
    n = n1_rows * FFT_N2
    kv = n1_rows // 2 + 1
    k1 = np.arange(FFT_KP, dtype=np.float64)
    valid = (k1 < kv).astype(np.float64)
    n1 = np.arange(n1_rows, dtype=np.float64)
    th1 = 2.0 * np.pi * np.outer(k1, n1) / n1_rows
    f1 = np.concatenate([np.cos(th1) * valid[:, None], -np.sin(th1) * valid[:, None]], axis=0)
    n2 = np.arange(FFT_N2, dtype=np.float64)
    tht = 2.0 * np.pi * np.outer(k1, n2) / n
    rep = lambda a: jnp.broadcast_to(jnp.asarray(a, F32)[:, :, None], (FFT_KP, FFT_N2, 128))
    twr = rep(np.cos(tht) * valid[:, None])
    twi = rep(-np.sin(tht) * valid[:, None])
    th2 = 2.0 * np.pi * np.outer(n2, n2) / FFT_N2
    cs, sn = np.cos(th2), np.sin(th2)
    f2p = np.block([[cs, sn], [-sn, cs]])
    f2pc = np.block([[cs, -sn], [sn, cs]])
    wk = np.where((k1 == 0) | (k1 == kv - 1), 1.0, 2.0) * valid / n
    half = n1_rows // 2
    thi = 2.0 * np.pi * np.outer(n1[:half], k1) / n1_rows
    gc = np.cos(thi) * wk[None, :]
    gs = np.sin(thi) * wk[None, :]
    as_bf = lambda a: jnp.asarray(a, F32).astype(BF16)
    return dict(f1=as_bf(f1), twr=twr, twi=twi,
                f2p=as_bf(f2p), f2pc=as_bf(f2pc), gc=as_bf(gc), gs=as_bf(gs))


def _fft1_body(f_ref, x_ref, ar_ref, ai_ref):
    o = _dg(f_ref[...], x_ref[...])
    ar_ref[...] = o[:FFT_KP].astype(BF16)
    ai_ref[...] = o[FFT_KP:].astype(BF16)


def _fft1_packed_body(f_ref, x_ref, ar_ref, ai_ref):
    k = x_ref.shape[0]
    hi, lo = _unpack_pair(x_ref[...])
    o = _dg(f_ref[:, 0:k], hi.astype(BF16)) + _dg(f_ref[:, k:], lo.astype(BF16))
    ar_ref[...] = o[:FFT_KP].astype(BF16)
    ai_ref[...] = o[FFT_KP:].astype(BF16)


def _fft1_call(f1, x2d, cb, packed=False):
    k, cols = x2d.shape
    f1 = f1[:, :2 * k] if packed else f1[:, :k]
    out = pl.BlockSpec((FFT_KP, cb), lambda i: (0, i))
    sh = jax.ShapeDtypeStruct((FFT_KP, cols), BF16)
    return pl.pallas_call(
        _fft1_packed_body if packed else _fft1_body,
        grid=(cols // cb,),
        in_specs=[pl.BlockSpec(f1.shape, lambda i: (0, 0)), pl.BlockSpec((k, cb), lambda i: (0, i))],
        out_specs=[out, out],
        out_shape=[sh, sh],
        compiler_params=_cparams(("arbitrary",), 32),
        name="fft1",
    )(f1, x2d)


def _twiddled(ar_ref, ai_ref, twr_ref, twi_ref, reps):
    a_r = ar_ref[...].astype(F32)
    a_i = ai_ref[...].astype(F32)
    tr = jnp.tile(twr_ref[...], (1, reps))
    ti = jnp.tile(twi_ref[...], (1, reps))
    st = jnp.concatenate([a_r * tr - a_i * ti, a_r * ti + a_i * tr], axis=0).astype(BF16)
    return st, tr, ti


FFT2_ROWS = 2


def _fft2_filt_body(kv, ar_ref, ai_ref, twr_ref, twi_ref, f2p_ref, k_ref):
    for j in range(FFT2_ROWS):
        k1 = pl.program_id(0) * FFT2_ROWS + j

        @pl.when(k1 < kv)
        def _():
            st, _, _ = _twiddled(ar_ref.at[j], ai_ref.at[j], twr_ref.at[j], twi_ref.at[j], ar_ref.shape[-1] // 128)
            k_ref[j] = _dg(f2p_ref[...], st).astype(BF16)

        @pl.when(k1 >= kv)
        def _():
            k_ref[j] = jnp.zeros(k_ref.shape[1:], BF16)


def _fft2_conv_body(kv, ar_ref, ai_ref, twr_ref, twi_ref, k_ref, f2p_ref, f2pc_ref, br_ref, bi_ref):
    for j in range(FFT2_ROWS):
        k1 = pl.program_id(0) * FFT2_ROWS + j

        @pl.when(k1 < kv)
        def _():
            st, tr, ti = _twiddled(ar_ref.at[j], ai_ref.at[j], twr_ref.at[j], twi_ref.at[j], ar_ref.shape[-1] // 128)
            x = _dg(f2p_ref[...], st)
            xr, xi = x[:FFT_N2], x[FFT_N2:]
            kr = k_ref[j, :FFT_N2, :].astype(F32)
            ki = k_ref[j, FFT_N2:, :].astype(F32)
            sy = jnp.concatenate([xr * kr - xi * ki, xr * ki + xi * kr], axis=0).astype(BF16)
            b = _dg(f2pc_ref[...], sy)
            b_r, b_i = b[:FFT_N2], b[FFT_N2:]
            br_ref[j] = (b_r * tr + b_i * ti).astype(BF16)
            bi_ref[j] = (b_i * tr - b_r * ti).astype(BF16)

        @pl.when(k1 >= kv)
        def _():
            br_ref[j] = jnp.zeros(br_ref.shape[1:], BF16)
            bi_ref[j] = jnp.zeros(bi_ref.shape[1:], BF16)


def _fft2_specs(ch, kv):
    src = lambda i: (jnp.minimum(i, (kv - 1) // FFT2_ROWS), 0, 0)
    blk = pl.BlockSpec((FFT2_ROWS, FFT_N2, ch), src)
    tw = pl.BlockSpec((FFT2_ROWS, FFT_N2, 128), src)
    mat = pl.BlockSpec((2 * FFT_N2, 2 * FFT_N2), lambda i: (0, 0))
    return blk, tw, mat, src


def _fft2_filt_call(ar, ai, cst, kv):
    ch = ar.shape[-1]
    blk, tw, mat, _ = _fft2_specs(ch, kv)
    return pl.pallas_call(
        functools.partial(_fft2_filt_body, kv),
        grid=(FFT_KP // FFT2_ROWS,),
        in_specs=[blk, blk, tw, tw, mat],
        out_specs=pl.BlockSpec((FFT2_ROWS, 2 * FFT_N2, ch), lambda i: (i, 0, 0)),
        out_shape=jax.ShapeDtypeStruct((FFT_KP, 2 * FFT_N2, ch), BF16),
        compiler_params=_cparams(("arbitrary",), 32),
        name="fft2_filt",
    )(ar, ai, cst["twr"], cst["twi"], cst["f2p"])


def _fft2_conv_call(ar, ai, khat, cst, kv):
    ch = ar.shape[-1]
    blk, tw, mat, src = _fft2_specs(ch, kv)
    sh = jax.ShapeDtypeStruct((FFT_KP, FFT_N2, ch), BF16)
    out = pl.BlockSpec((FFT2_ROWS, FFT_N2, ch), lambda i: (i, 0, 0))
    return pl.pallas_call(
        functools.partial(_fft2_conv_body, kv),
        grid=(FFT_KP // FFT2_ROWS,),
        in_specs=[blk, blk, tw, tw, pl.BlockSpec((FFT2_ROWS, 2 * FFT_N2, ch), src), mat, mat],
        out_specs=[out, out],
        out_shape=[sh, sh],
        compiler_params=_cparams(("arbitrary",), 32),
        name="fft2_conv",
    )(ar, ai, cst["twr"], cst["twi"], khat, cst["f2p"], cst["f2pc"])


def _ifft1_body(gc_ref, gs_ref, br_ref, bi_ref, il_ref, o_ref):
    y = _dg(gc_ref[...], br_ref[...]) - _dg(gs_ref[...], bi_ref[...])
    o_ref[...] = (y * il_ref[...]).astype(BF16)


def _ifft1_call(cst, br2d, bi2d, il_t, rows, cb):
    cols = br2d.shape[1]
    g = pl.BlockSpec((rows, FFT_KP), lambda i: (0, 0))
    kb = pl.BlockSpec((FFT_KP, cb), lambda i: (0, i))
    xb = pl.BlockSpec((rows, cb), lambda i: (0, i))
    vb = pl.BlockSpec((1, cb), lambda i: (0, 0))
    return pl.pallas_call(
        _ifft1_body,
        grid=(cols // cb,),
        in_specs=[g, g, kb, kb, vb],
        out_specs=xb,
        out_shape=jax.ShapeDtypeStruct((rows, cols), BF16),
        compiler_params=_cparams(("arbitrary",), 32),
        name="ifft1",
    )(cst["gc"], cst["gs"], br2d, bi2d, il_t)


def _outproj_body(hf_ref, hb_ref, zo_ref, yc_ref, x0_ref, xv_ref, ds_ref, x_ref, erow_ref, ecol_ref, gh_ref,
                  wa_ref, wb_ref, g1_ref, gf_ref, sh_ref, sc_ref, wr_ref, w1s_ref, w3s_ref, w2s_ref,
                  x1_ref, h2_ref, s_ref, shared_ref):
    hs = hf_ref[...].astype(F32) + hb_ref[...].astype(F32)
    gh = gh_ref[...]
    parts = []
    for h in range(N_HEADS):
        hh = hs[:, h * 128:(h + 1) * 128]
        ms = jnp.mean(hh * hh, axis=-1, keepdims=True)
        parts.append(hh * lax.rsqrt(ms + EPS) * gh[:, h * 128:(h + 1) * 128])
    ym = jnp.concatenate(parts, axis=-1) * _sigmoid(zo_ref[...].astype(F32))
    yh = x0_ref[...].astype(F32) * (yc_ref[...].astype(F32) + ds_ref[...] * xv_ref[...].astype(F32))
    y = _dg(ym.astype(BF16), wa_ref[...]) + _dg(yh.astype(BF16), wb_ref[...])
    rp = x_ref.shape[0] // GRID_W
    erow8 = erow_ref[...]
    erow = erow8[0:rp, :]
    for q in range(1, 8 // rp):
        erow = jnp.where(pl.program_id(0) % (8 // rp) == q, erow8[q * rp:(q + 1) * rp, :], erow)
    x1 = _add_pos(x_ref[...], erow, ecol_ref[...]) + g1_ref[...] * y
    x1_ref[...] = x1
    h2 = _norm_mod(x1, gf_ref[...], sh_ref[...], sc_ref[...])
    half = h2.shape[1] // 2
    _store_slabs(h2_ref, _pack_pair(h2[:, :half], h2[:, half:]))
    s_ref[...] = _sigmoid(_dot3(wr_ref[...], h2, _NT))
    h2b = h2.astype(BF16)
    a = (_silu(_dg(h2b, w1s_ref[...])) * _dg(h2b, w3s_ref[...])).astype(BF16)
    shared_ref[...] = _dg(a, w2s_ref[...]).astype(BF16)


def _outproj_call(hf, hb, z, yc, x0c, xv, ds, x, erow, ecol, gh, wa, wb, g1, gf, sh2, sc2, wrt, w1s, w3s, w2s, tm):
    m, d = x.shape
    dsh = w1s.shape[1]
    row = lambda cb: pl.BlockSpec((tm, 1024), lambda i: (i, cb))
    vec = lambda n: pl.BlockSpec((1, n), lambda i: (0, 0))
    full = pl.BlockSpec((tm, d), lambda i: (i, 0))
    const = lambda r, c: pl.BlockSpec((r, c), lambda i: (0, 0))
    return pl.pallas_call(
        _outproj_body,
        grid=(m // tm,),
        in_specs=[row(0), row(0), row(ZC_O), row(0), row(0), row(0), vec(HY_W), full,
                  pl.BlockSpec((8, d // 2), lambda i: (i * (tm // GRID_W) // 8, 0)),
                  const(GRID_W, d // 2),
                  vec(MV_W),
                  const(MV_W, d), const(HY_W, d),
                  vec(d), vec(d), vec(d), vec(d),
                  const(N_EXPERTS, d), const(d, dsh), const(d, dsh), const(dsh, d)],
        out_specs=[full, pl.BlockSpec((tm * SLAB, SLAB_W), lambda i: (i, 0)),
                   pl.BlockSpec((N_EXPERTS, tm), lambda i: (0, i)), full],
        out_shape=[jax.ShapeDtypeStruct((m, d), F32), jax.ShapeDtypeStruct((m * SLAB, SLAB_W), U32),
                   jax.ShapeDtypeStruct((N_EXPERTS, m), F32), jax.ShapeDtypeStruct((m, d), BF16)],
        compiler_params=_cparams(("arbitrary",), 56),
        name="outproj",
    )(hf, hb, z, yc, x0c, xv, ds, x, erow, ecol, gh, wa, wb, g1, gf, sh2, sc2, wrt, w1s, w3s, w2s)


def _first_max(x, idx, sentinel):
    m = jnp.max(x, axis=0, keepdims=True)
    return m, jnp.min(jnp.where(x == m, idx, sentinel), axis=0, keepdims=True)


def _route_body(s_ref, b_ref, e_ref, w_ref, r_ref, cnt_ref, u_scr, run_scr):
    i = pl.program_id(0)
    tt = s_ref.shape[1]

    @pl.when(i == 0)
    def _():
        rr = lax.broadcasted_iota(I32, (tt, tt), 0)
        cc = lax.broadcasted_iota(I32, (tt, tt), 1)
        u_scr[...] = jnp.where(rr < cc, 1.0, 0.0).astype(BF16)
        run_scr[...] = jnp.zeros_like(run_scr)

    s = s_ref[...]
    sel = s + b_ref[...][:, 0:1]
    sub8 = lax.broadcasted_iota(I32, (E_PER_GROUP, tt), 0).astype(F32)
    gs = jnp.zeros((N_GROUPS, tt), F32)
    for g in range(N_GROUPS):
        grp = sel[g * E_PER_GROUP:(g + 1) * E_PER_GROUP, :]
        m1, i1 = _first_max(grp, sub8, float(E_PER_GROUP))
        m2 = jnp.max(jnp.where(sub8 == i1, -jnp.inf, grp), axis=0, keepdims=True)
        gs = jnp.where(sub8 == g, m1 + m2, gs)
    gmask = jnp.zeros((N_GROUPS, tt), F32)
    for _ in range(TOPK_GROUPS):
        _, ig = _first_max(gs, sub8, float(N_GROUPS))
        hit = sub8 == ig
        gmask = jnp.where(hit, 1.0, gmask)
        gs = jnp.where(hit, -jnp.inf, gs)
    masked = jnp.concatenate(
        [jnp.where(jnp.broadcast_to(gmask[g:g + 1, :], (E_PER_GROUP, tt)) > 0.5,
                   sel[g * E_PER_GROUP:(g + 1) * E_PER_GROUP, :], -jnp.inf) for g in range(N_GROUPS)], axis=0)
    sub64 = lax.broadcasted_iota(I32, (N_EXPERTS, tt), 0).astype(F32)
    oh = jnp.zeros((N_EXPERTS, tt), F32)
    eks, wks = [], []
    for _ in range(TOP_K):
        _, ie = _first_max(masked, sub64, float(N_EXPERTS))
        hit = sub64 == ie
        wks.append(jnp.sum(jnp.where(hit, s, 0.0), axis=0, keepdims=True))
        eks.append(ie)
        masked = jnp.where(hit, -jnp.inf, masked)
        oh = jnp.where(hit, 1.0, oh)
    wsum = wks[0]
    for k in range(1, TOP_K):
        wsum = wsum + wks[k]
    run = run_scr[...]
    rank_t = _dg(oh.astype(BF16), u_scr[...]) + jnp.tile(run, (1, tt // 128))
    for k in range(TOP_K):
        e_ref[k:k + 1, :] = eks[k].astype(I32)
        w_ref[k:k + 1, :] = wks[k] / wsum * ROUTE_SCALE
        r_ref[k:k + 1, :] = jnp.sum(jnp.where(sub64 == eks[k], rank_t, 0.0), axis=0, keepdims=True).astype(I32)
    run_new = run + jnp.sum(oh, axis=1, keepdims=True)
    run_scr[...] = run_new
    cnt_ref[...] = run_new.astype(I32)


def _route_call(s_t, b_col, tt):
    m = s_t.shape[1]
    out = pl.BlockSpec((TOP_K, tt), lambda i: (0, i))
    return pl.pallas_call(
        _route_body,
        grid=(m // tt,),
        in_specs=[pl.BlockSpec((N_EXPERTS, tt), lambda i: (0, i)),
                  pl.BlockSpec((N_EXPERTS, 128), lambda i: (0, 0))],
        out_specs=[out, out, out, pl.BlockSpec((N_EXPERTS, 128), lambda i: (0, 0))],
        out_shape=[jax.ShapeDtypeStruct((TOP_K, m), I32), jax.ShapeDtypeStruct((TOP_K, m), F32),
                   jax.ShapeDtypeStruct((TOP_K, m), I32), jax.ShapeDtypeStruct((N_EXPERTS, 128), I32)],
        scratch_shapes=[pltpu.VMEM((tt, tt), BF16), pltpu.VMEM((N_EXPERTS, 128), F32)],
        compiler_params=_cparams(("arbitrary",), 32),
        name="route",
    )(s_t, b_col)


def _posk_body(pst_ref, e_ref, r_ref, p_ref):
    e = e_ref[...]
    acc = r_ref[...]
    for x in range(N_EXPERTS):
        acc = acc + jnp.where(e == x, pst_ref[x], 0)
    p_ref[...] = acc


def _posk_call(pstart, eidx, rank):
    k, m = eidx.shape
    tt = min(m, 2048)
    blk = pl.BlockSpec((k, tt), lambda i, pst: (0, i))
    return pl.pallas_call(
        _posk_body,
        grid_spec=pltpu.PrefetchScalarGridSpec(num_scalar_prefetch=1, grid=(m // tt,),
                                               in_specs=[blk, blk], out_specs=blk),
        out_shape=jax.ShapeDtypeStruct((k, m), I32),
        compiler_params=_cparams(("arbitrary",), 32),
        name="posk",
    )(pstart, eidx, rank)


def _slab(ref, r):
    return ref.at[pl.ds(pl.multiple_of(r * SLAB, SLAB), SLAB), :]


def _ffn_packed(x_ref, rows, w1, w3, w2, base=0):
    half = w1.shape[0] // 2
    xa, xb = _unpack_pair(_load_slabs(x_ref, rows, base))
    xa = xa.astype(BF16)
    xb = xb.astype(BF16)
    h1 = _dg(xa, w1[0:half, :]) + _dg(xb, w1[half:, :])
    h3 = _dg(xa, w3[0:half, :]) + _dg(xb, w3[half:, :])
    return _dg((_silu(h1) * h3).astype(BF16), w2[...])


def _dispatch_body(cnt_ref, pst_ref, pcn_ref, h2_ref, pos_ref, xs_ref, zrow, sem):
    i = pl.program_id(0)
    td = h2_ref.shape[0] // SLAB

    def row_copy(t, dst):
        return pltpu.make_async_copy(_slab(h2_ref, t), _slab(xs_ref, dst), sem)

    def issue(t, carry):
        for k in range(TOP_K):
            row_copy(t, pos_ref[k, t]).start(priority=k % 2)
        return carry

    lax.fori_loop(0, td, issue, 0)

    def drain(t, carry):
        for k in range(TOP_K):
            row_copy(0, 0).wait()
        return carry

    lax.fori_loop(0, td, drain, 0)

    @pl.when(i == pl.num_programs(0) - 1)
    def _():
        zrow[...] = jnp.zeros_like(zrow)

        def zero_copy(dst):
            return pltpu.make_async_copy(zrow, _slab(xs_ref, dst), sem)

        def per_expert(e, carry):
            base = pst_ref[e]
            lax.fori_loop(cnt_ref[e], pcn_ref[e], lambda rr, cc: (zero_copy(base + rr).start(), cc)[1], 0)
            lax.fori_loop(cnt_ref[e], pcn_ref[e], lambda rr, cc: (zero_copy(0).wait(), cc)[1], 0)
            return carry

        lax.fori_loop(0, N_EXPERTS, per_expert, 0)


def _dispatch_call(cnt, pstart, pcnt, h2s, pos, rows, td):
    m = h2s.shape[0] // SLAB
    return pl.pallas_call(
        _dispatch_body,
        grid_spec=pltpu.PrefetchScalarGridSpec(
            num_scalar_prefetch=3, grid=(m // td,),
            in_specs=[pl.BlockSpec((td * SLAB, SLAB_W), lambda i, *_: (i, 0)),
                      pl.BlockSpec((TOP_K, td), lambda i, *_: (0, i), memory_space=pltpu.SMEM)],
            out_specs=pl.BlockSpec(memory_space=pl.ANY),
            scratch_shapes=[pltpu.VMEM((SLAB, SLAB_W), U32), pltpu.SemaphoreType.DMA(())]),
        out_shape=jax.ShapeDtypeStruct((rows * SLAB, SLAB_W), U32),
        compiler_params=_cparams(("arbitrary",), 32),
        name="dispatch",
    )(cnt, pstart, pcnt, h2s, pos)


def _moe_body(te_ref, nu_ref, nxt_ref, par_ref, x_ref, w1_ref, w3_ref, w2_ref, y_ref,
              f1, f3, f2, w1b, w3b, w2b, sems):
    def fetch(ex, s):
        return (pltpu.make_async_copy(w1_ref.at[ex], f1.at[s], sems.at[s]),
                pltpu.make_async_copy(w3_ref.at[ex], f3.at[s], sems.at[s]),
                pltpu.make_async_copy(w2_ref.at[ex], f2.at[s], sems.at[s]))

    for sub in range(MOE_TILES_PER_STEP):
        ti = pl.program_id(0) * MOE_TILES_PER_STEP + sub
        used = ti < nu_ref[0]
        e = te_ref[ti]
        first = jnp.logical_or(ti == 0, e != te_ref[jnp.maximum(ti - 1, 0)])
        slot = par_ref[e]

        @pl.when(jnp.logical_and(used, ti == 0))
        def _():
            for c in fetch(e, slot):
                c.start()

        @pl.when(jnp.logical_and(used, first))
        def _():
            for c in fetch(e, slot):
                c.wait()
            nx = nxt_ref[e]

            @pl.when(nx < N_EXPERTS)
            def _():
                for c in fetch(nx, 1 - slot):
                    c.start()

            w1b[...] = f1[slot].astype(BF16)
            w3b[...] = f3[slot].astype(BF16)
            w2b[...] = f2[slot].astype(BF16)

        @pl.when(used)
        def _():
            half = w1b.shape[0] // 2
            base = sub * MOE_ROWS * SLAB
            y = _ffn_packed(x_ref, MOE_ROWS, w1b, w3b, w2b, base)
            _store_slabs(y_ref, _pack_pair(y[:, :half], y[:, half:]), base)


def _moe_call(tile_e, n_used, nxt, par, xs, w1, w3, w2):
    rows = xs.shape[0] // SLAB
    step_rows = MOE_ROWS * MOE_TILES_PER_STEP
    nt = rows // step_rows
    d, de = w1.shape[-2:]
    rmap = lambda i, te, nu, *_: (jnp.minimum(i, (nu[0] - 1) // MOE_TILES_PER_STEP), 0)
    hbm = pl.BlockSpec(memory_space=pl.ANY)
    return pl.pallas_call(
        _moe_body,
        grid_spec=pltpu.PrefetchScalarGridSpec(
            num_scalar_prefetch=4, grid=(nt,),
            in_specs=[pl.BlockSpec((step_rows * SLAB, SLAB_W), rmap), hbm, hbm, hbm],
            out_specs=pl.BlockSpec((step_rows * SLAB, SLAB_W), rmap),
            scratch_shapes=[pltpu.VMEM((2, d, de), F32), pltpu.VMEM((2, d, de), F32), pltpu.VMEM((2, de, d), F32),
                            pltpu.VMEM((d, de), BF16), pltpu.VMEM((d, de), BF16), pltpu.VMEM((de, d), BF16),
                            pltpu.SemaphoreType.DMA((2,))]),
        out_shape=jax.ShapeDtypeStruct((rows * SLAB, SLAB_W), U32),
        compiler_params=_cparams(("arbitrary",), 56),
        name="moe",
    )(tile_e, n_used, nxt, par, xs, w1, w3, w2)


def _final_body(x1_ref, sh_ref, pos_ref, posn_ref, wt_ref, ys_ref, g2_ref, gn_ref, o_ref, ybuf, sems):
    i = pl.program_id(0)
    n = pl.num_programs(0)
    tf, d = x1_ref.shape
    half = d // 2
    slot = i % 2
    slot_slabs = TOP_K * tf

    def row_copy(p_ref, s, k, t):
        return pltpu.make_async_copy(_slab(ys_ref, p_ref[k, t]), _slab(ybuf, s * slot_slabs + k * tf + t),
                                     sems.at[s])

    def issue(p_ref, s):
        def body(t, carry):
            for k in range(TOP_K):
                row_copy(p_ref, s, k, t).start(priority=k % 2)
            return carry

        lax.fori_loop(0, tf, body, 0)

    @pl.when(i == 0)
    def _():
        issue(pos_ref, 0)

    @pl.when(i + 1 < n)
    def _():
        issue(posn_ref, 1 - slot)

    def drain(t, carry):
        for k in range(TOP_K):
            pltpu.make_async_copy(_slab(ys_ref, 0), _slab(ybuf, 0), sems.at[slot]).wait()
        return carry

    lax.fori_loop(0, tf, drain, 0)

    wt = jnp.concatenate([jnp.transpose(jnp.concatenate(
        [wt_ref[:, c * 128:(c + 1) * 128], jnp.zeros((128 - TOP_K, 128), F32)], axis=0)) for c in range(tf // 128)],
        axis=0)
    shared = sh_ref[...].astype(F32)
    acc_a = shared[:, :half]
    acc_b = shared[:, half:]
    base = slot * slot_slabs * SLAB
    for k in range(TOP_K):
        ya, yb = _unpack_pair(_load_slabs(ybuf, tf, base=base + k * tf * SLAB))
        acc_a = acc_a + wt[:, k:k + 1] * ya
        acc_b = acc_b + wt[:, k:k + 1] * yb
    xo = x1_ref[...] + g2_ref[...] * jnp.concatenate([acc_a, acc_b], axis=1)
    ms = jnp.mean(xo * xo, axis=-1, keepdims=True)
    o_ref[...] = xo * lax.rsqrt(ms + EPS) * gn_ref[...]


def _final_call(x1, sh, pos, wts, ys, g2, gn, tf):
    m, d = x1.shape
    nt = m // tf
    full = pl.BlockSpec((tf, d), lambda i: (i, 0))
    vec = pl.BlockSpec((1, d), lambda i: (0, 0))
    return pl.pallas_call(
        _final_body,
        grid=(nt,),
        in_specs=[full, full,
                  pl.BlockSpec((TOP_K, tf), lambda i: (0, i), memory_space=pltpu.SMEM),
                  pl.BlockSpec((TOP_K, tf), lambda i: (0, jnp.minimum(i + 1, nt - 1)), memory_space=pltpu.SMEM),
                  pl.BlockSpec((TOP_K, tf), lambda i: (0, i)),
                  pl.BlockSpec(memory_space=pl.ANY), vec, vec],
        out_specs=full,
        out_shape=jax.ShapeDtypeStruct((m, d), F32),
        scratch_shapes=[pltpu.VMEM((2 * TOP_K * tf * SLAB, SLAB_W), U32), pltpu.SemaphoreType.DMA((2,))],
        compiler_params=_cparams(("arbitrary",), 48),
        name="final",
    )(x1, sh, pos, pos, wts, ys, g2, gn)


def _pos_tables(n_tokens):
    rows = n_tokens // GRID_W
    quarter = D_MODEL // 4
    omega = 1.0 / (10000.0 ** (jnp.arange(quarter, dtype=F32) / quarter))

    def emb1d(pos):
        ang = pos[:, None] * omega[None]
        return jnp.concatenate([jnp.sin(ang), jnp.cos(ang)], axis=-1)

    return emb1d(jnp.arange(rows, dtype=F32)), emb1d(jnp.arange(GRID_W, dtype=F32))


def _filter_feats(L):
    n1h = L // FFT_N2
    i_, h_, b_, a_ = jnp.meshgrid(jnp.arange(n1h // 8, dtype=I32), jnp.arange(2, dtype=I32),
                                  jnp.arange(FFT_N2, dtype=I32), jnp.arange(8, dtype=I32), indexing="ij")
    n = ((8 * i_ + a_ + h_ * n1h) * FFT_N2 + b_).reshape(-1)
    t = jnp.where(n <= L, n, 2 * L - n).astype(F32)
    t01 = t / max(L - 1, 1)
    w = 2.0 * math.pi * t / L
    bands = jnp.linspace(1e-4, FILT_BANDS - 1, FILT_BANDS, dtype=F32)
    feats = jnp.concatenate([t01[None, :], jnp.cos(bands[:, None] * w[None, :]), -jnp.sin(bands[:, None] * w[None, :]),
                             jnp.zeros((64 - 33, 2 * L), F32)], axis=0)
    return feats


def _pad_rows(a, rows):
    return jnp.concatenate([a, jnp.zeros((rows - a.shape[0],) + a.shape[1:], a.dtype)], axis=0)


def _layer(x, c, ctx, c_ctx, w_ada, b_ada, g_mix, g_ffn, w_in, b_gates, conv_k_w, conv_k_b,
           conv_q_w, conv_q_b, g_head, conv_hy_w, conv_hy_b, filt_w1, filt_b1, filt_freq1,
           filt_w2, filt_b2, filt_freq2, filt_w3, filt_b3, hy_dskip, w_out, w_router, b_router,
           w1_e, w3_e, w2_e, w1_s, w3_s, w2_s, g_final):
    L, d = x.shape
    lc = ctx.shape[0]
    row = lambda v: v.reshape(1, -1)

    cc = _pad_rows(jnp.stack([c, c_ctx], axis=0), 8)
    mods = _mod_call(cc, w_ada, row(b_ada))
    sh1, sc1, g1, sh2, sc2, g2 = [mods[0:1, k * d:(k + 1) * d] for k in range(6)]
    csh1, csc1 = mods[1:2, 0:d], mods[1:2, d:2 * d]

    w_r, w_g = _wprep_call(jnp.transpose(w_in[0]))
    bg = jnp.concatenate([b_gates, jnp.zeros((GATE_PAD - 4 * N_HEADS,), F32)]).reshape(1, GATE_PAD)
    e_row, e_col = _pos_tables(L)
    conv_w = jnp.concatenate([conv_k_w, conv_q_w], axis=1)
    conv_b = jnp.concatenate([conv_k_b, conv_q_b]).reshape(1, -1)
    conv_s = jnp.concatenate([jnp.ones((QK_W,), F32), jnp.full((QK_W,), QK_HEAD ** -0.5, F32)]).reshape(1, -1)

    z_c, gt_c = _inproj_call(ctx, jnp.zeros((8, d // 2), F32), e_col, row(g_mix), csh1, csc1, w_r, w_g,
                             use_pos=False, tm=min(lc, 256))
    kq_c = _conv_kq_call(z_c, conv_w, conv_b, conv_s, tm=min(lc, 256))
    s0 = jnp.zeros((2 * N_HEADS, CHUNK, 256), F32)
    m0 = jnp.zeros((2, 8, 128), F32)
    _, _, s_ctx, m_ctx = _mlstm_call(kq_c, z_c, gt_c, bg, s0, m0)

    z, gates = _inproj_call(x, e_row, e_col, row(g_mix), sh1, sc1, w_r, w_g, use_pos=True, tm=min(L, 1024))
    kq = _conv_kq_call(z, conv_w, conv_b, conv_s, tm=min(L, 512))
    x0c, xv = _conv_hy_call(z, conv_hy_w, row(conv_hy_b), tm=min(L, 512))
    hf, hb, _, _ = _mlstm_call(kq, z, gates, bg, s_ctx, m_ctx)

    n1 = 2 * L // FFT_N2
    cst = _fft_consts(n1)
    rates = jnp.linspace(-math.log(DECAY_TARGET) / SLOW_DECAY_PCT, -math.log(DECAY_TARGET) / FAST_DECAY_PCT,
                         HY_W, dtype=F32).reshape(1, -1)
    w1t = jnp.transpose(_pad_rows(filt_w1, 64))
    colrep = lambda v: jnp.broadcast_to(v.reshape(-1, 1), (v.shape[0], 128))
    kf, l1 = _filt_call(_filter_feats(L), w1t, colrep(filt_b1), colrep(filt_freq1), jnp.transpose(filt_w2),
                        colrep(filt_b2), colrep(filt_freq2), filt_w3, row(filt_b3), rates)
    cols = FFT_N2 * HY_W
    cb = 2048
    kar, kai = _fft1_call(cst["f1"], kf, cb, packed=True)
    kv = n1 // 2 + 1
    khat = _fft2_filt_call(kar.reshape(FFT_KP, FFT_N2, HY_W), kai.reshape(FFT_KP, FFT_N2, HY_W), cst, kv)
    uar, uai = _fft1_call(cst["f1"], xv.reshape(n1 // 2, cols), cb)
    br, bi = _fft2_conv_call(uar.reshape(FFT_KP, FFT_N2, HY_W), uai.reshape(FFT_KP, FFT_N2, HY_W), khat, cst, kv)
    reps = cb // HY_W
    il_t = jnp.tile(1.0 / l1, (1, reps))
    yc = _ifft1_call(cst, br.reshape(FFT_KP, cols), bi.reshape(FFT_KP, cols), il_t, n1 // 2, cb).reshape(L, HY_W)

    wo = w_out.astype(BF16)
    x1, h2s, s_t, sh = _outproj_call(hf, hb, z, yc, x0c, xv, row(hy_dskip), x, e_row, e_col, row(g_head),
                                    wo[:MV_W], wo[MV_W:], g1, row(g_ffn), sh2, sc2, jnp.transpose(w_router),
                                    w1_s.astype(BF16), w3_s.astype(BF16), w2_s.astype(BF16), tm=min(L, 256))

    b_col = jnp.broadcast_to(b_router.reshape(N_EXPERTS, 1), (N_EXPERTS, 128))
    eidx, wts, rank, cnt2 = _route_call(s_t, b_col, tt=min(L, 1024))
    cnt = cnt2[:, 0]
    pcnt = (cnt + MOE_ROWS - 1) // MOE_ROWS * MOE_ROWS
    pend = jnp.cumsum(pcnt)
    pstart = pend - pcnt
    rows = L * TOP_K + N_EXPERTS * MOE_ROWS
    nt = rows // MOE_ROWS
    tile_row = jnp.arange(nt, dtype=I32) * MOE_ROWS
    tile_e = jnp.minimum(jnp.sum((pend[None, :] <= tile_row[:, None]).astype(I32), axis=1), N_EXPERTS - 1)
    n_used = (pend[-1] // MOE_ROWS).astype(I32).reshape(1)
    pos = _posk_call(pstart.astype(I32), eidx, rank)

    xs = _dispatch_call(cnt, pstart.astype(I32), pcnt.astype(I32), h2s, pos, rows, td=min(L, 256))
    ex = jnp.arange(N_EXPERTS, dtype=I32)
    nonempty = pcnt > 0
    nxt = jnp.min(jnp.where((ex[None, :] > ex[:, None]) & nonempty[None, :], ex[None, :], N_EXPERTS), axis=1)
    par = (jnp.cumsum(nonempty.astype(I32)) + 1) % 2
    ys = _moe_call(tile_e, n_used, nxt.astype(I32), par.astype(I32), xs, w1_e, w3_e, w2_e)
    return _final_call(x1, sh, pos, wts, ys, g2, row(g_final), tf=min(L, 256))


def kernel(x, c, ctx, c_ctx, w_ada, b_ada, g_mix, g_ffn, w_in, b_gates, conv_k_w, conv_k_b, conv_q_w,
           conv_q_b, g_head, conv_hy_w, conv_hy_b, filt_w1, filt_b1, filt_freq1, filt_w2, filt_b2,
           filt_freq2, filt_w3, filt_b3, hy_dskip, w_out, w_router, b_router, w1_e, w3_e, w2_e,
           w1_s, w3_s, w2_s, g_final):
    assert x.shape[0] == 1 and w_ada.shape[0] == 1, "one batch element, one layer"
    out = _layer(x[0], c[0], ctx[0], c_ctx, w_ada[0], b_ada[0], g_mix[0], g_ffn[0], w_in, b_gates[0],
                 conv_k_w[0], conv_k_b[0], conv_q_w[0], conv_q_b[0], g_head[0], conv_hy_w[0], conv_hy_b[0],
                 filt_w1[0], filt_b1[0], filt_freq1[0], filt_w2[0], filt_b2[0], filt_freq2[0], filt_w3[0],
                 filt_b3[0], hy_dskip[0], w_out[0], w_router[0], b_router[0], w1_e[0], w3_e[0], w2_e[0],
                 w1_s[0], w3_s[0], w2_s[0], g_final)
    return out[None]
```

```python
import functools
import math

import numpy as np
import jax
import jax.numpy as jnp
from jax import lax
from jax.experimental import pallas as pl
from jax.experimental.pallas import tpu as pltpu

F32 = jnp.float32
BF16 = jnp.bfloat16
I32 = jnp.int32
U32 = jnp.uint32

D_MODEL = 2048
GRID_W = 64
N_HEADS = 8
QK_HEAD = 64
V_HEAD = 128
QK_W = N_HEADS * QK_HEAD
MV_W = N_HEADS * V_HEAD
HY_W = D_MODEL - MV_W
CHUNK = 128
FILT_BANDS = 16
FILT_HIDDEN = 64
DECAY_TARGET = 1e-2
FAST_DECAY_PCT = 0.3
SLOW_DECAY_PCT = 1.5
N_EXPERTS = 64
N_GROUPS = 8
E_PER_GROUP = 8
TOPK_GROUPS = 4
TOP_K = 8
D_EXPERT = 512
ROUTE_SCALE = 2.5
EPS = 1e-6
OFF_K = 0
OFF_V = OFF_K + QK_W
OFF_G = OFF_V + MV_W
OFF_Q = OFF_G + 4 * N_HEADS
OFF_O = OFF_Q + QK_W
OFF_HY = OFF_O + MV_W

ZC_KQ, ZC_V, ZC_O, ZC_X0, ZC_X1, ZC_HV = 0, 1, 2, 3, 4, 5
Z_COLS = 6 * 1024
GATE_PAD = 128

NEG = -1e30
MIB = 1024 * 1024

FFT_N2 = 128
FFT_KP = 144

MOE_ROWS = 256
MOE_TILES_PER_STEP = 4
MLSTM_CHUNKS_PER_STEP = 2


def _cparams(sem, vmem_mb, flags=None):
    return pltpu.CompilerParams(dimension_semantics=sem, vmem_limit_bytes=vmem_mb * MIB, flags=flags)


def _split2(x):
    hi = x.astype(BF16)
    lo = (x - hi.astype(F32)).astype(BF16)
    return hi, lo


_NN = (((1,), (0,)), ((), ()))
_NT = (((1,), (1,)), ((), ()))
_TN = (((0,), (0,)), ((), ()))


def _dg(a, b, dims=_NN):
    return lax.dot_general(a, b, dims, preferred_element_type=F32)


def _dot3(a, b, dims=_NN):
    ah, al = _split2(a)
    bh, bl = _split2(b)
    return _dg(ah, bh, dims) + _dg(al, bh, dims) + _dg(ah, bl, dims)


def _sigmoid(x):
    return 1.0 / (1.0 + jnp.exp(-x))


def _silu(x):
    return x * _sigmoid(x)


def _pack_pair(a, b):
    hi = lax.bitcast_convert_type(a.astype(BF16).astype(F32), U32)
    lo = lax.bitcast_convert_type(b.astype(BF16).astype(F32), U32)
    return hi | (lo >> 16)


def _unpack_pair(w):
    a = lax.bitcast_convert_type(w & jnp.uint32(0xFFFF0000), F32)
    b = lax.bitcast_convert_type(w << 16, F32)
    return a, b


SLAB = 8
SLAB_W = 128


def _store_slabs(ref, w, base=0):
    r = w.shape[0]
    for j in range(SLAB):
        ref[pl.ds(base + j, r, stride=SLAB), :] = w[:, j * SLAB_W:(j + 1) * SLAB_W]


def _load_slabs(ref, r, base=0):
    return jnp.concatenate([ref[pl.ds(base + j, r, stride=SLAB), :] for j in range(SLAB)], axis=1)


def _norm_mod(x, g, sh, sc):
    ms = jnp.mean(x * x, axis=-1, keepdims=True)
    return (x * lax.rsqrt(ms + EPS) * g) * (1.0 + sc) + sh


def _add_pos(x, erow, ecol):
    tm, d = x.shape
    half = d // 2
    parts = []
    for r in range(tm // GRID_W):
        xs = x[r * GRID_W:(r + 1) * GRID_W, :]
        parts.append(jnp.concatenate([xs[:, :half] + erow[r:r + 1, :], xs[:, half:] + ecol], axis=-1))
    return parts[0] if len(parts) == 1 else jnp.concatenate(parts, axis=0)


def _mod_body(cc_ref, w_ref, b_ref, o_ref):
    w = w_ref[...]
    reps = w.shape[1] // 128
    rows = [jnp.sum(w * jnp.tile(_silu(cc_ref[v]), (1, reps)), axis=0, keepdims=True) + b_ref[...]
            for v in range(cc_ref.shape[0])]
    o_ref[...] = jnp.concatenate(rows + [jnp.zeros((8 - len(rows), w.shape[1]), F32)], axis=0)


def _mod_call(cc, w, b):
    d, n = w.shape
    tn = 1024
    return pl.pallas_call(
        _mod_body,
        grid=(n // tn,),
        in_specs=[pl.BlockSpec((cc.shape[0], d, 128), lambda j: (0, 0, 0)),
                  pl.BlockSpec((d, tn), lambda j: (0, j)),
                  pl.BlockSpec((1, tn), lambda j: (0, j))],
        out_specs=pl.BlockSpec((8, tn), lambda j: (0, j)),
        out_shape=jax.ShapeDtypeStruct((8, n), F32),
        compiler_params=_cparams(("arbitrary",), 40),
        name="mod",
    )(cc, w, b)


def _wprep_body(w_ref, wr_ref, wg_ref):
    w = w_ref[...]
    wt = jnp.concatenate([w[OFF_K:OFF_V], w[OFF_Q:OFF_O], w[OFF_V:OFF_G], w[OFF_O:]], axis=0)
    wr_ref[...] = jnp.transpose(wt).astype(BF16)
    g = jnp.concatenate([w[OFF_G:OFF_Q], jnp.zeros((GATE_PAD - 4 * N_HEADS, w.shape[1]), F32)], axis=0)
    wg_ref[...] = jnp.transpose(g)


def _wprep_call(w_t):
    n, d = w_t.shape
    tr = 256
    return pl.pallas_call(
        _wprep_body,
        grid=(d // tr,),
        in_specs=[pl.BlockSpec((n, tr), lambda i: (0, i))],
        out_specs=[pl.BlockSpec((tr, Z_COLS), lambda i: (i, 0)), pl.BlockSpec((tr, GATE_PAD), lambda i: (i, 0))],
        out_shape=[jax.ShapeDtypeStruct((d, Z_COLS), BF16), jax.ShapeDtypeStruct((d, GATE_PAD), F32)],
        compiler_params=_cparams(("arbitrary",), 32),
        name="wprep",
    )(w_t)


def _inproj_body(use_pos, x_ref, erow_ref, ecol_ref, gm_ref, sh_ref, sc_ref, w_ref, wg_ref,
                 z_ref, g_ref, h_scr):
    @pl.when(pl.program_id(1) == 0)
    def _():
        x = x_ref[...]
        if use_pos:
            x = _add_pos(x, erow_ref[...], ecol_ref[...])
        h = _norm_mod(x, gm_ref[...], sh_ref[...], sc_ref[...])
        h_scr[...] = h.astype(BF16)
        g_ref[...] = _dg(h_scr[...], wg_ref[...].astype(BF16))

    z_ref[...] = jnp.dot(h_scr[...], w_ref[...], preferred_element_type=F32).astype(BF16)


def _inproj_call(x, erow, ecol, gm, sh, sc, w, wg, use_pos, tm):
    m, d = x.shape
    tn = 1024
    er = tm // GRID_W if use_pos else erow.shape[0]
    row_map = (lambda i, j: (i, 0)) if use_pos else (lambda i, j: (0, 0))
    return pl.pallas_call(
        functools.partial(_inproj_body, use_pos),
        grid=(m // tm, Z_COLS // tn),
        in_specs=[pl.BlockSpec((tm, d), lambda i, j: (i, 0)),
                  pl.BlockSpec((er, d // 2), row_map),
                  pl.BlockSpec((GRID_W, d // 2), lambda i, j: (0, 0)),
                  pl.BlockSpec((1, d), lambda i, j: (0, 0)),
                  pl.BlockSpec((1, d), lambda i, j: (0, 0)),
                  pl.BlockSpec((1, d), lambda i, j: (0, 0)),
                  pl.BlockSpec((d, tn), lambda i, j: (0, j)),
                  pl.BlockSpec((d, GATE_PAD), lambda i, j: (0, 0))],
        out_specs=[pl.BlockSpec((tm, tn), lambda i, j: (i, j)),
                   pl.BlockSpec((tm, GATE_PAD), lambda i, j: (i, 0))],
        out_shape=[jax.ShapeDtypeStruct((m, Z_COLS), BF16),
                   jax.ShapeDtypeStruct((m, GATE_PAD), F32)],
        scratch_shapes=[pltpu.VMEM((tm, d), BF16)],
        compiler_params=_cparams(("arbitrary", "arbitrary"), 48),
        name="inproj",
    )(x, erow, ecol, gm, sh, sc, w, wg)


def _conv3(zc, zp, zn, w, b, first, last):
    tm = zc.shape[0]
    row = lax.broadcasted_iota(I32, zc.shape, 0)
    prev_row = jnp.where(first, 0.0, zp[7:8, :])
    next_row = jnp.where(last, 0.0, zn[0:1, :])
    xm = jnp.where(row == 0, prev_row, pltpu.roll(zc, 1, 0))
    xp = jnp.where(row == tm - 1, next_row, pltpu.roll(zc, tm - 1, 0))
    return xm * w[0:1, :] + zc * w[1:2, :] + xp * w[2:3, :] + b


def _conv_kq_body(zc_ref, zp_ref, zn_ref, w_ref, b_ref, s_ref, o_ref):
    i = pl.program_id(0)
    u = _conv3(zc_ref[...].astype(F32), zp_ref[...].astype(F32), zn_ref[...].astype(F32),
               w_ref[...], b_ref[...], i == 0, i == pl.num_programs(0) - 1)
    o_ref[...] = (_silu(u) * s_ref[...]).astype(BF16)


def _halo_specs(tm, m, cb):
    nb8 = m // 8
    return [pl.BlockSpec((tm, 1024), lambda i: (i, cb)),
            pl.BlockSpec((8, 1024), lambda i: (jnp.maximum(i * (tm // 8) - 1, 0), cb)),
            pl.BlockSpec((8, 1024), lambda i: (jnp.minimum((i + 1) * (tm // 8), nb8 - 1), cb))]


def _conv_kq_call(z, w, b, s, tm):
    m = z.shape[0]
    vec = pl.BlockSpec((1, 1024), lambda i: (0, 0))
    return pl.pallas_call(
        _conv_kq_body,
        grid=(m // tm,),
        in_specs=_halo_specs(tm, m, ZC_KQ) + [pl.BlockSpec((3, 1024), lambda i: (0, 0)), vec, vec],
        out_specs=pl.BlockSpec((tm, 1024), lambda i: (i, 0)),
        out_shape=jax.ShapeDtypeStruct((m, 1024), BF16),
        compiler_params=_cparams(("arbitrary",), 32),
        name="conv_kq",
    )(z, z, z, w, b, s)


def _conv_hy_body(ac_ref, ap_ref, an_ref, bc_ref, bp_ref, bn_ref, cc_ref, cp_ref, cn_ref,
                  w_ref, b_ref, x0_ref, xv_ref):
    i = pl.program_id(0)
    first, last = i == 0, i == pl.num_programs(0) - 1
    w = w_ref[...]
    b = b_ref[...]

    def cv(c, p, n, k):
        return _conv3(c[...].astype(F32), p[...].astype(F32), n[...].astype(F32),
                      w[:, k * 1024:(k + 1) * 1024], b[:, k * 1024:(k + 1) * 1024], first, last)

    x0_ref[...] = cv(ac_ref, ap_ref, an_ref, 0).astype(BF16)
    xv_ref[...] = (cv(bc_ref, bp_ref, bn_ref, 1) * cv(cc_ref, cp_ref, cn_ref, 2)).astype(BF16)


def _conv_hy_call(z, w, b, tm):
    m = z.shape[0]
    out = pl.BlockSpec((tm, 1024), lambda i: (i, 0))
    return pl.pallas_call(
        _conv_hy_body,
        grid=(m // tm,),
        in_specs=(_halo_specs(tm, m, ZC_X0) + _halo_specs(tm, m, ZC_X1) + _halo_specs(tm, m, ZC_HV)
                  + [pl.BlockSpec((3, 3072), lambda i: (0, 0)), pl.BlockSpec((1, 3072), lambda i: (0, 0))]),
        out_specs=[out, out],
        out_shape=[jax.ShapeDtypeStruct((m, 1024), BF16), jax.ShapeDtypeStruct((m, 1024), BF16)],
        compiler_params=_cparams(("arbitrary",), 32),
        name="conv_hy",
    )(z, z, z, z, z, z, z, z, z, w, b)


def _mlstm_body(kqf_ref, vf_ref, gf_ref, kqb_ref, vb_ref, gb_ref, bg_ref, s0_ref, m0_ref,
                hf_ref, hb_ref, sfin_ref, mfin_ref, s_scr, m_scr):
    j = pl.program_id(0)

    @pl.when(j == 0)
    def _():
        s_scr[...] = s0_ref[...]
        m_scr[...] = m0_ref[...]

    r = lax.broadcasted_iota(I32, (CHUNK, CHUNK), 0)
    c = lax.broadcasted_iota(I32, (CHUNK, CHUNK), 1)
    ones_b = jnp.ones((CHUNK, CHUNK), BF16)
    bg = bg_ref[...]

    def lane_bcast(x, h, width=CHUNK):
        return jnp.broadcast_to(x[:, h:h + 1], (x.shape[0], width))

    def one_chunk(d, off):
        rows = pl.ds(off, CHUNK)
        kq = (kqf_ref, kqb_ref)[d][rows, :]
        v = (vf_ref, vb_ref)[d][rows, :]
        g_all = (gf_ref, gb_ref)[d][rows, :] + bg
        out_ref = (hf_ref, hb_ref)[d]
        tri = (r >= c) if d == 0 else (c >= r)
        tri_b = jnp.where(tri, 1.0, 0.0).astype(BF16)
        gi = g_all if d == 0 else pltpu.roll(g_all, CHUNK - 16, 1)
        gfp = pltpu.roll(g_all, CHUNK - 8 - 16 * d, 1)
        lf = jnp.minimum(gfp, 0.0) - jnp.log(1.0 + jnp.exp(-jnp.abs(gfp)))
        l1 = lf.astype(BF16)
        r1 = lf - l1.astype(F32)
        l2 = r1.astype(BF16)
        l3 = (r1 - l2.astype(F32)).astype(BF16)
        bcum = _dg(tri_b, l1) + _dg(tri_b, l2) + _dg(tri_b, l3)
        gtot = bcum[CHUNK - 1:CHUNK, :] if d == 0 else bcum[0:1, :]
        acol = gtot - bcum + gi
        m_loc = jnp.max(acol, axis=0, keepdims=True)
        m_st = m_scr[d, 0:1, :]
        m_new = jnp.maximum(gtot + m_st, m_loc)
        sp8 = jnp.broadcast_to(jnp.exp(gtot + m_st - m_new), (8, CHUNK))
        wst = jnp.exp(acol - m_new)
        rr = gi - bcum
        cm = rr
        for sh in (1, 2, 4, 8, 16, 32, 64):
            if d == 0:
                cm = jnp.maximum(cm, jnp.where(r >= sh, pltpu.roll(cm, sh, 0), NEG))
            else:
                cm = jnp.maximum(cm, jnp.where(r < CHUNK - sh, pltpu.roll(cm, CHUNK - sh, 0), NEG))
        mt = jnp.maximum(bcum + m_st, bcum + cm)
        c1 = bcum - mt
        rt = jnp.transpose(rr)
        wt = jnp.transpose(wst)
        m8 = jnp.broadcast_to(m_st, (8, CHUNK))
        kts = {}

        for h in range(N_HEADS):
            p, half = divmod(h, 2)
            lm = (c // QK_HEAD) == half
            kp = kq[:, p * 128:(p + 1) * 128]
            qp = kq[:, QK_W + p * 128:QK_W + (p + 1) * 128]
            vaug = jnp.concatenate([v[:, h * 128:(h + 1) * 128], ones_b], axis=1)
            qm = jnp.where(lm, qp, jnp.zeros_like(qp))
            c1b = lane_bcast(c1, h)
            pm = jnp.exp(jnp.where(tri, c1b + rt[h:h + 1, :], NEG))
            s = (_dg(qm, kp, _NT) * pm).astype(BF16)
            m_in = jnp.tile(lane_bcast(m8, h), (CHUNK // 8, 1))
            qs = (qm.astype(F32) * jnp.exp(c1b + m_in)).astype(BF16)
            st = s_scr[d * N_HEADS + h]
            tot = _dg(jnp.concatenate([s, qs], axis=1), jnp.concatenate([vaug, st.astype(BF16)], axis=0))
            den = jnp.maximum(jnp.abs(tot[:, 128:]), jnp.exp(-lane_bcast(mt, h)))
            out_ref[rows, h * 128:(h + 1) * 128] = (tot[:, :128] / den).astype(BF16)
            if p not in kts:
                kts[p] = jnp.transpose(kp.astype(F32))
            kw = jnp.where((r // QK_HEAD) == half, kts[p] * wt[h:h + 1, :], 0.0).astype(BF16)
            spb = jnp.tile(lane_bcast(sp8, h, 256), (CHUNK // 8, 1))
            s_scr[d * N_HEADS + h] = spb * st + _dg(kw, vaug)
        m_scr[d, 0:1, :] = m_new

    for sub in range(MLSTM_CHUNKS_PER_STEP):
        one_chunk(0, sub * CHUNK)
        one_chunk(1, (MLSTM_CHUNKS_PER_STEP - 1 - sub) * CHUNK)

    @pl.when(j == pl.num_programs(0) - 1)
    def _():
        sfin_ref[...] = s_scr[...]
        mfin_ref[...] = m_scr[...]


def _mlstm_call(kq, z, gates, bg, s0, m0):
    m = kq.shape[0]
    rows = CHUNK * MLSTM_CHUNKS_PER_STEP
    nc = m // rows
    fwd = lambda cb: (lambda j: (j, cb))
    bwd = lambda cb: (lambda j: (nc - 1 - j, cb))
    st_spec = pl.BlockSpec((2 * N_HEADS, CHUNK, 256), lambda j: (0, 0, 0))
    m_spec = pl.BlockSpec((2, 8, 128), lambda j: (0, 0, 0))
    return pl.pallas_call(
        _mlstm_body,
        grid=(nc,),
        in_specs=[pl.BlockSpec((rows, 1024), fwd(0)), pl.BlockSpec((rows, 1024), fwd(ZC_V)),
                  pl.BlockSpec((rows, GATE_PAD), fwd(0)),
                  pl.BlockSpec((rows, 1024), bwd(0)), pl.BlockSpec((rows, 1024), bwd(ZC_V)),
                  pl.BlockSpec((rows, GATE_PAD), bwd(0)),
                  pl.BlockSpec((1, GATE_PAD), lambda j: (0, 0)), st_spec, m_spec],
        out_specs=[pl.BlockSpec((rows, 1024), fwd(0)), pl.BlockSpec((rows, 1024), bwd(0)), st_spec, m_spec],
        out_shape=[jax.ShapeDtypeStruct((m, 1024), BF16), jax.ShapeDtypeStruct((m, 1024), BF16),
                   jax.ShapeDtypeStruct((2 * N_HEADS, CHUNK, 256), F32),
                   jax.ShapeDtypeStruct((2, 8, 128), F32)],
        scratch_shapes=[pltpu.VMEM((2 * N_HEADS, CHUNK, 256), F32), pltpu.VMEM((2, 8, 128), F32)],
        compiler_params=_cparams(("arbitrary",), 32),
        name="mlstm",
    )(kq, z, gates, kq, z, gates, bg, s0, m0)


def _filt_body(seq_len, ft_ref, w1_ref, b1_ref, f1_ref, w2_ref, b2_ref, f2_ref, w3_ref, b3_ref, rt_ref,
               kf_ref, l1_ref):
    i = pl.program_id(0)
    tn = ft_ref.shape[1]
    hp = tn // 2
    reps = tn // 128
    col = lambda ref: jnp.tile(ref[...], (1, reps))
    h1 = jnp.sin(col(f1_ref) * (_dot3(w1_ref[...], ft_ref[...]) + col(b1_ref)))
    h2 = jnp.sin(col(f2_ref) * (_dot3(w2_ref[...], h1) + col(b2_ref)))
    r = lax.broadcasted_iota(I32, (hp, HY_W), 0)
    n_fwd = (8 * i + (r & 7)) * FFT_N2 + (r >> 3)
    rates = rt_ref[...]
    halves = []
    l1 = jnp.zeros((1, HY_W), F32)
    for hx in range(2):
        h = (_dot3(h2[:, hx * hp:(hx + 1) * hp], w3_ref[:, hx * HY_W:(hx + 1) * HY_W], _TN)
             + b3_ref[:, hx * HY_W:(hx + 1) * HY_W])
        n = n_fwd + hx * seq_len
        t01 = jnp.where(n <= seq_len, n, 2 * seq_len - n).astype(F32) / float(max(seq_len - 1, 1))
        h = jnp.where(n == seq_len, 0.0, h * jnp.exp(-t01 * rates))
        l1 = l1 + jnp.sum(jnp.abs(h), axis=0, keepdims=True)
        halves.append(h)
    word = _pack_pair(halves[0], halves[1])
    for b in range(FFT_N2):
        kf_ref[:, b * HY_W:(b + 1) * HY_W] = word[8 * b:8 * b + 8, :]

    @pl.when(i == 0)
    def _():
        l1_ref[...] = jnp.zeros_like(l1_ref)

    l1_ref[...] += l1


def _filt_call(feats_t, w1t, b1, f1, w2t, b2, f2, w3, b3, rates):
    n = feats_t.shape[1]
    seq_len = n // 2
    tn = 2 * 8 * FFT_N2
    c64 = lambda shape: pl.BlockSpec(shape, lambda i: (0, 0))
    return pl.pallas_call(
        functools.partial(_filt_body, seq_len),
        grid=(n // tn,),
        in_specs=[pl.BlockSpec((64, tn), lambda i: (0, i)),
                  c64((64, 64)), c64((64, 128)), c64((64, 128)), c64((64, 64)), c64((64, 128)), c64((64, 128)),
                  c64((64, 2 * HY_W)), c64((1, 2 * HY_W)), c64((1, HY_W))],
        out_specs=[pl.BlockSpec((8, FFT_N2 * HY_W), lambda i: (i, 0)), pl.BlockSpec((1, HY_W), lambda i: (0, 0))],
        out_shape=[jax.ShapeDtypeStruct((seq_len // FFT_N2, FFT_N2 * HY_W), U32),
                   jax.ShapeDtypeStruct((1, HY_W), F32)],
        compiler_params=_cparams(("arbitrary",), 48),
        name="filt",
    )(feats_t, w1t, b1, f1, w2t, b2, f2, w3, b3, rates)


def _fft_consts(n1_rows):
    n = n1_rows * FFT_N2
    kv = n1_rows // 2 + 1
    k1 = np.arange(FFT_KP, dtype=np.float64)
    valid = (k1 < kv).astype(np.float64)
    n1 = np.arange(n1_rows, dtype=np.float64)
    th1 = 2.0 * np.pi * np.outer(k1, n1) / n1_rows
    f1 = np.concatenate([np.cos(th1) * valid[:, None], -np.sin(th1) * valid[:, None]], axis=0)
    n2 = np.arange(FFT_N2, dtype=np.float64)
    tht = 2.0 * np.pi * np.outer(k1, n2) / n
    rep = lambda a: jnp.broadcast_to(jnp.asarray(a, F32)[:, :, None], (FFT_KP, FFT_N2, 128))
    twr = rep(np.cos(tht) * valid[:, None])
    twi = rep(-np.sin(tht) * valid[:, None])
    th2 = 2.0 * np.pi * np.outer(n2, n2) / FFT_N2
    cs, sn = np.cos(th2), np.sin(th2)
    f2p = np.block([[cs, sn], [-sn, cs]])
    f2pc = np.block([[cs, -sn], [sn, cs]])
    wk = np.where((k1 == 0) | (k1 == kv - 1), 1.0, 2.0) * valid / n
    half = n1_rows // 2
    thi = 2.0 * np.pi * np.outer(n1[:half], k1) / n1_rows
    gc = np.cos(thi) * wk[None, :]
    gs = np.sin(thi) * wk[None, :]
    as_bf = lambda a: jnp.asarray(a, F32).astype(BF16)
    return dict(f1=as_bf(f1), twr=twr, twi=twi,
                f2p=as_bf(f2p), f2pc=as_bf(f2pc), gc=as_bf(gc), gs=as_bf(gs))


def _fft1_body(f_ref, x_ref, ar_ref, ai_ref):
    o = _dg(f_ref[...], x_ref[...])
    ar_ref[...] = o[:FFT_KP].astype(BF16)
    ai_ref[...] = o[FFT_KP:].astype(BF16)


def _fft1_packed_body(f_ref, x_ref, ar_ref, ai_ref):
    k = x_ref.shape[0]
    hi, lo = _unpack_pair(x_ref[...])
    o = _dg(f_ref[:, 0:k], hi.astype(BF16)) + _dg(f_ref[:, k:], lo.astype(BF16))
    ar_ref[...] = o[:FFT_KP].astype(BF16)
    ai_ref[...] = o[FFT_KP:].astype(BF16)


def _fft1_call(f1, x2d, cb, packed=False):
    k, cols = x2d.shape
    f1 = f1[:, :2 * k] if packed else f1[:, :k]
    out = pl.BlockSpec((FFT_KP, cb), lambda i: (0, i))
    sh = jax.ShapeDtypeStruct((FFT_KP, cols), BF16)
    return pl.pallas_call(
        _fft1_packed_body if packed else _fft1_body,
        grid=(cols // cb,),
        in_specs=[pl.BlockSpec(f1.shape, lambda i: (0, 0)), pl.BlockSpec((k, cb), lambda i: (0, i))],
        out_specs=[out, out],
        out_shape=[sh, sh],
        compiler_params=_cparams(("arbitrary",), 32),
        name="fft1",
    )(f1, x2d)


def _twiddled(ar_ref, ai_ref, twr_ref, twi_ref, reps):
    a_r = ar_ref[...].astype(F32)
    a_i = ai_ref[...].astype(F32)
    tr = jnp.tile(twr_ref[...], (1, reps))
    ti = jnp.tile(twi_ref[...], (1, reps))
    st = jnp.concatenate([a_r * tr - a_i * ti, a_r * ti + a_i * tr], axis=0).astype(BF16)
    return st, tr, ti


FFT2_ROWS = 2


def _fft2_filt_body(kv, ar_ref, ai_ref, twr_ref, twi_ref, f2p_ref, k_ref):
    for j in range(FFT2_ROWS):
        k1 = pl.program_id(0) * FFT2_ROWS + j

        @pl.when(k1 < kv)
        def _():
            st, _, _ = _twiddled(ar_ref.at[j], ai_ref.at[j], twr_ref.at[j], twi_ref.at[j], ar_ref.shape[-1] // 128)
            k_ref[j] = _dg(f2p_ref[...], st).astype(BF16)

        @pl.when(k1 >= kv)
        def _():
            k_ref[j] = jnp.zeros(k_ref.shape[1:], BF16)


def _fft2_conv_body(kv, ar_ref, ai_ref, twr_ref, twi_ref, k_ref, f2p_ref, f2pc_ref, br_ref, bi_ref):
    for j in range(FFT2_ROWS):
        k1 = pl.program_id(0) * FFT2_ROWS + j

        @pl.when(k1 < kv)
        def _():
            st, tr, ti = _twiddled(ar_ref.at[j], ai_ref.at[j], twr_ref.at[j], twi_ref.at[j], ar_ref.shape[-1] // 128)
            x = _dg(f2p_ref[...], st)
            xr, xi = x[:FFT_N2], x[FFT_N2:]
            kr = k_ref[j, :FFT_N2, :].astype(F32)
            ki = k_ref[j, FFT_N2:, :].astype(F32)
            sy = jnp.concatenate([xr * kr - xi * ki, xr * ki + xi * kr], axis=0).astype(BF16)
            b = _dg(f2pc_ref[...], sy)
            b_r, b_i = b[:FFT_N2], b[FFT_N2:]
            br_ref[j] = (b_r * tr + b_i * ti).astype(BF16)
            bi_ref[j] = (b_i * tr - b_r * ti).astype(BF16)

        @pl.when(k1 >= kv)
        def _():
            br_ref[j] = jnp.zeros(br_ref.shape[1:], BF16)
            bi_ref[j] = jnp.zeros(bi_ref.shape[1:], BF16)


def _fft2_specs(ch, kv):
    src = lambda i: (jnp.minimum(i, (kv - 1) // FFT2_ROWS), 0, 0)
    blk = pl.BlockSpec((FFT2_ROWS, FFT_N2, ch), src)
    tw = pl.BlockSpec((FFT2_ROWS, FFT_N2, 128), src)
    mat = pl.BlockSpec((2 * FFT_N2, 2 * FFT_N2), lambda i: (0, 0))
    return blk, tw, mat, src


def _fft2_filt_call(ar, ai, cst, kv):
    ch = ar.shape[-1]
    blk, tw, mat, _ = _fft2_specs(ch, kv)
    return pl.pallas_call(
        functools.partial(_fft2_filt_body, kv),
        grid=(FFT_KP // FFT2_ROWS,),
        in_specs=[blk, blk, tw, tw, mat],
        out_specs=pl.BlockSpec((FFT2_ROWS, 2 * FFT_N2, ch), lambda i: (i, 0, 0)),
        out_shape=jax.ShapeDtypeStruct((FFT_KP, 2 * FFT_N2, ch), BF16),
        compiler_params=_cparams(("arbitrary",), 32),
        name="fft2_filt",
    )(ar, ai, cst["twr"], cst["twi"], cst["f2p"])


def _fft2_conv_call(ar, ai, khat, cst, kv):
    ch = ar.shape[-1]
    blk, tw, mat, src = _fft2_specs(ch, kv)
    sh = jax.ShapeDtypeStruct((FFT_KP, FFT_N2, ch), BF16)
    out = pl.BlockSpec((FFT2_ROWS, FFT_N2, ch), lambda i: (i, 0, 0))
    return pl.pallas_call(
        functools.partial(_fft2_conv_body, kv),
        grid=(FFT_KP // FFT2_ROWS,),
        in_specs=[blk, blk, tw, tw, pl.BlockSpec((FFT2_ROWS, 2 * FFT_N2, ch), src), mat, mat],
        out_specs=[out, out],
        out_shape=[sh, sh],
        compiler_params=_cparams(("arbitrary",), 32),
        name="fft2_conv",
    )(ar, ai, cst["twr"], cst["twi"], khat, cst["f2p"], cst["f2pc"])


def _ifft1_body(gc_ref, gs_ref, br_ref, bi_ref, il_ref, o_ref):
    y = _dg(gc_ref[...], br_ref[...]) - _dg(gs_ref[...], bi_ref[...])
    o_ref[...] = (y * il_ref[...]).astype(BF16)


def _ifft1_call(cst, br2d, bi2d, il_t, rows, cb):
    cols = br2d.shape[1]
    g = pl.BlockSpec((rows, FFT_KP), lambda i: (0, 0))
    kb = pl.BlockSpec((FFT_KP, cb), lambda i: (0, i))
    xb = pl.BlockSpec((rows, cb), lambda i: (0, i))
    vb = pl.BlockSpec((1, cb), lambda i: (0, 0))
    return pl.pallas_call(
        _ifft1_body,
        grid=(cols // cb,),
        in_specs=[g, g, kb, kb, vb],
        out_specs=xb,
        out_shape=jax.ShapeDtypeStruct((rows, cols), BF16),
        compiler_params=_cparams(("arbitrary",), 32),
        name="ifft1",
    )(cst["gc"], cst["gs"], br2d, bi2d, il_t)


def _outproj_body(hf_ref, hb_ref, zo_ref, yc_ref, x0_ref, xv_ref, ds_ref, x_ref, erow_ref, ecol_ref, gh_ref,
                  wa_ref, wb_ref, g1_ref, gf_ref, sh_ref, sc_ref, wr_ref, w1s_ref, w3s_ref, w2s_ref,
                  x1_ref, h2_ref, s_ref, shared_ref):
    hs = hf_ref[...].astype(F32) + hb_ref[...].astype(F32)
    gh = gh_ref[...]
    parts = []
    for h in range(N_HEADS):
        hh = hs[:, h * 128:(h + 1) * 128]
        ms = jnp.mean(hh * hh, axis=-1, keepdims=True)
        parts.append(hh * lax.rsqrt(ms + EPS) * gh[:, h * 128:(h + 1) * 128])
    ym = jnp.concatenate(parts, axis=-1) * _sigmoid(zo_ref[...].astype(F32))
    yh = x0_ref[...].astype(F32) * (yc_ref[...].astype(F32) + ds_ref[...] * xv_ref[...].astype(F32))
    y = _dg(ym.astype(BF16), wa_ref[...]) + _dg(yh.astype(BF16), wb_ref[...])
    rp = x_ref.shape[0] // GRID_W
    erow8 = erow_ref[...]
    erow = erow8[0:rp, :]
    for q in range(1, 8 // rp):
        erow = jnp.where(pl.program_id(0) % (8 // rp) == q, erow8[q * rp:(q + 1) * rp, :], erow)
    x1 = _add_pos(x_ref[...], erow, ecol_ref[...]) + g1_ref[...] * y
    x1_ref[...] = x1
    h2 = _norm_mod(x1, gf_ref[...], sh_ref[...], sc_ref[...])
    half = h2.shape[1] // 2
    _store_slabs(h2_ref, _pack_pair(h2[:, :half], h2[:, half:]))
    s_ref[...] = _sigmoid(_dot3(wr_ref[...], h2, _NT))
    h2b = h2.astype(BF16)
    a = (_silu(_dg(h2b, w1s_ref[...])) * _dg(h2b, w3s_ref[...])).astype(BF16)
    shared_ref[...] = _dg(a, w2s_ref[...]).astype(BF16)


def _outproj_call(hf, hb, z, yc, x0c, xv, ds, x, erow, ecol, gh, wa, wb, g1, gf, sh2, sc2, wrt, w1s, w3s, w2s, tm):
    m, d = x.shape
    dsh = w1s.shape[1]
    row = lambda cb: pl.BlockSpec((tm, 1024), lambda i: (i, cb))
    vec = lambda n: pl.BlockSpec((1, n), lambda i: (0, 0))
    full = pl.BlockSpec((tm, d), lambda i: (i, 0))
    const = lambda r, c: pl.BlockSpec((r, c), lambda i: (0, 0))
    return pl.pallas_call(
        _outproj_body,
        grid=(m // tm,),
        in_specs=[row(0), row(0), row(ZC_O), row(0), row(0), row(0), vec(HY_W), full,
                  pl.BlockSpec((8, d // 2), lambda i: (i * (tm // GRID_W) // 8, 0)),
                  const(GRID_W, d // 2),
                  vec(MV_W),
                  const(MV_W, d), const(HY_W, d),
                  vec(d), vec(d), vec(d), vec(d),
                  const(N_EXPERTS, d), const(d, dsh), const(d, dsh), const(dsh, d)],
        out_specs=[full, pl.BlockSpec((tm * SLAB, SLAB_W), lambda i: (i, 0)),
                   pl.BlockSpec((N_EXPERTS, tm), lambda i: (0, i)), full],
        out_shape=[jax.ShapeDtypeStruct((m, d), F32), jax.ShapeDtypeStruct((m * SLAB, SLAB_W), U32),
                   jax.ShapeDtypeStruct((N_EXPERTS, m), F32), jax.ShapeDtypeStruct((m, d), BF16)],
        compiler_params=_cparams(("arbitrary",), 56),
        name="outproj",
    )(hf, hb, z, yc, x0c, xv, ds, x, erow, ecol, gh, wa, wb, g1, gf, sh2, sc2, wrt, w1s, w3s, w2s)


def _first_max(x, idx, sentinel):
    m = jnp.max(x, axis=0, keepdims=True)
    return m, jnp.min(jnp.where(x == m, idx, sentinel), axis=0, keepdims=True)


def _route_body(s_ref, b_ref, e_ref, w_ref, r_ref, cnt_ref, u_scr, run_scr):
    i = pl.program_id(0)
    tt = s_ref.shape[1]

    @pl.when(i == 0)
    def _():
        rr = lax.broadcasted_iota(I32, (tt, tt), 0)
        cc = lax.broadcasted_iota(I32, (tt, tt), 1)
        u_scr[...] = jnp.where(rr < cc, 1.0, 0.0).astype(BF16)
        run_scr[...] = jnp.zeros_like(run_scr)

    s = s_ref[...]
    sel = s + b_ref[...][:, 0:1]
    sub8 = lax.broadcasted_iota(I32, (E_PER_GROUP, tt), 0).astype(F32)
    gs = jnp.zeros((N_GROUPS, tt), F32)
    for g in range(N_GROUPS):
        grp = sel[g * E_PER_GROUP:(g + 1) * E_PER_GROUP, :]
        m1, i1 = _first_max(grp, sub8, float(E_PER_GROUP))
        m2 = jnp.max(jnp.where(sub8 == i1, -jnp.inf, grp), axis=0, keepdims=True)
        gs = jnp.where(sub8 == g, m1 + m2, gs)
    gmask = jnp.zeros((N_GROUPS, tt), F32)
    for _ in range(TOPK_GROUPS):
        _, ig = _first_max(gs, sub8, float(N_GROUPS))
        hit = sub8 == ig
        gmask = jnp.where(hit, 1.0, gmask)
        gs = jnp.where(hit, -jnp.inf, gs)
    masked = jnp.concatenate(
        [jnp.where(jnp.broadcast_to(gmask[g:g + 1, :], (E_PER_GROUP, tt)) > 0.5,
                   sel[g * E_PER_GROUP:(g + 1) * E_PER_GROUP, :], -jnp.inf) for g in range(N_GROUPS)], axis=0)
    sub64 = lax.broadcasted_iota(I32, (N_EXPERTS, tt), 0).astype(F32)
    oh = jnp.zeros((N_EXPERTS, tt), F32)
    eks, wks = [], []
    for _ in range(TOP_K):
        _, ie = _first_max(masked, sub64, float(N_EXPERTS))
        hit = sub64 == ie
        wks.append(jnp.sum(jnp.where(hit, s, 0.0), axis=0, keepdims=True))
        eks.append(ie)
        masked = jnp.where(hit, -jnp.inf, masked)
        oh = jnp.where(hit, 1.0, oh)
    wsum = wks[0]
    for k in range(1, TOP_K):
        wsum = wsum + wks[k]
    run = run_scr[...]
    rank_t = _dg(oh.astype(BF16), u_scr[...]) + jnp.tile(run, (1, tt // 128))
    for k in range(TOP_K):
        e_ref[k:k + 1, :] = eks[k].astype(I32)
        w_ref[k:k + 1, :] = wks[k] / wsum * ROUTE_SCALE
        r_ref[k:k + 1, :] = jnp.sum(jnp.where(sub64 == eks[k], rank_t, 0.0), axis=0, keepdims=True).astype(I32)
    run_new = run + jnp.sum(oh, axis=1, keepdims=True)
    run_scr[...] = run_new
    cnt_ref[...] = run_new.astype(I32)


def _route_call(s_t, b_col, tt):
    m = s_t.shape[1]
    out = pl.BlockSpec((TOP_K, tt), lambda i: (0, i))
    return pl.pallas_call(
        _route_body,
        grid=(m // tt,),
        in_specs=[pl.BlockSpec((N_EXPERTS, tt), lambda i: (0, i)),
                  pl.BlockSpec((N_EXPERTS, 128), lambda i: (0, 0))],
        out_specs=[out, out, out, pl.BlockSpec((N_EXPERTS, 128), lambda i: (0, 0))],
        out_shape=[jax.ShapeDtypeStruct((TOP_K, m), I32), jax.ShapeDtypeStruct((TOP_K, m), F32),
                   jax.ShapeDtypeStruct((TOP_K, m), I32), jax.ShapeDtypeStruct((N_EXPERTS, 128), I32)],
        scratch_shapes=[pltpu.VMEM((tt, tt), BF16), pltpu.VMEM((N_EXPERTS, 128), F32)],
        compiler_params=_cparams(("arbitrary",), 32),
        name="route",
    )(s_t, b_col)


def _posk_body(pst_ref, e_ref, r_ref, p_ref):
    e = e_ref[...]
    acc = r_ref[...]
    for x in range(N_EXPERTS):
        acc = acc + jnp.where(e == x, pst_ref[x], 0)
    p_ref[...] = acc


def _posk_call(pstart, eidx, rank):
    k, m = eidx.shape
    tt = min(m, 2048)
    blk = pl.BlockSpec((k, tt), lambda i, pst: (0, i))
    return pl.pallas_call(
        _posk_body,
        grid_spec=pltpu.PrefetchScalarGridSpec(num_scalar_prefetch=1, grid=(m // tt,),
                                               in_specs=[blk, blk], out_specs=blk),
        out_shape=jax.ShapeDtypeStruct((k, m), I32),
        compiler_params=_cparams(("arbitrary",), 32),
        name="posk",
    )(pstart, eidx, rank)


def _slab(ref, r):
    return ref.at[pl.ds(pl.multiple_of(r * SLAB, SLAB), SLAB), :]


def _ffn_packed(x_ref, rows, w1, w3, w2, base=0):
    half = w1.shape[0] // 2
    xa, xb = _unpack_pair(_load_slabs(x_ref, rows, base))
    xa = xa.astype(BF16)
    xb = xb.astype(BF16)
    h1 = _dg(xa, w1[0:half, :]) + _dg(xb, w1[half:, :])
    h3 = _dg(xa, w3[0:half, :]) + _dg(xb, w3[half:, :])
    return _dg((_silu(h1) * h3).astype(BF16), w2[...])


def _dispatch_body(cnt_ref, pst_ref, pcn_ref, h2_ref, pos_ref, xs_ref, zrow, sem):
    i = pl.program_id(0)
    td = h2_ref.shape[0] // SLAB

    def row_copy(t, dst):
        return pltpu.make_async_copy(_slab(h2_ref, t), _slab(xs_ref, dst), sem)

    def issue(t, carry):
        for k in range(TOP_K):
            row_copy(t, pos_ref[k, t]).start(priority=k % 2)
        return carry

    lax.fori_loop(0, td, issue, 0)

    def drain(t, carry):
        for k in range(TOP_K):
            row_copy(0, 0).wait()
        return carry

    lax.fori_loop(0, td, drain, 0)

    @pl.when(i == pl.num_programs(0) - 1)
    def _():
        zrow[...] = jnp.zeros_like(zrow)

        def zero_copy(dst):
            return pltpu.make_async_copy(zrow, _slab(xs_ref, dst), sem)

        def per_expert(e, carry):
            base = pst_ref[e]
            lax.fori_loop(cnt_ref[e], pcn_ref[e], lambda rr, cc: (zero_copy(base + rr).start(), cc)[1], 0)
            lax.fori_loop(cnt_ref[e], pcn_ref[e], lambda rr, cc: (zero_copy(0).wait(), cc)[1], 0)
            return carry

        lax.fori_loop(0, N_EXPERTS, per_expert, 0)


def _dispatch_call(cnt, pstart, pcnt, h2s, pos, rows, td):
    m = h2s.shape[0] // SLAB
    return pl.pallas_call(
        _dispatch_body,
        grid_spec=pltpu.PrefetchScalarGridSpec(
            num_scalar_prefetch=3, grid=(m // td,),
            in_specs=[pl.BlockSpec((td * SLAB, SLAB_W), lambda i, *_: (i, 0)),
                      pl.BlockSpec((TOP_K, td), lambda i, *_: (0, i), memory_space=pltpu.SMEM)],
            out_specs=pl.BlockSpec(memory_space=pl.ANY),
            scratch_shapes=[pltpu.VMEM((SLAB, SLAB_W), U32), pltpu.SemaphoreType.DMA(())]),
        out_shape=jax.ShapeDtypeStruct((rows * SLAB, SLAB_W), U32),
        compiler_params=_cparams(("arbitrary",), 32),
        name="dispatch",
    )(cnt, pstart, pcnt, h2s, pos)


def _moe_body(te_ref, nu_ref, nxt_ref, par_ref, x_ref, w1_ref, w3_ref, w2_ref, y_ref,
              f1, f3, f2, w1b, w3b, w2b, sems):
    def fetch(ex, s):
        return (pltpu.make_async_copy(w1_ref.at[ex], f1.at[s], sems.at[s]),
                pltpu.make_async_copy(w3_ref.at[ex], f3.at[s], sems.at[s]),
                pltpu.make_async_copy(w2_ref.at[ex], f2.at[s], sems.at[s]))

    for sub in range(MOE_TILES_PER_STEP):
        ti = pl.program_id(0) * MOE_TILES_PER_STEP + sub
        used = ti < nu_ref[0]
        e = te_ref[ti]
        first = jnp.logical_or(ti == 0, e != te_ref[jnp.maximum(ti - 1, 0)])
        slot = par_ref[e]

        @pl.when(jnp.logical_and(used, ti == 0))
        def _():
            for c in fetch(e, slot):
                c.start()

        @pl.when(jnp.logical_and(used, first))
        def _():
            for c in fetch(e, slot):
                c.wait()
            nx = nxt_ref[e]

            @pl.when(nx < N_EXPERTS)
            def _():
                for c in fetch(nx, 1 - slot):
                    c.start()

            w1b[...] = f1[slot].astype(BF16)
            w3b[...] = f3[slot].astype(BF16)
            w2b[...] = f2[slot].astype(BF16)

        @pl.when(used)
        def _():
            half = w1b.shape[0] // 2
            base = sub * MOE_ROWS * SLAB
            y = _ffn_packed(x_ref, MOE_ROWS, w1b, w3b, w2b, base)
            _store_slabs(y_ref, _pack_pair(y[:, :half], y[:, half:]), base)


def _moe_call(tile_e, n_used, nxt, par, xs, w1, w3, w2):
    rows = xs.shape[0] // SLAB
    step_rows = MOE_ROWS * MOE_TILES_PER_STEP
    nt = rows // step_rows
    d, de = w1.shape[-2:]
    rmap = lambda i, te, nu, *_: (jnp.minimum(i, (nu[0] - 1) // MOE_TILES_PER_STEP), 0)
    hbm = pl.BlockSpec(memory_space=pl.ANY)
    return pl.pallas_call(
        _moe_body,
        grid_spec=pltpu.PrefetchScalarGridSpec(
            num_scalar_prefetch=4, grid=(nt,),
            in_specs=[pl.BlockSpec((step_rows * SLAB, SLAB_W), rmap), hbm, hbm, hbm],
            out_specs=pl.BlockSpec((step_rows * SLAB, SLAB_W), rmap),
            scratch_shapes=[pltpu.VMEM((2, d, de), F32), pltpu.VMEM((2, d, de), F32), pltpu.VMEM((2, de, d), F32),
                            pltpu.VMEM((d, de), BF16), pltpu.VMEM((d, de), BF16), pltpu.VMEM((de, d), BF16),
                            pltpu.SemaphoreType.DMA((2,))]),
        out_shape=jax.ShapeDtypeStruct((rows * SLAB, SLAB_W), U32),
        compiler_params=_cparams(("arbitrary",), 56),
        name="moe",
    )(tile_e, n_used, nxt, par, xs, w1, w3, w2)


def _final_body(x1_ref, sh_ref, pos_ref, posn_ref, wt_ref, ys_ref, g2_ref, gn_ref, o_ref, ybuf, sems):
    i = pl.program_id(0)
    n = pl.num_programs(0)
    tf, d = x1_ref.shape
    half = d // 2
    slot = i % 2
    slot_slabs = TOP_K * tf

    def row_copy(p_ref, s, k, t):
        return pltpu.make_async_copy(_slab(ys_ref, p_ref[k, t]), _slab(ybuf, s * slot_slabs + k * tf + t),
                                     sems.at[s])

    def issue(p_ref, s):
        def body(t, carry):
            for k in range(TOP_K):
                row_copy(p_ref, s, k, t).start(priority=k % 2)
            return carry

        lax.fori_loop(0, tf, body, 0)

    @pl.when(i == 0)
    def _():
        issue(pos_ref, 0)

    @pl.when(i + 1 < n)
    def _():
        issue(posn_ref, 1 - slot)

    def drain(t, carry):
        for k in range(TOP_K):
            pltpu.make_async_copy(_slab(ys_ref, 0), _slab(ybuf, 0), sems.at[slot]).wait()
        return carry

    lax.fori_loop(0, tf, drain, 0)

    wt = jnp.concatenate([jnp.transpose(jnp.concatenate(
        [wt_ref[:, c * 128:(c + 1) * 128], jnp.zeros((128 - TOP_K, 128), F32)], axis=0)) for c in range(tf // 128)],
        axis=0)
    shared = sh_ref[...].astype(F32)
    acc_a = shared[:, :half]
    acc_b = shared[:, half:]
    base = slot * slot_slabs * SLAB
    for k in range(TOP_K):
        ya, yb = _unpack_pair(_load_slabs(ybuf, tf, base=base + k * tf * SLAB))
        acc_a = acc_a + wt[:, k:k + 1] * ya
        acc_b = acc_b + wt[:, k:k + 1] * yb
    xo = x1_ref[...] + g2_ref[...] * jnp.concatenate([acc_a, acc_b], axis=1)
    ms = jnp.mean(xo * xo, axis=-1, keepdims=True)
    o_ref[...] = xo * lax.rsqrt(ms + EPS) * gn_ref[...]


def _final_call(x1, sh, pos, wts, ys, g2, gn, tf):
    m, d = x1.shape
    nt = m // tf
    full = pl.BlockSpec((tf, d), lambda i: (i, 0))
    vec = pl.BlockSpec((1, d), lambda i: (0, 0))
    return pl.pallas_call(
        _final_body,
        grid=(nt,),
        in_specs=[full, full,
                  pl.BlockSpec((TOP_K, tf), lambda i: (0, i), memory_space=pltpu.SMEM),
                  pl.BlockSpec((TOP_K, tf), lambda i: (0, jnp.minimum(i + 1, nt - 1)), memory_space=pltpu.SMEM),
                  pl.BlockSpec((TOP_K, tf), lambda i: (0, i)),
                  pl.BlockSpec(memory_space=pl.ANY), vec, vec],
        out_specs=full,
        out_shape=jax.ShapeDtypeStruct((m, d), F32),
        scratch_shapes=[pltpu.VMEM((2 * TOP_K * tf * SLAB, SLAB_W), U32), pltpu.SemaphoreType.DMA((2,))],
        compiler_params=_cparams(("arbitrary",), 48),
        name="final",
    )(x1, sh, pos, pos, wts, ys, g2, gn)


def _pos_tables(n_tokens):
    rows = n_tokens // GRID_W
    quarter = D_MODEL // 4
    omega = 1.0 / (10000.0 ** (jnp.arange(quarter, dtype=F32) / quarter))

    def emb1d(pos):
        ang = pos[:, None] * omega[None]
        return jnp.concatenate([jnp.sin(ang), jnp.cos(ang)], axis=-1)

    return emb1d(jnp.arange(rows, dtype=F32)), emb1d(jnp.arange(GRID_W, dtype=F32))


def _filter_feats(L):
    n1h = L // FFT_N2
    i_, h_, b_, a_ = jnp.meshgrid(jnp.arange(n1h // 8, dtype=I32), jnp.arange(2, dtype=I32),
                                  jnp.arange(FFT_N2, dtype=I32), jnp.arange(8, dtype=I32), indexing="ij")
    n = ((8 * i_ + a_ + h_ * n1h) * FFT_N2 + b_).reshape(-1)
    t = jnp.where(n <= L, n, 2 * L - n).astype(F32)
    t01 = t / max(L - 1, 1)
    w = 2.0 * math.pi * t / L
    bands = jnp.linspace(1e-4, FILT_BANDS - 1, FILT_BANDS, dtype=F32)
    feats = jnp.concatenate([t01[None, :], jnp.cos(bands[:, None] * w[None, :]), -jnp.sin(bands[:, None] * w[None, :]),
                             jnp.zeros((64 - 33, 2 * L), F32)], axis=0)
    return feats


def _pad_rows(a, rows):
    return jnp.concatenate([a, jnp.zeros((rows - a.shape[0],) + a.shape[1:], a.dtype)], axis=0)


def _layer(x, c, ctx, c_ctx, w_ada, b_ada, g_mix, g_ffn, w_in, b_gates, conv_k_w, conv_k_b,
           conv_q_w, conv_q_b, g_head, conv_hy_w, conv_hy_b, filt_w1, filt_b1, filt_freq1,
           filt_w2, filt_b2, filt_freq2, filt_w3, filt_b3, hy_dskip, w_out, w_router, b_router,
           w1_e, w3_e, w2_e, w1_s, w3_s, w2_s, g_final):
    L, d = x.shape
    lc = ctx.shape[0]
    row = lambda v: v.reshape(1, -1)

    cc = jnp.broadcast_to(jnp.stack([c, c_ctx], axis=0)[:, :, None], (2, d, 128))
    mods = _mod_call(cc, w_ada, row(b_ada))
    sh1, sc1, g1, sh2, sc2, g2 = [mods[0:1, k * d:(k + 1) * d] for k in range(6)]
    csh1, csc1 = mods[1:2, 0:d], mods[1:2, d:2 * d]

    w_r, w_g = _wprep_call(jnp.transpose(w_in[0]))
    bg = jnp.concatenate([b_gates, jnp.zeros((GATE_PAD - 4 * N_HEADS,), F32)]).reshape(1, GATE_PAD)
    e_row, e_col = _pos_tables(L)
    conv_w = jnp.concatenate([conv_k_w, conv_q_w], axis=1)
    conv_b = jnp.concatenate([conv_k_b, conv_q_b]).reshape(1, -1)
    conv_s = jnp.concatenate([jnp.ones((QK_W,), F32), jnp.full((QK_W,), QK_HEAD ** -0.5, F32)]).reshape(1, -1)

    z_c, gt_c = _inproj_call(ctx, jnp.zeros((8, d // 2), F32), e_col, row(g_mix), csh1, csc1, w_r, w_g,
                             use_pos=False, tm=min(lc, 256))
    kq_c = _conv_kq_call(z_c, conv_w, conv_b, conv_s, tm=min(lc, 256))
    s0 = jnp.zeros((2 * N_HEADS, CHUNK, 256), F32)
    m0 = jnp.zeros((2, 8, 128), F32)
    _, _, s_ctx, m_ctx = _mlstm_call(kq_c, z_c, gt_c, bg, s0, m0)

    z, gates = _inproj_call(x, e_row, e_col, row(g_mix), sh1, sc1, w_r, w_g, use_pos=True, tm=min(L, 1024))
    kq = _conv_kq_call(z, conv_w, conv_b, conv_s, tm=min(L, 1024))
    x0c, xv = _conv_hy_call(z, conv_hy_w, row(conv_hy_b), tm=min(L, 512))
    hf, hb, _, _ = _mlstm_call(kq, z, gates, bg, s_ctx, m_ctx)

    n1 = 2 * L // FFT_N2
    cst = _fft_consts(n1)
    rates = jnp.linspace(-math.log(DECAY_TARGET) / SLOW_DECAY_PCT, -math.log(DECAY_TARGET) / FAST_DECAY_PCT,
                         HY_W, dtype=F32).reshape(1, -1)
    w1t = jnp.transpose(_pad_rows(filt_w1, 64))
    colrep = lambda v: jnp.broadcast_to(v.reshape(-1, 1), (v.shape[0], 128))
    kf, l1 = _filt_call(_filter_feats(L), w1t, colrep(filt_b1), colrep(filt_freq1), jnp.transpose(filt_w2),
                        colrep(filt_b2), colrep(filt_freq2), filt_w3, row(filt_b3), rates)
    cols = FFT_N2 * HY_W
    cb = 2048
    kar, kai = _fft1_call(cst["f1"], kf, cb, packed=True)
    kv = n1 // 2 + 1
    khat = _fft2_filt_call(kar.reshape(FFT_KP, FFT_N2, HY_W), kai.reshape(FFT_KP, FFT_N2, HY_W), cst, kv)
    uar, uai = _fft1_call(cst["f1"], xv.reshape(n1 // 2, cols), cb)
    br, bi = _fft2_conv_call(uar.reshape(FFT_KP, FFT_N2, HY_W), uai.reshape(FFT_KP, FFT_N2, HY_W), khat, cst, kv)
    reps = cb // HY_W
    il_t = jnp.tile(1.0 / l1, (1, reps))
    yc = _ifft1_call(cst, br.reshape(FFT_KP, cols), bi.reshape(FFT_KP, cols), il_t, n1 // 2, cb).reshape(L, HY_W)

    wo = w_out.astype(BF16)
    x1, h2s, s_t, sh = _outproj_call(hf, hb, z, yc, x0c, xv, row(hy_dskip), x, e_row, e_col, row(g_head),
                                    wo[:MV_W], wo[MV_W:], g1, row(g_ffn), sh2, sc2, jnp.transpose(w_router),
                                    w1_s.astype(BF16), w3_s.astype(BF16), w2_s.astype(BF16), tm=min(L, 256))

    b_col = jnp.broadcast_to(b_router.reshape(N_EXPERTS, 1), (N_EXPERTS, 128))
    eidx, wts, rank, cnt2 = _route_call(s_t, b_col, tt=min(L, 1024))
    cnt = cnt2[:, 0]
    pcnt = (cnt + MOE_ROWS - 1) // MOE_ROWS * MOE_ROWS
    pend = jnp.cumsum(pcnt)
    pstart = pend - pcnt
    rows = L * TOP_K + N_EXPERTS * MOE_ROWS
    nt = rows // MOE_ROWS
    tile_row = jnp.arange(nt, dtype=I32) * MOE_ROWS
    tile_e = jnp.minimum(jnp.sum((pend[None, :] <= tile_row[:, None]).astype(I32), axis=1), N_EXPERTS - 1)
    n_used = (pend[-1] // MOE_ROWS).astype(I32).reshape(1)
    pos = _posk_call(pstart.astype(I32), eidx, rank)

    xs = _dispatch_call(cnt, pstart.astype(I32), pcnt.astype(I32), h2s, pos, rows, td=min(L, 256))
    ex = jnp.arange(N_EXPERTS, dtype=I32)
    nonempty = pcnt > 0
    nxt = jnp.min(jnp.where((ex[None, :] > ex[:, None]) & nonempty[None, :], ex[None, :], N_EXPERTS), axis=1)
    par = (jnp.cumsum(nonempty.astype(I32)) + 1) % 2
    ys = _moe_call(tile_e, n_used, nxt.astype(I32), par.astype(I32), xs, w1_e, w3_e, w2_e)
    return _final_call(x1, sh, pos, wts, ys, g2, row(g_final), tf=min(L, 256))


def kernel(x, c, ctx, c_ctx, w_ada, b_ada, g_mix, g_ffn, w_in, b_gates, conv_k_w, conv_k_b, conv_q_w,
           conv_q_b, g_head, conv_hy_w, conv_hy_b, filt_w1, filt_b1, filt_freq1, filt_w2, filt_b2,
           filt_freq2, filt_w3, filt_b3, hy_dskip, w_out, w_router, b_router, w1_e, w3_e, w2_e,
           w1_s, w3_s, w2_s, g_final):
    assert x.shape[0] == 1 and w_ada.shape[0] == 1, "one batch element, one layer"
    out = _layer(x[0], c[0], ctx[0], c_ctx, w_ada[0], b_ada[0], g_mix[0], g_ffn[0], w_in, b_gates[0],
                 conv_k_w[0], conv_k_b[0], conv_q_w[0], conv_q_b[0], g_head[0], conv_hy_w[0], conv_hy_b[0],
                 filt_w1[0], filt_b1[0], filt_freq1[0], filt_w2[0], filt_b2[0], filt_freq2[0], filt_w3[0],
                 filt_b3[0], hy_dskip[0], w_out[0], w_router[0], b_router[0], w1_e[0], w3_e[0], w2_e[0],
                 w1_s[0], w3_s[0], w2_s[0], g_final)
    return out[None]
```

```python
import functools
import math

import numpy as np
import jax
import jax.numpy as jnp
from jax import lax
from jax.experimental import pallas as pl
from jax.experimental.pallas import tpu as pltpu

F32 = jnp.float32
BF16 = jnp.bfloat16
I32 = jnp.int32
U32 = jnp.uint32

D_MODEL = 2048
GRID_W = 64
N_HEADS = 8
QK_HEAD = 64
V_HEAD = 128
QK_W = N_HEADS * QK_HEAD
MV_W = N_HEADS * V_HEAD
HY_W = D_MODEL - MV_W
CHUNK = 128
FILT_BANDS = 16
FILT_HIDDEN = 64
DECAY_TARGET = 1e-2
FAST_DECAY_PCT = 0.3
SLOW_DECAY_PCT = 1.5
N_EXPERTS = 64
N_GROUPS = 8
E_PER_GROUP = 8
TOPK_GROUPS = 4
TOP_K = 8
D_EXPERT = 512
ROUTE_SCALE = 2.5
EPS = 1e-6
OFF_K = 0
OFF_V = OFF_K + QK_W
OFF_G = OFF_V + MV_W
OFF_Q = OFF_G + 4 * N_HEADS
OFF_O = OFF_Q + QK_W
OFF_HY = OFF_O + MV_W

ZC_KQ, ZC_V, ZC_O, ZC_X0, ZC_X1, ZC_HV = 0, 1, 2, 3, 4, 5
Z_COLS = 6 * 1024
GATE_PAD = 128

NEG = -1e30
MIB = 1024 * 1024

FFT_N2 = 128
FFT_KP = 144

MOE_ROWS = 256
MOE_TILES_PER_STEP = 4
MLSTM_CHUNKS_PER_STEP = 2


def _cparams(sem, vmem_mb, flags=None):
    return pltpu.CompilerParams(dimension_semantics=sem, vmem_limit_bytes=vmem_mb * MIB, flags=flags)


def _split2(x):
    hi = x.astype(BF16)
    lo = (x - hi.astype(F32)).astype(BF16)
    return hi, lo


_NN = (((1,), (0,)), ((), ()))
_NT = (((1,), (1,)), ((), ()))
_TN = (((0,), (0,)), ((), ()))


def _dg(a, b, dims=_NN):
    return lax.dot_general(a, b, dims, preferred_element_type=F32)


def _dot3(a, b, dims=_NN):
    ah, al = _split2(a)
    bh, bl = _split2(b)
    return _dg(ah, bh, dims) + _dg(al, bh, dims) + _dg(ah, bl, dims)


def _sigmoid(x):
    return 1.0 / (1.0 + jnp.exp(-x))


def _silu(x):
    return x * _sigmoid(x)


def _pack_pair(a, b):
    hi = lax.bitcast_convert_type(a.astype(BF16).astype(F32), U32)
    lo = lax.bitcast_convert_type(b.astype(BF16).astype(F32), U32)
    return hi | (lo >> 16)


def _unpack_pair(w):
    a = lax.bitcast_convert_type(w & jnp.uint32(0xFFFF0000), F32)
    b = lax.bitcast_convert_type(w << 16, F32)
    return a, b


SLAB = 8
SLAB_W = 128


def _store_slabs(ref, w, base=0):
    r = w.shape[0]
    for j in range(SLAB):
        ref[pl.ds(base + j, r, stride=SLAB), :] = w[:, j * SLAB_W:(j + 1) * SLAB_W]


def _load_slabs(ref, r, base=0):
    return jnp.concatenate([ref[pl.ds(base + j, r, stride=SLAB), :] for j in range(SLAB)], axis=1)


def _norm_mod(x, g, sh, sc):
    ms = jnp.mean(x * x, axis=-1, keepdims=True)
    return (x * lax.rsqrt(ms + EPS) * g) * (1.0 + sc) + sh


def _add_pos(x, erow, ecol):
    tm, d = x.shape
    half = d // 2
    parts = []
    for r in range(tm // GRID_W):
        xs = x[r * GRID_W:(r + 1) * GRID_W, :]
        parts.append(jnp.concatenate([xs[:, :half] + erow[r:r + 1, :], xs[:, half:] + ecol], axis=-1))
    return parts[0] if len(parts) == 1 else jnp.concatenate(parts, axis=0)


def _mod_body(cc_ref, w_ref, b_ref, o_ref):
    w = w_ref[...]
    nv = cc_ref.shape[0]
    reps = w.shape[1] // 128
    rows = [jnp.sum(w * jnp.tile(_silu(cc_ref[v]), (1, reps)), axis=0, keepdims=True) for v in range(nv)]
    part = jnp.concatenate(rows + [jnp.zeros((8 - nv, w.shape[1]), F32)], axis=0)

    @pl.when(pl.program_id(0) == 0)
    def _():
        o_ref[...] = jnp.broadcast_to(b_ref[...], o_ref.shape)

    o_ref[...] += part


def _mod_call(cc, w, b):
    d, n = w.shape
    tr = 256
    return pl.pallas_call(
        _mod_body,
        grid=(d // tr,),
        in_specs=[pl.BlockSpec((cc.shape[0], tr, 128), lambda j: (0, j, 0)),
                  pl.BlockSpec((tr, n), lambda j: (j, 0)),
                  pl.BlockSpec((1, n), lambda j: (0, 0))],
        out_specs=pl.BlockSpec((8, n), lambda j: (0, 0)),
        out_shape=jax.ShapeDtypeStruct((8, n), F32),
        compiler_params=_cparams(("arbitrary",), 40),
        name="mod",
    )(cc, w, b)


def _wprep_body(w_ref, wr_ref, wg_ref):
    w = w_ref[...]
    wt = jnp.concatenate([w[OFF_K:OFF_V], w[OFF_Q:OFF_O], w[OFF_V:OFF_G], w[OFF_O:]], axis=0)
    wr_ref[...] = jnp.transpose(wt).astype(BF16)
    g = jnp.concatenate([w[OFF_G:OFF_Q], jnp.zeros((GATE_PAD - 4 * N_HEADS, w.shape[1]), F32)], axis=0)
    wg_ref[...] = jnp.transpose(g)


def _wprep_call(w_t):
    n, d = w_t.shape
    tr = 256
    return pl.pallas_call(
        _wprep_body,
        grid=(d // tr,),
        in_specs=[pl.BlockSpec((n, tr), lambda i: (0, i))],
        out_specs=[pl.BlockSpec((tr, Z_COLS), lambda i: (i, 0)), pl.BlockSpec((tr, GATE_PAD), lambda i: (i, 0))],
        out_shape=[jax.ShapeDtypeStruct((d, Z_COLS), BF16), jax.ShapeDtypeStruct((d, GATE_PAD), F32)],
        compiler_params=_cparams(("arbitrary",), 32),
        name="wprep",
    )(w_t)


def _inproj_body(use_pos, x_ref, erow_ref, ecol_ref, gm_ref, sh_ref, sc_ref, w_ref, wg_ref,
                 z_ref, g_ref, h_scr):
    @pl.when(pl.program_id(1) == 0)
    def _():
        x = x_ref[...]
        if use_pos:
            x = _add_pos(x, erow_ref[...], ecol_ref[...])
        h = _norm_mod(x, gm_ref[...], sh_ref[...], sc_ref[...])
        h_scr[...] = h.astype(BF16)
        g_ref[...] = _dg(h_scr[...], wg_ref[...].astype(BF16))

    z_ref[...] = jnp.dot(h_scr[...], w_ref[...], preferred_element_type=F32).astype(BF16)


def _inproj_call(x, erow, ecol, gm, sh, sc, w, wg, use_pos, tm):
    m, d = x.shape
    tn = 1024
    er = tm // GRID_W if use_pos else erow.shape[0]
    row_map = (lambda i, j: (i, 0)) if use_pos else (lambda i, j: (0, 0))
    return pl.pallas_call(
        functools.partial(_inproj_body, use_pos),
        grid=(m // tm, Z_COLS // tn),
        in_specs=[pl.BlockSpec((tm, d), lambda i, j: (i, 0)),
                  pl.BlockSpec((er, d // 2), row_map),
                  pl.BlockSpec((GRID_W, d // 2), lambda i, j: (0, 0)),
                  pl.BlockSpec((1, d), lambda i, j: (0, 0)),
                  pl.BlockSpec((1, d), lambda i, j: (0, 0)),
                  pl.BlockSpec((1, d), lambda i, j: (0, 0)),
                  pl.BlockSpec((d, tn), lambda i, j: (0, j)),
                  pl.BlockSpec((d, GATE_PAD), lambda i, j: (0, 0))],
        out_specs=[pl.BlockSpec((tm, tn), lambda i, j: (i, j)),
                   pl.BlockSpec((tm, GATE_PAD), lambda i, j: (i, 0))],
        out_shape=[jax.ShapeDtypeStruct((m, Z_COLS), BF16),
                   jax.ShapeDtypeStruct((m, GATE_PAD), F32)],
        scratch_shapes=[pltpu.VMEM((tm, d), BF16)],
        compiler_params=_cparams(("arbitrary", "arbitrary"), 48),
        name="inproj",
    )(x, erow, ecol, gm, sh, sc, w, wg)


def _conv3(zc, zp, zn, w, b, first, last):
    tm = zc.shape[0]
    row = lax.broadcasted_iota(I32, zc.shape, 0)
    prev_row = jnp.where(first, 0.0, zp[7:8, :])
    next_row = jnp.where(last, 0.0, zn[0:1, :])
    xm = jnp.where(row == 0, prev_row, pltpu.roll(zc, 1, 0))
    xp = jnp.where(row == tm - 1, next_row, pltpu.roll(zc, tm - 1, 0))
    return xm * w[0:1, :] + zc * w[1:2, :] + xp * w[2:3, :] + b


def _conv_kq_body(zc_ref, zp_ref, zn_ref, w_ref, b_ref, s_ref, o_ref):
    i = pl.program_id(0)
    u = _conv3(zc_ref[...].astype(F32), zp_ref[...].astype(F32), zn_ref[...].astype(F32),
               w_ref[...], b_ref[...], i == 0, i == pl.num_programs(0) - 1)
    o_ref[...] = (_silu(u) * s_ref[...]).astype(BF16)


def _halo_specs(tm, m, cb):
    nb8 = m // 8
    return [pl.BlockSpec((tm, 1024), lambda i: (i, cb)),
            pl.BlockSpec((8, 1024), lambda i: (jnp.maximum(i * (tm // 8) - 1, 0), cb)),
            pl.BlockSpec((8, 1024), lambda i: (jnp.minimum((i + 1) * (tm // 8), nb8 - 1), cb))]


def _conv_kq_call(z, w, b, s, tm):
    m = z.shape[0]
    vec = pl.BlockSpec((1, 1024), lambda i: (0, 0))
    return pl.pallas_call(
        _conv_kq_body,
        grid=(m // tm,),
        in_specs=_halo_specs(tm, m, ZC_KQ) + [pl.BlockSpec((3, 1024), lambda i: (0, 0)), vec, vec],
        out_specs=pl.BlockSpec((tm, 1024), lambda i: (i, 0)),
        out_shape=jax.ShapeDtypeStruct((m, 1024), BF16),
        compiler_params=_cparams(("arbitrary",), 32),
        name="conv_kq",
    )(z, z, z, w, b, s)


def _conv_hy_body(ac_ref, ap_ref, an_ref, bc_ref, bp_ref, bn_ref, cc_ref, cp_ref, cn_ref,
                  w_ref, b_ref, x0_ref, xv_ref):
    i = pl.program_id(0)
    first, last = i == 0, i == pl.num_programs(0) - 1
    w = w_ref[...]
    b = b_ref[...]

    def cv(c, p, n, k):
        return _conv3(c[...].astype(F32), p[...].astype(F32), n[...].astype(F32),
                      w[:, k * 1024:(k + 1) * 1024], b[:, k * 1024:(k + 1) * 1024], first, last)

    x0_ref[...] = cv(ac_ref, ap_ref, an_ref, 0).astype(BF16)
    xv_ref[...] = (cv(bc_ref, bp_ref, bn_ref, 1) * cv(cc_ref, cp_ref, cn_ref, 2)).astype(BF16)


def _conv_hy_call(z, w, b, tm):
    m = z.shape[0]
    out = pl.BlockSpec((tm, 1024), lambda i: (i, 0))
    return pl.pallas_call(
        _conv_hy_body,
        grid=(m // tm,),
        in_specs=(_halo_specs(tm, m, ZC_X0) + _halo_specs(tm, m, ZC_X1) + _halo_specs(tm, m, ZC_HV)
                  + [pl.BlockSpec((3, 3072), lambda i: (0, 0)), pl.BlockSpec((1, 3072), lambda i: (0, 0))]),
        out_specs=[out, out],
        out_shape=[jax.ShapeDtypeStruct((m, 1024), BF16), jax.ShapeDtypeStruct((m, 1024), BF16)],
        compiler_params=_cparams(("arbitrary",), 32),
        name="conv_hy",
    )(z, z, z, z, z, z, z, z, z, w, b)


def _mlstm_body(kqf_ref, vf_ref, gf_ref, kqb_ref, vb_ref, gb_ref, bg_ref, s0_ref, m0_ref,
                hf_ref, hb_ref, sfin_ref, mfin_ref, s_scr, m_scr):
    j = pl.program_id(0)

    @pl.when(j == 0)
    def _():
        s_scr[...] = s0_ref[...]
        m_scr[...] = m0_ref[...]

    r = lax.broadcasted_iota(I32, (CHUNK, CHUNK), 0)
    c = lax.broadcasted_iota(I32, (CHUNK, CHUNK), 1)
    ones_b = jnp.ones((CHUNK, CHUNK), BF16)
    bg = bg_ref[...]

    def lane_bcast(x, h, width=CHUNK):
        return jnp.broadcast_to(x[:, h:h + 1], (x.shape[0], width))

    def one_chunk(d, off):
        rows = pl.ds(off, CHUNK)
        kq = (kqf_ref, kqb_ref)[d][rows, :]
        v = (vf_ref, vb_ref)[d][rows, :]
        g_all = (gf_ref, gb_ref)[d][rows, :] + bg
        out_ref = (hf_ref, hb_ref)[d]
        tri = (r >= c) if d == 0 else (c >= r)
        tri_b = jnp.where(tri, 1.0, 0.0).astype(BF16)
        gi = g_all if d == 0 else pltpu.roll(g_all, CHUNK - 16, 1)
        gfp = pltpu.roll(g_all, CHUNK - 8 - 16 * d, 1)
        lf = jnp.minimum(gfp, 0.0) - jnp.log(1.0 + jnp.exp(-jnp.abs(gfp)))
        l1 = lf.astype(BF16)
        r1 = lf - l1.astype(F32)
        l2 = r1.astype(BF16)
        l3 = (r1 - l2.astype(F32)).astype(BF16)
        bcum = _dg(tri_b, l1) + _dg(tri_b, l2) + _dg(tri_b, l3)
        gtot = bcum[CHUNK - 1:CHUNK, :] if d == 0 else bcum[0:1, :]
        acol = gtot - bcum + gi
        m_loc = jnp.max(acol, axis=0, keepdims=True)
        m_st = m_scr[d, 0:1, :]
        m_new = jnp.maximum(gtot + m_st, m_loc)
        sp8 = jnp.broadcast_to(jnp.exp(gtot + m_st - m_new), (8, CHUNK))
        wst = jnp.exp(acol - m_new)
        rr = gi - bcum
        cm = rr
        for sh in (1, 2, 4, 8, 16, 32, 64):
            if d == 0:
                cm = jnp.maximum(cm, jnp.where(r >= sh, pltpu.roll(cm, sh, 0), NEG))
            else:
                cm = jnp.maximum(cm, jnp.where(r < CHUNK - sh, pltpu.roll(cm, CHUNK - sh, 0), NEG))
        mt = jnp.maximum(bcum + m_st, bcum + cm)
        c1 = bcum - mt
        rt = jnp.transpose(rr)
        wt = jnp.transpose(wst)
        m8 = jnp.broadcast_to(m_st, (8, CHUNK))
        kts = {}

        for h in range(N_HEADS):
            p, half = divmod(h, 2)
            lm = (c // QK_HEAD) == half
            kp = kq[:, p * 128:(p + 1) * 128]
            qp = kq[:, QK_W + p * 128:QK_W + (p + 1) * 128]
            vaug = jnp.concatenate([v[:, h * 128:(h + 1) * 128], ones_b], axis=1)
            qm = jnp.where(lm, qp, jnp.zeros_like(qp))
            c1b = lane_bcast(c1, h)
            pm = jnp.exp(jnp.where(tri, c1b + rt[h:h + 1, :], NEG))
            s = (_dg(qm, kp, _NT) * pm).astype(BF16)
            m_in = jnp.tile(lane_bcast(m8, h), (CHUNK // 8, 1))
            qs = (qm.astype(F32) * jnp.exp(c1b + m_in)).astype(BF16)
            st = s_scr[d * N_HEADS + h]
            tot = _dg(jnp.concatenate([s, qs], axis=1), jnp.concatenate([vaug, st.astype(BF16)], axis=0))
            den = jnp.maximum(jnp.abs(tot[:, 128:]), jnp.exp(-lane_bcast(mt, h)))
            out_ref[rows, h * 128:(h + 1) * 128] = (tot[:, :128] / den).astype(BF16)
            if p not in kts:
                kts[p] = jnp.transpose(kp.astype(F32))
            kw = jnp.where((r // QK_HEAD) == half, kts[p] * wt[h:h + 1, :], 0.0).astype(BF16)
            spb = jnp.tile(lane_bcast(sp8, h, 256), (CHUNK // 8, 1))
            s_scr[d * N_HEADS + h] = spb * st + _dg(kw, vaug)
        m_scr[d, 0:1, :] = m_new

    for sub in range(MLSTM_CHUNKS_PER_STEP):
        one_chunk(0, sub * CHUNK)
        one_chunk(1, (MLSTM_CHUNKS_PER_STEP - 1 - sub) * CHUNK)

    @pl.when(j == pl.num_programs(0) - 1)
    def _():
        sfin_ref[...] = s_scr[...]
        mfin_ref[...] = m_scr[...]


def _mlstm_call(kq, z, gates, bg, s0, m0):
    m = kq.shape[0]
    rows = CHUNK * MLSTM_CHUNKS_PER_STEP
    nc = m // rows
    fwd = lambda cb: (lambda j: (j, cb))
    bwd = lambda cb: (lambda j: (nc - 1 - j, cb))
    st_spec = pl.BlockSpec((2 * N_HEADS, CHUNK, 256), lambda j: (0, 0, 0))
    m_spec = pl.BlockSpec((2, 8, 128), lambda j: (0, 0, 0))
    return pl.pallas_call(
        _mlstm_body,
        grid=(nc,),
        in_specs=[pl.BlockSpec((rows, 1024), fwd(0)), pl.BlockSpec((rows, 1024), fwd(ZC_V)),
                  pl.BlockSpec((rows, GATE_PAD), fwd(0)),
                  pl.BlockSpec((rows, 1024), bwd(0)), pl.BlockSpec((rows, 1024), bwd(ZC_V)),
                  pl.BlockSpec((rows, GATE_PAD), bwd(0)),
                  pl.BlockSpec((1, GATE_PAD), lambda j: (0, 0)), st_spec, m_spec],
        out_specs=[pl.BlockSpec((rows, 1024), fwd(0)), pl.BlockSpec((rows, 1024), bwd(0)), st_spec, m_spec],
        out_shape=[jax.ShapeDtypeStruct((m, 1024), BF16), jax.ShapeDtypeStruct((m, 1024), BF16),
                   jax.ShapeDtypeStruct((2 * N_HEADS, CHUNK, 256), F32),
                   jax.ShapeDtypeStruct((2, 8, 128), F32)],
        scratch_shapes=[pltpu.VMEM((2 * N_HEADS, CHUNK, 256), F32), pltpu.VMEM((2, 8, 128), F32)],
        compiler_params=_cparams(("arbitrary",), 32),
        name="mlstm",
    )(kq, z, gates, kq, z, gates, bg, s0, m0)


def _filt_body(seq_len, ft_ref, w1_ref, b1_ref, f1_ref, w2_ref, b2_ref, f2_ref, w3_ref, b3_ref, rt_ref,
               kf_ref, l1_ref):
    i = pl.program_id(0)
    tn = ft_ref.shape[1]
    hp = tn // 2
    reps = tn // 128
    col = lambda ref: jnp.tile(ref[...], (1, reps))
    h1 = jnp.sin(col(f1_ref) * (_dot3(w1_ref[...], ft_ref[...]) + col(b1_ref)))
    h2 = jnp.sin(col(f2_ref) * (_dot3(w2_ref[...], h1) + col(b2_ref)))
    r = lax.broadcasted_iota(I32, (hp, HY_W), 0)
    n_fwd = (8 * i + (r & 7)) * FFT_N2 + (r >> 3)
    rates = rt_ref[...]
    halves = []
    l1 = jnp.zeros((1, HY_W), F32)
    for hx in range(2):
        h = (_dot3(h2[:, hx * hp:(hx + 1) * hp], w3_ref[:, hx * HY_W:(hx + 1) * HY_W], _TN)
             + b3_ref[:, hx * HY_W:(hx + 1) * HY_W])
        n = n_fwd + hx * seq_len
        t01 = jnp.where(n <= seq_len, n, 2 * seq_len - n).astype(F32) / float(max(seq_len - 1, 1))
        h = jnp.where(n == seq_len, 0.0, h * jnp.exp(-t01 * rates))
        l1 = l1 + jnp.sum(jnp.abs(h), axis=0, keepdims=True)
        halves.append(h)
    word = _pack_pair(halves[0], halves[1])
    for b in range(FFT_N2):
        kf_ref[:, b * HY_W:(b + 1) * HY_W] = word[8 * b:8 * b + 8, :]

    @pl.when(i == 0)
    def _():
        l1_ref[...] = jnp.zeros_like(l1_ref)

    l1_ref[...] += l1


def _filt_call(feats_t, w1t, b1, f1, w2t, b2, f2, w3, b3, rates):
    n = feats_t.shape[1]
    seq_len = n // 2
    tn = 2 * 8 * FFT_N2
    c64 = lambda shape: pl.BlockSpec(shape, lambda i: (0, 0))
    return pl.pallas_call(
        functools.partial(_filt_body, seq_len),
        grid=(n // tn,),
        in_specs=[pl.BlockSpec((64, tn), lambda i: (0, i)),
                  c64((64, 64)), c64((64, 128)), c64((64, 128)), c64((64, 64)), c64((64, 128)), c64((64, 128)),
                  c64((64, 2 * HY_W)), c64((1, 2 * HY_W)), c64((1, HY_W))],
        out_specs=[pl.BlockSpec((8, FFT_N2 * HY_W), lambda i: (i, 0)), pl.BlockSpec((1, HY_W), lambda i: (0, 0))],
        out_shape=[jax.ShapeDtypeStruct((seq_len // FFT_N2, FFT_N2 * HY_W), U32),
                   jax.ShapeDtypeStruct((1, HY_W), F32)],
        compiler_params=_cparams(("arbitrary",), 48),
        name="filt",
    )(feats_t, w1t, b1, f1, w2t, b2, f2, w3, b3, rates)


def _fft_consts(n1_rows):
    n = n1_rows * FFT_N2
    kv = n1_rows // 2 + 1
    k1 = np.arange(FFT_KP, dtype=np.float64)
    valid = (k1 < kv).astype(np.float64)
    n1 = np.arange(n1_rows, dtype=np.float64)
    th1 = 2.0 * np.pi * np.outer(k1, n1) / n1_rows
    f1 = np.concatenate([np.cos(th1) * valid[:, None], -np.sin(th1) * valid[:, None]], axis=0)
    n2 = np.arange(FFT_N2, dtype=np.float64)
    tht = 2.0 * np.pi * np.outer(k1, n2) / n
    rep = lambda a: jnp.broadcast_to(jnp.asarray(a, F32)[:, :, None], (FFT_KP, FFT_N2, 128))
    twr = rep(np.cos(tht) * valid[:, None])
    twi = rep(-np.sin(tht) * valid[:, None])
    th2 = 2.0 * np.pi * np.outer(n2, n2) / FFT_N2
    cs, sn = np.cos(th2), np.sin(th2)
    f2p = np.block([[cs, sn], [-sn, cs]])
    f2pc = np.block([[cs, -sn], [sn, cs]])
    wk = np.where((k1 == 0) | (k1 == kv - 1), 1.0, 2.0) * valid / n
    half = n1_rows // 2
    thi = 2.0 * np.pi * np.outer(n1[:half], k1) / n1_rows
    gc = np.cos(thi) * wk[None, :]
    gs = np.sin(thi) * wk[None, :]
    as_bf = lambda a: jnp.asarray(a, F32).astype(BF16)
    return dict(f1=as_bf(f1), twr=twr, twi=twi,
                f2p=as_bf(f2p), f2pc=as_bf(f2pc), gc=as_bf(gc), gs=as_bf(gs))


def _fft1_body(f_ref, x_ref, ar_ref, ai_ref):
    o = _dg(f_ref[...], x_ref[...])
    ar_ref[...] = o[:FFT_KP].astype(BF16)
    ai_ref[...] = o[FFT_KP:].astype(BF16)


def _fft1_packed_body(f_ref, x_ref, ar_ref, ai_ref):
    k = x_ref.shape[0]
    hi, lo = _unpack_pair(x_ref[...])
    o = _dg(f_ref[:, 0:k], hi.astype(BF16)) + _dg(f_ref[:, k:], lo.astype(BF16))
    ar_ref[...] = o[:FFT_KP].astype(BF16)
    ai_ref[...] = o[FFT_KP:].astype(BF16)


def _fft1_call(f1, x2d, cb, packed=False):
    k, cols = x2d.shape
    f1 = f1[:, :2 * k] if packed else f1[:, :k]
    out = pl.BlockSpec((FFT_KP, cb), lambda i: (0, i))
    sh = jax.ShapeDtypeStruct((FFT_KP, cols), BF16)
    return pl.pallas_call(
        _fft1_packed_body if packed else _fft1_body,
        grid=(cols // cb,),
        in_specs=[pl.BlockSpec(f1.shape, lambda i: (0, 0)), pl.BlockSpec((k, cb), lambda i: (0, i))],
        out_specs=[out, out],
        out_shape=[sh, sh],
        compiler_params=_cparams(("arbitrary",), 32),
        name="fft1",
    )(f1, x2d)


def _twiddled(ar_ref, ai_ref, twr_ref, twi_ref, reps):
    a_r = ar_ref[...].astype(F32)
    a_i = ai_ref[...].astype(F32)
    tr = jnp.tile(twr_ref[...], (1, reps))
    ti = jnp.tile(twi_ref[...], (1, reps))
    st = jnp.concatenate([a_r * tr - a_i * ti, a_r * ti + a_i * tr], axis=0).astype(BF16)
    return st, tr, ti


FFT2_ROWS = 2


def _fft2_filt_body(kv, ar_ref, ai_ref, twr_ref, twi_ref, f2p_ref, k_ref):
    for j in range(FFT2_ROWS):
        k1 = pl.program_id(0) * FFT2_ROWS + j

        @pl.when(k1 < kv)
        def _():
            st, _, _ = _twiddled(ar_ref.at[j], ai_ref.at[j], twr_ref.at[j], twi_ref.at[j], ar_ref.shape[-1] // 128)
            k_ref[j] = _dg(f2p_ref[...], st).astype(BF16)

        @pl.when(k1 >= kv)
        def _():
            k_ref[j] = jnp.zeros(k_ref.shape[1:], BF16)


def _fft2_conv_body(kv, ar_ref, ai_ref, twr_ref, twi_ref, k_ref, f2p_ref, f2pc_ref, br_ref, bi_ref):
    for j in range(FFT2_ROWS):
        k1 = pl.program_id(0) * FFT2_ROWS + j

        @pl.when(k1 < kv)
        def _():
            st, tr, ti = _twiddled(ar_ref.at[j], ai_ref.at[j], twr_ref.at[j], twi_ref.at[j], ar_ref.shape[-1] // 128)
            x = _dg(f2p_ref[...], st)
            xr, xi = x[:FFT_N2], x[FFT_N2:]
            kr = k_ref[j, :FFT_N2, :].astype(F32)
            ki = k_ref[j, FFT_N2:, :].astype(F32)
            sy = jnp.concatenate([xr * kr - xi * ki, xr * ki + xi * kr], axis=0).astype(BF16)
            b = _dg(f2pc_ref[...], sy)
            b_r, b_i = b[:FFT_N2], b[FFT_N2:]
            br_ref[j] = (b_r * tr + b_i * ti).astype(BF16)
            bi_ref[j] = (b_i * tr - b_r * ti).astype(BF16)

        @pl.when(k1 >= kv)
        def _():
            br_ref[j] = jnp.zeros(br_ref.shape[1:], BF16)
            bi_ref[j] = jnp.zeros(bi_ref.shape[1:], BF16)


def _fft2_specs(ch, kv):
    src = lambda i: (jnp.minimum(i, (kv - 1) // FFT2_ROWS), 0, 0)
    blk = pl.BlockSpec((FFT2_ROWS, FFT_N2, ch), src)
    tw = pl.BlockSpec((FFT2_ROWS, FFT_N2, 128), src)
    mat = pl.BlockSpec((2 * FFT_N2, 2 * FFT_N2), lambda i: (0, 0))
    return blk, tw, mat, src


def _fft2_filt_call(ar, ai, cst, kv):
    ch = ar.shape[-1]
    blk, tw, mat, _ = _fft2_specs(ch, kv)
    return pl.pallas_call(
        functools.partial(_fft2_filt_body, kv),
        grid=(FFT_KP // FFT2_ROWS,),
        in_specs=[blk, blk, tw, tw, mat],
        out_specs=pl.BlockSpec((FFT2_ROWS, 2 * FFT_N2, ch), lambda i: (i, 0, 0)),
        out_shape=jax.ShapeDtypeStruct((FFT_KP, 2 * FFT_N2, ch), BF16),
        compiler_params=_cparams(("arbitrary",), 32),
        name="fft2_filt",
    )(ar, ai, cst["twr"], cst["twi"], cst["f2p"])


def _fft2_conv_call(ar, ai, khat, cst, kv):
    ch = ar.shape[-1]
    blk, tw, mat, src = _fft2_specs(ch, kv)
    sh = jax.ShapeDtypeStruct((FFT_KP, FFT_N2, ch), BF16)
    out = pl.BlockSpec((FFT2_ROWS, FFT_N2, ch), lambda i: (i, 0, 0))
    return pl.pallas_call(
        functools.partial(_fft2_conv_body, kv),
        grid=(FFT_KP // FFT2_ROWS,),
        in_specs=[blk, blk, tw, tw, pl.BlockSpec((FFT2_ROWS, 2 * FFT_N2, ch), src), mat, mat],
        out_specs=[out, out],
        out_shape=[sh, sh],
        compiler_params=_cparams(("arbitrary",), 32),
        name="fft2_conv",
    )(ar, ai, cst["twr"], cst["twi"], khat, cst["f2p"], cst["f2pc"])


def _ifft1_body(gc_ref, gs_ref, br_ref, bi_ref, il_ref, o_ref):
    y = _dg(gc_ref[...], br_ref[...]) - _dg(gs_ref[...], bi_ref[...])
    o_ref[...] = (y * il_ref[...]).astype(BF16)


def _ifft1_call(cst, br2d, bi2d, il_t, rows, cb):
    cols = br2d.shape[1]
    g = pl.BlockSpec((rows, FFT_KP), lambda i: (0, 0))
    kb = pl.BlockSpec((FFT_KP, cb), lambda i: (0, i))
    xb = pl.BlockSpec((rows, cb), lambda i: (0, i))
    vb = pl.BlockSpec((1, cb), lambda i: (0, 0))
    return pl.pallas_call(
        _ifft1_body,
        grid=(cols // cb,),
        in_specs=[g, g, kb, kb, vb],
        out_specs=xb,
        out_shape=jax.ShapeDtypeStruct((rows, cols), BF16),
        compiler_params=_cparams(("arbitrary",), 32),
        name="ifft1",
    )(cst["gc"], cst["gs"], br2d, bi2d, il_t)


def _outproj_body(hf_ref, hb_ref, zo_ref, yc_ref, x0_ref, xv_ref, ds_ref, x_ref, erow_ref, ecol_ref, gh_ref,
                  wa_ref, wb_ref, g1_ref, gf_ref, sh_ref, sc_ref, wr_ref, w1s_ref, w3s_ref, w2s_ref,
                  x1_ref, h2_ref, s_ref, shared_ref):
    hs = hf_ref[...].astype(F32) + hb_ref[...].astype(F32)
    gh = gh_ref[...]
    parts = []
    for h in range(N_HEADS):
        hh = hs[:, h * 128:(h + 1) * 128]
        ms = jnp.mean(hh * hh, axis=-1, keepdims=True)
        parts.append(hh * lax.rsqrt(ms + EPS) * gh[:, h * 128:(h + 1) * 128])
    ym = jnp.concatenate(parts, axis=-1) * _sigmoid(zo_ref[...].astype(F32))
    yh = x0_ref[...].astype(F32) * (yc_ref[...].astype(F32) + ds_ref[...] * xv_ref[...].astype(F32))
    y = _dg(ym.astype(BF16), wa_ref[...]) + _dg(yh.astype(BF16), wb_ref[...])
    rp = x_ref.shape[0] // GRID_W
    erow8 = erow_ref[...]
    erow = erow8[0:rp, :]
    for q in range(1, 8 // rp):
        erow = jnp.where(pl.program_id(0) % (8 // rp) == q, erow8[q * rp:(q + 1) * rp, :], erow)
    x1 = _add_pos(x_ref[...], erow, ecol_ref[...]) + g1_ref[...] * y
    x1_ref[...] = x1
    h2 = _norm_mod(x1, gf_ref[...], sh_ref[...], sc_ref[...])
    half = h2.shape[1] // 2
    _store_slabs(h2_ref, _pack_pair(h2[:, :half], h2[:, half:]))
    s_ref[...] = _sigmoid(_dot3(wr_ref[...], h2, _NT))
    h2b = h2.astype(BF16)
    a = (_silu(_dg(h2b, w1s_ref[...])) * _dg(h2b, w3s_ref[...])).astype(BF16)
    shared_ref[...] = _dg(a, w2s_ref[...]).astype(BF16)


def _outproj_call(hf, hb, z, yc, x0c, xv, ds, x, erow, ecol, gh, wa, wb, g1, gf, sh2, sc2, wrt, w1s, w3s, w2s, tm):
    m, d = x.shape
    dsh = w1s.shape[1]
    row = lambda cb: pl.BlockSpec((tm, 1024), lambda i: (i, cb))
    vec = lambda n: pl.BlockSpec((1, n), lambda i: (0, 0))
    full = pl.BlockSpec((tm, d), lambda i: (i, 0))
    const = lambda r, c: pl.BlockSpec((r, c), lambda i: (0, 0))
    return pl.pallas_call(
        _outproj_body,
        grid=(m // tm,),
        in_specs=[row(0), row(0), row(ZC_O), row(0), row(0), row(0), vec(HY_W), full,
                  pl.BlockSpec((8, d // 2), lambda i: (i * (tm // GRID_W) // 8, 0)),
                  const(GRID_W, d // 2),
                  vec(MV_W),
                  const(MV_W, d), const(HY_W, d),
                  vec(d), vec(d), vec(d), vec(d),
                  const(N_EXPERTS, d), const(d, dsh), const(d, dsh), const(dsh, d)],
        out_specs=[full, pl.BlockSpec((tm * SLAB, SLAB_W), lambda i: (i, 0)),
                   pl.BlockSpec((N_EXPERTS, tm), lambda i: (0, i)), full],
        out_shape=[jax.ShapeDtypeStruct((m, d), F32), jax.ShapeDtypeStruct((m * SLAB, SLAB_W), U32),
                   jax.ShapeDtypeStruct((N_EXPERTS, m), F32), jax.ShapeDtypeStruct((m, d), BF16)],
        compiler_params=_cparams(("arbitrary",), 56),
        name="outproj",
    )(hf, hb, z, yc, x0c, xv, ds, x, erow, ecol, gh, wa, wb, g1, gf, sh2, sc2, wrt, w1s, w3s, w2s)


def _first_max(x, idx, sentinel):
    m = jnp.max(x, axis=0, keepdims=True)
    return m, jnp.min(jnp.where(x == m, idx, sentinel), axis=0, keepdims=True)


def _route_body(s_ref, b_ref, e_ref, w_ref, r_ref, cnt_ref, u_scr, run_scr):
    i = pl.program_id(0)
    tt = s_ref.shape[1]

    @pl.when(i == 0)
    def _():
        rr = lax.broadcasted_iota(I32, (tt, tt), 0)
        cc = lax.broadcasted_iota(I32, (tt, tt), 1)
        u_scr[...] = jnp.where(rr < cc, 1.0, 0.0).astype(BF16)
        run_scr[...] = jnp.zeros_like(run_scr)

    s = s_ref[...]
    sel = s + b_ref[...][:, 0:1]
    sub8 = lax.broadcasted_iota(I32, (E_PER_GROUP, tt), 0).astype(F32)
    gs = jnp.zeros((N_GROUPS, tt), F32)
    for g in range(N_GROUPS):
        grp = sel[g * E_PER_GROUP:(g + 1) * E_PER_GROUP, :]
        m1, i1 = _first_max(grp, sub8, float(E_PER_GROUP))
        m2 = jnp.max(jnp.where(sub8 == i1, -jnp.inf, grp), axis=0, keepdims=True)
        gs = jnp.where(sub8 == g, m1 + m2, gs)
    gmask = jnp.zeros((N_GROUPS, tt), F32)
    for _ in range(TOPK_GROUPS):
        _, ig = _first_max(gs, sub8, float(N_GROUPS))
        hit = sub8 == ig
        gmask = jnp.where(hit, 1.0, gmask)
        gs = jnp.where(hit, -jnp.inf, gs)
    masked = jnp.concatenate(
        [jnp.where(jnp.broadcast_to(gmask[g:g + 1, :], (E_PER_GROUP, tt)) > 0.5,
                   sel[g * E_PER_GROUP:(g + 1) * E_PER_GROUP, :], -jnp.inf) for g in range(N_GROUPS)], axis=0)
    sub64 = lax.broadcasted_iota(I32, (N_EXPERTS, tt), 0).astype(F32)
    oh = jnp.zeros((N_EXPERTS, tt), F32)
    eks, wks = [], []
    for _ in range(TOP_K):
        _, ie = _first_max(masked, sub64, float(N_EXPERTS))
        hit = sub64 == ie
        wks.append(jnp.sum(jnp.where(hit, s, 0.0), axis=0, keepdims=True))
        eks.append(ie)
        masked = jnp.where(hit, -jnp.inf, masked)
        oh = jnp.where(hit, 1.0, oh)
    wsum = wks[0]
    for k in range(1, TOP_K):
        wsum = wsum + wks[k]
    run = run_scr[...]
    rank_t = _dg(oh.astype(BF16), u_scr[...]) + jnp.tile(run, (1, tt // 128))
    for k in range(TOP_K):
        e_ref[k:k + 1, :] = eks[k].astype(I32)
        w_ref[k:k + 1, :] = wks[k] / wsum * ROUTE_SCALE
        r_ref[k:k + 1, :] = jnp.sum(jnp.where(sub64 == eks[k], rank_t, 0.0), axis=0, keepdims=True).astype(I32)
    run_new = run + jnp.sum(oh, axis=1, keepdims=True)
    run_scr[...] = run_new
    cnt_ref[...] = run_new.astype(I32)


def _route_call(s_t, b_col, tt):
    m = s_t.shape[1]
    out = pl.BlockSpec((TOP_K, tt), lambda i: (0, i))
    return pl.pallas_call(
        _route_body,
        grid=(m // tt,),
        in_specs=[pl.BlockSpec((N_EXPERTS, tt), lambda i: (0, i)),
                  pl.BlockSpec((N_EXPERTS, 128), lambda i: (0, 0))],
        out_specs=[out, out, out, pl.BlockSpec((N_EXPERTS, 128), lambda i: (0, 0))],
        out_shape=[jax.ShapeDtypeStruct((TOP_K, m), I32), jax.ShapeDtypeStruct((TOP_K, m), F32),
                   jax.ShapeDtypeStruct((TOP_K, m), I32), jax.ShapeDtypeStruct((N_EXPERTS, 128), I32)],
        scratch_shapes=[pltpu.VMEM((tt, tt), BF16), pltpu.VMEM((N_EXPERTS, 128), F32)],
        compiler_params=_cparams(("arbitrary",), 32),
        name="route",
    )(s_t, b_col)


def _posk_body(pst_ref, e_ref, r_ref, p_ref):
    e = e_ref[...]
    acc = r_ref[...]
    for x in range(N_EXPERTS):
        acc = acc + jnp.where(e == x, pst_ref[x], 0)
    p_ref[...] = acc


def _posk_call(pstart, eidx, rank):
    k, m = eidx.shape
    tt = min(m, 2048)
    blk = pl.BlockSpec((k, tt), lambda i, pst: (0, i))
    return pl.pallas_call(
        _posk_body,
        grid_spec=pltpu.PrefetchScalarGridSpec(num_scalar_prefetch=1, grid=(m // tt,),
                                               in_specs=[blk, blk], out_specs=blk),
        out_shape=jax.ShapeDtypeStruct((k, m), I32),
        compiler_params=_cparams(("arbitrary",), 32),
        name="posk",
    )(pstart, eidx, rank)


def _slab(ref, r):
    return ref.at[pl.ds(pl.multiple_of(r * SLAB, SLAB), SLAB), :]


def _ffn_packed(x_ref, rows, w1, w3, w2, base=0):
    half = w1.shape[0] // 2
    xa, xb = _unpack_pair(_load_slabs(x_ref, rows, base))
    xa = xa.astype(BF16)
    xb = xb.astype(BF16)
    h1 = _dg(xa, w1[0:half, :]) + _dg(xb, w1[half:, :])
    h3 = _dg(xa, w3[0:half, :]) + _dg(xb, w3[half:, :])
    return _dg((_silu(h1) * h3).astype(BF16), w2[...])


def _dispatch_body(cnt_ref, pst_ref, pcn_ref, h2_ref, pos_ref, xs_ref, zrow, sem):
    i = pl.program_id(0)
    td = h2_ref.shape[0] // SLAB

    def row_copy(t, dst):
        return pltpu.make_async_copy(_slab(h2_ref, t), _slab(xs_ref, dst), sem)

    def issue(t, carry):
        for k in range(TOP_K):
            row_copy(t, pos_ref[k, t]).start(priority=k % 2)
        return carry

    lax.fori_loop(0, td, issue, 0)

    def drain(t, carry):
        for k in range(TOP_K):
            row_copy(0, 0).wait()
        return carry

    lax.fori_loop(0, td, drain, 0)

    @pl.when(i == pl.num_programs(0) - 1)
    def _():
        zrow[...] = jnp.zeros_like(zrow)

        def zero_copy(dst):
            return pltpu.make_async_copy(zrow, _slab(xs_ref, dst), sem)

        def per_expert(e, carry):
            base = pst_ref[e]
            lax.fori_loop(cnt_ref[e], pcn_ref[e], lambda rr, cc: (zero_copy(base + rr).start(), cc)[1], 0)
            lax.fori_loop(cnt_ref[e], pcn_ref[e], lambda rr, cc: (zero_copy(0).wait(), cc)[1], 0)
            return carry

        lax.fori_loop(0, N_EXPERTS, per_expert, 0)


def _dispatch_call(cnt, pstart, pcnt, h2s, pos, rows, td):
    m = h2s.shape[0] // SLAB
    return pl.pallas_call(
        _dispatch_body,
        grid_spec=pltpu.PrefetchScalarGridSpec(
            num_scalar_prefetch=3, grid=(m // td,),
            in_specs=[pl.BlockSpec((td * SLAB, SLAB_W), lambda i, *_: (i, 0)),
                      pl.BlockSpec((TOP_K, td), lambda i, *_: (0, i), memory_space=pltpu.SMEM)],
            out_specs=pl.BlockSpec(memory_space=pl.ANY),
            scratch_shapes=[pltpu.VMEM((SLAB, SLAB_W), U32), pltpu.SemaphoreType.DMA(())]),
        out_shape=jax.ShapeDtypeStruct((rows * SLAB, SLAB_W), U32),
        compiler_params=_cparams(("arbitrary",), 32),
        name="dispatch",
    )(cnt, pstart, pcnt, h2s, pos)


def _moe_body(te_ref, nu_ref, nxt_ref, par_ref, x_ref, w1_ref, w3_ref, w2_ref, y_ref,
              f1, f3, f2, w1b, w3b, w2b, sems):
    def fetch(ex, s):
        return (pltpu.make_async_copy(w1_ref.at[ex], f1.at[s], sems.at[s]),
                pltpu.make_async_copy(w3_ref.at[ex], f3.at[s], sems.at[s]),
                pltpu.make_async_copy(w2_ref.at[ex], f2.at[s], sems.at[s]))

    for sub in range(MOE_TILES_PER_STEP):
        ti = pl.program_id(0) * MOE_TILES_PER_STEP + sub
        used = ti < nu_ref[0]
        e = te_ref[ti]
        first = jnp.logical_or(ti == 0, e != te_ref[jnp.maximum(ti - 1, 0)])
        slot = par_ref[e]

        @pl.when(jnp.logical_and(used, ti == 0))
        def _():
            for c in fetch(e, slot):
                c.start()

        @pl.when(jnp.logical_and(used, first))
        def _():
            for c in fetch(e, slot):
                c.wait()
            nx = nxt_ref[e]

            @pl.when(nx < N_EXPERTS)
            def _():
                for c in fetch(nx, 1 - slot):
                    c.start()

            w1b[...] = f1[slot].astype(BF16)
            w3b[...] = f3[slot].astype(BF16)
            w2b[...] = f2[slot].astype(BF16)

        @pl.when(used)
        def _():
            half = w1b.shape[0] // 2
            base = sub * MOE_ROWS * SLAB
            y = _ffn_packed(x_ref, MOE_ROWS, w1b, w3b, w2b, base)
            _store_slabs(y_ref, _pack_pair(y[:, :half], y[:, half:]), base)


def _moe_call(tile_e, n_used, nxt, par, xs, w1, w3, w2):
    rows = xs.shape[0] // SLAB
    step_rows = MOE_ROWS * MOE_TILES_PER_STEP
    nt = rows // step_rows
    d, de = w1.shape[-2:]
    rmap = lambda i, te, nu, *_: (jnp.minimum(i, (nu[0] - 1) // MOE_TILES_PER_STEP), 0)
    hbm = pl.BlockSpec(memory_space=pl.ANY)
    return pl.pallas_call(
        _moe_body,
        grid_spec=pltpu.PrefetchScalarGridSpec(
            num_scalar_prefetch=4, grid=(nt,),
            in_specs=[pl.BlockSpec((step_rows * SLAB, SLAB_W), rmap), hbm, hbm, hbm],
            out_specs=pl.BlockSpec((step_rows * SLAB, SLAB_W), rmap),
            scratch_shapes=[pltpu.VMEM((2, d, de), F32), pltpu.VMEM((2, d, de), F32), pltpu.VMEM((2, de, d), F32),
                            pltpu.VMEM((d, de), BF16), pltpu.VMEM((d, de), BF16), pltpu.VMEM((de, d), BF16),
                            pltpu.SemaphoreType.DMA((2,))]),
        out_shape=jax.ShapeDtypeStruct((rows * SLAB, SLAB_W), U32),
        compiler_params=_cparams(("arbitrary",), 56),
        name="moe",
    )(tile_e, n_used, nxt, par, xs, w1, w3, w2)


def _final_body(x1_ref, sh_ref, pos_ref, posn_ref, wt_ref, ys_ref, g2_ref, gn_ref, o_ref,
                ybuf, acc_a, acc_b, sems):
    i = pl.program_id(0)
    n = pl.num_programs(0)
    tf, d = x1_ref.shape
    slot = i % 2
    slot_slabs = TOP_K * tf

    def row_copy(p_ref, s, k, t):
        return pltpu.make_async_copy(_slab(ys_ref, p_ref[k, t]), _slab(ybuf, s * slot_slabs + k * tf + t),
                                     sems.at[s])

    def issue_token(p_ref, s, t):
        for k in range(TOP_K):
            row_copy(p_ref, s, k, t).start(priority=k % 2)

    @pl.when(i == 0)
    def _():
        lax.fori_loop(0, tf, lambda t, c: (issue_token(pos_ref, 0, t), c)[1], 0)

    def drain(t, carry):
        for k in range(TOP_K):
            pltpu.make_async_copy(_slab(ys_ref, 0), _slab(ybuf, 0), sems.at[slot]).wait()
        return carry

    lax.fori_loop(0, tf, drain, 0)

    def combine_token(t):
        a = jnp.zeros((SLAB, SLAB_W), F32)
        b = jnp.zeros((SLAB, SLAB_W), F32)
        for k in range(TOP_K):
            ya, yb = _unpack_pair(_slab(ybuf, slot * slot_slabs + k * tf + t)[...])
            w = wt_ref[k, t]
            a = a + w * ya
            b = b + w * yb
        _slab(acc_a, t)[...] = a
        _slab(acc_b, t)[...] = b

    @pl.when(i + 1 < n)
    def _():
        def body(t, carry):
            combine_token(t)
            issue_token(posn_ref, 1 - slot, t)
            return carry

        lax.fori_loop(0, tf, body, 0)

    @pl.when(i + 1 >= n)
    def _():
        lax.fori_loop(0, tf, lambda t, c: (combine_token(t), c)[1], 0)

    moe = jnp.concatenate([_load_slabs(acc_a, tf), _load_slabs(acc_b, tf)], axis=1) + sh_ref[...].astype(F32)
    xo = x1_ref[...] + g2_ref[...] * moe
    ms = jnp.mean(xo * xo, axis=-1, keepdims=True)
    o_ref[...] = xo * lax.rsqrt(ms + EPS) * gn_ref[...]


def _final_call(x1, sh, pos, wts, ys, g2, gn, tf):
    m, d = x1.shape
    nt = m // tf
    full = pl.BlockSpec((tf, d), lambda i: (i, 0))
    vec = pl.BlockSpec((1, d), lambda i: (0, 0))
    return pl.pallas_call(
        _final_body,
        grid=(nt,),
        in_specs=[full, full,
                  pl.BlockSpec((TOP_K, tf), lambda i: (0, i), memory_space=pltpu.SMEM),
                  pl.BlockSpec((TOP_K, tf), lambda i: (0, jnp.minimum(i + 1, nt - 1)), memory_space=pltpu.SMEM),
                  pl.BlockSpec((TOP_K, tf), lambda i: (0, i), memory_space=pltpu.SMEM),
                  pl.BlockSpec(memory_space=pl.ANY), vec, vec],
        out_specs=full,
        out_shape=jax.ShapeDtypeStruct((m, d), F32),
        scratch_shapes=[pltpu.VMEM((2 * TOP_K * tf * SLAB, SLAB_W), U32),
                        pltpu.VMEM((tf * SLAB, SLAB_W), F32), pltpu.VMEM((tf * SLAB, SLAB_W), F32),
                        pltpu.SemaphoreType.DMA((2,))],
        compiler_params=_cparams(("arbitrary",), 48),
        name="final",
    )(x1, sh, pos, pos, wts, ys, g2, gn)


def _pos_tables(n_tokens):
    rows = n_tokens // GRID_W
    quarter = D_MODEL // 4
    omega = 1.0 / (10000.0 ** (jnp.arange(quarter, dtype=F32) / quarter))

    def emb1d(pos):
        ang = pos[:, None] * omega[None]
        return jnp.concatenate([jnp.sin(ang), jnp.cos(ang)], axis=-1)

    return emb1d(jnp.arange(rows, dtype=F32)), emb1d(jnp.arange(GRID_W, dtype=F32))


def _filter_feats(L):
    n1h = L // FFT_N2
    i_, h_, b_, a_ = jnp.meshgrid(jnp.arange(n1h // 8, dtype=I32), jnp.arange(2, dtype=I32),
                                  jnp.arange(FFT_N2, dtype=I32), jnp.arange(8, dtype=I32), indexing="ij")
    n = ((8 * i_ + a_ + h_ * n1h) * FFT_N2 + b_).reshape(-1)
    t = jnp.where(n <= L, n, 2 * L - n).astype(F32)
    t01 = t / max(L - 1, 1)
    w = 2.0 * math.pi * t / L
    bands = jnp.linspace(1e-4, FILT_BANDS - 1, FILT_BANDS, dtype=F32)
    feats = jnp.concatenate([t01[None, :], jnp.cos(bands[:, None] * w[None, :]), -jnp.sin(bands[:, None] * w[None, :]),
                             jnp.zeros((64 - 33, 2 * L), F32)], axis=0)
    return feats


def _pad_rows(a, rows):
    return jnp.concatenate([a, jnp.zeros((rows - a.shape[0],) + a.shape[1:], a.dtype)], axis=0)


def _layer(x, c, ctx, c_ctx, w_ada, b_ada, g_mix, g_ffn, w_in, b_gates, conv_k_w, conv_k_b,
           conv_q_w, conv_q_b, g_head, conv_hy_w, conv_hy_b, filt_w1, filt_b1, filt_freq1,
           filt_w2, filt_b2, filt_freq2, filt_w3, filt_b3, hy_dskip, w_out, w_router, b_router,
           w1_e, w3_e, w2_e, w1_s, w3_s, w2_s, g_final):
    L, d = x.shape
    lc = ctx.shape[0]
    row = lambda v: v.reshape(1, -1)

    cc = jnp.broadcast_to(jnp.stack([c, c_ctx], axis=0)[:, :, None], (2, d, 128))
    mods = _mod_call(cc, w_ada, row(b_ada))
    sh1, sc1, g1, sh2, sc2, g2 = [mods[0:1, k * d:(k + 1) * d] for k in range(6)]
    csh1, csc1 = mods[1:2, 0:d], mods[1:2, d:2 * d]

    w_r, w_g = _wprep_call(jnp.transpose(w_in[0]))
    bg = jnp.concatenate([b_gates, jnp.zeros((GATE_PAD - 4 * N_HEADS,), F32)]).reshape(1, GATE_PAD)
    e_row, e_col = _pos_tables(L)
    conv_w = jnp.concatenate([conv_k_w, conv_q_w], axis=1)
    conv_b = jnp.concatenate([conv_k_b, conv_q_b]).reshape(1, -1)
    conv_s = jnp.concatenate([jnp.ones((QK_W,), F32), jnp.full((QK_W,), QK_HEAD ** -0.5, F32)]).reshape(1, -1)

    z_c, gt_c = _inproj_call(ctx, jnp.zeros((8, d // 2), F32), e_col, row(g_mix), csh1, csc1, w_r, w_g,
                             use_pos=False, tm=min(lc, 256))
    kq_c = _conv_kq_call(z_c, conv_w, conv_b, conv_s, tm=min(lc, 256))
    s0 = jnp.zeros((2 * N_HEADS, CHUNK, 256), F32)
    m0 = jnp.zeros((2, 8, 128), F32)
    _, _, s_ctx, m_ctx = _mlstm_call(kq_c, z_c, gt_c, bg, s0, m0)

    z, gates = _inproj_call(x, e_row, e_col, row(g_mix), sh1, sc1, w_r, w_g, use_pos=True, tm=min(L, 1024))
    kq = _conv_kq_call(z, conv_w, conv_b, conv_s, tm=min(L, 1024))
    x0c, xv = _conv_hy_call(z, conv_hy_w, row(conv_hy_b), tm=min(L, 512))
    hf, hb, _, _ = _mlstm_call(kq, z, gates, bg, s_ctx, m_ctx)

    n1 = 2 * L // FFT_N2
    cst = _fft_consts(n1)
    rates = jnp.linspace(-math.log(DECAY_TARGET) / SLOW_DECAY_PCT, -math.log(DECAY_TARGET) / FAST_DECAY_PCT,
                         HY_W, dtype=F32).reshape(1, -1)
    w1t = jnp.transpose(_pad_rows(filt_w1, 64))
    colrep = lambda v: jnp.broadcast_to(v.reshape(-1, 1), (v.shape[0], 128))
    kf, l1 = _filt_call(_filter_feats(L), w1t, colrep(filt_b1), colrep(filt_freq1), jnp.transpose(filt_w2),
                        colrep(filt_b2), colrep(filt_freq2), filt_w3, row(filt_b3), rates)
    cols = FFT_N2 * HY_W
    cb = 2048
    kar, kai = _fft1_call(cst["f1"], kf, cb, packed=True)
    kv = n1 // 2 + 1
    khat = _fft2_filt_call(kar.reshape(FFT_KP, FFT_N2, HY_W), kai.reshape(FFT_KP, FFT_N2, HY_W), cst, kv)
    uar, uai = _fft1_call(cst["f1"], xv.reshape(n1 // 2, cols), cb)
    br, bi = _fft2_conv_call(uar.reshape(FFT_KP, FFT_N2, HY_W), uai.reshape(FFT_KP, FFT_N2, HY_W), khat, cst, kv)
    reps = cb // HY_W
    il_t = jnp.tile(1.0 / l1, (1, reps))
    yc = _ifft1_call(cst, br.reshape(FFT_KP, cols), bi.reshape(FFT_KP, cols), il_t, n1 // 2, cb).reshape(L, HY_W)

    wo = w_out.astype(BF16)
    x1, h2s, s_t, sh = _outproj_call(hf, hb, z, yc, x0c, xv, row(hy_dskip), x, e_row, e_col, row(g_head),
                                    wo[:MV_W], wo[MV_W:], g1, row(g_ffn), sh2, sc2, jnp.transpose(w_router),
                                    w1_s.astype(BF16), w3_s.astype(BF16), w2_s.astype(BF16), tm=min(L, 256))

    b_col = jnp.broadcast_to(b_router.reshape(N_EXPERTS, 1), (N_EXPERTS, 128))
    eidx, wts, rank, cnt2 = _route_call(s_t, b_col, tt=min(L, 1024))
    cnt = cnt2[:, 0]
    pcnt = (cnt + MOE_ROWS - 1) // MOE_ROWS * MOE_ROWS
    pend = jnp.cumsum(pcnt)
    pstart = pend - pcnt
    rows = L * TOP_K + N_EXPERTS * MOE_ROWS
    nt = rows // MOE_ROWS
    tile_row = jnp.arange(nt, dtype=I32) * MOE_ROWS
    tile_e = jnp.minimum(jnp.sum((pend[None, :] <= tile_row[:, None]).astype(I32), axis=1), N_EXPERTS - 1)
    n_used = (pend[-1] // MOE_ROWS).astype(I32).reshape(1)
    pos = _posk_call(pstart.astype(I32), eidx, rank)

    xs = _dispatch_call(cnt, pstart.astype(I32), pcnt.astype(I32), h2s, pos, rows, td=min(L, 256))
    ex = jnp.arange(N_EXPERTS, dtype=I32)
    nonempty = pcnt > 0
    nxt = jnp.min(jnp.where((ex[None, :] > ex[:, None]) & nonempty[None, :], ex[None, :], N_EXPERTS), axis=1)
    par = (jnp.cumsum(nonempty.astype(I32)) + 1) % 2
    ys = _moe_call(tile_e, n_used, nxt.astype(I32), par.astype(I32), xs, w1_e, w3_e, w2_e)
    return _final_call(x1, sh, pos, wts, ys, g2, row(g_final), tf=min(L, 256))


def kernel(x, c, ctx, c_ctx, w_ada, b_ada, g_mix, g_ffn, w_in, b_gates, conv_k_w, conv_k_b, conv_q_w,
           conv_q_b, g_head, conv_hy_w, conv_hy_b, filt_w1, filt_b1, filt_freq1, filt_w2, filt_b2,
           filt_freq2, filt_w3, filt_b3, hy_dskip, w_out, w_router, b_router, w1_e, w3_e, w2_e,
           w1_s, w3_s, w2_s, g_final):
    assert x.shape[0] == 1 and w_ada.shape[0] == 1, "one batch element, one layer"
    out = _layer(x[0], c[0], ctx[0], c_ctx, w_ada[0], b_ada[0], g_mix[0], g_ffn[0], w_in, b_gates[0],
                 conv_k_w[0], conv_k_b[0], conv_q_w[0], conv_q_b[0], g_head[0], conv_hy_w[0], conv_hy_b[0],
                 filt_w1[0], filt_b1[0], filt_freq1[0], filt_w2[0], filt_b2[0], filt_freq2[0], filt_w3[0],
                 filt_b3[0], hy_dskip[0], w_out[0], w_router[0], b_router[0], w1_e[0], w3_e[0], w2_e[0],
                 w1_s[0], w3_s[0], w2_s[0], g_final)
    return out[None]
```

```python
import functools
import math

import numpy as np
import jax
import jax.numpy as jnp
from jax import lax
from jax.experimental import pallas as pl
from jax.experimental.pallas import tpu as pltpu

F32 = jnp.float32
BF16 = jnp.bfloat16
I32 = jnp.int32
U32 = jnp.uint32

D_MODEL = 2048
GRID_W = 64
N_HEADS = 8
QK_HEAD = 64
V_HEAD = 128
QK_W = N_HEADS * QK_HEAD
MV_W = N_HEADS * V_HEAD
HY_W = D_MODEL - MV_W
CHUNK = 128
FILT_BANDS = 16
FILT_HIDDEN = 64
DECAY_TARGET = 1e-2
FAST_DECAY_PCT = 0.3
SLOW_DECAY_PCT = 1.5
N_EXPERTS = 64
N_GROUPS = 8
E_PER_GROUP = 8
TOPK_GROUPS = 4
TOP_K = 8
D_EXPERT = 512
ROUTE_SCALE = 2.5
EPS = 1e-6
OFF_K = 0
OFF_V = OFF_K + QK_W
OFF_G = OFF_V + MV_W
OFF_Q = OFF_G + 4 * N_HEADS
OFF_O = OFF_Q + QK_W
OFF_HY = OFF_O + MV_W

ZC_KQ, ZC_V, ZC_O, ZC_X0, ZC_X1, ZC_HV = 0, 1, 2, 3, 4, 5
Z_COLS = 6 * 1024
GATE_PAD = 128

NEG = -1e30
MIB = 1024 * 1024

FFT_N2 = 128
FFT_KP = 144

MOE_ROWS = 256
MOE_TILES_PER_STEP = 4
MLSTM_CHUNKS_PER_STEP = 4


def _cparams(sem, vmem_mb, flags=None):
    return pltpu.CompilerParams(dimension_semantics=sem, vmem_limit_bytes=vmem_mb * MIB, flags=flags)


def _split2(x):
    hi = x.astype(BF16)
    lo = (x - hi.astype(F32)).astype(BF16)
    return hi, lo


_NN = (((1,), (0,)), ((), ()))
_NT = (((1,), (1,)), ((), ()))
_TN = (((0,), (0,)), ((), ()))


def _dg(a, b, dims=_NN):
    return lax.dot_general(a, b, dims, preferred_element_type=F32)


def _dot3(a, b, dims=_NN):
    ah, al = _split2(a)
    bh, bl = _split2(b)
    return _dg(ah, bh, dims) + _dg(al, bh, dims) + _dg(ah, bl, dims)


def _sigmoid(x):
    return 1.0 / (1.0 + jnp.exp(-x))


def _silu(x):
    return x * _sigmoid(x)


def _pack_pair(a, b):
    hi = lax.bitcast_convert_type(a.astype(BF16).astype(F32), U32)
    lo = lax.bitcast_convert_type(b.astype(BF16).astype(F32), U32)
    return hi | (lo >> 16)


def _unpack_pair(w):
    a = lax.bitcast_convert_type(w & jnp.uint32(0xFFFF0000), F32)
    b = lax.bitcast_convert_type(w << 16, F32)
    return a, b


SLAB = 8
SLAB_W = 128


def _store_slabs(ref, w, base=0):
    r = w.shape[0]
    for j in range(SLAB):
        ref[pl.ds(base + j, r, stride=SLAB), :] = w[:, j * SLAB_W:(j + 1) * SLAB_W]


def _load_slabs(ref, r, base=0):
    return jnp.concatenate([ref[pl.ds(base + j, r, stride=SLAB), :] for j in range(SLAB)], axis=1)


def _norm_mod(x, g, sh, sc):
    ms = jnp.mean(x * x, axis=-1, keepdims=True)
    return (x * lax.rsqrt(ms + EPS) * g) * (1.0 + sc) + sh


def _add_pos(x, erow, ecol):
    tm, d = x.shape
    half = d // 2
    parts = []
    for r in range(tm // GRID_W):
        xs = x[r * GRID_W:(r + 1) * GRID_W, :]
        parts.append(jnp.concatenate([xs[:, :half] + erow[r:r + 1, :], xs[:, half:] + ecol], axis=-1))
    return parts[0] if len(parts) == 1 else jnp.concatenate(parts, axis=0)


def _mod_body(cc_ref, w_ref, b_ref, o_ref):
    w = w_ref[...]
    nv = cc_ref.shape[0]
    reps = w.shape[1] // 128
    rows = [jnp.sum(w * jnp.tile(_silu(cc_ref[v]), (1, reps)), axis=0, keepdims=True) for v in range(nv)]
    part = jnp.concatenate(rows + [jnp.zeros((8 - nv, w.shape[1]), F32)], axis=0)

    @pl.when(pl.program_id(0) == 0)
    def _():
        o_ref[...] = jnp.broadcast_to(b_ref[...], o_ref.shape)

    o_ref[...] += part


def _mod_call(cc, w, b):
    d, n = w.shape
    tr = 256
    return pl.pallas_call(
        _mod_body,
        grid=(d // tr,),
        in_specs=[pl.BlockSpec((cc.shape[0], tr, 128), lambda j: (0, j, 0)),
                  pl.BlockSpec((tr, n), lambda j: (j, 0)),
                  pl.BlockSpec((1, n), lambda j: (0, 0))],
        out_specs=pl.BlockSpec((8, n), lambda j: (0, 0)),
        out_shape=jax.ShapeDtypeStruct((8, n), F32),
        compiler_params=_cparams(("arbitrary",), 40),
        name="mod",
    )(cc, w, b)


def _wprep_body(w_ref, wr_ref, wg_ref):
    w = w_ref[...]
    wt = jnp.concatenate([w[OFF_K:OFF_V], w[OFF_Q:OFF_O], w[OFF_V:OFF_G], w[OFF_O:]], axis=0)
    wr_ref[...] = jnp.transpose(wt).astype(BF16)
    g = jnp.concatenate([w[OFF_G:OFF_Q], jnp.zeros((GATE_PAD - 4 * N_HEADS, w.shape[1]), F32)], axis=0)
    wg_ref[...] = jnp.transpose(g)


def _wprep_call(w_t):
    n, d = w_t.shape
    tr = 256
    return pl.pallas_call(
        _wprep_body,
        grid=(d // tr,),
        in_specs=[pl.BlockSpec((n, tr), lambda i: (0, i))],
        out_specs=[pl.BlockSpec((tr, Z_COLS), lambda i: (i, 0)), pl.BlockSpec((tr, GATE_PAD), lambda i: (i, 0))],
        out_shape=[jax.ShapeDtypeStruct((d, Z_COLS), BF16), jax.ShapeDtypeStruct((d, GATE_PAD), F32)],
        compiler_params=_cparams(("arbitrary",), 32),
        name="wprep",
    )(w_t)


def _inproj_body(use_pos, x_ref, erow_ref, ecol_ref, gm_ref, sh_ref, sc_ref, w_ref, wg_ref,
                 z_ref, g_ref, h_scr):
    @pl.when(pl.program_id(1) == 0)
    def _():
        x = x_ref[...]
        if use_pos:
            x = _add_pos(x, erow_ref[...], ecol_ref[...])
        h = _norm_mod(x, gm_ref[...], sh_ref[...], sc_ref[...])
        h_scr[...] = h.astype(BF16)
        g_ref[...] = _dg(h_scr[...], wg_ref[...].astype(BF16))

    z_ref[...] = jnp.dot(h_scr[...], w_ref[...], preferred_element_type=F32).astype(BF16)


def _inproj_call(x, erow, ecol, gm, sh, sc, w, wg, use_pos, tm):
    m, d = x.shape
    tn = 1024
    er = tm // GRID_W if use_pos else erow.shape[0]
    row_map = (lambda i, j: (i, 0)) if use_pos else (lambda i, j: (0, 0))
    return pl.pallas_call(
        functools.partial(_inproj_body, use_pos),
        grid=(m // tm, Z_COLS // tn),
        in_specs=[pl.BlockSpec((tm, d), lambda i, j: (i, 0)),
                  pl.BlockSpec((er, d // 2), row_map),
                  pl.BlockSpec((GRID_W, d // 2), lambda i, j: (0, 0)),
                  pl.BlockSpec((1, d), lambda i, j: (0, 0)),
                  pl.BlockSpec((1, d), lambda i, j: (0, 0)),
                  pl.BlockSpec((1, d), lambda i, j: (0, 0)),
                  pl.BlockSpec((d, tn), lambda i, j: (0, j)),
                  pl.BlockSpec((d, GATE_PAD), lambda i, j: (0, 0))],
        out_specs=[pl.BlockSpec((tm, tn), lambda i, j: (i, j)),
                   pl.BlockSpec((tm, GATE_PAD), lambda i, j: (i, 0))],
        out_shape=[jax.ShapeDtypeStruct((m, Z_COLS), BF16),
                   jax.ShapeDtypeStruct((m, GATE_PAD), F32)],
        scratch_shapes=[pltpu.VMEM((tm, d), BF16)],
        compiler_params=_cparams(("arbitrary", "arbitrary"), 48),
        name="inproj",
    )(x, erow, ecol, gm, sh, sc, w, wg)


def _conv3(zc, zp, zn, w, b, first, last):
    tm = zc.shape[0]
    row = lax.broadcasted_iota(I32, zc.shape, 0)
    prev_row = jnp.where(first, 0.0, zp[7:8, :])
    next_row = jnp.where(last, 0.0, zn[0:1, :])
    xm = jnp.where(row == 0, prev_row, pltpu.roll(zc, 1, 0))
    xp = jnp.where(row == tm - 1, next_row, pltpu.roll(zc, tm - 1, 0))
    return xm * w[0:1, :] + zc * w[1:2, :] + xp * w[2:3, :] + b


def _conv_kq_body(zc_ref, zp_ref, zn_ref, w_ref, b_ref, s_ref, o_ref):
    i = pl.program_id(0)
    u = _conv3(zc_ref[...].astype(F32), zp_ref[...].astype(F32), zn_ref[...].astype(F32),
               w_ref[...], b_ref[...], i == 0, i == pl.num_programs(0) - 1)
    o_ref[...] = (_silu(u) * s_ref[...]).astype(BF16)


def _halo_specs(tm, m, cb):
    nb8 = m // 8
    return [pl.BlockSpec((tm, 1024), lambda i: (i, cb)),
            pl.BlockSpec((8, 1024), lambda i: (jnp.maximum(i * (tm // 8) - 1, 0), cb)),
            pl.BlockSpec((8, 1024), lambda i: (jnp.minimum((i + 1) * (tm // 8), nb8 - 1), cb))]


def _conv_kq_call(z, w, b, s, tm):
    m = z.shape[0]
    vec = pl.BlockSpec((1, 1024), lambda i: (0, 0))
    return pl.pallas_call(
        _conv_kq_body,
        grid=(m // tm,),
        in_specs=_halo_specs(tm, m, ZC_KQ) + [pl.BlockSpec((3, 1024), lambda i: (0, 0)), vec, vec],
        out_specs=pl.BlockSpec((tm, 1024), lambda i: (i, 0)),
        out_shape=jax.ShapeDtypeStruct((m, 1024), BF16),
        compiler_params=_cparams(("arbitrary",), 32),
        name="conv_kq",
    )(z, z, z, w, b, s)


def _conv_hy_body(ac_ref, ap_ref, an_ref, bc_ref, bp_ref, bn_ref, cc_ref, cp_ref, cn_ref,
                  w_ref, b_ref, x0_ref, xv_ref):
    i = pl.program_id(0)
    first, last = i == 0, i == pl.num_programs(0) - 1
    w = w_ref[...]
    b = b_ref[...]

    def cv(c, p, n, k):
        return _conv3(c[...].astype(F32), p[...].astype(F32), n[...].astype(F32),
                      w[:, k * 1024:(k + 1) * 1024], b[:, k * 1024:(k + 1) * 1024], first, last)

    x0_ref[...] = cv(ac_ref, ap_ref, an_ref, 0).astype(BF16)
    xv_ref[...] = (cv(bc_ref, bp_ref, bn_ref, 1) * cv(cc_ref, cp_ref, cn_ref, 2)).astype(BF16)


def _conv_hy_call(z, w, b, tm):
    m = z.shape[0]
    out = pl.BlockSpec((tm, 1024), lambda i: (i, 0))
    return pl.pallas_call(
        _conv_hy_body,
        grid=(m // tm,),
        in_specs=(_halo_specs(tm, m, ZC_X0) + _halo_specs(tm, m, ZC_X1) + _halo_specs(tm, m, ZC_HV)
                  + [pl.BlockSpec((3, 3072), lambda i: (0, 0)), pl.BlockSpec((1, 3072), lambda i: (0, 0))]),
        out_specs=[out, out],
        out_shape=[jax.ShapeDtypeStruct((m, 1024), BF16), jax.ShapeDtypeStruct((m, 1024), BF16)],
        compiler_params=_cparams(("arbitrary",), 32),
        name="conv_hy",
    )(z, z, z, z, z, z, z, z, z, w, b)


def _mlstm_body(cps, kqf_ref, vf_ref, gf_ref, kqb_ref, vb_ref, gb_ref, bg_ref, s0_ref, m0_ref,
                hf_ref, hb_ref, sfin_ref, mfin_ref, s_scr, m_scr):
    j = pl.program_id(0)

    @pl.when(j == 0)
    def _():
        s_scr[...] = s0_ref[...]
        m_scr[...] = m0_ref[...]

    r = lax.broadcasted_iota(I32, (CHUNK, CHUNK), 0)
    c = lax.broadcasted_iota(I32, (CHUNK, CHUNK), 1)
    ones_b = jnp.ones((CHUNK, CHUNK), BF16)
    bg = bg_ref[...]

    def lane_bcast(x, h, width=CHUNK):
        return jnp.broadcast_to(x[:, h:h + 1], (x.shape[0], width))

    def one_chunk(d, off):
        rows = pl.ds(off, CHUNK)
        kq = (kqf_ref, kqb_ref)[d][rows, :]
        v = (vf_ref, vb_ref)[d][rows, :]
        g_all = (gf_ref, gb_ref)[d][rows, :] + bg
        out_ref = (hf_ref, hb_ref)[d]
        tri = (r >= c) if d == 0 else (c >= r)
        tri_b = jnp.where(tri, 1.0, 0.0).astype(BF16)
        gi = g_all if d == 0 else pltpu.roll(g_all, CHUNK - 16, 1)
        gfp = pltpu.roll(g_all, CHUNK - 8 - 16 * d, 1)
        lf = jnp.minimum(gfp, 0.0) - jnp.log(1.0 + jnp.exp(-jnp.abs(gfp)))
        l1 = lf.astype(BF16)
        r1 = lf - l1.astype(F32)
        l2 = r1.astype(BF16)
        l3 = (r1 - l2.astype(F32)).astype(BF16)
        bcum = _dg(tri_b, l1) + _dg(tri_b, l2) + _dg(tri_b, l3)
        gtot = bcum[CHUNK - 1:CHUNK, :] if d == 0 else bcum[0:1, :]
        acol = gtot - bcum + gi
        m_loc = jnp.max(acol, axis=0, keepdims=True)
        m_st = m_scr[d, 0:1, :]
        m_new = jnp.maximum(gtot + m_st, m_loc)
        sp8 = jnp.broadcast_to(jnp.exp(gtot + m_st - m_new), (8, CHUNK))
        wst = jnp.exp(acol - m_new)
        rr = gi - bcum
        cm = rr
        for sh in (1, 2, 4, 8, 16, 32, 64):
            if d == 0:
                cm = jnp.maximum(cm, jnp.where(r >= sh, pltpu.roll(cm, sh, 0), NEG))
            else:
                cm = jnp.maximum(cm, jnp.where(r < CHUNK - sh, pltpu.roll(cm, CHUNK - sh, 0), NEG))
        mt = jnp.maximum(bcum + m_st, bcum + cm)
        c1 = bcum - mt
        rt = jnp.transpose(rr)
        wt = jnp.transpose(wst)
        m8 = jnp.broadcast_to(m_st, (8, CHUNK))
        kts = {}

        for h in range(N_HEADS):
            p, half = divmod(h, 2)
            lm = (c // QK_HEAD) == half
            kp = kq[:, p * 128:(p + 1) * 128]
            qp = kq[:, QK_W + p * 128:QK_W + (p + 1) * 128]
            vaug = jnp.concatenate([v[:, h * 128:(h + 1) * 128], ones_b], axis=1)
            qm = jnp.where(lm, qp, jnp.zeros_like(qp))
            c1b = lane_bcast(c1, h)
            pm = jnp.exp(jnp.where(tri, c1b + rt[h:h + 1, :], NEG))
            s = (_dg(qm, kp, _NT) * pm).astype(BF16)
            m_in = jnp.tile(lane_bcast(m8, h), (CHUNK // 8, 1))
            qs = (qm.astype(F32) * jnp.exp(c1b + m_in)).astype(BF16)
            st = s_scr[d * N_HEADS + h]
            tot = _dg(jnp.concatenate([s, qs], axis=1), jnp.concatenate([vaug, st.astype(BF16)], axis=0))
            den = jnp.maximum(jnp.abs(tot[:, 128:]), jnp.exp(-lane_bcast(mt, h)))
            out_ref[rows, h * 128:(h + 1) * 128] = (tot[:, :128] / den).astype(BF16)
            if p not in kts:
                kts[p] = jnp.transpose(kp.astype(F32))
            kw = jnp.where((r // QK_HEAD) == half, kts[p] * wt[h:h + 1, :], 0.0).astype(BF16)
            spb = jnp.tile(lane_bcast(sp8, h, 256), (CHUNK // 8, 1))
            s_scr[d * N_HEADS + h] = spb * st + _dg(kw, vaug)
        m_scr[d, 0:1, :] = m_new

    for sub in range(cps):
        one_chunk(0, sub * CHUNK)
        one_chunk(1, (cps - 1 - sub) * CHUNK)

    @pl.when(j == pl.num_programs(0) - 1)
    def _():
        sfin_ref[...] = s_scr[...]
        mfin_ref[...] = m_scr[...]


def _mlstm_call(kq, z, gates, bg, s0, m0):
    m = kq.shape[0]
    cps = math.gcd(m // CHUNK, MLSTM_CHUNKS_PER_STEP)
    rows = CHUNK * cps
    nc = m // rows
    fwd = lambda cb: (lambda j: (j, cb))
    bwd = lambda cb: (lambda j: (nc - 1 - j, cb))
    st_spec = pl.BlockSpec((2 * N_HEADS, CHUNK, 256), lambda j: (0, 0, 0))
    m_spec = pl.BlockSpec((2, 8, 128), lambda j: (0, 0, 0))
    return pl.pallas_call(
        functools.partial(_mlstm_body, cps),
        grid=(nc,),
        in_specs=[pl.BlockSpec((rows, 1024), fwd(0)), pl.BlockSpec((rows, 1024), fwd(ZC_V)),
                  pl.BlockSpec((rows, GATE_PAD), fwd(0)),
                  pl.BlockSpec((rows, 1024), bwd(0)), pl.BlockSpec((rows, 1024), bwd(ZC_V)),
                  pl.BlockSpec((rows, GATE_PAD), bwd(0)),
                  pl.BlockSpec((1, GATE_PAD), lambda j: (0, 0)), st_spec, m_spec],
        out_specs=[pl.BlockSpec((rows, 1024), fwd(0)), pl.BlockSpec((rows, 1024), bwd(0)), st_spec, m_spec],
        out_shape=[jax.ShapeDtypeStruct((m, 1024), BF16), jax.ShapeDtypeStruct((m, 1024), BF16),
                   jax.ShapeDtypeStruct((2 * N_HEADS, CHUNK, 256), F32),
                   jax.ShapeDtypeStruct((2, 8, 128), F32)],
        scratch_shapes=[pltpu.VMEM((2 * N_HEADS, CHUNK, 256), F32), pltpu.VMEM((2, 8, 128), F32)],
        compiler_params=_cparams(("arbitrary",), 32),
        name="mlstm",
    )(kq, z, gates, kq, z, gates, bg, s0, m0)


def _filt_body(seq_len, ft_ref, w1_ref, b1_ref, f1_ref, w2_ref, b2_ref, f2_ref, w3_ref, b3_ref, rt_ref,
               kf_ref, l1_ref):
    i = pl.program_id(0)
    tn = ft_ref.shape[1]
    hp = tn // 2
    reps = tn // 128
    col = lambda ref: jnp.tile(ref[...], (1, reps))
    h1 = jnp.sin(col(f1_ref) * (_dot3(w1_ref[...], ft_ref[...]) + col(b1_ref)))
    h2 = jnp.sin(col(f2_ref) * (_dot3(w2_ref[...], h1) + col(b2_ref)))
    r = lax.broadcasted_iota(I32, (hp, HY_W), 0)
    n_fwd = (8 * i + (r & 7)) * FFT_N2 + (r >> 3)
    rates = rt_ref[...]
    halves = []
    l1 = jnp.zeros((1, HY_W), F32)
    for hx in range(2):
        h = (_dot3(h2[:, hx * hp:(hx + 1) * hp], w3_ref[:, hx * HY_W:(hx + 1) * HY_W], _TN)
             + b3_ref[:, hx * HY_W:(hx + 1) * HY_W])
        n = n_fwd + hx * seq_len
        t01 = jnp.where(n <= seq_len, n, 2 * seq_len - n).astype(F32) / float(max(seq_len - 1, 1))
        h = jnp.where(n == seq_len, 0.0, h * jnp.exp(-t01 * rates))
        l1 = l1 + jnp.sum(jnp.abs(h), axis=0, keepdims=True)
        halves.append(h)
    word = _pack_pair(halves[0], halves[1])
    for b in range(FFT_N2):
        kf_ref[:, b * HY_W:(b + 1) * HY_W] = word[8 * b:8 * b + 8, :]

    @pl.when(i == 0)
    def _():
        l1_ref[...] = jnp.zeros_like(l1_ref)

    l1_ref[...] += l1


def _filt_call(feats_t, w1t, b1, f1, w2t, b2, f2, w3, b3, rates):
    n = feats_t.shape[1]
    seq_len = n // 2
    tn = 2 * 8 * FFT_N2
    c64 = lambda shape: pl.BlockSpec(shape, lambda i: (0, 0))
    return pl.pallas_call(
        functools.partial(_filt_body, seq_len),
        grid=(n // tn,),
        in_specs=[pl.BlockSpec((64, tn), lambda i: (0, i)),
                  c64((64, 64)), c64((64, 128)), c64((64, 128)), c64((64, 64)), c64((64, 128)), c64((64, 128)),
                  c64((64, 2 * HY_W)), c64((1, 2 * HY_W)), c64((1, HY_W))],
        out_specs=[pl.BlockSpec((8, FFT_N2 * HY_W), lambda i: (i, 0)), pl.BlockSpec((1, HY_W), lambda i: (0, 0))],
        out_shape=[jax.ShapeDtypeStruct((seq_len // FFT_N2, FFT_N2 * HY_W), U32),
                   jax.ShapeDtypeStruct((1, HY_W), F32)],
        compiler_params=_cparams(("arbitrary",), 48),
        name="filt",
    )(feats_t, w1t, b1, f1, w2t, b2, f2, w3, b3, rates)


def _fft_consts(n1_rows):
    n = n1_rows * FFT_N2
    kv = n1_rows // 2 + 1
    k1 = np.arange(FFT_KP, dtype=np.float64)
    valid = (k1 < kv).astype(np.float64)
    n1 = np.arange(n1_rows, dtype=np.float64)
    th1 = 2.0 * np.pi * np.outer(k1, n1) / n1_rows
    f1 = np.concatenate([np.cos(th1) * valid[:, None], -np.sin(th1) * valid[:, None]], axis=0)
    n2 = np.arange(FFT_N2, dtype=np.float64)
    tht = 2.0 * np.pi * np.outer(k1, n2) / n
    rep = lambda a: jnp.broadcast_to(jnp.asarray(a, F32)[:, :, None], (FFT_KP, FFT_N2, 128))
    twr = rep(np.cos(tht) * valid[:, None])
    twi = rep(-np.sin(tht) * valid[:, None])
    th2 = 2.0 * np.pi * np.outer(n2, n2) / FFT_N2
    cs, sn = np.cos(th2), np.sin(th2)
    f2p = np.block([[cs, sn], [-sn, cs]])
    f2pc = np.block([[cs, -sn], [sn, cs]])
    wk = np.where((k1 == 0) | (k1 == kv - 1), 1.0, 2.0) * valid / n
    half = n1_rows // 2
    thi = 2.0 * np.pi * np.outer(n1[:half], k1) / n1_rows
    gc = np.cos(thi) * wk[None, :]
    gs = np.sin(thi) * wk[None, :]
    as_bf = lambda a: jnp.asarray(a, F32).astype(BF16)
    return dict(f1=as_bf(f1), twr=twr, twi=twi,
                f2p=as_bf(f2p), f2pc=as_bf(f2pc), gc=as_bf(gc), gs=as_bf(gs))


def _fft1_body(f_ref, x_ref, ar_ref, ai_ref):
    o = _dg(f_ref[...], x_ref[...])
    ar_ref[...] = o[:FFT_KP].astype(BF16)
    ai_ref[...] = o[FFT_KP:].astype(BF16)


def _fft1_packed_body(f_ref, x_ref, ar_ref, ai_ref):
    k = x_ref.shape[0]
    hi, lo = _unpack_pair(x_ref[...])
    o = _dg(f_ref[:, 0:k], hi.astype(BF16)) + _dg(f_ref[:, k:], lo.astype(BF16))
    ar_ref[...] = o[:FFT_KP].astype(BF16)
    ai_ref[...] = o[FFT_KP:].astype(BF16)


def _fft1_call(f1, x2d, cb, packed=False):
    k, cols = x2d.shape
    f1 = f1[:, :2 * k] if packed else f1[:, :k]
    out = pl.BlockSpec((FFT_KP, cb), lambda i: (0, i))
    sh = jax.ShapeDtypeStruct((FFT_KP, cols), BF16)
    return pl.pallas_call(
        _fft1_packed_body if packed else _fft1_body,
        grid=(cols // cb,),
        in_specs=[pl.BlockSpec(f1.shape, lambda i: (0, 0)), pl.BlockSpec((k, cb), lambda i: (0, i))],
        out_specs=[out, out],
        out_shape=[sh, sh],
        compiler_params=_cparams(("arbitrary",), 32),
        name="fft1",
    )(f1, x2d)


def _twiddled(ar_ref, ai_ref, twr_ref, twi_ref, reps):
    a_r = ar_ref[...].astype(F32)
    a_i = ai_ref[...].astype(F32)
    tr = jnp.tile(twr_ref[...], (1, reps))
    ti = jnp.tile(twi_ref[...], (1, reps))
    st = jnp.concatenate([a_r * tr - a_i * ti, a_r * ti + a_i * tr], axis=0).astype(BF16)
    return st, tr, ti


FFT2_ROWS = 2


def _fft2_filt_body(kv, ar_ref, ai_ref, twr_ref, twi_ref, f2p_ref, k_ref):
    for j in range(FFT2_ROWS):
        k1 = pl.program_id(0) * FFT2_ROWS + j

        @pl.when(k1 < kv)
        def _():
            st, _, _ = _twiddled(ar_ref.at[j], ai_ref.at[j], twr_ref.at[j], twi_ref.at[j], ar_ref.shape[-1] // 128)
            k_ref[j] = _dg(f2p_ref[...], st).astype(BF16)

        @pl.when(k1 >= kv)
        def _():
            k_ref[j] = jnp.zeros(k_ref.shape[1:], BF16)


def _fft2_conv_body(kv, ar_ref, ai_ref, twr_ref, twi_ref, k_ref, f2p_ref, f2pc_ref, br_ref, bi_ref):
    for j in range(FFT2_ROWS):
        k1 = pl.program_id(0) * FFT2_ROWS + j

        @pl.when(k1 < kv)
        def _():
            st, tr, ti = _twiddled(ar_ref.at[j], ai_ref.at[j], twr_ref.at[j], twi_ref.at[j], ar_ref.shape[-1] // 128)
            x = _dg(f2p_ref[...], st)
            xr, xi = x[:FFT_N2], x[FFT_N2:]
            kr = k_ref[j, :FFT_N2, :].astype(F32)
            ki = k_ref[j, FFT_N2:, :].astype(F32)
            sy = jnp.concatenate([xr * kr - xi * ki, xr * ki + xi * kr], axis=0).astype(BF16)
            b = _dg(f2pc_ref[...], sy)
            b_r, b_i = b[:FFT_N2], b[FFT_N2:]
            br_ref[j] = (b_r * tr + b_i * ti).astype(BF16)
            bi_ref[j] = (b_i * tr - b_r * ti).astype(BF16)

        @pl.when(k1 >= kv)
        def _():
            br_ref[j] = jnp.zeros(br_ref.shape[1:], BF16)
            bi_ref[j] = jnp.zeros(bi_ref.shape[1:], BF16)


def _fft2_specs(ch, kv):
    src = lambda i: (jnp.minimum(i, (kv - 1) // FFT2_ROWS), 0, 0)
    blk = pl.BlockSpec((FFT2_ROWS, FFT_N2, ch), src)
    tw = pl.BlockSpec((FFT2_ROWS, FFT_N2, 128), src)
    mat = pl.BlockSpec((2 * FFT_N2, 2 * FFT_N2), lambda i: (0, 0))
    return blk, tw, mat, src


def _fft2_filt_call(ar, ai, cst, kv):
    ch = ar.shape[-1]
    blk, tw, mat, _ = _fft2_specs(ch, kv)
    return pl.pallas_call(
        functools.partial(_fft2_filt_body, kv),
        grid=(FFT_KP // FFT2_ROWS,),
        in_specs=[blk, blk, tw, tw, mat],
        out_specs=pl.BlockSpec((FFT2_ROWS, 2 * FFT_N2, ch), lambda i: (i, 0, 0)),
        out_shape=jax.ShapeDtypeStruct((FFT_KP, 2 * FFT_N2, ch), BF16),
        compiler_params=_cparams(("arbitrary",), 32),
        name="fft2_filt",
    )(ar, ai, cst["twr"], cst["twi"], cst["f2p"])


def _fft2_conv_call(ar, ai, khat, cst, kv):
    ch = ar.shape[-1]
    blk, tw, mat, src = _fft2_specs(ch, kv)
    sh = jax.ShapeDtypeStruct((FFT_KP, FFT_N2, ch), BF16)
    out = pl.BlockSpec((FFT2_ROWS, FFT_N2, ch), lambda i: (i, 0, 0))
    return pl.pallas_call(
        functools.partial(_fft2_conv_body, kv),
        grid=(FFT_KP // FFT2_ROWS,),
        in_specs=[blk, blk, tw, tw, pl.BlockSpec((FFT2_ROWS, 2 * FFT_N2, ch), src), mat, mat],
        out_specs=[out, out],
        out_shape=[sh, sh],
        compiler_params=_cparams(("arbitrary",), 32),
        name="fft2_conv",
    )(ar, ai, cst["twr"], cst["twi"], khat, cst["f2p"], cst["f2pc"])


def _ifft1_body(gc_ref, gs_ref, br_ref, bi_ref, il_ref, o_ref):
    y = _dg(gc_ref[...], br_ref[...]) - _dg(gs_ref[...], bi_ref[...])
    o_ref[...] = (y * il_ref[...]).astype(BF16)


def _ifft1_call(cst, br2d, bi2d, il_t, rows, cb):
    cols = br2d.shape[1]
    g = pl.BlockSpec((rows, FFT_KP), lambda i: (0, 0))
    kb = pl.BlockSpec((FFT_KP, cb), lambda i: (0, i))
    xb = pl.BlockSpec((rows, cb), lambda i: (0, i))
    vb = pl.BlockSpec((1, cb), lambda i: (0, 0))
    return pl.pallas_call(
        _ifft1_body,
        grid=(cols // cb,),
        in_specs=[g, g, kb, kb, vb],
        out_specs=xb,
        out_shape=jax.ShapeDtypeStruct((rows, cols), BF16),
        compiler_params=_cparams(("arbitrary",), 32),
        name="ifft1",
    )(cst["gc"], cst["gs"], br2d, bi2d, il_t)


def _outproj_body(hf_ref, hb_ref, zo_ref, yc_ref, x0_ref, xv_ref, ds_ref, x_ref, erow_ref, ecol_ref, gh_ref,
                  wa_ref, wb_ref, g1_ref, gf_ref, sh_ref, sc_ref, wr_ref, w1s_ref, w3s_ref, w2s_ref,
                  x1_ref, h2_ref, s_ref, shared_ref):
    hs = hf_ref[...].astype(F32) + hb_ref[...].astype(F32)
    gh = gh_ref[...]
    parts = []
    for h in range(N_HEADS):
        hh = hs[:, h * 128:(h + 1) * 128]
        ms = jnp.mean(hh * hh, axis=-1, keepdims=True)
        parts.append(hh * lax.rsqrt(ms + EPS) * gh[:, h * 128:(h + 1) * 128])
    ym = jnp.concatenate(parts, axis=-1) * _sigmoid(zo_ref[...].astype(F32))
    yh = x0_ref[...].astype(F32) * (yc_ref[...].astype(F32) + ds_ref[...] * xv_ref[...].astype(F32))
    y = _dg(ym.astype(BF16), wa_ref[...]) + _dg(yh.astype(BF16), wb_ref[...])
    rp = x_ref.shape[0] // GRID_W
    erow8 = erow_ref[...]
    erow = erow8[0:rp, :]
    for q in range(1, 8 // rp):
        erow = jnp.where(pl.program_id(0) % (8 // rp) == q, erow8[q * rp:(q + 1) * rp, :], erow)
    x1 = _add_pos(x_ref[...], erow, ecol_ref[...]) + g1_ref[...] * y
    x1_ref[...] = x1
    h2 = _norm_mod(x1, gf_ref[...], sh_ref[...], sc_ref[...])
    half = h2.shape[1] // 2
    _store_slabs(h2_ref, _pack_pair(h2[:, :half], h2[:, half:]))
    s_ref[...] = _sigmoid(_dot3(wr_ref[...], h2, _NT))
    h2b = h2.astype(BF16)
    a = (_silu(_dg(h2b, w1s_ref[...])) * _dg(h2b, w3s_ref[...])).astype(BF16)
    shared_ref[...] = _dg(a, w2s_ref[...]).astype(BF16)


def _outproj_call(hf, hb, z, yc, x0c, xv, ds, x, erow, ecol, gh, wa, wb, g1, gf, sh2, sc2, wrt, w1s, w3s, w2s, tm):
    m, d = x.shape
    dsh = w1s.shape[1]
    row = lambda cb: pl.BlockSpec((tm, 1024), lambda i: (i, cb))
    vec = lambda n: pl.BlockSpec((1, n), lambda i: (0, 0))
    full = pl.BlockSpec((tm, d), lambda i: (i, 0))
    const = lambda r, c: pl.BlockSpec((r, c), lambda i: (0, 0))
    return pl.pallas_call(
        _outproj_body,
        grid=(m // tm,),
        in_specs=[row(0), row(0), row(ZC_O), row(0), row(0), row(0), vec(HY_W), full,
                  pl.BlockSpec((8, d // 2), lambda i: (i * (tm // GRID_W) // 8, 0)),
                  const(GRID_W, d // 2),
                  vec(MV_W),
                  const(MV_W, d), const(HY_W, d),
                  vec(d), vec(d), vec(d), vec(d),
                  const(N_EXPERTS, d), const(d, dsh), const(d, dsh), const(dsh, d)],
        out_specs=[full, pl.BlockSpec((tm * SLAB, SLAB_W), lambda i: (i, 0)),
                   pl.BlockSpec((N_EXPERTS, tm), lambda i: (0, i)), full],
        out_shape=[jax.ShapeDtypeStruct((m, d), F32), jax.ShapeDtypeStruct((m * SLAB, SLAB_W), U32),
                   jax.ShapeDtypeStruct((N_EXPERTS, m), F32), jax.ShapeDtypeStruct((m, d), BF16)],
        compiler_params=_cparams(("arbitrary",), 56),
        name="outproj",
    )(hf, hb, z, yc, x0c, xv, ds, x, erow, ecol, gh, wa, wb, g1, gf, sh2, sc2, wrt, w1s, w3s, w2s)


def _first_max(x, idx, sentinel):
    m = jnp.max(x, axis=0, keepdims=True)
    return m, jnp.min(jnp.where(x == m, idx, sentinel), axis=0, keepdims=True)


def _route_body(s_ref, b_ref, e_ref, w_ref, r_ref, cnt_ref, u_scr, run_scr):
    i = pl.program_id(0)
    tt = s_ref.shape[1]

    @pl.when(i == 0)
    def _():
        rr = lax.broadcasted_iota(I32, (tt, tt), 0)
        cc = lax.broadcasted_iota(I32, (tt, tt), 1)
        u_scr[...] = jnp.where(rr < cc, 1.0, 0.0).astype(BF16)
        run_scr[...] = jnp.zeros_like(run_scr)

    s = s_ref[...]
    sel = s + b_ref[...][:, 0:1]
    sub8 = lax.broadcasted_iota(I32, (E_PER_GROUP, tt), 0).astype(F32)
    gs = jnp.zeros((N_GROUPS, tt), F32)
    for g in range(N_GROUPS):
        grp = sel[g * E_PER_GROUP:(g + 1) * E_PER_GROUP, :]
        m1, i1 = _first_max(grp, sub8, float(E_PER_GROUP))
        m2 = jnp.max(jnp.where(sub8 == i1, -jnp.inf, grp), axis=0, keepdims=True)
        gs = jnp.where(sub8 == g, m1 + m2, gs)
    gmask = jnp.zeros((N_GROUPS, tt), F32)
    for _ in range(TOPK_GROUPS):
        _, ig = _first_max(gs, sub8, float(N_GROUPS))
        hit = sub8 == ig
        gmask = jnp.where(hit, 1.0, gmask)
        gs = jnp.where(hit, -jnp.inf, gs)
    masked = jnp.concatenate(
        [jnp.where(jnp.broadcast_to(gmask[g:g + 1, :], (E_PER_GROUP, tt)) > 0.5,
                   sel[g * E_PER_GROUP:(g + 1) * E_PER_GROUP, :], -jnp.inf) for g in range(N_GROUPS)], axis=0)
    sub64 = lax.broadcasted_iota(I32, (N_EXPERTS, tt), 0).astype(F32)
    oh = jnp.zeros((N_EXPERTS, tt), F32)
    eks, wks = [], []
    for _ in range(TOP_K):
        _, ie = _first_max(masked, sub64, float(N_EXPERTS))
        hit = sub64 == ie
        wks.append(jnp.sum(jnp.where(hit, s, 0.0), axis=0, keepdims=True))
        eks.append(ie)
        masked = jnp.where(hit, -jnp.inf, masked)
        oh = jnp.where(hit, 1.0, oh)
    wsum = wks[0]
    for k in range(1, TOP_K):
        wsum = wsum + wks[k]
    run = run_scr[...]
    rank_t = _dg(oh.astype(BF16), u_scr[...]) + jnp.tile(run, (1, tt // 128))
    for k in range(TOP_K):
        e_ref[k:k + 1, :] = eks[k].astype(I32)
        w_ref[k:k + 1, :] = wks[k] / wsum * ROUTE_SCALE
        r_ref[k:k + 1, :] = jnp.sum(jnp.where(sub64 == eks[k], rank_t, 0.0), axis=0, keepdims=True).astype(I32)
    run_new = run + jnp.sum(oh, axis=1, keepdims=True)
    run_scr[...] = run_new
    cnt_ref[...] = run_new.astype(I32)


def _route_call(s_t, b_col, tt):
    m = s_t.shape[1]
    out = pl.BlockSpec((TOP_K, tt), lambda i: (0, i))
    return pl.pallas_call(
        _route_body,
        grid=(m // tt,),
        in_specs=[pl.BlockSpec((N_EXPERTS, tt), lambda i: (0, i)),
                  pl.BlockSpec((N_EXPERTS, 128), lambda i: (0, 0))],
        out_specs=[out, out, out, pl.BlockSpec((N_EXPERTS, 128), lambda i: (0, 0))],
        out_shape=[jax.ShapeDtypeStruct((TOP_K, m), I32), jax.ShapeDtypeStruct((TOP_K, m), F32),
                   jax.ShapeDtypeStruct((TOP_K, m), I32), jax.ShapeDtypeStruct((N_EXPERTS, 128), I32)],
        scratch_shapes=[pltpu.VMEM((tt, tt), BF16), pltpu.VMEM((N_EXPERTS, 128), F32)],
        compiler_params=_cparams(("arbitrary",), 32),
        name="route",
    )(s_t, b_col)


def _posk_body(pst_ref, e_ref, r_ref, p_ref):
    e = e_ref[...]
    acc = r_ref[...]
    for x in range(N_EXPERTS):
        acc = acc + jnp.where(e == x, pst_ref[x], 0)
    p_ref[...] = acc


def _posk_call(pstart, eidx, rank):
    k, m = eidx.shape
    tt = min(m, 2048)
    blk = pl.BlockSpec((k, tt), lambda i, pst: (0, i))
    return pl.pallas_call(
        _posk_body,
        grid_spec=pltpu.PrefetchScalarGridSpec(num_scalar_prefetch=1, grid=(m // tt,),
                                               in_specs=[blk, blk], out_specs=blk),
        out_shape=jax.ShapeDtypeStruct((k, m), I32),
        compiler_params=_cparams(("arbitrary",), 32),
        name="posk",
    )(pstart, eidx, rank)


def _slab(ref, r):
    return ref.at[pl.ds(pl.multiple_of(r * SLAB, SLAB), SLAB), :]


def _ffn_packed(x_ref, rows, w1, w3, w2, base=0):
    half = w1.shape[0] // 2
    xa, xb = _unpack_pair(_load_slabs(x_ref, rows, base))
    xa = xa.astype(BF16)
    xb = xb.astype(BF16)
    h1 = _dg(xa, w1[0:half, :]) + _dg(xb, w1[half:, :])
    h3 = _dg(xa, w3[0:half, :]) + _dg(xb, w3[half:, :])
    return _dg((_silu(h1) * h3).astype(BF16), w2[...])


def _dispatch_body(cnt_ref, pst_ref, pcn_ref, h2_ref, pos_ref, xs_ref, zrow, sem):
    i = pl.program_id(0)
    td = h2_ref.shape[0] // SLAB

    def row_copy(t, dst):
        return pltpu.make_async_copy(_slab(h2_ref, t), _slab(xs_ref, dst), sem)

    def issue(t, carry):
        for k in range(TOP_K):
            row_copy(t, pos_ref[k, t]).start(priority=k % 2)
        return carry

    lax.fori_loop(0, td, issue, 0)

    def drain(t, carry):
        for k in range(TOP_K):
            row_copy(0, 0).wait()
        return carry

    lax.fori_loop(0, td, drain, 0)

    @pl.when(i == pl.num_programs(0) - 1)
    def _():
        zrow[...] = jnp.zeros_like(zrow)

        def zero_copy(dst):
            return pltpu.make_async_copy(zrow, _slab(xs_ref, dst), sem)

        def per_expert(e, carry):
            base = pst_ref[e]
            lax.fori_loop(cnt_ref[e], pcn_ref[e], lambda rr, cc: (zero_copy(base + rr).start(), cc)[1], 0)
            lax.fori_loop(cnt_ref[e], pcn_ref[e], lambda rr, cc: (zero_copy(0).wait(), cc)[1], 0)
            return carry

        lax.fori_loop(0, N_EXPERTS, per_expert, 0)


def _dispatch_call(cnt, pstart, pcnt, h2s, pos, rows, td):
    m = h2s.shape[0] // SLAB
    return pl.pallas_call(
        _dispatch_body,
        grid_spec=pltpu.PrefetchScalarGridSpec(
            num_scalar_prefetch=3, grid=(m // td,),
            in_specs=[pl.BlockSpec((td * SLAB, SLAB_W), lambda i, *_: (i, 0)),
                      pl.BlockSpec((TOP_K, td), lambda i, *_: (0, i), memory_space=pltpu.SMEM)],
            out_specs=pl.BlockSpec(memory_space=pl.ANY),
            scratch_shapes=[pltpu.VMEM((SLAB, SLAB_W), U32), pltpu.SemaphoreType.DMA(())]),
        out_shape=jax.ShapeDtypeStruct((rows * SLAB, SLAB_W), U32),
        compiler_params=_cparams(("arbitrary",), 32),
        name="dispatch",
    )(cnt, pstart, pcnt, h2s, pos)


def _moe_body(te_ref, nu_ref, nxt_ref, par_ref, x_ref, w1_ref, w3_ref, w2_ref, y_ref,
              f1, f3, f2, w1b, w3b, w2b, sems):
    def fetch(ex, s):
        return (pltpu.make_async_copy(w1_ref.at[ex], f1.at[s], sems.at[s]),
                pltpu.make_async_copy(w3_ref.at[ex], f3.at[s], sems.at[s]),
                pltpu.make_async_copy(w2_ref.at[ex], f2.at[s], sems.at[s]))

    for sub in range(MOE_TILES_PER_STEP):
        ti = pl.program_id(0) * MOE_TILES_PER_STEP + sub
        used = ti < nu_ref[0]
        e = te_ref[ti]
        first = jnp.logical_or(ti == 0, e != te_ref[jnp.maximum(ti - 1, 0)])
        slot = par_ref[e]

        @pl.when(jnp.logical_and(used, ti == 0))
        def _():
            for c in fetch(e, slot):
                c.start()

        @pl.when(jnp.logical_and(used, first))
        def _():
            for c in fetch(e, slot):
                c.wait()
            nx = nxt_ref[e]

            @pl.when(nx < N_EXPERTS)
            def _():
                for c in fetch(nx, 1 - slot):
                    c.start()

            w1b[...] = f1[slot].astype(BF16)
            w3b[...] = f3[slot].astype(BF16)
            w2b[...] = f2[slot].astype(BF16)

        @pl.when(used)
        def _():
            half = w1b.shape[0] // 2
            base = sub * MOE_ROWS * SLAB
            y = _ffn_packed(x_ref, MOE_ROWS, w1b, w3b, w2b, base)
            _store_slabs(y_ref, _pack_pair(y[:, :half], y[:, half:]), base)


def _moe_call(tile_e, n_used, nxt, par, xs, w1, w3, w2):
    rows = xs.shape[0] // SLAB
    step_rows = MOE_ROWS * MOE_TILES_PER_STEP
    nt = rows // step_rows
    d, de = w1.shape[-2:]
    rmap = lambda i, te, nu, *_: (jnp.minimum(i, (nu[0] - 1) // MOE_TILES_PER_STEP), 0)
    hbm = pl.BlockSpec(memory_space=pl.ANY)
    return pl.pallas_call(
        _moe_body,
        grid_spec=pltpu.PrefetchScalarGridSpec(
            num_scalar_prefetch=4, grid=(nt,),
            in_specs=[pl.BlockSpec((step_rows * SLAB, SLAB_W), rmap), hbm, hbm, hbm],
            out_specs=pl.BlockSpec((step_rows * SLAB, SLAB_W), rmap),
            scratch_shapes=[pltpu.VMEM((2, d, de), F32), pltpu.VMEM((2, d, de), F32), pltpu.VMEM((2, de, d), F32),
                            pltpu.VMEM((d, de), BF16), pltpu.VMEM((d, de), BF16), pltpu.VMEM((de, d), BF16),
                            pltpu.SemaphoreType.DMA((2,))]),
        out_shape=jax.ShapeDtypeStruct((rows * SLAB, SLAB_W), U32),
        compiler_params=_cparams(("arbitrary",), 56),
        name="moe",
    )(tile_e, n_used, nxt, par, xs, w1, w3, w2)


def _final_body(x1_ref, sh_ref, pos_ref, posn_ref, wt_ref, ys_ref, g2_ref, gn_ref, o_ref,
                ybuf, acc_a, acc_b, sems):
    i = pl.program_id(0)
    n = pl.num_programs(0)
    tf, d = x1_ref.shape
    slot = i % 2
    slot_slabs = TOP_K * tf

    def row_copy(p_ref, s, k, t):
        return pltpu.make_async_copy(_slab(ys_ref, p_ref[k, t]), _slab(ybuf, s * slot_slabs + k * tf + t),
                                     sems.at[s])

    def issue_token(p_ref, s, t):
        for k in range(TOP_K):
            row_copy(p_ref, s, k, t).start(priority=k % 2)

    @pl.when(i == 0)
    def _():
        lax.fori_loop(0, tf, lambda t, c: (issue_token(pos_ref, 0, t), c)[1], 0)

    def drain(t, carry):
        for k in range(TOP_K):
            pltpu.make_async_copy(_slab(ys_ref, 0), _slab(ybuf, 0), sems.at[slot]).wait()
        return carry

    lax.fori_loop(0, tf, drain, 0)

    def combine_token(t):
        a = jnp.zeros((SLAB, SLAB_W), F32)
        b = jnp.zeros((SLAB, SLAB_W), F32)
        for k in range(TOP_K):
            ya, yb = _unpack_pair(_slab(ybuf, slot * slot_slabs + k * tf + t)[...])
            w = wt_ref[k, t]
            a = a + w * ya
            b = b + w * yb
        _slab(acc_a, t)[...] = a
        _slab(acc_b, t)[...] = b

    @pl.when(i + 1 < n)
    def _():
        def body(t, carry):
            combine_token(t)
            issue_token(posn_ref, 1 - slot, t)
            return carry

        lax.fori_loop(0, tf, body, 0)

    @pl.when(i + 1 >= n)
    def _():
        lax.fori_loop(0, tf, lambda t, c: (combine_token(t), c)[1], 0)

    moe = jnp.concatenate([_load_slabs(acc_a, tf), _load_slabs(acc_b, tf)], axis=1) + sh_ref[...].astype(F32)
    xo = x1_ref[...] + g2_ref[...] * moe
    ms = jnp.mean(xo * xo, axis=-1, keepdims=True)
    o_ref[...] = xo * lax.rsqrt(ms + EPS) * gn_ref[...]


def _final_call(x1, sh, pos, wts, ys, g2, gn, tf):
    m, d = x1.shape
    nt = m // tf
    full = pl.BlockSpec((tf, d), lambda i: (i, 0))
    vec = pl.BlockSpec((1, d), lambda i: (0, 0))
    return pl.pallas_call(
        _final_body,
        grid=(nt,),
        in_specs=[full, full,
                  pl.BlockSpec((TOP_K, tf), lambda i: (0, i), memory_space=pltpu.SMEM),
                  pl.BlockSpec((TOP_K, tf), lambda i: (0, jnp.minimum(i + 1, nt - 1)), memory_space=pltpu.SMEM),
                  pl.BlockSpec((TOP_K, tf), lambda i: (0, i), memory_space=pltpu.SMEM),
                  pl.BlockSpec(memory_space=pl.ANY), vec, vec],
        out_specs=full,
        out_shape=jax.ShapeDtypeStruct((m, d), F32),
        scratch_shapes=[pltpu.VMEM((2 * TOP_K * tf * SLAB, SLAB_W), U32),
                        pltpu.VMEM((tf * SLAB, SLAB_W), F32), pltpu.VMEM((tf * SLAB, SLAB_W), F32),
                        pltpu.SemaphoreType.DMA((2,))],
        compiler_params=_cparams(("arbitrary",), 48),
        name="final",
    )(x1, sh, pos, pos, wts, ys, g2, gn)


def _pos_tables(n_tokens):
    rows = n_tokens // GRID_W
    quarter = D_MODEL // 4
    omega = 1.0 / (10000.0 ** (jnp.arange(quarter, dtype=F32) / quarter))

    def emb1d(pos):
        ang = pos[:, None] * omega[None]
        return jnp.concatenate([jnp.sin(ang), jnp.cos(ang)], axis=-1)

    return emb1d(jnp.arange(rows, dtype=F32)), emb1d(jnp.arange(GRID_W, dtype=F32))


def _filter_feats(L):
    n1h = L // FFT_N2
    i_, h_, b_, a_ = jnp.meshgrid(jnp.arange(n1h // 8, dtype=I32), jnp.arange(2, dtype=I32),
                                  jnp.arange(FFT_N2, dtype=I32), jnp.arange(8, dtype=I32), indexing="ij")
    n = ((8 * i_ + a_ + h_ * n1h) * FFT_N2 + b_).reshape(-1)
    t = jnp.where(n <= L, n, 2 * L - n).astype(F32)
    t01 = t / max(L - 1, 1)
    w = 2.0 * math.pi * t / L
    bands = jnp.linspace(1e-4, FILT_BANDS - 1, FILT_BANDS, dtype=F32)
    feats = jnp.concatenate([t01[None, :], jnp.cos(bands[:, None] * w[None, :]), -jnp.sin(bands[:, None] * w[None, :]),
                             jnp.zeros((64 - 33, 2 * L), F32)], axis=0)
    return feats


def _pad_rows(a, rows):
    return jnp.concatenate([a, jnp.zeros((rows - a.shape[0],) + a.shape[1:], a.dtype)], axis=0)


def _layer(x, c, ctx, c_ctx, w_ada, b_ada, g_mix, g_ffn, w_in, b_gates, conv_k_w, conv_k_b,
           conv_q_w, conv_q_b, g_head, conv_hy_w, conv_hy_b, filt_w1, filt_b1, filt_freq1,
           filt_w2, filt_b2, filt_freq2, filt_w3, filt_b3, hy_dskip, w_out, w_router, b_router,
           w1_e, w3_e, w2_e, w1_s, w3_s, w2_s, g_final):
    L, d = x.shape
    lc = ctx.shape[0]
    row = lambda v: v.reshape(1, -1)

    cc = jnp.broadcast_to(jnp.stack([c, c_ctx], axis=0)[:, :, None], (2, d, 128))
    mods = _mod_call(cc, w_ada, row(b_ada))
    sh1, sc1, g1, sh2, sc2, g2 = [mods[0:1, k * d:(k + 1) * d] for k in range(6)]
    csh1, csc1 = mods[1:2, 0:d], mods[1:2, d:2 * d]

    w_r, w_g = _wprep_call(jnp.transpose(w_in[0]))
    bg = jnp.concatenate([b_gates, jnp.zeros((GATE_PAD - 4 * N_HEADS,), F32)]).reshape(1, GATE_PAD)
    e_row, e_col = _pos_tables(L)
    conv_w = jnp.concatenate([conv_k_w, conv_q_w], axis=1)
    conv_b = jnp.concatenate([conv_k_b, conv_q_b]).reshape(1, -1)
    conv_s = jnp.concatenate([jnp.ones((QK_W,), F32), jnp.full((QK_W,), QK_HEAD ** -0.5, F32)]).reshape(1, -1)

    z_c, gt_c = _inproj_call(ctx, jnp.zeros((8, d // 2), F32), e_col, row(g_mix), csh1, csc1, w_r, w_g,
                             use_pos=False, tm=min(lc, 256))
    kq_c = _conv_kq_call(z_c, conv_w, conv_b, conv_s, tm=min(lc, 256))
    s0 = jnp.zeros((2 * N_HEADS, CHUNK, 256), F32)
    m0 = jnp.zeros((2, 8, 128), F32)
    _, _, s_ctx, m_ctx = _mlstm_call(kq_c, z_c, gt_c, bg, s0, m0)

    z, gates = _inproj_call(x, e_row, e_col, row(g_mix), sh1, sc1, w_r, w_g, use_pos=True, tm=min(L, 1024))
    kq = _conv_kq_call(z, conv_w, conv_b, conv_s, tm=min(L, 1024))
    x0c, xv = _conv_hy_call(z, conv_hy_w, row(conv_hy_b), tm=min(L, 512))
    hf, hb, _, _ = _mlstm_call(kq, z, gates, bg, s_ctx, m_ctx)

    n1 = 2 * L // FFT_N2
    cst = _fft_consts(n1)
    rates = jnp.linspace(-math.log(DECAY_TARGET) / SLOW_DECAY_PCT, -math.log(DECAY_TARGET) / FAST_DECAY_PCT,
                         HY_W, dtype=F32).reshape(1, -1)
    w1t = jnp.transpose(_pad_rows(filt_w1, 64))
    colrep = lambda v: jnp.broadcast_to(v.reshape(-1, 1), (v.shape[0], 128))
    kf, l1 = _filt_call(_filter_feats(L), w1t, colrep(filt_b1), colrep(filt_freq1), jnp.transpose(filt_w2),
                        colrep(filt_b2), colrep(filt_freq2), filt_w3, row(filt_b3), rates)
    cols = FFT_N2 * HY_W
    cb = 2048
    kar, kai = _fft1_call(cst["f1"], kf, cb, packed=True)
    kv = n1 // 2 + 1
    khat = _fft2_filt_call(kar.reshape(FFT_KP, FFT_N2, HY_W), kai.reshape(FFT_KP, FFT_N2, HY_W), cst, kv)
    uar, uai = _fft1_call(cst["f1"], xv.reshape(n1 // 2, cols), cb)
    br, bi = _fft2_conv_call(uar.reshape(FFT_KP, FFT_N2, HY_W), uai.reshape(FFT_KP, FFT_N2, HY_W), khat, cst, kv)
    reps = cb // HY_W
    il_t = jnp.tile(1.0 / l1, (1, reps))
    yc = _ifft1_call(cst, br.reshape(FFT_KP, cols), bi.reshape(FFT_KP, cols), il_t, n1 // 2, cb).reshape(L, HY_W)

    wo = w_out.astype(BF16)
    x1, h2s, s_t, sh = _outproj_call(hf, hb, z, yc, x0c, xv, row(hy_dskip), x, e_row, e_col, row(g_head),
                                    wo[:MV_W], wo[MV_W:], g1, row(g_ffn), sh2, sc2, jnp.transpose(w_router),
                                    w1_s.astype(BF16), w3_s.astype(BF16), w2_s.astype(BF16), tm=min(L, 256))

    b_col = jnp.broadcast_to(b_router.reshape(N_EXPERTS, 1), (N_EXPERTS, 128))
    eidx, wts, rank, cnt2 = _route_call(s_t, b_col, tt=min(L, 1024))
    cnt = cnt2[:, 0]
    pcnt = (cnt + MOE_ROWS - 1) // MOE_ROWS * MOE_ROWS
    pend = jnp.cumsum(pcnt)
    pstart = pend - pcnt
    rows = L * TOP_K + N_EXPERTS * MOE_ROWS
    nt = rows // MOE_ROWS
    tile_row = jnp.arange(nt, dtype=I32) * MOE_ROWS
    tile_e = jnp.minimum(jnp.sum((pend[None, :] <= tile_row[:, None]).astype(I32), axis=1), N_EXPERTS - 1)
    n_used = (pend[-1] // MOE_ROWS).astype(I32).reshape(1)
    pos = _posk_call(pstart.astype(I32), eidx, rank)

    xs = _dispatch_call(cnt, pstart.astype(I32), pcnt.astype(I32), h2s, pos, rows, td=min(L, 512))
    ex = jnp.arange(N_EXPERTS, dtype=I32)
    nonempty = pcnt > 0
    nxt = jnp.min(jnp.where((ex[None, :] > ex[:, None]) & nonempty[None, :], ex[None, :], N_EXPERTS), axis=1)
    par = (jnp.cumsum(nonempty.astype(I32)) + 1) % 2
    ys = _moe_call(tile_e, n_used, nxt.astype(I32), par.astype(I32), xs, w1_e, w3_e, w2_e)
    return _final_call(x1, sh, pos, wts, ys, g2, row(g_final), tf=min(L, 256))


def kernel(x, c, ctx, c_ctx, w_ada, b_ada, g_mix, g_ffn, w_in, b_gates, conv_k_w, conv_k_b, conv_q_w,
           conv_q_b, g_head, conv_hy_w, conv_hy_b, filt_w1, filt_b1, filt_freq1, filt_w2, filt_b2,
           filt_freq2, filt_w3, filt_b3, hy_dskip, w_out, w_router, b_router, w1_e, w3_e, w2_e,
           w1_s, w3_s, w2_s, g_final):
    assert x.shape[0] == 1 and w_ada.shape[0] == 1, "one batch element, one layer"
    out = _layer(x[0], c[0], ctx[0], c_ctx, w_ada[0], b_ada[0], g_mix[0], g_ffn[0], w_in, b_gates[0],
                 conv_k_w[0], conv_k_b[0], conv_q_w[0], conv_q_b[0], g_head[0], conv_hy_w[0], conv_hy_b[0],
                 filt_w1[0], filt_b1[0], filt_freq1[0], filt_w2[0], filt_b2[0], filt_freq2[0], filt_w3[0],
                 filt_b3[0], hy_dskip[0], w_out[0], w_router[0], b_router[0], w1_e[0], w3_e[0], w2_e[0],
                 w1_s[0], w3_s[0], w2_s[0], g_final)
    return out[None]
```

```python
import functools
import math

import numpy as np
import jax
import jax.numpy as jnp
from jax import lax
from jax.experimental import pallas as pl
from jax.experimental.pallas import tpu as pltpu

F32 = jnp.float32
BF16 = jnp.bfloat16
I32 = jnp.int32
U32 = jnp.uint32

D_MODEL = 2048
GRID_W = 64
N_HEADS = 8
QK_HEAD = 64
V_HEAD = 128
QK_W = N_HEADS * QK_HEAD
MV_W = N_HEADS * V_HEAD
HY_W = D_MODEL - MV_W
CHUNK = 128
FILT_BANDS = 16
FILT_HIDDEN = 64
DECAY_TARGET = 1e-2
FAST_DECAY_PCT = 0.3
SLOW_DECAY_PCT = 1.5
N_EXPERTS = 64
N_GROUPS = 8
E_PER_GROUP = 8
TOPK_GROUPS = 4
TOP_K = 8
D_EXPERT = 512
ROUTE_SCALE = 2.5
EPS = 1e-6
OFF_K = 0
OFF_V = OFF_K + QK_W
OFF_G = OFF_V + MV_W
OFF_Q = OFF_G + 4 * N_HEADS
OFF_O = OFF_Q + QK_W
OFF_HY = OFF_O + MV_W

ZC_KQ, ZC_V, ZC_O, ZC_X0, ZC_X1, ZC_HV = 0, 1, 2, 3, 4, 5
Z_COLS = 6 * 1024
GATE_PAD = 128

NEG = -1e30
MIB = 1024 * 1024
MXU_DEPTH = 256

FFT_N2 = 128
FFT_KP = 144

MOE_ROWS = 256
MOE_TILES_PER_STEP = 4
MLSTM_CHUNKS_PER_STEP = 4


def _cparams(sem, vmem_mb, flags=None):
    return pltpu.CompilerParams(dimension_semantics=sem, vmem_limit_bytes=vmem_mb * MIB, flags=flags)


def _split2(x):
    hi = x.astype(BF16)
    lo = (x - hi.astype(F32)).astype(BF16)
    return hi, lo


_NN = (((1,), (0,)), ((), ()))
_NT = (((1,), (1,)), ((), ()))
_TN = (((0,), (0,)), ((), ()))


def _dg(a, b, dims=_NN):
    return lax.dot_general(a, b, dims, preferred_element_type=F32)


def _dot3(a, b, dims=_NN):
    ah, al = _split2(a)
    bh, bl = _split2(b)
    ka, kb = dims[0][0][0], dims[0][1][0]
    if 3 * a.shape[ka] <= MXU_DEPTH:
        return _dg(jnp.concatenate([ah, al, ah], axis=ka), jnp.concatenate([bh, bh, bl], axis=kb), dims)
    return _dg(ah, bh, dims) + _dg(al, bh, dims) + _dg(ah, bl, dims)


def _sigmoid(x):
    return 1.0 / (1.0 + jnp.exp(-x))


def _silu(x):
    return x * _sigmoid(x)


def _pack_pair(a, b):
    hi = lax.bitcast_convert_type(a.astype(BF16).astype(F32), U32)
    lo = lax.bitcast_convert_type(b.astype(BF16).astype(F32), U32)
    return hi | (lo >> 16)


def _unpack_pair(w):
    a = lax.bitcast_convert_type(w & jnp.uint32(0xFFFF0000), F32)
    b = lax.bitcast_convert_type(w << 16, F32)
    return a, b


SLAB = 8
SLAB_W = 128


def _store_slabs(ref, w, base=0):
    r = w.shape[0]
    for j in range(SLAB):
        ref[pl.ds(base + j, r, stride=SLAB), :] = w[:, j * SLAB_W:(j + 1) * SLAB_W]


def _load_slabs(ref, r, base=0):
    return jnp.concatenate([ref[pl.ds(base + j, r, stride=SLAB), :] for j in range(SLAB)], axis=1)


def _norm_mod(x, g, sh, sc):
    ms = jnp.mean(x * x, axis=-1, keepdims=True)
    return (x * lax.rsqrt(ms + EPS) * g) * (1.0 + sc) + sh


def _add_pos(x, erow, ecol):
    tm, d = x.shape
    half = d // 2
    parts = []
    for r in range(tm // GRID_W):
        xs = x[r * GRID_W:(r + 1) * GRID_W, :]
        parts.append(jnp.concatenate([xs[:, :half] + erow[r:r + 1, :], xs[:, half:] + ecol], axis=-1))
    return parts[0] if len(parts) == 1 else jnp.concatenate(parts, axis=0)


def _mod_body(cc_ref, w_ref, b_ref, o_ref):
    w = w_ref[...]
    nv = cc_ref.shape[0]
    reps = w.shape[1] // 128
    rows = [jnp.sum(w * jnp.tile(_silu(cc_ref[v]), (1, reps)), axis=0, keepdims=True) for v in range(nv)]
    part = jnp.concatenate(rows + [jnp.zeros((8 - nv, w.shape[1]), F32)], axis=0)

    @pl.when(pl.program_id(0) == 0)
    def _():
        o_ref[...] = jnp.broadcast_to(b_ref[...], o_ref.shape)

    o_ref[...] += part


def _mod_call(cc, w, b):
    d, n = w.shape
    tr = 256
    return pl.pallas_call(
        _mod_body,
        grid=(d // tr,),
        in_specs=[pl.BlockSpec((cc.shape[0], tr, 128), lambda j: (0, j, 0)),
                  pl.BlockSpec((tr, n), lambda j: (j, 0)),
                  pl.BlockSpec((1, n), lambda j: (0, 0))],
        out_specs=pl.BlockSpec((8, n), lambda j: (0, 0)),
        out_shape=jax.ShapeDtypeStruct((8, n), F32),
        compiler_params=_cparams(("arbitrary",), 40),
        name="mod",
    )(cc, w, b)


def _wprep_body(w_ref, wr_ref, wg_ref):
    w = w_ref[...]
    wt = jnp.concatenate([w[OFF_K:OFF_V], w[OFF_Q:OFF_O], w[OFF_V:OFF_G], w[OFF_O:]], axis=0)
    wr_ref[...] = jnp.transpose(wt).astype(BF16)
    g = jnp.concatenate([w[OFF_G:OFF_Q], jnp.zeros((GATE_PAD - 4 * N_HEADS, w.shape[1]), F32)], axis=0)
    wg_ref[...] = jnp.transpose(g)


def _wprep_call(w_t):
    n, d = w_t.shape
    tr = 256
    return pl.pallas_call(
        _wprep_body,
        grid=(d // tr,),
        in_specs=[pl.BlockSpec((n, tr), lambda i: (0, i))],
        out_specs=[pl.BlockSpec((tr, Z_COLS), lambda i: (i, 0)), pl.BlockSpec((tr, GATE_PAD), lambda i: (i, 0))],
        out_shape=[jax.ShapeDtypeStruct((d, Z_COLS), BF16), jax.ShapeDtypeStruct((d, GATE_PAD), F32)],
        compiler_params=_cparams(("arbitrary",), 32),
        name="wprep",
    )(w_t)


def _inproj_body(use_pos, x_ref, erow_ref, ecol_ref, gm_ref, sh_ref, sc_ref, w_ref, wg_ref,
                 z_ref, g_ref, h_scr):
    @pl.when(pl.program_id(1) == 0)
    def _():
        x = x_ref[...]
        if use_pos:
            x = _add_pos(x, erow_ref[...], ecol_ref[...])
        h = _norm_mod(x, gm_ref[...], sh_ref[...], sc_ref[...])
        h_scr[...] = h.astype(BF16)
        g_ref[...] = _dg(h_scr[...], wg_ref[...].astype(BF16))

    z_ref[...] = jnp.dot(h_scr[...], w_ref[...], preferred_element_type=F32).astype(BF16)


def _inproj_call(x, erow, ecol, gm, sh, sc, w, wg, use_pos, tm):
    m, d = x.shape
    tn = 1024
    er = tm // GRID_W if use_pos else erow.shape[0]
    row_map = (lambda i, j: (i, 0)) if use_pos else (lambda i, j: (0, 0))
    return pl.pallas_call(
        functools.partial(_inproj_body, use_pos),
        grid=(m // tm, Z_COLS // tn),
        in_specs=[pl.BlockSpec((tm, d), lambda i, j: (i, 0)),
                  pl.BlockSpec((er, d // 2), row_map),
                  pl.BlockSpec((GRID_W, d // 2), lambda i, j: (0, 0)),
                  pl.BlockSpec((1, d), lambda i, j: (0, 0)),
                  pl.BlockSpec((1, d), lambda i, j: (0, 0)),
                  pl.BlockSpec((1, d), lambda i, j: (0, 0)),
                  pl.BlockSpec((d, tn), lambda i, j: (0, j)),
                  pl.BlockSpec((d, GATE_PAD), lambda i, j: (0, 0))],
        out_specs=[pl.BlockSpec((tm, tn), lambda i, j: (i, j)),
                   pl.BlockSpec((tm, GATE_PAD), lambda i, j: (i, 0))],
        out_shape=[jax.ShapeDtypeStruct((m, Z_COLS), BF16),
                   jax.ShapeDtypeStruct((m, GATE_PAD), F32)],
        scratch_shapes=[pltpu.VMEM((tm, d), BF16)],
        compiler_params=_cparams(("arbitrary", "arbitrary"), 48),
        name="inproj",
    )(x, erow, ecol, gm, sh, sc, w, wg)


def _conv3(zc, zp, zn, w, b, first, last):
    tm = zc.shape[0]
    row = lax.broadcasted_iota(I32, zc.shape, 0)
    prev_row = jnp.where(first, 0.0, zp[7:8, :])
    next_row = jnp.where(last, 0.0, zn[0:1, :])
    xm = jnp.where(row == 0, prev_row, pltpu.roll(zc, 1, 0))
    xp = jnp.where(row == tm - 1, next_row, pltpu.roll(zc, tm - 1, 0))
    return xm * w[0:1, :] + zc * w[1:2, :] + xp * w[2:3, :] + b


def _conv_kq_body(zc_ref, zp_ref, zn_ref, w_ref, b_ref, s_ref, o_ref):
    i = pl.program_id(0)
    u = _conv3(zc_ref[...].astype(F32), zp_ref[...].astype(F32), zn_ref[...].astype(F32),
               w_ref[...], b_ref[...], i == 0, i == pl.num_programs(0) - 1)
    o_ref[...] = (_silu(u) * s_ref[...]).astype(BF16)


def _halo_specs(tm, m, cb):
    nb8 = m // 8
    return [pl.BlockSpec((tm, 1024), lambda i: (i, cb)),
            pl.BlockSpec((8, 1024), lambda i: (jnp.maximum(i * (tm // 8) - 1, 0), cb)),
            pl.BlockSpec((8, 1024), lambda i: (jnp.minimum((i + 1) * (tm // 8), nb8 - 1), cb))]


def _conv_kq_call(z, w, b, s, tm):
    m = z.shape[0]
    vec = pl.BlockSpec((1, 1024), lambda i: (0, 0))
    return pl.pallas_call(
        _conv_kq_body,
        grid=(m // tm,),
        in_specs=_halo_specs(tm, m, ZC_KQ) + [pl.BlockSpec((3, 1024), lambda i: (0, 0)), vec, vec],
        out_specs=pl.BlockSpec((tm, 1024), lambda i: (i, 0)),
        out_shape=jax.ShapeDtypeStruct((m, 1024), BF16),
        compiler_params=_cparams(("arbitrary",), 32),
        name="conv_kq",
    )(z, z, z, w, b, s)


def _conv_hy_body(ac_ref, ap_ref, an_ref, bc_ref, bp_ref, bn_ref, cc_ref, cp_ref, cn_ref,
                  w_ref, b_ref, x0_ref, xv_ref):
    i = pl.program_id(0)
    first, last = i == 0, i == pl.num_programs(0) - 1
    w = w_ref[...]
    b = b_ref[...]

    def cv(c, p, n, k):
        return _conv3(c[...].astype(F32), p[...].astype(F32), n[...].astype(F32),
                      w[:, k * 1024:(k + 1) * 1024], b[:, k * 1024:(k + 1) * 1024], first, last)

    x0_ref[...] = cv(ac_ref, ap_ref, an_ref, 0).astype(BF16)
    xv_ref[...] = (cv(bc_ref, bp_ref, bn_ref, 1) * cv(cc_ref, cp_ref, cn_ref, 2)).astype(BF16)


def _conv_hy_call(z, w, b, tm):
    m = z.shape[0]
    out = pl.BlockSpec((tm, 1024), lambda i: (i, 0))
    return pl.pallas_call(
        _conv_hy_body,
        grid=(m // tm,),
        in_specs=(_halo_specs(tm, m, ZC_X0) + _halo_specs(tm, m, ZC_X1) + _halo_specs(tm, m, ZC_HV)
                  + [pl.BlockSpec((3, 3072), lambda i: (0, 0)), pl.BlockSpec((1, 3072), lambda i: (0, 0))]),
        out_specs=[out, out],
        out_shape=[jax.ShapeDtypeStruct((m, 1024), BF16), jax.ShapeDtypeStruct((m, 1024), BF16)],
        compiler_params=_cparams(("arbitrary",), 32),
        name="conv_hy",
    )(z, z, z, z, z, z, z, z, z, w, b)


def _mlstm_body(cps, kqf_ref, vf_ref, gf_ref, kqb_ref, vb_ref, gb_ref, bg_ref, s0_ref, m0_ref,
                hf_ref, hb_ref, sfin_ref, mfin_ref, s_scr, m_scr):
    j = pl.program_id(0)

    @pl.when(j == 0)
    def _():
        s_scr[...] = s0_ref[...]
        m_scr[...] = m0_ref[...]

    r = lax.broadcasted_iota(I32, (CHUNK, CHUNK), 0)
    c = lax.broadcasted_iota(I32, (CHUNK, CHUNK), 1)
    ones_b = jnp.ones((CHUNK, CHUNK), BF16)
    bg = bg_ref[...]

    def lane_bcast(x, h, width=CHUNK):
        return jnp.broadcast_to(x[:, h:h + 1], (x.shape[0], width))

    def one_chunk(d, off):
        rows = pl.ds(off, CHUNK)
        kq = (kqf_ref, kqb_ref)[d][rows, :]
        v = (vf_ref, vb_ref)[d][rows, :]
        g_all = (gf_ref, gb_ref)[d][rows, :] + bg
        out_ref = (hf_ref, hb_ref)[d]
        tri = (r >= c) if d == 0 else (c >= r)
        tri_b = jnp.where(tri, 1.0, 0.0).astype(BF16)
        gi = g_all if d == 0 else pltpu.roll(g_all, CHUNK - 16, 1)
        gfp = pltpu.roll(g_all, CHUNK - 8 - 16 * d, 1)
        lf = jnp.minimum(gfp, 0.0) - jnp.log(1.0 + jnp.exp(-jnp.abs(gfp)))
        l1 = lf.astype(BF16)
        r1 = lf - l1.astype(F32)
        l2 = r1.astype(BF16)
        l3 = (r1 - l2.astype(F32)).astype(BF16)
        bcum = _dg(tri_b, l1) + _dg(tri_b, l2) + _dg(tri_b, l3)
        gtot = bcum[CHUNK - 1:CHUNK, :] if d == 0 else bcum[0:1, :]
        acol = gtot - bcum + gi
        m_loc = jnp.max(acol, axis=0, keepdims=True)
        m_st = m_scr[d, 0:1, :]
        m_new = jnp.maximum(gtot + m_st, m_loc)
        sp8 = jnp.broadcast_to(jnp.exp(gtot + m_st - m_new), (8, CHUNK))
        wst = jnp.exp(acol - m_new)
        rr = gi - bcum
        cm = rr
        for sh in (1, 2, 4, 8, 16, 32, 64):
            if d == 0:
                cm = jnp.maximum(cm, jnp.where(r >= sh, pltpu.roll(cm, sh, 0), NEG))
            else:
                cm = jnp.maximum(cm, jnp.where(r < CHUNK - sh, pltpu.roll(cm, CHUNK - sh, 0), NEG))
        mt = jnp.maximum(bcum + m_st, bcum + cm)
        c1 = bcum - mt
        rt = jnp.transpose(rr)
        wt = jnp.transpose(wst)
        m8 = jnp.broadcast_to(m_st, (8, CHUNK))
        kts = {}

        for h in range(N_HEADS):
            p, half = divmod(h, 2)
            lm = (c // QK_HEAD) == half
            kp = kq[:, p * 128:(p + 1) * 128]
            qp = kq[:, QK_W + p * 128:QK_W + (p + 1) * 128]
            vaug = jnp.concatenate([v[:, h * 128:(h + 1) * 128], ones_b], axis=1)
            qm = jnp.where(lm, qp, jnp.zeros_like(qp))
            c1b = lane_bcast(c1, h)
            pm = jnp.exp(jnp.where(tri, c1b + rt[h:h + 1, :], NEG))
            s = (_dg(qm, kp, _NT) * pm).astype(BF16)
            m_in = jnp.tile(lane_bcast(m8, h), (CHUNK // 8, 1))
            qs = (qm.astype(F32) * jnp.exp(c1b + m_in)).astype(BF16)
            st = s_scr[d * N_HEADS + h]
            tot = _dg(jnp.concatenate([s, qs], axis=1), jnp.concatenate([vaug, st.astype(BF16)], axis=0))
            den = jnp.maximum(jnp.abs(tot[:, 128:]), jnp.exp(-lane_bcast(mt, h)))
            out_ref[rows, h * 128:(h + 1) * 128] = (tot[:, :128] / den).astype(BF16)
            if p not in kts:
                kts[p] = jnp.transpose(kp.astype(F32))
            kw = jnp.where((r // QK_HEAD) == half, kts[p] * wt[h:h + 1, :], 0.0).astype(BF16)
            spb = jnp.tile(lane_bcast(sp8, h, 256), (CHUNK // 8, 1))
            s_scr[d * N_HEADS + h] = spb * st + _dg(kw, vaug)
        m_scr[d, 0:1, :] = m_new

    for sub in range(cps):
        one_chunk(0, sub * CHUNK)
        one_chunk(1, (cps - 1 - sub) * CHUNK)

    @pl.when(j == pl.num_programs(0) - 1)
    def _():
        sfin_ref[...] = s_scr[...]
        mfin_ref[...] = m_scr[...]


def _mlstm_call(kq, z, gates, bg, s0, m0):
    m = kq.shape[0]
    cps = math.gcd(m // CHUNK, MLSTM_CHUNKS_PER_STEP)
    rows = CHUNK * cps
    nc = m // rows
    fwd = lambda cb: (lambda j: (j, cb))
    bwd = lambda cb: (lambda j: (nc - 1 - j, cb))
    st_spec = pl.BlockSpec((2 * N_HEADS, CHUNK, 256), lambda j: (0, 0, 0))
    m_spec = pl.BlockSpec((2, 8, 128), lambda j: (0, 0, 0))
    return pl.pallas_call(
        functools.partial(_mlstm_body, cps),
        grid=(nc,),
        in_specs=[pl.BlockSpec((rows, 1024), fwd(0)), pl.BlockSpec((rows, 1024), fwd(ZC_V)),
                  pl.BlockSpec((rows, GATE_PAD), fwd(0)),
                  pl.BlockSpec((rows, 1024), bwd(0)), pl.BlockSpec((rows, 1024), bwd(ZC_V)),
                  pl.BlockSpec((rows, GATE_PAD), bwd(0)),
                  pl.BlockSpec((1, GATE_PAD), lambda j: (0, 0)), st_spec, m_spec],
        out_specs=[pl.BlockSpec((rows, 1024), fwd(0)), pl.BlockSpec((rows, 1024), bwd(0)), st_spec, m_spec],
        out_shape=[jax.ShapeDtypeStruct((m, 1024), BF16), jax.ShapeDtypeStruct((m, 1024), BF16),
                   jax.ShapeDtypeStruct((2 * N_HEADS, CHUNK, 256), F32),
                   jax.ShapeDtypeStruct((2, 8, 128), F32)],
        scratch_shapes=[pltpu.VMEM((2 * N_HEADS, CHUNK, 256), F32), pltpu.VMEM((2, 8, 128), F32)],
        compiler_params=_cparams(("arbitrary",), 32),
        name="mlstm",
    )(kq, z, gates, kq, z, gates, bg, s0, m0)


def _filt_body(seq_len, ft_ref, w1_ref, b1_ref, f1_ref, w2_ref, b2_ref, f2_ref, w3_ref, b3_ref, rt_ref,
               kf_ref, l1_ref):
    i = pl.program_id(0)
    tn = ft_ref.shape[1]
    hp = tn // 2
    reps = tn // 128
    col = lambda ref: jnp.tile(ref[...], (1, reps))
    h1 = jnp.sin(col(f1_ref) * (_dot3(w1_ref[...], ft_ref[...]) + col(b1_ref)))
    h2 = jnp.sin(col(f2_ref) * (_dot3(w2_ref[...], h1) + col(b2_ref)))
    r = lax.broadcasted_iota(I32, (hp, HY_W), 0)
    n_fwd = (8 * i + (r & 7)) * FFT_N2 + (r >> 3)
    rates = rt_ref[...]
    halves = []
    l1 = jnp.zeros((1, HY_W), F32)
    for hx in range(2):
        h = (_dot3(h2[:, hx * hp:(hx + 1) * hp], w3_ref[:, hx * HY_W:(hx + 1) * HY_W], _TN)
             + b3_ref[:, hx * HY_W:(hx + 1) * HY_W])
        n = n_fwd + hx * seq_len
        t01 = jnp.where(n <= seq_len, n, 2 * seq_len - n).astype(F32) / float(max(seq_len - 1, 1))
        h = jnp.where(n == seq_len, 0.0, h * jnp.exp(-t01 * rates))
        l1 = l1 + jnp.sum(jnp.abs(h), axis=0, keepdims=True)
        halves.append(h)
    word = _pack_pair(halves[0], halves[1])
    for b in range(FFT_N2):
        kf_ref[:, b * HY_W:(b + 1) * HY_W] = word[8 * b:8 * b + 8, :]

    @pl.when(i == 0)
    def _():
        l1_ref[...] = jnp.zeros_like(l1_ref)

    l1_ref[...] += l1


def _filt_call(feats_t, w1t, b1, f1, w2t, b2, f2, w3, b3, rates):
    n = feats_t.shape[1]
    seq_len = n // 2
    tn = 2 * 8 * FFT_N2
    c64 = lambda shape: pl.BlockSpec(shape, lambda i: (0, 0))
    return pl.pallas_call(
        functools.partial(_filt_body, seq_len),
        grid=(n // tn,),
        in_specs=[pl.BlockSpec((64, tn), lambda i: (0, i)),
                  c64((64, 64)), c64((64, 128)), c64((64, 128)), c64((64, 64)), c64((64, 128)), c64((64, 128)),
                  c64((64, 2 * HY_W)), c64((1, 2 * HY_W)), c64((1, HY_W))],
        out_specs=[pl.BlockSpec((8, FFT_N2 * HY_W), lambda i: (i, 0)), pl.BlockSpec((1, HY_W), lambda i: (0, 0))],
        out_shape=[jax.ShapeDtypeStruct((seq_len // FFT_N2, FFT_N2 * HY_W), U32),
                   jax.ShapeDtypeStruct((1, HY_W), F32)],
        compiler_params=_cparams(("arbitrary",), 48),
        name="filt",
    )(feats_t, w1t, b1, f1, w2t, b2, f2, w3, b3, rates)


def _fft_consts(n1_rows):
    n = n1_rows * FFT_N2
    kv = n1_rows // 2 + 1
    k1 = np.arange(FFT_KP, dtype=np.float64)
    valid = (k1 < kv).astype(np.float64)
    n1 = np.arange(n1_rows, dtype=np.float64)
    th1 = 2.0 * np.pi * np.outer(k1, n1) / n1_rows
    f1 = np.concatenate([np.cos(th1) * valid[:, None], -np.sin(th1) * valid[:, None]], axis=0)
    n2 = np.arange(FFT_N2, dtype=np.float64)
    tht = 2.0 * np.pi * np.outer(k1, n2) / n
    rep = lambda a: jnp.broadcast_to(jnp.asarray(a, F32)[:, :, None], (FFT_KP, FFT_N2, 128))
    twr = rep(np.cos(tht) * valid[:, None])
    twi = rep(-np.sin(tht) * valid[:, None])
    th2 = 2.0 * np.pi * np.outer(n2, n2) / FFT_N2
    cs, sn = np.cos(th2), np.sin(th2)
    f2p = np.block([[cs, sn], [-sn, cs]])
    f2pc = np.block([[cs, -sn], [sn, cs]])
    wk = np.where((k1 == 0) | (k1 == kv - 1), 1.0, 2.0) * valid / n
    half = n1_rows // 2
    thi = 2.0 * np.pi * np.outer(n1[:half], k1) / n1_rows
    gc = np.cos(thi) * wk[None, :]
    gs = np.sin(thi) * wk[None, :]
    as_bf = lambda a: jnp.asarray(a, F32).astype(BF16)
    return dict(f1=as_bf(f1), twr=twr, twi=twi,
                f2p=as_bf(f2p), f2pc=as_bf(f2pc), gc=as_bf(gc), gs=as_bf(gs))


def _fft1_body(f_ref, x_ref, ar_ref, ai_ref):
    o = _dg(f_ref[...], x_ref[...])
    ar_ref[...] = o[:FFT_KP].astype(BF16)
    ai_ref[...] = o[FFT_KP:].astype(BF16)


def _fft1_packed_body(f_ref, x_ref, ar_ref, ai_ref):
    k = x_ref.shape[0]
    hi, lo = _unpack_pair(x_ref[...])
    o = _dg(f_ref[:, 0:k], hi.astype(BF16)) + _dg(f_ref[:, k:], lo.astype(BF16))
    ar_ref[...] = o[:FFT_KP].astype(BF16)
    ai_ref[...] = o[FFT_KP:].astype(BF16)


def _fft1_call(f1, x2d, cb, packed=False):
    k, cols = x2d.shape
    f1 = f1[:, :2 * k] if packed else f1[:, :k]
    out = pl.BlockSpec((FFT_KP, cb), lambda i: (0, i))
    sh = jax.ShapeDtypeStruct((FFT_KP, cols), BF16)
    return pl.pallas_call(
        _fft1_packed_body if packed else _fft1_body,
        grid=(cols // cb,),
        in_specs=[pl.BlockSpec(f1.shape, lambda i: (0, 0)), pl.BlockSpec((k, cb), lambda i: (0, i))],
        out_specs=[out, out],
        out_shape=[sh, sh],
        compiler_params=_cparams(("arbitrary",), 32),
        name="fft1",
    )(f1, x2d)


def _twiddled(ar_ref, ai_ref, twr_ref, twi_ref, reps):
    a_r = ar_ref[...].astype(F32)
    a_i = ai_ref[...].astype(F32)
    tr = jnp.tile(twr_ref[...], (1, reps))
    ti = jnp.tile(twi_ref[...], (1, reps))
    st = jnp.concatenate([a_r * tr - a_i * ti, a_r * ti + a_i * tr], axis=0).astype(BF16)
    return st, tr, ti


FFT2_ROWS = 2


def _fft2_filt_body(kv, ar_ref, ai_ref, twr_ref, twi_ref, f2p_ref, k_ref):
    for j in range(FFT2_ROWS):
        k1 = pl.program_id(0) * FFT2_ROWS + j

        @pl.when(k1 < kv)
        def _():
            st, _, _ = _twiddled(ar_ref.at[j], ai_ref.at[j], twr_ref.at[j], twi_ref.at[j], ar_ref.shape[-1] // 128)
            k_ref[j] = _dg(f2p_ref[...], st).astype(BF16)

        @pl.when(k1 >= kv)
        def _():
            k_ref[j] = jnp.zeros(k_ref.shape[1:], BF16)


def _fft2_conv_body(kv, ar_ref, ai_ref, twr_ref, twi_ref, k_ref, f2p_ref, f2pc_ref, br_ref, bi_ref):
    for j in range(FFT2_ROWS):
        k1 = pl.program_id(0) * FFT2_ROWS + j

        @pl.when(k1 < kv)
        def _():
            st, tr, ti = _twiddled(ar_ref.at[j], ai_ref.at[j], twr_ref.at[j], twi_ref.at[j], ar_ref.shape[-1] // 128)
            x = _dg(f2p_ref[...], st)
            xr, xi = x[:FFT_N2], x[FFT_N2:]
            kr = k_ref[j, :FFT_N2, :].astype(F32)
            ki = k_ref[j, FFT_N2:, :].astype(F32)
            sy = jnp.concatenate([xr * kr - xi * ki, xr * ki + xi * kr], axis=0).astype(BF16)
            b = _dg(f2pc_ref[...], sy)
            b_r, b_i = b[:FFT_N2], b[FFT_N2:]
            br_ref[j] = (b_r * tr + b_i * ti).astype(BF16)
            bi_ref[j] = (b_i * tr - b_r * ti).astype(BF16)

        @pl.when(k1 >= kv)
        def _():
            br_ref[j] = jnp.zeros(br_ref.shape[1:], BF16)
            bi_ref[j] = jnp.zeros(bi_ref.shape[1:], BF16)


def _fft2_specs(ch, kv):
    src = lambda i: (jnp.minimum(i, (kv - 1) // FFT2_ROWS), 0, 0)
    blk = pl.BlockSpec((FFT2_ROWS, FFT_N2, ch), src)
    tw = pl.BlockSpec((FFT2_ROWS, FFT_N2, 128), src)
    mat = pl.BlockSpec((2 * FFT_N2, 2 * FFT_N2), lambda i: (0, 0))
    return blk, tw, mat, src


def _fft2_filt_call(ar, ai, cst, kv):
    ch = ar.shape[-1]
    blk, tw, mat, _ = _fft2_specs(ch, kv)
    return pl.pallas_call(
        functools.partial(_fft2_filt_body, kv),
        grid=(FFT_KP // FFT2_ROWS,),
        in_specs=[blk, blk, tw, tw, mat],
        out_specs=pl.BlockSpec((FFT2_ROWS, 2 * FFT_N2, ch), lambda i: (i, 0, 0)),
        out_shape=jax.ShapeDtypeStruct((FFT_KP, 2 * FFT_N2, ch), BF16),
        compiler_params=_cparams(("arbitrary",), 32),
        name="fft2_filt",
    )(ar, ai, cst["twr"], cst["twi"], cst["f2p"])


def _fft2_conv_call(ar, ai, khat, cst, kv):
    ch = ar.shape[-1]
    blk, tw, mat, src = _fft2_specs(ch, kv)
    sh = jax.ShapeDtypeStruct((FFT_KP, FFT_N2, ch), BF16)
    out = pl.BlockSpec((FFT2_ROWS, FFT_N2, ch), lambda i: (i, 0, 0))
    return pl.pallas_call(
        functools.partial(_fft2_conv_body, kv),
        grid=(FFT_KP // FFT2_ROWS,),
        in_specs=[blk, blk, tw, tw, pl.BlockSpec((FFT2_ROWS, 2 * FFT_N2, ch), src), mat, mat],
        out_specs=[out, out],
        out_shape=[sh, sh],
        compiler_params=_cparams(("arbitrary",), 32),
        name="fft2_conv",
    )(ar, ai, cst["twr"], cst["twi"], khat, cst["f2p"], cst["f2pc"])


def _ifft1_body(gc_ref, gs_ref, br_ref, bi_ref, il_ref, o_ref):
    y = _dg(gc_ref[...], br_ref[...]) - _dg(gs_ref[...], bi_ref[...])
    o_ref[...] = (y * il_ref[...]).astype(BF16)


def _ifft1_call(cst, br2d, bi2d, il_t, rows, cb):
    cols = br2d.shape[1]
    g = pl.BlockSpec((rows, FFT_KP), lambda i: (0, 0))
    kb = pl.BlockSpec((FFT_KP, cb), lambda i: (0, i))
    xb = pl.BlockSpec((rows, cb), lambda i: (0, i))
    vb = pl.BlockSpec((1, cb), lambda i: (0, 0))
    return pl.pallas_call(
        _ifft1_body,
        grid=(cols // cb,),
        in_specs=[g, g, kb, kb, vb],
        out_specs=xb,
        out_shape=jax.ShapeDtypeStruct((rows, cols), BF16),
        compiler_params=_cparams(("arbitrary",), 32),
        name="ifft1",
    )(cst["gc"], cst["gs"], br2d, bi2d, il_t)


def _outproj_body(hf_ref, hb_ref, zo_ref, yc_ref, x0_ref, xv_ref, ds_ref, x_ref, erow_ref, ecol_ref, gh_ref,
                  wa_ref, wb_ref, g1_ref, gf_ref, sh_ref, sc_ref, wr_ref, w1s_ref, w3s_ref, w2s_ref,
                  x1_ref, h2_ref, s_ref, shared_ref):
    hs = hf_ref[...].astype(F32) + hb_ref[...].astype(F32)
    gh = gh_ref[...]
    parts = []
    for h in range(N_HEADS):
        hh = hs[:, h * 128:(h + 1) * 128]
        ms = jnp.mean(hh * hh, axis=-1, keepdims=True)
        parts.append(hh * lax.rsqrt(ms + EPS) * gh[:, h * 128:(h + 1) * 128])
    ym = jnp.concatenate(parts, axis=-1) * _sigmoid(zo_ref[...].astype(F32))
    yh = x0_ref[...].astype(F32) * (yc_ref[...].astype(F32) + ds_ref[...] * xv_ref[...].astype(F32))
    y = _dg(ym.astype(BF16), wa_ref[...]) + _dg(yh.astype(BF16), wb_ref[...])
    rp = x_ref.shape[0] // GRID_W
    erow8 = erow_ref[...]
    erow = erow8[0:rp, :]
    for q in range(1, 8 // rp):
        erow = jnp.where(pl.program_id(0) % (8 // rp) == q, erow8[q * rp:(q + 1) * rp, :], erow)
    x1 = _add_pos(x_ref[...], erow, ecol_ref[...]) + g1_ref[...] * y
    x1_ref[...] = x1
    h2 = _norm_mod(x1, gf_ref[...], sh_ref[...], sc_ref[...])
    half = h2.shape[1] // 2
    _store_slabs(h2_ref, _pack_pair(h2[:, :half], h2[:, half:]))
    s_ref[...] = _sigmoid(_dot3(wr_ref[...], h2, _NT))
    h2b = h2.astype(BF16)
    a = (_silu(_dg(h2b, w1s_ref[...])) * _dg(h2b, w3s_ref[...])).astype(BF16)
    shared_ref[...] = _dg(a, w2s_ref[...]).astype(BF16)


def _outproj_call(hf, hb, z, yc, x0c, xv, ds, x, erow, ecol, gh, wa, wb, g1, gf, sh2, sc2, wrt, w1s, w3s, w2s, tm):
    m, d = x.shape
    dsh = w1s.shape[1]
    row = lambda cb: pl.BlockSpec((tm, 1024), lambda i: (i, cb))
    vec = lambda n: pl.BlockSpec((1, n), lambda i: (0, 0))
    full = pl.BlockSpec((tm, d), lambda i: (i, 0))
    const = lambda r, c: pl.BlockSpec((r, c), lambda i: (0, 0))
    return pl.pallas_call(
        _outproj_body,
        grid=(m // tm,),
        in_specs=[row(0), row(0), row(ZC_O), row(0), row(0), row(0), vec(HY_W), full,
                  pl.BlockSpec((8, d // 2), lambda i: (i * (tm // GRID_W) // 8, 0)),
                  const(GRID_W, d // 2),
                  vec(MV_W),
                  const(MV_W, d), const(HY_W, d),
                  vec(d), vec(d), vec(d), vec(d),
                  const(N_EXPERTS, d), const(d, dsh), const(d, dsh), const(dsh, d)],
        out_specs=[full, pl.BlockSpec((tm * SLAB, SLAB_W), lambda i: (i, 0)),
                   pl.BlockSpec((N_EXPERTS, tm), lambda i: (0, i)), full],
        out_shape=[jax.ShapeDtypeStruct((m, d), F32), jax.ShapeDtypeStruct((m * SLAB, SLAB_W), U32),
                   jax.ShapeDtypeStruct((N_EXPERTS, m), F32), jax.ShapeDtypeStruct((m, d), BF16)],
        compiler_params=_cparams(("arbitrary",), 56),
        name="outproj",
    )(hf, hb, z, yc, x0c, xv, ds, x, erow, ecol, gh, wa, wb, g1, gf, sh2, sc2, wrt, w1s, w3s, w2s)


def _first_max(x, idx, sentinel):
    m = jnp.max(x, axis=0, keepdims=True)
    return m, jnp.min(jnp.where(x == m, idx, sentinel), axis=0, keepdims=True)


def _route_body(s_ref, b_ref, e_ref, w_ref, r_ref, cnt_ref, u_scr, run_scr):
    i = pl.program_id(0)
    tt = s_ref.shape[1]

    @pl.when(i == 0)
    def _():
        rr = lax.broadcasted_iota(I32, (tt, tt), 0)
        cc = lax.broadcasted_iota(I32, (tt, tt), 1)
        u_scr[...] = jnp.where(rr < cc, 1.0, 0.0).astype(BF16)
        run_scr[...] = jnp.zeros_like(run_scr)

    s = s_ref[...]
    sel = s + b_ref[...][:, 0:1]
    sub8 = lax.broadcasted_iota(I32, (E_PER_GROUP, tt), 0).astype(F32)
    gs = jnp.zeros((N_GROUPS, tt), F32)
    for g in range(N_GROUPS):
        grp = sel[g * E_PER_GROUP:(g + 1) * E_PER_GROUP, :]
        m1, i1 = _first_max(grp, sub8, float(E_PER_GROUP))
        m2 = jnp.max(jnp.where(sub8 == i1, -jnp.inf, grp), axis=0, keepdims=True)
        gs = jnp.where(sub8 == g, m1 + m2, gs)
    gmask = jnp.zeros((N_GROUPS, tt), F32)
    for _ in range(TOPK_GROUPS):
        _, ig = _first_max(gs, sub8, float(N_GROUPS))
        hit = sub8 == ig
        gmask = jnp.where(hit, 1.0, gmask)
        gs = jnp.where(hit, -jnp.inf, gs)
    masked = jnp.concatenate(
        [jnp.where(jnp.broadcast_to(gmask[g:g + 1, :], (E_PER_GROUP, tt)) > 0.5,
                   sel[g * E_PER_GROUP:(g + 1) * E_PER_GROUP, :], -jnp.inf) for g in range(N_GROUPS)], axis=0)
    sub64 = lax.broadcasted_iota(I32, (N_EXPERTS, tt), 0).astype(F32)
    oh = jnp.zeros((N_EXPERTS, tt), F32)
    eks, wks = [], []
    for _ in range(TOP_K):
        _, ie = _first_max(masked, sub64, float(N_EXPERTS))
        hit = sub64 == ie
        wks.append(jnp.sum(jnp.where(hit, s, 0.0), axis=0, keepdims=True))
        eks.append(ie)
        masked = jnp.where(hit, -jnp.inf, masked)
        oh = jnp.where(hit, 1.0, oh)
    wsum = wks[0]
    for k in range(1, TOP_K):
        wsum = wsum + wks[k]
    run = run_scr[...]
    rank_t = _dg(oh.astype(BF16), u_scr[...]) + jnp.tile(run, (1, tt // 128))
    for k in range(TOP_K):
        e_ref[k:k + 1, :] = eks[k].astype(I32)
        w_ref[k:k + 1, :] = wks[k] / wsum * ROUTE_SCALE
        r_ref[k:k + 1, :] = jnp.sum(jnp.where(sub64 == eks[k], rank_t, 0.0), axis=0, keepdims=True).astype(I32)
    run_new = run + jnp.sum(oh, axis=1, keepdims=True)
    run_scr[...] = run_new
    cnt_ref[...] = run_new.astype(I32)


def _route_call(s_t, b_col, tt):
    m = s_t.shape[1]
    out = pl.BlockSpec((TOP_K, tt), lambda i: (0, i))
    return pl.pallas_call(
        _route_body,
        grid=(m // tt,),
        in_specs=[pl.BlockSpec((N_EXPERTS, tt), lambda i: (0, i)),
                  pl.BlockSpec((N_EXPERTS, 128), lambda i: (0, 0))],
        out_specs=[out, out, out, pl.BlockSpec((N_EXPERTS, 128), lambda i: (0, 0))],
        out_shape=[jax.ShapeDtypeStruct((TOP_K, m), I32), jax.ShapeDtypeStruct((TOP_K, m), F32),
                   jax.ShapeDtypeStruct((TOP_K, m), I32), jax.ShapeDtypeStruct((N_EXPERTS, 128), I32)],
        scratch_shapes=[pltpu.VMEM((tt, tt), BF16), pltpu.VMEM((N_EXPERTS, 128), F32)],
        compiler_params=_cparams(("arbitrary",), 32),
        name="route",
    )(s_t, b_col)


def _posk_body(pst_ref, e_ref, r_ref, p_ref):
    e = e_ref[...]
    acc = r_ref[...]
    for x in range(N_EXPERTS):
        acc = acc + jnp.where(e == x, pst_ref[x], 0)
    p_ref[...] = acc


def _posk_call(pstart, eidx, rank):
    k, m = eidx.shape
    tt = min(m, 2048)
    blk = pl.BlockSpec((k, tt), lambda i, pst: (0, i))
    return pl.pallas_call(
        _posk_body,
        grid_spec=pltpu.PrefetchScalarGridSpec(num_scalar_prefetch=1, grid=(m // tt,),
                                               in_specs=[blk, blk], out_specs=blk),
        out_shape=jax.ShapeDtypeStruct((k, m), I32),
        compiler_params=_cparams(("arbitrary",), 32),
        name="posk",
    )(pstart, eidx, rank)


def _slab(ref, r):
    return ref.at[pl.ds(pl.multiple_of(r * SLAB, SLAB), SLAB), :]


def _ffn_packed(x_ref, rows, w1, w3, w2, base=0):
    half = w1.shape[0] // 2
    xa, xb = _unpack_pair(_load_slabs(x_ref, rows, base))
    xa = xa.astype(BF16)
    xb = xb.astype(BF16)
    h1 = _dg(xa, w1[0:half, :]) + _dg(xb, w1[half:, :])
    h3 = _dg(xa, w3[0:half, :]) + _dg(xb, w3[half:, :])
    return _dg((_silu(h1) * h3).astype(BF16), w2[...])


def _dispatch_body(cnt_ref, pst_ref, pcn_ref, h2_ref, pos_ref, xs_ref, zrow, sem):
    i = pl.program_id(0)
    td = h2_ref.shape[0] // SLAB

    def row_copy(t, dst):
        return pltpu.make_async_copy(_slab(h2_ref, t), _slab(xs_ref, dst), sem)

    def issue(t, carry):
        for k in range(TOP_K):
            row_copy(t, pos_ref[k, t]).start(priority=k % 2)
        return carry

    lax.fori_loop(0, td, issue, 0)

    def drain(t, carry):
        for k in range(TOP_K):
            row_copy(0, 0).wait()
        return carry

    lax.fori_loop(0, td, drain, 0)

    @pl.when(i == pl.num_programs(0) - 1)
    def _():
        zrow[...] = jnp.zeros_like(zrow)

        def zero_copy(dst):
            return pltpu.make_async_copy(zrow, _slab(xs_ref, dst), sem)

        def per_expert(e, carry):
            base = pst_ref[e]
            lax.fori_loop(cnt_ref[e], pcn_ref[e], lambda rr, cc: (zero_copy(base + rr).start(), cc)[1], 0)
            lax.fori_loop(cnt_ref[e], pcn_ref[e], lambda rr, cc: (zero_copy(0).wait(), cc)[1], 0)
            return carry

        lax.fori_loop(0, N_EXPERTS, per_expert, 0)


def _dispatch_call(cnt, pstart, pcnt, h2s, pos, rows, td):
    m = h2s.shape[0] // SLAB
    return pl.pallas_call(
        _dispatch_body,
        grid_spec=pltpu.PrefetchScalarGridSpec(
            num_scalar_prefetch=3, grid=(m // td,),
            in_specs=[pl.BlockSpec((td * SLAB, SLAB_W), lambda i, *_: (i, 0)),
                      pl.BlockSpec((TOP_K, td), lambda i, *_: (0, i), memory_space=pltpu.SMEM)],
            out_specs=pl.BlockSpec(memory_space=pl.ANY),
            scratch_shapes=[pltpu.VMEM((SLAB, SLAB_W), U32), pltpu.SemaphoreType.DMA(())]),
        out_shape=jax.ShapeDtypeStruct((rows * SLAB, SLAB_W), U32),
        compiler_params=_cparams(("arbitrary",), 32),
        name="dispatch",
    )(cnt, pstart, pcnt, h2s, pos)


def _moe_body(te_ref, nu_ref, nxt_ref, par_ref, x_ref, w1_ref, w3_ref, w2_ref, y_ref,
              f1, f3, f2, w1b, w3b, w2b, sems):
    def fetch(ex, s):
        return (pltpu.make_async_copy(w1_ref.at[ex], f1.at[s], sems.at[s]),
                pltpu.make_async_copy(w3_ref.at[ex], f3.at[s], sems.at[s]),
                pltpu.make_async_copy(w2_ref.at[ex], f2.at[s], sems.at[s]))

    for sub in range(MOE_TILES_PER_STEP):
        ti = pl.program_id(0) * MOE_TILES_PER_STEP + sub
        used = ti < nu_ref[0]
        e = te_ref[ti]
        first = jnp.logical_or(ti == 0, e != te_ref[jnp.maximum(ti - 1, 0)])
        slot = par_ref[e]

        @pl.when(jnp.logical_and(used, ti == 0))
        def _():
            for c in fetch(e, slot):
                c.start()

        @pl.when(jnp.logical_and(used, first))
        def _():
            for c in fetch(e, slot):
                c.wait()
            nx = nxt_ref[e]

            @pl.when(nx < N_EXPERTS)
            def _():
                for c in fetch(nx, 1 - slot):
                    c.start()

            w1b[...] = f1[slot].astype(BF16)
            w3b[...] = f3[slot].astype(BF16)
            w2b[...] = f2[slot].astype(BF16)

        @pl.when(used)
        def _():
            half = w1b.shape[0] // 2
            base = sub * MOE_ROWS * SLAB
            y = _ffn_packed(x_ref, MOE_ROWS, w1b, w3b, w2b, base)
            _store_slabs(y_ref, _pack_pair(y[:, :half], y[:, half:]), base)


def _moe_call(tile_e, n_used, nxt, par, xs, w1, w3, w2):
    rows = xs.shape[0] // SLAB
    step_rows = MOE_ROWS * MOE_TILES_PER_STEP
    nt = rows // step_rows
    d, de = w1.shape[-2:]
    rmap = lambda i, te, nu, *_: (jnp.minimum(i, (nu[0] - 1) // MOE_TILES_PER_STEP), 0)
    hbm = pl.BlockSpec(memory_space=pl.ANY)
    return pl.pallas_call(
        _moe_body,
        grid_spec=pltpu.PrefetchScalarGridSpec(
            num_scalar_prefetch=4, grid=(nt,),
            in_specs=[pl.BlockSpec((step_rows * SLAB, SLAB_W), rmap), hbm, hbm, hbm],
            out_specs=pl.BlockSpec((step_rows * SLAB, SLAB_W), rmap),
            scratch_shapes=[pltpu.VMEM((2, d, de), F32), pltpu.VMEM((2, d, de), F32), pltpu.VMEM((2, de, d), F32),
                            pltpu.VMEM((d, de), BF16), pltpu.VMEM((d, de), BF16), pltpu.VMEM((de, d), BF16),
                            pltpu.SemaphoreType.DMA((2,))]),
        out_shape=jax.ShapeDtypeStruct((rows * SLAB, SLAB_W), U32),
        compiler_params=_cparams(("arbitrary",), 56),
        name="moe",
    )(tile_e, n_used, nxt, par, xs, w1, w3, w2)


def _final_body(x1_ref, sh_ref, pos_ref, posn_ref, wt_ref, ys_ref, g2_ref, gn_ref, o_ref,
                ybuf, acc_a, acc_b, sems):
    i = pl.program_id(0)
    n = pl.num_programs(0)
    tf, d = x1_ref.shape
    slot = i % 2
    slot_slabs = TOP_K * tf

    def row_copy(p_ref, s, k, t):
        return pltpu.make_async_copy(_slab(ys_ref, p_ref[k, t]), _slab(ybuf, s * slot_slabs + k * tf + t),
                                     sems.at[s])

    def issue_token(p_ref, s, t):
        for k in range(TOP_K):
            row_copy(p_ref, s, k, t).start(priority=k % 2)

    @pl.when(i == 0)
    def _():
        lax.fori_loop(0, tf, lambda t, c: (issue_token(pos_ref, 0, t), c)[1], 0)

    def drain(t, carry):
        for k in range(TOP_K):
            pltpu.make_async_copy(_slab(ys_ref, 0), _slab(ybuf, 0), sems.at[slot]).wait()
        return carry

    lax.fori_loop(0, tf, drain, 0)

    def combine_token(t):
        a = jnp.zeros((SLAB, SLAB_W), F32)
        b = jnp.zeros((SLAB, SLAB_W), F32)
        for k in range(TOP_K):
            ya, yb = _unpack_pair(_slab(ybuf, slot * slot_slabs + k * tf + t)[...])
            w = wt_ref[k, t]
            a = a + w * ya
            b = b + w * yb
        _slab(acc_a, t)[...] = a
        _slab(acc_b, t)[...] = b

    @pl.when(i + 1 < n)
    def _():
        def body(t, carry):
            combine_token(t)
            issue_token(posn_ref, 1 - slot, t)
            return carry

        lax.fori_loop(0, tf, body, 0)

    @pl.when(i + 1 >= n)
    def _():
        lax.fori_loop(0, tf, lambda t, c: (combine_token(t), c)[1], 0)

    moe = jnp.concatenate([_load_slabs(acc_a, tf), _load_slabs(acc_b, tf)], axis=1) + sh_ref[...].astype(F32)
    xo = x1_ref[...] + g2_ref[...] * moe
    ms = jnp.mean(xo * xo, axis=-1, keepdims=True)
    o_ref[...] = xo * lax.rsqrt(ms + EPS) * gn_ref[...]


def _final_call(x1, sh, pos, wts, ys, g2, gn, tf):
    m, d = x1.shape
    nt = m // tf
    full = pl.BlockSpec((tf, d), lambda i: (i, 0))
    vec = pl.BlockSpec((1, d), lambda i: (0, 0))
    return pl.pallas_call(
        _final_body,
        grid=(nt,),
        in_specs=[full, full,
                  pl.BlockSpec((TOP_K, tf), lambda i: (0, i), memory_space=pltpu.SMEM),
                  pl.BlockSpec((TOP_K, tf), lambda i: (0, jnp.minimum(i + 1, nt - 1)), memory_space=pltpu.SMEM),
                  pl.BlockSpec((TOP_K, tf), lambda i: (0, i), memory_space=pltpu.SMEM),
                  pl.BlockSpec(memory_space=pl.ANY), vec, vec],
        out_specs=full,
        out_shape=jax.ShapeDtypeStruct((m, d), F32),
        scratch_shapes=[pltpu.VMEM((2 * TOP_K * tf * SLAB, SLAB_W), U32),
                        pltpu.VMEM((tf * SLAB, SLAB_W), F32), pltpu.VMEM((tf * SLAB, SLAB_W), F32),
                        pltpu.SemaphoreType.DMA((2,))],
        compiler_params=_cparams(("arbitrary",), 48),
        name="final",
    )(x1, sh, pos, pos, wts, ys, g2, gn)


def _pos_tables(n_tokens):
    rows = n_tokens // GRID_W
    quarter = D_MODEL // 4
    omega = 1.0 / (10000.0 ** (jnp.arange(quarter, dtype=F32) / quarter))

    def emb1d(pos):
        ang = pos[:, None] * omega[None]
        return jnp.concatenate([jnp.sin(ang), jnp.cos(ang)], axis=-1)

    return emb1d(jnp.arange(rows, dtype=F32)), emb1d(jnp.arange(GRID_W, dtype=F32))


def _filter_feats(L):
    n1h = L // FFT_N2
    i_, h_, b_, a_ = jnp.meshgrid(jnp.arange(n1h // 8, dtype=I32), jnp.arange(2, dtype=I32),
                                  jnp.arange(FFT_N2, dtype=I32), jnp.arange(8, dtype=I32), indexing="ij")
    n = ((8 * i_ + a_ + h_ * n1h) * FFT_N2 + b_).reshape(-1)
    t = jnp.where(n <= L, n, 2 * L - n).astype(F32)
    t01 = t / max(L - 1, 1)
    w = 2.0 * math.pi * t / L
    bands = jnp.linspace(1e-4, FILT_BANDS - 1, FILT_BANDS, dtype=F32)
    feats = jnp.concatenate([t01[None, :], jnp.cos(bands[:, None] * w[None, :]), -jnp.sin(bands[:, None] * w[None, :]),
                             jnp.zeros((64 - 33, 2 * L), F32)], axis=0)
    return feats


def _pad_rows(a, rows):
    return jnp.concatenate([a, jnp.zeros((rows - a.shape[0],) + a.shape[1:], a.dtype)], axis=0)


def _layer(x, c, ctx, c_ctx, w_ada, b_ada, g_mix, g_ffn, w_in, b_gates, conv_k_w, conv_k_b,
           conv_q_w, conv_q_b, g_head, conv_hy_w, conv_hy_b, filt_w1, filt_b1, filt_freq1,
           filt_w2, filt_b2, filt_freq2, filt_w3, filt_b3, hy_dskip, w_out, w_router, b_router,
           w1_e, w3_e, w2_e, w1_s, w3_s, w2_s, g_final):
    L, d = x.shape
    lc = ctx.shape[0]
    row = lambda v: v.reshape(1, -1)

    cc = jnp.broadcast_to(jnp.stack([c, c_ctx], axis=0)[:, :, None], (2, d, 128))
    mods = _mod_call(cc, w_ada, row(b_ada))
    sh1, sc1, g1, sh2, sc2, g2 = [mods[0:1, k * d:(k + 1) * d] for k in range(6)]
    csh1, csc1 = mods[1:2, 0:d], mods[1:2, d:2 * d]

    w_r, w_g = _wprep_call(jnp.transpose(w_in[0]))
    bg = jnp.concatenate([b_gates, jnp.zeros((GATE_PAD - 4 * N_HEADS,), F32)]).reshape(1, GATE_PAD)
    e_row, e_col = _pos_tables(L)
    conv_w = jnp.concatenate([conv_k_w, conv_q_w], axis=1)
    conv_b = jnp.concatenate([conv_k_b, conv_q_b]).reshape(1, -1)
    conv_s = jnp.concatenate([jnp.ones((QK_W,), F32), jnp.full((QK_W,), QK_HEAD ** -0.5, F32)]).reshape(1, -1)

    z_c, gt_c = _inproj_call(ctx, jnp.zeros((8, d // 2), F32), e_col, row(g_mix), csh1, csc1, w_r, w_g,
                             use_pos=False, tm=min(lc, 256))
    kq_c = _conv_kq_call(z_c, conv_w, conv_b, conv_s, tm=min(lc, 256))
    s0 = jnp.zeros((2 * N_HEADS, CHUNK, 256), F32)
    m0 = jnp.zeros((2, 8, 128), F32)
    _, _, s_ctx, m_ctx = _mlstm_call(kq_c, z_c, gt_c, bg, s0, m0)

    z, gates = _inproj_call(x, e_row, e_col, row(g_mix), sh1, sc1, w_r, w_g, use_pos=True, tm=min(L, 1024))
    kq = _conv_kq_call(z, conv_w, conv_b, conv_s, tm=min(L, 1024))
    x0c, xv = _conv_hy_call(z, conv_hy_w, row(conv_hy_b), tm=min(L, 512))
    hf, hb, _, _ = _mlstm_call(kq, z, gates, bg, s_ctx, m_ctx)

    n1 = 2 * L // FFT_N2
    cst = _fft_consts(n1)
    rates = jnp.linspace(-math.log(DECAY_TARGET) / SLOW_DECAY_PCT, -math.log(DECAY_TARGET) / FAST_DECAY_PCT,
                         HY_W, dtype=F32).reshape(1, -1)
    w1t = jnp.transpose(_pad_rows(filt_w1, 64))
    colrep = lambda v: jnp.broadcast_to(v.reshape(-1, 1), (v.shape[0], 128))
    kf, l1 = _filt_call(_filter_feats(L), w1t, colrep(filt_b1), colrep(filt_freq1), jnp.transpose(filt_w2),
                        colrep(filt_b2), colrep(filt_freq2), filt_w3, row(filt_b3), rates)
    cols = FFT_N2 * HY_W
    cb = 2048
    kar, kai = _fft1_call(cst["f1"], kf, cb, packed=True)
    kv = n1 // 2 + 1
    khat = _fft2_filt_call(kar.reshape(FFT_KP, FFT_N2, HY_W), kai.reshape(FFT_KP, FFT_N2, HY_W), cst, kv)
    uar, uai = _fft1_call(cst["f1"], xv.reshape(n1 // 2, cols), cb)
    br, bi = _fft2_conv_call(uar.reshape(FFT_KP, FFT_N2, HY_W), uai.reshape(FFT_KP, FFT_N2, HY_W), khat, cst, kv)
    reps = cb // HY_W
    il_t = jnp.tile(1.0 / l1, (1, reps))
    yc = _ifft1_call(cst, br.reshape(FFT_KP, cols), bi.reshape(FFT_KP, cols), il_t, n1 // 2, cb).reshape(L, HY_W)

    wo = w_out.astype(BF16)
    x1, h2s, s_t, sh = _outproj_call(hf, hb, z, yc, x0c, xv, row(hy_dskip), x, e_row, e_col, row(g_head),
                                    wo[:MV_W], wo[MV_W:], g1, row(g_ffn), sh2, sc2, jnp.transpose(w_router),
                                    w1_s.astype(BF16), w3_s.astype(BF16), w2_s.astype(BF16), tm=min(L, 256))

    b_col = jnp.broadcast_to(b_router.reshape(N_EXPERTS, 1), (N_EXPERTS, 128))
    eidx, wts, rank, cnt2 = _route_call(s_t, b_col, tt=min(L, 1024))
    cnt = cnt2[:, 0]
    pcnt = (cnt + MOE_ROWS - 1) // MOE_ROWS * MOE_ROWS
    pend = jnp.cumsum(pcnt)
    pstart = pend - pcnt
    rows = L * TOP_K + N_EXPERTS * MOE_ROWS
    nt = rows // MOE_ROWS
    tile_row = jnp.arange(nt, dtype=I32) * MOE_ROWS
    tile_e = jnp.minimum(jnp.sum((pend[None, :] <= tile_row[:, None]).astype(I32), axis=1), N_EXPERTS - 1)
    n_used = (pend[-1] // MOE_ROWS).astype(I32).reshape(1)
    pos = _posk_call(pstart.astype(I32), eidx, rank)

    xs = _dispatch_call(cnt, pstart.astype(I32), pcnt.astype(I32), h2s, pos, rows, td=min(L, 512))
    ex = jnp.arange(N_EXPERTS, dtype=I32)
    nonempty = pcnt > 0
    nxt = jnp.min(jnp.where((ex[None, :] > ex[:, None]) & nonempty[None, :], ex[None, :], N_EXPERTS), axis=1)
    par = (jnp.cumsum(nonempty.astype(I32)) + 1) % 2
    ys = _moe_call(tile_e, n_used, nxt.astype(I32), par.astype(I32), xs, w1_e, w3_e, w2_e)
    return _final_call(x1, sh, pos, wts, ys, g2, row(g_final), tf=min(L, 256))


def kernel(x, c, ctx, c_ctx, w_ada, b_ada, g_mix, g_ffn, w_in, b_gates, conv_k_w, conv_k_b, conv_q_w,
           conv_q_b, g_head, conv_hy_w, conv_hy_b, filt_w1, filt_b1, filt_freq1, filt_w2, filt_b2,
           filt_freq2, filt_w3, filt_b3, hy_dskip, w_out, w_router, b_router, w1_e, w3_e, w2_e,
           w1_s, w3_s, w2_s, g_final):
    assert x.shape[0] == 1 and w_ada.shape[0] == 1, "one batch element, one layer"
    out = _layer(x[0], c[0], ctx[0], c_ctx, w_ada[0], b_ada[0], g_mix[0], g_ffn[0], w_in, b_gates[0],
                 conv_k_w[0], conv_k_b[0], conv_q_w[0], conv_q_b[0], g_head[0], conv_hy_w[0], conv_hy_b[0],
                 filt_w1[0], filt_b1[0], filt_freq1[0], filt_w2[0], filt_b2[0], filt_freq2[0], filt_w3[0],
                 filt_b3[0], hy_dskip[0], w_out[0], w_router[0], b_router[0], w1_e[0], w3_e[0], w2_e[0],
                 w1_s[0], w3_s[0], w2_s[0], g_final)
    return out[None]
```

```python
import functools
import math

import numpy as np
import jax
import jax.numpy as jnp
from jax import lax
from jax.experimental import pallas as pl
from jax.experimental.pallas import tpu as pltpu

F32 = jnp.float32
BF16 = jnp.bfloat16
I32 = jnp.int32
U32 = jnp.uint32

D_MODEL = 2048
GRID_W = 64
N_HEADS = 8
QK_HEAD = 64
V_HEAD = 128
QK_W = N_HEADS * QK_HEAD
MV_W = N_HEADS * V_HEAD
HY_W = D_MODEL - MV_W
CHUNK = 128
FILT_BANDS = 16
FILT_HIDDEN = 64
DECAY_TARGET = 1e-2
FAST_DECAY_PCT = 0.3
SLOW_DECAY_PCT = 1.5
N_EXPERTS = 64
N_GROUPS = 8
E_PER_GROUP = 8
TOPK_GROUPS = 4
TOP_K = 8
D_EXPERT = 512
ROUTE_SCALE = 2.5
EPS = 1e-6
OFF_K = 0
OFF_V = OFF_K + QK_W
OFF_G = OFF_V + MV_W
OFF_Q = OFF_G + 4 * N_HEADS
OFF_O = OFF_Q + QK_W
OFF_HY = OFF_O + MV_W

ZC_KQ, ZC_V, ZC_O, ZC_X0, ZC_X1, ZC_HV = 0, 1, 2, 3, 4, 5
Z_COLS = 6 * 1024
GATE_PAD = 128

NEG = -1e30
MIB = 1024 * 1024
MXU_DEPTH = 256

FFT_N2 = 128
FFT_KP = 144

MOE_ROWS = 256
MOE_TILES_PER_STEP = 4
MLSTM_CHUNKS_PER_STEP = 4


def _cparams(sem, vmem_mb, flags=None):
    return pltpu.CompilerParams(dimension_semantics=sem, vmem_limit_bytes=vmem_mb * MIB, flags=flags)


def _split2(x):
    hi = x.astype(BF16)
    lo = (x - hi.astype(F32)).astype(BF16)
    return hi, lo


_NN = (((1,), (0,)), ((), ()))
_NT = (((1,), (1,)), ((), ()))
_TN = (((0,), (0,)), ((), ()))


def _dg(a, b, dims=_NN):
    return lax.dot_general(a, b, dims, preferred_element_type=F32)


def _dot3(a, b, dims=_NN):
    ah, al = _split2(a)
    bh, bl = _split2(b)
    ka, kb = dims[0][0][0], dims[0][1][0]
    if 3 * a.shape[ka] <= MXU_DEPTH:
        return _dg(jnp.concatenate([ah, al, ah], axis=ka), jnp.concatenate([bh, bh, bl], axis=kb), dims)
    return _dg(ah, bh, dims) + _dg(al, bh, dims) + _dg(ah, bl, dims)


def _sigmoid(x):
    return 1.0 / (1.0 + jnp.exp(-x))


def _silu(x):
    return x * _sigmoid(x)


def _pack_pair(a, b):
    hi = lax.bitcast_convert_type(a.astype(BF16).astype(F32), U32)
    lo = lax.bitcast_convert_type(b.astype(BF16).astype(F32), U32)
    return hi | (lo >> 16)


def _unpack_pair(w):
    a = lax.bitcast_convert_type(w & jnp.uint32(0xFFFF0000), F32)
    b = lax.bitcast_convert_type(w << 16, F32)
    return a, b


SLAB = 8
SLAB_W = 128


def _store_slabs(ref, w, base=0):
    r = w.shape[0]
    for j in range(SLAB):
        ref[pl.ds(base + j, r, stride=SLAB), :] = w[:, j * SLAB_W:(j + 1) * SLAB_W]


def _load_slabs(ref, r, base=0):
    return jnp.concatenate([ref[pl.ds(base + j, r, stride=SLAB), :] for j in range(SLAB)], axis=1)


def _norm_mod(x, g, sh, sc):
    ms = jnp.mean(x * x, axis=-1, keepdims=True)
    return (x * lax.rsqrt(ms + EPS) * g) * (1.0 + sc) + sh


def _add_pos(x, erow, ecol):
    tm, d = x.shape
    half = d // 2
    parts = []
    for r in range(tm // GRID_W):
        xs = x[r * GRID_W:(r + 1) * GRID_W, :]
        parts.append(jnp.concatenate([xs[:, :half] + erow[r:r + 1, :], xs[:, half:] + ecol], axis=-1))
    return parts[0] if len(parts) == 1 else jnp.concatenate(parts, axis=0)


def _mod_body(cc_ref, w_ref, b_ref, o_ref):
    w = w_ref[...]
    nv = cc_ref.shape[0]
    reps = w.shape[1] // 128
    rows = [jnp.sum(w * jnp.tile(_silu(cc_ref[v]), (1, reps)), axis=0, keepdims=True) for v in range(nv)]
    part = jnp.concatenate(rows + [jnp.zeros((8 - nv, w.shape[1]), F32)], axis=0)

    @pl.when(pl.program_id(0) == 0)
    def _():
        o_ref[...] = jnp.broadcast_to(b_ref[...], o_ref.shape)

    o_ref[...] += part


def _mod_call(cc, w, b):
    d, n = w.shape
    tr = 256
    return pl.pallas_call(
        _mod_body,
        grid=(d // tr,),
        in_specs=[pl.BlockSpec((cc.shape[0], tr, 128), lambda j: (0, j, 0)),
                  pl.BlockSpec((tr, n), lambda j: (j, 0)),
                  pl.BlockSpec((1, n), lambda j: (0, 0))],
        out_specs=pl.BlockSpec((8, n), lambda j: (0, 0)),
        out_shape=jax.ShapeDtypeStruct((8, n), F32),
        compiler_params=_cparams(("arbitrary",), 40),
        name="mod",
    )(cc, w, b)


def _wprep_body(w_ref, wr_ref, wg_ref):
    w = w_ref[...]
    wt = jnp.concatenate([w[OFF_K:OFF_V], w[OFF_Q:OFF_O], w[OFF_V:OFF_G], w[OFF_O:]], axis=0)
    wr_ref[...] = jnp.transpose(wt).astype(BF16)
    g = jnp.concatenate([w[OFF_G:OFF_Q], jnp.zeros((GATE_PAD - 4 * N_HEADS, w.shape[1]), F32)], axis=0)
    wg_ref[...] = jnp.transpose(g)


def _wprep_call(w_t):
    n, d = w_t.shape
    tr = 256
    return pl.pallas_call(
        _wprep_body,
        grid=(d // tr,),
        in_specs=[pl.BlockSpec((n, tr), lambda i: (0, i))],
        out_specs=[pl.BlockSpec((tr, Z_COLS), lambda i: (i, 0)), pl.BlockSpec((tr, GATE_PAD), lambda i: (i, 0))],
        out_shape=[jax.ShapeDtypeStruct((d, Z_COLS), BF16), jax.ShapeDtypeStruct((d, GATE_PAD), F32)],
        compiler_params=_cparams(("arbitrary",), 32),
        name="wprep",
    )(w_t)


def _inproj_body(use_pos, x_ref, erow_ref, ecol_ref, gm_ref, sh_ref, sc_ref, w_ref, wg_ref,
                 z_ref, g_ref, h_scr):
    @pl.when(pl.program_id(1) == 0)
    def _():
        x = x_ref[...]
        if use_pos:
            x = _add_pos(x, erow_ref[...], ecol_ref[...])
        h = _norm_mod(x, gm_ref[...], sh_ref[...], sc_ref[...])
        h_scr[...] = h.astype(BF16)
        g_ref[...] = _dg(h_scr[...], wg_ref[...].astype(BF16))

    z_ref[...] = jnp.dot(h_scr[...], w_ref[...], preferred_element_type=F32).astype(BF16)


def _inproj_call(x, erow, ecol, gm, sh, sc, w, wg, use_pos, tm):
    m, d = x.shape
    tn = 1024
    er = tm // GRID_W if use_pos else erow.shape[0]
    row_map = (lambda i, j: (i, 0)) if use_pos else (lambda i, j: (0, 0))
    return pl.pallas_call(
        functools.partial(_inproj_body, use_pos),
        grid=(m // tm, Z_COLS // tn),
        in_specs=[pl.BlockSpec((tm, d), lambda i, j: (i, 0)),
                  pl.BlockSpec((er, d // 2), row_map),
                  pl.BlockSpec((GRID_W, d // 2), lambda i, j: (0, 0)),
                  pl.BlockSpec((1, d), lambda i, j: (0, 0)),
                  pl.BlockSpec((1, d), lambda i, j: (0, 0)),
                  pl.BlockSpec((1, d), lambda i, j: (0, 0)),
                  pl.BlockSpec((d, tn), lambda i, j: (0, j)),
                  pl.BlockSpec((d, GATE_PAD), lambda i, j: (0, 0))],
        out_specs=[pl.BlockSpec((tm, tn), lambda i, j: (i, j)),
                   pl.BlockSpec((tm, GATE_PAD), lambda i, j: (i, 0))],
        out_shape=[jax.ShapeDtypeStruct((m, Z_COLS), BF16),
                   jax.ShapeDtypeStruct((m, GATE_PAD), F32)],
        scratch_shapes=[pltpu.VMEM((tm, d), BF16)],
        compiler_params=_cparams(("arbitrary", "arbitrary"), 48),
        name="inproj",
    )(x, erow, ecol, gm, sh, sc, w, wg)


def _conv3(zc, zp, zn, w, b, first, last):
    tm = zc.shape[0]
    row = lax.broadcasted_iota(I32, zc.shape, 0)
    prev_row = jnp.where(first, 0.0, zp[7:8, :])
    next_row = jnp.where(last, 0.0, zn[0:1, :])
    xm = jnp.where(row == 0, prev_row, pltpu.roll(zc, 1, 0))
    xp = jnp.where(row == tm - 1, next_row, pltpu.roll(zc, tm - 1, 0))
    return xm * w[0:1, :] + zc * w[1:2, :] + xp * w[2:3, :] + b


def _conv_kq_body(zc_ref, zp_ref, zn_ref, w_ref, b_ref, s_ref, o_ref):
    i = pl.program_id(0)
    u = _conv3(zc_ref[...].astype(F32), zp_ref[...].astype(F32), zn_ref[...].astype(F32),
               w_ref[...], b_ref[...], i == 0, i == pl.num_programs(0) - 1)
    o_ref[...] = (_silu(u) * s_ref[...]).astype(BF16)


def _halo_specs(tm, m, cb):
    nb8 = m // 8
    return [pl.BlockSpec((tm, 1024), lambda i: (i, cb)),
            pl.BlockSpec((8, 1024), lambda i: (jnp.maximum(i * (tm // 8) - 1, 0), cb)),
            pl.BlockSpec((8, 1024), lambda i: (jnp.minimum((i + 1) * (tm // 8), nb8 - 1), cb))]


def _conv_kq_call(z, w, b, s, tm):
    m = z.shape[0]
    vec = pl.BlockSpec((1, 1024), lambda i: (0, 0))
    return pl.pallas_call(
        _conv_kq_body,
        grid=(m // tm,),
        in_specs=_halo_specs(tm, m, ZC_KQ) + [pl.BlockSpec((3, 1024), lambda i: (0, 0)), vec, vec],
        out_specs=pl.BlockSpec((tm, 1024), lambda i: (i, 0)),
        out_shape=jax.ShapeDtypeStruct((m, 1024), BF16),
        compiler_params=_cparams(("arbitrary",), 32),
        name="conv_kq",
    )(z, z, z, w, b, s)


def _conv_hy_body(ac_ref, ap_ref, an_ref, bc_ref, bp_ref, bn_ref, cc_ref, cp_ref, cn_ref,
                  w_ref, b_ref, x0_ref, xv_ref):
    i = pl.program_id(0)
    first, last = i == 0, i == pl.num_programs(0) - 1
    w = w_ref[...]
    b = b_ref[...]

    def cv(c, p, n, k):
        return _conv3(c[...].astype(F32), p[...].astype(F32), n[...].astype(F32),
                      w[:, k * 1024:(k + 1) * 1024], b[:, k * 1024:(k + 1) * 1024], first, last)

    x0_ref[...] = cv(ac_ref, ap_ref, an_ref, 0).astype(BF16)
    xv_ref[...] = (cv(bc_ref, bp_ref, bn_ref, 1) * cv(cc_ref, cp_ref, cn_ref, 2)).astype(BF16)


def _conv_hy_call(z, w, b, tm):
    m = z.shape[0]
    out = pl.BlockSpec((tm, 1024), lambda i: (i, 0))
    return pl.pallas_call(
        _conv_hy_body,
        grid=(m // tm,),
        in_specs=(_halo_specs(tm, m, ZC_X0) + _halo_specs(tm, m, ZC_X1) + _halo_specs(tm, m, ZC_HV)
                  + [pl.BlockSpec((3, 3072), lambda i: (0, 0)), pl.BlockSpec((1, 3072), lambda i: (0, 0))]),
        out_specs=[out, out],
        out_shape=[jax.ShapeDtypeStruct((m, 1024), BF16), jax.ShapeDtypeStruct((m, 1024), BF16)],
        compiler_params=_cparams(("arbitrary",), 32),
        name="conv_hy",
    )(z, z, z, z, z, z, z, z, z, w, b)


def _mlstm_body(cps, kqf_ref, vf_ref, gf_ref, kqb_ref, vb_ref, gb_ref, bg_ref, s0_ref, m0_ref,
                hf_ref, hb_ref, sfin_ref, mfin_ref, s_scr, m_scr):
    j = pl.program_id(0)

    @pl.when(j == 0)
    def _():
        s_scr[...] = s0_ref[...]
        m_scr[...] = m0_ref[...]

    r = lax.broadcasted_iota(I32, (CHUNK, CHUNK), 0)
    c = lax.broadcasted_iota(I32, (CHUNK, CHUNK), 1)
    ones_b = jnp.ones((CHUNK, CHUNK), BF16)
    bg = bg_ref[...]

    def lane_bcast(x, h, width=CHUNK):
        return jnp.broadcast_to(x[:, h:h + 1], (x.shape[0], width))

    def one_chunk(d, off):
        rows = pl.ds(off, CHUNK)
        kq = (kqf_ref, kqb_ref)[d][rows, :]
        v = (vf_ref, vb_ref)[d][rows, :]
        g_all = (gf_ref, gb_ref)[d][rows, :] + bg
        out_ref = (hf_ref, hb_ref)[d]
        tri = (r >= c) if d == 0 else (c >= r)
        tri_b = jnp.where(tri, 1.0, 0.0).astype(BF16)
        gi = g_all if d == 0 else pltpu.roll(g_all, CHUNK - 16, 1)
        gfp = pltpu.roll(g_all, CHUNK - 8 - 16 * d, 1)
        lf = jnp.minimum(gfp, 0.0) - jnp.log(1.0 + jnp.exp(-jnp.abs(gfp)))
        l1 = lf.astype(BF16)
        r1 = lf - l1.astype(F32)
        l2 = r1.astype(BF16)
        l3 = (r1 - l2.astype(F32)).astype(BF16)
        bcum = _dg(tri_b, l1) + _dg(tri_b, l2) + _dg(tri_b, l3)
        gtot = bcum[CHUNK - 1:CHUNK, :] if d == 0 else bcum[0:1, :]
        acol = gtot - bcum + gi
        m_loc = jnp.max(acol, axis=0, keepdims=True)
        m_st = m_scr[d, 0:1, :]
        m_new = jnp.maximum(gtot + m_st, m_loc)
        sp8 = jnp.broadcast_to(jnp.exp(gtot + m_st - m_new), (8, CHUNK))
        wst = jnp.exp(acol - m_new)
        rr = gi - bcum
        cm = rr
        for sh in (1, 2, 4, 8, 16, 32, 64):
            if d == 0:
                cm = jnp.maximum(cm, jnp.where(r >= sh, pltpu.roll(cm, sh, 0), NEG))
            else:
                cm = jnp.maximum(cm, jnp.where(r < CHUNK - sh, pltpu.roll(cm, CHUNK - sh, 0), NEG))
        mt = jnp.maximum(bcum + m_st, bcum + cm)
        c1 = bcum - mt
        rt = jnp.transpose(rr)
        wt = jnp.transpose(wst)
        m8 = jnp.broadcast_to(m_st, (8, CHUNK))
        kts = {}

        for h in range(N_HEADS):
            p, half = divmod(h, 2)
            lm = (c // QK_HEAD) == half
            kp = kq[:, p * 128:(p + 1) * 128]
            qp = kq[:, QK_W + p * 128:QK_W + (p + 1) * 128]
            vaug = jnp.concatenate([v[:, h * 128:(h + 1) * 128], ones_b], axis=1)
            qm = jnp.where(lm, qp, jnp.zeros_like(qp))
            c1b = lane_bcast(c1, h)
            pm = jnp.exp(jnp.where(tri, c1b + rt[h:h + 1, :], NEG))
            s = (_dg(qm, kp, _NT) * pm).astype(BF16)
            m_in = jnp.tile(lane_bcast(m8, h), (CHUNK // 8, 1))
            qs = (qm.astype(F32) * jnp.exp(c1b + m_in)).astype(BF16)
            st = s_scr[d * N_HEADS + h]
            tot = _dg(jnp.concatenate([s, qs], axis=1), jnp.concatenate([vaug, st.astype(BF16)], axis=0))
            den = jnp.maximum(jnp.abs(tot[:, 128:]), jnp.exp(-lane_bcast(mt, h)))
            out_ref[rows, h * 128:(h + 1) * 128] = (tot[:, :128] / den).astype(BF16)
            if p not in kts:
                kts[p] = jnp.transpose(kp.astype(F32))
            kw = jnp.where((r // QK_HEAD) == half, kts[p] * wt[h:h + 1, :], 0.0).astype(BF16)
            spb = jnp.tile(lane_bcast(sp8, h, 256), (CHUNK // 8, 1))
            s_scr[d * N_HEADS + h] = spb * st + _dg(kw, vaug)
        m_scr[d, 0:1, :] = m_new

    for sub in range(cps):
        one_chunk(0, sub * CHUNK)
        one_chunk(1, (cps - 1 - sub) * CHUNK)

    @pl.when(j == pl.num_programs(0) - 1)
    def _():
        sfin_ref[...] = s_scr[...]
        mfin_ref[...] = m_scr[...]


def _mlstm_call(kq, z, gates, bg, s0, m0):
    m = kq.shape[0]
    cps = math.gcd(m // CHUNK, MLSTM_CHUNKS_PER_STEP)
    rows = CHUNK * cps
    nc = m // rows
    fwd = lambda cb: (lambda j: (j, cb))
    bwd = lambda cb: (lambda j: (nc - 1 - j, cb))
    st_spec = pl.BlockSpec((2 * N_HEADS, CHUNK, 256), lambda j: (0, 0, 0))
    m_spec = pl.BlockSpec((2, 8, 128), lambda j: (0, 0, 0))
    return pl.pallas_call(
        functools.partial(_mlstm_body, cps),
        grid=(nc,),
        in_specs=[pl.BlockSpec((rows, 1024), fwd(0)), pl.BlockSpec((rows, 1024), fwd(ZC_V)),
                  pl.BlockSpec((rows, GATE_PAD), fwd(0)),
                  pl.BlockSpec((rows, 1024), bwd(0)), pl.BlockSpec((rows, 1024), bwd(ZC_V)),
                  pl.BlockSpec((rows, GATE_PAD), bwd(0)),
                  pl.BlockSpec((1, GATE_PAD), lambda j: (0, 0)), st_spec, m_spec],
        out_specs=[pl.BlockSpec((rows, 1024), fwd(0)), pl.BlockSpec((rows, 1024), bwd(0)), st_spec, m_spec],
        out_shape=[jax.ShapeDtypeStruct((m, 1024), BF16), jax.ShapeDtypeStruct((m, 1024), BF16),
                   jax.ShapeDtypeStruct((2 * N_HEADS, CHUNK, 256), F32),
                   jax.ShapeDtypeStruct((2, 8, 128), F32)],
        scratch_shapes=[pltpu.VMEM((2 * N_HEADS, CHUNK, 256), F32), pltpu.VMEM((2, 8, 128), F32)],
        compiler_params=_cparams(("arbitrary",), 32),
        name="mlstm",
    )(kq, z, gates, kq, z, gates, bg, s0, m0)


def _filt_body(seq_len, ft_ref, w1_ref, b1_ref, f1_ref, w2_ref, b2_ref, f2_ref, w3_ref, b3_ref, rt_ref,
               kf_ref, l1_ref):
    i = pl.program_id(0)
    tn = ft_ref.shape[1]
    hp = tn // 2
    reps = tn // 128
    col = lambda ref: jnp.tile(ref[...], (1, reps))
    h1 = jnp.sin(col(f1_ref) * (_dot3(w1_ref[...], ft_ref[...]) + col(b1_ref)))
    h2 = jnp.sin(col(f2_ref) * (_dot3(w2_ref[...], h1) + col(b2_ref)))
    r = lax.broadcasted_iota(I32, (hp, HY_W), 0)
    n_fwd = (8 * i + (r & 7)) * FFT_N2 + (r >> 3)
    rates = rt_ref[...]
    halves = []
    l1 = jnp.zeros((1, HY_W), F32)
    for hx in range(2):
        h = (_dot3(h2[:, hx * hp:(hx + 1) * hp], w3_ref[:, hx * HY_W:(hx + 1) * HY_W], _TN)
             + b3_ref[:, hx * HY_W:(hx + 1) * HY_W])
        n = n_fwd + hx * seq_len
        t01 = jnp.where(n <= seq_len, n, 2 * seq_len - n).astype(F32) / float(max(seq_len - 1, 1))
        h = jnp.where(n == seq_len, 0.0, h * jnp.exp(-t01 * rates))
        l1 = l1 + jnp.sum(jnp.abs(h), axis=0, keepdims=True)
        halves.append(h)
    word = _pack_pair(halves[0], halves[1])
    for b in range(FFT_N2):
        kf_ref[:, b * HY_W:(b + 1) * HY_W] = word[8 * b:8 * b + 8, :]

    @pl.when(i == 0)
    def _():
        l1_ref[...] = jnp.zeros_like(l1_ref)

    l1_ref[...] += l1


def _filt_call(feats_t, w1t, b1, f1, w2t, b2, f2, w3, b3, rates):
    n = feats_t.shape[1]
    seq_len = n // 2
    tn = 2 * 8 * FFT_N2
    c64 = lambda shape: pl.BlockSpec(shape, lambda i: (0, 0))
    return pl.pallas_call(
        functools.partial(_filt_body, seq_len),
        grid=(n // tn,),
        in_specs=[pl.BlockSpec((64, tn), lambda i: (0, i)),
                  c64((64, 64)), c64((64, 128)), c64((64, 128)), c64((64, 64)), c64((64, 128)), c64((64, 128)),
                  c64((64, 2 * HY_W)), c64((1, 2 * HY_W)), c64((1, HY_W))],
        out_specs=[pl.BlockSpec((8, FFT_N2 * HY_W), lambda i: (i, 0)), pl.BlockSpec((1, HY_W), lambda i: (0, 0))],
        out_shape=[jax.ShapeDtypeStruct((seq_len // FFT_N2, FFT_N2 * HY_W), U32),
                   jax.ShapeDtypeStruct((1, HY_W), F32)],
        compiler_params=_cparams(("arbitrary",), 48),
        name="filt",
    )(feats_t, w1t, b1, f1, w2t, b2, f2, w3, b3, rates)


def _fft_consts(n1_rows):
    n = n1_rows * FFT_N2
    kv = n1_rows // 2 + 1
    k1 = np.arange(FFT_KP, dtype=np.float64)
    valid = (k1 < kv).astype(np.float64)
    n1 = np.arange(n1_rows, dtype=np.float64)
    th1 = 2.0 * np.pi * np.outer(k1, n1) / n1_rows
    f1 = np.concatenate([np.cos(th1) * valid[:, None], -np.sin(th1) * valid[:, None]], axis=0)
    n2 = np.arange(FFT_N2, dtype=np.float64)
    tht = 2.0 * np.pi * np.outer(k1, n2) / n
    rep = lambda a: jnp.broadcast_to(jnp.asarray(a, F32)[:, :, None], (FFT_KP, FFT_N2, 128))
    twr = rep(np.cos(tht) * valid[:, None])
    twi = rep(-np.sin(tht) * valid[:, None])
    th2 = 2.0 * np.pi * np.outer(n2, n2) / FFT_N2
    cs, sn = np.cos(th2), np.sin(th2)
    f2p = np.block([[cs, sn], [-sn, cs]])
    f2pc = np.block([[cs, -sn], [sn, cs]])
    wk = np.where((k1 == 0) | (k1 == kv - 1), 1.0, 2.0) * valid / n
    half = n1_rows // 2
    thi = 2.0 * np.pi * np.outer(n1[:half], k1) / n1_rows
    gc = np.cos(thi) * wk[None, :]
    gs = np.sin(thi) * wk[None, :]
    as_bf = lambda a: jnp.asarray(a, F32).astype(BF16)
    return dict(f1=as_bf(f1), twr=twr, twi=twi,
                f2p=as_bf(f2p), f2pc=as_bf(f2pc), gc=as_bf(gc), gs=as_bf(gs))


def _fft1_body(f_ref, x_ref, ar_ref, ai_ref):
    o = _dg(f_ref[...], x_ref[...])
    ar_ref[...] = o[:FFT_KP].astype(BF16)
    ai_ref[...] = o[FFT_KP:].astype(BF16)


def _fft1_packed_body(f_ref, x_ref, ar_ref, ai_ref):
    k = x_ref.shape[0]
    hi, lo = _unpack_pair(x_ref[...])
    o = _dg(f_ref[:, 0:k], hi.astype(BF16)) + _dg(f_ref[:, k:], lo.astype(BF16))
    ar_ref[...] = o[:FFT_KP].astype(BF16)
    ai_ref[...] = o[FFT_KP:].astype(BF16)


def _fft1_call(f1, x2d, cb, packed=False):
    k, cols = x2d.shape
    f1 = f1[:, :2 * k] if packed else f1[:, :k]
    out = pl.BlockSpec((FFT_KP, cb), lambda i: (0, i))
    sh = jax.ShapeDtypeStruct((FFT_KP, cols), BF16)
    return pl.pallas_call(
        _fft1_packed_body if packed else _fft1_body,
        grid=(cols // cb,),
        in_specs=[pl.BlockSpec(f1.shape, lambda i: (0, 0)), pl.BlockSpec((k, cb), lambda i: (0, i))],
        out_specs=[out, out],
        out_shape=[sh, sh],
        compiler_params=_cparams(("arbitrary",), 32),
        name="fft1",
    )(f1, x2d)


def _twiddled(ar_ref, ai_ref, twr_ref, twi_ref, reps):
    a_r = ar_ref[...].astype(F32)
    a_i = ai_ref[...].astype(F32)
    tr = jnp.tile(twr_ref[...], (1, reps))
    ti = jnp.tile(twi_ref[...], (1, reps))
    st = jnp.concatenate([a_r * tr - a_i * ti, a_r * ti + a_i * tr], axis=0).astype(BF16)
    return st, tr, ti


FFT2_ROWS = 4


def _fft2_filt_body(kv, ar_ref, ai_ref, twr_ref, twi_ref, f2p_ref, k_ref):
    for j in range(FFT2_ROWS):
        k1 = pl.program_id(0) * FFT2_ROWS + j

        @pl.when(k1 < kv)
        def _():
            st, _, _ = _twiddled(ar_ref.at[j], ai_ref.at[j], twr_ref.at[j], twi_ref.at[j], ar_ref.shape[-1] // 128)
            k_ref[j] = _dg(f2p_ref[...], st).astype(BF16)

        @pl.when(k1 >= kv)
        def _():
            k_ref[j] = jnp.zeros(k_ref.shape[1:], BF16)


def _fft2_conv_body(kv, ar_ref, ai_ref, twr_ref, twi_ref, k_ref, f2p_ref, f2pc_ref, br_ref, bi_ref):
    for j in range(FFT2_ROWS):
        k1 = pl.program_id(0) * FFT2_ROWS + j

        @pl.when(k1 < kv)
        def _():
            st, tr, ti = _twiddled(ar_ref.at[j], ai_ref.at[j], twr_ref.at[j], twi_ref.at[j], ar_ref.shape[-1] // 128)
            x = _dg(f2p_ref[...], st)
            xr, xi = x[:FFT_N2], x[FFT_N2:]
            kr = k_ref[j, :FFT_N2, :].astype(F32)
            ki = k_ref[j, FFT_N2:, :].astype(F32)
            sy = jnp.concatenate([xr * kr - xi * ki, xr * ki + xi * kr], axis=0).astype(BF16)
            b = _dg(f2pc_ref[...], sy)
            b_r, b_i = b[:FFT_N2], b[FFT_N2:]
            br_ref[j] = (b_r * tr + b_i * ti).astype(BF16)
            bi_ref[j] = (b_i * tr - b_r * ti).astype(BF16)

        @pl.when(k1 >= kv)
        def _():
            br_ref[j] = jnp.zeros(br_ref.shape[1:], BF16)
            bi_ref[j] = jnp.zeros(bi_ref.shape[1:], BF16)


def _fft2_specs(ch, kv):
    src = lambda i: (jnp.minimum(i, (kv - 1) // FFT2_ROWS), 0, 0)
    blk = pl.BlockSpec((FFT2_ROWS, FFT_N2, ch), src)
    tw = pl.BlockSpec((FFT2_ROWS, FFT_N2, 128), src)
    mat = pl.BlockSpec((2 * FFT_N2, 2 * FFT_N2), lambda i: (0, 0))
    return blk, tw, mat, src


def _fft2_filt_call(ar, ai, cst, kv):
    ch = ar.shape[-1]
    blk, tw, mat, _ = _fft2_specs(ch, kv)
    return pl.pallas_call(
        functools.partial(_fft2_filt_body, kv),
        grid=(FFT_KP // FFT2_ROWS,),
        in_specs=[blk, blk, tw, tw, mat],
        out_specs=pl.BlockSpec((FFT2_ROWS, 2 * FFT_N2, ch), lambda i: (i, 0, 0)),
        out_shape=jax.ShapeDtypeStruct((FFT_KP, 2 * FFT_N2, ch), BF16),
        compiler_params=_cparams(("arbitrary",), 32),
        name="fft2_filt",
    )(ar, ai, cst["twr"], cst["twi"], cst["f2p"])


def _fft2_conv_call(ar, ai, khat, cst, kv):
    ch = ar.shape[-1]
    blk, tw, mat, src = _fft2_specs(ch, kv)
    sh = jax.ShapeDtypeStruct((FFT_KP, FFT_N2, ch), BF16)
    out = pl.BlockSpec((FFT2_ROWS, FFT_N2, ch), lambda i: (i, 0, 0))
    return pl.pallas_call(
        functools.partial(_fft2_conv_body, kv),
        grid=(FFT_KP // FFT2_ROWS,),
        in_specs=[blk, blk, tw, tw, pl.BlockSpec((FFT2_ROWS, 2 * FFT_N2, ch), src), mat, mat],
        out_specs=[out, out],
        out_shape=[sh, sh],
        compiler_params=_cparams(("arbitrary",), 32),
        name="fft2_conv",
    )(ar, ai, cst["twr"], cst["twi"], khat, cst["f2p"], cst["f2pc"])


def _ifft1_body(gc_ref, gs_ref, br_ref, bi_ref, il_ref, o_ref):
    y = _dg(gc_ref[...], br_ref[...]) - _dg(gs_ref[...], bi_ref[...])
    o_ref[...] = (y * il_ref[...]).astype(BF16)


def _ifft1_call(cst, br2d, bi2d, il_t, rows, cb):
    cols = br2d.shape[1]
    g = pl.BlockSpec((rows, FFT_KP), lambda i: (0, 0))
    kb = pl.BlockSpec((FFT_KP, cb), lambda i: (0, i))
    xb = pl.BlockSpec((rows, cb), lambda i: (0, i))
    vb = pl.BlockSpec((1, cb), lambda i: (0, 0))
    return pl.pallas_call(
        _ifft1_body,
        grid=(cols // cb,),
        in_specs=[g, g, kb, kb, vb],
        out_specs=xb,
        out_shape=jax.ShapeDtypeStruct((rows, cols), BF16),
        compiler_params=_cparams(("arbitrary",), 32),
        name="ifft1",
    )(cst["gc"], cst["gs"], br2d, bi2d, il_t)


def _outproj_body(hf_ref, hb_ref, zo_ref, yc_ref, x0_ref, xv_ref, ds_ref, x_ref, erow_ref, ecol_ref, gh_ref,
                  wa_ref, wb_ref, g1_ref, gf_ref, sh_ref, sc_ref, wr_ref, w1s_ref, w3s_ref, w2s_ref,
                  x1_ref, h2_ref, s_ref, shared_ref):
    hs = hf_ref[...].astype(F32) + hb_ref[...].astype(F32)
    gh = gh_ref[...]
    parts = []
    for h in range(N_HEADS):
        hh = hs[:, h * 128:(h + 1) * 128]
        ms = jnp.mean(hh * hh, axis=-1, keepdims=True)
        parts.append(hh * lax.rsqrt(ms + EPS) * gh[:, h * 128:(h + 1) * 128])
    ym = jnp.concatenate(parts, axis=-1) * _sigmoid(zo_ref[...].astype(F32))
    yh = x0_ref[...].astype(F32) * (yc_ref[...].astype(F32) + ds_ref[...] * xv_ref[...].astype(F32))
    y = _dg(ym.astype(BF16), wa_ref[...]) + _dg(yh.astype(BF16), wb_ref[...])
    rp = x_ref.shape[0] // GRID_W
    erow8 = erow_ref[...]
    erow = erow8[0:rp, :]
    for q in range(1, 8 // rp):
        erow = jnp.where(pl.program_id(0) % (8 // rp) == q, erow8[q * rp:(q + 1) * rp, :], erow)
    x1 = _add_pos(x_ref[...], erow, ecol_ref[...]) + g1_ref[...] * y
    x1_ref[...] = x1
    h2 = _norm_mod(x1, gf_ref[...], sh_ref[...], sc_ref[...])
    half = h2.shape[1] // 2
    _store_slabs(h2_ref, _pack_pair(h2[:, :half], h2[:, half:]))
    s_ref[...] = _sigmoid(_dot3(wr_ref[...], h2, _NT))
    h2b = h2.astype(BF16)
    a = (_silu(_dg(h2b, w1s_ref[...])) * _dg(h2b, w3s_ref[...])).astype(BF16)
    shared_ref[...] = _dg(a, w2s_ref[...]).astype(BF16)


def _outproj_call(hf, hb, z, yc, x0c, xv, ds, x, erow, ecol, gh, wa, wb, g1, gf, sh2, sc2, wrt, w1s, w3s, w2s, tm):
    m, d = x.shape
    dsh = w1s.shape[1]
    row = lambda cb: pl.BlockSpec((tm, 1024), lambda i: (i, cb))
    vec = lambda n: pl.BlockSpec((1, n), lambda i: (0, 0))
    full = pl.BlockSpec((tm, d), lambda i: (i, 0))
    const = lambda r, c: pl.BlockSpec((r, c), lambda i: (0, 0))
    return pl.pallas_call(
        _outproj_body,
        grid=(m // tm,),
        in_specs=[row(0), row(0), row(ZC_O), row(0), row(0), row(0), vec(HY_W), full,
                  pl.BlockSpec((8, d // 2), lambda i: (i * (tm // GRID_W) // 8, 0)),
                  const(GRID_W, d // 2),
                  vec(MV_W),
                  const(MV_W, d), const(HY_W, d),
                  vec(d), vec(d), vec(d), vec(d),
                  const(N_EXPERTS, d), const(d, dsh), const(d, dsh), const(dsh, d)],
        out_specs=[full, pl.BlockSpec((tm * SLAB, SLAB_W), lambda i: (i, 0)),
                   pl.BlockSpec((N_EXPERTS, tm), lambda i: (0, i)), full],
        out_shape=[jax.ShapeDtypeStruct((m, d), F32), jax.ShapeDtypeStruct((m * SLAB, SLAB_W), U32),
                   jax.ShapeDtypeStruct((N_EXPERTS, m), F32), jax.ShapeDtypeStruct((m, d), BF16)],
        compiler_params=_cparams(("arbitrary",), 56),
        name="outproj",
    )(hf, hb, z, yc, x0c, xv, ds, x, erow, ecol, gh, wa, wb, g1, gf, sh2, sc2, wrt, w1s, w3s, w2s)


def _first_max(x, idx, sentinel):
    m = jnp.max(x, axis=0, keepdims=True)
    return m, jnp.min(jnp.where(x == m, idx, sentinel), axis=0, keepdims=True)


def _route_body(s_ref, b_ref, e_ref, w_ref, r_ref, cnt_ref, u_scr, run_scr):
    i = pl.program_id(0)
    tt = s_ref.shape[1]

    @pl.when(i == 0)
    def _():
        rr = lax.broadcasted_iota(I32, (tt, tt), 0)
        cc = lax.broadcasted_iota(I32, (tt, tt), 1)
        u_scr[...] = jnp.where(rr < cc, 1.0, 0.0).astype(BF16)
        run_scr[...] = jnp.zeros_like(run_scr)

    s = s_ref[...]
    sel = s + b_ref[...][:, 0:1]
    sub8 = lax.broadcasted_iota(I32, (E_PER_GROUP, tt), 0).astype(F32)
    gs = jnp.zeros((N_GROUPS, tt), F32)
    for g in range(N_GROUPS):
        grp = sel[g * E_PER_GROUP:(g + 1) * E_PER_GROUP, :]
        m1, i1 = _first_max(grp, sub8, float(E_PER_GROUP))
        m2 = jnp.max(jnp.where(sub8 == i1, -jnp.inf, grp), axis=0, keepdims=True)
        gs = jnp.where(sub8 == g, m1 + m2, gs)
    gmask = jnp.zeros((N_GROUPS, tt), F32)
    for _ in range(TOPK_GROUPS):
        _, ig = _first_max(gs, sub8, float(N_GROUPS))
        hit = sub8 == ig
        gmask = jnp.where(hit, 1.0, gmask)
        gs = jnp.where(hit, -jnp.inf, gs)
    masked = jnp.concatenate(
        [jnp.where(jnp.broadcast_to(gmask[g:g + 1, :], (E_PER_GROUP, tt)) > 0.5,
                   sel[g * E_PER_GROUP:(g + 1) * E_PER_GROUP, :], -jnp.inf) for g in range(N_GROUPS)], axis=0)
    sub64 = lax.broadcasted_iota(I32, (N_EXPERTS, tt), 0).astype(F32)
    oh = jnp.zeros((N_EXPERTS, tt), F32)
    eks, wks = [], []
    for _ in range(TOP_K):
        _, ie = _first_max(masked, sub64, float(N_EXPERTS))
        hit = sub64 == ie
        wks.append(jnp.sum(jnp.where(hit, s, 0.0), axis=0, keepdims=True))
        eks.append(ie)
        masked = jnp.where(hit, -jnp.inf, masked)
        oh = jnp.where(hit, 1.0, oh)
    wsum = wks[0]
    for k in range(1, TOP_K):
        wsum = wsum + wks[k]
    run = run_scr[...]
    rank_t = _dg(oh.astype(BF16), u_scr[...]) + jnp.tile(run, (1, tt // 128))
    for k in range(TOP_K):
        e_ref[k:k + 1, :] = eks[k].astype(I32)
        w_ref[k:k + 1, :] = wks[k] / wsum * ROUTE_SCALE
        r_ref[k:k + 1, :] = jnp.sum(jnp.where(sub64 == eks[k], rank_t, 0.0), axis=0, keepdims=True).astype(I32)
    run_new = run + jnp.sum(oh, axis=1, keepdims=True)
    run_scr[...] = run_new
    cnt_ref[...] = run_new.astype(I32)


def _route_call(s_t, b_col, tt):
    m = s_t.shape[1]
    out = pl.BlockSpec((TOP_K, tt), lambda i: (0, i))
    return pl.pallas_call(
        _route_body,
        grid=(m // tt,),
        in_specs=[pl.BlockSpec((N_EXPERTS, tt), lambda i: (0, i)),
                  pl.BlockSpec((N_EXPERTS, 128), lambda i: (0, 0))],
        out_specs=[out, out, out, pl.BlockSpec((N_EXPERTS, 128), lambda i: (0, 0))],
        out_shape=[jax.ShapeDtypeStruct((TOP_K, m), I32), jax.ShapeDtypeStruct((TOP_K, m), F32),
                   jax.ShapeDtypeStruct((TOP_K, m), I32), jax.ShapeDtypeStruct((N_EXPERTS, 128), I32)],
        scratch_shapes=[pltpu.VMEM((tt, tt), BF16), pltpu.VMEM((N_EXPERTS, 128), F32)],
        compiler_params=_cparams(("arbitrary",), 32),
        name="route",
    )(s_t, b_col)


def _posk_body(pst_ref, e_ref, r_ref, p_ref):
    e = e_ref[...]
    acc = r_ref[...]
    for x in range(N_EXPERTS):
        acc = acc + jnp.where(e == x, pst_ref[x], 0)
    p_ref[...] = acc


def _posk_call(pstart, eidx, rank):
    k, m = eidx.shape
    tt = min(m, 2048)
    blk = pl.BlockSpec((k, tt), lambda i, pst: (0, i))
    return pl.pallas_call(
        _posk_body,
        grid_spec=pltpu.PrefetchScalarGridSpec(num_scalar_prefetch=1, grid=(m // tt,),
                                               in_specs=[blk, blk], out_specs=blk),
        out_shape=jax.ShapeDtypeStruct((k, m), I32),
        compiler_params=_cparams(("arbitrary",), 32),
        name="posk",
    )(pstart, eidx, rank)


def _slab(ref, r):
    return ref.at[pl.ds(pl.multiple_of(r * SLAB, SLAB), SLAB), :]


def _ffn_packed(x_ref, rows, w1, w3, w2, base=0):
    half = w1.shape[0] // 2
    xa, xb = _unpack_pair(_load_slabs(x_ref, rows, base))
    xa = xa.astype(BF16)
    xb = xb.astype(BF16)
    h1 = _dg(xa, w1[0:half, :]) + _dg(xb, w1[half:, :])
    h3 = _dg(xa, w3[0:half, :]) + _dg(xb, w3[half:, :])
    return _dg((_silu(h1) * h3).astype(BF16), w2[...])


def _dispatch_body(cnt_ref, pst_ref, pcn_ref, h2_ref, pos_ref, xs_ref, zrow, sem):
    i = pl.program_id(0)
    td = h2_ref.shape[0] // SLAB

    def row_copy(t, dst):
        return pltpu.make_async_copy(_slab(h2_ref, t), _slab(xs_ref, dst), sem)

    def issue(t, carry):
        for k in range(TOP_K):
            row_copy(t, pos_ref[k, t]).start(priority=k % 2)
        return carry

    lax.fori_loop(0, td, issue, 0)

    def drain(t, carry):
        for k in range(TOP_K):
            row_copy(0, 0).wait()
        return carry

    lax.fori_loop(0, td, drain, 0)

    @pl.when(i == pl.num_programs(0) - 1)
    def _():
        zrow[...] = jnp.zeros_like(zrow)

        def zero_copy(dst):
            return pltpu.make_async_copy(zrow, _slab(xs_ref, dst), sem)

        def per_expert(e, carry):
            base = pst_ref[e]
            lax.fori_loop(cnt_ref[e], pcn_ref[e], lambda rr, cc: (zero_copy(base + rr).start(), cc)[1], 0)
            lax.fori_loop(cnt_ref[e], pcn_ref[e], lambda rr, cc: (zero_copy(0).wait(), cc)[1], 0)
            return carry

        lax.fori_loop(0, N_EXPERTS, per_expert, 0)


def _dispatch_call(cnt, pstart, pcnt, h2s, pos, rows, td):
    m = h2s.shape[0] // SLAB
    return pl.pallas_call(
        _dispatch_body,
        grid_spec=pltpu.PrefetchScalarGridSpec(
            num_scalar_prefetch=3, grid=(m // td,),
            in_specs=[pl.BlockSpec((td * SLAB, SLAB_W), lambda i, *_: (i, 0)),
                      pl.BlockSpec((TOP_K, td), lambda i, *_: (0, i), memory_space=pltpu.SMEM)],
            out_specs=pl.BlockSpec(memory_space=pl.ANY),
            scratch_shapes=[pltpu.VMEM((SLAB, SLAB_W), U32), pltpu.SemaphoreType.DMA(())]),
        out_shape=jax.ShapeDtypeStruct((rows * SLAB, SLAB_W), U32),
        compiler_params=_cparams(("arbitrary",), 32),
        name="dispatch",
    )(cnt, pstart, pcnt, h2s, pos)


def _moe_body(te_ref, nu_ref, nxt_ref, par_ref, x_ref, w1_ref, w3_ref, w2_ref, y_ref,
              f1, f3, f2, w1b, w3b, w2b, sems):
    def fetch(ex, s):
        return (pltpu.make_async_copy(w1_ref.at[ex], f1.at[s], sems.at[s]),
                pltpu.make_async_copy(w3_ref.at[ex], f3.at[s], sems.at[s]),
                pltpu.make_async_copy(w2_ref.at[ex], f2.at[s], sems.at[s]))

    for sub in range(MOE_TILES_PER_STEP):
        ti = pl.program_id(0) * MOE_TILES_PER_STEP + sub
        used = ti < nu_ref[0]
        e = te_ref[ti]
        first = jnp.logical_or(ti == 0, e != te_ref[jnp.maximum(ti - 1, 0)])
        slot = par_ref[e]

        @pl.when(jnp.logical_and(used, ti == 0))
        def _():
            for c in fetch(e, slot):
                c.start()

        @pl.when(jnp.logical_and(used, first))
        def _():
            for c in fetch(e, slot):
                c.wait()
            nx = nxt_ref[e]

            @pl.when(nx < N_EXPERTS)
            def _():
                for c in fetch(nx, 1 - slot):
                    c.start()

            w1b[...] = f1[slot].astype(BF16)
            w3b[...] = f3[slot].astype(BF16)
            w2b[...] = f2[slot].astype(BF16)

        @pl.when(used)
        def _():
            half = w1b.shape[0] // 2
            base = sub * MOE_ROWS * SLAB
            y = _ffn_packed(x_ref, MOE_ROWS, w1b, w3b, w2b, base)
            _store_slabs(y_ref, _pack_pair(y[:, :half], y[:, half:]), base)


def _moe_call(tile_e, n_used, nxt, par, xs, w1, w3, w2):
    rows = xs.shape[0] // SLAB
    step_rows = MOE_ROWS * MOE_TILES_PER_STEP
    nt = rows // step_rows
    d, de = w1.shape[-2:]
    rmap = lambda i, te, nu, *_: (jnp.minimum(i, (nu[0] - 1) // MOE_TILES_PER_STEP), 0)
    hbm = pl.BlockSpec(memory_space=pl.ANY)
    return pl.pallas_call(
        _moe_body,
        grid_spec=pltpu.PrefetchScalarGridSpec(
            num_scalar_prefetch=4, grid=(nt,),
            in_specs=[pl.BlockSpec((step_rows * SLAB, SLAB_W), rmap), hbm, hbm, hbm],
            out_specs=pl.BlockSpec((step_rows * SLAB, SLAB_W), rmap),
            scratch_shapes=[pltpu.VMEM((2, d, de), F32), pltpu.VMEM((2, d, de), F32), pltpu.VMEM((2, de, d), F32),
                            pltpu.VMEM((d, de), BF16), pltpu.VMEM((d, de), BF16), pltpu.VMEM((de, d), BF16),
                            pltpu.SemaphoreType.DMA((2,))]),
        out_shape=jax.ShapeDtypeStruct((rows * SLAB, SLAB_W), U32),
        compiler_params=_cparams(("arbitrary",), 56),
        name="moe",
    )(tile_e, n_used, nxt, par, xs, w1, w3, w2)


def _final_body(x1_ref, sh_ref, pos_ref, posn_ref, wt_ref, ys_ref, g2_ref, gn_ref, o_ref,
                ybuf, acc_a, acc_b, sems):
    i = pl.program_id(0)
    n = pl.num_programs(0)
    tf, d = x1_ref.shape
    slot = i % 2
    slot_slabs = TOP_K * tf

    def row_copy(p_ref, s, k, t):
        return pltpu.make_async_copy(_slab(ys_ref, p_ref[k, t]), _slab(ybuf, s * slot_slabs + k * tf + t),
                                     sems.at[s])

    def issue_token(p_ref, s, t):
        for k in range(TOP_K):
            row_copy(p_ref, s, k, t).start(priority=k % 2)

    @pl.when(i == 0)
    def _():
        lax.fori_loop(0, tf, lambda t, c: (issue_token(pos_ref, 0, t), c)[1], 0)

    def drain(t, carry):
        for k in range(TOP_K):
            pltpu.make_async_copy(_slab(ys_ref, 0), _slab(ybuf, 0), sems.at[slot]).wait()
        return carry

    lax.fori_loop(0, tf, drain, 0)

    def combine_token(t):
        a = jnp.zeros((SLAB, SLAB_W), F32)
        b = jnp.zeros((SLAB, SLAB_W), F32)
        for k in range(TOP_K):
            ya, yb = _unpack_pair(_slab(ybuf, slot * slot_slabs + k * tf + t)[...])
            w = wt_ref[k, t]
            a = a + w * ya
            b = b + w * yb
        _slab(acc_a, t)[...] = a
        _slab(acc_b, t)[...] = b

    @pl.when(i + 1 < n)
    def _():
        def body(t, carry):
            combine_token(t)
            issue_token(posn_ref, 1 - slot, t)
            return carry

        lax.fori_loop(0, tf, body, 0)

    @pl.when(i + 1 >= n)
    def _():
        lax.fori_loop(0, tf, lambda t, c: (combine_token(t), c)[1], 0)

    moe = jnp.concatenate([_load_slabs(acc_a, tf), _load_slabs(acc_b, tf)], axis=1) + sh_ref[...].astype(F32)
    xo = x1_ref[...] + g2_ref[...] * moe
    ms = jnp.mean(xo * xo, axis=-1, keepdims=True)
    o_ref[...] = xo * lax.rsqrt(ms + EPS) * gn_ref[...]


def _final_call(x1, sh, pos, wts, ys, g2, gn, tf):
    m, d = x1.shape
    nt = m // tf
    full = pl.BlockSpec((tf, d), lambda i: (i, 0))
    vec = pl.BlockSpec((1, d), lambda i: (0, 0))
    return pl.pallas_call(
        _final_body,
        grid=(nt,),
        in_specs=[full, full,
                  pl.BlockSpec((TOP_K, tf), lambda i: (0, i), memory_space=pltpu.SMEM),
                  pl.BlockSpec((TOP_K, tf), lambda i: (0, jnp.minimum(i + 1, nt - 1)), memory_space=pltpu.SMEM),
                  pl.BlockSpec((TOP_K, tf), lambda i: (0, i), memory_space=pltpu.SMEM),
                  pl.BlockSpec(memory_space=pl.ANY), vec, vec],
        out_specs=full,
        out_shape=jax.ShapeDtypeStruct((m, d), F32),
        scratch_shapes=[pltpu.VMEM((2 * TOP_K * tf * SLAB, SLAB_W), U32),
                        pltpu.VMEM((tf * SLAB, SLAB_W), F32), pltpu.VMEM((tf * SLAB, SLAB_W), F32),
                        pltpu.SemaphoreType.DMA((2,))],
        compiler_params=_cparams(("arbitrary",), 48),
        name="final",
    )(x1, sh, pos, pos, wts, ys, g2, gn)


def _pos_tables(n_tokens):
    rows = n_tokens // GRID_W
    quarter = D_MODEL // 4
    omega = 1.0 / (10000.0 ** (jnp.arange(quarter, dtype=F32) / quarter))

    def emb1d(pos):
        ang = pos[:, None] * omega[None]
        return jnp.concatenate([jnp.sin(ang), jnp.cos(ang)], axis=-1)

    return emb1d(jnp.arange(rows, dtype=F32)), emb1d(jnp.arange(GRID_W, dtype=F32))


def _filter_feats(L):
    n1h = L // FFT_N2
    i_, h_, b_, a_ = jnp.meshgrid(jnp.arange(n1h // 8, dtype=I32), jnp.arange(2, dtype=I32),
                                  jnp.arange(FFT_N2, dtype=I32), jnp.arange(8, dtype=I32), indexing="ij")
    n = ((8 * i_ + a_ + h_ * n1h) * FFT_N2 + b_).reshape(-1)
    t = jnp.where(n <= L, n, 2 * L - n).astype(F32)
    t01 = t / max(L - 1, 1)
    w = 2.0 * math.pi * t / L
    bands = jnp.linspace(1e-4, FILT_BANDS - 1, FILT_BANDS, dtype=F32)
    feats = jnp.concatenate([t01[None, :], jnp.cos(bands[:, None] * w[None, :]), -jnp.sin(bands[:, None] * w[None, :]),
                             jnp.zeros((64 - 33, 2 * L), F32)], axis=0)
    return feats


def _pad_rows(a, rows):
    return jnp.concatenate([a, jnp.zeros((rows - a.shape[0],) + a.shape[1:], a.dtype)], axis=0)


def _layer(x, c, ctx, c_ctx, w_ada, b_ada, g_mix, g_ffn, w_in, b_gates, conv_k_w, conv_k_b,
           conv_q_w, conv_q_b, g_head, conv_hy_w, conv_hy_b, filt_w1, filt_b1, filt_freq1,
           filt_w2, filt_b2, filt_freq2, filt_w3, filt_b3, hy_dskip, w_out, w_router, b_router,
           w1_e, w3_e, w2_e, w1_s, w3_s, w2_s, g_final):
    L, d = x.shape
    lc = ctx.shape[0]
    row = lambda v: v.reshape(1, -1)

    cc = jnp.broadcast_to(jnp.stack([c, c_ctx], axis=0)[:, :, None], (2, d, 128))
    mods = _mod_call(cc, w_ada, row(b_ada))
    sh1, sc1, g1, sh2, sc2, g2 = [mods[0:1, k * d:(k + 1) * d] for k in range(6)]
    csh1, csc1 = mods[1:2, 0:d], mods[1:2, d:2 * d]

    w_r, w_g = _wprep_call(jnp.transpose(w_in[0]))
    bg = jnp.concatenate([b_gates, jnp.zeros((GATE_PAD - 4 * N_HEADS,), F32)]).reshape(1, GATE_PAD)
    e_row, e_col = _pos_tables(L)
    conv_w = jnp.concatenate([conv_k_w, conv_q_w], axis=1)
    conv_b = jnp.concatenate([conv_k_b, conv_q_b]).reshape(1, -1)
    conv_s = jnp.concatenate([jnp.ones((QK_W,), F32), jnp.full((QK_W,), QK_HEAD ** -0.5, F32)]).reshape(1, -1)

    z_c, gt_c = _inproj_call(ctx, jnp.zeros((8, d // 2), F32), e_col, row(g_mix), csh1, csc1, w_r, w_g,
                             use_pos=False, tm=min(lc, 256))
    kq_c = _conv_kq_call(z_c, conv_w, conv_b, conv_s, tm=min(lc, 256))
    s0 = jnp.zeros((2 * N_HEADS, CHUNK, 256), F32)
    m0 = jnp.zeros((2, 8, 128), F32)
    _, _, s_ctx, m_ctx = _mlstm_call(kq_c, z_c, gt_c, bg, s0, m0)

    z, gates = _inproj_call(x, e_row, e_col, row(g_mix), sh1, sc1, w_r, w_g, use_pos=True, tm=min(L, 1024))
    kq = _conv_kq_call(z, conv_w, conv_b, conv_s, tm=min(L, 1024))
    x0c, xv = _conv_hy_call(z, conv_hy_w, row(conv_hy_b), tm=min(L, 512))
    hf, hb, _, _ = _mlstm_call(kq, z, gates, bg, s_ctx, m_ctx)

    n1 = 2 * L // FFT_N2
    cst = _fft_consts(n1)
    rates = jnp.linspace(-math.log(DECAY_TARGET) / SLOW_DECAY_PCT, -math.log(DECAY_TARGET) / FAST_DECAY_PCT,
                         HY_W, dtype=F32).reshape(1, -1)
    w1t = jnp.transpose(_pad_rows(filt_w1, 64))
    colrep = lambda v: jnp.broadcast_to(v.reshape(-1, 1), (v.shape[0], 128))
    kf, l1 = _filt_call(_filter_feats(L), w1t, colrep(filt_b1), colrep(filt_freq1), jnp.transpose(filt_w2),
                        colrep(filt_b2), colrep(filt_freq2), filt_w3, row(filt_b3), rates)
    cols = FFT_N2 * HY_W
    cb = 4096
    kar, kai = _fft1_call(cst["f1"], kf, cb, packed=True)
    kv = n1 // 2 + 1
    khat = _fft2_filt_call(kar.reshape(FFT_KP, FFT_N2, HY_W), kai.reshape(FFT_KP, FFT_N2, HY_W), cst, kv)
    uar, uai = _fft1_call(cst["f1"], xv.reshape(n1 // 2, cols), cb)
    br, bi = _fft2_conv_call(uar.reshape(FFT_KP, FFT_N2, HY_W), uai.reshape(FFT_KP, FFT_N2, HY_W), khat, cst, kv)
    reps = cb // HY_W
    il_t = jnp.tile(1.0 / l1, (1, reps))
    yc = _ifft1_call(cst, br.reshape(FFT_KP, cols), bi.reshape(FFT_KP, cols), il_t, n1 // 2, cb).reshape(L, HY_W)

    wo = w_out.astype(BF16)
    x1, h2s, s_t, sh = _outproj_call(hf, hb, z, yc, x0c, xv, row(hy_dskip), x, e_row, e_col, row(g_head),
                                    wo[:MV_W], wo[MV_W:], g1, row(g_ffn), sh2, sc2, jnp.transpose(w_router),
                                    w1_s.astype(BF16), w3_s.astype(BF16), w2_s.astype(BF16), tm=min(L, 256))

    b_col = jnp.broadcast_to(b_router.reshape(N_EXPERTS, 1), (N_EXPERTS, 128))
    eidx, wts, rank, cnt2 = _route_call(s_t, b_col, tt=min(L, 1024))
    cnt = cnt2[:, 0]
    pcnt = (cnt + MOE_ROWS - 1) // MOE_ROWS * MOE_ROWS
    pend = jnp.cumsum(pcnt)
    pstart = pend - pcnt
    rows = L * TOP_K + N_EXPERTS * MOE_ROWS
    nt = rows // MOE_ROWS
    tile_row = jnp.arange(nt, dtype=I32) * MOE_ROWS
    tile_e = jnp.minimum(jnp.sum((pend[None, :] <= tile_row[:, None]).astype(I32), axis=1), N_EXPERTS - 1)
    n_used = (pend[-1] // MOE_ROWS).astype(I32).reshape(1)
    pos = _posk_call(pstart.astype(I32), eidx, rank)

    xs = _dispatch_call(cnt, pstart.astype(I32), pcnt.astype(I32), h2s, pos, rows, td=min(L, 512))
    ex = jnp.arange(N_EXPERTS, dtype=I32)
    nonempty = pcnt > 0
    nxt = jnp.min(jnp.where((ex[None, :] > ex[:, None]) & nonempty[None, :], ex[None, :], N_EXPERTS), axis=1)
    par = (jnp.cumsum(nonempty.astype(I32)) + 1) % 2
    ys = _moe_call(tile_e, n_used, nxt.astype(I32), par.astype(I32), xs, w1_e, w3_e, w2_e)
    return _final_call(x1, sh, pos, wts, ys, g2, row(g_final), tf=min(L, 256))


def kernel(x, c, ctx, c_ctx, w_ada, b_ada, g_mix, g_ffn, w_in, b_gates, conv_k_w, conv_k_b, conv_q_w,
           conv_q_b, g_head, conv_hy_w, conv_hy_b, filt_w1, filt_b1, filt_freq1, filt_w2, filt_b2,
           filt_freq2, filt_w3, filt_b3, hy_dskip, w_out, w_router, b_router, w1_e, w3_e, w2_e,
           w1_s, w3_s, w2_s, g_final):
    assert x.shape[0] == 1 and w_ada.shape[0] == 1, "one batch element, one layer"
    out = _layer(x[0], c[0], ctx[0], c_ctx, w_ada[0], b_ada[0], g_mix[0], g_ffn[0], w_in, b_gates[0],
                 conv_k_w[0], conv_k_b[0], conv_q_w[0], conv_q_b[0], g_head[0], conv_hy_w[0], conv_hy_b[0],
                 filt_w1[0], filt_b1[0], filt_freq1[0], filt_w2[0], filt_b2[0], filt_freq2[0], filt_w3[0],
                 filt_b3[0], hy_dskip[0], w_out[0], w_router[0], b_router[0], w1_e[0], w3_e[0], w2_e[0],
                 w1_s[0], w3_s[0], w2_s[0], g_final)
    return out[None]
```

```python
import functools
import math

import numpy as np
import jax
import jax.numpy as jnp
from jax import lax
from jax.experimental import pallas as pl
from jax.experimental.pallas import tpu as pltpu

F32 = jnp.float32
BF16 = jnp.bfloat16
I32 = jnp.int32
U32 = jnp.uint32

D_MODEL = 2048
GRID_W = 64
N_HEADS = 8
QK_HEAD = 64
V_HEAD = 128
QK_W = N_HEADS * QK_HEAD
MV_W = N_HEADS * V_HEAD
HY_W = D_MODEL - MV_W
CHUNK = 128
FILT_BANDS = 16
FILT_HIDDEN = 64
DECAY_TARGET = 1e-2
FAST_DECAY_PCT = 0.3
SLOW_DECAY_PCT = 1.5
N_EXPERTS = 64
N_GROUPS = 8
E_PER_GROUP = 8
TOPK_GROUPS = 4
TOP_K = 8
D_EXPERT = 512
ROUTE_SCALE = 2.5
EPS = 1e-6
OFF_K = 0
OFF_V = OFF_K + QK_W
OFF_G = OFF_V + MV_W
OFF_Q = OFF_G + 4 * N_HEADS
OFF_O = OFF_Q + QK_W
OFF_HY = OFF_O + MV_W

ZC_KQ, ZC_V, ZC_O, ZC_X0, ZC_X1, ZC_HV = 0, 1, 2, 3, 4, 5
Z_COLS = 6 * 1024
GATE_PAD = 128

NEG = -1e30
MIB = 1024 * 1024
MXU_DEPTH = 256

FFT_N2 = 128
FFT_KP = 144

MOE_ROWS = 256
MOE_TILES_PER_STEP = 4
MLSTM_CHUNKS_PER_STEP = 4


def _cparams(sem, vmem_mb, flags=None):
    return pltpu.CompilerParams(dimension_semantics=sem, vmem_limit_bytes=vmem_mb * MIB, flags=flags)


def _split2(x):
    hi = x.astype(BF16)
    lo = (x - hi.astype(F32)).astype(BF16)
    return hi, lo


_NN = (((1,), (0,)), ((), ()))
_NT = (((1,), (1,)), ((), ()))
_TN = (((0,), (0,)), ((), ()))


def _dg(a, b, dims=_NN):
    return lax.dot_general(a, b, dims, preferred_element_type=F32)


def _dot3(a, b, dims=_NN):
    ah, al = _split2(a)
    bh, bl = _split2(b)
    ka, kb = dims[0][0][0], dims[0][1][0]
    if 3 * a.shape[ka] <= MXU_DEPTH:
        return _dg(jnp.concatenate([ah, al, ah], axis=ka), jnp.concatenate([bh, bh, bl], axis=kb), dims)
    return _dg(ah, bh, dims) + _dg(al, bh, dims) + _dg(ah, bl, dims)


def _sigmoid(x):
    return 1.0 / (1.0 + jnp.exp(-x))


def _silu(x):
    return x * _sigmoid(x)


def _pack_pair(a, b):
    hi = lax.bitcast_convert_type(a.astype(BF16).astype(F32), U32)
    lo = lax.bitcast_convert_type(b.astype(BF16).astype(F32), U32)
    return hi | (lo >> 16)


def _unpack_pair(w):
    a = lax.bitcast_convert_type(w & jnp.uint32(0xFFFF0000), F32)
    b = lax.bitcast_convert_type(w << 16, F32)
    return a, b


SLAB = 8
SLAB_W = 128


def _store_slabs(ref, w, base=0):
    r = w.shape[0]
    for j in range(SLAB):
        ref[pl.ds(base + j, r, stride=SLAB), :] = w[:, j * SLAB_W:(j + 1) * SLAB_W]


def _load_slabs(ref, r, base=0):
    return jnp.concatenate([ref[pl.ds(base + j, r, stride=SLAB), :] for j in range(SLAB)], axis=1)


def _norm_mod(x, g, sh, sc):
    ms = jnp.mean(x * x, axis=-1, keepdims=True)
    return (x * lax.rsqrt(ms + EPS) * g) * (1.0 + sc) + sh


def _add_pos(x, erow, ecol):
    tm, d = x.shape
    half = d // 2
    parts = []
    for r in range(tm // GRID_W):
        xs = x[r * GRID_W:(r + 1) * GRID_W, :]
        parts.append(jnp.concatenate([xs[:, :half] + erow[r:r + 1, :], xs[:, half:] + ecol], axis=-1))
    return parts[0] if len(parts) == 1 else jnp.concatenate(parts, axis=0)


def _mod_body(cc_ref, w_ref, b_ref, o_ref):
    w = w_ref[...]
    nv = cc_ref.shape[0]
    reps = w.shape[1] // 128
    rows = [jnp.sum(w * jnp.tile(_silu(cc_ref[v]), (1, reps)), axis=0, keepdims=True) for v in range(nv)]
    part = jnp.concatenate(rows + [jnp.zeros((8 - nv, w.shape[1]), F32)], axis=0)

    @pl.when(pl.program_id(0) == 0)
    def _():
        o_ref[...] = jnp.broadcast_to(b_ref[...], o_ref.shape)

    o_ref[...] += part


def _mod_call(cc, w, b):
    d, n = w.shape
    tr = 256
    return pl.pallas_call(
        _mod_body,
        grid=(d // tr,),
        in_specs=[pl.BlockSpec((cc.shape[0], tr, 128), lambda j: (0, j, 0)),
                  pl.BlockSpec((tr, n), lambda j: (j, 0)),
                  pl.BlockSpec((1, n), lambda j: (0, 0))],
        out_specs=pl.BlockSpec((8, n), lambda j: (0, 0)),
        out_shape=jax.ShapeDtypeStruct((8, n), F32),
        compiler_params=_cparams(("arbitrary",), 40),
        name="mod",
    )(cc, w, b)


def _wprep_body(w_ref, wr_ref, wg_ref):
    w = w_ref[...]
    wt = jnp.concatenate([w[OFF_K:OFF_V], w[OFF_Q:OFF_O], w[OFF_V:OFF_G], w[OFF_O:]], axis=0)
    wr_ref[...] = jnp.transpose(wt).astype(BF16)
    g = jnp.concatenate([w[OFF_G:OFF_Q], jnp.zeros((GATE_PAD - 4 * N_HEADS, w.shape[1]), F32)], axis=0)
    wg_ref[...] = jnp.transpose(g)


def _wprep_call(w_t):
    n, d = w_t.shape
    tr = 256
    return pl.pallas_call(
        _wprep_body,
        grid=(d // tr,),
        in_specs=[pl.BlockSpec((n, tr), lambda i: (0, i))],
        out_specs=[pl.BlockSpec((tr, Z_COLS), lambda i: (i, 0)), pl.BlockSpec((tr, GATE_PAD), lambda i: (i, 0))],
        out_shape=[jax.ShapeDtypeStruct((d, Z_COLS), BF16), jax.ShapeDtypeStruct((d, GATE_PAD), F32)],
        compiler_params=_cparams(("arbitrary",), 32),
        name="wprep",
    )(w_t)


def _inproj_body(use_pos, x_ref, erow_ref, ecol_ref, gm_ref, sh_ref, sc_ref, w_ref, wg_ref,
                 z_ref, g_ref, h_scr):
    @pl.when(pl.program_id(1) == 0)
    def _():
        x = x_ref[...]
        if use_pos:
            x = _add_pos(x, erow_ref[...], ecol_ref[...])
        h = _norm_mod(x, gm_ref[...], sh_ref[...], sc_ref[...])
        h_scr[...] = h.astype(BF16)
        g_ref[...] = _dg(h_scr[...], wg_ref[...].astype(BF16))

    z_ref[...] = jnp.dot(h_scr[...], w_ref[...], preferred_element_type=F32).astype(BF16)


def _inproj_call(x, erow, ecol, gm, sh, sc, w, wg, use_pos, tm):
    m, d = x.shape
    tn = 1024
    er = tm // GRID_W if use_pos else erow.shape[0]
    row_map = (lambda i, j: (i, 0)) if use_pos else (lambda i, j: (0, 0))
    return pl.pallas_call(
        functools.partial(_inproj_body, use_pos),
        grid=(m // tm, Z_COLS // tn),
        in_specs=[pl.BlockSpec((tm, d), lambda i, j: (i, 0)),
                  pl.BlockSpec((er, d // 2), row_map),
                  pl.BlockSpec((GRID_W, d // 2), lambda i, j: (0, 0)),
                  pl.BlockSpec((1, d), lambda i, j: (0, 0)),
                  pl.BlockSpec((1, d), lambda i, j: (0, 0)),
                  pl.BlockSpec((1, d), lambda i, j: (0, 0)),
                  pl.BlockSpec((d, tn), lambda i, j: (0, j)),
                  pl.BlockSpec((d, GATE_PAD), lambda i, j: (0, 0))],
        out_specs=[pl.BlockSpec((tm, tn), lambda i, j: (i, j)),
                   pl.BlockSpec((tm, GATE_PAD), lambda i, j: (i, 0))],
        out_shape=[jax.ShapeDtypeStruct((m, Z_COLS), BF16),
                   jax.ShapeDtypeStruct((m, GATE_PAD), F32)],
        scratch_shapes=[pltpu.VMEM((tm, d), BF16)],
        compiler_params=_cparams(("arbitrary", "arbitrary"), 48),
        name="inproj",
    )(x, erow, ecol, gm, sh, sc, w, wg)


def _conv3(zc, zp, zn, w, b, first, last):
    tm = zc.shape[0]
    row = lax.broadcasted_iota(I32, zc.shape, 0)
    prev_row = jnp.where(first, 0.0, zp[7:8, :])
    next_row = jnp.where(last, 0.0, zn[0:1, :])
    xm = jnp.where(row == 0, prev_row, pltpu.roll(zc, 1, 0))
    xp = jnp.where(row == tm - 1, next_row, pltpu.roll(zc, tm - 1, 0))
    return xm * w[0:1, :] + zc * w[1:2, :] + xp * w[2:3, :] + b


def _conv_kq_body(zc_ref, zp_ref, zn_ref, w_ref, b_ref, s_ref, o_ref):
    i = pl.program_id(0)
    u = _conv3(zc_ref[...].astype(F32), zp_ref[...].astype(F32), zn_ref[...].astype(F32),
               w_ref[...], b_ref[...], i == 0, i == pl.num_programs(0) - 1)
    o_ref[...] = (_silu(u) * s_ref[...]).astype(BF16)


def _halo_specs(tm, m, cb):
    nb8 = m // 8
    return [pl.BlockSpec((tm, 1024), lambda i: (i, cb)),
            pl.BlockSpec((8, 1024), lambda i: (jnp.maximum(i * (tm // 8) - 1, 0), cb)),
            pl.BlockSpec((8, 1024), lambda i: (jnp.minimum((i + 1) * (tm // 8), nb8 - 1), cb))]


def _conv_kq_call(z, w, b, s, tm):
    m = z.shape[0]
    vec = pl.BlockSpec((1, 1024), lambda i: (0, 0))
    return pl.pallas_call(
        _conv_kq_body,
        grid=(m // tm,),
        in_specs=_halo_specs(tm, m, ZC_KQ) + [pl.BlockSpec((3, 1024), lambda i: (0, 0)), vec, vec],
        out_specs=pl.BlockSpec((tm, 1024), lambda i: (i, 0)),
        out_shape=jax.ShapeDtypeStruct((m, 1024), BF16),
        compiler_params=_cparams(("arbitrary",), 32),
        name="conv_kq",
    )(z, z, z, w, b, s)


def _conv_hy_body(ac_ref, ap_ref, an_ref, bc_ref, bp_ref, bn_ref, cc_ref, cp_ref, cn_ref,
                  w_ref, b_ref, x0_ref, xv_ref):
    i = pl.program_id(0)
    first, last = i == 0, i == pl.num_programs(0) - 1
    w = w_ref[...]
    b = b_ref[...]

    def cv(c, p, n, k):
        return _conv3(c[...].astype(F32), p[...].astype(F32), n[...].astype(F32),
                      w[:, k * 1024:(k + 1) * 1024], b[:, k * 1024:(k + 1) * 1024], first, last)

    x0_ref[...] = cv(ac_ref, ap_ref, an_ref, 0).astype(BF16)
    xv_ref[...] = (cv(bc_ref, bp_ref, bn_ref, 1) * cv(cc_ref, cp_ref, cn_ref, 2)).astype(BF16)


def _conv_hy_call(z, w, b, tm):
    m = z.shape[0]
    out = pl.BlockSpec((tm, 1024), lambda i: (i, 0))
    return pl.pallas_call(
        _conv_hy_body,
        grid=(m // tm,),
        in_specs=(_halo_specs(tm, m, ZC_X0) + _halo_specs(tm, m, ZC_X1) + _halo_specs(tm, m, ZC_HV)
                  + [pl.BlockSpec((3, 3072), lambda i: (0, 0)), pl.BlockSpec((1, 3072), lambda i: (0, 0))]),
        out_specs=[out, out],
        out_shape=[jax.ShapeDtypeStruct((m, 1024), BF16), jax.ShapeDtypeStruct((m, 1024), BF16)],
        compiler_params=_cparams(("arbitrary",), 32),
        name="conv_hy",
    )(z, z, z, z, z, z, z, z, z, w, b)


def _mlstm_body(cps, kqf_ref, vf_ref, gf_ref, kqb_ref, vb_ref, gb_ref, bg_ref, s0_ref, m0_ref,
                hf_ref, hb_ref, sfin_ref, mfin_ref, s_scr, m_scr):
    j = pl.program_id(0)

    @pl.when(j == 0)
    def _():
        s_scr[...] = s0_ref[...]
        m_scr[...] = m0_ref[...]

    r = lax.broadcasted_iota(I32, (CHUNK, CHUNK), 0)
    c = lax.broadcasted_iota(I32, (CHUNK, CHUNK), 1)
    ones_b = jnp.ones((CHUNK, CHUNK), BF16)
    bg = bg_ref[...]

    def lane_bcast(x, h, width=CHUNK):
        return jnp.broadcast_to(x[:, h:h + 1], (x.shape[0], width))

    def one_chunk(d, off):
        rows = pl.ds(off, CHUNK)
        kq = (kqf_ref, kqb_ref)[d][rows, :]
        v = (vf_ref, vb_ref)[d][rows, :]
        g_all = (gf_ref, gb_ref)[d][rows, :] + bg
        out_ref = (hf_ref, hb_ref)[d]
        tri = (r >= c) if d == 0 else (c >= r)
        tri_b = jnp.where(tri, 1.0, 0.0).astype(BF16)
        gi = g_all if d == 0 else pltpu.roll(g_all, CHUNK - 16, 1)
        gfp = pltpu.roll(g_all, CHUNK - 8 - 16 * d, 1)
        lf = jnp.minimum(gfp, 0.0) - jnp.log(1.0 + jnp.exp(-jnp.abs(gfp)))
        l1 = lf.astype(BF16)
        r1 = lf - l1.astype(F32)
        l2 = r1.astype(BF16)
        l3 = (r1 - l2.astype(F32)).astype(BF16)
        bcum = _dg(tri_b, l1) + _dg(tri_b, l2) + _dg(tri_b, l3)
        gtot = bcum[CHUNK - 1:CHUNK, :] if d == 0 else bcum[0:1, :]
        acol = gtot - bcum + gi
        m_loc = jnp.max(acol, axis=0, keepdims=True)
        m_st = m_scr[d, 0:1, :]
        m_new = jnp.maximum(gtot + m_st, m_loc)
        sp8 = jnp.broadcast_to(jnp.exp(gtot + m_st - m_new), (8, CHUNK))
        wst = jnp.exp(acol - m_new)
        rr = gi - bcum
        cm = rr
        for sh in (1, 2, 4, 8, 16, 32, 64):
            if d == 0:
                cm = jnp.maximum(cm, jnp.where(r >= sh, pltpu.roll(cm, sh, 0), NEG))
            else:
                cm = jnp.maximum(cm, jnp.where(r < CHUNK - sh, pltpu.roll(cm, CHUNK - sh, 0), NEG))
        mt = jnp.maximum(bcum + m_st, bcum + cm)
        c1 = bcum - mt
        rt = jnp.transpose(rr)
        wt = jnp.transpose(wst)
        m8 = jnp.broadcast_to(m_st, (8, CHUNK))
        kts = {}

        for h in range(N_HEADS):
            p, half = divmod(h, 2)
            lm = (c // QK_HEAD) == half
            kp = kq[:, p * 128:(p + 1) * 128]
            qp = kq[:, QK_W + p * 128:QK_W + (p + 1) * 128]
            vaug = jnp.concatenate([v[:, h * 128:(h + 1) * 128], ones_b], axis=1)
            qm = jnp.where(lm, qp, jnp.zeros_like(qp))
            c1b = lane_bcast(c1, h)
            pm = jnp.exp(jnp.where(tri, c1b + rt[h:h + 1, :], NEG))
            s = (_dg(qm, kp, _NT) * pm).astype(BF16)
            m_in = jnp.tile(lane_bcast(m8, h), (CHUNK // 8, 1))
            qs = (qm.astype(F32) * jnp.exp(c1b + m_in)).astype(BF16)
            st = s_scr[d * N_HEADS + h]
            tot = _dg(jnp.concatenate([s, qs], axis=1), jnp.concatenate([vaug, st.astype(BF16)], axis=0))
            den = jnp.maximum(jnp.abs(tot[:, 128:]), jnp.exp(-lane_bcast(mt, h)))
            out_ref[rows, h * 128:(h + 1) * 128] = (tot[:, :128] / den).astype(BF16)
            if p not in kts:
                kts[p] = jnp.transpose(kp.astype(F32))
            kw = jnp.where((r // QK_HEAD) == half, kts[p] * wt[h:h + 1, :], 0.0).astype(BF16)
            spb = jnp.tile(lane_bcast(sp8, h, 256), (CHUNK // 8, 1))
            s_scr[d * N_HEADS + h] = spb * st + _dg(kw, vaug)
        m_scr[d, 0:1, :] = m_new

    for sub in range(cps):
        one_chunk(0, sub * CHUNK)
        one_chunk(1, (cps - 1 - sub) * CHUNK)

    @pl.when(j == pl.num_programs(0) - 1)
    def _():
        sfin_ref[...] = s_scr[...]
        mfin_ref[...] = m_scr[...]


def _mlstm_call(kq, z, gates, bg, s0, m0):
    m = kq.shape[0]
    cps = math.gcd(m // CHUNK, MLSTM_CHUNKS_PER_STEP)
    rows = CHUNK * cps
    nc = m // rows
    fwd = lambda cb: (lambda j: (j, cb))
    bwd = lambda cb: (lambda j: (nc - 1 - j, cb))
    st_spec = pl.BlockSpec((2 * N_HEADS, CHUNK, 256), lambda j: (0, 0, 0))
    m_spec = pl.BlockSpec((2, 8, 128), lambda j: (0, 0, 0))
    return pl.pallas_call(
        functools.partial(_mlstm_body, cps),
        grid=(nc,),
        in_specs=[pl.BlockSpec((rows, 1024), fwd(0)), pl.BlockSpec((rows, 1024), fwd(ZC_V)),
                  pl.BlockSpec((rows, GATE_PAD), fwd(0)),
                  pl.BlockSpec((rows, 1024), bwd(0)), pl.BlockSpec((rows, 1024), bwd(ZC_V)),
                  pl.BlockSpec((rows, GATE_PAD), bwd(0)),
                  pl.BlockSpec((1, GATE_PAD), lambda j: (0, 0)), st_spec, m_spec],
        out_specs=[pl.BlockSpec((rows, 1024), fwd(0)), pl.BlockSpec((rows, 1024), bwd(0)), st_spec, m_spec],
        out_shape=[jax.ShapeDtypeStruct((m, 1024), BF16), jax.ShapeDtypeStruct((m, 1024), BF16),
                   jax.ShapeDtypeStruct((2 * N_HEADS, CHUNK, 256), F32),
                   jax.ShapeDtypeStruct((2, 8, 128), F32)],
        scratch_shapes=[pltpu.VMEM((2 * N_HEADS, CHUNK, 256), F32), pltpu.VMEM((2, 8, 128), F32)],
        compiler_params=_cparams(("arbitrary",), 32),
        name="mlstm",
    )(kq, z, gates, kq, z, gates, bg, s0, m0)


def _filt_body(seq_len, ft_ref, w1_ref, b1_ref, f1_ref, w2_ref, b2_ref, f2_ref, w3_ref, b3_ref, rt_ref,
               kf_ref, l1_ref):
    i = pl.program_id(0)
    tn = ft_ref.shape[1]
    hp = tn // 2
    reps = tn // 128
    col = lambda ref: jnp.tile(ref[...], (1, reps))
    h1 = jnp.sin(col(f1_ref) * (_dot3(w1_ref[...], ft_ref[...]) + col(b1_ref)))
    h2 = jnp.sin(col(f2_ref) * (_dot3(w2_ref[...], h1) + col(b2_ref)))
    r = lax.broadcasted_iota(I32, (hp, HY_W), 0)
    n_fwd = (8 * i + (r & 7)) * FFT_N2 + (r >> 3)
    rates = rt_ref[...]
    halves = []
    l1 = jnp.zeros((1, HY_W), F32)
    for hx in range(2):
        h = (_dot3(h2[:, hx * hp:(hx + 1) * hp], w3_ref[:, hx * HY_W:(hx + 1) * HY_W], _TN)
             + b3_ref[:, hx * HY_W:(hx + 1) * HY_W])
        n = n_fwd + hx * seq_len
        t01 = jnp.where(n <= seq_len, n, 2 * seq_len - n).astype(F32) / float(max(seq_len - 1, 1))
        h = jnp.where(n == seq_len, 0.0, h * jnp.exp(-t01 * rates))
        l1 = l1 + jnp.sum(jnp.abs(h), axis=0, keepdims=True)
        halves.append(h)
    word = _pack_pair(halves[0], halves[1])
    for b in range(FFT_N2):
        kf_ref[:, b * HY_W:(b + 1) * HY_W] = word[8 * b:8 * b + 8, :]

    @pl.when(i == 0)
    def _():
        l1_ref[...] = jnp.zeros_like(l1_ref)

    l1_ref[...] += l1


def _filt_call(feats_t, w1t, b1, f1, w2t, b2, f2, w3, b3, rates):
    n = feats_t.shape[1]
    seq_len = n // 2
    tn = 2 * 8 * FFT_N2
    c64 = lambda shape: pl.BlockSpec(shape, lambda i: (0, 0))
    return pl.pallas_call(
        functools.partial(_filt_body, seq_len),
        grid=(n // tn,),
        in_specs=[pl.BlockSpec((64, tn), lambda i: (0, i)),
                  c64((64, 64)), c64((64, 128)), c64((64, 128)), c64((64, 64)), c64((64, 128)), c64((64, 128)),
                  c64((64, 2 * HY_W)), c64((1, 2 * HY_W)), c64((1, HY_W))],
        out_specs=[pl.BlockSpec((8, FFT_N2 * HY_W), lambda i: (i, 0)), pl.BlockSpec((1, HY_W), lambda i: (0, 0))],
        out_shape=[jax.ShapeDtypeStruct((seq_len // FFT_N2, FFT_N2 * HY_W), U32),
                   jax.ShapeDtypeStruct((1, HY_W), F32)],
        compiler_params=_cparams(("arbitrary",), 48),
        name="filt",
    )(feats_t, w1t, b1, f1, w2t, b2, f2, w3, b3, rates)


def _fft_consts(n1_rows):
    n = n1_rows * FFT_N2
    kv = n1_rows // 2 + 1
    k1 = np.arange(FFT_KP, dtype=np.float64)
    valid = (k1 < kv).astype(np.float64)
    n1 = np.arange(n1_rows, dtype=np.float64)
    th1 = 2.0 * np.pi * np.outer(k1, n1) / n1_rows
    f1 = np.concatenate([np.cos(th1) * valid[:, None], -np.sin(th1) * valid[:, None]], axis=0)
    n2 = np.arange(FFT_N2, dtype=np.float64)
    tht = 2.0 * np.pi * np.outer(k1, n2) / n
    rep = lambda a: jnp.broadcast_to(jnp.asarray(a, F32)[:, :, None], (FFT_KP, FFT_N2, 128))
    twr = rep(np.cos(tht) * valid[:, None])
    twi = rep(-np.sin(tht) * valid[:, None])
    th2 = 2.0 * np.pi * np.outer(n2, n2) / FFT_N2
    cs, sn = np.cos(th2), np.sin(th2)
    f2p = np.block([[cs, sn], [-sn, cs]])
    f2pc = np.block([[cs, -sn], [sn, cs]])
    wk = np.where((k1 == 0) | (k1 == kv - 1), 1.0, 2.0) * valid / n
    half = n1_rows // 2
    thi = 2.0 * np.pi * np.outer(n1[:half], k1) / n1_rows
    gc = np.cos(thi) * wk[None, :]
    gs = np.sin(thi) * wk[None, :]
    as_bf = lambda a: jnp.asarray(a, F32).astype(BF16)
    return dict(f1=as_bf(f1), twr=twr, twi=twi,
                f2p=as_bf(f2p), f2pc=as_bf(f2pc), gc=as_bf(gc), gs=as_bf(gs))


def _fft1_body(f_ref, x_ref, ar_ref, ai_ref):
    o = _dg(f_ref[...], x_ref[...])
    ar_ref[...] = o[:FFT_KP].astype(BF16)
    ai_ref[...] = o[FFT_KP:].astype(BF16)


def _fft1_packed_body(f_ref, x_ref, ar_ref, ai_ref):
    k = x_ref.shape[0]
    hi, lo = _unpack_pair(x_ref[...])
    o = _dg(f_ref[:, 0:k], hi.astype(BF16)) + _dg(f_ref[:, k:], lo.astype(BF16))
    ar_ref[...] = o[:FFT_KP].astype(BF16)
    ai_ref[...] = o[FFT_KP:].astype(BF16)


def _fft1_call(f1, x2d, cb, packed=False):
    k, cols = x2d.shape
    f1 = f1[:, :2 * k] if packed else f1[:, :k]
    out = pl.BlockSpec((FFT_KP, cb), lambda i: (0, i))
    sh = jax.ShapeDtypeStruct((FFT_KP, cols), BF16)
    return pl.pallas_call(
        _fft1_packed_body if packed else _fft1_body,
        grid=(cols // cb,),
        in_specs=[pl.BlockSpec(f1.shape, lambda i: (0, 0)), pl.BlockSpec((k, cb), lambda i: (0, i))],
        out_specs=[out, out],
        out_shape=[sh, sh],
        compiler_params=_cparams(("arbitrary",), 48),
        name="fft1",
    )(f1, x2d)


def _twiddled(ar_ref, ai_ref, twr_ref, twi_ref, reps):
    a_r = ar_ref[...].astype(F32)
    a_i = ai_ref[...].astype(F32)
    tr = jnp.tile(twr_ref[...], (1, reps))
    ti = jnp.tile(twi_ref[...], (1, reps))
    st = jnp.concatenate([a_r * tr - a_i * ti, a_r * ti + a_i * tr], axis=0).astype(BF16)
    return st, tr, ti


FFT2_ROWS = 8


def _fft2_filt_body(kv, ar_ref, ai_ref, twr_ref, twi_ref, f2p_ref, k_ref):
    for j in range(FFT2_ROWS):
        k1 = pl.program_id(0) * FFT2_ROWS + j

        @pl.when(k1 < kv)
        def _():
            st, _, _ = _twiddled(ar_ref.at[j], ai_ref.at[j], twr_ref.at[j], twi_ref.at[j], ar_ref.shape[-1] // 128)
            k_ref[j] = _dg(f2p_ref[...], st).astype(BF16)

        @pl.when(k1 >= kv)
        def _():
            k_ref[j] = jnp.zeros(k_ref.shape[1:], BF16)


def _fft2_conv_body(kv, ar_ref, ai_ref, twr_ref, twi_ref, k_ref, f2p_ref, f2pc_ref, br_ref, bi_ref):
    for j in range(FFT2_ROWS):
        k1 = pl.program_id(0) * FFT2_ROWS + j

        @pl.when(k1 < kv)
        def _():
            st, tr, ti = _twiddled(ar_ref.at[j], ai_ref.at[j], twr_ref.at[j], twi_ref.at[j], ar_ref.shape[-1] // 128)
            x = _dg(f2p_ref[...], st)
            xr, xi = x[:FFT_N2], x[FFT_N2:]
            kr = k_ref[j, :FFT_N2, :].astype(F32)
            ki = k_ref[j, FFT_N2:, :].astype(F32)
            sy = jnp.concatenate([xr * kr - xi * ki, xr * ki + xi * kr], axis=0).astype(BF16)
            b = _dg(f2pc_ref[...], sy)
            b_r, b_i = b[:FFT_N2], b[FFT_N2:]
            br_ref[j] = (b_r * tr + b_i * ti).astype(BF16)
            bi_ref[j] = (b_i * tr - b_r * ti).astype(BF16)

        @pl.when(k1 >= kv)
        def _():
            br_ref[j] = jnp.zeros(br_ref.shape[1:], BF16)
            bi_ref[j] = jnp.zeros(bi_ref.shape[1:], BF16)


def _fft2_specs(ch, kv):
    src = lambda i: (jnp.minimum(i, (kv - 1) // FFT2_ROWS), 0, 0)
    blk = pl.BlockSpec((FFT2_ROWS, FFT_N2, ch), src)
    tw = pl.BlockSpec((FFT2_ROWS, FFT_N2, 128), src)
    mat = pl.BlockSpec((2 * FFT_N2, 2 * FFT_N2), lambda i: (0, 0))
    return blk, tw, mat, src


def _fft2_filt_call(ar, ai, cst, kv):
    ch = ar.shape[-1]
    blk, tw, mat, _ = _fft2_specs(ch, kv)
    return pl.pallas_call(
        functools.partial(_fft2_filt_body, kv),
        grid=(FFT_KP // FFT2_ROWS,),
        in_specs=[blk, blk, tw, tw, mat],
        out_specs=pl.BlockSpec((FFT2_ROWS, 2 * FFT_N2, ch), lambda i: (i, 0, 0)),
        out_shape=jax.ShapeDtypeStruct((FFT_KP, 2 * FFT_N2, ch), BF16),
        compiler_params=_cparams(("arbitrary",), 48),
        name="fft2_filt",
    )(ar, ai, cst["twr"], cst["twi"], cst["f2p"])


def _fft2_conv_call(ar, ai, khat, cst, kv):
    ch = ar.shape[-1]
    blk, tw, mat, src = _fft2_specs(ch, kv)
    sh = jax.ShapeDtypeStruct((FFT_KP, FFT_N2, ch), BF16)
    out = pl.BlockSpec((FFT2_ROWS, FFT_N2, ch), lambda i: (i, 0, 0))
    return pl.pallas_call(
        functools.partial(_fft2_conv_body, kv),
        grid=(FFT_KP // FFT2_ROWS,),
        in_specs=[blk, blk, tw, tw, pl.BlockSpec((FFT2_ROWS, 2 * FFT_N2, ch), src), mat, mat],
        out_specs=[out, out],
        out_shape=[sh, sh],
        compiler_params=_cparams(("arbitrary",), 48),
        name="fft2_conv",
    )(ar, ai, cst["twr"], cst["twi"], khat, cst["f2p"], cst["f2pc"])


def _ifft1_body(gc_ref, gs_ref, br_ref, bi_ref, il_ref, o_ref):
    y = _dg(gc_ref[...], br_ref[...]) - _dg(gs_ref[...], bi_ref[...])
    o_ref[...] = (y * il_ref[...]).astype(BF16)


def _ifft1_call(cst, br2d, bi2d, il_t, rows, cb):
    cols = br2d.shape[1]
    g = pl.BlockSpec((rows, FFT_KP), lambda i: (0, 0))
    kb = pl.BlockSpec((FFT_KP, cb), lambda i: (0, i))
    xb = pl.BlockSpec((rows, cb), lambda i: (0, i))
    vb = pl.BlockSpec((1, cb), lambda i: (0, 0))
    return pl.pallas_call(
        _ifft1_body,
        grid=(cols // cb,),
        in_specs=[g, g, kb, kb, vb],
        out_specs=xb,
        out_shape=jax.ShapeDtypeStruct((rows, cols), BF16),
        compiler_params=_cparams(("arbitrary",), 32),
        name="ifft1",
    )(cst["gc"], cst["gs"], br2d, bi2d, il_t)


def _outproj_body(hf_ref, hb_ref, zo_ref, yc_ref, x0_ref, xv_ref, ds_ref, x_ref, erow_ref, ecol_ref, gh_ref,
                  wa_ref, wb_ref, g1_ref, gf_ref, sh_ref, sc_ref, wr_ref, w1s_ref, w3s_ref, w2s_ref,
                  x1_ref, h2_ref, s_ref, shared_ref):
    hs = hf_ref[...].astype(F32) + hb_ref[...].astype(F32)
    gh = gh_ref[...]
    parts = []
    for h in range(N_HEADS):
        hh = hs[:, h * 128:(h + 1) * 128]
        ms = jnp.mean(hh * hh, axis=-1, keepdims=True)
        parts.append(hh * lax.rsqrt(ms + EPS) * gh[:, h * 128:(h + 1) * 128])
    ym = jnp.concatenate(parts, axis=-1) * _sigmoid(zo_ref[...].astype(F32))
    yh = x0_ref[...].astype(F32) * (yc_ref[...].astype(F32) + ds_ref[...] * xv_ref[...].astype(F32))
    y = _dg(ym.astype(BF16), wa_ref[...]) + _dg(yh.astype(BF16), wb_ref[...])
    rp = x_ref.shape[0] // GRID_W
    erow8 = erow_ref[...]
    erow = erow8[0:rp, :]
    for q in range(1, 8 // rp):
        erow = jnp.where(pl.program_id(0) % (8 // rp) == q, erow8[q * rp:(q + 1) * rp, :], erow)
    x1 = _add_pos(x_ref[...], erow, ecol_ref[...]) + g1_ref[...] * y
    x1_ref[...] = x1
    h2 = _norm_mod(x1, gf_ref[...], sh_ref[...], sc_ref[...])
    half = h2.shape[1] // 2
    _store_slabs(h2_ref, _pack_pair(h2[:, :half], h2[:, half:]))
    s_ref[...] = _sigmoid(_dot3(wr_ref[...], h2, _NT))
    h2b = h2.astype(BF16)
    a = (_silu(_dg(h2b, w1s_ref[...])) * _dg(h2b, w3s_ref[...])).astype(BF16)
    shared_ref[...] = _dg(a, w2s_ref[...]).astype(BF16)


def _outproj_call(hf, hb, z, yc, x0c, xv, ds, x, erow, ecol, gh, wa, wb, g1, gf, sh2, sc2, wrt, w1s, w3s, w2s, tm):
    m, d = x.shape
    dsh = w1s.shape[1]
    row = lambda cb: pl.BlockSpec((tm, 1024), lambda i: (i, cb))
    vec = lambda n: pl.BlockSpec((1, n), lambda i: (0, 0))
    full = pl.BlockSpec((tm, d), lambda i: (i, 0))
    const = lambda r, c: pl.BlockSpec((r, c), lambda i: (0, 0))
    return pl.pallas_call(
        _outproj_body,
        grid=(m // tm,),
        in_specs=[row(0), row(0), row(ZC_O), row(0), row(0), row(0), vec(HY_W), full,
                  pl.BlockSpec((8, d // 2), lambda i: (i * (tm // GRID_W) // 8, 0)),
                  const(GRID_W, d // 2),
                  vec(MV_W),
                  const(MV_W, d), const(HY_W, d),
                  vec(d), vec(d), vec(d), vec(d),
                  const(N_EXPERTS, d), const(d, dsh), const(d, dsh), const(dsh, d)],
        out_specs=[full, pl.BlockSpec((tm * SLAB, SLAB_W), lambda i: (i, 0)),
                   pl.BlockSpec((N_EXPERTS, tm), lambda i: (0, i)), full],
        out_shape=[jax.ShapeDtypeStruct((m, d), F32), jax.ShapeDtypeStruct((m * SLAB, SLAB_W), U32),
                   jax.ShapeDtypeStruct((N_EXPERTS, m), F32), jax.ShapeDtypeStruct((m, d), BF16)],
        compiler_params=_cparams(("arbitrary",), 56),
        name="outproj",
    )(hf, hb, z, yc, x0c, xv, ds, x, erow, ecol, gh, wa, wb, g1, gf, sh2, sc2, wrt, w1s, w3s, w2s)


def _first_max(x, idx, sentinel):
    m = jnp.max(x, axis=0, keepdims=True)
    return m, jnp.min(jnp.where(x == m, idx, sentinel), axis=0, keepdims=True)


def _route_body(s_ref, b_ref, e_ref, w_ref, r_ref, cnt_ref, u_scr, run_scr):
    i = pl.program_id(0)
    tt = s_ref.shape[1]

    @pl.when(i == 0)
    def _():
        rr = lax.broadcasted_iota(I32, (tt, tt), 0)
        cc = lax.broadcasted_iota(I32, (tt, tt), 1)
        u_scr[...] = jnp.where(rr < cc, 1.0, 0.0).astype(BF16)
        run_scr[...] = jnp.zeros_like(run_scr)

    s = s_ref[...]
    sel = s + b_ref[...][:, 0:1]
    sub8 = lax.broadcasted_iota(I32, (E_PER_GROUP, tt), 0).astype(F32)
    gs = jnp.zeros((N_GROUPS, tt), F32)
    for g in range(N_GROUPS):
        grp = sel[g * E_PER_GROUP:(g + 1) * E_PER_GROUP, :]
        m1, i1 = _first_max(grp, sub8, float(E_PER_GROUP))
        m2 = jnp.max(jnp.where(sub8 == i1, -jnp.inf, grp), axis=0, keepdims=True)
        gs = jnp.where(sub8 == g, m1 + m2, gs)
    gmask = jnp.zeros((N_GROUPS, tt), F32)
    for _ in range(TOPK_GROUPS):
        _, ig = _first_max(gs, sub8, float(N_GROUPS))
        hit = sub8 == ig
        gmask = jnp.where(hit, 1.0, gmask)
        gs = jnp.where(hit, -jnp.inf, gs)
    masked = jnp.concatenate(
        [jnp.where(jnp.broadcast_to(gmask[g:g + 1, :], (E_PER_GROUP, tt)) > 0.5,
                   sel[g * E_PER_GROUP:(g + 1) * E_PER_GROUP, :], -jnp.inf) for g in range(N_GROUPS)], axis=0)
    sub64 = lax.broadcasted_iota(I32, (N_EXPERTS, tt), 0).astype(F32)
    oh = jnp.zeros((N_EXPERTS, tt), F32)
    eks, wks = [], []
    for _ in range(TOP_K):
        _, ie = _first_max(masked, sub64, float(N_EXPERTS))
        hit = sub64 == ie
        wks.append(jnp.sum(jnp.where(hit, s, 0.0), axis=0, keepdims=True))
        eks.append(ie)
        masked = jnp.where(hit, -jnp.inf, masked)
        oh = jnp.where(hit, 1.0, oh)
    wsum = wks[0]
    for k in range(1, TOP_K):
        wsum = wsum + wks[k]
    run = run_scr[...]
    rank_t = _dg(oh.astype(BF16), u_scr[...]) + jnp.tile(run, (1, tt // 128))
    for k in range(TOP_K):
        e_ref[k:k + 1, :] = eks[k].astype(I32)
        w_ref[k:k + 1, :] = wks[k] / wsum * ROUTE_SCALE
        r_ref[k:k + 1, :] = jnp.sum(jnp.where(sub64 == eks[k], rank_t, 0.0), axis=0, keepdims=True).astype(I32)
    run_new = run + jnp.sum(oh, axis=1, keepdims=True)
    run_scr[...] = run_new
    cnt_ref[...] = run_new.astype(I32)


def _route_call(s_t, b_col, tt):
    m = s_t.shape[1]
    out = pl.BlockSpec((TOP_K, tt), lambda i: (0, i))
    return pl.pallas_call(
        _route_body,
        grid=(m // tt,),
        in_specs=[pl.BlockSpec((N_EXPERTS, tt), lambda i: (0, i)),
                  pl.BlockSpec((N_EXPERTS, 128), lambda i: (0, 0))],
        out_specs=[out, out, out, pl.BlockSpec((N_EXPERTS, 128), lambda i: (0, 0))],
        out_shape=[jax.ShapeDtypeStruct((TOP_K, m), I32), jax.ShapeDtypeStruct((TOP_K, m), F32),
                   jax.ShapeDtypeStruct((TOP_K, m), I32), jax.ShapeDtypeStruct((N_EXPERTS, 128), I32)],
        scratch_shapes=[pltpu.VMEM((tt, tt), BF16), pltpu.VMEM((N_EXPERTS, 128), F32)],
        compiler_params=_cparams(("arbitrary",), 32),
        name="route",
    )(s_t, b_col)


def _posk_body(pst_ref, e_ref, r_ref, p_ref):
    e = e_ref[...]
    acc = r_ref[...]
    for x in range(N_EXPERTS):
        acc = acc + jnp.where(e == x, pst_ref[x], 0)
    p_ref[...] = acc


def _posk_call(pstart, eidx, rank):
    k, m = eidx.shape
    tt = min(m, 2048)
    blk = pl.BlockSpec((k, tt), lambda i, pst: (0, i))
    return pl.pallas_call(
        _posk_body,
        grid_spec=pltpu.PrefetchScalarGridSpec(num_scalar_prefetch=1, grid=(m // tt,),
                                               in_specs=[blk, blk], out_specs=blk),
        out_shape=jax.ShapeDtypeStruct((k, m), I32),
        compiler_params=_cparams(("arbitrary",), 32),
        name="posk",
    )(pstart, eidx, rank)


def _slab(ref, r):
    return ref.at[pl.ds(pl.multiple_of(r * SLAB, SLAB), SLAB), :]


def _ffn_packed(x_ref, rows, w1, w3, w2, base=0):
    half = w1.shape[0] // 2
    xa, xb = _unpack_pair(_load_slabs(x_ref, rows, base))
    xa = xa.astype(BF16)
    xb = xb.astype(BF16)
    h1 = _dg(xa, w1[0:half, :]) + _dg(xb, w1[half:, :])
    h3 = _dg(xa, w3[0:half, :]) + _dg(xb, w3[half:, :])
    return _dg((_silu(h1) * h3).astype(BF16), w2[...])


def _dispatch_body(cnt_ref, pst_ref, pcn_ref, h2_ref, pos_ref, xs_ref, zrow, sem):
    i = pl.program_id(0)
    td = h2_ref.shape[0] // SLAB

    def row_copy(t, dst):
        return pltpu.make_async_copy(_slab(h2_ref, t), _slab(xs_ref, dst), sem)

    def issue(t, carry):
        for k in range(TOP_K):
            row_copy(t, pos_ref[k, t]).start(priority=k % 2)
        return carry

    lax.fori_loop(0, td, issue, 0)

    def drain(t, carry):
        for k in range(TOP_K):
            row_copy(0, 0).wait()
        return carry

    lax.fori_loop(0, td, drain, 0)

    @pl.when(i == pl.num_programs(0) - 1)
    def _():
        zrow[...] = jnp.zeros_like(zrow)

        def zero_copy(dst):
            return pltpu.make_async_copy(zrow, _slab(xs_ref, dst), sem)

        def per_expert(e, carry):
            base = pst_ref[e]
            lax.fori_loop(cnt_ref[e], pcn_ref[e], lambda rr, cc: (zero_copy(base + rr).start(), cc)[1], 0)
            lax.fori_loop(cnt_ref[e], pcn_ref[e], lambda rr, cc: (zero_copy(0).wait(), cc)[1], 0)
            return carry

        lax.fori_loop(0, N_EXPERTS, per_expert, 0)


def _dispatch_call(cnt, pstart, pcnt, h2s, pos, rows, td):
    m = h2s.shape[0] // SLAB
    return pl.pallas_call(
        _dispatch_body,
        grid_spec=pltpu.PrefetchScalarGridSpec(
            num_scalar_prefetch=3, grid=(m // td,),
            in_specs=[pl.BlockSpec((td * SLAB, SLAB_W), lambda i, *_: (i, 0)),
                      pl.BlockSpec((TOP_K, td), lambda i, *_: (0, i), memory_space=pltpu.SMEM)],
            out_specs=pl.BlockSpec(memory_space=pl.ANY),
            scratch_shapes=[pltpu.VMEM((SLAB, SLAB_W), U32), pltpu.SemaphoreType.DMA(())]),
        out_shape=jax.ShapeDtypeStruct((rows * SLAB, SLAB_W), U32),
        compiler_params=_cparams(("arbitrary",), 32),
        name="dispatch",
    )(cnt, pstart, pcnt, h2s, pos)


def _moe_body(te_ref, nu_ref, nxt_ref, par_ref, x_ref, w1_ref, w3_ref, w2_ref, y_ref,
              f1, f3, f2, w1b, w3b, w2b, sems):
    def fetch(ex, s):
        return (pltpu.make_async_copy(w1_ref.at[ex], f1.at[s], sems.at[s]),
                pltpu.make_async_copy(w3_ref.at[ex], f3.at[s], sems.at[s]),
                pltpu.make_async_copy(w2_ref.at[ex], f2.at[s], sems.at[s]))

    for sub in range(MOE_TILES_PER_STEP):
        ti = pl.program_id(0) * MOE_TILES_PER_STEP + sub
        used = ti < nu_ref[0]
        e = te_ref[ti]
        first = jnp.logical_or(ti == 0, e != te_ref[jnp.maximum(ti - 1, 0)])
        slot = par_ref[e]

        @pl.when(jnp.logical_and(used, ti == 0))
        def _():
            for c in fetch(e, slot):
                c.start()

        @pl.when(jnp.logical_and(used, first))
        def _():
            for c in fetch(e, slot):
                c.wait()
            nx = nxt_ref[e]

            @pl.when(nx < N_EXPERTS)
            def _():
                for c in fetch(nx, 1 - slot):
                    c.start()

            w1b[...] = f1[slot].astype(BF16)
            w3b[...] = f3[slot].astype(BF16)
            w2b[...] = f2[slot].astype(BF16)

        @pl.when(used)
        def _():
            half = w1b.shape[0] // 2
            base = sub * MOE_ROWS * SLAB
            y = _ffn_packed(x_ref, MOE_ROWS, w1b, w3b, w2b, base)
            _store_slabs(y_ref, _pack_pair(y[:, :half], y[:, half:]), base)


def _moe_call(tile_e, n_used, nxt, par, xs, w1, w3, w2):
    rows = xs.shape[0] // SLAB
    step_rows = MOE_ROWS * MOE_TILES_PER_STEP
    nt = rows // step_rows
    d, de = w1.shape[-2:]
    rmap = lambda i, te, nu, *_: (jnp.minimum(i, (nu[0] - 1) // MOE_TILES_PER_STEP), 0)
    hbm = pl.BlockSpec(memory_space=pl.ANY)
    return pl.pallas_call(
        _moe_body,
        grid_spec=pltpu.PrefetchScalarGridSpec(
            num_scalar_prefetch=4, grid=(nt,),
            in_specs=[pl.BlockSpec((step_rows * SLAB, SLAB_W), rmap), hbm, hbm, hbm],
            out_specs=pl.BlockSpec((step_rows * SLAB, SLAB_W), rmap),
            scratch_shapes=[pltpu.VMEM((2, d, de), F32), pltpu.VMEM((2, d, de), F32), pltpu.VMEM((2, de, d), F32),
                            pltpu.VMEM((d, de), BF16), pltpu.VMEM((d, de), BF16), pltpu.VMEM((de, d), BF16),
                            pltpu.SemaphoreType.DMA((2,))]),
        out_shape=jax.ShapeDtypeStruct((rows * SLAB, SLAB_W), U32),
        compiler_params=_cparams(("arbitrary",), 56),
        name="moe",
    )(tile_e, n_used, nxt, par, xs, w1, w3, w2)


def _final_body(x1_ref, sh_ref, pos_ref, posn_ref, wt_ref, ys_ref, g2_ref, gn_ref, o_ref,
                ybuf, acc_a, acc_b, sems):
    i = pl.program_id(0)
    n = pl.num_programs(0)
    tf, d = x1_ref.shape
    slot = i % 2
    slot_slabs = TOP_K * tf

    def row_copy(p_ref, s, k, t):
        return pltpu.make_async_copy(_slab(ys_ref, p_ref[k, t]), _slab(ybuf, s * slot_slabs + k * tf + t),
                                     sems.at[s])

    def issue_token(p_ref, s, t):
        for k in range(TOP_K):
            row_copy(p_ref, s, k, t).start(priority=k % 2)

    @pl.when(i == 0)
    def _():
        lax.fori_loop(0, tf, lambda t, c: (issue_token(pos_ref, 0, t), c)[1], 0)

    def drain(t, carry):
        for k in range(TOP_K):
            pltpu.make_async_copy(_slab(ys_ref, 0), _slab(ybuf, 0), sems.at[slot]).wait()
        return carry

    lax.fori_loop(0, tf, drain, 0)

    def combine_token(t):
        a = jnp.zeros((SLAB, SLAB_W), F32)
        b = jnp.zeros((SLAB, SLAB_W), F32)
        for k in range(TOP_K):
            ya, yb = _unpack_pair(_slab(ybuf, slot * slot_slabs + k * tf + t)[...])
            w = wt_ref[k, t]
            a = a + w * ya
            b = b + w * yb
        _slab(acc_a, t)[...] = a
        _slab(acc_b, t)[...] = b

    @pl.when(i + 1 < n)
    def _():
        def body(t, carry):
            combine_token(t)
            issue_token(posn_ref, 1 - slot, t)
            return carry

        lax.fori_loop(0, tf, body, 0)

    @pl.when(i + 1 >= n)
    def _():
        lax.fori_loop(0, tf, lambda t, c: (combine_token(t), c)[1], 0)

    moe = jnp.concatenate([_load_slabs(acc_a, tf), _load_slabs(acc_b, tf)], axis=1) + sh_ref[...].astype(F32)
    xo = x1_ref[...] + g2_ref[...] * moe
    ms = jnp.mean(xo * xo, axis=-1, keepdims=True)
    o_ref[...] = xo * lax.rsqrt(ms + EPS) * gn_ref[...]


def _final_call(x1, sh, pos, wts, ys, g2, gn, tf):
    m, d = x1.shape
    nt = m // tf
    full = pl.BlockSpec((tf, d), lambda i: (i, 0))
    vec = pl.BlockSpec((1, d), lambda i: (0, 0))
    return pl.pallas_call(
        _final_body,
        grid=(nt,),
        in_specs=[full, full,
                  pl.BlockSpec((TOP_K, tf), lambda i: (0, i), memory_space=pltpu.SMEM),
                  pl.BlockSpec((TOP_K, tf), lambda i: (0, jnp.minimum(i + 1, nt - 1)), memory_space=pltpu.SMEM),
                  pl.BlockSpec((TOP_K, tf), lambda i: (0, i), memory_space=pltpu.SMEM),
                  pl.BlockSpec(memory_space=pl.ANY), vec, vec],
        out_specs=full,
        out_shape=jax.ShapeDtypeStruct((m, d), F32),
        scratch_shapes=[pltpu.VMEM((2 * TOP_K * tf * SLAB, SLAB_W), U32),
                        pltpu.VMEM((tf * SLAB, SLAB_W), F32), pltpu.VMEM((tf * SLAB, SLAB_W), F32),
                        pltpu.SemaphoreType.DMA((2,))],
        compiler_params=_cparams(("arbitrary",), 48),
        name="final",
    )(x1, sh, pos, pos, wts, ys, g2, gn)


def _pos_tables(n_tokens):
    rows = n_tokens // GRID_W
    quarter = D_MODEL // 4
    omega = 1.0 / (10000.0 ** (jnp.arange(quarter, dtype=F32) / quarter))

    def emb1d(pos):
        ang = pos[:, None] * omega[None]
        return jnp.concatenate([jnp.sin(ang), jnp.cos(ang)], axis=-1)

    return emb1d(jnp.arange(rows, dtype=F32)), emb1d(jnp.arange(GRID_W, dtype=F32))


def _filter_feats(L):
    n1h = L // FFT_N2
    i_, h_, b_, a_ = jnp.meshgrid(jnp.arange(n1h // 8, dtype=I32), jnp.arange(2, dtype=I32),
                                  jnp.arange(FFT_N2, dtype=I32), jnp.arange(8, dtype=I32), indexing="ij")
    n = ((8 * i_ + a_ + h_ * n1h) * FFT_N2 + b_).reshape(-1)
    t = jnp.where(n <= L, n, 2 * L - n).astype(F32)
    t01 = t / max(L - 1, 1)
    w = 2.0 * math.pi * t / L
    bands = jnp.linspace(1e-4, FILT_BANDS - 1, FILT_BANDS, dtype=F32)
    feats = jnp.concatenate([t01[None, :], jnp.cos(bands[:, None] * w[None, :]), -jnp.sin(bands[:, None] * w[None, :]),
                             jnp.zeros((64 - 33, 2 * L), F32)], axis=0)
    return feats


def _pad_rows(a, rows):
    return jnp.concatenate([a, jnp.zeros((rows - a.shape[0],) + a.shape[1:], a.dtype)], axis=0)


def _layer(x, c, ctx, c_ctx, w_ada, b_ada, g_mix, g_ffn, w_in, b_gates, conv_k_w, conv_k_b,
           conv_q_w, conv_q_b, g_head, conv_hy_w, conv_hy_b, filt_w1, filt_b1, filt_freq1,
           filt_w2, filt_b2, filt_freq2, filt_w3, filt_b3, hy_dskip, w_out, w_router, b_router,
           w1_e, w3_e, w2_e, w1_s, w3_s, w2_s, g_final):
    L, d = x.shape
    lc = ctx.shape[0]
    row = lambda v: v.reshape(1, -1)

    cc = jnp.broadcast_to(jnp.stack([c, c_ctx], axis=0)[:, :, None], (2, d, 128))
    mods = _mod_call(cc, w_ada, row(b_ada))
    sh1, sc1, g1, sh2, sc2, g2 = [mods[0:1, k * d:(k + 1) * d] for k in range(6)]
    csh1, csc1 = mods[1:2, 0:d], mods[1:2, d:2 * d]

    w_r, w_g = _wprep_call(jnp.transpose(w_in[0]))
    bg = jnp.concatenate([b_gates, jnp.zeros((GATE_PAD - 4 * N_HEADS,), F32)]).reshape(1, GATE_PAD)
    e_row, e_col = _pos_tables(L)
    conv_w = jnp.concatenate([conv_k_w, conv_q_w], axis=1)
    conv_b = jnp.concatenate([conv_k_b, conv_q_b]).reshape(1, -1)
    conv_s = jnp.concatenate([jnp.ones((QK_W,), F32), jnp.full((QK_W,), QK_HEAD ** -0.5, F32)]).reshape(1, -1)

    z_c, gt_c = _inproj_call(ctx, jnp.zeros((8, d // 2), F32), e_col, row(g_mix), csh1, csc1, w_r, w_g,
                             use_pos=False, tm=min(lc, 256))
    kq_c = _conv_kq_call(z_c, conv_w, conv_b, conv_s, tm=min(lc, 256))
    s0 = jnp.zeros((2 * N_HEADS, CHUNK, 256), F32)
    m0 = jnp.zeros((2, 8, 128), F32)
    _, _, s_ctx, m_ctx = _mlstm_call(kq_c, z_c, gt_c, bg, s0, m0)

    z, gates = _inproj_call(x, e_row, e_col, row(g_mix), sh1, sc1, w_r, w_g, use_pos=True, tm=min(L, 1024))
    kq = _conv_kq_call(z, conv_w, conv_b, conv_s, tm=min(L, 1024))
    x0c, xv = _conv_hy_call(z, conv_hy_w, row(conv_hy_b), tm=min(L, 512))
    hf, hb, _, _ = _mlstm_call(kq, z, gates, bg, s_ctx, m_ctx)

    n1 = 2 * L // FFT_N2
    cst = _fft_consts(n1)
    rates = jnp.linspace(-math.log(DECAY_TARGET) / SLOW_DECAY_PCT, -math.log(DECAY_TARGET) / FAST_DECAY_PCT,
                         HY_W, dtype=F32).reshape(1, -1)
    w1t = jnp.transpose(_pad_rows(filt_w1, 64))
    colrep = lambda v: jnp.broadcast_to(v.reshape(-1, 1), (v.shape[0], 128))
    kf, l1 = _filt_call(_filter_feats(L), w1t, colrep(filt_b1), colrep(filt_freq1), jnp.transpose(filt_w2),
                        colrep(filt_b2), colrep(filt_freq2), filt_w3, row(filt_b3), rates)
    cols = FFT_N2 * HY_W
    cb = 8192
    kar, kai = _fft1_call(cst["f1"], kf, cb, packed=True)
    kv = n1 // 2 + 1
    khat = _fft2_filt_call(kar.reshape(FFT_KP, FFT_N2, HY_W), kai.reshape(FFT_KP, FFT_N2, HY_W), cst, kv)
    uar, uai = _fft1_call(cst["f1"], xv.reshape(n1 // 2, cols), cb)
    br, bi = _fft2_conv_call(uar.reshape(FFT_KP, FFT_N2, HY_W), uai.reshape(FFT_KP, FFT_N2, HY_W), khat, cst, kv)
    reps = cb // HY_W
    il_t = jnp.tile(1.0 / l1, (1, reps))
    yc = _ifft1_call(cst, br.reshape(FFT_KP, cols), bi.reshape(FFT_KP, cols), il_t, n1 // 2, cb).reshape(L, HY_W)

    wo = w_out.astype(BF16)
    x1, h2s, s_t, sh = _outproj_call(hf, hb, z, yc, x0c, xv, row(hy_dskip), x, e_row, e_col, row(g_head),
                                    wo[:MV_W], wo[MV_W:], g1, row(g_ffn), sh2, sc2, jnp.transpose(w_router),
                                    w1_s.astype(BF16), w3_s.astype(BF16), w2_s.astype(BF16), tm=min(L, 256))

    b_col = jnp.broadcast_to(b_router.reshape(N_EXPERTS, 1), (N_EXPERTS, 128))
    eidx, wts, rank, cnt2 = _route_call(s_t, b_col, tt=min(L, 1024))
    cnt = cnt2[:, 0]
    pcnt = (cnt + MOE_ROWS - 1) // MOE_ROWS * MOE_ROWS
    pend = jnp.cumsum(pcnt)
    pstart = pend - pcnt
    rows = L * TOP_K + N_EXPERTS * MOE_ROWS
    nt = rows // MOE_ROWS
    tile_row = jnp.arange(nt, dtype=I32) * MOE_ROWS
    tile_e = jnp.minimum(jnp.sum((pend[None, :] <= tile_row[:, None]).astype(I32), axis=1), N_EXPERTS - 1)
    n_used = (pend[-1] // MOE_ROWS).astype(I32).reshape(1)
    pos = _posk_call(pstart.astype(I32), eidx, rank)

    xs = _dispatch_call(cnt, pstart.astype(I32), pcnt.astype(I32), h2s, pos, rows, td=min(L, 512))
    ex = jnp.arange(N_EXPERTS, dtype=I32)
    nonempty = pcnt > 0
    nxt = jnp.min(jnp.where((ex[None, :] > ex[:, None]) & nonempty[None, :], ex[None, :], N_EXPERTS), axis=1)
    par = (jnp.cumsum(nonempty.astype(I32)) + 1) % 2
    ys = _moe_call(tile_e, n_used, nxt.astype(I32), par.astype(I32), xs, w1_e, w3_e, w2_e)
    return _final_call(x1, sh, pos, wts, ys, g2, row(g_final), tf=min(L, 256))


def kernel(x, c, ctx, c_ctx, w_ada, b_ada, g_mix, g_ffn, w_in, b_gates, conv_k_w, conv_k_b, conv_q_w,
           conv_q_b, g_head, conv_hy_w, conv_hy_b, filt_w1, filt_b1, filt_freq1, filt_w2, filt_b2,
           filt_freq2, filt_w3, filt_b3, hy_dskip, w_out, w_router, b_router, w1_e, w3_e, w2_e,
           w1_s, w3_s, w2_s, g_final):
    assert x.shape[0] == 1 and w_ada.shape[0] == 1, "one batch element, one layer"
    out = _layer(x[0], c[0], ctx[0], c_ctx, w_ada[0], b_ada[0], g_mix[0], g_ffn[0], w_in, b_gates[0],
                 conv_k_w[0], conv_k_b[0], conv_q_w[0], conv_q_b[0], g_head[0], conv_hy_w[0], conv_hy_b[0],
                 filt_w1[0], filt_b1[0], filt_freq1[0], filt_w2[0], filt_b2[0], filt_freq2[0], filt_w3[0],
                 filt_b3[0], hy_dskip[0], w_out[0], w_router[0], b_router[0], w1_e[0], w3_e[0], w2_e[0],
                 w1_s[0], w3_s[0], w2_s[0], g_final)
    return out[None]
```

```python
import functools
import math

import numpy as np
import jax
import jax.numpy as jnp
from jax import lax
from jax.experimental import pallas as pl
from jax.experimental.pallas import tpu as pltpu

F32 = jnp.float32
BF16 = jnp.bfloat16
I32 = jnp.int32
U32 = jnp.uint32

D_MODEL = 2048
GRID_W = 64
N_HEADS = 8
QK_HEAD = 64
V_HEAD = 128
QK_W = N_HEADS * QK_HEAD
MV_W = N_HEADS * V_HEAD
HY_W = D_MODEL - MV_W
CHUNK = 128
FILT_BANDS = 16
FILT_HIDDEN = 64
DECAY_TARGET = 1e-2
FAST_DECAY_PCT = 0.3
SLOW_DECAY_PCT = 1.5
N_EXPERTS = 64
N_GROUPS = 8
E_PER_GROUP = 8
TOPK_GROUPS = 4
TOP_K = 8
D_EXPERT = 512
ROUTE_SCALE = 2.5
EPS = 1e-6
OFF_K = 0
OFF_V = OFF_K + QK_W
OFF_G = OFF_V + MV_W
OFF_Q = OFF_G + 4 * N_HEADS
OFF_O = OFF_Q + QK_W
OFF_HY = OFF_O + MV_W

ZC_KQ, ZC_V, ZC_O, ZC_X0, ZC_X1, ZC_HV = 0, 1, 2, 3, 4, 5
Z_COLS = 6 * 1024
GATE_PAD = 128

NEG = -1e30
MIB = 1024 * 1024
MXU_DEPTH = 256

FFT_N2 = 128
FFT_KP = 144

MOE_ROWS = 256
MOE_TILES_PER_STEP = 4
XRING = 3
MLSTM_CHUNKS_PER_STEP = 4


def _cparams(sem, vmem_mb, flags=None):
    return pltpu.CompilerParams(dimension_semantics=sem, vmem_limit_bytes=vmem_mb * MIB, flags=flags)


def _split2(x):
    hi = x.astype(BF16)
    lo = (x - hi.astype(F32)).astype(BF16)
    return hi, lo


_NN = (((1,), (0,)), ((), ()))
_NT = (((1,), (1,)), ((), ()))
_TN = (((0,), (0,)), ((), ()))


def _dg(a, b, dims=_NN):
    return lax.dot_general(a, b, dims, preferred_element_type=F32)


def _dot3(a, b, dims=_NN):
    ah, al = _split2(a)
    bh, bl = _split2(b)
    ka, kb = dims[0][0][0], dims[0][1][0]
    if 3 * a.shape[ka] <= MXU_DEPTH:
        return _dg(jnp.concatenate([ah, al, ah], axis=ka), jnp.concatenate([bh, bh, bl], axis=kb), dims)
    return _dg(ah, bh, dims) + _dg(al, bh, dims) + _dg(ah, bl, dims)


def _sigmoid(x):
    return 1.0 / (1.0 + jnp.exp(-x))


def _silu(x):
    return x * _sigmoid(x)


def _pack_pair(a, b):
    hi = lax.bitcast_convert_type(a.astype(BF16).astype(F32), U32)
    lo = lax.bitcast_convert_type(b.astype(BF16).astype(F32), U32)
    return hi | (lo >> 16)


def _unpack_pair(w):
    a = lax.bitcast_convert_type(w & jnp.uint32(0xFFFF0000), F32)
    b = lax.bitcast_convert_type(w << 16, F32)
    return a, b


SLAB = 8
SLAB_W = 128


def _store_slabs(ref, w, base=0):
    r = w.shape[0]
    for j in range(SLAB):
        ref[pl.ds(base + j, r, stride=SLAB), :] = w[:, j * SLAB_W:(j + 1) * SLAB_W]


def _load_slabs(ref, r, base=0):
    return jnp.concatenate([ref[pl.ds(base + j, r, stride=SLAB), :] for j in range(SLAB)], axis=1)


def _norm_mod(x, g, sh, sc):
    ms = jnp.mean(x * x, axis=-1, keepdims=True)
    return (x * lax.rsqrt(ms + EPS) * g) * (1.0 + sc) + sh


def _add_pos(x, erow, ecol):
    tm, d = x.shape
    half = d // 2
    parts = []
    for r in range(tm // GRID_W):
        xs = x[r * GRID_W:(r + 1) * GRID_W, :]
        parts.append(jnp.concatenate([xs[:, :half] + erow[r:r + 1, :], xs[:, half:] + ecol], axis=-1))
    return parts[0] if len(parts) == 1 else jnp.concatenate(parts, axis=0)


def _mod_body(cc_ref, w_ref, b_ref, o_ref):
    w = w_ref[...]
    nv = cc_ref.shape[0]
    reps = w.shape[1] // 128
    rows = [jnp.sum(w * jnp.tile(_silu(cc_ref[v]), (1, reps)), axis=0, keepdims=True) for v in range(nv)]
    part = jnp.concatenate(rows + [jnp.zeros((8 - nv, w.shape[1]), F32)], axis=0)

    @pl.when(pl.program_id(0) == 0)
    def _():
        o_ref[...] = jnp.broadcast_to(b_ref[...], o_ref.shape)

    o_ref[...] += part


def _mod_call(cc, w, b):
    d, n = w.shape
    tr = 256
    return pl.pallas_call(
        _mod_body,
        grid=(d // tr,),
        in_specs=[pl.BlockSpec((cc.shape[0], tr, 128), lambda j: (0, j, 0)),
                  pl.BlockSpec((tr, n), lambda j: (j, 0)),
                  pl.BlockSpec((1, n), lambda j: (0, 0))],
        out_specs=pl.BlockSpec((8, n), lambda j: (0, 0)),
        out_shape=jax.ShapeDtypeStruct((8, n), F32),
        compiler_params=_cparams(("arbitrary",), 40),
        name="mod",
    )(cc, w, b)


def _wprep_body(w_ref, wr_ref, wg_ref):
    w = w_ref[...]
    wt = jnp.concatenate([w[OFF_K:OFF_V], w[OFF_Q:OFF_O], w[OFF_V:OFF_G], w[OFF_O:]], axis=0)
    wr_ref[...] = jnp.transpose(wt).astype(BF16)
    g = jnp.concatenate([w[OFF_G:OFF_Q], jnp.zeros((GATE_PAD - 4 * N_HEADS, w.shape[1]), F32)], axis=0)
    wg_ref[...] = jnp.transpose(g)


def _wprep_call(w_t):
    n, d = w_t.shape
    tr = 256
    return pl.pallas_call(
        _wprep_body,
        grid=(d // tr,),
        in_specs=[pl.BlockSpec((n, tr), lambda i: (0, i))],
        out_specs=[pl.BlockSpec((tr, Z_COLS), lambda i: (i, 0)), pl.BlockSpec((tr, GATE_PAD), lambda i: (i, 0))],
        out_shape=[jax.ShapeDtypeStruct((d, Z_COLS), BF16), jax.ShapeDtypeStruct((d, GATE_PAD), F32)],
        compiler_params=_cparams(("arbitrary",), 32),
        name="wprep",
    )(w_t)


def _inproj_body(use_pos, x_ref, erow_ref, ecol_ref, gm_ref, sh_ref, sc_ref, w_ref, wg_ref,
                 z_ref, g_ref, h_scr):
    @pl.when(pl.program_id(1) == 0)
    def _():
        x = x_ref[...]
        if use_pos:
            x = _add_pos(x, erow_ref[...], ecol_ref[...])
        h = _norm_mod(x, gm_ref[...], sh_ref[...], sc_ref[...])
        h_scr[...] = h.astype(BF16)
        g_ref[...] = _dg(h_scr[...], wg_ref[...].astype(BF16))

    z_ref[...] = jnp.dot(h_scr[...], w_ref[...], preferred_element_type=F32).astype(BF16)


def _inproj_call(x, erow, ecol, gm, sh, sc, w, wg, use_pos, tm):
    m, d = x.shape
    tn = 1024
    er = tm // GRID_W if use_pos else erow.shape[0]
    row_map = (lambda i, j: (i, 0)) if use_pos else (lambda i, j: (0, 0))
    return pl.pallas_call(
        functools.partial(_inproj_body, use_pos),
        grid=(m // tm, Z_COLS // tn),
        in_specs=[pl.BlockSpec((tm, d), lambda i, j: (i, 0)),
                  pl.BlockSpec((er, d // 2), row_map),
                  pl.BlockSpec((GRID_W, d // 2), lambda i, j: (0, 0)),
                  pl.BlockSpec((1, d), lambda i, j: (0, 0)),
                  pl.BlockSpec((1, d), lambda i, j: (0, 0)),
                  pl.BlockSpec((1, d), lambda i, j: (0, 0)),
                  pl.BlockSpec((d, tn), lambda i, j: (0, j)),
                  pl.BlockSpec((d, GATE_PAD), lambda i, j: (0, 0))],
        out_specs=[pl.BlockSpec((tm, tn), lambda i, j: (i, j)),
                   pl.BlockSpec((tm, GATE_PAD), lambda i, j: (i, 0))],
        out_shape=[jax.ShapeDtypeStruct((m, Z_COLS), BF16),
                   jax.ShapeDtypeStruct((m, GATE_PAD), F32)],
        scratch_shapes=[pltpu.VMEM((tm, d), BF16)],
        compiler_params=_cparams(("arbitrary", "arbitrary"), 48),
        name="inproj",
    )(x, erow, ecol, gm, sh, sc, w, wg)


def _conv3(zc, zp, zn, w, b, first, last):
    tm = zc.shape[0]
    row = lax.broadcasted_iota(I32, zc.shape, 0)
    prev_row = jnp.where(first, 0.0, zp[7:8, :])
    next_row = jnp.where(last, 0.0, zn[0:1, :])
    xm = jnp.where(row == 0, prev_row, pltpu.roll(zc, 1, 0))
    xp = jnp.where(row == tm - 1, next_row, pltpu.roll(zc, tm - 1, 0))
    return xm * w[0:1, :] + zc * w[1:2, :] + xp * w[2:3, :] + b


def _conv_kq_body(zc_ref, zp_ref, zn_ref, w_ref, b_ref, s_ref, o_ref):
    i = pl.program_id(0)
    u = _conv3(zc_ref[...].astype(F32), zp_ref[...].astype(F32), zn_ref[...].astype(F32),
               w_ref[...], b_ref[...], i == 0, i == pl.num_programs(0) - 1)
    o_ref[...] = (_silu(u) * s_ref[...]).astype(BF16)


def _halo_specs(tm, m, cb):
    nb8 = m // 8
    return [pl.BlockSpec((tm, 1024), lambda i: (i, cb)),
            pl.BlockSpec((8, 1024), lambda i: (jnp.maximum(i * (tm // 8) - 1, 0), cb)),
            pl.BlockSpec((8, 1024), lambda i: (jnp.minimum((i + 1) * (tm // 8), nb8 - 1), cb))]


def _conv_kq_call(z, w, b, s, tm):
    m = z.shape[0]
    vec = pl.BlockSpec((1, 1024), lambda i: (0, 0))
    return pl.pallas_call(
        _conv_kq_body,
        grid=(m // tm,),
        in_specs=_halo_specs(tm, m, ZC_KQ) + [pl.BlockSpec((3, 1024), lambda i: (0, 0)), vec, vec],
        out_specs=pl.BlockSpec((tm, 1024), lambda i: (i, 0)),
        out_shape=jax.ShapeDtypeStruct((m, 1024), BF16),
        compiler_params=_cparams(("arbitrary",), 32),
        name="conv_kq",
    )(z, z, z, w, b, s)


def _conv_hy_body(ac_ref, ap_ref, an_ref, bc_ref, bp_ref, bn_ref, cc_ref, cp_ref, cn_ref,
                  w_ref, b_ref, x0_ref, xv_ref):
    i = pl.program_id(0)
    first, last = i == 0, i == pl.num_programs(0) - 1
    w = w_ref[...]
    b = b_ref[...]

    def cv(c, p, n, k):
        return _conv3(c[...].astype(F32), p[...].astype(F32), n[...].astype(F32),
                      w[:, k * 1024:(k + 1) * 1024], b[:, k * 1024:(k + 1) * 1024], first, last)

    x0_ref[...] = cv(ac_ref, ap_ref, an_ref, 0).astype(BF16)
    xv_ref[...] = (cv(bc_ref, bp_ref, bn_ref, 1) * cv(cc_ref, cp_ref, cn_ref, 2)).astype(BF16)


def _conv_hy_call(z, w, b, tm):
    m = z.shape[0]
    out = pl.BlockSpec((tm, 1024), lambda i: (i, 0))
    return pl.pallas_call(
        _conv_hy_body,
        grid=(m // tm,),
        in_specs=(_halo_specs(tm, m, ZC_X0) + _halo_specs(tm, m, ZC_X1) + _halo_specs(tm, m, ZC_HV)
                  + [pl.BlockSpec((3, 3072), lambda i: (0, 0)), pl.BlockSpec((1, 3072), lambda i: (0, 0))]),
        out_specs=[out, out],
        out_shape=[jax.ShapeDtypeStruct((m, 1024), BF16), jax.ShapeDtypeStruct((m, 1024), BF16)],
        compiler_params=_cparams(("arbitrary",), 32),
        name="conv_hy",
    )(z, z, z, z, z, z, z, z, z, w, b)


def _mlstm_body(cps, kqf_ref, vf_ref, gf_ref, kqb_ref, vb_ref, gb_ref, bg_ref, s0_ref, m0_ref,
                hf_ref, hb_ref, sfin_ref, mfin_ref, s_scr, m_scr):
    j = pl.program_id(0)

    @pl.when(j == 0)
    def _():
        s_scr[...] = s0_ref[...]
        m_scr[...] = m0_ref[...]

    r = lax.broadcasted_iota(I32, (CHUNK, CHUNK), 0)
    c = lax.broadcasted_iota(I32, (CHUNK, CHUNK), 1)
    ones_b = jnp.ones((CHUNK, CHUNK), BF16)
    bg = bg_ref[...]

    def lane_bcast(x, h, width=CHUNK):
        return jnp.broadcast_to(x[:, h:h + 1], (x.shape[0], width))

    def one_chunk(d, off):
        rows = pl.ds(off, CHUNK)
        kq = (kqf_ref, kqb_ref)[d][rows, :]
        v = (vf_ref, vb_ref)[d][rows, :]
        g_all = (gf_ref, gb_ref)[d][rows, :] + bg
        out_ref = (hf_ref, hb_ref)[d]
        tri = (r >= c) if d == 0 else (c >= r)
        tri_b = jnp.where(tri, 1.0, 0.0).astype(BF16)
        gi = g_all if d == 0 else pltpu.roll(g_all, CHUNK - 16, 1)
        gfp = pltpu.roll(g_all, CHUNK - 8 - 16 * d, 1)
        lf = jnp.minimum(gfp, 0.0) - jnp.log(1.0 + jnp.exp(-jnp.abs(gfp)))
        l1 = lf.astype(BF16)
        r1 = lf - l1.astype(F32)
        l2 = r1.astype(BF16)
        l3 = (r1 - l2.astype(F32)).astype(BF16)
        bcum = _dg(tri_b, l1) + _dg(tri_b, l2) + _dg(tri_b, l3)
        gtot = bcum[CHUNK - 1:CHUNK, :] if d == 0 else bcum[0:1, :]
        acol = gtot - bcum + gi
        m_loc = jnp.max(acol, axis=0, keepdims=True)
        m_st = m_scr[d, 0:1, :]
        m_new = jnp.maximum(gtot + m_st, m_loc)
        sp8 = jnp.broadcast_to(jnp.exp(gtot + m_st - m_new), (8, CHUNK))
        wst = jnp.exp(acol - m_new)
        rr = gi - bcum
        cm = rr
        for sh in (1, 2, 4, 8, 16, 32, 64):
            if d == 0:
                cm = jnp.maximum(cm, jnp.where(r >= sh, pltpu.roll(cm, sh, 0), NEG))
            else:
                cm = jnp.maximum(cm, jnp.where(r < CHUNK - sh, pltpu.roll(cm, CHUNK - sh, 0), NEG))
        mt = jnp.maximum(bcum + m_st, bcum + cm)
        c1 = bcum - mt
        rt = jnp.transpose(rr)
        wt = jnp.transpose(wst)
        m8 = jnp.broadcast_to(m_st, (8, CHUNK))
        kts = {}

        for h in range(N_HEADS):
            p, half = divmod(h, 2)
            lm = (c // QK_HEAD) == half
            kp = kq[:, p * 128:(p + 1) * 128]
            qp = kq[:, QK_W + p * 128:QK_W + (p + 1) * 128]
            vaug = jnp.concatenate([v[:, h * 128:(h + 1) * 128], ones_b], axis=1)
            qm = jnp.where(lm, qp, jnp.zeros_like(qp))
            c1b = lane_bcast(c1, h)
            pm = jnp.exp(jnp.where(tri, c1b + rt[h:h + 1, :], NEG))
            s = (_dg(qm, kp, _NT) * pm).astype(BF16)
            m_in = jnp.tile(lane_bcast(m8, h), (CHUNK // 8, 1))
            qs = (qm.astype(F32) * jnp.exp(c1b + m_in)).astype(BF16)
            st = s_scr[d * N_HEADS + h]
            tot = _dg(jnp.concatenate([s, qs], axis=1), jnp.concatenate([vaug, st.astype(BF16)], axis=0))
            den = jnp.maximum(jnp.abs(tot[:, 128:]), jnp.exp(-lane_bcast(mt, h)))
            out_ref[rows, h * 128:(h + 1) * 128] = (tot[:, :128] / den).astype(BF16)
            if p not in kts:
                kts[p] = jnp.transpose(kp.astype(F32))
            kw = jnp.where((r // QK_HEAD) == half, kts[p] * wt[h:h + 1, :], 0.0).astype(BF16)
            spb = jnp.tile(lane_bcast(sp8, h, 256), (CHUNK // 8, 1))
            s_scr[d * N_HEADS + h] = spb * st + _dg(kw, vaug)
        m_scr[d, 0:1, :] = m_new

    for sub in range(cps):
        one_chunk(0, sub * CHUNK)
        one_chunk(1, (cps - 1 - sub) * CHUNK)

    @pl.when(j == pl.num_programs(0) - 1)
    def _():
        sfin_ref[...] = s_scr[...]
        mfin_ref[...] = m_scr[...]


def _mlstm_call(kq, z, gates, bg, s0, m0):
    m = kq.shape[0]
    cps = math.gcd(m // CHUNK, MLSTM_CHUNKS_PER_STEP)
    rows = CHUNK * cps
    nc = m // rows
    fwd = lambda cb: (lambda j: (j, cb))
    bwd = lambda cb: (lambda j: (nc - 1 - j, cb))
    st_spec = pl.BlockSpec((2 * N_HEADS, CHUNK, 256), lambda j: (0, 0, 0))
    m_spec = pl.BlockSpec((2, 8, 128), lambda j: (0, 0, 0))
    return pl.pallas_call(
        functools.partial(_mlstm_body, cps),
        grid=(nc,),
        in_specs=[pl.BlockSpec((rows, 1024), fwd(0)), pl.BlockSpec((rows, 1024), fwd(ZC_V)),
                  pl.BlockSpec((rows, GATE_PAD), fwd(0)),
                  pl.BlockSpec((rows, 1024), bwd(0)), pl.BlockSpec((rows, 1024), bwd(ZC_V)),
                  pl.BlockSpec((rows, GATE_PAD), bwd(0)),
                  pl.BlockSpec((1, GATE_PAD), lambda j: (0, 0)), st_spec, m_spec],
        out_specs=[pl.BlockSpec((rows, 1024), fwd(0)), pl.BlockSpec((rows, 1024), bwd(0)), st_spec, m_spec],
        out_shape=[jax.ShapeDtypeStruct((m, 1024), BF16), jax.ShapeDtypeStruct((m, 1024), BF16),
                   jax.ShapeDtypeStruct((2 * N_HEADS, CHUNK, 256), F32),
                   jax.ShapeDtypeStruct((2, 8, 128), F32)],
        scratch_shapes=[pltpu.VMEM((2 * N_HEADS, CHUNK, 256), F32), pltpu.VMEM((2, 8, 128), F32)],
        compiler_params=_cparams(("arbitrary",), 32),
        name="mlstm",
    )(kq, z, gates, kq, z, gates, bg, s0, m0)


def _filt_body(seq_len, ft_ref, w1_ref, b1_ref, f1_ref, w2_ref, b2_ref, f2_ref, w3_ref, b3_ref, rt_ref,
               kf_ref, l1_ref):
    i = pl.program_id(0)
    tn = ft_ref.shape[1]
    hp = tn // 2
    reps = tn // 128
    col = lambda ref: jnp.tile(ref[...], (1, reps))
    h1 = jnp.sin(col(f1_ref) * (_dot3(w1_ref[...], ft_ref[...]) + col(b1_ref)))
    h2 = jnp.sin(col(f2_ref) * (_dot3(w2_ref[...], h1) + col(b2_ref)))
    r = lax.broadcasted_iota(I32, (hp, HY_W), 0)
    n_fwd = (8 * i + (r & 7)) * FFT_N2 + (r >> 3)
    rates = rt_ref[...]
    halves = []
    l1 = jnp.zeros((1, HY_W), F32)
    for hx in range(2):
        h = (_dot3(h2[:, hx * hp:(hx + 1) * hp], w3_ref[:, hx * HY_W:(hx + 1) * HY_W], _TN)
             + b3_ref[:, hx * HY_W:(hx + 1) * HY_W])
        n = n_fwd + hx * seq_len
        t01 = jnp.where(n <= seq_len, n, 2 * seq_len - n).astype(F32) / float(max(seq_len - 1, 1))
        h = jnp.where(n == seq_len, 0.0, h * jnp.exp(-t01 * rates))
        l1 = l1 + jnp.sum(jnp.abs(h), axis=0, keepdims=True)
        halves.append(h)
    word = _pack_pair(halves[0], halves[1])
    for b in range(FFT_N2):
        kf_ref[:, b * HY_W:(b + 1) * HY_W] = word[8 * b:8 * b + 8, :]

    @pl.when(i == 0)
    def _():
        l1_ref[...] = jnp.zeros_like(l1_ref)

    l1_ref[...] += l1


def _filt_call(feats_t, w1t, b1, f1, w2t, b2, f2, w3, b3, rates):
    n = feats_t.shape[1]
    seq_len = n // 2
    tn = 2 * 8 * FFT_N2
    c64 = lambda shape: pl.BlockSpec(shape, lambda i: (0, 0))
    return pl.pallas_call(
        functools.partial(_filt_body, seq_len),
        grid=(n // tn,),
        in_specs=[pl.BlockSpec((64, tn), lambda i: (0, i)),
                  c64((64, 64)), c64((64, 128)), c64((64, 128)), c64((64, 64)), c64((64, 128)), c64((64, 128)),
                  c64((64, 2 * HY_W)), c64((1, 2 * HY_W)), c64((1, HY_W))],
        out_specs=[pl.BlockSpec((8, FFT_N2 * HY_W), lambda i: (i, 0)), pl.BlockSpec((1, HY_W), lambda i: (0, 0))],
        out_shape=[jax.ShapeDtypeStruct((seq_len // FFT_N2, FFT_N2 * HY_W), U32),
                   jax.ShapeDtypeStruct((1, HY_W), F32)],
        compiler_params=_cparams(("arbitrary",), 48),
        name="filt",
    )(feats_t, w1t, b1, f1, w2t, b2, f2, w3, b3, rates)


def _fft_consts(n1_rows):
    n = n1_rows * FFT_N2
    kv = n1_rows // 2 + 1
    k1 = np.arange(FFT_KP, dtype=np.float64)
    valid = (k1 < kv).astype(np.float64)
    n1 = np.arange(n1_rows, dtype=np.float64)
    th1 = 2.0 * np.pi * np.outer(k1, n1) / n1_rows
    f1 = np.concatenate([np.cos(th1) * valid[:, None], -np.sin(th1) * valid[:, None]], axis=0)
    n2 = np.arange(FFT_N2, dtype=np.float64)
    tht = 2.0 * np.pi * np.outer(k1, n2) / n
    rep = lambda a: jnp.broadcast_to(jnp.asarray(a, F32)[:, :, None], (FFT_KP, FFT_N2, 128))
    twr = rep(np.cos(tht) * valid[:, None])
    twi = rep(-np.sin(tht) * valid[:, None])
    th2 = 2.0 * np.pi * np.outer(n2, n2) / FFT_N2
    cs, sn = np.cos(th2), np.sin(th2)
    f2p = np.block([[cs, sn], [-sn, cs]])
    f2pc = np.block([[cs, -sn], [sn, cs]])
    wk = np.where((k1 == 0) | (k1 == kv - 1), 1.0, 2.0) * valid / n
    half = n1_rows // 2
    thi = 2.0 * np.pi * np.outer(n1[:half], k1) / n1_rows
    gc = np.cos(thi) * wk[None, :]
    gs = np.sin(thi) * wk[None, :]
    as_bf = lambda a: jnp.asarray(a, F32).astype(BF16)
    return dict(f1=as_bf(f1), twr=twr, twi=twi,
                f2p=as_bf(f2p), f2pc=as_bf(f2pc), gc=as_bf(gc), gs=as_bf(gs))


def _fft1_body(f_ref, x_ref, ar_ref, ai_ref):
    o = _dg(f_ref[...], x_ref[...])
    ar_ref[...] = o[:FFT_KP].astype(BF16)
    ai_ref[...] = o[FFT_KP:].astype(BF16)


def _fft1_packed_body(f_ref, x_ref, ar_ref, ai_ref):
    k = x_ref.shape[0]
    hi, lo = _unpack_pair(x_ref[...])
    o = _dg(f_ref[:, 0:k], hi.astype(BF16)) + _dg(f_ref[:, k:], lo.astype(BF16))
    ar_ref[...] = o[:FFT_KP].astype(BF16)
    ai_ref[...] = o[FFT_KP:].astype(BF16)


def _fft1_call(f1, x2d, cb, packed=False):
    k, cols = x2d.shape
    f1 = f1[:, :2 * k] if packed else f1[:, :k]
    out = pl.BlockSpec((FFT_KP, cb), lambda i: (0, i))
    sh = jax.ShapeDtypeStruct((FFT_KP, cols), BF16)
    return pl.pallas_call(
        _fft1_packed_body if packed else _fft1_body,
        grid=(cols // cb,),
        in_specs=[pl.BlockSpec(f1.shape, lambda i: (0, 0)), pl.BlockSpec((k, cb), lambda i: (0, i))],
        out_specs=[out, out],
        out_shape=[sh, sh],
        compiler_params=_cparams(("arbitrary",), 48),
        name="fft1",
    )(f1, x2d)


def _twiddled(ar_ref, ai_ref, twr_ref, twi_ref, reps):
    a_r = ar_ref[...].astype(F32)
    a_i = ai_ref[...].astype(F32)
    tr = jnp.tile(twr_ref[...], (1, reps))
    ti = jnp.tile(twi_ref[...], (1, reps))
    st = jnp.concatenate([a_r * tr - a_i * ti, a_r * ti + a_i * tr], axis=0).astype(BF16)
    return st, tr, ti


FFT2_ROWS = 8


def _fft2_filt_body(kv, ar_ref, ai_ref, twr_ref, twi_ref, f2p_ref, k_ref):
    for j in range(FFT2_ROWS):
        k1 = pl.program_id(0) * FFT2_ROWS + j

        @pl.when(k1 < kv)
        def _():
            st, _, _ = _twiddled(ar_ref.at[j], ai_ref.at[j], twr_ref.at[j], twi_ref.at[j], ar_ref.shape[-1] // 128)
            k_ref[j] = _dg(f2p_ref[...], st).astype(BF16)

        @pl.when(k1 >= kv)
        def _():
            k_ref[j] = jnp.zeros(k_ref.shape[1:], BF16)


def _fft2_conv_body(kv, ar_ref, ai_ref, twr_ref, twi_ref, k_ref, f2p_ref, f2pc_ref, br_ref, bi_ref):
    for j in range(FFT2_ROWS):
        k1 = pl.program_id(0) * FFT2_ROWS + j

        @pl.when(k1 < kv)
        def _():
            st, tr, ti = _twiddled(ar_ref.at[j], ai_ref.at[j], twr_ref.at[j], twi_ref.at[j], ar_ref.shape[-1] // 128)
            x = _dg(f2p_ref[...], st)
            xr, xi = x[:FFT_N2], x[FFT_N2:]
            kr = k_ref[j, :FFT_N2, :].astype(F32)
            ki = k_ref[j, FFT_N2:, :].astype(F32)
            sy = jnp.concatenate([xr * kr - xi * ki, xr * ki + xi * kr], axis=0).astype(BF16)
            b = _dg(f2pc_ref[...], sy)
            b_r, b_i = b[:FFT_N2], b[FFT_N2:]
            br_ref[j] = (b_r * tr + b_i * ti).astype(BF16)
            bi_ref[j] = (b_i * tr - b_r * ti).astype(BF16)

        @pl.when(k1 >= kv)
        def _():
            br_ref[j] = jnp.zeros(br_ref.shape[1:], BF16)
            bi_ref[j] = jnp.zeros(bi_ref.shape[1:], BF16)


def _fft2_specs(ch, kv):
    src = lambda i: (jnp.minimum(i, (kv - 1) // FFT2_ROWS), 0, 0)
    blk = pl.BlockSpec((FFT2_ROWS, FFT_N2, ch), src)
    tw = pl.BlockSpec((FFT2_ROWS, FFT_N2, 128), src)
    mat = pl.BlockSpec((2 * FFT_N2, 2 * FFT_N2), lambda i: (0, 0))
    return blk, tw, mat, src


def _fft2_filt_call(ar, ai, cst, kv):
    ch = ar.shape[-1]
    blk, tw, mat, _ = _fft2_specs(ch, kv)
    return pl.pallas_call(
        functools.partial(_fft2_filt_body, kv),
        grid=(FFT_KP // FFT2_ROWS,),
        in_specs=[blk, blk, tw, tw, mat],
        out_specs=pl.BlockSpec((FFT2_ROWS, 2 * FFT_N2, ch), lambda i: (i, 0, 0)),
        out_shape=jax.ShapeDtypeStruct((FFT_KP, 2 * FFT_N2, ch), BF16),
        compiler_params=_cparams(("arbitrary",), 48),
        name="fft2_filt",
    )(ar, ai, cst["twr"], cst["twi"], cst["f2p"])


def _fft2_conv_call(ar, ai, khat, cst, kv):
    ch = ar.shape[-1]
    blk, tw, mat, src = _fft2_specs(ch, kv)
    sh = jax.ShapeDtypeStruct((FFT_KP, FFT_N2, ch), BF16)
    out = pl.BlockSpec((FFT2_ROWS, FFT_N2, ch), lambda i: (i, 0, 0))
    return pl.pallas_call(
        functools.partial(_fft2_conv_body, kv),
        grid=(FFT_KP // FFT2_ROWS,),
        in_specs=[blk, blk, tw, tw, pl.BlockSpec((FFT2_ROWS, 2 * FFT_N2, ch), src), mat, mat],
        out_specs=[out, out],
        out_shape=[sh, sh],
        compiler_params=_cparams(("arbitrary",), 48),
        name="fft2_conv",
    )(ar, ai, cst["twr"], cst["twi"], khat, cst["f2p"], cst["f2pc"])


def _ifft1_body(gc_ref, gs_ref, br_ref, bi_ref, il_ref, o_ref):
    y = _dg(gc_ref[...], br_ref[...]) - _dg(gs_ref[...], bi_ref[...])
    o_ref[...] = (y * il_ref[...]).astype(BF16)


def _ifft1_call(cst, br2d, bi2d, il_t, rows, cb):
    cols = br2d.shape[1]
    g = pl.BlockSpec((rows, FFT_KP), lambda i: (0, 0))
    kb = pl.BlockSpec((FFT_KP, cb), lambda i: (0, i))
    xb = pl.BlockSpec((rows, cb), lambda i: (0, i))
    vb = pl.BlockSpec((1, cb), lambda i: (0, 0))
    return pl.pallas_call(
        _ifft1_body,
        grid=(cols // cb,),
        in_specs=[g, g, kb, kb, vb],
        out_specs=xb,
        out_shape=jax.ShapeDtypeStruct((rows, cols), BF16),
        compiler_params=_cparams(("arbitrary",), 32),
        name="ifft1",
    )(cst["gc"], cst["gs"], br2d, bi2d, il_t)


def _outproj_body(hf_ref, hb_ref, zo_ref, yc_ref, x0_ref, xv_ref, ds_ref, x_ref, erow_ref, ecol_ref, gh_ref,
                  wa_ref, wb_ref, g1_ref, gf_ref, sh_ref, sc_ref, wr_ref, w1s_ref, w3s_ref, w2s_ref,
                  x1_ref, h2_ref, s_ref, shared_ref):
    hs = hf_ref[...].astype(F32) + hb_ref[...].astype(F32)
    gh = gh_ref[...]
    parts = []
    for h in range(N_HEADS):
        hh = hs[:, h * 128:(h + 1) * 128]
        ms = jnp.mean(hh * hh, axis=-1, keepdims=True)
        parts.append(hh * lax.rsqrt(ms + EPS) * gh[:, h * 128:(h + 1) * 128])
    ym = jnp.concatenate(parts, axis=-1) * _sigmoid(zo_ref[...].astype(F32))
    yh = x0_ref[...].astype(F32) * (yc_ref[...].astype(F32) + ds_ref[...] * xv_ref[...].astype(F32))
    y = _dg(ym.astype(BF16), wa_ref[...]) + _dg(yh.astype(BF16), wb_ref[...])
    rp = x_ref.shape[0] // GRID_W
    erow8 = erow_ref[...]
    erow = erow8[0:rp, :]
    for q in range(1, 8 // rp):
        erow = jnp.where(pl.program_id(0) % (8 // rp) == q, erow8[q * rp:(q + 1) * rp, :], erow)
    x1 = _add_pos(x_ref[...], erow, ecol_ref[...]) + g1_ref[...] * y
    x1_ref[...] = x1
    h2 = _norm_mod(x1, gf_ref[...], sh_ref[...], sc_ref[...])
    half = h2.shape[1] // 2
    _store_slabs(h2_ref, _pack_pair(h2[:, :half], h2[:, half:]))
    s_ref[...] = _sigmoid(_dot3(wr_ref[...], h2, _NT))
    h2b = h2.astype(BF16)
    a = (_silu(_dg(h2b, w1s_ref[...])) * _dg(h2b, w3s_ref[...])).astype(BF16)
    shared_ref[...] = _dg(a, w2s_ref[...]).astype(BF16)


def _outproj_call(hf, hb, z, yc, x0c, xv, ds, x, erow, ecol, gh, wa, wb, g1, gf, sh2, sc2, wrt, w1s, w3s, w2s, tm):
    m, d = x.shape
    dsh = w1s.shape[1]
    row = lambda cb: pl.BlockSpec((tm, 1024), lambda i: (i, cb))
    vec = lambda n: pl.BlockSpec((1, n), lambda i: (0, 0))
    full = pl.BlockSpec((tm, d), lambda i: (i, 0))
    const = lambda r, c: pl.BlockSpec((r, c), lambda i: (0, 0))
    return pl.pallas_call(
        _outproj_body,
        grid=(m // tm,),
        in_specs=[row(0), row(0), row(ZC_O), row(0), row(0), row(0), vec(HY_W), full,
                  pl.BlockSpec((8, d // 2), lambda i: (i * (tm // GRID_W) // 8, 0)),
                  const(GRID_W, d // 2),
                  vec(MV_W),
                  const(MV_W, d), const(HY_W, d),
                  vec(d), vec(d), vec(d), vec(d),
                  const(N_EXPERTS, d), const(d, dsh), const(d, dsh), const(dsh, d)],
        out_specs=[full, pl.BlockSpec((tm * SLAB, SLAB_W), lambda i: (i, 0)),
                   pl.BlockSpec((N_EXPERTS, tm), lambda i: (0, i)), full],
        out_shape=[jax.ShapeDtypeStruct((m, d), F32), jax.ShapeDtypeStruct((m * SLAB, SLAB_W), U32),
                   jax.ShapeDtypeStruct((N_EXPERTS, m), F32), jax.ShapeDtypeStruct((m, d), BF16)],
        compiler_params=_cparams(("arbitrary",), 56),
        name="outproj",
    )(hf, hb, z, yc, x0c, xv, ds, x, erow, ecol, gh, wa, wb, g1, gf, sh2, sc2, wrt, w1s, w3s, w2s)


def _first_max(x, idx, sentinel):
    m = jnp.max(x, axis=0, keepdims=True)
    return m, jnp.min(jnp.where(x == m, idx, sentinel), axis=0, keepdims=True)


def _route_body(s_ref, b_ref, e_ref, w_ref, r_ref, cnt_ref, u_scr, run_scr):
    i = pl.program_id(0)
    tt = s_ref.shape[1]

    @pl.when(i == 0)
    def _():
        rr = lax.broadcasted_iota(I32, (tt, tt), 0)
        cc = lax.broadcasted_iota(I32, (tt, tt), 1)
        u_scr[...] = jnp.where(rr < cc, 1.0, 0.0).astype(BF16)
        run_scr[...] = jnp.zeros_like(run_scr)

    s = s_ref[...]
    sel = s + b_ref[...][:, 0:1]
    sub8 = lax.broadcasted_iota(I32, (E_PER_GROUP, tt), 0).astype(F32)
    gs = jnp.zeros((N_GROUPS, tt), F32)
    for g in range(N_GROUPS):
        grp = sel[g * E_PER_GROUP:(g + 1) * E_PER_GROUP, :]
        m1, i1 = _first_max(grp, sub8, float(E_PER_GROUP))
        m2 = jnp.max(jnp.where(sub8 == i1, -jnp.inf, grp), axis=0, keepdims=True)
        gs = jnp.where(sub8 == g, m1 + m2, gs)
    gmask = jnp.zeros((N_GROUPS, tt), F32)
    for _ in range(TOPK_GROUPS):
        _, ig = _first_max(gs, sub8, float(N_GROUPS))
        hit = sub8 == ig
        gmask = jnp.where(hit, 1.0, gmask)
        gs = jnp.where(hit, -jnp.inf, gs)
    masked = jnp.concatenate(
        [jnp.where(jnp.broadcast_to(gmask[g:g + 1, :], (E_PER_GROUP, tt)) > 0.5,
                   sel[g * E_PER_GROUP:(g + 1) * E_PER_GROUP, :], -jnp.inf) for g in range(N_GROUPS)], axis=0)
    sub64 = lax.broadcasted_iota(I32, (N_EXPERTS, tt), 0).astype(F32)
    oh = jnp.zeros((N_EXPERTS, tt), F32)
    eks, wks = [], []
    for _ in range(TOP_K):
        _, ie = _first_max(masked, sub64, float(N_EXPERTS))
        hit = sub64 == ie
        wks.append(jnp.sum(jnp.where(hit, s, 0.0), axis=0, keepdims=True))
        eks.append(ie)
        masked = jnp.where(hit, -jnp.inf, masked)
        oh = jnp.where(hit, 1.0, oh)
    wsum = wks[0]
    for k in range(1, TOP_K):
        wsum = wsum + wks[k]
    run = run_scr[...]
    rank_t = _dg(oh.astype(BF16), u_scr[...]) + jnp.tile(run, (1, tt // 128))
    for k in range(TOP_K):
        e_ref[k:k + 1, :] = eks[k].astype(I32)
        w_ref[k:k + 1, :] = wks[k] / wsum * ROUTE_SCALE
        r_ref[k:k + 1, :] = jnp.sum(jnp.where(sub64 == eks[k], rank_t, 0.0), axis=0, keepdims=True).astype(I32)
    run_new = run + jnp.sum(oh, axis=1, keepdims=True)
    run_scr[...] = run_new
    cnt_ref[...] = run_new.astype(I32)


def _route_call(s_t, b_col, tt):
    m = s_t.shape[1]
    out = pl.BlockSpec((TOP_K, tt), lambda i: (0, i))
    return pl.pallas_call(
        _route_body,
        grid=(m // tt,),
        in_specs=[pl.BlockSpec((N_EXPERTS, tt), lambda i: (0, i)),
                  pl.BlockSpec((N_EXPERTS, 128), lambda i: (0, 0))],
        out_specs=[out, out, out, pl.BlockSpec((N_EXPERTS, 128), lambda i: (0, 0))],
        out_shape=[jax.ShapeDtypeStruct((TOP_K, m), I32), jax.ShapeDtypeStruct((TOP_K, m), F32),
                   jax.ShapeDtypeStruct((TOP_K, m), I32), jax.ShapeDtypeStruct((N_EXPERTS, 128), I32)],
        scratch_shapes=[pltpu.VMEM((tt, tt), BF16), pltpu.VMEM((N_EXPERTS, 128), F32)],
        compiler_params=_cparams(("arbitrary",), 32),
        name="route",
    )(s_t, b_col)


def _posk_body(pst_ref, e_ref, r_ref, p_ref):
    e = e_ref[...]
    acc = r_ref[...]
    for x in range(N_EXPERTS):
        acc = acc + jnp.where(e == x, pst_ref[x], 0)
    p_ref[...] = acc


def _posk_call(pstart, eidx, rank):
    k, m = eidx.shape
    tt = min(m, 2048)
    blk = pl.BlockSpec((k, tt), lambda i, pst: (0, i))
    return pl.pallas_call(
        _posk_body,
        grid_spec=pltpu.PrefetchScalarGridSpec(num_scalar_prefetch=1, grid=(m // tt,),
                                               in_specs=[blk, blk], out_specs=blk),
        out_shape=jax.ShapeDtypeStruct((k, m), I32),
        compiler_params=_cparams(("arbitrary",), 32),
        name="posk",
    )(pstart, eidx, rank)


def _slab(ref, r):
    return ref.at[pl.ds(pl.multiple_of(r * SLAB, SLAB), SLAB), :]


def _ffn_packed(x_ref, rows, w1, w3, w2, base=0):
    half = w1.shape[0] // 2
    xa, xb = _unpack_pair(_load_slabs(x_ref, rows, base))
    xa = xa.astype(BF16)
    xb = xb.astype(BF16)
    h1 = _dg(xa, w1[0:half, :]) + _dg(xb, w1[half:, :])
    h3 = _dg(xa, w3[0:half, :]) + _dg(xb, w3[half:, :])
    return _dg((_silu(h1) * h3).astype(BF16), w2[...])


def _dispatch_body(cnt_ref, pst_ref, pcn_ref, h2_ref, pos_ref, xs_ref, zrow, sem):
    i = pl.program_id(0)
    td = h2_ref.shape[0] // SLAB

    def row_copy(t, dst):
        return pltpu.make_async_copy(_slab(h2_ref, t), _slab(xs_ref, dst), sem)

    def issue(t, carry):
        for k in range(TOP_K):
            row_copy(t, pos_ref[k, t]).start(priority=k % 2)
        return carry

    lax.fori_loop(0, td, issue, 0)

    def drain(t, carry):
        for k in range(TOP_K):
            row_copy(0, 0).wait()
        return carry

    lax.fori_loop(0, td, drain, 0)

    @pl.when(i == pl.num_programs(0) - 1)
    def _():
        zrow[...] = jnp.zeros_like(zrow)

        def zero_copy(dst):
            return pltpu.make_async_copy(zrow, _slab(xs_ref, dst), sem)

        def per_expert(e, carry):
            base = pst_ref[e]
            lax.fori_loop(cnt_ref[e], pcn_ref[e], lambda rr, cc: (zero_copy(base + rr).start(), cc)[1], 0)
            lax.fori_loop(cnt_ref[e], pcn_ref[e], lambda rr, cc: (zero_copy(0).wait(), cc)[1], 0)
            return carry

        lax.fori_loop(0, N_EXPERTS, per_expert, 0)


def _dispatch_call(cnt, pstart, pcnt, h2s, pos, rows, td):
    m = h2s.shape[0] // SLAB
    return pl.pallas_call(
        _dispatch_body,
        grid_spec=pltpu.PrefetchScalarGridSpec(
            num_scalar_prefetch=3, grid=(m // td,),
            in_specs=[pl.BlockSpec((td * SLAB, SLAB_W), lambda i, *_: (i, 0)),
                      pl.BlockSpec((TOP_K, td), lambda i, *_: (0, i), memory_space=pltpu.SMEM)],
            out_specs=pl.BlockSpec(memory_space=pl.ANY),
            scratch_shapes=[pltpu.VMEM((SLAB, SLAB_W), U32), pltpu.SemaphoreType.DMA(())]),
        out_shape=jax.ShapeDtypeStruct((rows * SLAB, SLAB_W), U32),
        compiler_params=_cparams(("arbitrary",), 32),
        name="dispatch",
    )(cnt, pstart, pcnt, h2s, pos)


def _moe_body(te_ref, nu_ref, nxt_ref, par_ref, x_ref, w1_ref, w3_ref, w2_ref, y_ref,
              f1, f3, f2, w1b, w3b, w2b, xbuf, sems, xsems):
    def fetch(ex, s):
        return (pltpu.make_async_copy(w1_ref.at[ex], f1.at[s], sems.at[s]),
                pltpu.make_async_copy(w3_ref.at[ex], f3.at[s], sems.at[s]),
                pltpu.make_async_copy(w2_ref.at[ex], f2.at[s], sems.at[s]))

    i = pl.program_id(0)
    blk = xbuf.shape[1]
    n_steps = (nu_ref[0] + MOE_TILES_PER_STEP - 1) // MOE_TILES_PER_STEP

    def xfetch(step):
        s = step % XRING
        return pltpu.make_async_copy(x_ref.at[pl.ds(pl.multiple_of(step * blk, SLAB), blk), :], xbuf.at[s],
                                     xsems.at[s])

    for ahead in range(XRING - 1):
        @pl.when(jnp.logical_and(i == 0, ahead < n_steps))
        def _():
            xfetch(ahead).start()

    @pl.when(i + XRING - 1 < n_steps)
    def _():
        xfetch(i + XRING - 1).start()

    @pl.when(i < n_steps)
    def _():
        xfetch(i).wait()

    x_cur = xbuf.at[i % XRING]

    for sub in range(MOE_TILES_PER_STEP):
        ti = pl.program_id(0) * MOE_TILES_PER_STEP + sub
        used = ti < nu_ref[0]
        e = te_ref[ti]
        first = jnp.logical_or(ti == 0, e != te_ref[jnp.maximum(ti - 1, 0)])
        slot = par_ref[e]

        @pl.when(jnp.logical_and(used, ti == 0))
        def _():
            for c in fetch(e, slot):
                c.start()

        @pl.when(jnp.logical_and(used, first))
        def _():
            for c in fetch(e, slot):
                c.wait()
            nx = nxt_ref[e]

            @pl.when(nx < N_EXPERTS)
            def _():
                for c in fetch(nx, 1 - slot):
                    c.start()

            w1b[...] = f1[slot].astype(BF16)
            w3b[...] = f3[slot].astype(BF16)
            w2b[...] = f2[slot].astype(BF16)

        @pl.when(used)
        def _():
            half = w1b.shape[0] // 2
            base = sub * MOE_ROWS * SLAB
            y = _ffn_packed(x_cur, MOE_ROWS, w1b, w3b, w2b, base)
            _store_slabs(y_ref, _pack_pair(y[:, :half], y[:, half:]), base)


def _moe_call(tile_e, n_used, nxt, par, xs, w1, w3, w2):
    rows = xs.shape[0] // SLAB
    step_rows = MOE_ROWS * MOE_TILES_PER_STEP
    nt = rows // step_rows
    d, de = w1.shape[-2:]
    rmap = lambda i, te, nu, *_: (jnp.minimum(i, (nu[0] - 1) // MOE_TILES_PER_STEP), 0)
    hbm = pl.BlockSpec(memory_space=pl.ANY)
    return pl.pallas_call(
        _moe_body,
        grid_spec=pltpu.PrefetchScalarGridSpec(
            num_scalar_prefetch=4, grid=(nt,),
            in_specs=[hbm, hbm, hbm, hbm],
            out_specs=pl.BlockSpec((step_rows * SLAB, SLAB_W), rmap),
            scratch_shapes=[pltpu.VMEM((2, d, de), F32), pltpu.VMEM((2, d, de), F32), pltpu.VMEM((2, de, d), F32),
                            pltpu.VMEM((d, de), BF16), pltpu.VMEM((d, de), BF16), pltpu.VMEM((de, d), BF16),
                            pltpu.VMEM((XRING, step_rows * SLAB, SLAB_W), U32),
                            pltpu.SemaphoreType.DMA((2,)), pltpu.SemaphoreType.DMA((XRING,))]),
        out_shape=jax.ShapeDtypeStruct((rows * SLAB, SLAB_W), U32),
        compiler_params=_cparams(("arbitrary",), 56),
        name="moe",
    )(tile_e, n_used, nxt, par, xs, w1, w3, w2)


def _final_body(x1_ref, sh_ref, pos_ref, posn_ref, wt_ref, ys_ref, g2_ref, gn_ref, o_ref,
                ybuf, acc_a, acc_b, sems):
    i = pl.program_id(0)
    n = pl.num_programs(0)
    tf, d = x1_ref.shape
    slot = i % 2
    slot_slabs = TOP_K * tf

    def row_copy(p_ref, s, k, t):
        return pltpu.make_async_copy(_slab(ys_ref, p_ref[k, t]), _slab(ybuf, s * slot_slabs + k * tf + t),
                                     sems.at[s])

    def issue_token(p_ref, s, t):
        for k in range(TOP_K):
            row_copy(p_ref, s, k, t).start(priority=k % 2)

    @pl.when(i == 0)
    def _():
        lax.fori_loop(0, tf, lambda t, c: (issue_token(pos_ref, 0, t), c)[1], 0)

    def drain(t, carry):
        for k in range(TOP_K):
            pltpu.make_async_copy(_slab(ys_ref, 0), _slab(ybuf, 0), sems.at[slot]).wait()
        return carry

    lax.fori_loop(0, tf, drain, 0)

    def combine_token(t):
        a = jnp.zeros((SLAB, SLAB_W), F32)
        b = jnp.zeros((SLAB, SLAB_W), F32)
        for k in range(TOP_K):
            ya, yb = _unpack_pair(_slab(ybuf, slot * slot_slabs + k * tf + t)[...])
            w = wt_ref[k, t]
            a = a + w * ya
            b = b + w * yb
        _slab(acc_a, t)[...] = a
        _slab(acc_b, t)[...] = b

    @pl.when(i + 1 < n)
    def _():
        def body(t, carry):
            combine_token(t)
            issue_token(posn_ref, 1 - slot, t)
            return carry

        lax.fori_loop(0, tf, body, 0)

    @pl.when(i + 1 >= n)
    def _():
        lax.fori_loop(0, tf, lambda t, c: (combine_token(t), c)[1], 0)

    moe = jnp.concatenate([_load_slabs(acc_a, tf), _load_slabs(acc_b, tf)], axis=1) + sh_ref[...].astype(F32)
    xo = x1_ref[...] + g2_ref[...] * moe
    ms = jnp.mean(xo * xo, axis=-1, keepdims=True)
    o_ref[...] = xo * lax.rsqrt(ms + EPS) * gn_ref[...]


def _final_call(x1, sh, pos, wts, ys, g2, gn, tf):
    m, d = x1.shape
    nt = m // tf
    full = pl.BlockSpec((tf, d), lambda i: (i, 0))
    vec = pl.BlockSpec((1, d), lambda i: (0, 0))
    return pl.pallas_call(
        _final_body,
        grid=(nt,),
        in_specs=[full, full,
                  pl.BlockSpec((TOP_K, tf), lambda i: (0, i), memory_space=pltpu.SMEM),
                  pl.BlockSpec((TOP_K, tf), lambda i: (0, jnp.minimum(i + 1, nt - 1)), memory_space=pltpu.SMEM),
                  pl.BlockSpec((TOP_K, tf), lambda i: (0, i), memory_space=pltpu.SMEM),
                  pl.BlockSpec(memory_space=pl.ANY), vec, vec],
        out_specs=full,
        out_shape=jax.ShapeDtypeStruct((m, d), F32),
        scratch_shapes=[pltpu.VMEM((2 * TOP_K * tf * SLAB, SLAB_W), U32),
                        pltpu.VMEM((tf * SLAB, SLAB_W), F32), pltpu.VMEM((tf * SLAB, SLAB_W), F32),
                        pltpu.SemaphoreType.DMA((2,))],
        compiler_params=_cparams(("arbitrary",), 48),
        name="final",
    )(x1, sh, pos, pos, wts, ys, g2, gn)


def _pos_tables(n_tokens):
    rows = n_tokens // GRID_W
    quarter = D_MODEL // 4
    omega = 1.0 / (10000.0 ** (jnp.arange(quarter, dtype=F32) / quarter))

    def emb1d(pos):
        ang = pos[:, None] * omega[None]
        return jnp.concatenate([jnp.sin(ang), jnp.cos(ang)], axis=-1)

    return emb1d(jnp.arange(rows, dtype=F32)), emb1d(jnp.arange(GRID_W, dtype=F32))


def _filter_feats(L):
    n1h = L // FFT_N2
    i_, h_, b_, a_ = jnp.meshgrid(jnp.arange(n1h // 8, dtype=I32), jnp.arange(2, dtype=I32),
                                  jnp.arange(FFT_N2, dtype=I32), jnp.arange(8, dtype=I32), indexing="ij")
    n = ((8 * i_ + a_ + h_ * n1h) * FFT_N2 + b_).reshape(-1)
    t = jnp.where(n <= L, n, 2 * L - n).astype(F32)
    t01 = t / max(L - 1, 1)
    w = 2.0 * math.pi * t / L
    bands = jnp.linspace(1e-4, FILT_BANDS - 1, FILT_BANDS, dtype=F32)
    feats = jnp.concatenate([t01[None, :], jnp.cos(bands[:, None] * w[None, :]), -jnp.sin(bands[:, None] * w[None, :]),
                             jnp.zeros((64 - 33, 2 * L), F32)], axis=0)
    return feats


def _pad_rows(a, rows):
    return jnp.concatenate([a, jnp.zeros((rows - a.shape[0],) + a.shape[1:], a.dtype)], axis=0)


def _layer(x, c, ctx, c_ctx, w_ada, b_ada, g_mix, g_ffn, w_in, b_gates, conv_k_w, conv_k_b,
           conv_q_w, conv_q_b, g_head, conv_hy_w, conv_hy_b, filt_w1, filt_b1, filt_freq1,
           filt_w2, filt_b2, filt_freq2, filt_w3, filt_b3, hy_dskip, w_out, w_router, b_router,
           w1_e, w3_e, w2_e, w1_s, w3_s, w2_s, g_final):
    L, d = x.shape
    lc = ctx.shape[0]
    row = lambda v: v.reshape(1, -1)

    cc = jnp.broadcast_to(jnp.stack([c, c_ctx], axis=0)[:, :, None], (2, d, 128))
    mods = _mod_call(cc, w_ada, row(b_ada))
    sh1, sc1, g1, sh2, sc2, g2 = [mods[0:1, k * d:(k + 1) * d] for k in range(6)]
    csh1, csc1 = mods[1:2, 0:d], mods[1:2, d:2 * d]

    w_r, w_g = _wprep_call(jnp.transpose(w_in[0]))
    bg = jnp.concatenate([b_gates, jnp.zeros((GATE_PAD - 4 * N_HEADS,), F32)]).reshape(1, GATE_PAD)
    e_row, e_col = _pos_tables(L)
    conv_w = jnp.concatenate([conv_k_w, conv_q_w], axis=1)
    conv_b = jnp.concatenate([conv_k_b, conv_q_b]).reshape(1, -1)
    conv_s = jnp.concatenate([jnp.ones((QK_W,), F32), jnp.full((QK_W,), QK_HEAD ** -0.5, F32)]).reshape(1, -1)

    z_c, gt_c = _inproj_call(ctx, jnp.zeros((8, d // 2), F32), e_col, row(g_mix), csh1, csc1, w_r, w_g,
                             use_pos=False, tm=min(lc, 256))
    kq_c = _conv_kq_call(z_c, conv_w, conv_b, conv_s, tm=min(lc, 256))
    s0 = jnp.zeros((2 * N_HEADS, CHUNK, 256), F32)
    m0 = jnp.zeros((2, 8, 128), F32)
    _, _, s_ctx, m_ctx = _mlstm_call(kq_c, z_c, gt_c, bg, s0, m0)

    z, gates = _inproj_call(x, e_row, e_col, row(g_mix), sh1, sc1, w_r, w_g, use_pos=True, tm=min(L, 1024))
    kq = _conv_kq_call(z, conv_w, conv_b, conv_s, tm=min(L, 1024))
    x0c, xv = _conv_hy_call(z, conv_hy_w, row(conv_hy_b), tm=min(L, 512))
    hf, hb, _, _ = _mlstm_call(kq, z, gates, bg, s_ctx, m_ctx)

    n1 = 2 * L // FFT_N2
    cst = _fft_consts(n1)
    rates = jnp.linspace(-math.log(DECAY_TARGET) / SLOW_DECAY_PCT, -math.log(DECAY_TARGET) / FAST_DECAY_PCT,
                         HY_W, dtype=F32).reshape(1, -1)
    w1t = jnp.transpose(_pad_rows(filt_w1, 64))
    colrep = lambda v: jnp.broadcast_to(v.reshape(-1, 1), (v.shape[0], 128))
    kf, l1 = _filt_call(_filter_feats(L), w1t, colrep(filt_b1), colrep(filt_freq1), jnp.transpose(filt_w2),
                        colrep(filt_b2), colrep(filt_freq2), filt_w3, row(filt_b3), rates)
    cols = FFT_N2 * HY_W
    cb = 8192
    kar, kai = _fft1_call(cst["f1"], kf, cb, packed=True)
    kv = n1 // 2 + 1
    khat = _fft2_filt_call(kar.reshape(FFT_KP, FFT_N2, HY_W), kai.reshape(FFT_KP, FFT_N2, HY_W), cst, kv)
    uar, uai = _fft1_call(cst["f1"], xv.reshape(n1 // 2, cols), cb)
    br, bi = _fft2_conv_call(uar.reshape(FFT_KP, FFT_N2, HY_W), uai.reshape(FFT_KP, FFT_N2, HY_W), khat, cst, kv)
    reps = cb // HY_W
    il_t = jnp.tile(1.0 / l1, (1, reps))
    yc = _ifft1_call(cst, br.reshape(FFT_KP, cols), bi.reshape(FFT_KP, cols), il_t, n1 // 2, cb).reshape(L, HY_W)

    wo = w_out.astype(BF16)
    x1, h2s, s_t, sh = _outproj_call(hf, hb, z, yc, x0c, xv, row(hy_dskip), x, e_row, e_col, row(g_head),
                                    wo[:MV_W], wo[MV_W:], g1, row(g_ffn), sh2, sc2, jnp.transpose(w_router),
                                    w1_s.astype(BF16), w3_s.astype(BF16), w2_s.astype(BF16), tm=min(L, 256))

    b_col = jnp.broadcast_to(b_router.reshape(N_EXPERTS, 1), (N_EXPERTS, 128))
    eidx, wts, rank, cnt2 = _route_call(s_t, b_col, tt=min(L, 1024))
    cnt = cnt2[:, 0]
    pcnt = (cnt + MOE_ROWS - 1) // MOE_ROWS * MOE_ROWS
    pend = jnp.cumsum(pcnt)
    pstart = pend - pcnt
    rows = L * TOP_K + N_EXPERTS * MOE_ROWS
    nt = rows // MOE_ROWS
    tile_row = jnp.arange(nt, dtype=I32) * MOE_ROWS
    tile_e = jnp.minimum(jnp.sum((pend[None, :] <= tile_row[:, None]).astype(I32), axis=1), N_EXPERTS - 1)
    n_used = (pend[-1] // MOE_ROWS).astype(I32).reshape(1)
    pos = _posk_call(pstart.astype(I32), eidx, rank)

    xs = _dispatch_call(cnt, pstart.astype(I32), pcnt.astype(I32), h2s, pos, rows, td=min(L, 512))
    ex = jnp.arange(N_EXPERTS, dtype=I32)
    nonempty = pcnt > 0
    nxt = jnp.min(jnp.where((ex[None, :] > ex[:, None]) & nonempty[None, :], ex[None, :], N_EXPERTS), axis=1)
    par = (jnp.cumsum(nonempty.astype(I32)) + 1) % 2
    ys = _moe_call(tile_e, n_used, nxt.astype(I32), par.astype(I32), xs, w1_e, w3_e, w2_e)
    return _final_call(x1, sh, pos, wts, ys, g2, row(g_final), tf=min(L, 256))


def kernel(x, c, ctx, c_ctx, w_ada, b_ada, g_mix, g_ffn, w_in, b_gates, conv_k_w, conv_k_b, conv_q_w,
           conv_q_b, g_head, conv_hy_w, conv_hy_b, filt_w1, filt_b1, filt_freq1, filt_w2, filt_b2,
           filt_freq2, filt_w3, filt_b3, hy_dskip, w_out, w_router, b_router, w1_e, w3_e, w2_e,
           w1_s, w3_s, w2_s, g_final):
    assert x.shape[0] == 1 and w_ada.shape[0] == 1, "one batch element, one layer"
    out = _layer(x[0], c[0], ctx[0], c_ctx, w_ada[0], b_ada[0], g_mix[0], g_ffn[0], w_in, b_gates[0],
                 conv_k_w[0], conv_k_b[0], conv_q_w[0], conv_q_b[0], g_head[0], conv_hy_w[0], conv_hy_b[0],
                 filt_w1[0], filt_b1[0], filt_freq1[0], filt_w2[0], filt_b2[0], filt_freq2[0], filt_w3[0],
                 filt_b3[0], hy_dskip[0], w_out[0], w_router[0], b_router[0], w1_e[0], w3_e[0], w2_e[0],
                 w1_s[0], w3_s[0], w2_s[0], g_final)
    return out[None]
```
